```python
import math
import jax
import jax.numpy as jnp
from jax import lax
import numpy as np

D_MODEL = 1024
BATCH = 16
SEQ = 2048
DEPTH = 2

GRID_W = 64
CTX_LEN = 256

RWKV_HEADS = 6
RWKV_HEAD_DIM = 64
RWKV_WIDTH = RWKV_HEADS * RWKV_HEAD_DIM
RWKV_DECAY_RANK = 32
RWKV_ICLR_RANK = 32
RWKV_GATE_RANK = 64
RWKV_GN_EPS = 64e-5
RWKV_COLS = 3 * RWKV_WIDTH + 2 * RWKV_DECAY_RANK + 2 * RWKV_ICLR_RANK + RWKV_GATE_RANK

GLA_HEADS = 6
GLA_KEY_DIM = 32
GLA_VAL_DIM = 64
GLA_WIDTH = GLA_HEADS * GLA_VAL_DIM
GLA_GATE_RANK = 16
GLA_TEMP = 16.0
GLA_CHUNK = 64
GLA_COLS = 2 * GLA_HEADS * GLA_KEY_DIM + 2 * GLA_WIDTH + 2 * GLA_GATE_RANK

S5_GROUP = 16
S5_GROUPS = 16
S5_WIDTH = S5_GROUP * S5_GROUPS
S5_STATE = 64
S5_DT_MIN = 1e-3
S5_DT_MAX = 1e-1

MIX_WIDTH = RWKV_WIDTH + GLA_WIDTH + S5_WIDTH
N_IN = RWKV_COLS + GLA_COLS + S5_WIDTH

N_EXPERTS = 256
TOP_K = 8
N_EXPERT_GROUPS = 8
TOPK_GROUPS = 4
EXPERT_FF = 256
SHARED_FF = 256
ROUTE_SCALE = 2.5
MOE_BLOCK = 256
LN_EPS = 1e-6

kernel_name = 'hybrid_rwkv7_gla_s5_moe_diffusion_block'


def split_cols(z, sizes):
    return jnp.split(z, np.cumsum(sizes)[:-1].tolist(), axis=-1)


def layer_norm(z, g=None, b=None, eps=LN_EPS):
    zf = z.astype(jnp.float32)
    mu = zf.mean(-1, keepdims=True)
    var = jnp.square(zf - mu).mean(-1, keepdims=True)
    out = (zf - mu) * lax.rsqrt(var + eps)
    if g is not None:
        out = out * g.astype(jnp.float32) + b.astype(jnp.float32)
    return out.astype(z.dtype)


def modulate(z, shift, scale):
    return layer_norm(z) * (1 + scale) + shift


def dir_stack(fwd, bwd):
    return jnp.stack([fwd, jnp.flip(bwd, axis=1)])


def dir_merge(y2):
    return y2[0] + jnp.flip(y2[1], axis=1)


def short_conv(z, taps, rows):
    B_, T, C = z.shape
    if rows is None:
        return lax.conv_general_dilated(z, taps[1][:, None, :], (1,), 'SAME',
                                        dimension_numbers=('NWC', 'WIO', 'NWC'), feature_group_count=C)
    zi = z.reshape(B_, rows, GRID_W, C)
    out = lax.conv_general_dilated(zi, taps[:, :, None, :], (1, 1), 'SAME',
                                   dimension_numbers=('NHWC', 'HWIO', 'NHWC'), feature_group_count=C)
    return out.reshape(B_, T, C)


def rwkv7_inputs(pa, w0, w_up, a0, a_up, g_up, k_k, k_a):
    B_, T, _ = pa.shape
    W = RWKV_WIDTH
    r, k, v, wd, ad, gd = split_cols(pa, [W, W, W, 2 * RWKV_DECAY_RANK, 2 * RWKV_ICLR_RANK, RWKV_GATE_RANK])
    wd = wd.reshape(B_, T, 2, RWKV_DECAY_RANK)
    ad = ad.reshape(B_, T, 2, RWKV_ICLR_RANK)
    w_raw = w0[:, None, None, :] + jnp.einsum('btdr,drc->dbtc', jnp.tanh(wd), w_up)
    w_raw = -jax.nn.softplus(-w_raw) - 0.5
    decay = jnp.exp(-jnp.exp(w_raw))
    a = jax.nn.sigmoid(a0[:, None, None, :] + jnp.einsum('btdr,drc->dbtc', ad, a_up))
    g = jax.nn.sigmoid(gd) @ g_up
    hd = lambda z: z.reshape(*z.shape[:-1], RWKV_HEADS, RWKV_HEAD_DIM)
    kk = hd(k * k_k).astype(jnp.float32)
    kk = kk * lax.rsqrt(jnp.sum(kk * kk, axis=-1, keepdims=True) + 1e-12)
    k2 = k[None] * (1 + (a - 1) * k_a)
    return hd(r), hd(decay), hd(k2), hd(v), kk, hd(a), g


def rwkv7_scan(r, w, k, v, a_vec, b_vec, s0):
    def step(S, inp):
        r_t, w_t, k_t, v_t, a_t, b_t = inp
        sa = jnp.einsum('dbhij,dbhj->dbhi', S, a_t)
        S = S * w_t[..., None, :] + sa[..., :, None] * b_t[..., None, :] + v_t[..., :, None] * k_t[..., None, :]
        return S, jnp.einsum('dbhij,dbhj->dbhi', S, r_t)
    xs = tuple(jnp.moveaxis(z.astype(jnp.float32), 2, 0) for z in (r, w, k, v, a_vec, b_vec))
    s_fin, y = lax.scan(step, s0.astype(jnp.float32), xs)
    return jnp.moveaxis(y, 0, 2), s_fin


def gla_inputs(pb, a_up, a_bias):
    B_, T, _ = pb.shape
    KD = GLA_HEADS * GLA_KEY_DIM
    q, k, v, g, ad = split_cols(pb, [KD, KD, GLA_WIDTH, GLA_WIDTH, 2 * GLA_GATE_RANK])
    ad = ad.reshape(B_, T, 2, GLA_GATE_RANK)
    la = jax.nn.log_sigmoid(jnp.einsum('btdr,drc->dbtc', ad, a_up) + a_bias[:, None, None, :]) / GLA_TEMP
    hk = lambda z: z.reshape(*z.shape[:-1], GLA_HEADS, GLA_KEY_DIM)
    return (hk(q) * GLA_KEY_DIM ** -0.5, hk(k), v.reshape(B_, T, GLA_HEADS, GLA_VAL_DIM), g, hk(la))


def gla_chunked(q, k, v, log_a, s0):
    B_, T, H, K = q.shape
    V = v.shape[-1]
    n = T // GLA_CHUNK
    blk = lambda z: z.astype(jnp.float32).reshape(B_, n, GLA_CHUNK, H, z.shape[-1])
    q, k, v, log_a = blk(q), blk(k), blk(v), blk(log_a)
    b = jnp.cumsum(log_a, axis=2)
    b_last = b[:, :, -1:]
    q_in = q * jnp.exp(b)
    k_in = k * jnp.exp(-b)
    k_st = k * jnp.exp(b_last - b)
    causal = jnp.tril(jnp.ones((GLA_CHUNK, GLA_CHUNK), dtype=bool))
    att = jnp.where(causal, jnp.einsum('bnchk,bnshk->bnhcs', q_in, k_in), 0.0)
    o_intra = jnp.einsum('bnhcs,bnshv->bnchv', att, v)

    def step(S, inp):
        qn, kn, vn, dn = inp
        o = jnp.einsum('bchk,bhkv->bchv', qn, S)
        S = S * dn[..., None] + jnp.einsum('bchk,bchv->bhkv', kn, vn)
        return S, o
    xs = (jnp.moveaxis(q_in, 1, 0), jnp.moveaxis(k_st, 1, 0), jnp.moveaxis(v, 1, 0),
          jnp.moveaxis(jnp.exp(b_last[:, :, 0]), 1, 0))
    s_fin, o_inter = lax.scan(step, s0.astype(jnp.float32), xs)
    o = o_intra + jnp.moveaxis(o_inter, 0, 1)
    return o.reshape(B_, T, H, V), s_fin


def complex_affine_combine(e1, e2):
    a1r, a1i, b1r, b1i = e1
    a2r, a2i, b2r, b2i = e2
    return (a2r * a1r - a2i * a1i, a2r * a1i + a2i * a1r,
            a2r * b1r - a2i * b1i + b2r, a2r * b1i + a2i * b1r + b2i)


def s5_scan(u, lam_re, lam_im, log_dt, b_re, b_im, s0_re, s0_im):
    f32 = jnp.float32
    lam_re, lam_im = lam_re.astype(f32), lam_im.astype(f32)
    b_re, b_im = b_re.astype(f32), b_im.astype(f32)
    dt = jnp.exp(log_dt.astype(f32))[:, None]
    zr, zi = lam_re * dt, lam_im * dt
    mag = jnp.exp(zr)
    ab_r, ab_i = mag * jnp.cos(zi), mag * jnp.sin(zi)
    den = lam_re * lam_re + lam_im * lam_im
    f_r = ((ab_r - 1) * lam_re + ab_i * lam_im) / den
    f_i = (ab_i * lam_re - (ab_r - 1) * lam_im) / den
    bb_r = f_r[..., None] * b_re - f_i[..., None] * b_im
    bb_i = f_r[..., None] * b_im + f_i[..., None] * b_re
    bu_r = jnp.einsum('gpc,btgc->btgp', bb_r, u)
    bu_i = jnp.einsum('gpc,btgc->btgp', bb_i, u)
    bu_r = bu_r.at[:, 0].add(ab_r * s0_re - ab_i * s0_im)
    bu_i = bu_i.at[:, 0].add(ab_r * s0_im + ab_i * s0_re)
    a_r = jnp.broadcast_to(ab_r, bu_r.shape)
    a_i = jnp.broadcast_to(ab_i, bu_i.shape)
    _, _, x_r, x_i = lax.associative_scan(complex_affine_combine, (a_r, a_i, bu_r, bu_i), axis=1)
    return x_r, x_i


def token_mixers(h, rows, p, s_rwkv, s_gla, s5_re, s5_im, with_output):
    f32 = jnp.float32
    B_, T, _ = h.shape
    proj = h @ p['w_in']
    pa, pb, pc = split_cols(proj, [RWKV_COLS, GLA_COLS, S5_WIDTH])
    same = lambda z: dir_stack(z, z)
    pa = short_conv(pa, p['rwkv_conv'], rows)
    r, dec, k2, v, kk, a, g = rwkv7_inputs(pa, p['rwkv_w0'], p['rwkv_w_up'], p['rwkv_a0'], p['rwkv_a_up'],
                                           p['rwkv_g_up'], p['rwkv_k_k'], p['rwkv_k_a'])
    y_a2, s_rwkv = rwkv7_scan(same(r), dir_stack(dec[0], dec[1]), dir_stack(k2[0], k2[1]), same(v),
                              same(-kk), dir_stack(kk * a[0], kk * a[1]), s_rwkv)
    q, kg, vg, gg, la = gla_inputs(pb, p['gla_a_up'], p['gla_a_bias'])
    o2, s_gla = jax.vmap(gla_chunked)(same(q), same(kg), same(vg), dir_stack(la[0], la[1]), s_gla)
    u = pc.astype(f32).reshape(B_, T, S5_GROUPS, S5_GROUP)
    s5_x, fin_re, fin_im = [], [], []
    for d in range(2):
        ud = u if d == 0 else jnp.flip(u, axis=1)
        x_re, x_im = s5_scan(ud, p['s5_lam_re'][d], p['s5_lam_im'][d], p['s5_log_dt'][d],
                             p['s5_b_re'][d], p['s5_b_im'][d], s5_re[d], s5_im[d])
        s5_x.append((x_re, x_im))
        fin_re.append(x_re[:, -1])
        fin_im.append(x_im[:, -1])
    states = (s_rwkv, s_gla, jnp.stack(fin_re), jnp.stack(fin_im))
    if not with_output:
        return None, states
    y = dir_merge(y_a2)
    mu = y.mean(-1, keepdims=True)
    var = jnp.square(y - mu).mean(-1, keepdims=True)
    gn = ((y - mu) * lax.rsqrt(var + RWKV_GN_EPS) * p['rwkv_ln_g'].reshape(RWKV_HEADS, RWKV_HEAD_DIM)
          + p['rwkv_ln_b'].reshape(RWKV_HEADS, RWKV_HEAD_DIM))
    bonus = jnp.sum(r[None] * k2 * p['rwkv_r_k'], axis=(0, -1))[..., None] * v
    y_a = (gn.astype(h.dtype) + bonus).reshape(B_, T, RWKV_WIDTH) * g
    o = dir_merge(o2)
    o = o * lax.rsqrt(jnp.square(o).mean(-1, keepdims=True) + 1e-6) * p['gla_norm_g'].astype(f32)
    y_b = o.astype(h.dtype).reshape(B_, T, GLA_WIDTH) * jax.nn.silu(gg)
    y_c = p['s5_d'].astype(f32).reshape(S5_GROUPS, S5_GROUP) * u
    for d in range(2):
        x_re, x_im = s5_x[d]
        yd = (jnp.einsum('gcp,btgp->btgc', p['s5_c_re'][d].astype(f32), x_re)
              - jnp.einsum('gcp,btgp->btgc', p['s5_c_im'][d].astype(f32), x_im))
        y_c = y_c + (yd if d == 0 else jnp.flip(yd, axis=1))
    y_c = jax.nn.gelu(y_c.reshape(B_, T, S5_WIDTH)).astype(h.dtype)
    y_c = y_c * jax.nn.sigmoid(y_c @ p['s5_glu_w'] + p['s5_glu_b'])
    y_mix = jnp.concatenate([y_a, y_b, y_c], axis=-1) @ p['w_out']
    return y_mix, states


def moe_ffn(h, router_w, router_bias, w13, w2, sh_w13, sh_w2):
    n_tok, d_model = h.shape
    s = jax.nn.sigmoid(h.astype(jnp.float32) @ router_w.astype(jnp.float32))
    s_sel = s + router_bias.astype(jnp.float32)
    grp = s_sel.reshape(n_tok, N_EXPERT_GROUPS, N_EXPERTS // N_EXPERT_GROUPS)
    g_score = lax.top_k(grp, 2)[0].sum(-1)
    _, g_idx = lax.top_k(g_score, TOPK_GROUPS)
    g_mask = jnp.any(g_idx[..., None] == jnp.arange(N_EXPERT_GROUPS), axis=1)
    e_mask = jnp.repeat(g_mask, N_EXPERTS // N_EXPERT_GROUPS, axis=1)
    _, e_idx = lax.top_k(jnp.where(e_mask, s_sel, -jnp.inf), TOP_K)
    wts = jnp.take_along_axis(s, e_idx, axis=1)
    wts = wts / wts.sum(-1, keepdims=True) * ROUTE_SCALE
    n_pairs = n_tok * TOP_K
    n_blocks = (n_pairs + MOE_BLOCK - 1) // MOE_BLOCK + N_EXPERTS
    flat_e = e_idx.reshape(-1).astype(jnp.int32)
    order = jnp.argsort(flat_e)
    sorted_e = flat_e[order]
    tok = (order // TOP_K).astype(jnp.int32)
    sizes = jnp.bincount(flat_e, length=N_EXPERTS).astype(jnp.int32)
    padded = (sizes + MOE_BLOCK - 1) // MOE_BLOCK * MOE_BLOCK
    p_end = jnp.cumsum(padded)
    rank = jnp.arange(n_pairs, dtype=jnp.int32) - (jnp.cumsum(sizes) - sizes)[sorted_e]
    dest = (p_end - padded)[sorted_e] + rank
    slot_tok = jnp.full((n_blocks * MOE_BLOCK,), n_tok, jnp.int32).at[dest].set(tok)
    slot_w = jnp.zeros((n_blocks * MOE_BLOCK,), h.dtype).at[dest].set(wts.reshape(-1)[order].astype(h.dtype))
    block_e = jnp.minimum(jnp.searchsorted(p_end, jnp.arange(n_blocks, dtype=jnp.int32) * MOE_BLOCK,
                                           side='right'), N_EXPERTS - 1).astype(jnp.int32)
    h_pad = jnp.concatenate([h, jnp.zeros((1, d_model), h.dtype)], axis=0)
    xb = h_pad[slot_tok].reshape(n_blocks, MOE_BLOCK, d_model)

    def expert_block(args):
        xe, e = args
        gate, up = jnp.split(xe @ w13[e], 2, axis=-1)
        return (jax.nn.silu(gate) * up) @ w2[e]
    out = lax.map(expert_block, (xb, block_e)).reshape(n_blocks * MOE_BLOCK, d_model)
    routed = jax.ops.segment_sum(out * slot_w[:, None], slot_tok, num_segments=n_tok + 1)[:n_tok]
    sg, su = jnp.split(h @ sh_w13, 2, axis=-1)
    return routed + (jax.nn.silu(sg) * su) @ sh_w2


def setup_inputs(seed: int = 0) -> dict:
    key = jax.random.key(seed)
    keys = iter(jax.random.split(key, 64))
    f32 = jnp.float32
    L, D, E = DEPTH, D_MODEL, N_EXPERTS
    beta = (8 * DEPTH) ** -0.25

    def nrm(shape, scale):
        return jax.random.normal(next(keys), shape, f32) * scale

    x = nrm((BATCH, SEQ, D), 1.0)
    c = nrm((BATCH, D), 1.0)
    ctx = nrm((BATCH, CTX_LEN, D), 1.0)
    c_ctx = nrm((D,), 1.0)
    w_mod = nrm((L, D, 6 * D), 0.5 * D ** -0.5)
    b_mod = nrm((L, 6 * D), 0.02)
    w_in = nrm((L, D, N_IN), D ** -0.5)
    rwkv_conv = (1.0 / 16.0 + nrm((L, 3, 3, RWKV_COLS), 0.02)).at[:, 1, 1, :].add(0.5 - 1.0 / 16.0)
    rwkv_w0 = -1.0 + nrm((L, 2, RWKV_WIDTH), 0.5)
    rwkv_w_up = nrm((L, 2, RWKV_DECAY_RANK, RWKV_WIDTH), 0.5 * RWKV_DECAY_RANK ** -0.5)
    rwkv_a0 = nrm((L, 2, RWKV_WIDTH), 0.5)
    rwkv_a_up = nrm((L, 2, RWKV_ICLR_RANK, RWKV_WIDTH), 0.5 * RWKV_ICLR_RANK ** -0.5)
    rwkv_g_up = nrm((L, RWKV_GATE_RANK, RWKV_WIDTH), RWKV_GATE_RANK ** -0.5)
    rwkv_k_k = 0.85 + nrm((L, RWKV_WIDTH), 0.1)
    rwkv_k_a = 1.0 + nrm((L, RWKV_WIDTH), 0.1)
    rwkv_r_k = nrm((L, RWKV_HEADS, RWKV_HEAD_DIM), 0.1)
    rwkv_ln_g = 1.0 + nrm((L, RWKV_WIDTH), 0.05)
    rwkv_ln_b = nrm((L, RWKV_WIDTH), 0.02)
    gla_a_up = nrm((L, 2, GLA_GATE_RANK, GLA_HEADS * GLA_KEY_DIM), GLA_GATE_RANK ** -0.5)
    gla_a_bias = 2.0 + nrm((L, 2, GLA_HEADS * GLA_KEY_DIM), 0.5)
    gla_norm_g = 1.0 + nrm((L, GLA_VAL_DIM), 0.05)
    s5_lam_re = -0.5 + nrm((L, 2, S5_STATE), 0.01)
    s5_lam_im = jnp.pi * jnp.arange(S5_STATE, dtype=f32) + nrm((L, 2, S5_STATE), 0.01)
    s5_log_dt = jax.random.uniform(next(keys), (L, 2, S5_GROUPS), f32,
                                   minval=math.log(S5_DT_MIN), maxval=math.log(S5_DT_MAX))
    s5_b_re = nrm((L, 2, S5_GROUPS, S5_STATE, S5_GROUP), (2 * S5_GROUP) ** -0.5)
    s5_b_im = nrm((L, 2, S5_GROUPS, S5_STATE, S5_GROUP), (2 * S5_GROUP) ** -0.5)
    s5_c_re = nrm((L, 2, S5_GROUPS, S5_GROUP, S5_STATE), (2 * S5_STATE) ** -0.5)
    s5_c_im = nrm((L, 2, S5_GROUPS, S5_GROUP, S5_STATE), (2 * S5_STATE) ** -0.5)
    s5_d = nrm((L, S5_WIDTH), 1.0)
    s5_glu_w = nrm((L, S5_WIDTH, S5_WIDTH), S5_WIDTH ** -0.5)
    s5_glu_b = nrm((L, S5_WIDTH), 0.02)
    w_out = nrm((L, MIX_WIDTH, D), beta * MIX_WIDTH ** -0.5)
    ln1_g = 1.0 + nrm((L, D), 0.05)
    ln1_b = nrm((L, D), 0.02)
    router_w = nrm((L, D, E), D ** -0.5)
    router_bias = nrm((L, E), 0.01)
    exp_w13 = nrm((L, E, D, 2 * EXPERT_FF), D ** -0.5)
    exp_w2 = nrm((L, E, EXPERT_FF, D), beta * EXPERT_FF ** -0.5)
    sh_w13 = nrm((L, D, 2 * SHARED_FF), D ** -0.5)
    sh_w2 = nrm((L, SHARED_FF, D), beta * SHARED_FF ** -0.5)
    ln2_g = 1.0 + nrm((L, D), 0.05)
    ln2_b = nrm((L, D), 0.02)
    return {'x': x, 'c': c, 'ctx': ctx, 'c_ctx': c_ctx, 'w_mod': w_mod, 'b_mod': b_mod, 'w_in': w_in,
            'rwkv_conv': rwkv_conv, 'rwkv_w0': rwkv_w0, 'rwkv_w_up': rwkv_w_up, 'rwkv_a0': rwkv_a0,
            'rwkv_a_up': rwkv_a_up, 'rwkv_g_up': rwkv_g_up, 'rwkv_k_k': rwkv_k_k, 'rwkv_k_a': rwkv_k_a,
            'rwkv_r_k': rwkv_r_k, 'rwkv_ln_g': rwkv_ln_g, 'rwkv_ln_b': rwkv_ln_b, 'gla_a_up': gla_a_up,
            'gla_a_bias': gla_a_bias, 'gla_norm_g': gla_norm_g, 's5_lam_re': s5_lam_re,
            's5_lam_im': s5_lam_im, 's5_log_dt': s5_log_dt, 's5_b_re': s5_b_re, 's5_b_im': s5_b_im,
            's5_c_re': s5_c_re, 's5_c_im': s5_c_im, 's5_d': s5_d, 's5_glu_w': s5_glu_w,
            's5_glu_b': s5_glu_b, 'w_out': w_out, 'ln1_g': ln1_g, 'ln1_b': ln1_b, 'router_w': router_w,
            'router_bias': router_bias, 'exp_w13': exp_w13, 'exp_w2': exp_w2, 'sh_w13': sh_w13,
            'sh_w2': sh_w2, 'ln2_g': ln2_g, 'ln2_b': ln2_b}


def reference(x, c, ctx, c_ctx, w_mod, b_mod, w_in, rwkv_conv, rwkv_w0, rwkv_w_up, rwkv_a0, rwkv_a_up,
              rwkv_g_up, rwkv_k_k, rwkv_k_a, rwkv_r_k, rwkv_ln_g, rwkv_ln_b, gla_a_up, gla_a_bias,
              gla_norm_g, s5_lam_re, s5_lam_im, s5_log_dt, s5_b_re, s5_b_im, s5_c_re, s5_c_im, s5_d,
              s5_glu_w, s5_glu_b, w_out, ln1_g, ln1_b, router_w, router_bias, exp_w13, exp_w2, sh_w13,
              sh_w2, ln2_g, ln2_b):
    f32 = jnp.float32
    alpha = (2 * DEPTH) ** 0.25
    B_, T, D = x.shape
    rows = T // GRID_W
    n_ctx = ctx.shape[1]
    for l in range(DEPTH):
        last = l == DEPTH - 1
        p = {'w_in': w_in[l], 'rwkv_conv': rwkv_conv[l], 'rwkv_w0': rwkv_w0[l], 'rwkv_w_up': rwkv_w_up[l],
             'rwkv_a0': rwkv_a0[l], 'rwkv_a_up': rwkv_a_up[l], 'rwkv_g_up': rwkv_g_up[l],
             'rwkv_k_k': rwkv_k_k[l], 'rwkv_k_a': rwkv_k_a[l], 'rwkv_r_k': rwkv_r_k[l],
             'rwkv_ln_g': rwkv_ln_g[l], 'rwkv_ln_b': rwkv_ln_b[l], 'gla_a_up': gla_a_up[l],
             'gla_a_bias': gla_a_bias[l], 'gla_norm_g': gla_norm_g[l], 's5_lam_re': s5_lam_re[l],
             's5_lam_im': s5_lam_im[l], 's5_log_dt': s5_log_dt[l], 's5_b_re': s5_b_re[l],
             's5_b_im': s5_b_im[l], 's5_c_re': s5_c_re[l], 's5_c_im': s5_c_im[l], 's5_d': s5_d[l],
             's5_glu_w': s5_glu_w[l], 's5_glu_b': s5_glu_b[l], 'w_out': w_out[l]}
        mod_lat = jax.nn.silu(c) @ w_mod[l] + b_mod[l]
        mod_ctx = jax.nn.silu(c_ctx) @ w_mod[l] + b_mod[l]
        sh1, sc1, g1, sh2, sc2, g2 = jnp.split(mod_lat[:, None, :], 6, axis=-1)
        csh1, csc1, cg1, csh2, csc2, cg2 = jnp.split(mod_ctx, 6, axis=-1)
        zero_states = (jnp.zeros((2, B_, RWKV_HEADS, RWKV_HEAD_DIM, RWKV_HEAD_DIM), f32),
                       jnp.zeros((2, B_, GLA_HEADS, GLA_KEY_DIM, GLA_VAL_DIM), f32),
                       jnp.zeros((2, B_, S5_GROUPS, S5_STATE), f32),
                       jnp.zeros((2, B_, S5_GROUPS, S5_STATE), f32))
        y_ctx, ctx_states = token_mixers(modulate(ctx, csh1, csc1), None, p, *zero_states, not last)
        y_lat, _ = token_mixers(modulate(x, sh1, sc1), rows, p, *ctx_states, True)
        x = layer_norm(alpha * x + g1 * y_lat, ln1_g[l], ln1_b[l])
        h2 = modulate(x, sh2, sc2).reshape(B_ * T, D)
        if last:
            f_lat = moe_ffn(h2, router_w[l], router_bias[l], exp_w13[l], exp_w2[l], sh_w13[l], sh_w2[l])
        else:
            ctx = layer_norm(alpha * ctx + cg1 * y_ctx, ln1_g[l], ln1_b[l])
            hc2 = modulate(ctx, csh2, csc2).reshape(B_ * n_ctx, D)
            f_all = moe_ffn(jnp.concatenate([hc2, h2], axis=0), router_w[l], router_bias[l],
                            exp_w13[l], exp_w2[l], sh_w13[l], sh_w2[l])
            f_ctx, f_lat = f_all[:B_ * n_ctx], f_all[B_ * n_ctx:]
            ctx = layer_norm(alpha * ctx + cg2 * f_ctx.reshape(ctx.shape), ln2_g[l], ln2_b[l])
        x = layer_norm(alpha * x + g2 * f_lat.reshape(x.shape), ln2_g[l], ln2_b[l])
    return x
```

```python
import functools
import math

import numpy as np
import jax
import jax.numpy as jnp
from jax import lax
from jax.experimental import pallas as pl
from jax.experimental.pallas import tpu as pltpu

F32 = jnp.float32
BF16 = jnp.bfloat16
I32 = jnp.int32

GRID_W = 64
RWKV_HEADS = 6
HEAD = 64
RW = RWKV_HEADS * HEAD
DECAY_RANK = 32
ICLR_RANK = 32
GATE_RANK = 64
RWKV_GN_EPS = 64e-5
GLA_HEADS = 6
GLA_K = 32
GLA_RANK = 16
GLA_TEMP = 16.0
S5_GROUPS = 16
S5_GROUP = 16
S5_STATE = 64
S5_W = S5_GROUPS * S5_GROUP
S5_LANES = 2 * S5_GROUPS * S5_STATE
N_EXPERTS = 256
TOP_K = 8
N_EGROUPS = 8
TOPK_GROUPS = 4
ROUTE_SCALE = 2.5
LN_EPS = 1e-6
CHUNK = 64
MOE_BLK = 256
LANE = 128
SUB = 8
VMEM_LIMIT = 56 * 1024 * 1024

PA_W = 3 * RW + 2 * LANE
PBG_W = 256 + 256 + RW + LANE
NP_W = PA_W + PBG_W + RW + S5_W


def _cp(sem):
    return pltpu.CompilerParams(dimension_semantics=sem, vmem_limit_bytes=VMEM_LIMIT)


def _sigmoid(x):
    return 1.0 / (1.0 + jnp.exp(-x))


def _silu(x):
    return x * _sigmoid(x)


def _bdot(a, b, dims=None):
    a = a.astype(BF16)
    b = b.astype(BF16)
    if dims is None:
        return jnp.dot(a, b, preferred_element_type=F32)
    return lax.dot_general(a, b, dims, preferred_element_type=F32)


def _split2(x):
    hi = x.astype(BF16)
    lo = (x - hi.astype(F32)).astype(BF16)
    return hi, lo


def _split3(x):
    hi = x.astype(BF16)
    r = x - hi.astype(F32)
    mid = r.astype(BF16)
    lo = (r - mid.astype(F32)).astype(BF16)
    return hi, mid, lo


NT_DIMS = (((1,), (1,)), ((), ()))
TN_DIMS = (((0,), (0,)), ((), ()))


def _dot33(a, b, dims=None):
    ah, al = _split2(a)
    bh, bl = _split2(b)
    return _bdot(ah, bh, dims) + (_bdot(ah, bl, dims) + _bdot(al, bh, dims))


def _dot_exact_lhs(m_exact, x, dims=None):
    h, m, l = _split3(x)
    return _bdot(m_exact, h, dims) + (_bdot(m_exact, m, dims) + _bdot(m_exact, l, dims))


def _dot_exact_rhs(x, m_exact, dims=None):
    h, m, l = _split3(x)
    return _bdot(h, m_exact, dims) + (_bdot(m, m_exact, dims) + _bdot(l, m_exact, dims))


def _layer_norm(x):
    mu = jnp.mean(x, axis=-1, keepdims=True)
    xc = x - mu
    var = jnp.mean(xc * xc, axis=-1, keepdims=True)
    return xc * lax.rsqrt(var + LN_EPS)


def _mod_kernel(c_ref, w_ref, b_ref, o_ref):
    s = _silu(c_ref[...])
    o_ref[0] = _dot33(s, w_ref[0]) + b_ref[0]


def mod_table(cc, w_mod, b_mod):
    L, D, D6 = w_mod.shape
    R = cc.shape[0]
    tn = 1536
    return pl.pallas_call(
        _mod_kernel,
        out_shape=jax.ShapeDtypeStruct((L, R, D6), F32),
        grid=(L, D6 // tn),
        in_specs=[pl.BlockSpec((R, D), lambda l, j: (0, 0)),
                  pl.BlockSpec((1, D, tn), lambda l, j: (l, 0, j)),
                  pl.BlockSpec((1, 1, tn), lambda l, j: (l, 0, j))],
        out_specs=pl.BlockSpec((1, R, tn), lambda l, j: (l, 0, j)),
        compiler_params=_cp(("arbitrary", "arbitrary")),
        name="mod_table",
    )(cc, w_mod, b_mod.reshape(L, 1, D6))


def _inproj_kernel(x_ref, mod_ref, w_ref, pa_ref, pbg_ref, gg_ref, u_ref):
    x = x_ref[...]
    h = _layer_norm(x) * (1.0 + mod_ref[0, 1:2, :]) + mod_ref[0, 0:1, :]
    hb = h.astype(BF16)
    o = 0
    for ref in (pa_ref, pbg_ref, gg_ref, u_ref):
        w = ref.shape[-1]
        ref[...] = jnp.dot(hb, w_ref[:, o:o + w], preferred_element_type=F32)
        o += w


def inproj(x2, mod3, w_bf, row_of_tile, tm):
    N, D = x2.shape
    return pl.pallas_call(
        _inproj_kernel,
        out_shape=[jax.ShapeDtypeStruct((N, PA_W), F32),
                   jax.ShapeDtypeStruct((N, PBG_W), F32),
                   jax.ShapeDtypeStruct((N, RW), F32),
                   jax.ShapeDtypeStruct((N, S5_W), F32)],
        grid=(N // tm,),
        in_specs=[pl.BlockSpec((tm, D), lambda i: (i, 0)),
                  pl.BlockSpec((1, 6, D), lambda i: (row_of_tile(i), 0, 0)),
                  pl.BlockSpec((D, NP_W), lambda i: (0, 0))],
        out_specs=[pl.BlockSpec((tm, PA_W), lambda i: (i, 0)),
                   pl.BlockSpec((tm, PBG_W), lambda i: (i, 0)),
                   pl.BlockSpec((tm, RW), lambda i: (i, 0)),
                   pl.BlockSpec((tm, S5_W), lambda i: (i, 0))],
        compiler_params=_cp(("arbitrary",)),
        name="inproj",
    )(x2, mod3, w_bf)


CONV_PAD = 72


def _conv_kernel(x_ref, taps_ref, o_ref, buf_ref, *, T, W, vertical):
    pad = CONV_PAD
    zeros = jnp.zeros((pad, LANE), F32)
    buf_ref[0:pad, :] = zeros
    buf_ref[pad + T:pad + T + pad, :] = zeros
    buf_ref[pad:pad + T, :] = x_ref[0]
    ch = min(T, 256)
    col = lax.broadcasted_iota(I32, (ch, LANE), 0) & (W - 1)
    left_ok = col >= 1
    right_ok = col <= W - 2
    for c in range(T // ch):
        base = pad + c * ch
        acc = jnp.zeros((ch, LANE), F32)
        for dr in ((0, 1, 2) if vertical else (1,)):
            for dc in range(3):
                off = (dr - 1) * W + (dc - 1)
                v = buf_ref[base + off:base + off + ch, :]
                if dc == 0:
                    v = jnp.where(left_ok, v, 0.0)
                elif dc == 2:
                    v = jnp.where(right_ok, v, 0.0)
                acc = acc + v * taps_ref[3 * dr + dc:3 * dr + dc + 1, :]
        o_ref[0, c * ch:(c + 1) * ch, :] = acc


def short_conv(pa3, taps9, W):
    B, T, C = pa3.shape
    vertical = T > W
    assert W & (W - 1) == 0 and (not vertical or W + 1 <= CONV_PAD)
    kern = functools.partial(_conv_kernel, T=T, W=W, vertical=vertical)
    return pl.pallas_call(
        kern,
        out_shape=jax.ShapeDtypeStruct((B, T, C), F32),
        grid=(B, C // LANE),
        in_specs=[pl.BlockSpec((1, T, LANE), lambda b, j: (b, 0, j)),
                  pl.BlockSpec((9, LANE), lambda b, j: (0, j))],
        out_specs=pl.BlockSpec((1, T, LANE), lambda b, j: (b, 0, j)),
        scratch_shapes=[pltpu.VMEM((T + 2 * CONV_PAD, LANE), F32)],
        compiler_params=_cp(("arbitrary", "arbitrary")),
        name="short_conv",
    )(pa3, taps9)


def _rwkv_prep_kernel(pa_ref, wup_ref, aup_ref, gup_ref, w0_ref, a0_ref, kk_ref, ka_ref, rk_ref, hb_ref,
                      ss_ref, sd_ref, gb_ref):
    r = pa_ref[:, 0:RW]
    k = pa_ref[:, RW:2 * RW]
    v = pa_ref[:, 2 * RW:3 * RW]
    wa = pa_ref[:, 3 * RW:3 * RW + LANE]
    gd = pa_ref[:, 3 * RW + LANE:3 * RW + 2 * LANE]
    z = w0_ref[...] + _dot33(jnp.tanh(wa), wup_ref[...])
    lw = -_sigmoid(z) * math.exp(-0.5)
    a = _sigmoid(a0_ref[...] + _dot33(wa, aup_ref[...]))
    g = _dot33(_sigmoid(gd), gup_ref[...])
    hb = hb_ref[...]
    kk = k * kk_ref[...]
    kk = kk * lax.rsqrt(_dot_exact_rhs(kk * kk, hb) + 1e-12)
    ka = ka_ref[...]
    ss_ref[:, 0:RW] = r
    ss_ref[:, RW:2 * RW] = v
    ss_ref[:, 2 * RW:3 * RW] = kk
    rk2 = jnp.zeros_like(r)
    for d in range(2):
        ad = a[:, d * RW:(d + 1) * RW]
        k2 = k * (1.0 + (ad - 1.0) * ka)
        sd_ref[d, :, 0:RW] = lw[:, d * RW:(d + 1) * RW]
        sd_ref[d, :, RW:2 * RW] = k2
        sd_ref[d, :, 2 * RW:3 * RW] = kk * ad
        rk2 = rk2 + r * k2
    bonus = _dot_exact_rhs(rk2 * rk_ref[...], hb) * v
    gb_ref[:, 0:RW] = g
    gb_ref[:, RW:2 * RW] = bonus


def rwkv_prep(pa2, wts, tm):
    N = pa2.shape[0]
    full = lambda a: pl.BlockSpec(a.shape, lambda i: (0,) * a.ndim)
    return pl.pallas_call(
        _rwkv_prep_kernel,
        out_shape=[jax.ShapeDtypeStruct((N, 3 * RW), F32),
                   jax.ShapeDtypeStruct((2, N, 3 * RW), F32),
                   jax.ShapeDtypeStruct((N, 2 * RW), F32)],
        grid=(N // tm,),
        in_specs=[pl.BlockSpec((tm, PA_W), lambda i: (i, 0))] + [full(a) for a in wts],
        out_specs=[pl.BlockSpec((tm, 3 * RW), lambda i: (i, 0)),
                   pl.BlockSpec((2, tm, 3 * RW), lambda i: (0, i, 0)),
                   pl.BlockSpec((tm, 2 * RW), lambda i: (i, 0))],
        compiler_params=_cp(("arbitrary",)),
        name="rwkv_prep",
    )(pa2, *wts)


def _rwkv_scan_kernel(ss_ref, sd_ref, s0_ref, y_ref, sfin_ref, s_scr, *, nc):
    d = pl.program_id(0)
    n = pl.program_id(2)
    sgn = 1 - 2 * d
    C = CHUNK
    P = 2 * C

    @pl.when(n == 0)
    def _():
        s_scr[...] = s0_ref[0, 0]

    row = lax.broadcasted_iota(I32, (P, P), 0)
    col = lax.broadcasted_iota(I32, (P, P), 1)
    same = (row >> 6) == (col >> 6)
    dlt = ((row & (C - 1)) - (col & (C - 1))) * sgn
    strict = same & (dlt > 0)
    incl = same & (dlt >= 0)
    eye = (row == col).astype(F32)
    rc = lax.broadcasted_iota(I32, (C, C), 0)
    cc = lax.broadcasted_iota(I32, (C, C), 1)
    tri = jnp.where((rc - cc) * sgn >= 0, 1.0, 0.0).astype(BF16)
    head0 = lax.broadcasted_iota(I32, (C, LANE), 1) < HEAD

    def stack(x):
        return jnp.concatenate([jnp.where(head0, x, 0.0), jnp.where(head0, 0.0, x)], axis=0)

    for p in range(RW // LANE):
        lo, hi = p * LANE, (p + 1) * LANE
        r = ss_ref[:, lo:hi]
        v = ss_ref[:, RW + lo:RW + hi]
        kk = ss_ref[:, 2 * RW + lo:2 * RW + hi]
        lw = sd_ref[0, :, lo:hi]
        k2 = sd_ref[0, :, RW + lo:RW + hi]
        b = sd_ref[0, :, 2 * RW + lo:2 * RW + hi]
        cl = _dot_exact_lhs(tri, lw)
        tot = jnp.sum(lw, axis=0, keepdims=True)
        pinv = jnp.exp(-cl)
        ptot = jnp.exp(tot)
        a_st = stack(-kk * jnp.exp(cl - lw))
        r_st = stack(r * jnp.exp(cl))
        b_st = stack(b * pinv)
        k_st = stack(k2 * pinv)
        v_st = stack(v)
        left = jnp.concatenate([a_st, r_st], axis=0)
        right = jnp.concatenate([b_st, k_st], axis=0)
        aa = _bdot(left, right, NT_DIMS)
        nmat = jnp.where(strict, aa[0:P, 0:P], 0.0)
        a_ak = jnp.where(strict, aa[0:P, P:2 * P], 0.0)
        a_rb = jnp.where(incl, aa[P:2 * P, 0:P], 0.0)
        a_rk = jnp.where(incl, aa[P:2 * P, P:2 * P], 0.0)
        tinv = eye
        for lvl in range(6):
            e = jnp.where(((row >> (lvl + 1)) == (col >> (lvl + 1))) & ((row >> lvl) != (col >> lvl)), nmat, 0.0)
            tinv = tinv + _bdot(tinv, _bdot(e, tinv))
        s = s_scr[p]
        a_s = _bdot(left, s, NT_DIMS)
        u = _bdot(tinv, a_s[0:P] + _bdot(a_ak, v_st))
        uv = jnp.concatenate([u, v_st], axis=0)
        y_st = a_s[P:2 * P] + _bdot(jnp.concatenate([a_rb, a_rk], axis=1), uv)
        y_ref[0, :, lo:hi] = y_st[0:C] + y_st[C:P]
        bk = jnp.concatenate([b_st, k_st], axis=0) * ptot
        s_scr[p] = s * ptot + _bdot(uv, bk, TN_DIMS)

    @pl.when(n == nc - 1)
    def _():
        sfin_ref[0, 0] = s_scr[...]


def rwkv_scan(ss, sd, s0, B, T):
    nc = T // CHUNK
    npair = RW // LANE
    chunk = lambda d, b, n: b * nc + n + d * (nc - 1 - 2 * n)
    kern = functools.partial(_rwkv_scan_kernel, nc=nc)
    return pl.pallas_call(
        kern,
        out_shape=[jax.ShapeDtypeStruct((2, B * T, RW), F32),
                   jax.ShapeDtypeStruct((2, B, npair, LANE, LANE), F32)],
        grid=(2, B, nc),
        in_specs=[pl.BlockSpec((CHUNK, 3 * RW), lambda d, b, n: (chunk(d, b, n), 0)),
                  pl.BlockSpec((1, CHUNK, 3 * RW), lambda d, b, n: (d, chunk(d, b, n), 0)),
                  pl.BlockSpec((1, 1, npair, LANE, LANE), lambda d, b, n: (d, b, 0, 0, 0))],
        out_specs=[pl.BlockSpec((1, CHUNK, RW), lambda d, b, n: (d, chunk(d, b, n), 0)),
                   pl.BlockSpec((1, 1, npair, LANE, LANE), lambda d, b, n: (d, b, 0, 0, 0))],
        scratch_shapes=[pltpu.VMEM((npair, LANE, LANE), F32)],
        compiler_params=_cp(("arbitrary", "arbitrary", "arbitrary")),
        name="rwkv_scan",
    )(ss, sd, s0)


GLA_KP = 256


def _gla_kernel(pbg_ref, aup_ref, ab_ref, s0_ref, o_ref, sfin_ref, s_scr, *, nc):
    d = pl.program_id(0)
    n = pl.program_id(2)
    sgn = 1 - 2 * d
    C = CHUNK

    @pl.when(n == 0)
    def _():
        s_scr[...] = s0_ref[0, 0]

    q = pbg_ref[:, 0:GLA_KP]
    k = pbg_ref[:, GLA_KP:2 * GLA_KP]
    v = pbg_ref[:, 2 * GLA_KP:2 * GLA_KP + RW]
    ad = pbg_ref[:, 2 * GLA_KP + RW:2 * GLA_KP + RW + LANE]
    x = _dot33(ad, aup_ref[0]) + ab_ref[0]
    la = (jnp.minimum(x, 0.0) - jnp.log(1.0 + jnp.exp(-jnp.abs(x)))) * (1.0 / GLA_TEMP)
    rc = lax.broadcasted_iota(I32, (C, C), 0)
    cc = lax.broadcasted_iota(I32, (C, C), 1)
    tri = jnp.where((rc - cc) * sgn >= 0, 1.0, 0.0).astype(BF16)
    bcum = _dot_exact_lhs(tri, la)
    tot = jnp.sum(la, axis=0, keepdims=True)
    q_in = q * jnp.exp(bcum) * (GLA_K ** -0.5)
    k_in = k * jnp.exp(-bcum)
    k_st = k * jnp.exp(tot - bcum)
    klane = lax.broadcasted_iota(I32, (C, GLA_KP), 1)
    q_rows = jnp.concatenate([jnp.where((klane >> 5) == h, q_in, 0.0) for h in range(GLA_HEADS)], axis=0)
    att = _bdot(q_rows, k_in, NT_DIMS)
    rt = lax.broadcasted_iota(I32, (GLA_HEADS * C, C), 0) & (C - 1)
    ct = lax.broadcasted_iota(I32, (GLA_HEADS * C, C), 1)
    att = jnp.where((rt - ct) * sgn >= 0, att, 0.0)
    s = s_scr[...]
    o_rows = _bdot(att, v) + _bdot(q_rows, s, NT_DIMS)
    vlane = lax.broadcasted_iota(I32, (C, RW), 1)
    o = jnp.zeros((C, RW), F32)
    for h in range(GLA_HEADS):
        o = o + jnp.where((vlane >> 6) == h, o_rows[h * C:(h + 1) * C], 0.0)
    o_ref[0] = o
    sv = lax.broadcasted_iota(I32, (RW, GLA_KP), 0) >> 6
    sk = lax.broadcasted_iota(I32, (RW, GLA_KP), 1) >> 5
    s_scr[...] = s * jnp.exp(tot) + jnp.where(sv == sk, _bdot(v, k_st, TN_DIMS), 0.0)

    @pl.when(n == nc - 1)
    def _():
        sfin_ref[0, 0] = s_scr[...]


def gla_scan(pbg, aup, ab, s0, B, T):
    nc = T // CHUNK
    chunk = lambda d, b, n: b * nc + n + d * (nc - 1 - 2 * n)
    kern = functools.partial(_gla_kernel, nc=nc)
    return pl.pallas_call(
        kern,
        out_shape=[jax.ShapeDtypeStruct((2, B * T, RW), F32),
                   jax.ShapeDtypeStruct((2, B, RW, GLA_KP), F32)],
        grid=(2, B, nc),
        in_specs=[pl.BlockSpec((CHUNK, PBG_W), lambda d, b, n: (chunk(d, b, n), 0)),
                  pl.BlockSpec((1, LANE, GLA_KP), lambda d, b, n: (d, 0, 0)),
                  pl.BlockSpec((1, 1, GLA_KP), lambda d, b, n: (d, 0, 0)),
                  pl.BlockSpec((1, 1, RW, GLA_KP), lambda d, b, n: (d, b, 0, 0))],
        out_specs=[pl.BlockSpec((1, CHUNK, RW), lambda d, b, n: (d, chunk(d, b, n), 0)),
                   pl.BlockSpec((1, 1, RW, GLA_KP), lambda d, b, n: (d, b, 0, 0))],
        scratch_shapes=[pltpu.VMEM((RW, GLA_KP), F32)],
        compiler_params=_cp(("arbitrary", "arbitrary", "arbitrary")),
        name="gla_scan",
    )(pbg, aup, ab, s0)


S5_PITCH = 72


def _s5_kernel(u_ref, a_ref, bbd_ref, cbd_ref, s0_ref, y_ref, sfin_ref, x_scr, st_scr, *, nc, nb):
    d = pl.program_id(0)
    n = pl.program_id(1)
    C = CHUNK
    half = S5_LANES // 2

    @pl.when(n == 0)
    def _():
        st_scr[...] = s0_ref[0]

    nslab = S5_LANES // LANE
    bbd = bbd_ref[0]
    for b in range(nb):
        bu = _bdot(u_ref[b], bbd)
        for j in range(nslab):
            x_scr[j, b * S5_PITCH:b * S5_PITCH + C, :] = bu[:, j * LANE:(j + 1) * LANE]
    a_re = a_ref[0, :, 0:half]
    a_im = a_ref[0, :, half:S5_LANES]

    def step(i, carry):
        re, im = carry
        t = i + d * (C - 1 - 2 * i)
        rows = pl.ds(t, nb, stride=S5_PITCH)
        bu = jnp.concatenate([x_scr[j, rows, :] for j in range(nslab)], axis=1)
        nre = a_re * re - a_im * im + bu[:, 0:half]
        nim = a_re * im + a_im * re + bu[:, half:S5_LANES]
        for j in range(nslab // 2):
            x_scr[j, rows, :] = nre[:, j * LANE:(j + 1) * LANE]
            x_scr[nslab // 2 + j, rows, :] = nim[:, j * LANE:(j + 1) * LANE]
        return nre, nim

    st = st_scr[...]
    re, im = lax.fori_loop(0, C, step, (st[:, 0:half], st[:, half:S5_LANES]))
    st_scr[...] = jnp.concatenate([re, im], axis=1)
    cbd = cbd_ref[0]
    for b in range(nb):
        xs = jnp.concatenate([x_scr[j, b * S5_PITCH:b * S5_PITCH + C, :] for j in range(nslab)], axis=1)
        y_ref[0, b] = _bdot(xs, cbd)

    @pl.when(n == nc - 1)
    def _():
        sfin_ref[0] = st_scr[...]


def s5_scan(u3, a_bar, bbd, cbd, s0):
    B, T, _ = u3.shape
    nc = T // CHUNK
    chunk = lambda d, n: n + d * (nc - 1 - 2 * n)
    kern = functools.partial(_s5_kernel, nc=nc, nb=B)
    return pl.pallas_call(
        kern,
        out_shape=[jax.ShapeDtypeStruct((2, B, T, S5_W), F32),
                   jax.ShapeDtypeStruct((2, B, S5_LANES), F32)],
        grid=(2, nc),
        in_specs=[pl.BlockSpec((B, CHUNK, S5_W), lambda d, n: (0, chunk(d, n), 0)),
                  pl.BlockSpec((1, 1, S5_LANES), lambda d, n: (d, 0, 0)),
                  pl.BlockSpec((1, S5_W, S5_LANES), lambda d, n: (d, 0, 0)),
                  pl.BlockSpec((1, S5_LANES, S5_W), lambda d, n: (d, 0, 0)),
                  pl.BlockSpec((1, B, S5_LANES), lambda d, n: (d, 0, 0))],
        out_specs=[pl.BlockSpec((1, B, CHUNK, S5_W), lambda d, n: (d, 0, chunk(d, n), 0)),
                   pl.BlockSpec((1, B, S5_LANES), lambda d, n: (d, 0, 0))],
        scratch_shapes=[pltpu.VMEM((S5_LANES // LANE, B * S5_PITCH, LANE), F32),
                        pltpu.VMEM((B, S5_LANES), F32)],
        compiler_params=_cp(("arbitrary", "arbitrary")),
        name="s5_scan",
    )(u3, a_bar, bbd, cbd, s0)


def _tile_rows_store(ref, val):
    tm = val.shape[0]
    for s in range(val.shape[1] // LANE):
        ref[pl.ds(s, tm, stride=SUB), :] = val[:, s * LANE:(s + 1) * LANE]


def _tile_rows_load(ref, tm, base=0):
    return jnp.concatenate([ref[pl.ds(base + s, tm, stride=SUB), :] for s in range(SUB)], axis=1)


def _mix_out_kernel(y2_ref, gb_ref, o2_ref, gg_ref, yd_ref, u_ref, x_ref, mod_ref,
                    hb_ref, lng_ref, lnb_ref, gng_ref, s5d_ref, gluw_ref, glub_ref, wout_ref, l1g_ref, l1b_ref,
                    x1_ref, h2_ref, h2t_ref, *, alpha):
    hb = hb_ref[...]
    inv = 1.0 / HEAD
    y = y2_ref[0] + y2_ref[1]
    yc = y - _dot_exact_rhs(y, hb) * inv
    var = _dot_exact_rhs(yc * yc, hb) * inv
    gn = yc * lax.rsqrt(var + RWKV_GN_EPS) * lng_ref[...] + lnb_ref[...]
    y_a = (gn + gb_ref[:, RW:2 * RW]) * gb_ref[:, 0:RW]
    o = o2_ref[0] + o2_ref[1]
    o = o * lax.rsqrt(_dot_exact_rhs(o * o, hb) * inv + 1e-6) * gng_ref[...]
    y_b = o * _silu(gg_ref[...])
    c = s5d_ref[...] * u_ref[...] + yd_ref[0] + yd_ref[1]
    c = 0.5 * c * (1.0 + jnp.tanh(math.sqrt(2.0 / math.pi) * (c + 0.044715 * (c * c * c))))
    y_c = c * _sigmoid(_bdot(c, gluw_ref[...]) + glub_ref[...])
    y_mix = (_bdot(y_a, wout_ref[0:RW, :]) + _bdot(y_b, wout_ref[RW:2 * RW, :])
             + _bdot(y_c, wout_ref[2 * RW:2 * RW + S5_W, :]))
    x1 = _layer_norm(alpha * x_ref[...] + mod_ref[0, 2:3, :] * y_mix) * l1g_ref[...] + l1b_ref[...]
    x1_ref[...] = x1
    h2 = _layer_norm(x1) * (1.0 + mod_ref[0, 4:5, :]) + mod_ref[0, 3:4, :]
    h2_ref[...] = h2
    _tile_rows_store(h2t_ref, h2)


def mix_out(y2, gb, o2, gg, yd, u, x2, mod3, wts, row_of_tile, tm, alpha):
    N, D = x2.shape
    full = lambda a: pl.BlockSpec(a.shape, lambda i: (0,) * a.ndim)
    kern = functools.partial(_mix_out_kernel, alpha=alpha)
    return pl.pallas_call(
        kern,
        out_shape=[jax.ShapeDtypeStruct((N, D), F32),
                   jax.ShapeDtypeStruct((N, D), F32),
                   jax.ShapeDtypeStruct((N * SUB, LANE), F32)],
        grid=(N // tm,),
        in_specs=[pl.BlockSpec((2, tm, RW), lambda i: (0, i, 0)),
                  pl.BlockSpec((tm, 2 * RW), lambda i: (i, 0)),
                  pl.BlockSpec((2, tm, RW), lambda i: (0, i, 0)),
                  pl.BlockSpec((tm, RW), lambda i: (i, 0)),
                  pl.BlockSpec((2, tm, S5_W), lambda i: (0, i, 0)),
                  pl.BlockSpec((tm, S5_W), lambda i: (i, 0)),
                  pl.BlockSpec((tm, D), lambda i: (i, 0)),
                  pl.BlockSpec((1, 6, D), lambda i: (row_of_tile(i), 0, 0))] + [full(a) for a in wts],
        out_specs=[pl.BlockSpec((tm, D), lambda i: (i, 0)),
                   pl.BlockSpec((tm, D), lambda i: (i, 0)),
                   pl.BlockSpec((tm * SUB, LANE), lambda i: (i, 0))],
        compiler_params=_cp(("arbitrary",)),
        name="mix_out",
    )(y2, gb, o2, gg, yd, u, x2, mod3, *wts)


def _first_max(x, idx, big):
    m = jnp.max(x, axis=0, keepdims=True)
    first = jnp.min(jnp.where(x == m, idx, big), axis=0, keepdims=True)
    return m, first


def _router_kernel(h_ref, rwt_ref, bias_ref, e_ref, w_ref):
    tm = h_ref.shape[0]
    gsz = N_EXPERTS // N_EGROUPS
    ninf = -jnp.inf
    s = _sigmoid(_dot33(rwt_ref[...], h_ref[...], NT_DIMS))
    ssel = s + bias_ref[:, 0:1]
    gi = lax.broadcasted_iota(I32, (gsz, tm), 0)
    gscore = []
    for g in range(N_EGROUPS):
        xg = ssel[g * gsz:(g + 1) * gsz, :]
        m1, i1 = _first_max(xg, gi, gsz)
        m2 = jnp.max(jnp.where(gi == i1, ninf, xg), axis=0, keepdims=True)
        gscore.append(m1 + m2)
    cur = jnp.concatenate(gscore, axis=0)
    gidx = lax.broadcasted_iota(I32, (N_EGROUPS, tm), 0)
    picked = jnp.zeros((N_EGROUPS, tm), F32)
    for _ in range(TOPK_GROUPS):
        _, first = _first_max(cur, gidx, N_EGROUPS)
        hit = gidx == first
        picked = jnp.where(hit, 1.0, picked)
        cur = jnp.where(hit, ninf, cur)
    x = jnp.concatenate(
        [jnp.where(picked[g:g + 1, :] > 0.5, ssel[g * gsz:(g + 1) * gsz, :], ninf) for g in range(N_EGROUPS)], axis=0)
    ei = lax.broadcasted_iota(I32, (N_EXPERTS, tm), 0)
    idxs, ws = [], []
    for _ in range(TOP_K):
        _, first = _first_max(x, ei, N_EXPERTS)
        hit = ei == first
        idxs.append(first)
        ws.append(jnp.sum(jnp.where(hit, s, 0.0), axis=0, keepdims=True))
        x = jnp.where(hit, ninf, x)
    w = jnp.concatenate(ws, axis=0)
    e_ref[...] = jnp.concatenate(idxs, axis=0)
    w_ref[...] = w / jnp.sum(w, axis=0, keepdims=True) * ROUTE_SCALE


def moe_router(h2, rwt, bias_b, tm):
    N, D = h2.shape
    return pl.pallas_call(
        _router_kernel,
        out_shape=[jax.ShapeDtypeStruct((TOP_K, N), I32), jax.ShapeDtypeStruct((TOP_K, N), F32)],
        grid=(N // tm,),
        in_specs=[pl.BlockSpec((tm, D), lambda i: (i, 0)),
                  pl.BlockSpec((N_EXPERTS, D), lambda i: (0, 0)),
                  pl.BlockSpec((N_EXPERTS, LANE), lambda i: (0, 0))],
        out_specs=[pl.BlockSpec((TOP_K, tm), lambda i: (0, i)),
                   pl.BlockSpec((TOP_K, tm), lambda i: (0, i))],
        compiler_params=_cp(("arbitrary",)),
        name="moe_router",
    )(h2, rwt, bias_b)


def _moe_count_kernel(e_ref, pstart_ref, plan_ref, blk_ref, cnt_scr, *, nt, nbp):
    i = pl.program_id(0)
    tp = e_ref.shape[1]

    @pl.when(i == 0)
    def _():
        cnt_scr[...] = jnp.zeros_like(cnt_scr)

    ei = lax.broadcasted_iota(I32, (N_EXPERTS, tp), 0)
    acc = jnp.zeros((N_EXPERTS, 1), F32)
    for k in range(TOP_K):
        acc = acc + jnp.sum(jnp.where(ei == e_ref[k:k + 1, :], 1.0, 0.0), axis=1, keepdims=True)
    cnt_scr[...] = cnt_scr[...] + acc

    @pl.when(i == nt - 1)
    def _():
        cnt = cnt_scr[...].astype(I32)
        padded = (cnt + (MOE_BLK - 1)) & (-MOE_BLK)
        r = lax.broadcasted_iota(I32, (N_EXPERTS, N_EXPERTS), 0)
        c = lax.broadcasted_iota(I32, (N_EXPERTS, N_EXPERTS), 1)
        tri = jnp.where(c <= r, 1.0, 0.0).astype(BF16)
        padded_b = jnp.broadcast_to(padded.astype(F32), (N_EXPERTS, LANE))
        p_end = _dot_exact_lhs(tri, padded_b)
        pstart = p_end - padded_b
        pstart_ref[...] = pstart.astype(I32)
        diag = r == c
        ps_row = jnp.sum(jnp.where(diag, pstart[:, 0:1], 0.0), axis=0, keepdims=True)
        cnt_row = jnp.sum(jnp.where(diag, cnt_scr[...], 0.0), axis=0, keepdims=True)
        plan_ref[...] = jnp.concatenate([ps_row, cnt_row, jnp.zeros((SUB - 2, N_EXPERTS), F32)], axis=0).astype(I32)
        lim =(lax.broadcasted_iota(I32, (N_EXPERTS, nbp), 1) * MOE_BLK).astype(F32)
        be = jnp.sum(jnp.where(p_end[:, 0:1] <= lim, 1.0, 0.0), axis=0, keepdims=True)
        be = jnp.minimum(be, N_EXPERTS - 1.0)
        nused = jnp.max(p_end[:, 0:1], axis=0, keepdims=True) * (1.0 / MOE_BLK)
        blk_ref[...] = jnp.concatenate([jnp.broadcast_to(be, (SUB // 2, nbp)),
                                        jnp.broadcast_to(nused, (SUB // 2, nbp))], axis=0).astype(I32)


def moe_counts(eidx, tp, nbp):
    N = eidx.shape[1]
    nt = N // tp
    kern = functools.partial(_moe_count_kernel, nt=nt, nbp=nbp)
    return pl.pallas_call(
        kern,
        out_shape=[jax.ShapeDtypeStruct((N_EXPERTS, LANE), I32),
                   jax.ShapeDtypeStruct((SUB, N_EXPERTS), I32),
                   jax.ShapeDtypeStruct((SUB, nbp), I32)],
        grid=(nt,),
        in_specs=[pl.BlockSpec((TOP_K, tp), lambda i: (0, i))],
        out_specs=[pl.BlockSpec((N_EXPERTS, LANE), lambda i: (0, 0)),
                   pl.BlockSpec((SUB, N_EXPERTS), lambda i: (0, 0)),
                   pl.BlockSpec((SUB, nbp), lambda i: (0, 0))],
        scratch_shapes=[pltpu.VMEM((N_EXPERTS, 1), F32)],
        compiler_params=_cp(("arbitrary",)),
        name="moe_counts",
    )(eidx)


def _moe_dest_kernel(e_ref, pstart_ref, dest_ref, base_scr):
    i = pl.program_id(0)
    tp = e_ref.shape[1]

    @pl.when(i == 0)
    def _():
        base_scr[...] = pstart_ref[:, 0:1].astype(F32)

    ei = lax.broadcasted_iota(I32, (N_EXPERTS, tp), 0)
    r = lax.broadcasted_iota(I32, (tp, tp), 0)
    c = lax.broadcasted_iota(I32, (tp, tp), 1)
    tri = jnp.where(r <= c, 1.0, 0.0).astype(BF16)
    base = base_scr[...]
    rows = []
    for k in range(TOP_K):
        hit = ei == e_ref[k:k + 1, :]
        oh = jnp.where(hit, 1.0, 0.0)
        cum = _bdot(oh, tri)
        rows.append(jnp.sum(jnp.where(hit, cum - 1.0 + base, 0.0), axis=0, keepdims=True))
        base = base + cum[:, tp - 1:tp]
    base_scr[...] = base
    dest_ref[...] = jnp.concatenate(rows, axis=0).astype(I32)


def moe_dest(eidx, pstart, tp):
    N = eidx.shape[1]
    return pl.pallas_call(
        _moe_dest_kernel,
        out_shape=jax.ShapeDtypeStruct((TOP_K, N), I32),
        grid=(N // tp,),
        in_specs=[pl.BlockSpec((TOP_K, tp), lambda i: (0, i)),
                  pl.BlockSpec((N_EXPERTS, LANE), lambda i: (0, 0))],
        out_specs=pl.BlockSpec((TOP_K, tp), lambda i: (0, i)),
        scratch_shapes=[pltpu.VMEM((N_EXPERTS, 1), F32)],
        compiler_params=_cp(("arbitrary",)),
        name="moe_dest",
    )(eidx, pstart)


def _tile_at(ref, token):
    return ref.at[pl.ds(pl.multiple_of(token * SUB, SUB), SUB), :]


def _moe_scatter_kernel(dest_ref, h_hbm, xs_hbm, sem):
    i = pl.program_id(0)
    tp = dest_ref.shape[1]

    def issue(j, carry):
        for k in range(TOP_K):
            pltpu.make_async_copy(_tile_at(h_hbm, i * tp + j), _tile_at(xs_hbm, dest_ref[k, j]), sem).start()
        return carry

    lax.fori_loop(0, tp, issue, 0)

    def drain(j, carry):
        for k in range(TOP_K):
            pltpu.make_async_copy(h_hbm.at[pl.ds(0, SUB), :], xs_hbm.at[pl.ds(0, SUB), :], sem).wait()
        return carry

    lax.fori_loop(0, tp, drain, 0)


def moe_scatter(dest, h2t, n_slots, tp):
    N = dest.shape[1]
    return pl.pallas_call(
        _moe_scatter_kernel,
        out_shape=jax.ShapeDtypeStruct((n_slots * SUB, LANE), F32),
        grid=(N // tp,),
        in_specs=[pl.BlockSpec((TOP_K, tp), lambda i: (0, i), memory_space=pltpu.SMEM),
                  pl.BlockSpec(memory_space=pl.ANY)],
        out_specs=pl.BlockSpec(memory_space=pl.ANY),
        scratch_shapes=[pltpu.SemaphoreType.DMA(())],
        compiler_params=_cp(("arbitrary",)),
        name="moe_scatter",
    )(dest, h2t)


def _moe_padfill_kernel(ps_ref, xs_in, xs_hbm, zero_scr, sem):
    del xs_in
    e = pl.program_id(0)
    zero_scr[...] = jnp.zeros_like(zero_scr)
    start = ps_ref[0, e]
    cnt = ps_ref[1, e]
    padded = (cnt + (MOE_BLK - 1)) & (-MOE_BLK)

    def issue(r, carry):
        pltpu.make_async_copy(zero_scr, _tile_at(xs_hbm, start + r), sem).start()
        return carry

    lax.fori_loop(cnt, padded, issue, 0)

    def drain(r, carry):
        pltpu.make_async_copy(zero_scr, xs_hbm.at[pl.ds(0, SUB), :], sem).wait()
        return carry

    lax.fori_loop(cnt, padded, drain, 0)


def moe_padfill(pstart, xs):
    return pl.pallas_call(
        _moe_padfill_kernel,
        out_shape=jax.ShapeDtypeStruct(xs.shape, xs.dtype),
        grid=(N_EXPERTS,),
        in_specs=[pl.BlockSpec(memory_space=pltpu.SMEM),
                  pl.BlockSpec(memory_space=pl.ANY)],
        out_specs=pl.BlockSpec(memory_space=pl.ANY),
        scratch_shapes=[pltpu.VMEM((SUB, LANE), F32), pltpu.SemaphoreType.DMA(())],
        input_output_aliases={1: 0},
        compiler_params=_cp(("arbitrary",)),
        name="moe_padfill",
    )(pstart, xs)


def _experts_kernel(blk_ref, xs_ref, w13_ref, w2_ref, y_ref, w13_bf, w2_bf):
    i = pl.program_id(0)
    nb = xs_ref.shape[0] // SUB
    ff = w2_bf.shape[0]

    @pl.when(i < blk_ref[SUB // 2, 0])
    def _():
        prev = blk_ref[0, jnp.maximum(i - 1, 0)]

        @pl.when((i == 0) | (blk_ref[0, i] != prev))
        def _():
            w13_bf[...] = w13_ref[0, 0].astype(BF16)
            w2_bf[...] = w2_ref[0, 0].astype(BF16)

        x = _tile_rows_load(xs_ref, nb).astype(BF16)
        h = jnp.dot(x, w13_bf[...], preferred_element_type=F32)
        act = (_silu(h[:, 0:ff]) * h[:, ff:2 * ff]).astype(BF16)
        _tile_rows_store(y_ref, jnp.dot(act, w2_bf[...], preferred_element_type=F32))


def moe_experts(blk, xs, w13, w2, layer, nb_total):
    _, E, D, F2 = w13.shape
    last = lambda i, b: jnp.minimum(i, b[SUB // 2, 0] - 1)
    grid_spec = pltpu.PrefetchScalarGridSpec(
        num_scalar_prefetch=1,
        grid=(nb_total,),
        in_specs=[pl.BlockSpec((MOE_BLK * SUB, LANE), lambda i, b: (last(i, b), 0)),
                  pl.BlockSpec((1, 1, D, F2), lambda i, b: (layer, b[0, last(i, b)], 0, 0)),
                  pl.BlockSpec((1, 1, F2 // 2, D), lambda i, b: (layer, b[0, last(i, b)], 0, 0))],
        out_specs=pl.BlockSpec((MOE_BLK * SUB, LANE), lambda i, b: (last(i, b), 0)),
        scratch_shapes=[pltpu.VMEM((D, F2), BF16), pltpu.VMEM((F2 // 2, D), BF16)])
    return pl.pallas_call(
        _experts_kernel,
        out_shape=jax.ShapeDtypeStruct(xs.shape, F32),
        grid_spec=grid_spec,
        compiler_params=_cp(("arbitrary",)),
        name="moe_experts",
    )(blk, xs, w13, w2)


def _moe_combine_kernel(dest_ref, wt_ref, h_ref, x_ref, mod_ref, s13_ref, s2_ref, l2g_ref, l2b_ref, y_hbm,
                        o_ref, g_scr, sem, *, alpha):
    tm = h_ref.shape[0]
    ff = s2_ref.shape[0]

    def issue(j, carry):
        for k in range(TOP_K):
            pltpu.make_async_copy(_tile_at(y_hbm, dest_ref[k, j]), _tile_at(g_scr.at[k], j), sem).start()
        return carry

    lax.fori_loop(0, tm, issue, 0)
    hs = _bdot(h_ref[...], s13_ref[...])
    f = _bdot(_silu(hs[:, 0:ff]) * hs[:, ff:2 * ff], s2_ref[...])

    def drain(j, carry):
        for k in range(TOP_K):
            pltpu.make_async_copy(y_hbm.at[pl.ds(0, SUB), :], g_scr.at[0, pl.ds(0, SUB), :], sem).wait()
        return carry

    lax.fori_loop(0, tm, drain, 0)
    for k in range(TOP_K):
        f = f + wt_ref[:, k:k + 1] * _tile_rows_load(g_scr.at[k], tm)
    o_ref[...] = (_layer_norm(alpha * x_ref[...] + mod_ref[0, 5:6, :] * f) * l2g_ref[...] + l2b_ref[...])


def moe_combine(dest, wt_t, h2, x1, mod3, wts, y, row_of_tile, tm, alpha):
    N, D = h2.shape
    full = lambda a: pl.BlockSpec(a.shape, lambda i: (0,) * a.ndim)
    kern = functools.partial(_moe_combine_kernel, alpha=alpha)
    return pl.pallas_call(
        kern,
        out_shape=jax.ShapeDtypeStruct((N, D), F32),
        grid=(N // tm,),
        in_specs=[pl.BlockSpec((TOP_K, tm), lambda i: (0, i), memory_space=pltpu.SMEM),
                  pl.BlockSpec((tm, TOP_K), lambda i: (i, 0)),
                  pl.BlockSpec((tm, D), lambda i: (i, 0)),
                  pl.BlockSpec((tm, D), lambda i: (i, 0)),
                  pl.BlockSpec((1, 6, D), lambda i: (row_of_tile(i), 0, 0))]
                 + [full(a) for a in wts] + [pl.BlockSpec(memory_space=pl.ANY)],
        out_specs=pl.BlockSpec((tm, D), lambda i: (i, 0)),
        scratch_shapes=[pltpu.VMEM((TOP_K, tm * SUB, LANE), F32), pltpu.SemaphoreType.DMA(())],
        compiler_params=_cp(("arbitrary",)),
        name="moe_combine",
    )(dest, wt_t, h2, x1, mod3, *wts, y)


def moe_ffn(h2, h2t, x1, mod3, mp, row_of_tile, tm, alpha):
    N = h2.shape[0]
    nb_total = (N * TOP_K + MOE_BLK - 1) // MOE_BLK + N_EXPERTS
    nbp = ((nb_total + LANE - 1) // LANE) * LANE
    eidx, wts = moe_router(h2, mp['rwt'], mp['bias'], tm)
    pstart, plan, blk = moe_counts(eidx, tm, nbp)
    dest = moe_dest(eidx, pstart, tm)
    xs = moe_scatter(dest, h2t, nb_total * MOE_BLK, tm)
    xs = moe_padfill(plan, xs)
    y = moe_experts(blk, xs, mp['w13'], mp['w2'], mp['layer'], nb_total)
    return moe_combine(dest, wts.T, h2, x1, mod3, mp['comb'], y, row_of_tile, tm, alpha)


def token_mixers(x2, B, T, W, mod3, row_of_tile, lp, states, tm):
    pa, pbg, gg, u = inproj(x2, mod3, lp['w_in'], row_of_tile, tm)
    pac = short_conv(pa.reshape(B, T, PA_W), lp['taps'], W).reshape(B * T, PA_W)
    ss, sd, gb = rwkv_prep(pac, lp['prep'], tm)
    s_rwkv, s_gla, s_s5 = states
    y2, f_rwkv = rwkv_scan(ss, sd, s_rwkv, B, T)
    o2, f_gla = gla_scan(pbg, lp['gla_aup'], lp['gla_ab'], s_gla, B, T)
    yd, f_s5 = s5_scan(u.reshape(B, T, S5_W), lp['s5_a'], lp['s5_bbd'], lp['s5_cbd'], s_s5)
    return (y2, gb, o2, gg, yd.reshape(2, B * T, S5_W), u), (f_rwkv, f_gla, f_s5)


def zero_states(B):
    return (jnp.zeros((2, B, RW // LANE, LANE, LANE), F32),
            jnp.zeros((2, B, RW, GLA_KP), F32),
            jnp.zeros((2, B, S5_LANES), F32))


def _inproj_columns():
    r_cols = 3 * RW + 2 * DECAY_RANK + 2 * ICLR_RANK + GATE_RANK
    kd = GLA_HEADS * GLA_K
    gq, gk, gv = r_cols, r_cols + kd, r_cols + 2 * kd
    gg = gv + RW
    gad = gg + RW
    pc = gad + 2 * GLA_RANK
    z = lambda n: [-1] * n
    cols = list(range(0, r_cols)) + z(PA_W - r_cols)
    cols += list(range(gq, gq + kd)) + z(256 - kd)
    cols += list(range(gk, gk + kd)) + z(256 - kd)
    cols += list(range(gv, gv + RW))
    cols += list(range(gad, gad + 2 * GLA_RANK)) + z(LANE - 2 * GLA_RANK)
    cols += list(range(gg, gg + RW))
    cols += list(range(pc, pc + S5_W))
    assert len(cols) == NP_W
    return np.asarray(cols, np.int32)


def _head_block_ones():
    h = np.arange(RW) // HEAD
    return jnp.asarray(h[:, None] == h[None, :], BF16)


def _layer_params(l, p):
    D = p['w_in'].shape[1]
    cols = _inproj_columns()
    w_in = jnp.concatenate([p['w_in'][l], jnp.zeros((D, 1), F32)], axis=1)
    w_in = jnp.take(w_in, jnp.asarray(np.where(cols < 0, w_in.shape[1] - 1, cols)), axis=1).astype(BF16)
    taps = p['rwkv_conv'][l].reshape(9, -1)
    taps = jnp.concatenate([taps, jnp.zeros((9, PA_W - taps.shape[1]), F32)], axis=1)
    wup = jnp.zeros((LANE, 2 * RW), F32)
    aup = jnp.zeros((LANE, 2 * RW), F32)
    for d in range(2):
        wup = wup.at[d * DECAY_RANK:(d + 1) * DECAY_RANK, d * RW:(d + 1) * RW].set(p['rwkv_w_up'][l, d])
        o = 2 * DECAY_RANK + d * ICLR_RANK
        aup = aup.at[o:o + ICLR_RANK, d * RW:(d + 1) * RW].set(p['rwkv_a_up'][l, d])
    gup = jnp.zeros((LANE, RW), F32).at[0:GATE_RANK].set(p['rwkv_g_up'][l])
    row = lambda a: a.reshape(1, -1)
    prep = (wup, aup, gup, row(p['rwkv_w0'][l]), row(p['rwkv_a0'][l]), row(p['rwkv_k_k'][l]),
            row(p['rwkv_k_a'][l]), row(p['rwkv_r_k'][l]), _head_block_ones())
    kd = GLA_HEADS * GLA_K
    gla_aup = jnp.zeros((2, LANE, GLA_KP), F32)
    for d in range(2):
        gla_aup = gla_aup.at[d, d * GLA_RANK:(d + 1) * GLA_RANK, 0:kd].set(p['gla_a_up'][l, d])
    gla_ab = jnp.zeros((2, 1, GLA_KP), F32).at[:, 0, 0:kd].set(p['gla_a_bias'][l])
    lam_re, lam_im = p['s5_lam_re'][l], p['s5_lam_im'][l]
    dt = jnp.exp(p['s5_log_dt'][l])[:, :, None]
    zr, zi = lam_re[:, None, :] * dt, lam_im[:, None, :] * dt
    mag = jnp.exp(zr)
    ab_r, ab_i = mag * jnp.cos(zi), mag * jnp.sin(zi)
    den = (lam_re * lam_re + lam_im * lam_im)[:, None, :]
    f_r = ((ab_r - 1) * lam_re[:, None, :] + ab_i * lam_im[:, None, :]) / den
    f_i = (ab_i * lam_re[:, None, :] - (ab_r - 1) * lam_im[:, None, :]) / den
    b_re, b_im = p['s5_b_re'][l], p['s5_b_im'][l]
    bb_r = f_r[..., None] * b_re - f_i[..., None] * b_im
    bb_i = f_r[..., None] * b_im + f_i[..., None] * b_re
    eye_g = jnp.eye(S5_GROUPS, dtype=F32)
    half = S5_LANES // 2

    def in_blockdiag(bb):
        return jnp.einsum('dgpc,gh->dgchp', bb, eye_g).reshape(2, S5_W, half)

    def out_blockdiag(cc):
        return jnp.einsum('dgcp,gh->dgphc', cc, eye_g).reshape(2, half, S5_W)

    s5_bbd = jnp.concatenate([in_blockdiag(bb_r), in_blockdiag(bb_i)], axis=2).astype(BF16)
    s5_cbd = jnp.concatenate([out_blockdiag(p['s5_c_re'][l]), -out_blockdiag(p['s5_c_im'][l])], axis=1).astype(BF16)
    s5_a = jnp.concatenate([ab_r.reshape(2, 1, half), ab_i.reshape(2, 1, half)], axis=2)
    mix = (_head_block_ones(), row(p['rwkv_ln_g'][l]), row(p['rwkv_ln_b'][l]),
           row(jnp.tile(p['gla_norm_g'][l], GLA_HEADS)), row(p['s5_d'][l]),
           p['s5_glu_w'][l].astype(BF16), row(p['s5_glu_b'][l]), p['w_out'][l].astype(BF16),
           row(p['ln1_g'][l]), row(p['ln1_b'][l]))
    moe = dict(rwt=p['router_w'][l].T,
               bias=jnp.broadcast_to(p['router_bias'][l][:, None], (N_EXPERTS, LANE)),
               w13=p['exp_w13'], w2=p['exp_w2'], layer=l,
               comb=(p['sh_w13'][l].astype(BF16), p['sh_w2'][l].astype(BF16), row(p['ln2_g'][l]), row(p['ln2_b'][l])))
    return dict(w_in=w_in, taps=taps, prep=prep, gla_aup=gla_aup, gla_ab=gla_ab,
                s5_a=s5_a, s5_bbd=s5_bbd, s5_cbd=s5_cbd, mix=mix, moe=moe)


_ARG_NAMES = ('x', 'c', 'ctx', 'c_ctx', 'w_mod', 'b_mod', 'w_in', 'rwkv_conv', 'rwkv_w0', 'rwkv_w_up', 'rwkv_a0',
              'rwkv_a_up', 'rwkv_g_up', 'rwkv_k_k', 'rwkv_k_a', 'rwkv_r_k', 'rwkv_ln_g', 'rwkv_ln_b', 'gla_a_up',
              'gla_a_bias', 'gla_norm_g', 's5_lam_re', 's5_lam_im', 's5_log_dt', 's5_b_re', 's5_b_im', 's5_c_re',
              's5_c_im', 's5_d', 's5_glu_w', 's5_glu_b', 'w_out', 'ln1_g', 'ln1_b', 'router_w', 'router_bias',
              'exp_w13', 'exp_w2', 'sh_w13', 'sh_w2', 'ln2_g', 'ln2_b')


def _tile(n, pref):
    t = pref
    while n % t:
        t //= 2
    return t


def kernel(x, c, ctx, c_ctx, w_mod, b_mod, w_in, rwkv_conv, rwkv_w0, rwkv_w_up, rwkv_a0, rwkv_a_up, rwkv_g_up,
           rwkv_k_k, rwkv_k_a, rwkv_r_k, rwkv_ln_g, rwkv_ln_b, gla_a_up, gla_a_bias, gla_norm_g, s5_lam_re,
           s5_lam_im, s5_log_dt, s5_b_re, s5_b_im, s5_c_re, s5_c_im, s5_d, s5_glu_w, s5_glu_b, w_out, ln1_g,
           ln1_b, router_w, router_bias, exp_w13, exp_w2, sh_w13, sh_w2, ln2_g, ln2_b):
    p = dict(zip(_ARG_NAMES, (x, c, ctx, c_ctx, w_mod, b_mod, w_in, rwkv_conv, rwkv_w0, rwkv_w_up, rwkv_a0,
                              rwkv_a_up, rwkv_g_up, rwkv_k_k, rwkv_k_a, rwkv_r_k, rwkv_ln_g, rwkv_ln_b, gla_a_up,
                              gla_a_bias, gla_norm_g, s5_lam_re, s5_lam_im, s5_log_dt, s5_b_re, s5_b_im, s5_c_re,
                              s5_c_im, s5_d, s5_glu_w, s5_glu_b, w_out, ln1_g, ln1_b, router_w, router_bias,
                              exp_w13, exp_w2, sh_w13, sh_w2, ln2_g, ln2_b)))
    B, T, D = x.shape
    TC = ctx.shape[1]
    L = w_mod.shape[0]
    alpha = (2 * L) ** 0.25
    n_lat, n_ctx = B * T, B * TC
    R = ((B + 1 + SUB - 1) // SUB) * SUB
    cc = jnp.zeros((R, D), F32).at[0:B].set(c).at[B].set(c_ctx)
    mod = mod_table(cc, w_mod, b_mod)
    tm = _tile(T, 256)
    tmc = _tile(n_ctx, 256)
    tmm = min(_tile(T, 128), _tile(n_ctx, 128))
    lat_row = lambda i: (i * tm) // T
    ctx_row = lambda i: B
    x2 = x.reshape(n_lat, D)
    c2 = ctx.reshape(n_ctx, D)
    for l in range(L):
        last = l == L - 1
        lp = _layer_params(l, p)
        mod3 = mod[l].reshape(R, 6, D)
        outs_c, st_c = token_mixers(c2, B, TC, TC, mod3, ctx_row, lp, zero_states(B), tmc)
        outs, _ = token_mixers(x2, B, T, GRID_W, mod3, lat_row, lp, st_c, tm)
        x1, h2, h2t = mix_out(*outs, x2, mod3, lp['mix'], lat_row, tm, alpha)
        if last:
            x2 = moe_ffn(h2, h2t, x1, mod3, lp['moe'], lambda i: (i * tmm) // T, tmm, alpha)
        else:
            c1, hc2, hc2t = mix_out(*outs_c, c2, mod3, lp['mix'], ctx_row, tmc, alpha)
            row_all = lambda i: jnp.where(i * tmm < n_ctx, B, (i * tmm - n_ctx) // T)
            out = moe_ffn(jnp.concatenate([hc2, h2]), jnp.concatenate([hc2t, h2t]), jnp.concatenate([c1, x1]),
                          mod3, lp['moe'], row_all, tmm, alpha)
            c2, x2 = out[:n_ctx], out[n_ctx:]
    return x2.reshape(B, T, D)
```

```python
import functools
import math

import numpy as np
import jax
import jax.numpy as jnp
from jax import lax
from jax.experimental import pallas as pl
from jax.experimental.pallas import tpu as pltpu

F32 = jnp.float32
BF16 = jnp.bfloat16
I32 = jnp.int32

GRID_W = 64
RWKV_HEADS = 6
HEAD = 64
RW = RWKV_HEADS * HEAD
DECAY_RANK = 32
ICLR_RANK = 32
GATE_RANK = 64
RWKV_GN_EPS = 64e-5
GLA_HEADS = 6
GLA_K = 32
GLA_RANK = 16
GLA_TEMP = 16.0
S5_GROUPS = 16
S5_GROUP = 16
S5_STATE = 64
S5_W = S5_GROUPS * S5_GROUP
S5_LANES = 2 * S5_GROUPS * S5_STATE
N_EXPERTS = 256
TOP_K = 8
N_EGROUPS = 8
TOPK_GROUPS = 4
ROUTE_SCALE = 2.5
LN_EPS = 1e-6
CHUNK = 64
MOE_BLK = 256
LANE = 128
SUB = 8
VMEM_LIMIT = 56 * 1024 * 1024

PA_W = 3 * RW + 2 * LANE
PBG_W = 256 + 256 + RW + LANE
NP_W = PA_W + PBG_W + RW + S5_W


def _cp(sem):
    return pltpu.CompilerParams(dimension_semantics=sem, vmem_limit_bytes=VMEM_LIMIT)


def _sigmoid(x):
    return 1.0 / (1.0 + jnp.exp(-x))


def _silu(x):
    return x * _sigmoid(x)


def _bdot(a, b, dims=None):
    a = a.astype(BF16)
    b = b.astype(BF16)
    if dims is None:
        return jnp.dot(a, b, preferred_element_type=F32)
    return lax.dot_general(a, b, dims, preferred_element_type=F32)


def _split2(x):
    hi = x.astype(BF16)
    lo = (x - hi.astype(F32)).astype(BF16)
    return hi, lo


def _split3(x):
    hi = x.astype(BF16)
    r = x - hi.astype(F32)
    mid = r.astype(BF16)
    lo = (r - mid.astype(F32)).astype(BF16)
    return hi, mid, lo


NT_DIMS = (((1,), (1,)), ((), ()))
TN_DIMS = (((0,), (0,)), ((), ()))


def _dot33(a, b, dims=None):
    ah, al = _split2(a)
    bh, bl = _split2(b)
    return _bdot(ah, bh, dims) + (_bdot(ah, bl, dims) + _bdot(al, bh, dims))


def _dot_exact_lhs(m_exact, x, dims=None):
    h, m, l = _split3(x)
    return _bdot(m_exact, h, dims) + (_bdot(m_exact, m, dims) + _bdot(m_exact, l, dims))


def _dot_exact_rhs(x, m_exact, dims=None):
    h, m, l = _split3(x)
    return _bdot(h, m_exact, dims) + (_bdot(m, m_exact, dims) + _bdot(l, m_exact, dims))


def _layer_norm(x):
    mu = jnp.mean(x, axis=-1, keepdims=True)
    xc = x - mu
    var = jnp.mean(xc * xc, axis=-1, keepdims=True)
    return xc * lax.rsqrt(var + LN_EPS)


def _mod_kernel(c_ref, w_ref, b_ref, o_ref):
    s = _silu(c_ref[...])
    o_ref[0] = _dot33(s, w_ref[0]) + b_ref[0]


def mod_table(cc, w_mod, b_mod):
    L, D, D6 = w_mod.shape
    R = cc.shape[0]
    tn = 1536
    return pl.pallas_call(
        _mod_kernel,
        out_shape=jax.ShapeDtypeStruct((L, R, D6), F32),
        grid=(L, D6 // tn),
        in_specs=[pl.BlockSpec((R, D), lambda l, j: (0, 0)),
                  pl.BlockSpec((1, D, tn), lambda l, j: (l, 0, j)),
                  pl.BlockSpec((1, 1, tn), lambda l, j: (l, 0, j))],
        out_specs=pl.BlockSpec((1, R, tn), lambda l, j: (l, 0, j)),
        compiler_params=_cp(("arbitrary", "arbitrary")),
        name="mod_table",
    )(cc, w_mod, b_mod.reshape(L, 1, D6))


def _inproj_kernel(x_ref, mod_ref, w_ref, pa_ref, pbg_ref, gg_ref, u_ref):
    x = x_ref[...]
    h = _layer_norm(x) * (1.0 + mod_ref[0, 1:2, :]) + mod_ref[0, 0:1, :]
    hb = h.astype(BF16)
    o = 0
    for ref in (pa_ref, pbg_ref, gg_ref, u_ref):
        w = ref.shape[-1]
        ref[...] = jnp.dot(hb, w_ref[:, o:o + w], preferred_element_type=F32)
        o += w


def inproj(x2, mod3, w_bf, row_of_tile, tm):
    N, D = x2.shape
    return pl.pallas_call(
        _inproj_kernel,
        out_shape=[jax.ShapeDtypeStruct((N, PA_W), F32),
                   jax.ShapeDtypeStruct((N, PBG_W), F32),
                   jax.ShapeDtypeStruct((N, RW), F32),
                   jax.ShapeDtypeStruct((N, S5_W), F32)],
        grid=(N // tm,),
        in_specs=[pl.BlockSpec((tm, D), lambda i: (i, 0)),
                  pl.BlockSpec((1, 6, D), lambda i: (row_of_tile(i), 0, 0)),
                  pl.BlockSpec((D, NP_W), lambda i: (0, 0))],
        out_specs=[pl.BlockSpec((tm, PA_W), lambda i: (i, 0)),
                   pl.BlockSpec((tm, PBG_W), lambda i: (i, 0)),
                   pl.BlockSpec((tm, RW), lambda i: (i, 0)),
                   pl.BlockSpec((tm, S5_W), lambda i: (i, 0))],
        compiler_params=_cp(("arbitrary",)),
        name="inproj",
    )(x2, mod3, w_bf)


CONV_PAD = 72


def _conv_kernel(x_ref, taps_ref, o_ref, buf_ref, *, T, W, vertical):
    pad = CONV_PAD
    zeros = jnp.zeros((pad, LANE), F32)
    buf_ref[0:pad, :] = zeros
    buf_ref[pad + T:pad + T + pad, :] = zeros
    buf_ref[pad:pad + T, :] = x_ref[0]
    ch = min(T, 256)
    col = lax.broadcasted_iota(I32, (ch, LANE), 0) & (W - 1)
    left_ok = col >= 1
    right_ok = col <= W - 2
    for c in range(T // ch):
        base = pad + c * ch
        acc = jnp.zeros((ch, LANE), F32)
        for dr in ((0, 1, 2) if vertical else (1,)):
            for dc in range(3):
                off = (dr - 1) * W + (dc - 1)
                v = buf_ref[base + off:base + off + ch, :]
                if dc == 0:
                    v = jnp.where(left_ok, v, 0.0)
                elif dc == 2:
                    v = jnp.where(right_ok, v, 0.0)
                acc = acc + v * taps_ref[3 * dr + dc:3 * dr + dc + 1, :]
        o_ref[0, c * ch:(c + 1) * ch, :] = acc


def short_conv(pa3, taps9, W):
    B, T, C = pa3.shape
    vertical = T > W
    assert W & (W - 1) == 0 and (not vertical or W + 1 <= CONV_PAD)
    kern = functools.partial(_conv_kernel, T=T, W=W, vertical=vertical)
    return pl.pallas_call(
        kern,
        out_shape=jax.ShapeDtypeStruct((B, T, C), F32),
        grid=(B, C // LANE),
        in_specs=[pl.BlockSpec((1, T, LANE), lambda b, j: (b, 0, j)),
                  pl.BlockSpec((9, LANE), lambda b, j: (0, j))],
        out_specs=pl.BlockSpec((1, T, LANE), lambda b, j: (b, 0, j)),
        scratch_shapes=[pltpu.VMEM((T + 2 * CONV_PAD, LANE), F32)],
        compiler_params=_cp(("arbitrary", "arbitrary")),
        name="short_conv",
    )(pa3, taps9)


def _rwkv_prep_kernel(pa_ref, wup_ref, aup_ref, gup_ref, w0_ref, a0_ref, kk_ref, ka_ref, rk_ref, hb_ref,
                      ss_ref, sd_ref, gb_ref):
    r = pa_ref[:, 0:RW]
    k = pa_ref[:, RW:2 * RW]
    v = pa_ref[:, 2 * RW:3 * RW]
    wa = pa_ref[:, 3 * RW:3 * RW + LANE]
    gd = pa_ref[:, 3 * RW + LANE:3 * RW + 2 * LANE]
    z = w0_ref[...] + _dot33(jnp.tanh(wa), wup_ref[...])
    lw = -_sigmoid(z) * math.exp(-0.5)
    a = _sigmoid(a0_ref[...] + _dot33(wa, aup_ref[...]))
    g = _dot33(_sigmoid(gd), gup_ref[...])
    hb = hb_ref[...]
    kk = k * kk_ref[...]
    kk = kk * lax.rsqrt(_dot_exact_rhs(kk * kk, hb) + 1e-12)
    ka = ka_ref[...]
    ss_ref[:, 0:RW] = r
    ss_ref[:, RW:2 * RW] = v
    ss_ref[:, 2 * RW:3 * RW] = kk
    rk2 = jnp.zeros_like(r)
    for d in range(2):
        ad = a[:, d * RW:(d + 1) * RW]
        k2 = k * (1.0 + (ad - 1.0) * ka)
        sd_ref[d, :, 0:RW] = lw[:, d * RW:(d + 1) * RW]
        sd_ref[d, :, RW:2 * RW] = k2
        sd_ref[d, :, 2 * RW:3 * RW] = kk * ad
        rk2 = rk2 + r * k2
    bonus = _dot_exact_rhs(rk2 * rk_ref[...], hb) * v
    gb_ref[:, 0:RW] = g
    gb_ref[:, RW:2 * RW] = bonus


def rwkv_prep(pa2, wts, tm):
    N = pa2.shape[0]
    full = lambda a: pl.BlockSpec(a.shape, lambda i: (0,) * a.ndim)
    return pl.pallas_call(
        _rwkv_prep_kernel,
        out_shape=[jax.ShapeDtypeStruct((N, 3 * RW), F32),
                   jax.ShapeDtypeStruct((2, N, 3 * RW), F32),
                   jax.ShapeDtypeStruct((N, 2 * RW), F32)],
        grid=(N // tm,),
        in_specs=[pl.BlockSpec((tm, PA_W), lambda i: (i, 0))] + [full(a) for a in wts],
        out_specs=[pl.BlockSpec((tm, 3 * RW), lambda i: (i, 0)),
                   pl.BlockSpec((2, tm, 3 * RW), lambda i: (0, i, 0)),
                   pl.BlockSpec((tm, 2 * RW), lambda i: (i, 0))],
        compiler_params=_cp(("arbitrary",)),
        name="rwkv_prep",
    )(pa2, *wts)


RWKV_GROUP = 8


def _rwkv_scan_kernel(ss_ref, sd_ref, s0_ref, y_ref, sfin_ref, s_scr, *, nsteps, group, reverse):
    n = pl.program_id(1)
    C = CHUNK
    P = 2 * C
    npair = RW // LANE

    @pl.when(n == 0)
    def _():
        s_scr[...] = s0_ref[0]

    row = lax.broadcasted_iota(I32, (P, P), 0)
    col = lax.broadcasted_iota(I32, (P, P), 1)
    same = (row >> 6) == (col >> 6)
    dlt = (col & (C - 1)) - (row & (C - 1)) if reverse else (row & (C - 1)) - (col & (C - 1))
    strict = same & (dlt > 0)
    incl = same & (dlt >= 0)
    eye = (row == col).astype(F32)
    lvl_masks = [((row >> (lvl + 1)) == (col >> (lvl + 1))) & ((row >> lvl) != (col >> lvl)) for lvl in range(6)]
    rc = lax.broadcasted_iota(I32, (C, C), 0)
    cc = lax.broadcasted_iota(I32, (C, C), 1)
    tri = jnp.where((cc >= rc) if reverse else (rc >= cc), 1.0, 0.0).astype(BF16)
    head0 = lax.broadcasted_iota(I32, (C, LANE), 1) < HEAD

    def stack(x):
        return jnp.concatenate([jnp.where(head0, x, 0.0), jnp.where(head0, 0.0, x)], axis=0)

    streams = [(g, p) for g in range(group) for p in range(npair)]
    tm_ = {}
    for (g, p) in streams:
        t0, t1 = g * C, (g + 1) * C
        lo, hi = p * LANE, (p + 1) * LANE
        lw = sd_ref[0, t0:t1, lo:hi]
        tm_[(g, p)] = dict(lw=lw, cl=_dot_exact_lhs(tri, lw))
    for (g, p) in streams:
        t = tm_[(g, p)]
        t0, t1 = g * C, (g + 1) * C
        lo, hi = p * LANE, (p + 1) * LANE
        r = ss_ref[t0:t1, lo:hi]
        v = ss_ref[t0:t1, RW + lo:RW + hi]
        kk = ss_ref[t0:t1, 2 * RW + lo:2 * RW + hi]
        k2 = sd_ref[0, t0:t1, RW + lo:RW + hi]
        b = sd_ref[0, t0:t1, 2 * RW + lo:2 * RW + hi]
        cl, lw = t['cl'], t['lw']
        t['ptot'] = jnp.exp(jnp.sum(lw, axis=0, keepdims=True))
        pinv = jnp.exp(-cl)
        left = jnp.concatenate([stack(-kk * jnp.exp(cl - lw)), stack(r * jnp.exp(cl))], axis=0)
        right = jnp.concatenate([stack(b * pinv), stack(k2 * pinv)], axis=0)
        t['v_st'] = stack(v)
        t['left'] = left.astype(BF16)
        t['bk'] = (right * t['ptot']).astype(BF16)
        aa = _bdot(left, right, NT_DIMS)
        t['nmat'] = jnp.where(strict, aa[0:P, 0:P], 0.0)
        t['a_ak'] = jnp.where(strict, aa[0:P, P:2 * P], 0.0)
        t['a_rbk'] = jnp.concatenate([jnp.where(incl, aa[P:2 * P, 0:P], 0.0),
                                      jnp.where(incl, aa[P:2 * P, P:2 * P], 0.0)], axis=1).astype(BF16)
        t['tinv'] = eye
    for sk in streams:
        t = tm_[sk]
        t['akv'] = _bdot(t['a_ak'], t['v_st'])
    for m in lvl_masks:
        for sk in streams:
            t = tm_[sk]
            t['et'] = _bdot(jnp.where(m, t['nmat'], 0.0), t['tinv'])
        for sk in streams:
            t = tm_[sk]
            t['tinv'] = t['tinv'] + _bdot(t['tinv'], t['et'])
    state = [s_scr[p] for p in range(npair)]
    pairs = range(npair)
    for g in (range(group - 1, -1, -1) if reverse else range(group)):
        ts = [tm_[(g, p)] for p in pairs]
        a_s = [_bdot(ts[p]['left'], state[p], NT_DIMS) for p in pairs]
        u = [_bdot(ts[p]['tinv'], a_s[p][0:P] + ts[p]['akv']) for p in pairs]
        uv = [jnp.concatenate([u[p], ts[p]['v_st']], axis=0) for p in pairs]
        state = [state[p] * ts[p]['ptot'] + _bdot(uv[p], ts[p]['bk'], TN_DIMS) for p in pairs]
        for p in pairs:
            y_st = a_s[p][P:2 * P] + _bdot(ts[p]['a_rbk'], uv[p])
            y_ref[g * C:(g + 1) * C, p * LANE:(p + 1) * LANE] = y_st[0:C] + y_st[C:P]
    for p in pairs:
        s_scr[p] = state[p]

    @pl.when(n == nsteps - 1)
    def _():
        sfin_ref[0] = s_scr[...]


def rwkv_scan_dir(ss, sd, s0, B, T, d):
    nc = T // CHUNK
    group = min(RWKV_GROUP, nc)
    nsteps = nc // group
    npair = RW // LANE
    rows = group * CHUNK
    blk = (lambda b, n: b * nsteps + nsteps - 1 - n) if d else (lambda b, n: b * nsteps + n)
    kern = functools.partial(_rwkv_scan_kernel, nsteps=nsteps, group=group, reverse=bool(d))
    return pl.pallas_call(
        kern,
        out_shape=[jax.ShapeDtypeStruct((B * T, RW), F32),
                   jax.ShapeDtypeStruct((B, npair, LANE, LANE), F32)],
        grid=(B, nsteps),
        in_specs=[pl.BlockSpec((rows, 3 * RW), lambda b, n: (blk(b, n), 0)),
                  pl.BlockSpec((1, rows, 3 * RW), lambda b, n: (d, blk(b, n), 0)),
                  pl.BlockSpec((1, npair, LANE, LANE), lambda b, n: (b, 0, 0, 0))],
        out_specs=[pl.BlockSpec((rows, RW), lambda b, n: (blk(b, n), 0)),
                   pl.BlockSpec((1, npair, LANE, LANE), lambda b, n: (b, 0, 0, 0))],
        scratch_shapes=[pltpu.VMEM((npair, LANE, LANE), F32)],
        compiler_params=_cp(("arbitrary", "arbitrary")),
        name="rwkv_scan_bwd" if d else "rwkv_scan_fwd",
    )(ss, sd, s0)


def rwkv_scan(ss, sd, s0, B, T):
    y0, f0 = rwkv_scan_dir(ss, sd, s0[0], B, T, 0)
    y1, f1 = rwkv_scan_dir(ss, sd, s0[1], B, T, 1)
    return (y0, y1), (f0, f1)


GLA_KP = 256


GLA_GROUP = 8


def _gla_kernel(pbg_ref, aup_ref, ab_ref, s0_ref, o_ref, sfin_ref, s_scr, *, nsteps, group, reverse):
    n = pl.program_id(1)
    C = CHUNK
    R = group * C

    @pl.when(n == 0)
    def _():
        s_scr[...] = s0_ref[0]

    q = pbg_ref[:, 0:GLA_KP]
    k = pbg_ref[:, GLA_KP:2 * GLA_KP]
    v = pbg_ref[:, 2 * GLA_KP:2 * GLA_KP + RW]
    ad = pbg_ref[:, 2 * GLA_KP + RW:2 * GLA_KP + RW + LANE]
    x = _dot33(ad, aup_ref[0]) + ab_ref[0]
    la = (jnp.minimum(x, 0.0) - jnp.log(1.0 + jnp.exp(-jnp.abs(x)))) * (1.0 / GLA_TEMP)
    rr = lax.broadcasted_iota(I32, (R, R), 0)
    rc = lax.broadcasted_iota(I32, (R, R), 1)
    same = (rr >> 6) == (rc >> 6)
    order = (rc >= rr) if reverse else (rr >= rc)
    bcum = _dot_exact_lhs(jnp.where(same & order, 1.0, 0.0).astype(BF16), la)
    tot = _dot_exact_lhs(jnp.where(same, 1.0, 0.0).astype(BF16), la)
    q_in = q * jnp.exp(bcum) * (GLA_K ** -0.5)
    k_in = k * jnp.exp(-bcum)
    k_st = k * jnp.exp(tot - bcum)
    dn = jnp.exp(tot)
    klane = lax.broadcasted_iota(I32, (C, GLA_KP), 1)
    rt = lax.broadcasted_iota(I32, (GLA_HEADS * C, C), 0) & (C - 1)
    ct = lax.broadcasted_iota(I32, (GLA_HEADS * C, C), 1)
    causal = (ct >= rt) if reverse else (rt >= ct)
    vlane = lax.broadcasted_iota(I32, (C, RW), 1)
    sv = lax.broadcasted_iota(I32, (RW, GLA_KP), 0) >> 6
    sk = lax.broadcasted_iota(I32, (RW, GLA_KP), 1) >> 5
    chunks = range(group)
    sl = [slice(g * C, (g + 1) * C) for g in chunks]
    q_rows = [jnp.concatenate([jnp.where((klane >> 5) == h, q_in[sl[g]], 0.0) for h in range(GLA_HEADS)],
                              axis=0).astype(BF16) for g in chunks]
    att = [jnp.where(causal, _bdot(q_rows[g], k_in[sl[g]], NT_DIMS), 0.0) for g in chunks]
    o_rows = [_bdot(att[g], v[sl[g]]) for g in chunks]
    kv = [jnp.where(sv == sk, _bdot(v[sl[g]], k_st[sl[g]], TN_DIMS), 0.0) for g in chunks]
    s = s_scr[...]
    for g in (reversed(chunks) if reverse else chunks):
        og = o_rows[g] + _bdot(q_rows[g], s, NT_DIMS)
        o = jnp.zeros((C, RW), F32)
        for h in range(GLA_HEADS):
            o = o + jnp.where((vlane >> 6) == h, og[h * C:(h + 1) * C], 0.0)
        o_ref[sl[g], :] = o
        s = s * dn[g * C:g * C + 1, :] + kv[g]
    s_scr[...] = s

    @pl.when(n == nsteps - 1)
    def _():
        sfin_ref[0] = s_scr[...]


def gla_scan_dir(pbg, aup, ab, s0, B, T, d):
    nc = T // CHUNK
    group = min(GLA_GROUP, nc)
    nsteps = nc // group
    rows = group * CHUNK
    blk = (lambda b, n: b * nsteps + nsteps - 1 - n) if d else (lambda b, n: b * nsteps + n)
    kern = functools.partial(_gla_kernel, nsteps=nsteps, group=group, reverse=bool(d))
    return pl.pallas_call(
        kern,
        out_shape=[jax.ShapeDtypeStruct((B * T, RW), F32),
                   jax.ShapeDtypeStruct((B, RW, GLA_KP), F32)],
        grid=(B, nsteps),
        in_specs=[pl.BlockSpec((rows, PBG_W), lambda b, n: (blk(b, n), 0)),
                  pl.BlockSpec((1, LANE, GLA_KP), lambda b, n: (d, 0, 0)),
                  pl.BlockSpec((1, 1, GLA_KP), lambda b, n: (d, 0, 0)),
                  pl.BlockSpec((1, RW, GLA_KP), lambda b, n: (b, 0, 0))],
        out_specs=[pl.BlockSpec((rows, RW), lambda b, n: (blk(b, n), 0)),
                   pl.BlockSpec((1, RW, GLA_KP), lambda b, n: (b, 0, 0))],
        scratch_shapes=[pltpu.VMEM((RW, GLA_KP), F32)],
        compiler_params=_cp(("arbitrary", "arbitrary")),
        name="gla_scan_bwd" if d else "gla_scan_fwd",
    )(pbg, aup, ab, s0)


def gla_scan(pbg, aup, ab, s0, B, T):
    o0, f0 = gla_scan_dir(pbg, aup, ab, s0[0], B, T, 0)
    o1, f1 = gla_scan_dir(pbg, aup, ab, s0[1], B, T, 1)
    return (o0, o1), (f0, f1)


S5_PITCH = 72


def _s5_kernel(u_ref, a_ref, bbd_ref, cbd_ref, s0_ref, y_ref, sfin_ref, x_scr, st_scr, *, nc, nb):
    d = pl.program_id(0)
    n = pl.program_id(1)
    C = CHUNK
    half = S5_LANES // 2

    @pl.when(n == 0)
    def _():
        st_scr[...] = s0_ref[0]

    nslab = S5_LANES // LANE
    bbd = bbd_ref[0]
    for b in range(nb):
        bu = _bdot(u_ref[b], bbd)
        for j in range(nslab):
            x_scr[j, b * S5_PITCH:b * S5_PITCH + C, :] = bu[:, j * LANE:(j + 1) * LANE]
    a_re = a_ref[0, :, 0:half]
    a_im = a_ref[0, :, half:S5_LANES]

    def step(i, carry):
        re, im = carry
        t = i + d * (C - 1 - 2 * i)
        rows = pl.ds(t, nb, stride=S5_PITCH)
        bu = jnp.concatenate([x_scr[j, rows, :] for j in range(nslab)], axis=1)
        nre = a_re * re - a_im * im + bu[:, 0:half]
        nim = a_re * im + a_im * re + bu[:, half:S5_LANES]
        for j in range(nslab // 2):
            x_scr[j, rows, :] = nre[:, j * LANE:(j + 1) * LANE]
            x_scr[nslab // 2 + j, rows, :] = nim[:, j * LANE:(j + 1) * LANE]
        return nre, nim

    st = st_scr[...]
    re, im = lax.fori_loop(0, C, step, (st[:, 0:half], st[:, half:S5_LANES]))
    st_scr[...] = jnp.concatenate([re, im], axis=1)
    cbd = cbd_ref[0]
    for b in range(nb):
        xs = jnp.concatenate([x_scr[j, b * S5_PITCH:b * S5_PITCH + C, :] for j in range(nslab)], axis=1)
        y_ref[0, b] = _bdot(xs, cbd)

    @pl.when(n == nc - 1)
    def _():
        sfin_ref[0] = st_scr[...]


def s5_scan(u3, a_bar, bbd, cbd, s0):
    B, T, _ = u3.shape
    nc = T // CHUNK
    chunk = lambda d, n: n + d * (nc - 1 - 2 * n)
    kern = functools.partial(_s5_kernel, nc=nc, nb=B)
    return pl.pallas_call(
        kern,
        out_shape=[jax.ShapeDtypeStruct((2, B, T, S5_W), F32),
                   jax.ShapeDtypeStruct((2, B, S5_LANES), F32)],
        grid=(2, nc),
        in_specs=[pl.BlockSpec((B, CHUNK, S5_W), lambda d, n: (0, chunk(d, n), 0)),
                  pl.BlockSpec((1, 1, S5_LANES), lambda d, n: (d, 0, 0)),
                  pl.BlockSpec((1, S5_W, S5_LANES), lambda d, n: (d, 0, 0)),
                  pl.BlockSpec((1, S5_LANES, S5_W), lambda d, n: (d, 0, 0)),
                  pl.BlockSpec((1, B, S5_LANES), lambda d, n: (d, 0, 0))],
        out_specs=[pl.BlockSpec((1, B, CHUNK, S5_W), lambda d, n: (d, 0, chunk(d, n), 0)),
                   pl.BlockSpec((1, B, S5_LANES), lambda d, n: (d, 0, 0))],
        scratch_shapes=[pltpu.VMEM((S5_LANES // LANE, B * S5_PITCH, LANE), F32),
                        pltpu.VMEM((B, S5_LANES), F32)],
        compiler_params=_cp(("arbitrary", "arbitrary")),
        name="s5_scan",
    )(u3, a_bar, bbd, cbd, s0)


def _tile_rows_store(ref, val):
    tm = val.shape[0]
    for s in range(val.shape[1] // LANE):
        ref[pl.ds(s, tm, stride=SUB), :] = val[:, s * LANE:(s + 1) * LANE]


def _tile_rows_load(ref, tm, base=0):
    return jnp.concatenate([ref[pl.ds(base + s, tm, stride=SUB), :] for s in range(SUB)], axis=1)


def _mix_out_kernel(y0_ref, y1_ref, gb_ref, o0_ref, o1_ref, gg_ref, yd_ref, u_ref, x_ref, mod_ref,
                    hb_ref, lng_ref, lnb_ref, gng_ref, s5d_ref, gluw_ref, glub_ref, wout_ref, l1g_ref, l1b_ref,
                    x1_ref, h2_ref, h2t_ref, *, alpha):
    hb = hb_ref[...]
    inv = 1.0 / HEAD
    y = y0_ref[...] + y1_ref[...]
    yc = y - _dot_exact_rhs(y, hb) * inv
    var = _dot_exact_rhs(yc * yc, hb) * inv
    gn = yc * lax.rsqrt(var + RWKV_GN_EPS) * lng_ref[...] + lnb_ref[...]
    y_a = (gn + gb_ref[:, RW:2 * RW]) * gb_ref[:, 0:RW]
    o = o0_ref[...] + o1_ref[...]
    o = o * lax.rsqrt(_dot_exact_rhs(o * o, hb) * inv + 1e-6) * gng_ref[...]
    y_b = o * _silu(gg_ref[...])
    c = s5d_ref[...] * u_ref[...] + yd_ref[0] + yd_ref[1]
    c = 0.5 * c * (1.0 + jnp.tanh(math.sqrt(2.0 / math.pi) * (c + 0.044715 * (c * c * c))))
    y_c = c * _sigmoid(_bdot(c, gluw_ref[...]) + glub_ref[...])
    y_mix = (_bdot(y_a, wout_ref[0:RW, :]) + _bdot(y_b, wout_ref[RW:2 * RW, :])
             + _bdot(y_c, wout_ref[2 * RW:2 * RW + S5_W, :]))
    x1 = _layer_norm(alpha * x_ref[...] + mod_ref[0, 2:3, :] * y_mix) * l1g_ref[...] + l1b_ref[...]
    x1_ref[...] = x1
    h2 = _layer_norm(x1) * (1.0 + mod_ref[0, 4:5, :]) + mod_ref[0, 3:4, :]
    h2_ref[...] = h2
    _tile_rows_store(h2t_ref, h2)


def mix_out(y01, gb, o01, gg, yd, u, x2, mod3, wts, row_of_tile, tm, alpha):
    N, D = x2.shape
    full = lambda a: pl.BlockSpec(a.shape, lambda i: (0,) * a.ndim)
    kern = functools.partial(_mix_out_kernel, alpha=alpha)
    return pl.pallas_call(
        kern,
        out_shape=[jax.ShapeDtypeStruct((N, D), F32),
                   jax.ShapeDtypeStruct((N, D), F32),
                   jax.ShapeDtypeStruct((N * SUB, LANE), F32)],
        grid=(N // tm,),
        in_specs=[pl.BlockSpec((tm, RW), lambda i: (i, 0)),
                  pl.BlockSpec((tm, RW), lambda i: (i, 0)),
                  pl.BlockSpec((tm, 2 * RW), lambda i: (i, 0)),
                  pl.BlockSpec((tm, RW), lambda i: (i, 0)),
                  pl.BlockSpec((tm, RW), lambda i: (i, 0)),
                  pl.BlockSpec((tm, RW), lambda i: (i, 0)),
                  pl.BlockSpec((2, tm, S5_W), lambda i: (0, i, 0)),
                  pl.BlockSpec((tm, S5_W), lambda i: (i, 0)),
                  pl.BlockSpec((tm, D), lambda i: (i, 0)),
                  pl.BlockSpec((1, 6, D), lambda i: (row_of_tile(i), 0, 0))] + [full(a) for a in wts],
        out_specs=[pl.BlockSpec((tm, D), lambda i: (i, 0)),
                   pl.BlockSpec((tm, D), lambda i: (i, 0)),
                   pl.BlockSpec((tm * SUB, LANE), lambda i: (i, 0))],
        compiler_params=_cp(("arbitrary",)),
        name="mix_out",
    )(y01[0], y01[1], gb, o01[0], o01[1], gg, yd, u, x2, mod3, *wts)


def _first_max(x, idx, big):
    m = jnp.max(x, axis=0, keepdims=True)
    first = jnp.min(jnp.where(x == m, idx, big), axis=0, keepdims=True)
    return m, first


def _router_kernel(h_ref, rwt_ref, bias_ref, e_ref, w_ref):
    tm = h_ref.shape[0]
    gsz = N_EXPERTS // N_EGROUPS
    ninf = -jnp.inf
    s = _sigmoid(_dot33(rwt_ref[...], h_ref[...], NT_DIMS))
    ssel = s + bias_ref[:, 0:1]
    gi = lax.broadcasted_iota(I32, (gsz, tm), 0)
    gscore = []
    for g in range(N_EGROUPS):
        xg = ssel[g * gsz:(g + 1) * gsz, :]
        m1, i1 = _first_max(xg, gi, gsz)
        m2 = jnp.max(jnp.where(gi == i1, ninf, xg), axis=0, keepdims=True)
        gscore.append(m1 + m2)
    cur = jnp.concatenate(gscore, axis=0)
    gidx = lax.broadcasted_iota(I32, (N_EGROUPS, tm), 0)
    picked = jnp.zeros((N_EGROUPS, tm), F32)
    for _ in range(TOPK_GROUPS):
        _, first = _first_max(cur, gidx, N_EGROUPS)
        hit = gidx == first
        picked = jnp.where(hit, 1.0, picked)
        cur = jnp.where(hit, ninf, cur)
    x = jnp.concatenate(
        [jnp.where(picked[g:g + 1, :] > 0.5, ssel[g * gsz:(g + 1) * gsz, :], ninf) for g in range(N_EGROUPS)], axis=0)
    ei = lax.broadcasted_iota(I32, (N_EXPERTS, tm), 0)
    idxs, ws = [], []
    for _ in range(TOP_K):
        _, first = _first_max(x, ei, N_EXPERTS)
        hit = ei == first
        idxs.append(first)
        ws.append(jnp.sum(jnp.where(hit, s, 0.0), axis=0, keepdims=True))
        x = jnp.where(hit, ninf, x)
    w = jnp.concatenate(ws, axis=0)
    e_ref[...] = jnp.concatenate(idxs, axis=0)
    w_ref[...] = w / jnp.sum(w, axis=0, keepdims=True) * ROUTE_SCALE


def moe_router(h2, rwt, bias_b, tm):
    N, D = h2.shape
    return pl.pallas_call(
        _router_kernel,
        out_shape=[jax.ShapeDtypeStruct((TOP_K, N), I32), jax.ShapeDtypeStruct((TOP_K, N), F32)],
        grid=(N // tm,),
        in_specs=[pl.BlockSpec((tm, D), lambda i: (i, 0)),
                  pl.BlockSpec((N_EXPERTS, D), lambda i: (0, 0)),
                  pl.BlockSpec((N_EXPERTS, LANE), lambda i: (0, 0))],
        out_specs=[pl.BlockSpec((TOP_K, tm), lambda i: (0, i)),
                   pl.BlockSpec((TOP_K, tm), lambda i: (0, i))],
        compiler_params=_cp(("arbitrary",)),
        name="moe_router",
    )(h2, rwt, bias_b)


def _moe_count_kernel(e_ref, pstart_ref, plan_ref, blk_ref, cnt_scr, *, nt, nbp):
    i = pl.program_id(0)
    tp = e_ref.shape[1]

    @pl.when(i == 0)
    def _():
        cnt_scr[...] = jnp.zeros_like(cnt_scr)

    ei = lax.broadcasted_iota(I32, (N_EXPERTS, tp), 0)
    acc = jnp.zeros((N_EXPERTS, 1), F32)
    for k in range(TOP_K):
        acc = acc + jnp.sum(jnp.where(ei == e_ref[k:k + 1, :], 1.0, 0.0), axis=1, keepdims=True)
    cnt_scr[...] = cnt_scr[...] + acc

    @pl.when(i == nt - 1)
    def _():
        cnt = cnt_scr[...].astype(I32)
        padded = (cnt + (MOE_BLK - 1)) & (-MOE_BLK)
        r = lax.broadcasted_iota(I32, (N_EXPERTS, N_EXPERTS), 0)
        c = lax.broadcasted_iota(I32, (N_EXPERTS, N_EXPERTS), 1)
        tri = jnp.where(c <= r, 1.0, 0.0).astype(BF16)
        padded_b = jnp.broadcast_to(padded.astype(F32), (N_EXPERTS, LANE))
        p_end = _dot_exact_lhs(tri, padded_b)
        pstart = p_end - padded_b
        pstart_ref[...] = pstart.astype(I32)
        diag = r == c
        ps_row = jnp.sum(jnp.where(diag, pstart[:, 0:1], 0.0), axis=0, keepdims=True)
        cnt_row = jnp.sum(jnp.where(diag, cnt_scr[...], 0.0), axis=0, keepdims=True)
        plan_ref[...] = jnp.concatenate([ps_row, cnt_row, jnp.zeros((SUB - 2, N_EXPERTS), F32)], axis=0).astype(I32)
        lim =(lax.broadcasted_iota(I32, (N_EXPERTS, nbp), 1) * MOE_BLK).astype(F32)
        be = jnp.sum(jnp.where(p_end[:, 0:1] <= lim, 1.0, 0.0), axis=0, keepdims=True)
        be = jnp.minimum(be, N_EXPERTS - 1.0)
        nused = jnp.max(p_end[:, 0:1], axis=0, keepdims=True) * (1.0 / MOE_BLK)
        blk_ref[...] = jnp.concatenate([jnp.broadcast_to(be, (SUB // 2, nbp)),
                                        jnp.broadcast_to(nused, (SUB // 2, nbp))], axis=0).astype(I32)


def moe_counts(eidx, tp, nbp):
    N = eidx.shape[1]
    nt = N // tp
    kern = functools.partial(_moe_count_kernel, nt=nt, nbp=nbp)
    return pl.pallas_call(
        kern,
        out_shape=[jax.ShapeDtypeStruct((N_EXPERTS, LANE), I32),
                   jax.ShapeDtypeStruct((SUB, N_EXPERTS), I32),
                   jax.ShapeDtypeStruct((SUB, nbp), I32)],
        grid=(nt,),
        in_specs=[pl.BlockSpec((TOP_K, tp), lambda i: (0, i))],
        out_specs=[pl.BlockSpec((N_EXPERTS, LANE), lambda i: (0, 0)),
                   pl.BlockSpec((SUB, N_EXPERTS), lambda i: (0, 0)),
                   pl.BlockSpec((SUB, nbp), lambda i: (0, 0))],
        scratch_shapes=[pltpu.VMEM((N_EXPERTS, 1), F32)],
        compiler_params=_cp(("arbitrary",)),
        name="moe_counts",
    )(eidx)


def _moe_dest_kernel(e_ref, pstart_ref, dest_ref, base_scr):
    i = pl.program_id(0)
    tp = e_ref.shape[1]

    @pl.when(i == 0)
    def _():
        base_scr[...] = pstart_ref[:, 0:1].astype(F32)

    ei = lax.broadcasted_iota(I32, (N_EXPERTS, tp), 0)
    r = lax.broadcasted_iota(I32, (tp, tp), 0)
    c = lax.broadcasted_iota(I32, (tp, tp), 1)
    tri = jnp.where(r <= c, 1.0, 0.0).astype(BF16)
    base = base_scr[...]
    rows = []
    for k in range(TOP_K):
        hit = ei == e_ref[k:k + 1, :]
        oh = jnp.where(hit, 1.0, 0.0)
        cum = _bdot(oh, tri)
        rows.append(jnp.sum(jnp.where(hit, cum - 1.0 + base, 0.0), axis=0, keepdims=True))
        base = base + cum[:, tp - 1:tp]
    base_scr[...] = base
    dest_ref[...] = jnp.concatenate(rows, axis=0).astype(I32)


def moe_dest(eidx, pstart, tp):
    N = eidx.shape[1]
    return pl.pallas_call(
        _moe_dest_kernel,
        out_shape=jax.ShapeDtypeStruct((TOP_K, N), I32),
        grid=(N // tp,),
        in_specs=[pl.BlockSpec((TOP_K, tp), lambda i: (0, i)),
                  pl.BlockSpec((N_EXPERTS, LANE), lambda i: (0, 0))],
        out_specs=pl.BlockSpec((TOP_K, tp), lambda i: (0, i)),
        scratch_shapes=[pltpu.VMEM((N_EXPERTS, 1), F32)],
        compiler_params=_cp(("arbitrary",)),
        name="moe_dest",
    )(eidx, pstart)


def _tile_at(ref, token):
    return ref.at[pl.ds(pl.multiple_of(token * SUB, SUB), SUB), :]


def _moe_scatter_kernel(dest_ref, h_ref, xs_hbm, sem):
    tp = dest_ref.shape[1]

    def issue(j, carry):
        for k in range(TOP_K):
            pltpu.make_async_copy(_tile_at(h_ref, j), _tile_at(xs_hbm, dest_ref[k, j]), sem).start()
        return carry

    lax.fori_loop(0, tp, issue, 0)

    def drain(j, carry):
        for k in range(TOP_K):
            pltpu.make_async_copy(h_ref.at[pl.ds(0, SUB), :], xs_hbm.at[pl.ds(0, SUB), :], sem).wait()
        return carry

    lax.fori_loop(0, tp, drain, 0)


def moe_scatter(dest, h2t, n_slots, tp):
    N = dest.shape[1]
    return pl.pallas_call(
        _moe_scatter_kernel,
        out_shape=jax.ShapeDtypeStruct((n_slots * SUB, LANE), F32),
        grid=(N // tp,),
        in_specs=[pl.BlockSpec((TOP_K, tp), lambda i: (0, i), memory_space=pltpu.SMEM),
                  pl.BlockSpec((tp * SUB, LANE), lambda i: (i, 0))],
        out_specs=pl.BlockSpec(memory_space=pl.ANY),
        scratch_shapes=[pltpu.SemaphoreType.DMA(())],
        compiler_params=_cp(("arbitrary",)),
        name="moe_scatter",
    )(dest, h2t)


def _moe_padfill_kernel(ps_ref, xs_in, xs_hbm, zero_scr, sem):
    del xs_in
    e = pl.program_id(0)
    zero_scr[...] = jnp.zeros_like(zero_scr)
    start = ps_ref[0, e]
    cnt = ps_ref[1, e]
    padded = (cnt + (MOE_BLK - 1)) & (-MOE_BLK)

    def issue(r, carry):
        pltpu.make_async_copy(zero_scr, _tile_at(xs_hbm, start + r), sem).start()
        return carry

    lax.fori_loop(cnt, padded, issue, 0)

    def drain(r, carry):
        pltpu.make_async_copy(zero_scr, xs_hbm.at[pl.ds(0, SUB), :], sem).wait()
        return carry

    lax.fori_loop(cnt, padded, drain, 0)


def moe_padfill(pstart, xs):
    return pl.pallas_call(
        _moe_padfill_kernel,
        out_shape=jax.ShapeDtypeStruct(xs.shape, xs.dtype),
        grid=(N_EXPERTS,),
        in_specs=[pl.BlockSpec(memory_space=pltpu.SMEM),
                  pl.BlockSpec(memory_space=pl.ANY)],
        out_specs=pl.BlockSpec(memory_space=pl.ANY),
        scratch_shapes=[pltpu.VMEM((SUB, LANE), F32), pltpu.SemaphoreType.DMA(())],
        input_output_aliases={1: 0},
        compiler_params=_cp(("arbitrary",)),
        name="moe_padfill",
    )(pstart, xs)


def _experts_kernel(blk_ref, xs_ref, w13_ref, w2_ref, y_ref, w13_bf, w2_bf):
    i = pl.program_id(0)
    nb = xs_ref.shape[0] // SUB
    ff = w2_bf.shape[0]

    @pl.when(i < blk_ref[SUB // 2, 0])
    def _():
        prev = blk_ref[0, jnp.maximum(i - 1, 0)]

        @pl.when((i == 0) | (blk_ref[0, i] != prev))
        def _():
            w13_bf[...] = w13_ref[0, 0].astype(BF16)
            w2_bf[...] = w2_ref[0, 0].astype(BF16)

        x = _tile_rows_load(xs_ref, nb).astype(BF16)
        h = jnp.dot(x, w13_bf[...], preferred_element_type=F32)
        act = (_silu(h[:, 0:ff]) * h[:, ff:2 * ff]).astype(BF16)
        _tile_rows_store(y_ref, jnp.dot(act, w2_bf[...], preferred_element_type=F32))


def moe_experts(blk, xs, w13, w2, layer, nb_total):
    _, E, D, F2 = w13.shape
    last = lambda i, b: jnp.minimum(i, b[SUB // 2, 0] - 1)
    grid_spec = pltpu.PrefetchScalarGridSpec(
        num_scalar_prefetch=1,
        grid=(nb_total,),
        in_specs=[pl.BlockSpec((MOE_BLK * SUB, LANE), lambda i, b: (last(i, b), 0)),
                  pl.BlockSpec((1, 1, D, F2), lambda i, b: (layer, b[0, last(i, b)], 0, 0)),
                  pl.BlockSpec((1, 1, F2 // 2, D), lambda i, b: (layer, b[0, last(i, b)], 0, 0))],
        out_specs=pl.BlockSpec((MOE_BLK * SUB, LANE), lambda i, b: (last(i, b), 0)),
        scratch_shapes=[pltpu.VMEM((D, F2), BF16), pltpu.VMEM((F2 // 2, D), BF16)])
    return pl.pallas_call(
        _experts_kernel,
        out_shape=jax.ShapeDtypeStruct(xs.shape, F32),
        grid_spec=grid_spec,
        compiler_params=_cp(("arbitrary",)),
        name="moe_experts",
    )(blk, xs, w13, w2)


def _moe_combine_kernel(dest_ref, wt_ref, h_ref, x_ref, mod_ref, s13_ref, s2_ref, l2g_ref, l2b_ref, y_hbm,
                        o_ref, g_scr, sem, *, alpha):
    tm = h_ref.shape[0]
    ff = s2_ref.shape[0]

    def issue(j, carry):
        for k in range(TOP_K):
            pltpu.make_async_copy(_tile_at(y_hbm, dest_ref[k, j]), _tile_at(g_scr.at[k], j), sem).start()
        return carry

    lax.fori_loop(0, tm, issue, 0)
    hs = _bdot(h_ref[...], s13_ref[...])
    f = _bdot(_silu(hs[:, 0:ff]) * hs[:, ff:2 * ff], s2_ref[...])

    def drain(j, carry):
        for k in range(TOP_K):
            pltpu.make_async_copy(y_hbm.at[pl.ds(0, SUB), :], g_scr.at[0, pl.ds(0, SUB), :], sem).wait()
        return carry

    lax.fori_loop(0, tm, drain, 0)
    for k in range(TOP_K):
        f = f + wt_ref[:, k:k + 1] * _tile_rows_load(g_scr.at[k], tm)
    o_ref[...] = (_layer_norm(alpha * x_ref[...] + mod_ref[0, 5:6, :] * f) * l2g_ref[...] + l2b_ref[...])


def moe_combine(dest, wt_t, h2, x1, mod3, wts, y, row_of_tile, tm, alpha):
    N, D = h2.shape
    full = lambda a: pl.BlockSpec(a.shape, lambda i: (0,) * a.ndim)
    kern = functools.partial(_moe_combine_kernel, alpha=alpha)
    return pl.pallas_call(
        kern,
        out_shape=jax.ShapeDtypeStruct((N, D), F32),
        grid=(N // tm,),
        in_specs=[pl.BlockSpec((TOP_K, tm), lambda i: (0, i), memory_space=pltpu.SMEM),
                  pl.BlockSpec((tm, TOP_K), lambda i: (i, 0)),
                  pl.BlockSpec((tm, D), lambda i: (i, 0)),
                  pl.BlockSpec((tm, D), lambda i: (i, 0)),
                  pl.BlockSpec((1, 6, D), lambda i: (row_of_tile(i), 0, 0))]
                 + [full(a) for a in wts] + [pl.BlockSpec(memory_space=pl.ANY)],
        out_specs=pl.BlockSpec((tm, D), lambda i: (i, 0)),
        scratch_shapes=[pltpu.VMEM((TOP_K, tm * SUB, LANE), F32), pltpu.SemaphoreType.DMA(())],
        compiler_params=_cp(("arbitrary",)),
        name="moe_combine",
    )(dest, wt_t, h2, x1, mod3, *wts, y)


def moe_ffn(h2, h2t, x1, mod3, mp, row_of_tile, tm, alpha):
    N = h2.shape[0]
    nb_total = (N * TOP_K + MOE_BLK - 1) // MOE_BLK + N_EXPERTS
    nbp = ((nb_total + LANE - 1) // LANE) * LANE
    eidx, wts = moe_router(h2, mp['rwt'], mp['bias'], tm)
    pstart, plan, blk = moe_counts(eidx, tm, nbp)
    dest = moe_dest(eidx, pstart, tm)
    xs = moe_scatter(dest, h2t, nb_total * MOE_BLK, tm)
    xs = moe_padfill(plan, xs)
    y = moe_experts(blk, xs, mp['w13'], mp['w2'], mp['layer'], nb_total)
    return moe_combine(dest, wts.T, h2, x1, mod3, mp['comb'], y, row_of_tile, tm, alpha)


def token_mixers(x2, B, T, W, mod3, row_of_tile, lp, states, tm):
    pa, pbg, gg, u = inproj(x2, mod3, lp['w_in'], row_of_tile, tm)
    pac = short_conv(pa.reshape(B, T, PA_W), lp['taps'], W).reshape(B * T, PA_W)
    ss, sd, gb = rwkv_prep(pac, lp['prep'], tm)
    s_rwkv, s_gla, s_s5 = states
    y2, f_rwkv = rwkv_scan(ss, sd, s_rwkv, B, T)
    o2, f_gla = gla_scan(pbg, lp['gla_aup'], lp['gla_ab'], s_gla, B, T)
    yd, f_s5 = s5_scan(u.reshape(B, T, S5_W), lp['s5_a'], lp['s5_bbd'], lp['s5_cbd'], s_s5)
    return (y2, gb, o2, gg, yd.reshape(2, B * T, S5_W), u), (f_rwkv, f_gla, f_s5)


def zero_states(B):
    return ((jnp.zeros((B, RW // LANE, LANE, LANE), F32),) * 2,
            (jnp.zeros((B, RW, GLA_KP), F32),) * 2,
            jnp.zeros((2, B, S5_LANES), F32))


def _inproj_columns():
    r_cols = 3 * RW + 2 * DECAY_RANK + 2 * ICLR_RANK + GATE_RANK
    kd = GLA_HEADS * GLA_K
    gq, gk, gv = r_cols, r_cols + kd, r_cols + 2 * kd
    gg = gv + RW
    gad = gg + RW
    pc = gad + 2 * GLA_RANK
    z = lambda n: [-1] * n
    cols = list(range(0, r_cols)) + z(PA_W - r_cols)
    cols += list(range(gq, gq + kd)) + z(256 - kd)
    cols += list(range(gk, gk + kd)) + z(256 - kd)
    cols += list(range(gv, gv + RW))
    cols += list(range(gad, gad + 2 * GLA_RANK)) + z(LANE - 2 * GLA_RANK)
    cols += list(range(gg, gg + RW))
    cols += list(range(pc, pc + S5_W))
    assert len(cols) == NP_W
    return np.asarray(cols, np.int32)


def _head_block_ones():
    h = np.arange(RW) // HEAD
    return jnp.asarray(h[:, None] == h[None, :], BF16)


def _layer_params(l, p):
    D = p['w_in'].shape[1]
    cols = _inproj_columns()
    w_in = jnp.concatenate([p['w_in'][l], jnp.zeros((D, 1), F32)], axis=1)
    w_in = jnp.take(w_in, jnp.asarray(np.where(cols < 0, w_in.shape[1] - 1, cols)), axis=1).astype(BF16)
    taps = p['rwkv_conv'][l].reshape(9, -1)
    taps = jnp.concatenate([taps, jnp.zeros((9, PA_W - taps.shape[1]), F32)], axis=1)
    wup = jnp.zeros((LANE, 2 * RW), F32)
    aup = jnp.zeros((LANE, 2 * RW), F32)
    for d in range(2):
        wup = wup.at[d * DECAY_RANK:(d + 1) * DECAY_RANK, d * RW:(d + 1) * RW].set(p['rwkv_w_up'][l, d])
        o = 2 * DECAY_RANK + d * ICLR_RANK
        aup = aup.at[o:o + ICLR_RANK, d * RW:(d + 1) * RW].set(p['rwkv_a_up'][l, d])
    gup = jnp.zeros((LANE, RW), F32).at[0:GATE_RANK].set(p['rwkv_g_up'][l])
    row = lambda a: a.reshape(1, -1)
    prep = (wup, aup, gup, row(p['rwkv_w0'][l]), row(p['rwkv_a0'][l]), row(p['rwkv_k_k'][l]),
            row(p['rwkv_k_a'][l]), row(p['rwkv_r_k'][l]), _head_block_ones())
    kd = GLA_HEADS * GLA_K
    gla_aup = jnp.zeros((2, LANE, GLA_KP), F32)
    for d in range(2):
        gla_aup = gla_aup.at[d, d * GLA_RANK:(d + 1) * GLA_RANK, 0:kd].set(p['gla_a_up'][l, d])
    gla_ab = jnp.zeros((2, 1, GLA_KP), F32).at[:, 0, 0:kd].set(p['gla_a_bias'][l])
    lam_re, lam_im = p['s5_lam_re'][l], p['s5_lam_im'][l]
    dt = jnp.exp(p['s5_log_dt'][l])[:, :, None]
    zr, zi = lam_re[:, None, :] * dt, lam_im[:, None, :] * dt
    mag = jnp.exp(zr)
    ab_r, ab_i = mag * jnp.cos(zi), mag * jnp.sin(zi)
    den = (lam_re * lam_re + lam_im * lam_im)[:, None, :]
    f_r = ((ab_r - 1) * lam_re[:, None, :] + ab_i * lam_im[:, None, :]) / den
    f_i = (ab_i * lam_re[:, None, :] - (ab_r - 1) * lam_im[:, None, :]) / den
    b_re, b_im = p['s5_b_re'][l], p['s5_b_im'][l]
    bb_r = f_r[..., None] * b_re - f_i[..., None] * b_im
    bb_i = f_r[..., None] * b_im + f_i[..., None] * b_re
    eye_g = jnp.eye(S5_GROUPS, dtype=F32)
    half = S5_LANES // 2

    def in_blockdiag(bb):
        return jnp.einsum('dgpc,gh->dgchp', bb, eye_g).reshape(2, S5_W, half)

    def out_blockdiag(cc):
        return jnp.einsum('dgcp,gh->dgphc', cc, eye_g).reshape(2, half, S5_W)

    s5_bbd = jnp.concatenate([in_blockdiag(bb_r), in_blockdiag(bb_i)], axis=2).astype(BF16)
    s5_cbd = jnp.concatenate([out_blockdiag(p['s5_c_re'][l]), -out_blockdiag(p['s5_c_im'][l])], axis=1).astype(BF16)
    s5_a = jnp.concatenate([ab_r.reshape(2, 1, half), ab_i.reshape(2, 1, half)], axis=2)
    mix = (_head_block_ones(), row(p['rwkv_ln_g'][l]), row(p['rwkv_ln_b'][l]),
           row(jnp.tile(p['gla_norm_g'][l], GLA_HEADS)), row(p['s5_d'][l]),
           p['s5_glu_w'][l].astype(BF16), row(p['s5_glu_b'][l]), p['w_out'][l].astype(BF16),
           row(p['ln1_g'][l]), row(p['ln1_b'][l]))
    moe = dict(rwt=p['router_w'][l].T,
               bias=jnp.broadcast_to(p['router_bias'][l][:, None], (N_EXPERTS, LANE)),
               w13=p['exp_w13'], w2=p['exp_w2'], layer=l,
               comb=(p['sh_w13'][l].astype(BF16), p['sh_w2'][l].astype(BF16), row(p['ln2_g'][l]), row(p['ln2_b'][l])))
    return dict(w_in=w_in, taps=taps, prep=prep, gla_aup=gla_aup, gla_ab=gla_ab,
                s5_a=s5_a, s5_bbd=s5_bbd, s5_cbd=s5_cbd, mix=mix, moe=moe)


_ARG_NAMES = ('x', 'c', 'ctx', 'c_ctx', 'w_mod', 'b_mod', 'w_in', 'rwkv_conv', 'rwkv_w0', 'rwkv_w_up', 'rwkv_a0',
              'rwkv_a_up', 'rwkv_g_up', 'rwkv_k_k', 'rwkv_k_a', 'rwkv_r_k', 'rwkv_ln_g', 'rwkv_ln_b', 'gla_a_up',
              'gla_a_bias', 'gla_norm_g', 's5_lam_re', 's5_lam_im', 's5_log_dt', 's5_b_re', 's5_b_im', 's5_c_re',
              's5_c_im', 's5_d', 's5_glu_w', 's5_glu_b', 'w_out', 'ln1_g', 'ln1_b', 'router_w', 'router_bias',
              'exp_w13', 'exp_w2', 'sh_w13', 'sh_w2', 'ln2_g', 'ln2_b')


def _tile(n, pref):
    t = pref
    while n % t:
        t //= 2
    return t


def kernel(x, c, ctx, c_ctx, w_mod, b_mod, w_in, rwkv_conv, rwkv_w0, rwkv_w_up, rwkv_a0, rwkv_a_up, rwkv_g_up,
           rwkv_k_k, rwkv_k_a, rwkv_r_k, rwkv_ln_g, rwkv_ln_b, gla_a_up, gla_a_bias, gla_norm_g, s5_lam_re,
           s5_lam_im, s5_log_dt, s5_b_re, s5_b_im, s5_c_re, s5_c_im, s5_d, s5_glu_w, s5_glu_b, w_out, ln1_g,
           ln1_b, router_w, router_bias, exp_w13, exp_w2, sh_w13, sh_w2, ln2_g, ln2_b):
    p = dict(zip(_ARG_NAMES, (x, c, ctx, c_ctx, w_mod, b_mod, w_in, rwkv_conv, rwkv_w0, rwkv_w_up, rwkv_a0,
                              rwkv_a_up, rwkv_g_up, rwkv_k_k, rwkv_k_a, rwkv_r_k, rwkv_ln_g, rwkv_ln_b, gla_a_up,
                              gla_a_bias, gla_norm_g, s5_lam_re, s5_lam_im, s5_log_dt, s5_b_re, s5_b_im, s5_c_re,
                              s5_c_im, s5_d, s5_glu_w, s5_glu_b, w_out, ln1_g, ln1_b, router_w, router_bias,
                              exp_w13, exp_w2, sh_w13, sh_w2, ln2_g, ln2_b)))
    B, T, D = x.shape
    TC = ctx.shape[1]
    L = w_mod.shape[0]
    alpha = (2 * L) ** 0.25
    n_lat, n_ctx = B * T, B * TC
    R = ((B + 1 + SUB - 1) // SUB) * SUB
    cc = jnp.zeros((R, D), F32).at[0:B].set(c).at[B].set(c_ctx)
    mod = mod_table(cc, w_mod, b_mod)
    tm = _tile(T, 256)
    tmc = _tile(n_ctx, 256)
    tmm = min(_tile(T, 128), _tile(n_ctx, 128))
    lat_row = lambda i: (i * tm) // T
    ctx_row = lambda i: B
    x2 = x.reshape(n_lat, D)
    c2 = ctx.reshape(n_ctx, D)
    for l in range(L):
        last = l == L - 1
        lp = _layer_params(l, p)
        mod3 = mod[l].reshape(R, 6, D)
        outs_c, st_c = token_mixers(c2, B, TC, TC, mod3, ctx_row, lp, zero_states(B), tmc)
        outs, _ = token_mixers(x2, B, T, GRID_W, mod3, lat_row, lp, st_c, tm)
        x1, h2, h2t = mix_out(*outs, x2, mod3, lp['mix'], lat_row, tm, alpha)
        if last:
            x2 = moe_ffn(h2, h2t, x1, mod3, lp['moe'], lambda i: (i * tmm) // T, tmm, alpha)
        else:
            c1, hc2, hc2t = mix_out(*outs_c, c2, mod3, lp['mix'], ctx_row, tmc, alpha)
            row_all = lambda i: jnp.where(i * tmm < n_ctx, B, (i * tmm - n_ctx) // T)
            out = moe_ffn(jnp.concatenate([hc2, h2]), jnp.concatenate([hc2t, h2t]), jnp.concatenate([c1, x1]),
                          mod3, lp['moe'], row_all, tmm, alpha)
            c2, x2 = out[:n_ctx], out[n_ctx:]
    return x2.reshape(B, T, D)
```

```python
import functools
import math

import numpy as np
import jax
import jax.numpy as jnp
from jax import lax
from jax.experimental import pallas as pl
from jax.experimental.pallas import tpu as pltpu

F32 = jnp.float32
BF16 = jnp.bfloat16
I32 = jnp.int32

GRID_W = 64
RWKV_HEADS = 6
HEAD = 64
RW = RWKV_HEADS * HEAD
DECAY_RANK = 32
ICLR_RANK = 32
GATE_RANK = 64
RWKV_GN_EPS = 64e-5
GLA_HEADS = 6
GLA_K = 32
GLA_RANK = 16
GLA_TEMP = 16.0
S5_GROUPS = 16
S5_GROUP = 16
S5_STATE = 64
S5_W = S5_GROUPS * S5_GROUP
S5_LANES = 2 * S5_GROUPS * S5_STATE
N_EXPERTS = 256
TOP_K = 8
N_EGROUPS = 8
TOPK_GROUPS = 4
ROUTE_SCALE = 2.5
LN_EPS = 1e-6
CHUNK = 64
MOE_BLK = 256
LANE = 128
SUB = 8
VMEM_LIMIT = 56 * 1024 * 1024

PA_W = 3 * RW + 2 * LANE
PBG_W = 256 + 256 + RW + LANE
NP_W = PA_W + PBG_W + RW + S5_W


def _cp(sem):
    return pltpu.CompilerParams(dimension_semantics=sem, vmem_limit_bytes=VMEM_LIMIT)


def _sigmoid(x):
    return 1.0 / (1.0 + jnp.exp(-x))


def _silu(x):
    return x * _sigmoid(x)


def _bdot(a, b, dims=None):
    a = a.astype(BF16)
    b = b.astype(BF16)
    if dims is None:
        return jnp.dot(a, b, preferred_element_type=F32)
    return lax.dot_general(a, b, dims, preferred_element_type=F32)


def _split2(x):
    hi = x.astype(BF16)
    lo = (x - hi.astype(F32)).astype(BF16)
    return hi, lo


def _split3(x):
    hi = x.astype(BF16)
    r = x - hi.astype(F32)
    mid = r.astype(BF16)
    lo = (r - mid.astype(F32)).astype(BF16)
    return hi, mid, lo


NT_DIMS = (((1,), (1,)), ((), ()))
TN_DIMS = (((0,), (0,)), ((), ()))


def _dot33(a, b, dims=None):
    ah, al = _split2(a)
    bh, bl = _split2(b)
    return _bdot(ah, bh, dims) + (_bdot(ah, bl, dims) + _bdot(al, bh, dims))


def _dot_exact_lhs(m_exact, x, dims=None):
    h, m, l = _split3(x)
    return _bdot(m_exact, h, dims) + (_bdot(m_exact, m, dims) + _bdot(m_exact, l, dims))


def _dot_exact_rhs(x, m_exact, dims=None):
    h, m, l = _split3(x)
    return _bdot(h, m_exact, dims) + (_bdot(m, m_exact, dims) + _bdot(l, m_exact, dims))


def _layer_norm(x):
    mu = jnp.mean(x, axis=-1, keepdims=True)
    xc = x - mu
    var = jnp.mean(xc * xc, axis=-1, keepdims=True)
    return xc * lax.rsqrt(var + LN_EPS)


def _mod_kernel(c_ref, w_ref, b_ref, o_ref):
    s = _silu(c_ref[...])
    o_ref[0] = _dot33(s, w_ref[0]) + b_ref[0]


def mod_table(cc, w_mod, b_mod):
    L, D, D6 = w_mod.shape
    R = cc.shape[0]
    tn = 1536
    return pl.pallas_call(
        _mod_kernel,
        out_shape=jax.ShapeDtypeStruct((L, R, D6), F32),
        grid=(L, D6 // tn),
        in_specs=[pl.BlockSpec((R, D), lambda l, j: (0, 0)),
                  pl.BlockSpec((1, D, tn), lambda l, j: (l, 0, j)),
                  pl.BlockSpec((1, 1, tn), lambda l, j: (l, 0, j))],
        out_specs=pl.BlockSpec((1, R, tn), lambda l, j: (l, 0, j)),
        compiler_params=_cp(("arbitrary", "arbitrary")),
        name="mod_table",
    )(cc, w_mod, b_mod.reshape(L, 1, D6))


def _inproj_kernel(x_ref, mod_ref, w_ref, pa_ref, pbg_ref, gg_ref, u_ref):
    x = x_ref[...]
    h = _layer_norm(x) * (1.0 + mod_ref[0, 1:2, :]) + mod_ref[0, 0:1, :]
    hb = h.astype(BF16)
    o = 0
    for ref in (pa_ref, pbg_ref, gg_ref, u_ref):
        w = ref.shape[-1]
        ref[...] = jnp.dot(hb, w_ref[:, o:o + w], preferred_element_type=F32)
        o += w


def inproj(x2, mod3, w_bf, row_of_tile, tm):
    N, D = x2.shape
    return pl.pallas_call(
        _inproj_kernel,
        out_shape=[jax.ShapeDtypeStruct((N, PA_W), F32),
                   jax.ShapeDtypeStruct((N, PBG_W), F32),
                   jax.ShapeDtypeStruct((N, RW), F32),
                   jax.ShapeDtypeStruct((N, S5_W), F32)],
        grid=(N // tm,),
        in_specs=[pl.BlockSpec((tm, D), lambda i: (i, 0)),
                  pl.BlockSpec((1, 6, D), lambda i: (row_of_tile(i), 0, 0)),
                  pl.BlockSpec((D, NP_W), lambda i: (0, 0))],
        out_specs=[pl.BlockSpec((tm, PA_W), lambda i: (i, 0)),
                   pl.BlockSpec((tm, PBG_W), lambda i: (i, 0)),
                   pl.BlockSpec((tm, RW), lambda i: (i, 0)),
                   pl.BlockSpec((tm, S5_W), lambda i: (i, 0))],
        compiler_params=_cp(("arbitrary",)),
        name="inproj",
    )(x2, mod3, w_bf)


CONV_PAD = 72


def _conv_kernel(x_ref, taps_ref, o_ref, buf_ref, *, T, W, vertical):
    pad = CONV_PAD
    zeros = jnp.zeros((pad, LANE), F32)
    buf_ref[0:pad, :] = zeros
    buf_ref[pad + T:pad + T + pad, :] = zeros
    buf_ref[pad:pad + T, :] = x_ref[0]
    ch = min(T, 256)
    col = lax.broadcasted_iota(I32, (ch, LANE), 0) & (W - 1)
    left_ok = col >= 1
    right_ok = col <= W - 2
    for c in range(T // ch):
        base = pad + c * ch
        acc = jnp.zeros((ch, LANE), F32)
        for dr in ((0, 1, 2) if vertical else (1,)):
            for dc in range(3):
                off = (dr - 1) * W + (dc - 1)
                v = buf_ref[base + off:base + off + ch, :]
                if dc == 0:
                    v = jnp.where(left_ok, v, 0.0)
                elif dc == 2:
                    v = jnp.where(right_ok, v, 0.0)
                acc = acc + v * taps_ref[3 * dr + dc:3 * dr + dc + 1, :]
        o_ref[0, c * ch:(c + 1) * ch, :] = acc


def short_conv(pa3, taps9, W):
    B, T, C = pa3.shape
    vertical = T > W
    assert W & (W - 1) == 0 and (not vertical or W + 1 <= CONV_PAD)
    kern = functools.partial(_conv_kernel, T=T, W=W, vertical=vertical)
    return pl.pallas_call(
        kern,
        out_shape=jax.ShapeDtypeStruct((B, T, C), F32),
        grid=(B, C // LANE),
        in_specs=[pl.BlockSpec((1, T, LANE), lambda b, j: (b, 0, j)),
                  pl.BlockSpec((9, LANE), lambda b, j: (0, j))],
        out_specs=pl.BlockSpec((1, T, LANE), lambda b, j: (b, 0, j)),
        scratch_shapes=[pltpu.VMEM((T + 2 * CONV_PAD, LANE), F32)],
        compiler_params=_cp(("arbitrary", "arbitrary")),
        name="short_conv",
    )(pa3, taps9)


def _rwkv_prep_kernel(pa_ref, wup_ref, aup_ref, gup_ref, w0_ref, a0_ref, kk_ref, ka_ref, rk_ref, hb_ref,
                      ss_ref, sd_ref, gb_ref):
    r = pa_ref[:, 0:RW]
    k = pa_ref[:, RW:2 * RW]
    v = pa_ref[:, 2 * RW:3 * RW]
    wa = pa_ref[:, 3 * RW:3 * RW + LANE]
    gd = pa_ref[:, 3 * RW + LANE:3 * RW + 2 * LANE]
    z = w0_ref[...] + _dot33(jnp.tanh(wa), wup_ref[...])
    lw = -_sigmoid(z) * math.exp(-0.5)
    a = _sigmoid(a0_ref[...] + _dot33(wa, aup_ref[...]))
    g = _dot33(_sigmoid(gd), gup_ref[...])
    hb = hb_ref[...]
    kk = k * kk_ref[...]
    kk = kk * lax.rsqrt(_dot_exact_rhs(kk * kk, hb) + 1e-12)
    ka = ka_ref[...]
    ss_ref[:, 0:RW] = r
    ss_ref[:, RW:2 * RW] = v
    ss_ref[:, 2 * RW:3 * RW] = kk
    rk2 = jnp.zeros_like(r)
    for d in range(2):
        ad = a[:, d * RW:(d + 1) * RW]
        k2 = k * (1.0 + (ad - 1.0) * ka)
        sd_ref[d, :, 0:RW] = lw[:, d * RW:(d + 1) * RW]
        sd_ref[d, :, RW:2 * RW] = k2
        sd_ref[d, :, 2 * RW:3 * RW] = kk * ad
        rk2 = rk2 + r * k2
    bonus = _dot_exact_rhs(rk2 * rk_ref[...], hb) * v
    gb_ref[:, 0:RW] = g
    gb_ref[:, RW:2 * RW] = bonus


def rwkv_prep(pa2, wts, tm):
    N = pa2.shape[0]
    full = lambda a: pl.BlockSpec(a.shape, lambda i: (0,) * a.ndim)
    return pl.pallas_call(
        _rwkv_prep_kernel,
        out_shape=[jax.ShapeDtypeStruct((N, 3 * RW), F32),
                   jax.ShapeDtypeStruct((2, N, 3 * RW), F32),
                   jax.ShapeDtypeStruct((N, 2 * RW), F32)],
        grid=(N // tm,),
        in_specs=[pl.BlockSpec((tm, PA_W), lambda i: (i, 0))] + [full(a) for a in wts],
        out_specs=[pl.BlockSpec((tm, 3 * RW), lambda i: (i, 0)),
                   pl.BlockSpec((2, tm, 3 * RW), lambda i: (0, i, 0)),
                   pl.BlockSpec((tm, 2 * RW), lambda i: (i, 0))],
        compiler_params=_cp(("arbitrary",)),
        name="rwkv_prep",
    )(pa2, *wts)


RWKV_GROUP = 8


def _rwkv_scan_kernel(ss_ref, sd_ref, s0_ref, y_ref, sfin_ref, s_scr, *, nsteps, group, reverse):
    n = pl.program_id(1)
    C = CHUNK
    P = 2 * C
    npair = RW // LANE

    @pl.when(n == 0)
    def _():
        s_scr[...] = s0_ref[0]

    row = lax.broadcasted_iota(I32, (P, P), 0)
    col = lax.broadcasted_iota(I32, (P, P), 1)
    same = (row >> 6) == (col >> 6)
    dlt = (col & (C - 1)) - (row & (C - 1)) if reverse else (row & (C - 1)) - (col & (C - 1))
    strict = same & (dlt > 0)
    incl = same & (dlt >= 0)
    eye = (row == col).astype(F32)
    lvl_masks = [((row >> (lvl + 1)) == (col >> (lvl + 1))) & ((row >> lvl) != (col >> lvl)) for lvl in range(6)]
    rc = lax.broadcasted_iota(I32, (C, C), 0)
    cc = lax.broadcasted_iota(I32, (C, C), 1)
    tri = jnp.where((cc >= rc) if reverse else (rc >= cc), 1.0, 0.0).astype(BF16)
    head0 = lax.broadcasted_iota(I32, (C, LANE), 1) < HEAD

    def stack(x):
        return jnp.concatenate([jnp.where(head0, x, 0.0), jnp.where(head0, 0.0, x)], axis=0)

    streams = [(g, p) for g in range(group) for p in range(npair)]
    tm_ = {}
    for (g, p) in streams:
        t0, t1 = g * C, (g + 1) * C
        lo, hi = p * LANE, (p + 1) * LANE
        lw = sd_ref[0, t0:t1, lo:hi]
        tm_[(g, p)] = dict(lw=lw, cl=_dot_exact_lhs(tri, lw))
    for (g, p) in streams:
        t = tm_[(g, p)]
        t0, t1 = g * C, (g + 1) * C
        lo, hi = p * LANE, (p + 1) * LANE
        r = ss_ref[t0:t1, lo:hi]
        v = ss_ref[t0:t1, RW + lo:RW + hi]
        kk = ss_ref[t0:t1, 2 * RW + lo:2 * RW + hi]
        k2 = sd_ref[0, t0:t1, RW + lo:RW + hi]
        b = sd_ref[0, t0:t1, 2 * RW + lo:2 * RW + hi]
        cl, lw = t['cl'], t['lw']
        t['ptot'] = jnp.exp(jnp.sum(lw, axis=0, keepdims=True))
        pinv = jnp.exp(-cl)
        left = jnp.concatenate([stack(-kk * jnp.exp(cl - lw)), stack(r * jnp.exp(cl))], axis=0)
        right = jnp.concatenate([stack(b * pinv), stack(k2 * pinv)], axis=0)
        t['v_st'] = stack(v)
        t['left'] = left.astype(BF16)
        t['bk'] = (right * t['ptot']).astype(BF16)
        aa = _bdot(left, right, NT_DIMS)
        t['nmat'] = jnp.where(strict, aa[0:P, 0:P], 0.0)
        t['a_ak'] = jnp.where(strict, aa[0:P, P:2 * P], 0.0)
        t['a_rbk'] = jnp.concatenate([jnp.where(incl, aa[P:2 * P, 0:P], 0.0),
                                      jnp.where(incl, aa[P:2 * P, P:2 * P], 0.0)], axis=1).astype(BF16)
        t['tinv'] = eye
    for sk in streams:
        t = tm_[sk]
        t['akv'] = _bdot(t['a_ak'], t['v_st'])
    for m in lvl_masks:
        for sk in streams:
            t = tm_[sk]
            t['et'] = _bdot(jnp.where(m, t['nmat'], 0.0), t['tinv'])
        for sk in streams:
            t = tm_[sk]
            t['tinv'] = t['tinv'] + _bdot(t['tinv'], t['et'])
    state = [s_scr[p] for p in range(npair)]
    pairs = range(npair)
    for g in (range(group - 1, -1, -1) if reverse else range(group)):
        ts = [tm_[(g, p)] for p in pairs]
        a_s = [_bdot(ts[p]['left'], state[p], NT_DIMS) for p in pairs]
        u = [_bdot(ts[p]['tinv'], a_s[p][0:P] + ts[p]['akv']) for p in pairs]
        uv = [jnp.concatenate([u[p], ts[p]['v_st']], axis=0) for p in pairs]
        state = [state[p] * ts[p]['ptot'] + _bdot(uv[p], ts[p]['bk'], TN_DIMS) for p in pairs]
        for p in pairs:
            y_st = a_s[p][P:2 * P] + _bdot(ts[p]['a_rbk'], uv[p])
            y_ref[g * C:(g + 1) * C, p * LANE:(p + 1) * LANE] = y_st[0:C] + y_st[C:P]
    for p in pairs:
        s_scr[p] = state[p]

    @pl.when(n == nsteps - 1)
    def _():
        sfin_ref[0] = s_scr[...]


def rwkv_scan_dir(ss, sd, s0, B, T, d):
    nc = T // CHUNK
    group = min(RWKV_GROUP, nc)
    nsteps = nc // group
    npair = RW // LANE
    rows = group * CHUNK
    blk = (lambda b, n: b * nsteps + nsteps - 1 - n) if d else (lambda b, n: b * nsteps + n)
    kern = functools.partial(_rwkv_scan_kernel, nsteps=nsteps, group=group, reverse=bool(d))
    return pl.pallas_call(
        kern,
        out_shape=[jax.ShapeDtypeStruct((B * T, RW), F32),
                   jax.ShapeDtypeStruct((B, npair, LANE, LANE), F32)],
        grid=(B, nsteps),
        in_specs=[pl.BlockSpec((rows, 3 * RW), lambda b, n: (blk(b, n), 0)),
                  pl.BlockSpec((1, rows, 3 * RW), lambda b, n: (d, blk(b, n), 0)),
                  pl.BlockSpec((1, npair, LANE, LANE), lambda b, n: (b, 0, 0, 0))],
        out_specs=[pl.BlockSpec((rows, RW), lambda b, n: (blk(b, n), 0)),
                   pl.BlockSpec((1, npair, LANE, LANE), lambda b, n: (b, 0, 0, 0))],
        scratch_shapes=[pltpu.VMEM((npair, LANE, LANE), F32)],
        compiler_params=_cp(("arbitrary", "arbitrary")),
        name="rwkv_scan_bwd" if d else "rwkv_scan_fwd",
    )(ss, sd, s0)


def rwkv_scan(ss, sd, s0, B, T):
    y0, f0 = rwkv_scan_dir(ss, sd, s0[0], B, T, 0)
    y1, f1 = rwkv_scan_dir(ss, sd, s0[1], B, T, 1)
    return (y0, y1), (f0, f1)


GLA_KP = 256


GLA_GROUP = 8


def _gla_kernel(pbg_ref, aup_ref, ab_ref, s0_ref, o_ref, sfin_ref, s_scr, *, nsteps, group, reverse):
    n = pl.program_id(1)
    C = CHUNK
    R = group * C

    @pl.when(n == 0)
    def _():
        s_scr[...] = s0_ref[0]

    q = pbg_ref[:, 0:GLA_KP]
    k = pbg_ref[:, GLA_KP:2 * GLA_KP]
    v = pbg_ref[:, 2 * GLA_KP:2 * GLA_KP + RW]
    ad = pbg_ref[:, 2 * GLA_KP + RW:2 * GLA_KP + RW + LANE]
    x = _dot33(ad, aup_ref[0]) + ab_ref[0]
    la = (jnp.minimum(x, 0.0) - jnp.log(1.0 + jnp.exp(-jnp.abs(x)))) * (1.0 / GLA_TEMP)
    rr = lax.broadcasted_iota(I32, (R, R), 0)
    rc = lax.broadcasted_iota(I32, (R, R), 1)
    same = (rr >> 6) == (rc >> 6)
    order = (rc >= rr) if reverse else (rr >= rc)
    bcum = _dot_exact_lhs(jnp.where(same & order, 1.0, 0.0).astype(BF16), la)
    tot = _dot_exact_lhs(jnp.where(same, 1.0, 0.0).astype(BF16), la)
    q_in = q * jnp.exp(bcum) * (GLA_K ** -0.5)
    k_in = k * jnp.exp(-bcum)
    k_st = k * jnp.exp(tot - bcum)
    dn = jnp.exp(tot)
    klane = lax.broadcasted_iota(I32, (C, GLA_KP), 1)
    rt = lax.broadcasted_iota(I32, (GLA_HEADS * C, C), 0) & (C - 1)
    ct = lax.broadcasted_iota(I32, (GLA_HEADS * C, C), 1)
    causal = (ct >= rt) if reverse else (rt >= ct)
    vlane = lax.broadcasted_iota(I32, (C, RW), 1)
    sv = lax.broadcasted_iota(I32, (RW, GLA_KP), 0) >> 6
    sk = lax.broadcasted_iota(I32, (RW, GLA_KP), 1) >> 5
    chunks = range(group)
    sl = [slice(g * C, (g + 1) * C) for g in chunks]
    q_rows = [jnp.concatenate([jnp.where((klane >> 5) == h, q_in[sl[g]], 0.0) for h in range(GLA_HEADS)],
                              axis=0).astype(BF16) for g in chunks]
    att = [jnp.where(causal, _bdot(q_rows[g], k_in[sl[g]], NT_DIMS), 0.0) for g in chunks]
    o_rows = [_bdot(att[g], v[sl[g]]) for g in chunks]
    kv = [jnp.where(sv == sk, _bdot(v[sl[g]], k_st[sl[g]], TN_DIMS), 0.0) for g in chunks]
    s = s_scr[...]
    for g in (reversed(chunks) if reverse else chunks):
        og = o_rows[g] + _bdot(q_rows[g], s, NT_DIMS)
        o = jnp.zeros((C, RW), F32)
        for h in range(GLA_HEADS):
            o = o + jnp.where((vlane >> 6) == h, og[h * C:(h + 1) * C], 0.0)
        o_ref[sl[g], :] = o
        s = s * dn[g * C:g * C + 1, :] + kv[g]
    s_scr[...] = s

    @pl.when(n == nsteps - 1)
    def _():
        sfin_ref[0] = s_scr[...]


def gla_scan_dir(pbg, aup, ab, s0, B, T, d):
    nc = T // CHUNK
    group = min(GLA_GROUP, nc)
    nsteps = nc // group
    rows = group * CHUNK
    blk = (lambda b, n: b * nsteps + nsteps - 1 - n) if d else (lambda b, n: b * nsteps + n)
    kern = functools.partial(_gla_kernel, nsteps=nsteps, group=group, reverse=bool(d))
    return pl.pallas_call(
        kern,
        out_shape=[jax.ShapeDtypeStruct((B * T, RW), F32),
                   jax.ShapeDtypeStruct((B, RW, GLA_KP), F32)],
        grid=(B, nsteps),
        in_specs=[pl.BlockSpec((rows, PBG_W), lambda b, n: (blk(b, n), 0)),
                  pl.BlockSpec((1, LANE, GLA_KP), lambda b, n: (d, 0, 0)),
                  pl.BlockSpec((1, 1, GLA_KP), lambda b, n: (d, 0, 0)),
                  pl.BlockSpec((1, RW, GLA_KP), lambda b, n: (b, 0, 0))],
        out_specs=[pl.BlockSpec((rows, RW), lambda b, n: (blk(b, n), 0)),
                   pl.BlockSpec((1, RW, GLA_KP), lambda b, n: (b, 0, 0))],
        scratch_shapes=[pltpu.VMEM((RW, GLA_KP), F32)],
        compiler_params=_cp(("arbitrary", "arbitrary")),
        name="gla_scan_bwd" if d else "gla_scan_fwd",
    )(pbg, aup, ab, s0)


def gla_scan(pbg, aup, ab, s0, B, T):
    o0, f0 = gla_scan_dir(pbg, aup, ab, s0[0], B, T, 0)
    o1, f1 = gla_scan_dir(pbg, aup, ab, s0[1], B, T, 1)
    return (o0, o1), (f0, f1)


S5_PITCH = 72


def _s5_kernel(u_ref, a_ref, bbd_ref, cbd_ref, s0_ref, y_ref, sfin_ref, x_scr, st_scr, *, nc, nb):
    d = pl.program_id(0)
    n = pl.program_id(1)
    C = CHUNK
    half = S5_LANES // 2

    @pl.when(n == 0)
    def _():
        st_scr[...] = s0_ref[0]

    nslab = S5_LANES // LANE
    bbd = bbd_ref[0]
    for b in range(nb):
        bu = _bdot(u_ref[b], bbd)
        for j in range(nslab):
            x_scr[j, b * S5_PITCH:b * S5_PITCH + C, :] = bu[:, j * LANE:(j + 1) * LANE]
    a_re = a_ref[0, :, 0:half]
    a_im = a_ref[0, :, half:S5_LANES]

    def step(i, carry):
        re, im = carry
        t = i + d * (C - 1 - 2 * i)
        rows = pl.ds(t, nb, stride=S5_PITCH)
        bu = jnp.concatenate([x_scr[j, rows, :] for j in range(nslab)], axis=1)
        nre = a_re * re - a_im * im + bu[:, 0:half]
        nim = a_re * im + a_im * re + bu[:, half:S5_LANES]
        for j in range(nslab // 2):
            x_scr[j, rows, :] = nre[:, j * LANE:(j + 1) * LANE]
            x_scr[nslab // 2 + j, rows, :] = nim[:, j * LANE:(j + 1) * LANE]
        return nre, nim

    st = st_scr[...]
    re, im = lax.fori_loop(0, C, step, (st[:, 0:half], st[:, half:S5_LANES]))
    st_scr[...] = jnp.concatenate([re, im], axis=1)
    cbd = cbd_ref[0]
    for b in range(nb):
        xs = jnp.concatenate([x_scr[j, b * S5_PITCH:b * S5_PITCH + C, :] for j in range(nslab)], axis=1)
        y_ref[0, b] = _bdot(xs, cbd)

    @pl.when(n == nc - 1)
    def _():
        sfin_ref[0] = st_scr[...]


def s5_scan(u3, a_bar, bbd, cbd, s0):
    B, T, _ = u3.shape
    nc = T // CHUNK
    chunk = lambda d, n: n + d * (nc - 1 - 2 * n)
    kern = functools.partial(_s5_kernel, nc=nc, nb=B)
    return pl.pallas_call(
        kern,
        out_shape=[jax.ShapeDtypeStruct((2, B, T, S5_W), F32),
                   jax.ShapeDtypeStruct((2, B, S5_LANES), F32)],
        grid=(2, nc),
        in_specs=[pl.BlockSpec((B, CHUNK, S5_W), lambda d, n: (0, chunk(d, n), 0)),
                  pl.BlockSpec((1, 1, S5_LANES), lambda d, n: (d, 0, 0)),
                  pl.BlockSpec((1, S5_W, S5_LANES), lambda d, n: (d, 0, 0)),
                  pl.BlockSpec((1, S5_LANES, S5_W), lambda d, n: (d, 0, 0)),
                  pl.BlockSpec((1, B, S5_LANES), lambda d, n: (d, 0, 0))],
        out_specs=[pl.BlockSpec((1, B, CHUNK, S5_W), lambda d, n: (d, 0, chunk(d, n), 0)),
                   pl.BlockSpec((1, B, S5_LANES), lambda d, n: (d, 0, 0))],
        scratch_shapes=[pltpu.VMEM((S5_LANES // LANE, B * S5_PITCH, LANE), F32),
                        pltpu.VMEM((B, S5_LANES), F32)],
        compiler_params=_cp(("arbitrary", "arbitrary")),
        name="s5_scan",
    )(u3, a_bar, bbd, cbd, s0)


def _tile_rows_store(ref, val):
    tm = val.shape[0]
    for s in range(val.shape[1] // LANE):
        ref[pl.ds(s, tm, stride=SUB), :] = val[:, s * LANE:(s + 1) * LANE]


def _tile_rows_load(ref, tm, base=0):
    return jnp.concatenate([ref[pl.ds(base + s, tm, stride=SUB), :] for s in range(SUB)], axis=1)


def _mix_out_kernel(y0_ref, y1_ref, gb_ref, o0_ref, o1_ref, gg_ref, yd_ref, u_ref, x_ref, mod_ref,
                    hb_ref, lng_ref, lnb_ref, gng_ref, s5d_ref, gluw_ref, glub_ref, wout_ref, l1g_ref, l1b_ref,
                    x1_ref, h2_ref, h2t_ref, *, alpha):
    hb = hb_ref[...]
    inv = 1.0 / HEAD
    y = y0_ref[...] + y1_ref[...]
    yc = y - _dot_exact_rhs(y, hb) * inv
    var = _dot_exact_rhs(yc * yc, hb) * inv
    gn = yc * lax.rsqrt(var + RWKV_GN_EPS) * lng_ref[...] + lnb_ref[...]
    y_a = (gn + gb_ref[:, RW:2 * RW]) * gb_ref[:, 0:RW]
    o = o0_ref[...] + o1_ref[...]
    o = o * lax.rsqrt(_dot_exact_rhs(o * o, hb) * inv + 1e-6) * gng_ref[...]
    y_b = o * _silu(gg_ref[...])
    c = s5d_ref[...] * u_ref[...] + yd_ref[0] + yd_ref[1]
    c = 0.5 * c * (1.0 + jnp.tanh(math.sqrt(2.0 / math.pi) * (c + 0.044715 * (c * c * c))))
    y_c = c * _sigmoid(_bdot(c, gluw_ref[...]) + glub_ref[...])
    y_mix = (_bdot(y_a, wout_ref[0:RW, :]) + _bdot(y_b, wout_ref[RW:2 * RW, :])
             + _bdot(y_c, wout_ref[2 * RW:2 * RW + S5_W, :]))
    x1 = _layer_norm(alpha * x_ref[...] + mod_ref[0, 2:3, :] * y_mix) * l1g_ref[...] + l1b_ref[...]
    x1_ref[...] = x1
    h2 = _layer_norm(x1) * (1.0 + mod_ref[0, 4:5, :]) + mod_ref[0, 3:4, :]
    h2_ref[...] = h2
    _tile_rows_store(h2t_ref, h2)


def mix_out(y01, gb, o01, gg, yd, u, x2, mod3, wts, row_of_tile, tm, alpha):
    N, D = x2.shape
    full = lambda a: pl.BlockSpec(a.shape, lambda i: (0,) * a.ndim)
    kern = functools.partial(_mix_out_kernel, alpha=alpha)
    return pl.pallas_call(
        kern,
        out_shape=[jax.ShapeDtypeStruct((N, D), F32),
                   jax.ShapeDtypeStruct((N, D), F32),
                   jax.ShapeDtypeStruct((N * SUB, LANE), F32)],
        grid=(N // tm,),
        in_specs=[pl.BlockSpec((tm, RW), lambda i: (i, 0)),
                  pl.BlockSpec((tm, RW), lambda i: (i, 0)),
                  pl.BlockSpec((tm, 2 * RW), lambda i: (i, 0)),
                  pl.BlockSpec((tm, RW), lambda i: (i, 0)),
                  pl.BlockSpec((tm, RW), lambda i: (i, 0)),
                  pl.BlockSpec((tm, RW), lambda i: (i, 0)),
                  pl.BlockSpec((2, tm, S5_W), lambda i: (0, i, 0)),
                  pl.BlockSpec((tm, S5_W), lambda i: (i, 0)),
                  pl.BlockSpec((tm, D), lambda i: (i, 0)),
                  pl.BlockSpec((1, 6, D), lambda i: (row_of_tile(i), 0, 0))] + [full(a) for a in wts],
        out_specs=[pl.BlockSpec((tm, D), lambda i: (i, 0)),
                   pl.BlockSpec((tm, D), lambda i: (i, 0)),
                   pl.BlockSpec((tm * SUB, LANE), lambda i: (i, 0))],
        compiler_params=_cp(("arbitrary",)),
        name="mix_out",
    )(y01[0], y01[1], gb, o01[0], o01[1], gg, yd, u, x2, mod3, *wts)


def _first_max(x, idx, big):
    m = jnp.max(x, axis=0, keepdims=True)
    first = jnp.min(jnp.where(x == m, idx, big), axis=0, keepdims=True)
    return m, first


def _router_kernel(h_ref, rwt_ref, bias_ref, e_ref, w_ref):
    tm = h_ref.shape[0]
    gsz = N_EXPERTS // N_EGROUPS
    ninf = -jnp.inf
    s = _sigmoid(_dot33(rwt_ref[...], h_ref[...], NT_DIMS))
    ssel = s + bias_ref[:, 0:1]
    gi = lax.broadcasted_iota(I32, (gsz, tm), 0)
    gscore = []
    for g in range(N_EGROUPS):
        xg = ssel[g * gsz:(g + 1) * gsz, :]
        m1, i1 = _first_max(xg, gi, gsz)
        m2 = jnp.max(jnp.where(gi == i1, ninf, xg), axis=0, keepdims=True)
        gscore.append(m1 + m2)
    cur = jnp.concatenate(gscore, axis=0)
    gidx = lax.broadcasted_iota(I32, (N_EGROUPS, tm), 0)
    picked = jnp.zeros((N_EGROUPS, tm), F32)
    for _ in range(TOPK_GROUPS):
        _, first = _first_max(cur, gidx, N_EGROUPS)
        hit = gidx == first
        picked = jnp.where(hit, 1.0, picked)
        cur = jnp.where(hit, ninf, cur)
    x = jnp.concatenate(
        [jnp.where(picked[g:g + 1, :] > 0.5, ssel[g * gsz:(g + 1) * gsz, :], ninf) for g in range(N_EGROUPS)], axis=0)
    ei = lax.broadcasted_iota(I32, (N_EXPERTS, tm), 0)
    idxs, ws = [], []
    for _ in range(TOP_K):
        _, first = _first_max(x, ei, N_EXPERTS)
        hit = ei == first
        idxs.append(first)
        ws.append(jnp.sum(jnp.where(hit, s, 0.0), axis=0, keepdims=True))
        x = jnp.where(hit, ninf, x)
    w = jnp.concatenate(ws, axis=0)
    e_ref[...] = jnp.concatenate(idxs, axis=0)
    w_ref[...] = w / jnp.sum(w, axis=0, keepdims=True) * ROUTE_SCALE


def moe_router(h2, rwt, bias_b, tm):
    N, D = h2.shape
    return pl.pallas_call(
        _router_kernel,
        out_shape=[jax.ShapeDtypeStruct((TOP_K, N), I32), jax.ShapeDtypeStruct((TOP_K, N), F32)],
        grid=(N // tm,),
        in_specs=[pl.BlockSpec((tm, D), lambda i: (i, 0)),
                  pl.BlockSpec((N_EXPERTS, D), lambda i: (0, 0)),
                  pl.BlockSpec((N_EXPERTS, LANE), lambda i: (0, 0))],
        out_specs=[pl.BlockSpec((TOP_K, tm), lambda i: (0, i)),
                   pl.BlockSpec((TOP_K, tm), lambda i: (0, i))],
        compiler_params=_cp(("arbitrary",)),
        name="moe_router",
    )(h2, rwt, bias_b)


def _moe_count_kernel(e_ref, pstart_ref, plan_ref, blk_ref, cnt_scr, *, nt, nbp):
    i = pl.program_id(0)
    tp = e_ref.shape[1]

    @pl.when(i == 0)
    def _():
        cnt_scr[...] = jnp.zeros_like(cnt_scr)

    ei = lax.broadcasted_iota(I32, (N_EXPERTS, tp), 0)
    acc = jnp.zeros((N_EXPERTS, 1), F32)
    for k in range(TOP_K):
        acc = acc + jnp.sum(jnp.where(ei == e_ref[k:k + 1, :], 1.0, 0.0), axis=1, keepdims=True)
    cnt_scr[...] = cnt_scr[...] + acc

    @pl.when(i == nt - 1)
    def _():
        cnt = cnt_scr[...].astype(I32)
        padded = (cnt + (MOE_BLK - 1)) & (-MOE_BLK)
        r = lax.broadcasted_iota(I32, (N_EXPERTS, N_EXPERTS), 0)
        c = lax.broadcasted_iota(I32, (N_EXPERTS, N_EXPERTS), 1)
        tri = jnp.where(c <= r, 1.0, 0.0).astype(BF16)
        padded_b = jnp.broadcast_to(padded.astype(F32), (N_EXPERTS, LANE))
        p_end = _dot_exact_lhs(tri, padded_b)
        pstart = p_end - padded_b
        pstart_ref[...] = pstart.astype(I32)
        diag = r == c
        ps_row = jnp.sum(jnp.where(diag, pstart[:, 0:1], 0.0), axis=0, keepdims=True)
        cnt_row = jnp.sum(jnp.where(diag, cnt_scr[...], 0.0), axis=0, keepdims=True)
        plan_ref[...] = jnp.concatenate([ps_row, cnt_row, jnp.zeros((SUB - 2, N_EXPERTS), F32)], axis=0).astype(I32)
        lim =(lax.broadcasted_iota(I32, (N_EXPERTS, nbp), 1) * MOE_BLK).astype(F32)
        be = jnp.sum(jnp.where(p_end[:, 0:1] <= lim, 1.0, 0.0), axis=0, keepdims=True)
        be = jnp.minimum(be, N_EXPERTS - 1.0)
        nused = jnp.max(p_end[:, 0:1], axis=0, keepdims=True) * (1.0 / MOE_BLK)
        blk_ref[...] = jnp.concatenate([jnp.broadcast_to(be, (SUB // 2, nbp)),
                                        jnp.broadcast_to(nused, (SUB // 2, nbp))], axis=0).astype(I32)


def moe_counts(eidx, tp, nbp):
    N = eidx.shape[1]
    nt = N // tp
    kern = functools.partial(_moe_count_kernel, nt=nt, nbp=nbp)
    return pl.pallas_call(
        kern,
        out_shape=[jax.ShapeDtypeStruct((N_EXPERTS, LANE), I32),
                   jax.ShapeDtypeStruct((SUB, N_EXPERTS), I32),
                   jax.ShapeDtypeStruct((SUB, nbp), I32)],
        grid=(nt,),
        in_specs=[pl.BlockSpec((TOP_K, tp), lambda i: (0, i))],
        out_specs=[pl.BlockSpec((N_EXPERTS, LANE), lambda i: (0, 0)),
                   pl.BlockSpec((SUB, N_EXPERTS), lambda i: (0, 0)),
                   pl.BlockSpec((SUB, nbp), lambda i: (0, 0))],
        scratch_shapes=[pltpu.VMEM((N_EXPERTS, 1), F32)],
        compiler_params=_cp(("arbitrary",)),
        name="moe_counts",
    )(eidx)


def _moe_dest_kernel(e_ref, pstart_ref, dest_ref, base_scr):
    i = pl.program_id(0)
    tp = e_ref.shape[1]

    @pl.when(i == 0)
    def _():
        base_scr[...] = pstart_ref[:, 0:1].astype(F32)

    ei = lax.broadcasted_iota(I32, (N_EXPERTS, tp), 0)
    r = lax.broadcasted_iota(I32, (tp, tp), 0)
    c = lax.broadcasted_iota(I32, (tp, tp), 1)
    tri = jnp.where(r <= c, 1.0, 0.0).astype(BF16)
    base = base_scr[...]
    rows = []
    for k in range(TOP_K):
        hit = ei == e_ref[k:k + 1, :]
        oh = jnp.where(hit, 1.0, 0.0)
        cum = _bdot(oh, tri)
        rows.append(jnp.sum(jnp.where(hit, cum - 1.0 + base, 0.0), axis=0, keepdims=True))
        base = base + cum[:, tp - 1:tp]
    base_scr[...] = base
    dest_ref[...] = jnp.concatenate(rows, axis=0).astype(I32)


def moe_dest(eidx, pstart, tp):
    N = eidx.shape[1]
    return pl.pallas_call(
        _moe_dest_kernel,
        out_shape=jax.ShapeDtypeStruct((TOP_K, N), I32),
        grid=(N // tp,),
        in_specs=[pl.BlockSpec((TOP_K, tp), lambda i: (0, i)),
                  pl.BlockSpec((N_EXPERTS, LANE), lambda i: (0, 0))],
        out_specs=pl.BlockSpec((TOP_K, tp), lambda i: (0, i)),
        scratch_shapes=[pltpu.VMEM((N_EXPERTS, 1), F32)],
        compiler_params=_cp(("arbitrary",)),
        name="moe_dest",
    )(eidx, pstart)


def _tile_at(ref, token):
    return ref.at[pl.ds(pl.multiple_of(token * SUB, SUB), SUB), :]


def _moe_scatter_kernel(dest_ref, h_ref, xs_hbm, sem):
    tp = dest_ref.shape[1]

    def issue(j, carry):
        for k in range(TOP_K):
            pltpu.make_async_copy(_tile_at(h_ref, j), _tile_at(xs_hbm, dest_ref[k, j]), sem).start(priority=k % 2)
        return carry

    lax.fori_loop(0, tp, issue, 0)

    def drain(j, carry):
        for k in range(TOP_K):
            pltpu.make_async_copy(h_ref.at[pl.ds(0, SUB), :], xs_hbm.at[pl.ds(0, SUB), :], sem).wait()
        return carry

    lax.fori_loop(0, tp, drain, 0)


def moe_scatter(dest, h2t, n_slots, tp):
    N = dest.shape[1]
    return pl.pallas_call(
        _moe_scatter_kernel,
        out_shape=jax.ShapeDtypeStruct((n_slots * SUB, LANE), F32),
        grid=(N // tp,),
        in_specs=[pl.BlockSpec((TOP_K, tp), lambda i: (0, i), memory_space=pltpu.SMEM),
                  pl.BlockSpec((tp * SUB, LANE), lambda i: (i, 0))],
        out_specs=pl.BlockSpec(memory_space=pl.ANY),
        scratch_shapes=[pltpu.SemaphoreType.DMA(())],
        compiler_params=_cp(("arbitrary",)),
        name="moe_scatter",
    )(dest, h2t)


PAD_BITS = tuple(1 << b for b in reversed(range(MOE_BLK.bit_length() - 1)))


def _moe_padfill_kernel(ps_ref, xs_in, xs_hbm, zero_scr, sem):
    del xs_in
    zero_scr[...] = jnp.zeros_like(zero_scr)

    def pad_copies(e, wait):
        cnt = ps_ref[1, e]
        npad = ((cnt + (MOE_BLK - 1)) & (-MOE_BLK)) - cnt
        off = ps_ref[0, e] + cnt
        for bit in PAD_BITS:
            @pl.when((npad & bit) != 0)
            def _():
                cp = pltpu.make_async_copy(zero_scr.at[pl.ds(0, bit * SUB), :],
                                           xs_hbm.at[pl.ds(pl.multiple_of(off * SUB, SUB), bit * SUB), :], sem)
                if wait:
                    cp.wait()
                else:
                    cp.start()
            off = off + (npad & bit)

    def issue(e, carry):
        pad_copies(e, False)
        return carry

    def drain(e, carry):
        pad_copies(e, True)
        return carry

    lax.fori_loop(0, N_EXPERTS, issue, 0)
    lax.fori_loop(0, N_EXPERTS, drain, 0)


def moe_padfill(pstart, xs):
    return pl.pallas_call(
        _moe_padfill_kernel,
        out_shape=jax.ShapeDtypeStruct(xs.shape, xs.dtype),
        grid=(1,),
        in_specs=[pl.BlockSpec(memory_space=pltpu.SMEM),
                  pl.BlockSpec(memory_space=pl.ANY)],
        out_specs=pl.BlockSpec(memory_space=pl.ANY),
        scratch_shapes=[pltpu.VMEM((PAD_BITS[0] * SUB, LANE), F32), pltpu.SemaphoreType.DMA(())],
        input_output_aliases={1: 0},
        compiler_params=_cp(("arbitrary",)),
        name="moe_padfill",
    )(pstart, xs)


def _experts_kernel(blk_ref, xs_ref, w13_ref, w2_ref, y_ref, w13_bf, w2_bf):
    i = pl.program_id(0)
    nb = xs_ref.shape[0] // SUB
    ff = w2_bf.shape[0]

    @pl.when(i < blk_ref[SUB // 2, 0])
    def _():
        prev = blk_ref[0, jnp.maximum(i - 1, 0)]

        @pl.when((i == 0) | (blk_ref[0, i] != prev))
        def _():
            w13_bf[...] = w13_ref[0, 0].astype(BF16)
            w2_bf[...] = w2_ref[0, 0].astype(BF16)

        x = _tile_rows_load(xs_ref, nb).astype(BF16)
        h = jnp.dot(x, w13_bf[...], preferred_element_type=F32)
        act = (_silu(h[:, 0:ff]) * h[:, ff:2 * ff]).astype(BF16)
        _tile_rows_store(y_ref, jnp.dot(act, w2_bf[...], preferred_element_type=F32))


def moe_experts(blk, xs, w13, w2, layer, nb_total):
    _, E, D, F2 = w13.shape
    last = lambda i, b: jnp.minimum(i, b[SUB // 2, 0] - 1)
    grid_spec = pltpu.PrefetchScalarGridSpec(
        num_scalar_prefetch=1,
        grid=(nb_total,),
        in_specs=[pl.BlockSpec((MOE_BLK * SUB, LANE), lambda i, b: (last(i, b), 0)),
                  pl.BlockSpec((1, 1, D, F2), lambda i, b: (layer, b[0, last(i, b)], 0, 0)),
                  pl.BlockSpec((1, 1, F2 // 2, D), lambda i, b: (layer, b[0, last(i, b)], 0, 0))],
        out_specs=pl.BlockSpec((MOE_BLK * SUB, LANE), lambda i, b: (last(i, b), 0)),
        scratch_shapes=[pltpu.VMEM((D, F2), BF16), pltpu.VMEM((F2 // 2, D), BF16)])
    return pl.pallas_call(
        _experts_kernel,
        out_shape=jax.ShapeDtypeStruct(xs.shape, F32),
        grid_spec=grid_spec,
        compiler_params=_cp(("arbitrary",)),
        name="moe_experts",
    )(blk, xs, w13, w2)


def _moe_combine_kernel(dest_ref, wt_ref, h_ref, x_ref, mod_ref, s13_ref, s2_ref, l2g_ref, l2b_ref, y_hbm,
                        o_ref, g_scr, sem, *, alpha):
    tm = h_ref.shape[0]
    ff = s2_ref.shape[0]

    def issue(j, carry):
        for k in range(TOP_K):
            pltpu.make_async_copy(_tile_at(y_hbm, dest_ref[k, j]), _tile_at(g_scr.at[k], j), sem).start(priority=k % 2)
        return carry

    lax.fori_loop(0, tm, issue, 0)
    hs = _bdot(h_ref[...], s13_ref[...])
    f = _bdot(_silu(hs[:, 0:ff]) * hs[:, ff:2 * ff], s2_ref[...])

    def drain(j, carry):
        for k in range(TOP_K):
            pltpu.make_async_copy(y_hbm.at[pl.ds(0, SUB), :], g_scr.at[0, pl.ds(0, SUB), :], sem).wait()
        return carry

    lax.fori_loop(0, tm, drain, 0)
    for k in range(TOP_K):
        f = f + wt_ref[:, k:k + 1] * _tile_rows_load(g_scr.at[k], tm)
    o_ref[...] = (_layer_norm(alpha * x_ref[...] + mod_ref[0, 5:6, :] * f) * l2g_ref[...] + l2b_ref[...])


def moe_combine(dest, wt_t, h2, x1, mod3, wts, y, row_of_tile, tm, alpha):
    N, D = h2.shape
    full = lambda a: pl.BlockSpec(a.shape, lambda i: (0,) * a.ndim)
    kern = functools.partial(_moe_combine_kernel, alpha=alpha)
    return pl.pallas_call(
        kern,
        out_shape=jax.ShapeDtypeStruct((N, D), F32),
        grid=(N // tm,),
        in_specs=[pl.BlockSpec((TOP_K, tm), lambda i: (0, i), memory_space=pltpu.SMEM),
                  pl.BlockSpec((tm, TOP_K), lambda i: (i, 0)),
                  pl.BlockSpec((tm, D), lambda i: (i, 0)),
                  pl.BlockSpec((tm, D), lambda i: (i, 0)),
                  pl.BlockSpec((1, 6, D), lambda i: (row_of_tile(i), 0, 0))]
                 + [full(a) for a in wts] + [pl.BlockSpec(memory_space=pl.ANY)],
        out_specs=pl.BlockSpec((tm, D), lambda i: (i, 0)),
        scratch_shapes=[pltpu.VMEM((TOP_K, tm * SUB, LANE), F32), pltpu.SemaphoreType.DMA(())],
        compiler_params=_cp(("arbitrary",)),
        name="moe_combine",
    )(dest, wt_t, h2, x1, mod3, *wts, y)


def moe_ffn(h2, h2t, x1, mod3, mp, row_of_tile, tm, alpha):
    N = h2.shape[0]
    nb_total = (N * TOP_K + MOE_BLK - 1) // MOE_BLK + N_EXPERTS
    nbp = ((nb_total + LANE - 1) // LANE) * LANE
    eidx, wts = moe_router(h2, mp['rwt'], mp['bias'], tm)
    pstart, plan, blk = moe_counts(eidx, tm, nbp)
    dest = moe_dest(eidx, pstart, tm)
    xs = moe_scatter(dest, h2t, nb_total * MOE_BLK, tm)
    xs = moe_padfill(plan, xs)
    y = moe_experts(blk, xs, mp['w13'], mp['w2'], mp['layer'], nb_total)
    return moe_combine(dest, wts.T, h2, x1, mod3, mp['comb'], y, row_of_tile, tm, alpha)


def token_mixers(x2, B, T, W, mod3, row_of_tile, lp, states, tm):
    pa, pbg, gg, u = inproj(x2, mod3, lp['w_in'], row_of_tile, tm)
    pac = short_conv(pa.reshape(B, T, PA_W), lp['taps'], W).reshape(B * T, PA_W)
    ss, sd, gb = rwkv_prep(pac, lp['prep'], tm)
    s_rwkv, s_gla, s_s5 = states
    y2, f_rwkv = rwkv_scan(ss, sd, s_rwkv, B, T)
    o2, f_gla = gla_scan(pbg, lp['gla_aup'], lp['gla_ab'], s_gla, B, T)
    yd, f_s5 = s5_scan(u.reshape(B, T, S5_W), lp['s5_a'], lp['s5_bbd'], lp['s5_cbd'], s_s5)
    return (y2, gb, o2, gg, yd.reshape(2, B * T, S5_W), u), (f_rwkv, f_gla, f_s5)


def zero_states(B):
    return ((jnp.zeros((B, RW // LANE, LANE, LANE), F32),) * 2,
            (jnp.zeros((B, RW, GLA_KP), F32),) * 2,
            jnp.zeros((2, B, S5_LANES), F32))


def _inproj_columns():
    r_cols = 3 * RW + 2 * DECAY_RANK + 2 * ICLR_RANK + GATE_RANK
    kd = GLA_HEADS * GLA_K
    gq, gk, gv = r_cols, r_cols + kd, r_cols + 2 * kd
    gg = gv + RW
    gad = gg + RW
    pc = gad + 2 * GLA_RANK
    z = lambda n: [-1] * n
    cols = list(range(0, r_cols)) + z(PA_W - r_cols)
    cols += list(range(gq, gq + kd)) + z(256 - kd)
    cols += list(range(gk, gk + kd)) + z(256 - kd)
    cols += list(range(gv, gv + RW))
    cols += list(range(gad, gad + 2 * GLA_RANK)) + z(LANE - 2 * GLA_RANK)
    cols += list(range(gg, gg + RW))
    cols += list(range(pc, pc + S5_W))
    assert len(cols) == NP_W
    return np.asarray(cols, np.int32)


def _head_block_ones():
    h = np.arange(RW) // HEAD
    return jnp.asarray(h[:, None] == h[None, :], BF16)


def _layer_params(l, p):
    D = p['w_in'].shape[1]
    cols = _inproj_columns()
    w_in = jnp.concatenate([p['w_in'][l], jnp.zeros((D, 1), F32)], axis=1)
    w_in = jnp.take(w_in, jnp.asarray(np.where(cols < 0, w_in.shape[1] - 1, cols)), axis=1).astype(BF16)
    taps = p['rwkv_conv'][l].reshape(9, -1)
    taps = jnp.concatenate([taps, jnp.zeros((9, PA_W - taps.shape[1]), F32)], axis=1)
    wup = jnp.zeros((LANE, 2 * RW), F32)
    aup = jnp.zeros((LANE, 2 * RW), F32)
    for d in range(2):
        wup = wup.at[d * DECAY_RANK:(d + 1) * DECAY_RANK, d * RW:(d + 1) * RW].set(p['rwkv_w_up'][l, d])
        o = 2 * DECAY_RANK + d * ICLR_RANK
        aup = aup.at[o:o + ICLR_RANK, d * RW:(d + 1) * RW].set(p['rwkv_a_up'][l, d])
    gup = jnp.zeros((LANE, RW), F32).at[0:GATE_RANK].set(p['rwkv_g_up'][l])
    row = lambda a: a.reshape(1, -1)
    prep = (wup, aup, gup, row(p['rwkv_w0'][l]), row(p['rwkv_a0'][l]), row(p['rwkv_k_k'][l]),
            row(p['rwkv_k_a'][l]), row(p['rwkv_r_k'][l]), _head_block_ones())
    kd = GLA_HEADS * GLA_K
    gla_aup = jnp.zeros((2, LANE, GLA_KP), F32)
    for d in range(2):
        gla_aup = gla_aup.at[d, d * GLA_RANK:(d + 1) * GLA_RANK, 0:kd].set(p['gla_a_up'][l, d])
    gla_ab = jnp.zeros((2, 1, GLA_KP), F32).at[:, 0, 0:kd].set(p['gla_a_bias'][l])
    lam_re, lam_im = p['s5_lam_re'][l], p['s5_lam_im'][l]
    dt = jnp.exp(p['s5_log_dt'][l])[:, :, None]
    zr, zi = lam_re[:, None, :] * dt, lam_im[:, None, :] * dt
    mag = jnp.exp(zr)
    ab_r, ab_i = mag * jnp.cos(zi), mag * jnp.sin(zi)
    den = (lam_re * lam_re + lam_im * lam_im)[:, None, :]
    f_r = ((ab_r - 1) * lam_re[:, None, :] + ab_i * lam_im[:, None, :]) / den
    f_i = (ab_i * lam_re[:, None, :] - (ab_r - 1) * lam_im[:, None, :]) / den
    b_re, b_im = p['s5_b_re'][l], p['s5_b_im'][l]
    bb_r = f_r[..., None] * b_re - f_i[..., None] * b_im
    bb_i = f_r[..., None] * b_im + f_i[..., None] * b_re
    eye_g = jnp.eye(S5_GROUPS, dtype=F32)
    half = S5_LANES // 2

    def in_blockdiag(bb):
        return jnp.einsum('dgpc,gh->dgchp', bb, eye_g).reshape(2, S5_W, half)

    def out_blockdiag(cc):
        return jnp.einsum('dgcp,gh->dgphc', cc, eye_g).reshape(2, half, S5_W)

    s5_bbd = jnp.concatenate([in_blockdiag(bb_r), in_blockdiag(bb_i)], axis=2).astype(BF16)
    s5_cbd = jnp.concatenate([out_blockdiag(p['s5_c_re'][l]), -out_blockdiag(p['s5_c_im'][l])], axis=1).astype(BF16)
    s5_a = jnp.concatenate([ab_r.reshape(2, 1, half), ab_i.reshape(2, 1, half)], axis=2)
    mix = (_head_block_ones(), row(p['rwkv_ln_g'][l]), row(p['rwkv_ln_b'][l]),
           row(jnp.tile(p['gla_norm_g'][l], GLA_HEADS)), row(p['s5_d'][l]),
           p['s5_glu_w'][l].astype(BF16), row(p['s5_glu_b'][l]), p['w_out'][l].astype(BF16),
           row(p['ln1_g'][l]), row(p['ln1_b'][l]))
    moe = dict(rwt=p['router_w'][l].T,
               bias=jnp.broadcast_to(p['router_bias'][l][:, None], (N_EXPERTS, LANE)),
               w13=p['exp_w13'], w2=p['exp_w2'], layer=l,
               comb=(p['sh_w13'][l].astype(BF16), p['sh_w2'][l].astype(BF16), row(p['ln2_g'][l]), row(p['ln2_b'][l])))
    return dict(w_in=w_in, taps=taps, prep=prep, gla_aup=gla_aup, gla_ab=gla_ab,
                s5_a=s5_a, s5_bbd=s5_bbd, s5_cbd=s5_cbd, mix=mix, moe=moe)


_ARG_NAMES = ('x', 'c', 'ctx', 'c_ctx', 'w_mod', 'b_mod', 'w_in', 'rwkv_conv', 'rwkv_w0', 'rwkv_w_up', 'rwkv_a0',
              'rwkv_a_up', 'rwkv_g_up', 'rwkv_k_k', 'rwkv_k_a', 'rwkv_r_k', 'rwkv_ln_g', 'rwkv_ln_b', 'gla_a_up',
              'gla_a_bias', 'gla_norm_g', 's5_lam_re', 's5_lam_im', 's5_log_dt', 's5_b_re', 's5_b_im', 's5_c_re',
              's5_c_im', 's5_d', 's5_glu_w', 's5_glu_b', 'w_out', 'ln1_g', 'ln1_b', 'router_w', 'router_bias',
              'exp_w13', 'exp_w2', 'sh_w13', 'sh_w2', 'ln2_g', 'ln2_b')


def _tile(n, pref):
    t = pref
    while n % t:
        t //= 2
    return t


def kernel(x, c, ctx, c_ctx, w_mod, b_mod, w_in, rwkv_conv, rwkv_w0, rwkv_w_up, rwkv_a0, rwkv_a_up, rwkv_g_up,
           rwkv_k_k, rwkv_k_a, rwkv_r_k, rwkv_ln_g, rwkv_ln_b, gla_a_up, gla_a_bias, gla_norm_g, s5_lam_re,
           s5_lam_im, s5_log_dt, s5_b_re, s5_b_im, s5_c_re, s5_c_im, s5_d, s5_glu_w, s5_glu_b, w_out, ln1_g,
           ln1_b, router_w, router_bias, exp_w13, exp_w2, sh_w13, sh_w2, ln2_g, ln2_b):
    p = dict(zip(_ARG_NAMES, (x, c, ctx, c_ctx, w_mod, b_mod, w_in, rwkv_conv, rwkv_w0, rwkv_w_up, rwkv_a0,
                              rwkv_a_up, rwkv_g_up, rwkv_k_k, rwkv_k_a, rwkv_r_k, rwkv_ln_g, rwkv_ln_b, gla_a_up,
                              gla_a_bias, gla_norm_g, s5_lam_re, s5_lam_im, s5_log_dt, s5_b_re, s5_b_im, s5_c_re,
                              s5_c_im, s5_d, s5_glu_w, s5_glu_b, w_out, ln1_g, ln1_b, router_w, router_bias,
                              exp_w13, exp_w2, sh_w13, sh_w2, ln2_g, ln2_b)))
    B, T, D = x.shape
    TC = ctx.shape[1]
    L = w_mod.shape[0]
    alpha = (2 * L) ** 0.25
    n_lat, n_ctx = B * T, B * TC
    R = ((B + 1 + SUB - 1) // SUB) * SUB
    cc = jnp.zeros((R, D), F32).at[0:B].set(c).at[B].set(c_ctx)
    mod = mod_table(cc, w_mod, b_mod)
    tm = _tile(T, 256)
    tmc = _tile(n_ctx, 256)
    tmm = min(_tile(T, 128), _tile(n_ctx, 128))
    lat_row = lambda i: (i * tm) // T
    ctx_row = lambda i: B
    x2 = x.reshape(n_lat, D)
    c2 = ctx.reshape(n_ctx, D)
    for l in range(L):
        last = l == L - 1
        lp = _layer_params(l, p)
        mod3 = mod[l].reshape(R, 6, D)
        outs_c, st_c = token_mixers(c2, B, TC, TC, mod3, ctx_row, lp, zero_states(B), tmc)
        outs, _ = token_mixers(x2, B, T, GRID_W, mod3, lat_row, lp, st_c, tm)
        x1, h2, h2t = mix_out(*outs, x2, mod3, lp['mix'], lat_row, tm, alpha)
        if last:
            x2 = moe_ffn(h2, h2t, x1, mod3, lp['moe'], lambda i: (i * tmm) // T, tmm, alpha)
        else:
            c1, hc2, hc2t = mix_out(*outs_c, c2, mod3, lp['mix'], ctx_row, tmc, alpha)
            row_all = lambda i: jnp.where(i * tmm < n_ctx, B, (i * tmm - n_ctx) // T)
            out = moe_ffn(jnp.concatenate([hc2, h2]), jnp.concatenate([hc2t, h2t]), jnp.concatenate([c1, x1]),
                          mod3, lp['moe'], row_all, tmm, alpha)
            c2, x2 = out[:n_ctx], out[n_ctx:]
    return x2.reshape(B, T, D)
```

```python
import functools
import math

import numpy as np
import jax
import jax.numpy as jnp
from jax import lax
from jax.experimental import pallas as pl
from jax.experimental.pallas import tpu as pltpu

F32 = jnp.float32
BF16 = jnp.bfloat16
I32 = jnp.int32

GRID_W = 64
RWKV_HEADS = 6
HEAD = 64
RW = RWKV_HEADS * HEAD
DECAY_RANK = 32
ICLR_RANK = 32
GATE_RANK = 64
RWKV_GN_EPS = 64e-5
GLA_HEADS = 6
GLA_K = 32
GLA_RANK = 16
GLA_TEMP = 16.0
S5_GROUPS = 16
S5_GROUP = 16
S5_STATE = 64
S5_W = S5_GROUPS * S5_GROUP
S5_LANES = 2 * S5_GROUPS * S5_STATE
N_EXPERTS = 256
TOP_K = 8
N_EGROUPS = 8
TOPK_GROUPS = 4
ROUTE_SCALE = 2.5
LN_EPS = 1e-6
CHUNK = 64
MOE_BLK = 256
LANE = 128
SUB = 8
VMEM_LIMIT = 56 * 1024 * 1024

PA_W = 3 * RW + 2 * LANE
PBG_W = 256 + 256 + RW + LANE
NP_W = PA_W + PBG_W + RW + S5_W


def _cp(sem):
    return pltpu.CompilerParams(dimension_semantics=sem, vmem_limit_bytes=VMEM_LIMIT)


def _sigmoid(x):
    return 1.0 / (1.0 + jnp.exp(-x))


def _silu(x):
    return x * _sigmoid(x)


def _bdot(a, b, dims=None):
    a = a.astype(BF16)
    b = b.astype(BF16)
    if dims is None:
        return jnp.dot(a, b, preferred_element_type=F32)
    return lax.dot_general(a, b, dims, preferred_element_type=F32)


def _split2(x):
    hi = x.astype(BF16)
    lo = (x - hi.astype(F32)).astype(BF16)
    return hi, lo


def _split3(x):
    hi = x.astype(BF16)
    r = x - hi.astype(F32)
    mid = r.astype(BF16)
    lo = (r - mid.astype(F32)).astype(BF16)
    return hi, mid, lo


NT_DIMS = (((1,), (1,)), ((), ()))
TN_DIMS = (((0,), (0,)), ((), ()))


def _dot33(a, b, dims=None):
    ah, al = _split2(a)
    bh, bl = _split2(b)
    return _bdot(ah, bh, dims) + (_bdot(ah, bl, dims) + _bdot(al, bh, dims))


def _dot_exact_lhs(m_exact, x, dims=None):
    h, m, l = _split3(x)
    return _bdot(m_exact, h, dims) + (_bdot(m_exact, m, dims) + _bdot(m_exact, l, dims))


def _dot_exact_rhs(x, m_exact, dims=None):
    h, m, l = _split3(x)
    return _bdot(h, m_exact, dims) + (_bdot(m, m_exact, dims) + _bdot(l, m_exact, dims))


def _dot_split_rhs(x, m_exact, dims=None):
    h, l = _split2(x)
    return _bdot(h, m_exact, dims) + _bdot(l, m_exact, dims)


def _layer_norm(x):
    mu = jnp.mean(x, axis=-1, keepdims=True)
    xc = x - mu
    var = jnp.mean(xc * xc, axis=-1, keepdims=True)
    return xc * lax.rsqrt(var + LN_EPS)


def _mod_kernel(c_ref, w_ref, b_ref, o_ref):
    s = _silu(c_ref[...])
    o_ref[0] = _dot33(s, w_ref[0]) + b_ref[0]


def mod_table(cc, w_mod, b_mod):
    L, D, D6 = w_mod.shape
    R = cc.shape[0]
    tn = 1536
    return pl.pallas_call(
        _mod_kernel,
        out_shape=jax.ShapeDtypeStruct((L, R, D6), F32),
        grid=(L, D6 // tn),
        in_specs=[pl.BlockSpec((R, D), lambda l, j: (0, 0)),
                  pl.BlockSpec((1, D, tn), lambda l, j: (l, 0, j)),
                  pl.BlockSpec((1, 1, tn), lambda l, j: (l, 0, j))],
        out_specs=pl.BlockSpec((1, R, tn), lambda l, j: (l, 0, j)),
        compiler_params=_cp(("arbitrary", "arbitrary")),
        name="mod_table",
    )(cc, w_mod, b_mod.reshape(L, 1, D6))


def _inproj_kernel(x_ref, mod_ref, w_ref, pa_ref, pbg_ref, gg_ref, u_ref):
    x = x_ref[...]
    h = _layer_norm(x) * (1.0 + mod_ref[0, 1:2, :]) + mod_ref[0, 0:1, :]
    hb = h.astype(BF16)
    o = 0
    for ref in (pa_ref, pbg_ref, gg_ref, u_ref):
        w = ref.shape[-1]
        ref[...] = jnp.dot(hb, w_ref[:, o:o + w], preferred_element_type=F32)
        o += w


def inproj(x2, mod3, w_bf, row_of_tile, tm):
    N, D = x2.shape
    return pl.pallas_call(
        _inproj_kernel,
        out_shape=[jax.ShapeDtypeStruct((N, PA_W), F32),
                   jax.ShapeDtypeStruct((N, PBG_W), F32),
                   jax.ShapeDtypeStruct((N, RW), F32),
                   jax.ShapeDtypeStruct((N, S5_W), F32)],
        grid=(N // tm,),
        in_specs=[pl.BlockSpec((tm, D), lambda i: (i, 0)),
                  pl.BlockSpec((1, 6, D), lambda i: (row_of_tile(i), 0, 0)),
                  pl.BlockSpec((D, NP_W), lambda i: (0, 0))],
        out_specs=[pl.BlockSpec((tm, PA_W), lambda i: (i, 0)),
                   pl.BlockSpec((tm, PBG_W), lambda i: (i, 0)),
                   pl.BlockSpec((tm, RW), lambda i: (i, 0)),
                   pl.BlockSpec((tm, S5_W), lambda i: (i, 0))],
        compiler_params=_cp(("arbitrary",)),
        name="inproj",
    )(x2, mod3, w_bf)


CONV_PAD = 72


def _conv_kernel(x_ref, taps_ref, o_ref, buf_ref, *, T, W, vertical):
    pad = CONV_PAD
    zeros = jnp.zeros((pad, LANE), F32)
    buf_ref[0:pad, :] = zeros
    buf_ref[pad + T:pad + T + pad, :] = zeros
    buf_ref[pad:pad + T, :] = x_ref[0]
    ch = min(T, 256)
    col = lax.broadcasted_iota(I32, (ch, LANE), 0) & (W - 1)
    left_ok = col >= 1
    right_ok = col <= W - 2
    for c in range(T // ch):
        base = pad + c * ch
        acc = jnp.zeros((ch, LANE), F32)
        for dr in ((0, 1, 2) if vertical else (1,)):
            for dc in range(3):
                off = (dr - 1) * W + (dc - 1)
                v = buf_ref[base + off:base + off + ch, :]
                if dc == 0:
                    v = jnp.where(left_ok, v, 0.0)
                elif dc == 2:
                    v = jnp.where(right_ok, v, 0.0)
                acc = acc + v * taps_ref[3 * dr + dc:3 * dr + dc + 1, :]
        o_ref[0, c * ch:(c + 1) * ch, :] = acc


def short_conv(pa3, taps9, W):
    B, T, C = pa3.shape
    vertical = T > W
    assert W & (W - 1) == 0 and (not vertical or W + 1 <= CONV_PAD)
    kern = functools.partial(_conv_kernel, T=T, W=W, vertical=vertical)
    return pl.pallas_call(
        kern,
        out_shape=jax.ShapeDtypeStruct((B, T, C), F32),
        grid=(B, C // LANE),
        in_specs=[pl.BlockSpec((1, T, LANE), lambda b, j: (b, 0, j)),
                  pl.BlockSpec((9, LANE), lambda b, j: (0, j))],
        out_specs=pl.BlockSpec((1, T, LANE), lambda b, j: (b, 0, j)),
        scratch_shapes=[pltpu.VMEM((T + 2 * CONV_PAD, LANE), F32)],
        compiler_params=_cp(("arbitrary", "arbitrary")),
        name="short_conv",
    )(pa3, taps9)


def _rwkv_prep_kernel(pa_ref, wup_ref, aup_ref, gup_ref, w0_ref, a0_ref, kk_ref, ka_ref, rk_ref, hb_ref,
                      ss_ref, sd_ref, gb_ref):
    r = pa_ref[:, 0:RW]
    k = pa_ref[:, RW:2 * RW]
    v = pa_ref[:, 2 * RW:3 * RW]
    wa = pa_ref[:, 3 * RW:3 * RW + LANE]
    gd = pa_ref[:, 3 * RW + LANE:3 * RW + 2 * LANE]
    z = w0_ref[...] + _dot33(jnp.tanh(wa), wup_ref[...])
    lw = -_sigmoid(z) * math.exp(-0.5)
    a = _sigmoid(a0_ref[...] + _bdot(wa, aup_ref[...]))
    g = _bdot(_sigmoid(gd), gup_ref[...])
    hb = hb_ref[...]
    kk = k * kk_ref[...]
    kk = kk * lax.rsqrt(_dot_split_rhs(kk * kk, hb) + 1e-12)
    ka = ka_ref[...]
    ss_ref[:, 0:RW] = r
    ss_ref[:, RW:2 * RW] = v
    ss_ref[:, 2 * RW:3 * RW] = kk
    rk2 = jnp.zeros_like(r)
    for d in range(2):
        ad = a[:, d * RW:(d + 1) * RW]
        k2 = k * (1.0 + (ad - 1.0) * ka)
        sd_ref[d, :, 0:RW] = lw[:, d * RW:(d + 1) * RW]
        sd_ref[d, :, RW:2 * RW] = k2
        sd_ref[d, :, 2 * RW:3 * RW] = kk * ad
        rk2 = rk2 + r * k2
    bonus = _dot_split_rhs(rk2 * rk_ref[...], hb) * v
    gb_ref[:, 0:RW] = g
    gb_ref[:, RW:2 * RW] = bonus


def rwkv_prep(pa2, wts, tm):
    N = pa2.shape[0]
    full = lambda a: pl.BlockSpec(a.shape, lambda i: (0,) * a.ndim)
    return pl.pallas_call(
        _rwkv_prep_kernel,
        out_shape=[jax.ShapeDtypeStruct((N, 3 * RW), F32),
                   jax.ShapeDtypeStruct((2, N, 3 * RW), F32),
                   jax.ShapeDtypeStruct((N, 2 * RW), F32)],
        grid=(N // tm,),
        in_specs=[pl.BlockSpec((tm, PA_W), lambda i: (i, 0))] + [full(a) for a in wts],
        out_specs=[pl.BlockSpec((tm, 3 * RW), lambda i: (i, 0)),
                   pl.BlockSpec((2, tm, 3 * RW), lambda i: (0, i, 0)),
                   pl.BlockSpec((tm, 2 * RW), lambda i: (i, 0))],
        compiler_params=_cp(("arbitrary",)),
        name="rwkv_prep",
    )(pa2, *wts)


RWKV_GROUP = 8


def _rwkv_scan_kernel(ss_ref, sd_ref, s0_ref, y_ref, sfin_ref, s_scr, *, nsteps, group, reverse):
    n = pl.program_id(1)
    C = CHUNK
    P = 2 * C
    npair = RW // LANE

    @pl.when(n == 0)
    def _():
        s_scr[...] = s0_ref[0]

    row = lax.broadcasted_iota(I32, (P, P), 0)
    col = lax.broadcasted_iota(I32, (P, P), 1)
    same = (row >> 6) == (col >> 6)
    dlt = (col & (C - 1)) - (row & (C - 1)) if reverse else (row & (C - 1)) - (col & (C - 1))
    strict = same & (dlt > 0)
    incl = same & (dlt >= 0)
    eye = (row == col).astype(F32)
    lvl_masks = [((row >> (lvl + 1)) == (col >> (lvl + 1))) & ((row >> lvl) != (col >> lvl)) for lvl in range(6)]
    rc = lax.broadcasted_iota(I32, (C, C), 0)
    cc = lax.broadcasted_iota(I32, (C, C), 1)
    tri = jnp.where((cc >= rc) if reverse else (rc >= cc), 1.0, 0.0).astype(BF16)
    head0 = lax.broadcasted_iota(I32, (C, LANE), 1) < HEAD

    def stack(x):
        return jnp.concatenate([jnp.where(head0, x, 0.0), jnp.where(head0, 0.0, x)], axis=0)

    streams = [(g, p) for g in range(group) for p in range(npair)]
    tm_ = {}
    for (g, p) in streams:
        t0, t1 = g * C, (g + 1) * C
        lo, hi = p * LANE, (p + 1) * LANE
        lw = sd_ref[0, t0:t1, lo:hi]
        tm_[(g, p)] = dict(lw=lw, cl=_dot_exact_lhs(tri, lw))
    for (g, p) in streams:
        t = tm_[(g, p)]
        t0, t1 = g * C, (g + 1) * C
        lo, hi = p * LANE, (p + 1) * LANE
        r = ss_ref[t0:t1, lo:hi]
        v = ss_ref[t0:t1, RW + lo:RW + hi]
        kk = ss_ref[t0:t1, 2 * RW + lo:2 * RW + hi]
        k2 = sd_ref[0, t0:t1, RW + lo:RW + hi]
        b = sd_ref[0, t0:t1, 2 * RW + lo:2 * RW + hi]
        cl, lw = t['cl'], t['lw']
        t['ptot'] = jnp.exp(jnp.sum(lw, axis=0, keepdims=True))
        pinv = jnp.exp(-cl)
        left = jnp.concatenate([stack(-kk * jnp.exp(cl - lw)), stack(r * jnp.exp(cl))], axis=0)
        right = jnp.concatenate([stack(b * pinv), stack(k2 * pinv)], axis=0)
        t['v_st'] = stack(v)
        t['left'] = left.astype(BF16)
        t['bk'] = (right * t['ptot']).astype(BF16)
        aa = _bdot(left, right, NT_DIMS)
        t['nmat'] = jnp.where(strict, aa[0:P, 0:P], 0.0)
        t['a_ak'] = jnp.where(strict, aa[0:P, P:2 * P], 0.0)
        t['a_rbk'] = jnp.concatenate([jnp.where(incl, aa[P:2 * P, 0:P], 0.0),
                                      jnp.where(incl, aa[P:2 * P, P:2 * P], 0.0)], axis=1).astype(BF16)
        t['tinv'] = eye
    for sk in streams:
        t = tm_[sk]
        t['akv'] = _bdot(t['a_ak'], t['v_st'])
    for m in lvl_masks:
        for sk in streams:
            t = tm_[sk]
            t['et'] = _bdot(jnp.where(m, t['nmat'], 0.0), t['tinv'])
        for sk in streams:
            t = tm_[sk]
            t['tinv'] = t['tinv'] + _bdot(t['tinv'], t['et'])
    state = [s_scr[p] for p in range(npair)]
    pairs = range(npair)
    for g in (range(group - 1, -1, -1) if reverse else range(group)):
        ts = [tm_[(g, p)] for p in pairs]
        a_s = [_bdot(ts[p]['left'], state[p], NT_DIMS) for p in pairs]
        u = [_bdot(ts[p]['tinv'], a_s[p][0:P] + ts[p]['akv']) for p in pairs]
        uv = [jnp.concatenate([u[p], ts[p]['v_st']], axis=0) for p in pairs]
        state = [state[p] * ts[p]['ptot'] + _bdot(uv[p], ts[p]['bk'], TN_DIMS) for p in pairs]
        for p in pairs:
            y_st = a_s[p][P:2 * P] + _bdot(ts[p]['a_rbk'], uv[p])
            y_ref[g * C:(g + 1) * C, p * LANE:(p + 1) * LANE] = y_st[0:C] + y_st[C:P]
    for p in pairs:
        s_scr[p] = state[p]

    @pl.when(n == nsteps - 1)
    def _():
        sfin_ref[0] = s_scr[...]


def rwkv_scan_dir(ss, sd, s0, B, T, d):
    nc = T // CHUNK
    group = min(RWKV_GROUP, nc)
    nsteps = nc // group
    npair = RW // LANE
    rows = group * CHUNK
    blk = (lambda b, n: b * nsteps + nsteps - 1 - n) if d else (lambda b, n: b * nsteps + n)
    kern = functools.partial(_rwkv_scan_kernel, nsteps=nsteps, group=group, reverse=bool(d))
    return pl.pallas_call(
        kern,
        out_shape=[jax.ShapeDtypeStruct((B * T, RW), F32),
                   jax.ShapeDtypeStruct((B, npair, LANE, LANE), F32)],
        grid=(B, nsteps),
        in_specs=[pl.BlockSpec((rows, 3 * RW), lambda b, n: (blk(b, n), 0)),
                  pl.BlockSpec((1, rows, 3 * RW), lambda b, n: (d, blk(b, n), 0)),
                  pl.BlockSpec((1, npair, LANE, LANE), lambda b, n: (b, 0, 0, 0))],
        out_specs=[pl.BlockSpec((rows, RW), lambda b, n: (blk(b, n), 0)),
                   pl.BlockSpec((1, npair, LANE, LANE), lambda b, n: (b, 0, 0, 0))],
        scratch_shapes=[pltpu.VMEM((npair, LANE, LANE), F32)],
        compiler_params=_cp(("arbitrary", "arbitrary")),
        name="rwkv_scan_bwd" if d else "rwkv_scan_fwd",
    )(ss, sd, s0)


def rwkv_scan(ss, sd, s0, B, T):
    y0, f0 = rwkv_scan_dir(ss, sd, s0[0], B, T, 0)
    y1, f1 = rwkv_scan_dir(ss, sd, s0[1], B, T, 1)
    return (y0, y1), (f0, f1)


GLA_KP = 256


GLA_GROUP = 8


def _gla_kernel(pbg_ref, aup_ref, ab_ref, s0_ref, o_ref, sfin_ref, s_scr, *, nsteps, group, reverse):
    n = pl.program_id(1)
    C = CHUNK
    R = group * C

    @pl.when(n == 0)
    def _():
        s_scr[...] = s0_ref[0]

    q = pbg_ref[:, 0:GLA_KP]
    k = pbg_ref[:, GLA_KP:2 * GLA_KP]
    v = pbg_ref[:, 2 * GLA_KP:2 * GLA_KP + RW]
    ad = pbg_ref[:, 2 * GLA_KP + RW:2 * GLA_KP + RW + LANE]
    x = _dot33(ad, aup_ref[0]) + ab_ref[0]
    la = (jnp.minimum(x, 0.0) - jnp.log(1.0 + jnp.exp(-jnp.abs(x)))) * (1.0 / GLA_TEMP)
    chunks = range(group)
    sl = [slice(g * C, (g + 1) * C) for g in chunks]
    rr = lax.broadcasted_iota(I32, (C, C), 0)
    rc = lax.broadcasted_iota(I32, (C, C), 1)
    tri = jnp.where((rc >= rr) if reverse else (rr >= rc), 1.0, 0.0).astype(BF16)
    bcums = [_dot_exact_lhs(tri, la[sl[g]]) for g in chunks]
    last = 0 if reverse else C - 1
    tots = [bc[last:last + 1, :] for bc in bcums]
    bcum = jnp.concatenate(bcums, axis=0)
    tot = jnp.concatenate([jnp.broadcast_to(t, (C, GLA_KP)) for t in tots], axis=0)
    q_in = q * jnp.exp(bcum) * (GLA_K ** -0.5)
    k_in = k * jnp.exp(-bcum)
    k_st = k * jnp.exp(tot - bcum)
    dn = [jnp.exp(t) for t in tots]
    klane = lax.broadcasted_iota(I32, (C, GLA_KP), 1)
    rt = lax.broadcasted_iota(I32, (GLA_HEADS * C, C), 0) & (C - 1)
    ct = lax.broadcasted_iota(I32, (GLA_HEADS * C, C), 1)
    causal = (ct >= rt) if reverse else (rt >= ct)
    vlane = lax.broadcasted_iota(I32, (C, RW), 1)
    sv = lax.broadcasted_iota(I32, (RW, GLA_KP), 0) >> 6
    sk = lax.broadcasted_iota(I32, (RW, GLA_KP), 1) >> 5
    q_rows = [jnp.concatenate([jnp.where((klane >> 5) == h, q_in[sl[g]], 0.0) for h in range(GLA_HEADS)],
                              axis=0).astype(BF16) for g in chunks]
    att = [jnp.where(causal, _bdot(q_rows[g], k_in[sl[g]], NT_DIMS), 0.0) for g in chunks]
    o_rows = [_bdot(att[g], v[sl[g]]) for g in chunks]
    kv = [jnp.where(sv == sk, _bdot(v[sl[g]], k_st[sl[g]], TN_DIMS), 0.0) for g in chunks]
    s = s_scr[...]
    for g in (reversed(chunks) if reverse else chunks):
        o = _bdot(q_in[sl[g]], s, NT_DIMS)
        for h in range(GLA_HEADS):
            o = o + jnp.where((vlane >> 6) == h, o_rows[g][h * C:(h + 1) * C], 0.0)
        o_ref[sl[g], :] = o
        s = s * dn[g] + kv[g]
    s_scr[...] = s

    @pl.when(n == nsteps - 1)
    def _():
        sfin_ref[0] = s_scr[...]


def gla_scan_dir(pbg, aup, ab, s0, B, T, d):
    nc = T // CHUNK
    group = min(GLA_GROUP, nc)
    nsteps = nc // group
    rows = group * CHUNK
    blk = (lambda b, n: b * nsteps + nsteps - 1 - n) if d else (lambda b, n: b * nsteps + n)
    kern = functools.partial(_gla_kernel, nsteps=nsteps, group=group, reverse=bool(d))
    return pl.pallas_call(
        kern,
        out_shape=[jax.ShapeDtypeStruct((B * T, RW), F32),
                   jax.ShapeDtypeStruct((B, RW, GLA_KP), F32)],
        grid=(B, nsteps),
        in_specs=[pl.BlockSpec((rows, PBG_W), lambda b, n: (blk(b, n), 0)),
                  pl.BlockSpec((1, LANE, GLA_KP), lambda b, n: (d, 0, 0)),
                  pl.BlockSpec((1, 1, GLA_KP), lambda b, n: (d, 0, 0)),
                  pl.BlockSpec((1, RW, GLA_KP), lambda b, n: (b, 0, 0))],
        out_specs=[pl.BlockSpec((rows, RW), lambda b, n: (blk(b, n), 0)),
                   pl.BlockSpec((1, RW, GLA_KP), lambda b, n: (b, 0, 0))],
        scratch_shapes=[pltpu.VMEM((RW, GLA_KP), F32)],
        compiler_params=_cp(("arbitrary", "arbitrary")),
        name="gla_scan_bwd" if d else "gla_scan_fwd",
    )(pbg, aup, ab, s0)


def gla_scan(pbg, aup, ab, s0, B, T):
    o0, f0 = gla_scan_dir(pbg, aup, ab, s0[0], B, T, 0)
    o1, f1 = gla_scan_dir(pbg, aup, ab, s0[1], B, T, 1)
    return (o0, o1), (f0, f1)


S5_PITCH = 72


def _s5_kernel(u_ref, a_ref, bbd_ref, cbd_ref, s0_ref, y_ref, sfin_ref, x_scr, st_scr, *, nc, nb):
    d = pl.program_id(0)
    n = pl.program_id(1)
    C = CHUNK
    half = S5_LANES // 2

    @pl.when(n == 0)
    def _():
        st_scr[...] = s0_ref[0]

    nslab = S5_LANES // LANE
    bbd = bbd_ref[0]
    for b in range(nb):
        bu = _bdot(u_ref[b], bbd)
        for j in range(nslab):
            x_scr[j, b * S5_PITCH:b * S5_PITCH + C, :] = bu[:, j * LANE:(j + 1) * LANE]
    a_re = a_ref[0, :, 0:half]
    a_im = a_ref[0, :, half:S5_LANES]

    def step(i, carry):
        re, im = carry
        t = i + d * (C - 1 - 2 * i)
        rows = pl.ds(t, nb, stride=S5_PITCH)
        bu = jnp.concatenate([x_scr[j, rows, :] for j in range(nslab)], axis=1)
        nre = a_re * re - a_im * im + bu[:, 0:half]
        nim = a_re * im + a_im * re + bu[:, half:S5_LANES]
        for j in range(nslab // 2):
            x_scr[j, rows, :] = nre[:, j * LANE:(j + 1) * LANE]
            x_scr[nslab // 2 + j, rows, :] = nim[:, j * LANE:(j + 1) * LANE]
        return nre, nim

    st = st_scr[...]
    re, im = lax.fori_loop(0, C, step, (st[:, 0:half], st[:, half:S5_LANES]))
    st_scr[...] = jnp.concatenate([re, im], axis=1)
    cbd = cbd_ref[0]
    for b in range(nb):
        xs = jnp.concatenate([x_scr[j, b * S5_PITCH:b * S5_PITCH + C, :] for j in range(nslab)], axis=1)
        y_ref[0, b] = _bdot(xs, cbd)

    @pl.when(n == nc - 1)
    def _():
        sfin_ref[0] = st_scr[...]


def s5_scan(u3, a_bar, bbd, cbd, s0):
    B, T, _ = u3.shape
    nc = T // CHUNK
    chunk = lambda d, n: n + d * (nc - 1 - 2 * n)
    kern = functools.partial(_s5_kernel, nc=nc, nb=B)
    return pl.pallas_call(
        kern,
        out_shape=[jax.ShapeDtypeStruct((2, B, T, S5_W), F32),
                   jax.ShapeDtypeStruct((2, B, S5_LANES), F32)],
        grid=(2, nc),
        in_specs=[pl.BlockSpec((B, CHUNK, S5_W), lambda d, n: (0, chunk(d, n), 0)),
                  pl.BlockSpec((1, 1, S5_LANES), lambda d, n: (d, 0, 0)),
                  pl.BlockSpec((1, S5_W, S5_LANES), lambda d, n: (d, 0, 0)),
                  pl.BlockSpec((1, S5_LANES, S5_W), lambda d, n: (d, 0, 0)),
                  pl.BlockSpec((1, B, S5_LANES), lambda d, n: (d, 0, 0))],
        out_specs=[pl.BlockSpec((1, B, CHUNK, S5_W), lambda d, n: (d, 0, chunk(d, n), 0)),
                   pl.BlockSpec((1, B, S5_LANES), lambda d, n: (d, 0, 0))],
        scratch_shapes=[pltpu.VMEM((S5_LANES // LANE, B * S5_PITCH, LANE), F32),
                        pltpu.VMEM((B, S5_LANES), F32)],
        compiler_params=_cp(("arbitrary", "arbitrary")),
        name="s5_scan",
    )(u3, a_bar, bbd, cbd, s0)


def _tile_rows_store(ref, val):
    tm = val.shape[0]
    for s in range(val.shape[1] // LANE):
        ref[pl.ds(s, tm, stride=SUB), :] = val[:, s * LANE:(s + 1) * LANE]


def _tile_rows_load(ref, tm, base=0):
    return jnp.concatenate([ref[pl.ds(base + s, tm, stride=SUB), :] for s in range(SUB)], axis=1)


def _mix_out_kernel(y0_ref, y1_ref, gb_ref, o0_ref, o1_ref, gg_ref, yd_ref, u_ref, x_ref, mod_ref,
                    hb_ref, lng_ref, lnb_ref, gng_ref, s5d_ref, gluw_ref, glub_ref, wout_ref, l1g_ref, l1b_ref,
                    x1_ref, h2_ref, h2t_ref, *, alpha):
    hb = hb_ref[...]
    inv = 1.0 / HEAD
    y = y0_ref[...] + y1_ref[...]
    yc = y - _dot_split_rhs(y, hb) * inv
    var = _dot_split_rhs(yc * yc, hb) * inv
    gn = yc * lax.rsqrt(var + RWKV_GN_EPS) * lng_ref[...] + lnb_ref[...]
    y_a = (gn + gb_ref[:, RW:2 * RW]) * gb_ref[:, 0:RW]
    o = o0_ref[...] + o1_ref[...]
    o = o * lax.rsqrt(_dot_split_rhs(o * o, hb) * inv + 1e-6) * gng_ref[...]
    y_b = o * _silu(gg_ref[...])
    c = s5d_ref[...] * u_ref[...] + yd_ref[0] + yd_ref[1]
    c = 0.5 * c * (1.0 + jnp.tanh(math.sqrt(2.0 / math.pi) * (c + 0.044715 * (c * c * c))))
    y_c = c * _sigmoid(_bdot(c, gluw_ref[...]) + glub_ref[...])
    y_mix = (_bdot(y_a, wout_ref[0:RW, :]) + _bdot(y_b, wout_ref[RW:2 * RW, :])
             + _bdot(y_c, wout_ref[2 * RW:2 * RW + S5_W, :]))
    x1 = _layer_norm(alpha * x_ref[...] + mod_ref[0, 2:3, :] * y_mix) * l1g_ref[...] + l1b_ref[...]
    x1_ref[...] = x1
    h2 = _layer_norm(x1) * (1.0 + mod_ref[0, 4:5, :]) + mod_ref[0, 3:4, :]
    h2_ref[...] = h2
    _tile_rows_store(h2t_ref, h2)


def mix_out(y01, gb, o01, gg, yd, u, x2, mod3, wts, row_of_tile, tm, alpha):
    N, D = x2.shape
    full = lambda a: pl.BlockSpec(a.shape, lambda i: (0,) * a.ndim)
    kern = functools.partial(_mix_out_kernel, alpha=alpha)
    return pl.pallas_call(
        kern,
        out_shape=[jax.ShapeDtypeStruct((N, D), F32),
                   jax.ShapeDtypeStruct((N, D), F32),
                   jax.ShapeDtypeStruct((N * SUB, LANE), F32)],
        grid=(N // tm,),
        in_specs=[pl.BlockSpec((tm, RW), lambda i: (i, 0)),
                  pl.BlockSpec((tm, RW), lambda i: (i, 0)),
                  pl.BlockSpec((tm, 2 * RW), lambda i: (i, 0)),
                  pl.BlockSpec((tm, RW), lambda i: (i, 0)),
                  pl.BlockSpec((tm, RW), lambda i: (i, 0)),
                  pl.BlockSpec((tm, RW), lambda i: (i, 0)),
                  pl.BlockSpec((2, tm, S5_W), lambda i: (0, i, 0)),
                  pl.BlockSpec((tm, S5_W), lambda i: (i, 0)),
                  pl.BlockSpec((tm, D), lambda i: (i, 0)),
                  pl.BlockSpec((1, 6, D), lambda i: (row_of_tile(i), 0, 0))] + [full(a) for a in wts],
        out_specs=[pl.BlockSpec((tm, D), lambda i: (i, 0)),
                   pl.BlockSpec((tm, D), lambda i: (i, 0)),
                   pl.BlockSpec((tm * SUB, LANE), lambda i: (i, 0))],
        compiler_params=_cp(("arbitrary",)),
        name="mix_out",
    )(y01[0], y01[1], gb, o01[0], o01[1], gg, yd, u, x2, mod3, *wts)


def _first_max(x, idx, big):
    m = jnp.max(x, axis=0, keepdims=True)
    first = jnp.min(jnp.where(x == m, idx, big), axis=0, keepdims=True)
    return m, first


def _router_kernel(h_ref, rwt_ref, bias_ref, e_ref, w_ref):
    tm = h_ref.shape[0]
    gsz = N_EXPERTS // N_EGROUPS
    ninf = -jnp.inf
    s = _sigmoid(_dot33(rwt_ref[...], h_ref[...], NT_DIMS))
    ssel = s + bias_ref[:, 0:1]
    gi = lax.broadcasted_iota(I32, (gsz, tm), 0)
    gscore = []
    for g in range(N_EGROUPS):
        xg = ssel[g * gsz:(g + 1) * gsz, :]
        m1, i1 = _first_max(xg, gi, gsz)
        m2 = jnp.max(jnp.where(gi == i1, ninf, xg), axis=0, keepdims=True)
        gscore.append(m1 + m2)
    cur = jnp.concatenate(gscore, axis=0)
    gidx = lax.broadcasted_iota(I32, (N_EGROUPS, tm), 0)
    picked = jnp.zeros((N_EGROUPS, tm), F32)
    for _ in range(TOPK_GROUPS):
        _, first = _first_max(cur, gidx, N_EGROUPS)
        hit = gidx == first
        picked = jnp.where(hit, 1.0, picked)
        cur = jnp.where(hit, ninf, cur)
    x = jnp.concatenate(
        [jnp.where(picked[g:g + 1, :] > 0.5, ssel[g * gsz:(g + 1) * gsz, :], ninf) for g in range(N_EGROUPS)], axis=0)
    ei = lax.broadcasted_iota(I32, (N_EXPERTS, tm), 0)
    idxs, ws = [], []
    for _ in range(TOP_K):
        _, first = _first_max(x, ei, N_EXPERTS)
        hit = ei == first
        idxs.append(first)
        ws.append(jnp.sum(jnp.where(hit, s, 0.0), axis=0, keepdims=True))
        x = jnp.where(hit, ninf, x)
    w = jnp.concatenate(ws, axis=0)
    e_ref[...] = jnp.concatenate(idxs, axis=0)
    w_ref[...] = w / jnp.sum(w, axis=0, keepdims=True) * ROUTE_SCALE


def moe_router(h2, rwt, bias_b, tm):
    N, D = h2.shape
    return pl.pallas_call(
        _router_kernel,
        out_shape=[jax.ShapeDtypeStruct((TOP_K, N), I32), jax.ShapeDtypeStruct((TOP_K, N), F32)],
        grid=(N // tm,),
        in_specs=[pl.BlockSpec((tm, D), lambda i: (i, 0)),
                  pl.BlockSpec((N_EXPERTS, D), lambda i: (0, 0)),
                  pl.BlockSpec((N_EXPERTS, LANE), lambda i: (0, 0))],
        out_specs=[pl.BlockSpec((TOP_K, tm), lambda i: (0, i)),
                   pl.BlockSpec((TOP_K, tm), lambda i: (0, i))],
        compiler_params=_cp(("arbitrary",)),
        name="moe_router",
    )(h2, rwt, bias_b)


def _moe_count_kernel(e_ref, pstart_ref, plan_ref, blk_ref, cnt_scr, *, nt, nbp):
    i = pl.program_id(0)
    tp = e_ref.shape[1]

    @pl.when(i == 0)
    def _():
        cnt_scr[...] = jnp.zeros_like(cnt_scr)

    ei = lax.broadcasted_iota(I32, (N_EXPERTS, tp), 0)
    acc = jnp.zeros((N_EXPERTS, 1), F32)
    for k in range(TOP_K):
        acc = acc + jnp.sum(jnp.where(ei == e_ref[k:k + 1, :], 1.0, 0.0), axis=1, keepdims=True)
    cnt_scr[...] = cnt_scr[...] + acc

    @pl.when(i == nt - 1)
    def _():
        cnt = cnt_scr[...].astype(I32)
        padded = (cnt + (MOE_BLK - 1)) & (-MOE_BLK)
        r = lax.broadcasted_iota(I32, (N_EXPERTS, N_EXPERTS), 0)
        c = lax.broadcasted_iota(I32, (N_EXPERTS, N_EXPERTS), 1)
        tri = jnp.where(c <= r, 1.0, 0.0).astype(BF16)
        padded_b = jnp.broadcast_to(padded.astype(F32), (N_EXPERTS, LANE))
        p_end = _dot_exact_lhs(tri, padded_b)
        pstart = p_end - padded_b
        pstart_ref[...] = pstart.astype(I32)
        diag = r == c
        ps_row = jnp.sum(jnp.where(diag, pstart[:, 0:1], 0.0), axis=0, keepdims=True)
        cnt_row = jnp.sum(jnp.where(diag, cnt_scr[...], 0.0), axis=0, keepdims=True)
        plan_ref[...] = jnp.concatenate([ps_row, cnt_row, jnp.zeros((SUB - 2, N_EXPERTS), F32)], axis=0).astype(I32)
        lim =(lax.broadcasted_iota(I32, (N_EXPERTS, nbp), 1) * MOE_BLK).astype(F32)
        be = jnp.sum(jnp.where(p_end[:, 0:1] <= lim, 1.0, 0.0), axis=0, keepdims=True)
        be = jnp.minimum(be, N_EXPERTS - 1.0)
        nused = jnp.max(p_end[:, 0:1], axis=0, keepdims=True) * (1.0 / MOE_BLK)
        blk_ref[...] = jnp.concatenate([jnp.broadcast_to(be, (SUB // 2, nbp)),
                                        jnp.broadcast_to(nused, (SUB // 2, nbp))], axis=0).astype(I32)


def moe_counts(eidx, tp, nbp):
    N = eidx.shape[1]
    nt = N // tp
    kern = functools.partial(_moe_count_kernel, nt=nt, nbp=nbp)
    return pl.pallas_call(
        kern,
        out_shape=[jax.ShapeDtypeStruct((N_EXPERTS, LANE), I32),
                   jax.ShapeDtypeStruct((SUB, N_EXPERTS), I32),
                   jax.ShapeDtypeStruct((SUB, nbp), I32)],
        grid=(nt,),
        in_specs=[pl.BlockSpec((TOP_K, tp), lambda i: (0, i))],
        out_specs=[pl.BlockSpec((N_EXPERTS, LANE), lambda i: (0, 0)),
                   pl.BlockSpec((SUB, N_EXPERTS), lambda i: (0, 0)),
                   pl.BlockSpec((SUB, nbp), lambda i: (0, 0))],
        scratch_shapes=[pltpu.VMEM((N_EXPERTS, 1), F32)],
        compiler_params=_cp(("arbitrary",)),
        name="moe_counts",
    )(eidx)


def _moe_dest_kernel(e_ref, pstart_ref, dest_ref, base_scr):
    i = pl.program_id(0)
    tp = e_ref.shape[1]

    @pl.when(i == 0)
    def _():
        base_scr[...] = pstart_ref[:, 0:1].astype(F32)

    ei = lax.broadcasted_iota(I32, (N_EXPERTS, tp), 0)
    r = lax.broadcasted_iota(I32, (tp, tp), 0)
    c = lax.broadcasted_iota(I32, (tp, tp), 1)
    tri = jnp.where(r <= c, 1.0, 0.0).astype(BF16)
    base = base_scr[...]
    rows = []
    for k in range(TOP_K):
        hit = ei == e_ref[k:k + 1, :]
        oh = jnp.where(hit, 1.0, 0.0)
        cum = _bdot(oh, tri)
        rows.append(jnp.sum(jnp.where(hit, cum - 1.0 + base, 0.0), axis=0, keepdims=True))
        base = base + cum[:, tp - 1:tp]
    base_scr[...] = base
    dest_ref[...] = jnp.concatenate(rows, axis=0).astype(I32)


def moe_dest(eidx, pstart, tp):
    N = eidx.shape[1]
    return pl.pallas_call(
        _moe_dest_kernel,
        out_shape=jax.ShapeDtypeStruct((TOP_K, N), I32),
        grid=(N // tp,),
        in_specs=[pl.BlockSpec((TOP_K, tp), lambda i: (0, i)),
                  pl.BlockSpec((N_EXPERTS, LANE), lambda i: (0, 0))],
        out_specs=pl.BlockSpec((TOP_K, tp), lambda i: (0, i)),
        scratch_shapes=[pltpu.VMEM((N_EXPERTS, 1), F32)],
        compiler_params=_cp(("arbitrary",)),
        name="moe_dest",
    )(eidx, pstart)


def _tile_at(ref, token):
    return ref.at[pl.ds(pl.multiple_of(token * SUB, SUB), SUB), :]


def _moe_scatter_kernel(dest_ref, h_ref, xs_hbm, sem):
    tp = dest_ref.shape[1]

    def issue(j, carry):
        for k in range(TOP_K):
            pltpu.make_async_copy(_tile_at(h_ref, j), _tile_at(xs_hbm, dest_ref[k, j]), sem).start(priority=k % 2)
        return carry

    lax.fori_loop(0, tp, issue, 0)

    def drain(j, carry):
        for k in range(TOP_K):
            pltpu.make_async_copy(h_ref.at[pl.ds(0, SUB), :], xs_hbm.at[pl.ds(0, SUB), :], sem).wait()
        return carry

    lax.fori_loop(0, tp, drain, 0)


def moe_scatter(dest, h2t, n_slots, tp):
    N = dest.shape[1]
    return pl.pallas_call(
        _moe_scatter_kernel,
        out_shape=jax.ShapeDtypeStruct((n_slots * SUB, LANE), F32),
        grid=(N // tp,),
        in_specs=[pl.BlockSpec((TOP_K, tp), lambda i: (0, i), memory_space=pltpu.SMEM),
                  pl.BlockSpec((tp * SUB, LANE), lambda i: (i, 0))],
        out_specs=pl.BlockSpec(memory_space=pl.ANY),
        scratch_shapes=[pltpu.SemaphoreType.DMA(())],
        compiler_params=_cp(("arbitrary",)),
        name="moe_scatter",
    )(dest, h2t)


PAD_BITS = tuple(1 << b for b in reversed(range(MOE_BLK.bit_length() - 1)))


def _moe_padfill_kernel(ps_ref, xs_in, xs_hbm, zero_scr, sem):
    del xs_in
    zero_scr[...] = jnp.zeros_like(zero_scr)

    def pad_copies(e, wait):
        cnt = ps_ref[1, e]
        npad = ((cnt + (MOE_BLK - 1)) & (-MOE_BLK)) - cnt
        off = ps_ref[0, e] + cnt
        for bit in PAD_BITS:
            @pl.when((npad & bit) != 0)
            def _():
                cp = pltpu.make_async_copy(zero_scr.at[pl.ds(0, bit * SUB), :],
                                           xs_hbm.at[pl.ds(pl.multiple_of(off * SUB, SUB), bit * SUB), :], sem)
                if wait:
                    cp.wait()
                else:
                    cp.start()
            off = off + (npad & bit)

    def issue(e, carry):
        pad_copies(e, False)
        return carry

    def drain(e, carry):
        pad_copies(e, True)
        return carry

    lax.fori_loop(0, N_EXPERTS, issue, 0)
    lax.fori_loop(0, N_EXPERTS, drain, 0)


def moe_padfill(pstart, xs):
    return pl.pallas_call(
        _moe_padfill_kernel,
        out_shape=jax.ShapeDtypeStruct(xs.shape, xs.dtype),
        grid=(1,),
        in_specs=[pl.BlockSpec(memory_space=pltpu.SMEM),
                  pl.BlockSpec(memory_space=pl.ANY)],
        out_specs=pl.BlockSpec(memory_space=pl.ANY),
        scratch_shapes=[pltpu.VMEM((PAD_BITS[0] * SUB, LANE), F32), pltpu.SemaphoreType.DMA(())],
        input_output_aliases={1: 0},
        compiler_params=_cp(("arbitrary",)),
        name="moe_padfill",
    )(pstart, xs)


def _experts_kernel(blk_ref, xs_ref, w13_ref, w2_ref, y_ref, w13_bf, w2_bf):
    i = pl.program_id(0)
    nb = xs_ref.shape[0] // SUB
    ff = w2_bf.shape[0]

    @pl.when(i < blk_ref[SUB // 2, 0])
    def _():
        prev = blk_ref[0, jnp.maximum(i - 1, 0)]

        @pl.when((i == 0) | (blk_ref[0, i] != prev))
        def _():
            w13_bf[...] = w13_ref[0, 0].astype(BF16)
            w2_bf[...] = w2_ref[0, 0].astype(BF16)

        x = _tile_rows_load(xs_ref, nb).astype(BF16)
        h = jnp.dot(x, w13_bf[...], preferred_element_type=F32)
        act = (_silu(h[:, 0:ff]) * h[:, ff:2 * ff]).astype(BF16)
        _tile_rows_store(y_ref, jnp.dot(act, w2_bf[...], preferred_element_type=F32))


def moe_experts(blk, xs, w13, w2, layer, nb_total):
    _, E, D, F2 = w13.shape
    last = lambda i, b: jnp.minimum(i, b[SUB // 2, 0] - 1)
    grid_spec = pltpu.PrefetchScalarGridSpec(
        num_scalar_prefetch=1,
        grid=(nb_total,),
        in_specs=[pl.BlockSpec((MOE_BLK * SUB, LANE), lambda i, b: (last(i, b), 0)),
                  pl.BlockSpec((1, 1, D, F2), lambda i, b: (layer, b[0, last(i, b)], 0, 0)),
                  pl.BlockSpec((1, 1, F2 // 2, D), lambda i, b: (layer, b[0, last(i, b)], 0, 0))],
        out_specs=pl.BlockSpec((MOE_BLK * SUB, LANE), lambda i, b: (last(i, b), 0)),
        scratch_shapes=[pltpu.VMEM((D, F2), BF16), pltpu.VMEM((F2 // 2, D), BF16)])
    return pl.pallas_call(
        _experts_kernel,
        out_shape=jax.ShapeDtypeStruct(xs.shape, F32),
        grid_spec=grid_spec,
        compiler_params=_cp(("arbitrary",)),
        name="moe_experts",
    )(blk, xs, w13, w2)


def _moe_combine_kernel(dest_ref, wt_ref, h_ref, x_ref, mod_ref, s13_ref, s2_ref, l2g_ref, l2b_ref, y_hbm,
                        o_ref, g_scr, sem, *, alpha):
    tm = h_ref.shape[0]
    ff = s2_ref.shape[0]

    def issue(j, carry):
        for k in range(TOP_K):
            pltpu.make_async_copy(_tile_at(y_hbm, dest_ref[k, j]), _tile_at(g_scr.at[k], j), sem).start(priority=k % 2)
        return carry

    lax.fori_loop(0, tm, issue, 0)
    hs = _bdot(h_ref[...], s13_ref[...])
    f = _bdot(_silu(hs[:, 0:ff]) * hs[:, ff:2 * ff], s2_ref[...])

    def drain(j, carry):
        for k in range(TOP_K):
            pltpu.make_async_copy(y_hbm.at[pl.ds(0, SUB), :], g_scr.at[0, pl.ds(0, SUB), :], sem).wait()
        return carry

    lax.fori_loop(0, tm, drain, 0)
    for k in range(TOP_K):
        f = f + wt_ref[:, k:k + 1] * _tile_rows_load(g_scr.at[k], tm)
    o_ref[...] = (_layer_norm(alpha * x_ref[...] + mod_ref[0, 5:6, :] * f) * l2g_ref[...] + l2b_ref[...])


def moe_combine(dest, wt_t, h2, x1, mod3, wts, y, row_of_tile, tm, alpha):
    N, D = h2.shape
    full = lambda a: pl.BlockSpec(a.shape, lambda i: (0,) * a.ndim)
    kern = functools.partial(_moe_combine_kernel, alpha=alpha)
    return pl.pallas_call(
        kern,
        out_shape=jax.ShapeDtypeStruct((N, D), F32),
        grid=(N // tm,),
        in_specs=[pl.BlockSpec((TOP_K, tm), lambda i: (0, i), memory_space=pltpu.SMEM),
                  pl.BlockSpec((tm, TOP_K), lambda i: (i, 0)),
                  pl.BlockSpec((tm, D), lambda i: (i, 0)),
                  pl.BlockSpec((tm, D), lambda i: (i, 0)),
                  pl.BlockSpec((1, 6, D), lambda i: (row_of_tile(i), 0, 0))]
                 + [full(a) for a in wts] + [pl.BlockSpec(memory_space=pl.ANY)],
        out_specs=pl.BlockSpec((tm, D), lambda i: (i, 0)),
        scratch_shapes=[pltpu.VMEM((TOP_K, tm * SUB, LANE), F32), pltpu.SemaphoreType.DMA(())],
        compiler_params=_cp(("arbitrary",)),
        name="moe_combine",
    )(dest, wt_t, h2, x1, mod3, *wts, y)


def moe_ffn(h2, h2t, x1, mod3, mp, row_of_tile, tm, alpha):
    N = h2.shape[0]
    nb_total = (N * TOP_K + MOE_BLK - 1) // MOE_BLK + N_EXPERTS
    nbp = ((nb_total + LANE - 1) // LANE) * LANE
    eidx, wts = moe_router(h2, mp['rwt'], mp['bias'], tm)
    pstart, plan, blk = moe_counts(eidx, tm, nbp)
    dest = moe_dest(eidx, pstart, tm)
    xs = moe_scatter(dest, h2t, nb_total * MOE_BLK, tm)
    xs = moe_padfill(plan, xs)
    y = moe_experts(blk, xs, mp['w13'], mp['w2'], mp['layer'], nb_total)
    return moe_combine(dest, wts.T, h2, x1, mod3, mp['comb'], y, row_of_tile, tm, alpha)


def token_mixers(x2, B, T, W, mod3, row_of_tile, lp, states, tm):
    pa, pbg, gg, u = inproj(x2, mod3, lp['w_in'], row_of_tile, tm)
    pac = short_conv(pa.reshape(B, T, PA_W), lp['taps'], W).reshape(B * T, PA_W)
    ss, sd, gb = rwkv_prep(pac, lp['prep'], tm)
    s_rwkv, s_gla, s_s5 = states
    y2, f_rwkv = rwkv_scan(ss, sd, s_rwkv, B, T)
    o2, f_gla = gla_scan(pbg, lp['gla_aup'], lp['gla_ab'], s_gla, B, T)
    yd, f_s5 = s5_scan(u.reshape(B, T, S5_W), lp['s5_a'], lp['s5_bbd'], lp['s5_cbd'], s_s5)
    return (y2, gb, o2, gg, yd.reshape(2, B * T, S5_W), u), (f_rwkv, f_gla, f_s5)


def zero_states(B):
    return ((jnp.zeros((B, RW // LANE, LANE, LANE), F32),) * 2,
            (jnp.zeros((B, RW, GLA_KP), F32),) * 2,
            jnp.zeros((2, B, S5_LANES), F32))


def _inproj_columns():
    r_cols = 3 * RW + 2 * DECAY_RANK + 2 * ICLR_RANK + GATE_RANK
    kd = GLA_HEADS * GLA_K
    gq, gk, gv = r_cols, r_cols + kd, r_cols + 2 * kd
    gg = gv + RW
    gad = gg + RW
    pc = gad + 2 * GLA_RANK
    z = lambda n: [-1] * n
    cols = list(range(0, r_cols)) + z(PA_W - r_cols)
    cols += list(range(gq, gq + kd)) + z(256 - kd)
    cols += list(range(gk, gk + kd)) + z(256 - kd)
    cols += list(range(gv, gv + RW))
    cols += list(range(gad, gad + 2 * GLA_RANK)) + z(LANE - 2 * GLA_RANK)
    cols += list(range(gg, gg + RW))
    cols += list(range(pc, pc + S5_W))
    assert len(cols) == NP_W
    return np.asarray(cols, np.int32)


def _head_block_ones():
    h = np.arange(RW) // HEAD
    return jnp.asarray(h[:, None] == h[None, :], BF16)


def _layer_params(l, p):
    D = p['w_in'].shape[1]
    cols = _inproj_columns()
    w_in = jnp.concatenate([p['w_in'][l], jnp.zeros((D, 1), F32)], axis=1)
    w_in = jnp.take(w_in, jnp.asarray(np.where(cols < 0, w_in.shape[1] - 1, cols)), axis=1).astype(BF16)
    taps = p['rwkv_conv'][l].reshape(9, -1)
    taps = jnp.concatenate([taps, jnp.zeros((9, PA_W - taps.shape[1]), F32)], axis=1)
    wup = jnp.zeros((LANE, 2 * RW), F32)
    aup = jnp.zeros((LANE, 2 * RW), F32)
    for d in range(2):
        wup = wup.at[d * DECAY_RANK:(d + 1) * DECAY_RANK, d * RW:(d + 1) * RW].set(p['rwkv_w_up'][l, d])
        o = 2 * DECAY_RANK + d * ICLR_RANK
        aup = aup.at[o:o + ICLR_RANK, d * RW:(d + 1) * RW].set(p['rwkv_a_up'][l, d])
    gup = jnp.zeros((LANE, RW), F32).at[0:GATE_RANK].set(p['rwkv_g_up'][l])
    row = lambda a: a.reshape(1, -1)
    prep = (wup, aup, gup, row(p['rwkv_w0'][l]), row(p['rwkv_a0'][l]), row(p['rwkv_k_k'][l]),
            row(p['rwkv_k_a'][l]), row(p['rwkv_r_k'][l]), _head_block_ones())
    kd = GLA_HEADS * GLA_K
    gla_aup = jnp.zeros((2, LANE, GLA_KP), F32)
    for d in range(2):
        gla_aup = gla_aup.at[d, d * GLA_RANK:(d + 1) * GLA_RANK, 0:kd].set(p['gla_a_up'][l, d])
    gla_ab = jnp.zeros((2, 1, GLA_KP), F32).at[:, 0, 0:kd].set(p['gla_a_bias'][l])
    lam_re, lam_im = p['s5_lam_re'][l], p['s5_lam_im'][l]
    dt = jnp.exp(p['s5_log_dt'][l])[:, :, None]
    zr, zi = lam_re[:, None, :] * dt, lam_im[:, None, :] * dt
    mag = jnp.exp(zr)
    ab_r, ab_i = mag * jnp.cos(zi), mag * jnp.sin(zi)
    den = (lam_re * lam_re + lam_im * lam_im)[:, None, :]
    f_r = ((ab_r - 1) * lam_re[:, None, :] + ab_i * lam_im[:, None, :]) / den
    f_i = (ab_i * lam_re[:, None, :] - (ab_r - 1) * lam_im[:, None, :]) / den
    b_re, b_im = p['s5_b_re'][l], p['s5_b_im'][l]
    bb_r = f_r[..., None] * b_re - f_i[..., None] * b_im
    bb_i = f_r[..., None] * b_im + f_i[..., None] * b_re
    eye_g = jnp.eye(S5_GROUPS, dtype=F32)
    half = S5_LANES // 2

    def in_blockdiag(bb):
        return jnp.einsum('dgpc,gh->dgchp', bb, eye_g).reshape(2, S5_W, half)

    def out_blockdiag(cc):
        return jnp.einsum('dgcp,gh->dgphc', cc, eye_g).reshape(2, half, S5_W)

    s5_bbd = jnp.concatenate([in_blockdiag(bb_r), in_blockdiag(bb_i)], axis=2).astype(BF16)
    s5_cbd = jnp.concatenate([out_blockdiag(p['s5_c_re'][l]), -out_blockdiag(p['s5_c_im'][l])], axis=1).astype(BF16)
    s5_a = jnp.concatenate([ab_r.reshape(2, 1, half), ab_i.reshape(2, 1, half)], axis=2)
    mix = (_head_block_ones(), row(p['rwkv_ln_g'][l]), row(p['rwkv_ln_b'][l]),
           row(jnp.tile(p['gla_norm_g'][l], GLA_HEADS)), row(p['s5_d'][l]),
           p['s5_glu_w'][l].astype(BF16), row(p['s5_glu_b'][l]), p['w_out'][l].astype(BF16),
           row(p['ln1_g'][l]), row(p['ln1_b'][l]))
    moe = dict(rwt=p['router_w'][l].T,
               bias=jnp.broadcast_to(p['router_bias'][l][:, None], (N_EXPERTS, LANE)),
               w13=p['exp_w13'], w2=p['exp_w2'], layer=l,
               comb=(p['sh_w13'][l].astype(BF16), p['sh_w2'][l].astype(BF16), row(p['ln2_g'][l]), row(p['ln2_b'][l])))
    return dict(w_in=w_in, taps=taps, prep=prep, gla_aup=gla_aup, gla_ab=gla_ab,
                s5_a=s5_a, s5_bbd=s5_bbd, s5_cbd=s5_cbd, mix=mix, moe=moe)


_ARG_NAMES = ('x', 'c', 'ctx', 'c_ctx', 'w_mod', 'b_mod', 'w_in', 'rwkv_conv', 'rwkv_w0', 'rwkv_w_up', 'rwkv_a0',
              'rwkv_a_up', 'rwkv_g_up', 'rwkv_k_k', 'rwkv_k_a', 'rwkv_r_k', 'rwkv_ln_g', 'rwkv_ln_b', 'gla_a_up',
              'gla_a_bias', 'gla_norm_g', 's5_lam_re', 's5_lam_im', 's5_log_dt', 's5_b_re', 's5_b_im', 's5_c_re',
              's5_c_im', 's5_d', 's5_glu_w', 's5_glu_b', 'w_out', 'ln1_g', 'ln1_b', 'router_w', 'router_bias',
              'exp_w13', 'exp_w2', 'sh_w13', 'sh_w2', 'ln2_g', 'ln2_b')


def _tile(n, pref):
    t = pref
    while n % t:
        t //= 2
    return t


def kernel(x, c, ctx, c_ctx, w_mod, b_mod, w_in, rwkv_conv, rwkv_w0, rwkv_w_up, rwkv_a0, rwkv_a_up, rwkv_g_up,
           rwkv_k_k, rwkv_k_a, rwkv_r_k, rwkv_ln_g, rwkv_ln_b, gla_a_up, gla_a_bias, gla_norm_g, s5_lam_re,
           s5_lam_im, s5_log_dt, s5_b_re, s5_b_im, s5_c_re, s5_c_im, s5_d, s5_glu_w, s5_glu_b, w_out, ln1_g,
           ln1_b, router_w, router_bias, exp_w13, exp_w2, sh_w13, sh_w2, ln2_g, ln2_b):
    p = dict(zip(_ARG_NAMES, (x, c, ctx, c_ctx, w_mod, b_mod, w_in, rwkv_conv, rwkv_w0, rwkv_w_up, rwkv_a0,
                              rwkv_a_up, rwkv_g_up, rwkv_k_k, rwkv_k_a, rwkv_r_k, rwkv_ln_g, rwkv_ln_b, gla_a_up,
                              gla_a_bias, gla_norm_g, s5_lam_re, s5_lam_im, s5_log_dt, s5_b_re, s5_b_im, s5_c_re,
                              s5_c_im, s5_d, s5_glu_w, s5_glu_b, w_out, ln1_g, ln1_b, router_w, router_bias,
                              exp_w13, exp_w2, sh_w13, sh_w2, ln2_g, ln2_b)))
    B, T, D = x.shape
    TC = ctx.shape[1]
    L = w_mod.shape[0]
    alpha = (2 * L) ** 0.25
    n_lat, n_ctx = B * T, B * TC
    R = ((B + 1 + SUB - 1) // SUB) * SUB
    cc = jnp.zeros((R, D), F32).at[0:B].set(c).at[B].set(c_ctx)
    mod = mod_table(cc, w_mod, b_mod)
    tm = _tile(T, 256)
    tmc = _tile(n_ctx, 256)
    tmm = min(_tile(T, 128), _tile(n_ctx, 128))
    lat_row = lambda i: (i * tm) // T
    ctx_row = lambda i: B
    x2 = x.reshape(n_lat, D)
    c2 = ctx.reshape(n_ctx, D)
    for l in range(L):
        last = l == L - 1
        lp = _layer_params(l, p)
        mod3 = mod[l].reshape(R, 6, D)
        outs_c, st_c = token_mixers(c2, B, TC, TC, mod3, ctx_row, lp, zero_states(B), tmc)
        outs, _ = token_mixers(x2, B, T, GRID_W, mod3, lat_row, lp, st_c, tm)
        x1, h2, h2t = mix_out(*outs, x2, mod3, lp['mix'], lat_row, tm, alpha)
        if last:
            x2 = moe_ffn(h2, h2t, x1, mod3, lp['moe'], lambda i: (i * tmm) // T, tmm, alpha)
        else:
            c1, hc2, hc2t = mix_out(*outs_c, c2, mod3, lp['mix'], ctx_row, tmc, alpha)
            row_all = lambda i: jnp.where(i * tmm < n_ctx, B, (i * tmm - n_ctx) // T)
            out = moe_ffn(jnp.concatenate([hc2, h2]), jnp.concatenate([hc2t, h2t]), jnp.concatenate([c1, x1]),
                          mod3, lp['moe'], row_all, tmm, alpha)
            c2, x2 = out[:n_ctx], out[n_ctx:]
    return x2.reshape(B, T, D)
```

```python
import functools
import math

import numpy as np
import jax
import jax.numpy as jnp
from jax import lax
from jax.experimental import pallas as pl
from jax.experimental.pallas import tpu as pltpu

F32 = jnp.float32
BF16 = jnp.bfloat16
I32 = jnp.int32

GRID_W = 64
RWKV_HEADS = 6
HEAD = 64
RW = RWKV_HEADS * HEAD
DECAY_RANK = 32
ICLR_RANK = 32
GATE_RANK = 64
RWKV_GN_EPS = 64e-5
GLA_HEADS = 6
GLA_K = 32
GLA_RANK = 16
GLA_TEMP = 16.0
S5_GROUPS = 16
S5_GROUP = 16
S5_STATE = 64
S5_W = S5_GROUPS * S5_GROUP
S5_LANES = 2 * S5_GROUPS * S5_STATE
N_EXPERTS = 256
TOP_K = 8
N_EGROUPS = 8
TOPK_GROUPS = 4
ROUTE_SCALE = 2.5
LN_EPS = 1e-6
CHUNK = 64
MOE_BLK = 256
LANE = 128
SUB = 8
VMEM_LIMIT = 56 * 1024 * 1024

PA_W = 3 * RW + 2 * LANE
PBG_W = 256 + 256 + RW + LANE
NP_W = PA_W + PBG_W + RW + S5_W


def _cp(sem):
    return pltpu.CompilerParams(dimension_semantics=sem, vmem_limit_bytes=VMEM_LIMIT)


def _sigmoid(x):
    return 1.0 / (1.0 + jnp.exp(-x))


def _silu(x):
    return x * _sigmoid(x)


def _bdot(a, b, dims=None):
    a = a.astype(BF16)
    b = b.astype(BF16)
    if dims is None:
        return jnp.dot(a, b, preferred_element_type=F32)
    return lax.dot_general(a, b, dims, preferred_element_type=F32)


def _split2(x):
    hi = x.astype(BF16)
    lo = (x - hi.astype(F32)).astype(BF16)
    return hi, lo


def _split3(x):
    hi = x.astype(BF16)
    r = x - hi.astype(F32)
    mid = r.astype(BF16)
    lo = (r - mid.astype(F32)).astype(BF16)
    return hi, mid, lo


NT_DIMS = (((1,), (1,)), ((), ()))
TN_DIMS = (((0,), (0,)), ((), ()))


def _dot33(a, b, dims=None):
    ah, al = _split2(a)
    bh, bl = _split2(b)
    return _bdot(ah, bh, dims) + (_bdot(ah, bl, dims) + _bdot(al, bh, dims))


def _dot_exact_lhs(m_exact, x, dims=None):
    h, m, l = _split3(x)
    return _bdot(m_exact, h, dims) + (_bdot(m_exact, m, dims) + _bdot(m_exact, l, dims))


def _dot_exact_rhs(x, m_exact, dims=None):
    h, m, l = _split3(x)
    return _bdot(h, m_exact, dims) + (_bdot(m, m_exact, dims) + _bdot(l, m_exact, dims))


def _dot_split_rhs(x, m_exact, dims=None):
    h, l = _split2(x)
    return _bdot(h, m_exact, dims) + _bdot(l, m_exact, dims)


def _layer_norm(x):
    mu = jnp.mean(x, axis=-1, keepdims=True)
    xc = x - mu
    var = jnp.mean(xc * xc, axis=-1, keepdims=True)
    return xc * lax.rsqrt(var + LN_EPS)


def _mod_kernel(c_ref, w_ref, b_ref, o_ref):
    s = _silu(c_ref[...])
    o_ref[0] = _dot33(s, w_ref[0]) + b_ref[0]


def mod_table(cc, w_mod, b_mod):
    L, D, D6 = w_mod.shape
    R = cc.shape[0]
    tn = 1536
    return pl.pallas_call(
        _mod_kernel,
        out_shape=jax.ShapeDtypeStruct((L, R, D6), F32),
        grid=(L, D6 // tn),
        in_specs=[pl.BlockSpec((R, D), lambda l, j: (0, 0)),
                  pl.BlockSpec((1, D, tn), lambda l, j: (l, 0, j)),
                  pl.BlockSpec((1, 1, tn), lambda l, j: (l, 0, j))],
        out_specs=pl.BlockSpec((1, R, tn), lambda l, j: (l, 0, j)),
        compiler_params=_cp(("arbitrary", "arbitrary")),
        name="mod_table",
    )(cc, w_mod, b_mod.reshape(L, 1, D6))


def _inproj_kernel(x_ref, mod_ref, w_ref, pa_ref, pbg_ref, gg_ref, u_ref):
    x = x_ref[...]
    h = _layer_norm(x) * (1.0 + mod_ref[0, 1:2, :]) + mod_ref[0, 0:1, :]
    hb = h.astype(BF16)
    o = 0
    for ref in (pa_ref, pbg_ref, gg_ref, u_ref):
        w = ref.shape[-1]
        ref[...] = jnp.dot(hb, w_ref[:, o:o + w], preferred_element_type=F32)
        o += w


def inproj(x2, mod3, w_bf, row_of_tile, tm):
    N, D = x2.shape
    return pl.pallas_call(
        _inproj_kernel,
        out_shape=[jax.ShapeDtypeStruct((N, PA_W), F32),
                   jax.ShapeDtypeStruct((N, PBG_W), F32),
                   jax.ShapeDtypeStruct((N, RW), F32),
                   jax.ShapeDtypeStruct((N, S5_W), F32)],
        grid=(N // tm,),
        in_specs=[pl.BlockSpec((tm, D), lambda i: (i, 0)),
                  pl.BlockSpec((1, 6, D), lambda i: (row_of_tile(i), 0, 0)),
                  pl.BlockSpec((D, NP_W), lambda i: (0, 0))],
        out_specs=[pl.BlockSpec((tm, PA_W), lambda i: (i, 0)),
                   pl.BlockSpec((tm, PBG_W), lambda i: (i, 0)),
                   pl.BlockSpec((tm, RW), lambda i: (i, 0)),
                   pl.BlockSpec((tm, S5_W), lambda i: (i, 0))],
        compiler_params=_cp(("arbitrary",)),
        name="inproj",
    )(x2, mod3, w_bf)


CONV_PAD = 72


def _conv_kernel(x_ref, taps_ref, o_ref, buf_ref, *, T, W, vertical):
    pad = CONV_PAD
    zeros = jnp.zeros((pad, LANE), F32)
    buf_ref[0:pad, :] = zeros
    buf_ref[pad + T:pad + T + pad, :] = zeros
    buf_ref[pad:pad + T, :] = x_ref[0]
    ch = min(T, 256)
    col = lax.broadcasted_iota(I32, (ch, LANE), 0) & (W - 1)
    left_ok = col >= 1
    right_ok = col <= W - 2
    for c in range(T // ch):
        base = pad + c * ch
        acc = jnp.zeros((ch, LANE), F32)
        for dr in ((0, 1, 2) if vertical else (1,)):
            for dc in range(3):
                off = (dr - 1) * W + (dc - 1)
                v = buf_ref[base + off:base + off + ch, :]
                if dc == 0:
                    v = jnp.where(left_ok, v, 0.0)
                elif dc == 2:
                    v = jnp.where(right_ok, v, 0.0)
                acc = acc + v * taps_ref[3 * dr + dc:3 * dr + dc + 1, :]
        o_ref[0, c * ch:(c + 1) * ch, :] = acc


def short_conv(pa3, taps9, W):
    B, T, C = pa3.shape
    vertical = T > W
    assert W & (W - 1) == 0 and (not vertical or W + 1 <= CONV_PAD)
    kern = functools.partial(_conv_kernel, T=T, W=W, vertical=vertical)
    return pl.pallas_call(
        kern,
        out_shape=jax.ShapeDtypeStruct((B, T, C), F32),
        grid=(B, C // LANE),
        in_specs=[pl.BlockSpec((1, T, LANE), lambda b, j: (b, 0, j)),
                  pl.BlockSpec((9, LANE), lambda b, j: (0, j))],
        out_specs=pl.BlockSpec((1, T, LANE), lambda b, j: (b, 0, j)),
        scratch_shapes=[pltpu.VMEM((T + 2 * CONV_PAD, LANE), F32)],
        compiler_params=_cp(("arbitrary", "arbitrary")),
        name="short_conv",
    )(pa3, taps9)


def _rwkv_prep_kernel(pa_ref, wup_ref, aup_ref, gup_ref, w0_ref, a0_ref, kk_ref, ka_ref, rk_ref, hb_ref,
                      ss_ref, sd_ref, gb_ref):
    r = pa_ref[:, 0:RW]
    k = pa_ref[:, RW:2 * RW]
    v = pa_ref[:, 2 * RW:3 * RW]
    wa = pa_ref[:, 3 * RW:3 * RW + LANE]
    gd = pa_ref[:, 3 * RW + LANE:3 * RW + 2 * LANE]
    z = w0_ref[...] + _dot33(jnp.tanh(wa), wup_ref[...])
    lw = -_sigmoid(z) * math.exp(-0.5)
    a = _sigmoid(a0_ref[...] + _bdot(wa, aup_ref[...]))
    g = _bdot(_sigmoid(gd), gup_ref[...])
    hb = hb_ref[...]
    kk = k * kk_ref[...]
    kk = kk * lax.rsqrt(_dot_split_rhs(kk * kk, hb) + 1e-12)
    ka = ka_ref[...]
    ss_ref[:, 0:RW] = r
    ss_ref[:, RW:2 * RW] = v
    ss_ref[:, 2 * RW:3 * RW] = kk
    rk2 = jnp.zeros_like(r)
    for d in range(2):
        ad = a[:, d * RW:(d + 1) * RW]
        k2 = k * (1.0 + (ad - 1.0) * ka)
        sd_ref[d, :, 0:RW] = lw[:, d * RW:(d + 1) * RW]
        sd_ref[d, :, RW:2 * RW] = k2
        sd_ref[d, :, 2 * RW:3 * RW] = kk * ad
        rk2 = rk2 + r * k2
    bonus = _dot_split_rhs(rk2 * rk_ref[...], hb) * v
    gb_ref[:, 0:RW] = g
    gb_ref[:, RW:2 * RW] = bonus


def rwkv_prep(pa2, wts, tm):
    N = pa2.shape[0]
    full = lambda a: pl.BlockSpec(a.shape, lambda i: (0,) * a.ndim)
    return pl.pallas_call(
        _rwkv_prep_kernel,
        out_shape=[jax.ShapeDtypeStruct((N, 3 * RW), F32),
                   jax.ShapeDtypeStruct((2, N, 3 * RW), F32),
                   jax.ShapeDtypeStruct((N, 2 * RW), F32)],
        grid=(N // tm,),
        in_specs=[pl.BlockSpec((tm, PA_W), lambda i: (i, 0))] + [full(a) for a in wts],
        out_specs=[pl.BlockSpec((tm, 3 * RW), lambda i: (i, 0)),
                   pl.BlockSpec((2, tm, 3 * RW), lambda i: (0, i, 0)),
                   pl.BlockSpec((tm, 2 * RW), lambda i: (i, 0))],
        compiler_params=_cp(("arbitrary",)),
        name="rwkv_prep",
    )(pa2, *wts)


RWKV_GROUP = 8


def _rwkv_scan_kernel(ss_ref, sd_ref, s0_ref, y_ref, sfin_ref, s_scr, *, nsteps, group, reverse):
    n = pl.program_id(1)
    C = CHUNK
    P = 2 * C
    npair = RW // LANE

    @pl.when(n == 0)
    def _():
        s_scr[...] = s0_ref[0]

    row = lax.broadcasted_iota(I32, (P, P), 0)
    col = lax.broadcasted_iota(I32, (P, P), 1)
    same = (row >> 6) == (col >> 6)
    dlt = (col & (C - 1)) - (row & (C - 1)) if reverse else (row & (C - 1)) - (col & (C - 1))
    strict = same & (dlt > 0)
    incl = same & (dlt >= 0)
    eye = (row == col).astype(F32)
    lvl_masks = [((row >> (lvl + 1)) == (col >> (lvl + 1))) & ((row >> lvl) != (col >> lvl)) for lvl in range(6)]
    rc = lax.broadcasted_iota(I32, (C, C), 0)
    cc = lax.broadcasted_iota(I32, (C, C), 1)
    tri = jnp.where((cc >= rc) if reverse else (rc >= cc), 1.0, 0.0).astype(BF16)
    head0 = lax.broadcasted_iota(I32, (C, LANE), 1) < HEAD

    def stack(x):
        return jnp.concatenate([jnp.where(head0, x, 0.0), jnp.where(head0, 0.0, x)], axis=0)

    streams = [(g, p) for g in range(group) for p in range(npair)]
    tm_ = {}
    for (g, p) in streams:
        t0, t1 = g * C, (g + 1) * C
        lo, hi = p * LANE, (p + 1) * LANE
        lw = sd_ref[0, t0:t1, lo:hi]
        tm_[(g, p)] = dict(lw=lw, cl=_dot_exact_lhs(tri, lw))
    for (g, p) in streams:
        t = tm_[(g, p)]
        t0, t1 = g * C, (g + 1) * C
        lo, hi = p * LANE, (p + 1) * LANE
        r = ss_ref[t0:t1, lo:hi]
        v = ss_ref[t0:t1, RW + lo:RW + hi]
        kk = ss_ref[t0:t1, 2 * RW + lo:2 * RW + hi]
        k2 = sd_ref[0, t0:t1, RW + lo:RW + hi]
        b = sd_ref[0, t0:t1, 2 * RW + lo:2 * RW + hi]
        cl, lw = t['cl'], t['lw']
        t['ptot'] = jnp.exp(jnp.sum(lw, axis=0, keepdims=True))
        pinv = jnp.exp(-cl)
        left = jnp.concatenate([stack(-kk * jnp.exp(cl - lw)), stack(r * jnp.exp(cl))], axis=0)
        right = jnp.concatenate([stack(b * pinv), stack(k2 * pinv)], axis=0)
        t['v_st'] = stack(v)
        t['left'] = left.astype(BF16)
        t['bk'] = (right * t['ptot']).astype(BF16)
        aa = _bdot(left, right, NT_DIMS)
        t['nmat'] = jnp.where(strict, aa[0:P, 0:P], 0.0)
        t['a_ak'] = jnp.where(strict, aa[0:P, P:2 * P], 0.0)
        t['a_rbk'] = jnp.concatenate([jnp.where(incl, aa[P:2 * P, 0:P], 0.0),
                                      jnp.where(incl, aa[P:2 * P, P:2 * P], 0.0)], axis=1).astype(BF16)
        t['tinv'] = eye
    for sk in streams:
        t = tm_[sk]
        t['akv'] = _bdot(t['a_ak'], t['v_st'])
    for m in lvl_masks:
        for sk in streams:
            t = tm_[sk]
            t['et'] = _bdot(jnp.where(m, t['nmat'], 0.0), t['tinv'])
        for sk in streams:
            t = tm_[sk]
            t['tinv'] = t['tinv'] + _bdot(t['tinv'], t['et'])
    state = [s_scr[p] for p in range(npair)]
    pairs = range(npair)
    for g in (range(group - 1, -1, -1) if reverse else range(group)):
        ts = [tm_[(g, p)] for p in pairs]
        a_s = [_bdot(ts[p]['left'], state[p], NT_DIMS) for p in pairs]
        u = [_bdot(ts[p]['tinv'], a_s[p][0:P] + ts[p]['akv']) for p in pairs]
        uv = [jnp.concatenate([u[p], ts[p]['v_st']], axis=0) for p in pairs]
        state = [state[p] * ts[p]['ptot'] + _bdot(uv[p], ts[p]['bk'], TN_DIMS) for p in pairs]
        for p in pairs:
            y_st = a_s[p][P:2 * P] + _bdot(ts[p]['a_rbk'], uv[p])
            y_ref[g * C:(g + 1) * C, p * LANE:(p + 1) * LANE] = y_st[0:C] + y_st[C:P]
    for p in pairs:
        s_scr[p] = state[p]

    @pl.when(n == nsteps - 1)
    def _():
        sfin_ref[0] = s_scr[...]


def rwkv_scan_dir(ss, sd, s0, B, T, d):
    nc = T // CHUNK
    group = min(RWKV_GROUP, nc)
    nsteps = nc // group
    npair = RW // LANE
    rows = group * CHUNK
    blk = (lambda b, n: b * nsteps + nsteps - 1 - n) if d else (lambda b, n: b * nsteps + n)
    kern = functools.partial(_rwkv_scan_kernel, nsteps=nsteps, group=group, reverse=bool(d))
    return pl.pallas_call(
        kern,
        out_shape=[jax.ShapeDtypeStruct((B * T, RW), F32),
                   jax.ShapeDtypeStruct((B, npair, LANE, LANE), F32)],
        grid=(B, nsteps),
        in_specs=[pl.BlockSpec((rows, 3 * RW), lambda b, n: (blk(b, n), 0)),
                  pl.BlockSpec((1, rows, 3 * RW), lambda b, n: (d, blk(b, n), 0)),
                  pl.BlockSpec((1, npair, LANE, LANE), lambda b, n: (b, 0, 0, 0))],
        out_specs=[pl.BlockSpec((rows, RW), lambda b, n: (blk(b, n), 0)),
                   pl.BlockSpec((1, npair, LANE, LANE), lambda b, n: (b, 0, 0, 0))],
        scratch_shapes=[pltpu.VMEM((npair, LANE, LANE), F32)],
        compiler_params=_cp(("arbitrary", "arbitrary")),
        name="rwkv_scan_bwd" if d else "rwkv_scan_fwd",
    )(ss, sd, s0)


def rwkv_scan(ss, sd, s0, B, T):
    y0, f0 = rwkv_scan_dir(ss, sd, s0[0], B, T, 0)
    y1, f1 = rwkv_scan_dir(ss, sd, s0[1], B, T, 1)
    return (y0, y1), (f0, f1)


GLA_KP = 256


GLA_GROUP = 8


def _gla_kernel(pbg_ref, aup_ref, ab_ref, s0_ref, o_ref, sfin_ref, s_scr, *, nsteps, group, reverse):
    n = pl.program_id(1)
    C = CHUNK
    R = group * C

    @pl.when(n == 0)
    def _():
        s_scr[...] = s0_ref[0]

    q = pbg_ref[:, 0:GLA_KP]
    k = pbg_ref[:, GLA_KP:2 * GLA_KP]
    v = pbg_ref[:, 2 * GLA_KP:2 * GLA_KP + RW]
    ad = pbg_ref[:, 2 * GLA_KP + RW:2 * GLA_KP + RW + LANE]
    x = _dot33(ad, aup_ref[0]) + ab_ref[0]
    la = (jnp.minimum(x, 0.0) - jnp.log(1.0 + jnp.exp(-jnp.abs(x)))) * (1.0 / GLA_TEMP)
    chunks = range(group)
    sl = [slice(g * C, (g + 1) * C) for g in chunks]
    rr = lax.broadcasted_iota(I32, (C, C), 0)
    rc = lax.broadcasted_iota(I32, (C, C), 1)
    tri = jnp.where((rc >= rr) if reverse else (rr >= rc), 1.0, 0.0).astype(BF16)
    bcums = [_dot_exact_lhs(tri, la[sl[g]]) for g in chunks]
    last = 0 if reverse else C - 1
    tots = [bc[last:last + 1, :] for bc in bcums]
    bcum = jnp.concatenate(bcums, axis=0)
    tot = jnp.concatenate([jnp.broadcast_to(t, (C, GLA_KP)) for t in tots], axis=0)
    q_in = q * jnp.exp(bcum) * (GLA_K ** -0.5)
    k_in = k * jnp.exp(-bcum)
    k_st = k * jnp.exp(tot - bcum)
    dn = [jnp.exp(t) for t in tots]
    klane = lax.broadcasted_iota(I32, (C, GLA_KP), 1)
    rt = lax.broadcasted_iota(I32, (GLA_HEADS * C, C), 0) & (C - 1)
    ct = lax.broadcasted_iota(I32, (GLA_HEADS * C, C), 1)
    causal = (ct >= rt) if reverse else (rt >= ct)
    vlane = lax.broadcasted_iota(I32, (C, RW), 1)
    sv = lax.broadcasted_iota(I32, (RW, GLA_KP), 0) >> 6
    sk = lax.broadcasted_iota(I32, (RW, GLA_KP), 1) >> 5
    q_rows = [jnp.concatenate([jnp.where((klane >> 5) == h, q_in[sl[g]], 0.0) for h in range(GLA_HEADS)],
                              axis=0).astype(BF16) for g in chunks]
    att = [jnp.where(causal, _bdot(q_rows[g], k_in[sl[g]], NT_DIMS), 0.0) for g in chunks]
    o_rows = [_bdot(att[g], v[sl[g]]) for g in chunks]
    kv = [jnp.where(sv == sk, _bdot(v[sl[g]], k_st[sl[g]], TN_DIMS), 0.0) for g in chunks]
    s = s_scr[...]
    for g in (reversed(chunks) if reverse else chunks):
        o = _bdot(q_in[sl[g]], s, NT_DIMS)
        for h in range(GLA_HEADS):
            o = o + jnp.where((vlane >> 6) == h, o_rows[g][h * C:(h + 1) * C], 0.0)
        o_ref[sl[g], :] = o
        s = s * dn[g] + kv[g]
    s_scr[...] = s

    @pl.when(n == nsteps - 1)
    def _():
        sfin_ref[0] = s_scr[...]


def gla_scan_dir(pbg, aup, ab, s0, B, T, d):
    nc = T // CHUNK
    group = min(GLA_GROUP, nc)
    nsteps = nc // group
    rows = group * CHUNK
    blk = (lambda b, n: b * nsteps + nsteps - 1 - n) if d else (lambda b, n: b * nsteps + n)
    kern = functools.partial(_gla_kernel, nsteps=nsteps, group=group, reverse=bool(d))
    return pl.pallas_call(
        kern,
        out_shape=[jax.ShapeDtypeStruct((B * T, RW), F32),
                   jax.ShapeDtypeStruct((B, RW, GLA_KP), F32)],
        grid=(B, nsteps),
        in_specs=[pl.BlockSpec((rows, PBG_W), lambda b, n: (blk(b, n), 0)),
                  pl.BlockSpec((1, LANE, GLA_KP), lambda b, n: (d, 0, 0)),
                  pl.BlockSpec((1, 1, GLA_KP), lambda b, n: (d, 0, 0)),
                  pl.BlockSpec((1, RW, GLA_KP), lambda b, n: (b, 0, 0))],
        out_specs=[pl.BlockSpec((rows, RW), lambda b, n: (blk(b, n), 0)),
                   pl.BlockSpec((1, RW, GLA_KP), lambda b, n: (b, 0, 0))],
        scratch_shapes=[pltpu.VMEM((RW, GLA_KP), F32)],
        compiler_params=_cp(("arbitrary", "arbitrary")),
        name="gla_scan_bwd" if d else "gla_scan_fwd",
    )(pbg, aup, ab, s0)


def gla_scan(pbg, aup, ab, s0, B, T):
    o0, f0 = gla_scan_dir(pbg, aup, ab, s0[0], B, T, 0)
    o1, f1 = gla_scan_dir(pbg, aup, ab, s0[1], B, T, 1)
    return (o0, o1), (f0, f1)


S5_PITCH = 72


def _s5_kernel(u_ref, a_ref, bbd_ref, cbd_ref, s0_ref, y_ref, sfin_ref, x_scr, st_scr, *, nc, nb):
    d = pl.program_id(0)
    n = pl.program_id(1)
    C = CHUNK
    half = S5_LANES // 2

    @pl.when(n == 0)
    def _():
        st_scr[...] = s0_ref[0]

    nslab = S5_LANES // LANE
    bbd = bbd_ref[0]
    for b in range(nb):
        bu = _bdot(u_ref[b], bbd)
        for j in range(nslab):
            x_scr[j, b * S5_PITCH:b * S5_PITCH + C, :] = bu[:, j * LANE:(j + 1) * LANE]
    a_re = a_ref[0, :, 0:half]
    a_im = a_ref[0, :, half:S5_LANES]

    def step(i, carry):
        re, im = carry
        t = i + d * (C - 1 - 2 * i)
        rows = pl.ds(t, nb, stride=S5_PITCH)
        bu = jnp.concatenate([x_scr[j, rows, :] for j in range(nslab)], axis=1)
        nre = a_re * re - a_im * im + bu[:, 0:half]
        nim = a_re * im + a_im * re + bu[:, half:S5_LANES]
        for j in range(nslab // 2):
            x_scr[j, rows, :] = nre[:, j * LANE:(j + 1) * LANE]
            x_scr[nslab // 2 + j, rows, :] = nim[:, j * LANE:(j + 1) * LANE]
        return nre, nim

    st = st_scr[...]
    re, im = lax.fori_loop(0, C, step, (st[:, 0:half], st[:, half:S5_LANES]))
    st_scr[...] = jnp.concatenate([re, im], axis=1)
    cbd = cbd_ref[0]
    for b in range(nb):
        xs = jnp.concatenate([x_scr[j, b * S5_PITCH:b * S5_PITCH + C, :] for j in range(nslab)], axis=1)
        y_ref[0, b] = _bdot(xs, cbd)

    @pl.when(n == nc - 1)
    def _():
        sfin_ref[0] = st_scr[...]


def s5_scan(u3, a_bar, bbd, cbd, s0):
    B, T, _ = u3.shape
    nc = T // CHUNK
    chunk = lambda d, n: n + d * (nc - 1 - 2 * n)
    kern = functools.partial(_s5_kernel, nc=nc, nb=B)
    return pl.pallas_call(
        kern,
        out_shape=[jax.ShapeDtypeStruct((2, B, T, S5_W), F32),
                   jax.ShapeDtypeStruct((2, B, S5_LANES), F32)],
        grid=(2, nc),
        in_specs=[pl.BlockSpec((B, CHUNK, S5_W), lambda d, n: (0, chunk(d, n), 0)),
                  pl.BlockSpec((1, 1, S5_LANES), lambda d, n: (d, 0, 0)),
                  pl.BlockSpec((1, S5_W, S5_LANES), lambda d, n: (d, 0, 0)),
                  pl.BlockSpec((1, S5_LANES, S5_W), lambda d, n: (d, 0, 0)),
                  pl.BlockSpec((1, B, S5_LANES), lambda d, n: (d, 0, 0))],
        out_specs=[pl.BlockSpec((1, B, CHUNK, S5_W), lambda d, n: (d, 0, chunk(d, n), 0)),
                   pl.BlockSpec((1, B, S5_LANES), lambda d, n: (d, 0, 0))],
        scratch_shapes=[pltpu.VMEM((S5_LANES // LANE, B * S5_PITCH, LANE), F32),
                        pltpu.VMEM((B, S5_LANES), F32)],
        compiler_params=_cp(("arbitrary", "arbitrary")),
        name="s5_scan",
    )(u3, a_bar, bbd, cbd, s0)


def _tile_rows_store(ref, val):
    tm = val.shape[0]
    for s in range(val.shape[1] // LANE):
        ref[pl.ds(s, tm, stride=SUB), :] = val[:, s * LANE:(s + 1) * LANE]


def _tile_rows_load(ref, tm, base=0):
    return jnp.concatenate([ref[pl.ds(base + s, tm, stride=SUB), :] for s in range(SUB)], axis=1)


def _mix_out_kernel(y0_ref, y1_ref, gb_ref, o0_ref, o1_ref, gg_ref, yd_ref, u_ref, x_ref, mod_ref,
                    hb_ref, lng_ref, lnb_ref, gng_ref, s5d_ref, gluw_ref, glub_ref, wout_ref, l1g_ref, l1b_ref,
                    x1_ref, h2_ref, h2t_ref, *, alpha):
    hb = hb_ref[...]
    inv = 1.0 / HEAD
    y = y0_ref[...] + y1_ref[...]
    yc = y - _dot_split_rhs(y, hb) * inv
    var = _dot_split_rhs(yc * yc, hb) * inv
    gn = yc * lax.rsqrt(var + RWKV_GN_EPS) * lng_ref[...] + lnb_ref[...]
    y_a = (gn + gb_ref[:, RW:2 * RW]) * gb_ref[:, 0:RW]
    o = o0_ref[...] + o1_ref[...]
    o = o * lax.rsqrt(_dot_split_rhs(o * o, hb) * inv + 1e-6) * gng_ref[...]
    y_b = o * _silu(gg_ref[...])
    c = s5d_ref[...] * u_ref[...] + yd_ref[0] + yd_ref[1]
    c = 0.5 * c * (1.0 + jnp.tanh(math.sqrt(2.0 / math.pi) * (c + 0.044715 * (c * c * c))))
    y_c = c * _sigmoid(_bdot(c, gluw_ref[...]) + glub_ref[...])
    y_mix = (_bdot(y_a, wout_ref[0:RW, :]) + _bdot(y_b, wout_ref[RW:2 * RW, :])
             + _bdot(y_c, wout_ref[2 * RW:2 * RW + S5_W, :]))
    x1 = _layer_norm(alpha * x_ref[...] + mod_ref[0, 2:3, :] * y_mix) * l1g_ref[...] + l1b_ref[...]
    x1_ref[...] = x1
    h2 = _layer_norm(x1) * (1.0 + mod_ref[0, 4:5, :]) + mod_ref[0, 3:4, :]
    h2_ref[...] = h2
    _tile_rows_store(h2t_ref, h2)


def mix_out(y01, gb, o01, gg, yd, u, x2, mod3, wts, row_of_tile, tm, alpha):
    N, D = x2.shape
    full = lambda a: pl.BlockSpec(a.shape, lambda i: (0,) * a.ndim)
    kern = functools.partial(_mix_out_kernel, alpha=alpha)
    return pl.pallas_call(
        kern,
        out_shape=[jax.ShapeDtypeStruct((N, D), F32),
                   jax.ShapeDtypeStruct((N, D), F32),
                   jax.ShapeDtypeStruct((N * SUB, LANE), F32)],
        grid=(N // tm,),
        in_specs=[pl.BlockSpec((tm, RW), lambda i: (i, 0)),
                  pl.BlockSpec((tm, RW), lambda i: (i, 0)),
                  pl.BlockSpec((tm, 2 * RW), lambda i: (i, 0)),
                  pl.BlockSpec((tm, RW), lambda i: (i, 0)),
                  pl.BlockSpec((tm, RW), lambda i: (i, 0)),
                  pl.BlockSpec((tm, RW), lambda i: (i, 0)),
                  pl.BlockSpec((2, tm, S5_W), lambda i: (0, i, 0)),
                  pl.BlockSpec((tm, S5_W), lambda i: (i, 0)),
                  pl.BlockSpec((tm, D), lambda i: (i, 0)),
                  pl.BlockSpec((1, 6, D), lambda i: (row_of_tile(i), 0, 0))] + [full(a) for a in wts],
        out_specs=[pl.BlockSpec((tm, D), lambda i: (i, 0)),
                   pl.BlockSpec((tm, D), lambda i: (i, 0)),
                   pl.BlockSpec((tm * SUB, LANE), lambda i: (i, 0))],
        compiler_params=_cp(("arbitrary",)),
        name="mix_out",
    )(y01[0], y01[1], gb, o01[0], o01[1], gg, yd, u, x2, mod3, *wts)


def _first_max(x, idx, big):
    m = jnp.max(x, axis=0, keepdims=True)
    first = jnp.min(jnp.where(x == m, idx, big), axis=0, keepdims=True)
    return m, first


def _router_kernel(h_ref, rwt_ref, bias_ref, e_ref, w_ref):
    tm = h_ref.shape[0]
    gsz = N_EXPERTS // N_EGROUPS
    ninf = -jnp.inf
    s = _sigmoid(_dot33(rwt_ref[...], h_ref[...], NT_DIMS))
    ssel = s + bias_ref[:, 0:1]
    gi = lax.broadcasted_iota(I32, (gsz, tm), 0)
    gscore = []
    for g in range(N_EGROUPS):
        xg = ssel[g * gsz:(g + 1) * gsz, :]
        m1, i1 = _first_max(xg, gi, gsz)
        m2 = jnp.max(jnp.where(gi == i1, ninf, xg), axis=0, keepdims=True)
        gscore.append(m1 + m2)
    cur = jnp.concatenate(gscore, axis=0)
    gidx = lax.broadcasted_iota(I32, (N_EGROUPS, tm), 0)
    picked = jnp.zeros((N_EGROUPS, tm), F32)
    for _ in range(TOPK_GROUPS):
        _, first = _first_max(cur, gidx, N_EGROUPS)
        hit = gidx == first
        picked = jnp.where(hit, 1.0, picked)
        cur = jnp.where(hit, ninf, cur)
    x = jnp.concatenate(
        [jnp.where(picked[g:g + 1, :] > 0.5, ssel[g * gsz:(g + 1) * gsz, :], ninf) for g in range(N_EGROUPS)], axis=0)
    ei = lax.broadcasted_iota(I32, (N_EXPERTS, tm), 0)
    idxs, ws = [], []
    for _ in range(TOP_K):
        _, first = _first_max(x, ei, N_EXPERTS)
        hit = ei == first
        idxs.append(first)
        ws.append(jnp.sum(jnp.where(hit, s, 0.0), axis=0, keepdims=True))
        x = jnp.where(hit, ninf, x)
    w = jnp.concatenate(ws, axis=0)
    e_ref[...] = jnp.concatenate(idxs, axis=0)
    w_ref[...] = w / jnp.sum(w, axis=0, keepdims=True) * ROUTE_SCALE


def moe_router(h2, rwt, bias_b, tm):
    N, D = h2.shape
    return pl.pallas_call(
        _router_kernel,
        out_shape=[jax.ShapeDtypeStruct((TOP_K, N), I32), jax.ShapeDtypeStruct((TOP_K, N), F32)],
        grid=(N // tm,),
        in_specs=[pl.BlockSpec((tm, D), lambda i: (i, 0)),
                  pl.BlockSpec((N_EXPERTS, D), lambda i: (0, 0)),
                  pl.BlockSpec((N_EXPERTS, LANE), lambda i: (0, 0))],
        out_specs=[pl.BlockSpec((TOP_K, tm), lambda i: (0, i)),
                   pl.BlockSpec((TOP_K, tm), lambda i: (0, i))],
        compiler_params=_cp(("arbitrary",)),
        name="moe_router",
    )(h2, rwt, bias_b)


def _moe_count_kernel(e_ref, pstart_ref, plan_ref, blk_ref, cnt_scr, *, nt, nbp):
    i = pl.program_id(0)
    tp = e_ref.shape[1]

    @pl.when(i == 0)
    def _():
        cnt_scr[...] = jnp.zeros_like(cnt_scr)

    ei = lax.broadcasted_iota(I32, (N_EXPERTS, tp), 0)
    acc = jnp.zeros((N_EXPERTS, 1), F32)
    for k in range(TOP_K):
        acc = acc + jnp.sum(jnp.where(ei == e_ref[k:k + 1, :], 1.0, 0.0), axis=1, keepdims=True)
    cnt_scr[...] = cnt_scr[...] + acc

    @pl.when(i == nt - 1)
    def _():
        cnt = cnt_scr[...].astype(I32)
        padded = (cnt + (MOE_BLK - 1)) & (-MOE_BLK)
        r = lax.broadcasted_iota(I32, (N_EXPERTS, N_EXPERTS), 0)
        c = lax.broadcasted_iota(I32, (N_EXPERTS, N_EXPERTS), 1)
        tri = jnp.where(c <= r, 1.0, 0.0).astype(BF16)
        padded_b = jnp.broadcast_to(padded.astype(F32), (N_EXPERTS, LANE))
        p_end = _dot_exact_lhs(tri, padded_b)
        pstart = p_end - padded_b
        pstart_ref[...] = pstart.astype(I32)
        diag = r == c
        ps_row = jnp.sum(jnp.where(diag, pstart[:, 0:1], 0.0), axis=0, keepdims=True)
        cnt_row = jnp.sum(jnp.where(diag, cnt_scr[...], 0.0), axis=0, keepdims=True)
        plan_ref[...] = jnp.concatenate([ps_row, cnt_row, jnp.zeros((SUB - 2, N_EXPERTS), F32)], axis=0).astype(I32)
        lim =(lax.broadcasted_iota(I32, (N_EXPERTS, nbp), 1) * MOE_BLK).astype(F32)
        be = jnp.sum(jnp.where(p_end[:, 0:1] <= lim, 1.0, 0.0), axis=0, keepdims=True)
        be = jnp.minimum(be, N_EXPERTS - 1.0)
        nused = jnp.max(p_end[:, 0:1], axis=0, keepdims=True) * (1.0 / MOE_BLK)
        blk_ref[...] = jnp.concatenate([jnp.broadcast_to(be, (SUB // 2, nbp)),
                                        jnp.broadcast_to(nused, (SUB // 2, nbp))], axis=0).astype(I32)


def moe_counts(eidx, tp, nbp):
    N = eidx.shape[1]
    nt = N // tp
    kern = functools.partial(_moe_count_kernel, nt=nt, nbp=nbp)
    return pl.pallas_call(
        kern,
        out_shape=[jax.ShapeDtypeStruct((N_EXPERTS, LANE), I32),
                   jax.ShapeDtypeStruct((SUB, N_EXPERTS), I32),
                   jax.ShapeDtypeStruct((SUB, nbp), I32)],
        grid=(nt,),
        in_specs=[pl.BlockSpec((TOP_K, tp), lambda i: (0, i))],
        out_specs=[pl.BlockSpec((N_EXPERTS, LANE), lambda i: (0, 0)),
                   pl.BlockSpec((SUB, N_EXPERTS), lambda i: (0, 0)),
                   pl.BlockSpec((SUB, nbp), lambda i: (0, 0))],
        scratch_shapes=[pltpu.VMEM((N_EXPERTS, 1), F32)],
        compiler_params=_cp(("arbitrary",)),
        name="moe_counts",
    )(eidx)


def _moe_dest_kernel(e_ref, pstart_ref, dest_ref, base_scr):
    i = pl.program_id(0)
    tp = e_ref.shape[1]

    @pl.when(i == 0)
    def _():
        base_scr[...] = pstart_ref[:, 0:1].astype(F32)

    ei = lax.broadcasted_iota(I32, (N_EXPERTS, tp), 0)
    r = lax.broadcasted_iota(I32, (tp, tp), 0)
    c = lax.broadcasted_iota(I32, (tp, tp), 1)
    tri = jnp.where(r <= c, 1.0, 0.0).astype(BF16)
    base = base_scr[...]
    rows = []
    for k in range(TOP_K):
        hit = ei == e_ref[k:k + 1, :]
        oh = jnp.where(hit, 1.0, 0.0)
        cum = _bdot(oh, tri)
        rows.append(jnp.sum(jnp.where(hit, cum - 1.0 + base, 0.0), axis=0, keepdims=True))
        base = base + cum[:, tp - 1:tp]
    base_scr[...] = base
    dest_ref[...] = jnp.concatenate(rows, axis=0).astype(I32)


def moe_dest(eidx, pstart, tp):
    N = eidx.shape[1]
    return pl.pallas_call(
        _moe_dest_kernel,
        out_shape=jax.ShapeDtypeStruct((TOP_K, N), I32),
        grid=(N // tp,),
        in_specs=[pl.BlockSpec((TOP_K, tp), lambda i: (0, i)),
                  pl.BlockSpec((N_EXPERTS, LANE), lambda i: (0, 0))],
        out_specs=pl.BlockSpec((TOP_K, tp), lambda i: (0, i)),
        scratch_shapes=[pltpu.VMEM((N_EXPERTS, 1), F32)],
        compiler_params=_cp(("arbitrary",)),
        name="moe_dest",
    )(eidx, pstart)


def _tile_at(ref, token):
    return ref.at[pl.ds(pl.multiple_of(token * SUB, SUB), SUB), :]


def _moe_scatter_kernel(dest_ref, h_ref, xs_hbm, sem):
    tp = dest_ref.shape[1]

    def issue(j, carry):
        for k in range(TOP_K):
            pltpu.make_async_copy(_tile_at(h_ref, j), _tile_at(xs_hbm, dest_ref[k, j]), sem).start(priority=k % 2)
        return carry

    lax.fori_loop(0, tp, issue, 0)

    def drain(j, carry):
        for k in range(TOP_K):
            pltpu.make_async_copy(h_ref.at[pl.ds(0, SUB), :], xs_hbm.at[pl.ds(0, SUB), :], sem).wait()
        return carry

    lax.fori_loop(0, tp, drain, 0)


def moe_scatter(dest, h2t, n_slots, tp):
    N = dest.shape[1]
    return pl.pallas_call(
        _moe_scatter_kernel,
        out_shape=jax.ShapeDtypeStruct((n_slots * SUB, LANE), F32),
        grid=(N // tp,),
        in_specs=[pl.BlockSpec((TOP_K, tp), lambda i: (0, i), memory_space=pltpu.SMEM),
                  pl.BlockSpec((tp * SUB, LANE), lambda i: (i, 0))],
        out_specs=pl.BlockSpec(memory_space=pl.ANY),
        scratch_shapes=[pltpu.SemaphoreType.DMA(())],
        compiler_params=_cp(("arbitrary",)),
        name="moe_scatter",
    )(dest, h2t)


PAD_BITS = tuple(1 << b for b in reversed(range(MOE_BLK.bit_length() - 1)))


def _moe_padfill_kernel(ps_ref, xs_in, xs_hbm, zero_scr, sem):
    del xs_in
    zero_scr[...] = jnp.zeros_like(zero_scr)

    def pad_copies(e, wait):
        cnt = ps_ref[1, e]
        npad = ((cnt + (MOE_BLK - 1)) & (-MOE_BLK)) - cnt
        off = ps_ref[0, e] + cnt
        for bit in PAD_BITS:
            @pl.when((npad & bit) != 0)
            def _():
                cp = pltpu.make_async_copy(zero_scr.at[pl.ds(0, bit * SUB), :],
                                           xs_hbm.at[pl.ds(pl.multiple_of(off * SUB, SUB), bit * SUB), :], sem)
                if wait:
                    cp.wait()
                else:
                    cp.start()
            off = off + (npad & bit)

    def issue(e, carry):
        pad_copies(e, False)
        return carry

    def drain(e, carry):
        pad_copies(e, True)
        return carry

    lax.fori_loop(0, N_EXPERTS, issue, 0)
    lax.fori_loop(0, N_EXPERTS, drain, 0)


def moe_padfill(pstart, xs):
    return pl.pallas_call(
        _moe_padfill_kernel,
        out_shape=jax.ShapeDtypeStruct(xs.shape, xs.dtype),
        grid=(1,),
        in_specs=[pl.BlockSpec(memory_space=pltpu.SMEM),
                  pl.BlockSpec(memory_space=pl.ANY)],
        out_specs=pl.BlockSpec(memory_space=pl.ANY),
        scratch_shapes=[pltpu.VMEM((PAD_BITS[0] * SUB, LANE), F32), pltpu.SemaphoreType.DMA(())],
        input_output_aliases={1: 0},
        compiler_params=_cp(("arbitrary",)),
        name="moe_padfill",
    )(pstart, xs)


def _experts_kernel(blk_ref, xs_ref, w13_ref, w2_ref, y_ref, w13_bf, w2_bf):
    i = pl.program_id(0)
    nb = xs_ref.shape[0] // SUB
    ff = w2_bf.shape[0]

    @pl.when(i < blk_ref[SUB // 2, 0])
    def _():
        prev = blk_ref[0, jnp.maximum(i - 1, 0)]

        @pl.when((i == 0) | (blk_ref[0, i] != prev))
        def _():
            w13_bf[...] = w13_ref[0, 0].astype(BF16)
            w2_bf[...] = w2_ref[0, 0].astype(BF16)

        x = _tile_rows_load(xs_ref, nb).astype(BF16)
        h = jnp.dot(x, w13_bf[...], preferred_element_type=F32)
        act = (_silu(h[:, 0:ff]) * h[:, ff:2 * ff]).astype(BF16)
        _tile_rows_store(y_ref, jnp.dot(act, w2_bf[...], preferred_element_type=F32))


def moe_experts(blk, xs, w13, w2, layer, nb_total):
    _, E, D, F2 = w13.shape
    last = lambda i, b: jnp.minimum(i, b[SUB // 2, 0] - 1)
    grid_spec = pltpu.PrefetchScalarGridSpec(
        num_scalar_prefetch=1,
        grid=(nb_total,),
        in_specs=[pl.BlockSpec((MOE_BLK * SUB, LANE), lambda i, b: (last(i, b), 0)),
                  pl.BlockSpec((1, 1, D, F2), lambda i, b: (layer, b[0, last(i, b)], 0, 0)),
                  pl.BlockSpec((1, 1, F2 // 2, D), lambda i, b: (layer, b[0, last(i, b)], 0, 0))],
        out_specs=pl.BlockSpec((MOE_BLK * SUB, LANE), lambda i, b: (last(i, b), 0)),
        scratch_shapes=[pltpu.VMEM((D, F2), BF16), pltpu.VMEM((F2 // 2, D), BF16)])
    return pl.pallas_call(
        _experts_kernel,
        out_shape=jax.ShapeDtypeStruct(xs.shape, F32),
        grid_spec=grid_spec,
        compiler_params=_cp(("arbitrary",)),
        name="moe_experts",
    )(blk, xs, w13, w2)


def _moe_combine_kernel(dest_ref, wt_ref, h_ref, x_ref, mod_ref, s13_ref, s2_ref, l2g_ref, l2b_ref, y_hbm,
                        o_ref, g_scr, sem, *, alpha, nt):
    s = pl.program_id(0)
    tm = h_ref.shape[0]
    ff = s2_ref.shape[0]
    rows = tm * SUB
    slot_g = s % 2
    slot_c = 1 - slot_g

    def slot_copy(slot):
        return pltpu.make_async_copy(y_hbm.at[pl.ds(0, TOP_K * rows), :], g_scr.at[slot], sem.at[slot])

    @pl.when(s == 0)
    def _():
        g_scr[1] = jnp.zeros((TOP_K * rows, LANE), F32)

    @pl.when(s > 0)
    def _():
        slot_copy(slot_c).wait()

    for j in range(tm):
        for k in range(TOP_K):
            pltpu.make_async_copy(_tile_at(y_hbm, dest_ref[k, j]),
                                  g_scr.at[slot_g, pl.ds(k * rows + j * SUB, SUB), :],
                                  sem.at[slot_g]).start(priority=k % 2)
    hs = _bdot(h_ref[...], s13_ref[...])
    f = _bdot(_silu(hs[:, 0:ff]) * hs[:, ff:2 * ff], s2_ref[...])
    for k in range(TOP_K):
        f = f + wt_ref[:, k:k + 1] * _tile_rows_load(g_scr.at[slot_c], tm, base=k * rows)
    o_ref[...] = (_layer_norm(alpha * x_ref[...] + mod_ref[0, 5:6, :] * f) * l2g_ref[...] + l2b_ref[...])

    @pl.when(s == nt)
    def _():
        slot_copy(slot_g).wait()


def moe_combine(dest, wt_t, h2, x1, mod3, wts, y, row_of_tile, tm, alpha):
    N, D = h2.shape
    nt = N // tm
    full = lambda a: pl.BlockSpec(a.shape, lambda s: (0,) * a.ndim)
    kern = functools.partial(_moe_combine_kernel, alpha=alpha, nt=nt)
    nxt = lambda s: jnp.minimum(s, nt - 1)
    cur = lambda s: jnp.maximum(s - 1, 0)
    return pl.pallas_call(
        kern,
        out_shape=jax.ShapeDtypeStruct((N, D), F32),
        grid=(nt + 1,),
        in_specs=[pl.BlockSpec((TOP_K, tm), lambda s: (0, nxt(s)), memory_space=pltpu.SMEM),
                  pl.BlockSpec((tm, TOP_K), lambda s: (cur(s), 0)),
                  pl.BlockSpec((tm, D), lambda s: (cur(s), 0)),
                  pl.BlockSpec((tm, D), lambda s: (cur(s), 0)),
                  pl.BlockSpec((1, 6, D), lambda s: (row_of_tile(cur(s)), 0, 0))]
                 + [full(a) for a in wts] + [pl.BlockSpec(memory_space=pl.ANY)],
        out_specs=pl.BlockSpec((tm, D), lambda s: (cur(s), 0)),
        scratch_shapes=[pltpu.VMEM((2, TOP_K * tm * SUB, LANE), F32), pltpu.SemaphoreType.DMA((2,))],
        compiler_params=_cp(("arbitrary",)),
        name="moe_combine",
    )(dest, wt_t, h2, x1, mod3, *wts, y)


def moe_ffn(h2, h2t, x1, mod3, mp, row_of_tile, tm, alpha):
    N = h2.shape[0]
    nb_total = (N * TOP_K + MOE_BLK - 1) // MOE_BLK + N_EXPERTS
    nbp = ((nb_total + LANE - 1) // LANE) * LANE
    eidx, wts = moe_router(h2, mp['rwt'], mp['bias'], tm)
    pstart, plan, blk = moe_counts(eidx, tm, nbp)
    dest = moe_dest(eidx, pstart, tm)
    xs = moe_scatter(dest, h2t, nb_total * MOE_BLK, tm)
    xs = moe_padfill(plan, xs)
    y = moe_experts(blk, xs, mp['w13'], mp['w2'], mp['layer'], nb_total)
    return moe_combine(dest, wts.T, h2, x1, mod3, mp['comb'], y, row_of_tile, tm, alpha)


def token_mixers(x2, B, T, W, mod3, row_of_tile, lp, states, tm):
    pa, pbg, gg, u = inproj(x2, mod3, lp['w_in'], row_of_tile, tm)
    pac = short_conv(pa.reshape(B, T, PA_W), lp['taps'], W).reshape(B * T, PA_W)
    ss, sd, gb = rwkv_prep(pac, lp['prep'], tm)
    s_rwkv, s_gla, s_s5 = states
    y2, f_rwkv = rwkv_scan(ss, sd, s_rwkv, B, T)
    o2, f_gla = gla_scan(pbg, lp['gla_aup'], lp['gla_ab'], s_gla, B, T)
    yd, f_s5 = s5_scan(u.reshape(B, T, S5_W), lp['s5_a'], lp['s5_bbd'], lp['s5_cbd'], s_s5)
    return (y2, gb, o2, gg, yd.reshape(2, B * T, S5_W), u), (f_rwkv, f_gla, f_s5)


def zero_states(B):
    return ((jnp.zeros((B, RW // LANE, LANE, LANE), F32),) * 2,
            (jnp.zeros((B, RW, GLA_KP), F32),) * 2,
            jnp.zeros((2, B, S5_LANES), F32))


def _inproj_columns():
    r_cols = 3 * RW + 2 * DECAY_RANK + 2 * ICLR_RANK + GATE_RANK
    kd = GLA_HEADS * GLA_K
    gq, gk, gv = r_cols, r_cols + kd, r_cols + 2 * kd
    gg = gv + RW
    gad = gg + RW
    pc = gad + 2 * GLA_RANK
    z = lambda n: [-1] * n
    cols = list(range(0, r_cols)) + z(PA_W - r_cols)
    cols += list(range(gq, gq + kd)) + z(256 - kd)
    cols += list(range(gk, gk + kd)) + z(256 - kd)
    cols += list(range(gv, gv + RW))
    cols += list(range(gad, gad + 2 * GLA_RANK)) + z(LANE - 2 * GLA_RANK)
    cols += list(range(gg, gg + RW))
    cols += list(range(pc, pc + S5_W))
    assert len(cols) == NP_W
    return np.asarray(cols, np.int32)


def _head_block_ones():
    h = np.arange(RW) // HEAD
    return jnp.asarray(h[:, None] == h[None, :], BF16)


def _layer_params(l, p):
    D = p['w_in'].shape[1]
    cols = _inproj_columns()
    w_in = jnp.concatenate([p['w_in'][l], jnp.zeros((D, 1), F32)], axis=1)
    w_in = jnp.take(w_in, jnp.asarray(np.where(cols < 0, w_in.shape[1] - 1, cols)), axis=1).astype(BF16)
    taps = p['rwkv_conv'][l].reshape(9, -1)
    taps = jnp.concatenate([taps, jnp.zeros((9, PA_W - taps.shape[1]), F32)], axis=1)
    wup = jnp.zeros((LANE, 2 * RW), F32)
    aup = jnp.zeros((LANE, 2 * RW), F32)
    for d in range(2):
        wup = wup.at[d * DECAY_RANK:(d + 1) * DECAY_RANK, d * RW:(d + 1) * RW].set(p['rwkv_w_up'][l, d])
        o = 2 * DECAY_RANK + d * ICLR_RANK
        aup = aup.at[o:o + ICLR_RANK, d * RW:(d + 1) * RW].set(p['rwkv_a_up'][l, d])
    gup = jnp.zeros((LANE, RW), F32).at[0:GATE_RANK].set(p['rwkv_g_up'][l])
    row = lambda a: a.reshape(1, -1)
    prep = (wup, aup, gup, row(p['rwkv_w0'][l]), row(p['rwkv_a0'][l]), row(p['rwkv_k_k'][l]),
            row(p['rwkv_k_a'][l]), row(p['rwkv_r_k'][l]), _head_block_ones())
    kd = GLA_HEADS * GLA_K
    gla_aup = jnp.zeros((2, LANE, GLA_KP), F32)
    for d in range(2):
        gla_aup = gla_aup.at[d, d * GLA_RANK:(d + 1) * GLA_RANK, 0:kd].set(p['gla_a_up'][l, d])
    gla_ab = jnp.zeros((2, 1, GLA_KP), F32).at[:, 0, 0:kd].set(p['gla_a_bias'][l])
    lam_re, lam_im = p['s5_lam_re'][l], p['s5_lam_im'][l]
    dt = jnp.exp(p['s5_log_dt'][l])[:, :, None]
    zr, zi = lam_re[:, None, :] * dt, lam_im[:, None, :] * dt
    mag = jnp.exp(zr)
    ab_r, ab_i = mag * jnp.cos(zi), mag * jnp.sin(zi)
    den = (lam_re * lam_re + lam_im * lam_im)[:, None, :]
    f_r = ((ab_r - 1) * lam_re[:, None, :] + ab_i * lam_im[:, None, :]) / den
    f_i = (ab_i * lam_re[:, None, :] - (ab_r - 1) * lam_im[:, None, :]) / den
    b_re, b_im = p['s5_b_re'][l], p['s5_b_im'][l]
    bb_r = f_r[..., None] * b_re - f_i[..., None] * b_im
    bb_i = f_r[..., None] * b_im + f_i[..., None] * b_re
    eye_g = jnp.eye(S5_GROUPS, dtype=F32)
    half = S5_LANES // 2

    def in_blockdiag(bb):
        return jnp.einsum('dgpc,gh->dgchp', bb, eye_g).reshape(2, S5_W, half)

    def out_blockdiag(cc):
        return jnp.einsum('dgcp,gh->dgphc', cc, eye_g).reshape(2, half, S5_W)

    s5_bbd = jnp.concatenate([in_blockdiag(bb_r), in_blockdiag(bb_i)], axis=2).astype(BF16)
    s5_cbd = jnp.concatenate([out_blockdiag(p['s5_c_re'][l]), -out_blockdiag(p['s5_c_im'][l])], axis=1).astype(BF16)
    s5_a = jnp.concatenate([ab_r.reshape(2, 1, half), ab_i.reshape(2, 1, half)], axis=2)
    mix = (_head_block_ones(), row(p['rwkv_ln_g'][l]), row(p['rwkv_ln_b'][l]),
           row(jnp.tile(p['gla_norm_g'][l], GLA_HEADS)), row(p['s5_d'][l]),
           p['s5_glu_w'][l].astype(BF16), row(p['s5_glu_b'][l]), p['w_out'][l].astype(BF16),
           row(p['ln1_g'][l]), row(p['ln1_b'][l]))
    moe = dict(rwt=p['router_w'][l].T,
               bias=jnp.broadcast_to(p['router_bias'][l][:, None], (N_EXPERTS, LANE)),
               w13=p['exp_w13'], w2=p['exp_w2'], layer=l,
               comb=(p['sh_w13'][l].astype(BF16), p['sh_w2'][l].astype(BF16), row(p['ln2_g'][l]), row(p['ln2_b'][l])))
    return dict(w_in=w_in, taps=taps, prep=prep, gla_aup=gla_aup, gla_ab=gla_ab,
                s5_a=s5_a, s5_bbd=s5_bbd, s5_cbd=s5_cbd, mix=mix, moe=moe)


_ARG_NAMES = ('x', 'c', 'ctx', 'c_ctx', 'w_mod', 'b_mod', 'w_in', 'rwkv_conv', 'rwkv_w0', 'rwkv_w_up', 'rwkv_a0',
              'rwkv_a_up', 'rwkv_g_up', 'rwkv_k_k', 'rwkv_k_a', 'rwkv_r_k', 'rwkv_ln_g', 'rwkv_ln_b', 'gla_a_up',
              'gla_a_bias', 'gla_norm_g', 's5_lam_re', 's5_lam_im', 's5_log_dt', 's5_b_re', 's5_b_im', 's5_c_re',
              's5_c_im', 's5_d', 's5_glu_w', 's5_glu_b', 'w_out', 'ln1_g', 'ln1_b', 'router_w', 'router_bias',
              'exp_w13', 'exp_w2', 'sh_w13', 'sh_w2', 'ln2_g', 'ln2_b')


def _tile(n, pref):
    t = pref
    while n % t:
        t //= 2
    return t


def kernel(x, c, ctx, c_ctx, w_mod, b_mod, w_in, rwkv_conv, rwkv_w0, rwkv_w_up, rwkv_a0, rwkv_a_up, rwkv_g_up,
           rwkv_k_k, rwkv_k_a, rwkv_r_k, rwkv_ln_g, rwkv_ln_b, gla_a_up, gla_a_bias, gla_norm_g, s5_lam_re,
           s5_lam_im, s5_log_dt, s5_b_re, s5_b_im, s5_c_re, s5_c_im, s5_d, s5_glu_w, s5_glu_b, w_out, ln1_g,
           ln1_b, router_w, router_bias, exp_w13, exp_w2, sh_w13, sh_w2, ln2_g, ln2_b):
    p = dict(zip(_ARG_NAMES, (x, c, ctx, c_ctx, w_mod, b_mod, w_in, rwkv_conv, rwkv_w0, rwkv_w_up, rwkv_a0,
                              rwkv_a_up, rwkv_g_up, rwkv_k_k, rwkv_k_a, rwkv_r_k, rwkv_ln_g, rwkv_ln_b, gla_a_up,
                              gla_a_bias, gla_norm_g, s5_lam_re, s5_lam_im, s5_log_dt, s5_b_re, s5_b_im, s5_c_re,
                              s5_c_im, s5_d, s5_glu_w, s5_glu_b, w_out, ln1_g, ln1_b, router_w, router_bias,
                              exp_w13, exp_w2, sh_w13, sh_w2, ln2_g, ln2_b)))
    B, T, D = x.shape
    TC = ctx.shape[1]
    L = w_mod.shape[0]
    alpha = (2 * L) ** 0.25
    n_lat, n_ctx = B * T, B * TC
    R = ((B + 1 + SUB - 1) // SUB) * SUB
    cc = jnp.zeros((R, D), F32).at[0:B].set(c).at[B].set(c_ctx)
    mod = mod_table(cc, w_mod, b_mod)
    tm = _tile(T, 256)
    tmc = _tile(n_ctx, 256)
    tmm = min(_tile(T, 128), _tile(n_ctx, 128))
    lat_row = lambda i: (i * tm) // T
    ctx_row = lambda i: B
    x2 = x.reshape(n_lat, D)
    c2 = ctx.reshape(n_ctx, D)
    for l in range(L):
        last = l == L - 1
        lp = _layer_params(l, p)
        mod3 = mod[l].reshape(R, 6, D)
        outs_c, st_c = token_mixers(c2, B, TC, TC, mod3, ctx_row, lp, zero_states(B), tmc)
        outs, _ = token_mixers(x2, B, T, GRID_W, mod3, lat_row, lp, st_c, tm)
        x1, h2, h2t = mix_out(*outs, x2, mod3, lp['mix'], lat_row, tm, alpha)
        if last:
            x2 = moe_ffn(h2, h2t, x1, mod3, lp['moe'], lambda i: (i * tmm) // T, tmm, alpha)
        else:
            c1, hc2, hc2t = mix_out(*outs_c, c2, mod3, lp['mix'], ctx_row, tmc, alpha)
            row_all = lambda i: jnp.where(i * tmm < n_ctx, B, (i * tmm - n_ctx) // T)
            out = moe_ffn(jnp.concatenate([hc2, h2]), jnp.concatenate([hc2t, h2t]), jnp.concatenate([c1, x1]),
                          mod3, lp['moe'], row_all, tmm, alpha)
            c2, x2 = out[:n_ctx], out[n_ctx:]
    return x2.reshape(B, T, D)
```

```python
import functools
import math

import numpy as np
import jax
import jax.numpy as jnp
from jax import lax
from jax.experimental import pallas as pl
from jax.experimental.pallas import tpu as pltpu

F32 = jnp.float32
BF16 = jnp.bfloat16
I32 = jnp.int32

GRID_W = 64
RWKV_HEADS = 6
HEAD = 64
RW = RWKV_HEADS * HEAD
DECAY_RANK = 32
ICLR_RANK = 32
GATE_RANK = 64
RWKV_GN_EPS = 64e-5
GLA_HEADS = 6
GLA_K = 32
GLA_RANK = 16
GLA_TEMP = 16.0
S5_GROUPS = 16
S5_GROUP = 16
S5_STATE = 64
S5_W = S5_GROUPS * S5_GROUP
S5_LANES = 2 * S5_GROUPS * S5_STATE
N_EXPERTS = 256
TOP_K = 8
N_EGROUPS = 8
TOPK_GROUPS = 4
ROUTE_SCALE = 2.5
LN_EPS = 1e-6
CHUNK = 64
MOE_BLK = 256
LANE = 128
SUB = 8
VMEM_LIMIT = 56 * 1024 * 1024

PA_W = 3 * RW + 2 * LANE
PBG_W = 256 + 256 + RW + LANE
NP_W = PA_W + PBG_W + RW + S5_W


def _cp(sem):
    return pltpu.CompilerParams(dimension_semantics=sem, vmem_limit_bytes=VMEM_LIMIT)


def _sigmoid(x):
    return 1.0 / (1.0 + jnp.exp(-x))


def _silu(x):
    return x * _sigmoid(x)


def _bdot(a, b, dims=None):
    a = a.astype(BF16)
    b = b.astype(BF16)
    if dims is None:
        return jnp.dot(a, b, preferred_element_type=F32)
    return lax.dot_general(a, b, dims, preferred_element_type=F32)


def _split2(x):
    hi = x.astype(BF16)
    lo = (x - hi.astype(F32)).astype(BF16)
    return hi, lo


def _split3(x):
    hi = x.astype(BF16)
    r = x - hi.astype(F32)
    mid = r.astype(BF16)
    lo = (r - mid.astype(F32)).astype(BF16)
    return hi, mid, lo


NT_DIMS = (((1,), (1,)), ((), ()))
TN_DIMS = (((0,), (0,)), ((), ()))


def _dot33(a, b, dims=None):
    ah, al = _split2(a)
    bh, bl = _split2(b)
    return _bdot(ah, bh, dims) + (_bdot(ah, bl, dims) + _bdot(al, bh, dims))


def _dot_exact_lhs(m_exact, x, dims=None):
    h, m, l = _split3(x)
    return _bdot(m_exact, h, dims) + (_bdot(m_exact, m, dims) + _bdot(m_exact, l, dims))


def _dot_exact_rhs(x, m_exact, dims=None):
    h, m, l = _split3(x)
    return _bdot(h, m_exact, dims) + (_bdot(m, m_exact, dims) + _bdot(l, m_exact, dims))


def _dot_split_rhs(x, m_exact, dims=None):
    h, l = _split2(x)
    return _bdot(h, m_exact, dims) + _bdot(l, m_exact, dims)


def _layer_norm(x):
    mu = jnp.mean(x, axis=-1, keepdims=True)
    xc = x - mu
    var = jnp.mean(xc * xc, axis=-1, keepdims=True)
    return xc * lax.rsqrt(var + LN_EPS)


def _mod_kernel(c_ref, w_ref, b_ref, o_ref):
    s = _silu(c_ref[...])
    o_ref[0] = _dot33(s, w_ref[0]) + b_ref[0]


def mod_table(cc, w_mod, b_mod):
    L, D, D6 = w_mod.shape
    R = cc.shape[0]
    tn = 1536
    return pl.pallas_call(
        _mod_kernel,
        out_shape=jax.ShapeDtypeStruct((L, R, D6), F32),
        grid=(L, D6 // tn),
        in_specs=[pl.BlockSpec((R, D), lambda l, j: (0, 0)),
                  pl.BlockSpec((1, D, tn), lambda l, j: (l, 0, j)),
                  pl.BlockSpec((1, 1, tn), lambda l, j: (l, 0, j))],
        out_specs=pl.BlockSpec((1, R, tn), lambda l, j: (l, 0, j)),
        compiler_params=_cp(("arbitrary", "arbitrary")),
        name="mod_table",
    )(cc, w_mod, b_mod.reshape(L, 1, D6))


def _inproj_kernel(x_ref, mod_ref, w_ref, pa_ref, pbg_ref, gg_ref, u_ref):
    x = x_ref[...]
    h = _layer_norm(x) * (1.0 + mod_ref[0, 1:2, :]) + mod_ref[0, 0:1, :]
    hb = h.astype(BF16)
    o = 0
    for ref in (pa_ref, pbg_ref, gg_ref, u_ref):
        w = ref.shape[-1]
        ref[...] = jnp.dot(hb, w_ref[:, o:o + w], preferred_element_type=F32)
        o += w


def inproj(x2, mod3, w_bf, row_of_tile, tm):
    N, D = x2.shape
    return pl.pallas_call(
        _inproj_kernel,
        out_shape=[jax.ShapeDtypeStruct((N, PA_W), F32),
                   jax.ShapeDtypeStruct((N, PBG_W), F32),
                   jax.ShapeDtypeStruct((N, RW), F32),
                   jax.ShapeDtypeStruct((N, S5_W), F32)],
        grid=(N // tm,),
        in_specs=[pl.BlockSpec((tm, D), lambda i: (i, 0)),
                  pl.BlockSpec((1, 6, D), lambda i: (row_of_tile(i), 0, 0)),
                  pl.BlockSpec((D, NP_W), lambda i: (0, 0))],
        out_specs=[pl.BlockSpec((tm, PA_W), lambda i: (i, 0)),
                   pl.BlockSpec((tm, PBG_W), lambda i: (i, 0)),
                   pl.BlockSpec((tm, RW), lambda i: (i, 0)),
                   pl.BlockSpec((tm, S5_W), lambda i: (i, 0))],
        compiler_params=_cp(("arbitrary",)),
        name="inproj",
    )(x2, mod3, w_bf)


CONV_PAD = 72


def _conv_kernel(x_ref, taps_ref, o_ref, buf_ref, *, T, W, vertical):
    pad = CONV_PAD
    zeros = jnp.zeros((pad, LANE), F32)
    buf_ref[0:pad, :] = zeros
    buf_ref[pad + T:pad + T + pad, :] = zeros
    buf_ref[pad:pad + T, :] = x_ref[0]
    ch = min(T, 256)
    col = lax.broadcasted_iota(I32, (ch, LANE), 0) & (W - 1)
    left_ok = col >= 1
    right_ok = col <= W - 2
    for c in range(T // ch):
        base = pad + c * ch
        acc = jnp.zeros((ch, LANE), F32)
        for dr in ((0, 1, 2) if vertical else (1,)):
            for dc in range(3):
                off = (dr - 1) * W + (dc - 1)
                v = buf_ref[base + off:base + off + ch, :]
                if dc == 0:
                    v = jnp.where(left_ok, v, 0.0)
                elif dc == 2:
                    v = jnp.where(right_ok, v, 0.0)
                acc = acc + v * taps_ref[3 * dr + dc:3 * dr + dc + 1, :]
        o_ref[0, c * ch:(c + 1) * ch, :] = acc


def short_conv(pa3, taps9, W):
    B, T, C = pa3.shape
    vertical = T > W
    assert W & (W - 1) == 0 and (not vertical or W + 1 <= CONV_PAD)
    kern = functools.partial(_conv_kernel, T=T, W=W, vertical=vertical)
    return pl.pallas_call(
        kern,
        out_shape=jax.ShapeDtypeStruct((B, T, C), F32),
        grid=(B, C // LANE),
        in_specs=[pl.BlockSpec((1, T, LANE), lambda b, j: (b, 0, j)),
                  pl.BlockSpec((9, LANE), lambda b, j: (0, j))],
        out_specs=pl.BlockSpec((1, T, LANE), lambda b, j: (b, 0, j)),
        scratch_shapes=[pltpu.VMEM((T + 2 * CONV_PAD, LANE), F32)],
        compiler_params=_cp(("arbitrary", "arbitrary")),
        name="short_conv",
    )(pa3, taps9)


def _rwkv_prep_kernel(pa_ref, wup_ref, aup_ref, gup_ref, w0_ref, a0_ref, kk_ref, ka_ref, rk_ref, hb_ref,
                      ss_ref, sd_ref, gb_ref):
    r = pa_ref[:, 0:RW]
    k = pa_ref[:, RW:2 * RW]
    v = pa_ref[:, 2 * RW:3 * RW]
    wa = pa_ref[:, 3 * RW:3 * RW + LANE]
    gd = pa_ref[:, 3 * RW + LANE:3 * RW + 2 * LANE]
    z = w0_ref[...] + _dot33(jnp.tanh(wa), wup_ref[...])
    lw = -_sigmoid(z) * math.exp(-0.5)
    a = _sigmoid(a0_ref[...] + _bdot(wa, aup_ref[...]))
    g = _bdot(_sigmoid(gd), gup_ref[...])
    hb = hb_ref[...]
    kk = k * kk_ref[...]
    kk = kk * lax.rsqrt(_dot_split_rhs(kk * kk, hb) + 1e-12)
    ka = ka_ref[...]
    ss_ref[:, 0:RW] = r
    ss_ref[:, RW:2 * RW] = v
    ss_ref[:, 2 * RW:3 * RW] = kk
    rk2 = jnp.zeros_like(r)
    for d in range(2):
        ad = a[:, d * RW:(d + 1) * RW]
        k2 = k * (1.0 + (ad - 1.0) * ka)
        sd_ref[d, :, 0:RW] = lw[:, d * RW:(d + 1) * RW]
        sd_ref[d, :, RW:2 * RW] = k2
        sd_ref[d, :, 2 * RW:3 * RW] = kk * ad
        rk2 = rk2 + r * k2
    bonus = _dot_split_rhs(rk2 * rk_ref[...], hb) * v
    gb_ref[:, 0:RW] = g
    gb_ref[:, RW:2 * RW] = bonus


def rwkv_prep(pa2, wts, tm):
    N = pa2.shape[0]
    full = lambda a: pl.BlockSpec(a.shape, lambda i: (0,) * a.ndim)
    return pl.pallas_call(
        _rwkv_prep_kernel,
        out_shape=[jax.ShapeDtypeStruct((N, 3 * RW), F32),
                   jax.ShapeDtypeStruct((2, N, 3 * RW), F32),
                   jax.ShapeDtypeStruct((N, 2 * RW), F32)],
        grid=(N // tm,),
        in_specs=[pl.BlockSpec((tm, PA_W), lambda i: (i, 0))] + [full(a) for a in wts],
        out_specs=[pl.BlockSpec((tm, 3 * RW), lambda i: (i, 0)),
                   pl.BlockSpec((2, tm, 3 * RW), lambda i: (0, i, 0)),
                   pl.BlockSpec((tm, 2 * RW), lambda i: (i, 0))],
        compiler_params=_cp(("arbitrary",)),
        name="rwkv_prep",
    )(pa2, *wts)


RWKV_GROUP = 8


def _rwkv_scan_kernel(ss_ref, sd_ref, s0_ref, y_ref, sfin_ref, s_scr, *, nsteps, group, reverse):
    n = pl.program_id(1)
    C = CHUNK
    P = 2 * C
    npair = RW // LANE

    @pl.when(n == 0)
    def _():
        s_scr[...] = s0_ref[0]

    row = lax.broadcasted_iota(I32, (P, P), 0)
    col = lax.broadcasted_iota(I32, (P, P), 1)
    same = (row >> 6) == (col >> 6)
    dlt = (col & (C - 1)) - (row & (C - 1)) if reverse else (row & (C - 1)) - (col & (C - 1))
    strict = same & (dlt > 0)
    incl = same & (dlt >= 0)
    eye = (row == col).astype(F32)
    lvl_masks = [((row >> (lvl + 1)) == (col >> (lvl + 1))) & ((row >> lvl) != (col >> lvl)) for lvl in range(6)]
    rc = lax.broadcasted_iota(I32, (C, C), 0)
    cc = lax.broadcasted_iota(I32, (C, C), 1)
    tri = jnp.where((cc >= rc) if reverse else (rc >= cc), 1.0, 0.0).astype(BF16)
    head0 = lax.broadcasted_iota(I32, (C, LANE), 1) < HEAD

    def stack(x):
        return jnp.concatenate([jnp.where(head0, x, 0.0), jnp.where(head0, 0.0, x)], axis=0)

    streams = [(g, p) for g in range(group) for p in range(npair)]
    tm_ = {}
    for (g, p) in streams:
        t0, t1 = g * C, (g + 1) * C
        lo, hi = p * LANE, (p + 1) * LANE
        lw = sd_ref[0, t0:t1, lo:hi]
        tm_[(g, p)] = dict(lw=lw, cl=_dot_exact_lhs(tri, lw))
    for (g, p) in streams:
        t = tm_[(g, p)]
        t0, t1 = g * C, (g + 1) * C
        lo, hi = p * LANE, (p + 1) * LANE
        r = ss_ref[t0:t1, lo:hi]
        v = ss_ref[t0:t1, RW + lo:RW + hi]
        kk = ss_ref[t0:t1, 2 * RW + lo:2 * RW + hi]
        k2 = sd_ref[0, t0:t1, RW + lo:RW + hi]
        b = sd_ref[0, t0:t1, 2 * RW + lo:2 * RW + hi]
        cl, lw = t['cl'], t['lw']
        t['ptot'] = jnp.exp(jnp.sum(lw, axis=0, keepdims=True))
        pinv = jnp.exp(-cl)
        left = jnp.concatenate([stack(-kk * jnp.exp(cl - lw)), stack(r * jnp.exp(cl))], axis=0)
        right = jnp.concatenate([stack(b * pinv), stack(k2 * pinv)], axis=0)
        t['v_st'] = stack(v)
        t['left'] = left.astype(BF16)
        t['bk'] = (right * t['ptot']).astype(BF16)
        aa = _bdot(left, right, NT_DIMS)
        t['nmat'] = jnp.where(strict, aa[0:P, 0:P], 0.0)
        t['a_ak'] = jnp.where(strict, aa[0:P, P:2 * P], 0.0)
        t['a_rbk'] = jnp.concatenate([jnp.where(incl, aa[P:2 * P, 0:P], 0.0),
                                      jnp.where(incl, aa[P:2 * P, P:2 * P], 0.0)], axis=1).astype(BF16)
        t['tinv'] = eye
    for sk in streams:
        t = tm_[sk]
        t['akv'] = _bdot(t['a_ak'], t['v_st'])
    for m in lvl_masks:
        for sk in streams:
            t = tm_[sk]
            t['et'] = _bdot(jnp.where(m, t['nmat'], 0.0), t['tinv'])
        for sk in streams:
            t = tm_[sk]
            t['tinv'] = t['tinv'] + _bdot(t['tinv'], t['et'])
    state = [s_scr[p] for p in range(npair)]
    pairs = range(npair)
    for g in (range(group - 1, -1, -1) if reverse else range(group)):
        ts = [tm_[(g, p)] for p in pairs]
        a_s = [_bdot(ts[p]['left'], state[p], NT_DIMS) for p in pairs]
        u = [_bdot(ts[p]['tinv'], a_s[p][0:P] + ts[p]['akv']) for p in pairs]
        uv = [jnp.concatenate([u[p], ts[p]['v_st']], axis=0) for p in pairs]
        state = [state[p] * ts[p]['ptot'] + _bdot(uv[p], ts[p]['bk'], TN_DIMS) for p in pairs]
        for p in pairs:
            y_st = a_s[p][P:2 * P] + _bdot(ts[p]['a_rbk'], uv[p])
            y_ref[g * C:(g + 1) * C, p * LANE:(p + 1) * LANE] = y_st[0:C] + y_st[C:P]
    for p in pairs:
        s_scr[p] = state[p]

    @pl.when(n == nsteps - 1)
    def _():
        sfin_ref[0] = s_scr[...]


def rwkv_scan_dir(ss, sd, s0, B, T, d):
    nc = T // CHUNK
    group = min(RWKV_GROUP, nc)
    nsteps = nc // group
    npair = RW // LANE
    rows = group * CHUNK
    blk = (lambda b, n: b * nsteps + nsteps - 1 - n) if d else (lambda b, n: b * nsteps + n)
    kern = functools.partial(_rwkv_scan_kernel, nsteps=nsteps, group=group, reverse=bool(d))
    return pl.pallas_call(
        kern,
        out_shape=[jax.ShapeDtypeStruct((B * T, RW), F32),
                   jax.ShapeDtypeStruct((B, npair, LANE, LANE), F32)],
        grid=(B, nsteps),
        in_specs=[pl.BlockSpec((rows, 3 * RW), lambda b, n: (blk(b, n), 0)),
                  pl.BlockSpec((1, rows, 3 * RW), lambda b, n: (d, blk(b, n), 0)),
                  pl.BlockSpec((1, npair, LANE, LANE), lambda b, n: (b, 0, 0, 0))],
        out_specs=[pl.BlockSpec((rows, RW), lambda b, n: (blk(b, n), 0)),
                   pl.BlockSpec((1, npair, LANE, LANE), lambda b, n: (b, 0, 0, 0))],
        scratch_shapes=[pltpu.VMEM((npair, LANE, LANE), F32)],
        compiler_params=_cp(("arbitrary", "arbitrary")),
        name="rwkv_scan_bwd" if d else "rwkv_scan_fwd",
    )(ss, sd, s0)


def rwkv_scan(ss, sd, s0, B, T):
    y0, f0 = rwkv_scan_dir(ss, sd, s0[0], B, T, 0)
    y1, f1 = rwkv_scan_dir(ss, sd, s0[1], B, T, 1)
    return (y0, y1), (f0, f1)


GLA_KP = 256


GLA_GROUP = 8


def _gla_kernel(pbg_ref, aup_ref, ab_ref, s0_ref, o_ref, sfin_ref, s_scr, *, nsteps, group, reverse):
    n = pl.program_id(1)
    C = CHUNK
    R = group * C

    @pl.when(n == 0)
    def _():
        s_scr[...] = s0_ref[0]

    q = pbg_ref[:, 0:GLA_KP]
    k = pbg_ref[:, GLA_KP:2 * GLA_KP]
    v = pbg_ref[:, 2 * GLA_KP:2 * GLA_KP + RW]
    ad = pbg_ref[:, 2 * GLA_KP + RW:2 * GLA_KP + RW + LANE]
    x = _dot33(ad, aup_ref[0]) + ab_ref[0]
    la = (jnp.minimum(x, 0.0) - jnp.log(1.0 + jnp.exp(-jnp.abs(x)))) * (1.0 / GLA_TEMP)
    chunks = range(group)
    sl = [slice(g * C, (g + 1) * C) for g in chunks]
    rr = lax.broadcasted_iota(I32, (C, C), 0)
    rc = lax.broadcasted_iota(I32, (C, C), 1)
    tri = jnp.where((rc >= rr) if reverse else (rr >= rc), 1.0, 0.0).astype(BF16)
    bcums = [_dot_exact_lhs(tri, la[sl[g]]) for g in chunks]
    last = 0 if reverse else C - 1
    tots = [bc[last:last + 1, :] for bc in bcums]
    bcum = jnp.concatenate(bcums, axis=0)
    tot = jnp.concatenate([jnp.broadcast_to(t, (C, GLA_KP)) for t in tots], axis=0)
    q_in = q * jnp.exp(bcum) * (GLA_K ** -0.5)
    k_in = k * jnp.exp(-bcum)
    k_st = k * jnp.exp(tot - bcum)
    dn = [jnp.exp(t) for t in tots]
    klane = lax.broadcasted_iota(I32, (C, GLA_KP), 1)
    rt = lax.broadcasted_iota(I32, (GLA_HEADS * C, C), 0) & (C - 1)
    ct = lax.broadcasted_iota(I32, (GLA_HEADS * C, C), 1)
    causal = (ct >= rt) if reverse else (rt >= ct)
    vlane = lax.broadcasted_iota(I32, (C, RW), 1)
    sv = lax.broadcasted_iota(I32, (RW, GLA_KP), 0) >> 6
    sk = lax.broadcasted_iota(I32, (RW, GLA_KP), 1) >> 5
    q_rows = [jnp.concatenate([jnp.where((klane >> 5) == h, q_in[sl[g]], 0.0) for h in range(GLA_HEADS)],
                              axis=0).astype(BF16) for g in chunks]
    att = [jnp.where(causal, _bdot(q_rows[g], k_in[sl[g]], NT_DIMS), 0.0) for g in chunks]
    o_rows = [_bdot(att[g], v[sl[g]]) for g in chunks]
    kv = [jnp.where(sv == sk, _bdot(v[sl[g]], k_st[sl[g]], TN_DIMS), 0.0) for g in chunks]
    s = s_scr[...]
    for g in (reversed(chunks) if reverse else chunks):
        o = _bdot(q_in[sl[g]], s, NT_DIMS)
        for h in range(GLA_HEADS):
            o = o + jnp.where((vlane >> 6) == h, o_rows[g][h * C:(h + 1) * C], 0.0)
        o_ref[sl[g], :] = o
        s = s * dn[g] + kv[g]
    s_scr[...] = s

    @pl.when(n == nsteps - 1)
    def _():
        sfin_ref[0] = s_scr[...]


def gla_scan_dir(pbg, aup, ab, s0, B, T, d):
    nc = T // CHUNK
    group = min(GLA_GROUP, nc)
    nsteps = nc // group
    rows = group * CHUNK
    blk = (lambda b, n: b * nsteps + nsteps - 1 - n) if d else (lambda b, n: b * nsteps + n)
    kern = functools.partial(_gla_kernel, nsteps=nsteps, group=group, reverse=bool(d))
    return pl.pallas_call(
        kern,
        out_shape=[jax.ShapeDtypeStruct((B * T, RW), F32),
                   jax.ShapeDtypeStruct((B, RW, GLA_KP), F32)],
        grid=(B, nsteps),
        in_specs=[pl.BlockSpec((rows, PBG_W), lambda b, n: (blk(b, n), 0)),
                  pl.BlockSpec((1, LANE, GLA_KP), lambda b, n: (d, 0, 0)),
                  pl.BlockSpec((1, 1, GLA_KP), lambda b, n: (d, 0, 0)),
                  pl.BlockSpec((1, RW, GLA_KP), lambda b, n: (b, 0, 0))],
        out_specs=[pl.BlockSpec((rows, RW), lambda b, n: (blk(b, n), 0)),
                   pl.BlockSpec((1, RW, GLA_KP), lambda b, n: (b, 0, 0))],
        scratch_shapes=[pltpu.VMEM((RW, GLA_KP), F32)],
        compiler_params=_cp(("arbitrary", "arbitrary")),
        name="gla_scan_bwd" if d else "gla_scan_fwd",
    )(pbg, aup, ab, s0)


def gla_scan(pbg, aup, ab, s0, B, T):
    o0, f0 = gla_scan_dir(pbg, aup, ab, s0[0], B, T, 0)
    o1, f1 = gla_scan_dir(pbg, aup, ab, s0[1], B, T, 1)
    return (o0, o1), (f0, f1)


S5_PITCH = 72


def _s5_kernel(u_ref, a_ref, bbd_ref, cbd_ref, s0_ref, y_ref, sfin_ref, x_scr, st_scr, *, nc, nb):
    d = pl.program_id(0)
    n = pl.program_id(1)
    C = CHUNK
    half = S5_LANES // 2

    @pl.when(n == 0)
    def _():
        st_scr[...] = s0_ref[0]

    nslab = S5_LANES // LANE
    bbd = bbd_ref[0]
    for b in range(nb):
        bu = _bdot(u_ref[b], bbd)
        for j in range(nslab):
            x_scr[j, b * S5_PITCH:b * S5_PITCH + C, :] = bu[:, j * LANE:(j + 1) * LANE]
    a_re = a_ref[0, :, 0:half]
    a_im = a_ref[0, :, half:S5_LANES]

    def step(i, carry):
        re, im = carry
        t = i + d * (C - 1 - 2 * i)
        rows = pl.ds(t, nb, stride=S5_PITCH)
        bu = jnp.concatenate([x_scr[j, rows, :] for j in range(nslab)], axis=1)
        nre = a_re * re - a_im * im + bu[:, 0:half]
        nim = a_re * im + a_im * re + bu[:, half:S5_LANES]
        for j in range(nslab // 2):
            x_scr[j, rows, :] = nre[:, j * LANE:(j + 1) * LANE]
            x_scr[nslab // 2 + j, rows, :] = nim[:, j * LANE:(j + 1) * LANE]
        return nre, nim

    st = st_scr[...]
    re, im = lax.fori_loop(0, C, step, (st[:, 0:half], st[:, half:S5_LANES]))
    st_scr[...] = jnp.concatenate([re, im], axis=1)
    cbd = cbd_ref[0]
    for b in range(nb):
        xs = jnp.concatenate([x_scr[j, b * S5_PITCH:b * S5_PITCH + C, :] for j in range(nslab)], axis=1)
        y_ref[0, b] = _bdot(xs, cbd)

    @pl.when(n == nc - 1)
    def _():
        sfin_ref[0] = st_scr[...]


def s5_scan(u3, a_bar, bbd, cbd, s0):
    B, T, _ = u3.shape
    nc = T // CHUNK
    chunk = lambda d, n: n + d * (nc - 1 - 2 * n)
    kern = functools.partial(_s5_kernel, nc=nc, nb=B)
    return pl.pallas_call(
        kern,
        out_shape=[jax.ShapeDtypeStruct((2, B, T, S5_W), F32),
                   jax.ShapeDtypeStruct((2, B, S5_LANES), F32)],
        grid=(2, nc),
        in_specs=[pl.BlockSpec((B, CHUNK, S5_W), lambda d, n: (0, chunk(d, n), 0)),
                  pl.BlockSpec((1, 1, S5_LANES), lambda d, n: (d, 0, 0)),
                  pl.BlockSpec((1, S5_W, S5_LANES), lambda d, n: (d, 0, 0)),
                  pl.BlockSpec((1, S5_LANES, S5_W), lambda d, n: (d, 0, 0)),
                  pl.BlockSpec((1, B, S5_LANES), lambda d, n: (d, 0, 0))],
        out_specs=[pl.BlockSpec((1, B, CHUNK, S5_W), lambda d, n: (d, 0, chunk(d, n), 0)),
                   pl.BlockSpec((1, B, S5_LANES), lambda d, n: (d, 0, 0))],
        scratch_shapes=[pltpu.VMEM((S5_LANES // LANE, B * S5_PITCH, LANE), F32),
                        pltpu.VMEM((B, S5_LANES), F32)],
        compiler_params=_cp(("arbitrary", "arbitrary")),
        name="s5_scan",
    )(u3, a_bar, bbd, cbd, s0)


ROWT = 4
U32 = jnp.uint32


def _pack_bf16_pair(a, b):
    au = lax.bitcast_convert_type(a.astype(BF16).astype(F32), U32)
    bu = lax.bitcast_convert_type(b.astype(BF16).astype(F32), U32)
    return (au >> 16) | bu


def _unpack_bf16_pair(w):
    return (lax.bitcast_convert_type(w << 16, F32), lax.bitcast_convert_type(w & jnp.uint32(0xFFFF0000), F32))


def _tile_rows_store(ref, val):
    tm = val.shape[0]
    assert val.shape[1] == 2 * ROWT * LANE
    for s in range(ROWT):
        ref[pl.ds(s, tm, stride=ROWT), :] = _pack_bf16_pair(val[:, 2 * s * LANE:(2 * s + 1) * LANE],
                                                            val[:, (2 * s + 1) * LANE:(2 * s + 2) * LANE])


def _tile_rows_load(ref, tm, base=0):
    parts = []
    for s in range(ROWT):
        parts.extend(_unpack_bf16_pair(ref[pl.ds(base + s, tm, stride=ROWT), :]))
    return jnp.concatenate(parts, axis=1)


def _mix_out_kernel(y0_ref, y1_ref, gb_ref, o0_ref, o1_ref, gg_ref, yd_ref, u_ref, x_ref, mod_ref,
                    hb_ref, lng_ref, lnb_ref, gng_ref, s5d_ref, gluw_ref, glub_ref, wout_ref, l1g_ref, l1b_ref,
                    x1_ref, h2_ref, h2t_ref, *, alpha):
    hb = hb_ref[...]
    inv = 1.0 / HEAD
    y = y0_ref[...] + y1_ref[...]
    yc = y - _dot_split_rhs(y, hb) * inv
    var = _dot_split_rhs(yc * yc, hb) * inv
    gn = yc * lax.rsqrt(var + RWKV_GN_EPS) * lng_ref[...] + lnb_ref[...]
    y_a = (gn + gb_ref[:, RW:2 * RW]) * gb_ref[:, 0:RW]
    o = o0_ref[...] + o1_ref[...]
    o = o * lax.rsqrt(_dot_split_rhs(o * o, hb) * inv + 1e-6) * gng_ref[...]
    y_b = o * _silu(gg_ref[...])
    c = s5d_ref[...] * u_ref[...] + yd_ref[0] + yd_ref[1]
    c = 0.5 * c * (1.0 + jnp.tanh(math.sqrt(2.0 / math.pi) * (c + 0.044715 * (c * c * c))))
    y_c = c * _sigmoid(_bdot(c, gluw_ref[...]) + glub_ref[...])
    y_mix = (_bdot(y_a, wout_ref[0:RW, :]) + _bdot(y_b, wout_ref[RW:2 * RW, :])
             + _bdot(y_c, wout_ref[2 * RW:2 * RW + S5_W, :]))
    x1 = _layer_norm(alpha * x_ref[...] + mod_ref[0, 2:3, :] * y_mix) * l1g_ref[...] + l1b_ref[...]
    x1_ref[...] = x1
    h2 = _layer_norm(x1) * (1.0 + mod_ref[0, 4:5, :]) + mod_ref[0, 3:4, :]
    h2_ref[...] = h2
    _tile_rows_store(h2t_ref, h2)


def mix_out(y01, gb, o01, gg, yd, u, x2, mod3, wts, row_of_tile, tm, alpha):
    N, D = x2.shape
    full = lambda a: pl.BlockSpec(a.shape, lambda i: (0,) * a.ndim)
    kern = functools.partial(_mix_out_kernel, alpha=alpha)
    return pl.pallas_call(
        kern,
        out_shape=[jax.ShapeDtypeStruct((N, D), F32),
                   jax.ShapeDtypeStruct((N, D), F32),
                   jax.ShapeDtypeStruct((N * ROWT, LANE), U32)],
        grid=(N // tm,),
        in_specs=[pl.BlockSpec((tm, RW), lambda i: (i, 0)),
                  pl.BlockSpec((tm, RW), lambda i: (i, 0)),
                  pl.BlockSpec((tm, 2 * RW), lambda i: (i, 0)),
                  pl.BlockSpec((tm, RW), lambda i: (i, 0)),
                  pl.BlockSpec((tm, RW), lambda i: (i, 0)),
                  pl.BlockSpec((tm, RW), lambda i: (i, 0)),
                  pl.BlockSpec((2, tm, S5_W), lambda i: (0, i, 0)),
                  pl.BlockSpec((tm, S5_W), lambda i: (i, 0)),
                  pl.BlockSpec((tm, D), lambda i: (i, 0)),
                  pl.BlockSpec((1, 6, D), lambda i: (row_of_tile(i), 0, 0))] + [full(a) for a in wts],
        out_specs=[pl.BlockSpec((tm, D), lambda i: (i, 0)),
                   pl.BlockSpec((tm, D), lambda i: (i, 0)),
                   pl.BlockSpec((tm * ROWT, LANE), lambda i: (i, 0))],
        compiler_params=_cp(("arbitrary",)),
        name="mix_out",
    )(y01[0], y01[1], gb, o01[0], o01[1], gg, yd, u, x2, mod3, *wts)


def _first_max(x, idx, big):
    m = jnp.max(x, axis=0, keepdims=True)
    first = jnp.min(jnp.where(x == m, idx, big), axis=0, keepdims=True)
    return m, first


def _router_kernel(h_ref, rwt_ref, bias_ref, e_ref, w_ref):
    tm = h_ref.shape[0]
    gsz = N_EXPERTS // N_EGROUPS
    ninf = -jnp.inf
    s = _sigmoid(_dot33(rwt_ref[...], h_ref[...], NT_DIMS))
    ssel = s + bias_ref[:, 0:1]
    gi = lax.broadcasted_iota(I32, (gsz, tm), 0)
    gscore = []
    for g in range(N_EGROUPS):
        xg = ssel[g * gsz:(g + 1) * gsz, :]
        m1, i1 = _first_max(xg, gi, gsz)
        m2 = jnp.max(jnp.where(gi == i1, ninf, xg), axis=0, keepdims=True)
        gscore.append(m1 + m2)
    cur = jnp.concatenate(gscore, axis=0)
    gidx = lax.broadcasted_iota(I32, (N_EGROUPS, tm), 0)
    picked = jnp.zeros((N_EGROUPS, tm), F32)
    for _ in range(TOPK_GROUPS):
        _, first = _first_max(cur, gidx, N_EGROUPS)
        hit = gidx == first
        picked = jnp.where(hit, 1.0, picked)
        cur = jnp.where(hit, ninf, cur)
    x = jnp.concatenate(
        [jnp.where(picked[g:g + 1, :] > 0.5, ssel[g * gsz:(g + 1) * gsz, :], ninf) for g in range(N_EGROUPS)], axis=0)
    ei = lax.broadcasted_iota(I32, (N_EXPERTS, tm), 0)
    idxs, ws = [], []
    for _ in range(TOP_K):
        _, first = _first_max(x, ei, N_EXPERTS)
        hit = ei == first
        idxs.append(first)
        ws.append(jnp.sum(jnp.where(hit, s, 0.0), axis=0, keepdims=True))
        x = jnp.where(hit, ninf, x)
    w = jnp.concatenate(ws, axis=0)
    e_ref[...] = jnp.concatenate(idxs, axis=0)
    w_ref[...] = w / jnp.sum(w, axis=0, keepdims=True) * ROUTE_SCALE


def moe_router(h2, rwt, bias_b, tm):
    N, D = h2.shape
    return pl.pallas_call(
        _router_kernel,
        out_shape=[jax.ShapeDtypeStruct((TOP_K, N), I32), jax.ShapeDtypeStruct((TOP_K, N), F32)],
        grid=(N // tm,),
        in_specs=[pl.BlockSpec((tm, D), lambda i: (i, 0)),
                  pl.BlockSpec((N_EXPERTS, D), lambda i: (0, 0)),
                  pl.BlockSpec((N_EXPERTS, LANE), lambda i: (0, 0))],
        out_specs=[pl.BlockSpec((TOP_K, tm), lambda i: (0, i)),
                   pl.BlockSpec((TOP_K, tm), lambda i: (0, i))],
        compiler_params=_cp(("arbitrary",)),
        name="moe_router",
    )(h2, rwt, bias_b)


def _moe_count_kernel(e_ref, pstart_ref, plan_ref, blk_ref, cnt_scr, *, nt, nbp):
    i = pl.program_id(0)
    tp = e_ref.shape[1]

    @pl.when(i == 0)
    def _():
        cnt_scr[...] = jnp.zeros_like(cnt_scr)

    ei = lax.broadcasted_iota(I32, (N_EXPERTS, tp), 0)
    acc = jnp.zeros((N_EXPERTS, 1), F32)
    for k in range(TOP_K):
        acc = acc + jnp.sum(jnp.where(ei == e_ref[k:k + 1, :], 1.0, 0.0), axis=1, keepdims=True)
    cnt_scr[...] = cnt_scr[...] + acc

    @pl.when(i == nt - 1)
    def _():
        cnt = cnt_scr[...].astype(I32)
        padded = (cnt + (MOE_BLK - 1)) & (-MOE_BLK)
        r = lax.broadcasted_iota(I32, (N_EXPERTS, N_EXPERTS), 0)
        c = lax.broadcasted_iota(I32, (N_EXPERTS, N_EXPERTS), 1)
        tri = jnp.where(c <= r, 1.0, 0.0).astype(BF16)
        padded_b = jnp.broadcast_to(padded.astype(F32), (N_EXPERTS, LANE))
        p_end = _dot_exact_lhs(tri, padded_b)
        pstart = p_end - padded_b
        pstart_ref[...] = pstart.astype(I32)
        diag = r == c
        ps_row = jnp.sum(jnp.where(diag, pstart[:, 0:1], 0.0), axis=0, keepdims=True)
        cnt_row = jnp.sum(jnp.where(diag, cnt_scr[...], 0.0), axis=0, keepdims=True)
        plan_ref[...] = jnp.concatenate([ps_row, cnt_row, jnp.zeros((SUB - 2, N_EXPERTS), F32)], axis=0).astype(I32)
        lim =(lax.broadcasted_iota(I32, (N_EXPERTS, nbp), 1) * MOE_BLK).astype(F32)
        be = jnp.sum(jnp.where(p_end[:, 0:1] <= lim, 1.0, 0.0), axis=0, keepdims=True)
        be = jnp.minimum(be, N_EXPERTS - 1.0)
        nused = jnp.max(p_end[:, 0:1], axis=0, keepdims=True) * (1.0 / MOE_BLK)
        blk_ref[...] = jnp.concatenate([jnp.broadcast_to(be, (SUB // 2, nbp)),
                                        jnp.broadcast_to(nused, (SUB // 2, nbp))], axis=0).astype(I32)


def moe_counts(eidx, tp, nbp):
    N = eidx.shape[1]
    nt = N // tp
    kern = functools.partial(_moe_count_kernel, nt=nt, nbp=nbp)
    return pl.pallas_call(
        kern,
        out_shape=[jax.ShapeDtypeStruct((N_EXPERTS, LANE), I32),
                   jax.ShapeDtypeStruct((SUB, N_EXPERTS), I32),
                   jax.ShapeDtypeStruct((SUB, nbp), I32)],
        grid=(nt,),
        in_specs=[pl.BlockSpec((TOP_K, tp), lambda i: (0, i))],
        out_specs=[pl.BlockSpec((N_EXPERTS, LANE), lambda i: (0, 0)),
                   pl.BlockSpec((SUB, N_EXPERTS), lambda i: (0, 0)),
                   pl.BlockSpec((SUB, nbp), lambda i: (0, 0))],
        scratch_shapes=[pltpu.VMEM((N_EXPERTS, 1), F32)],
        compiler_params=_cp(("arbitrary",)),
        name="moe_counts",
    )(eidx)


def _moe_dest_kernel(e_ref, pstart_ref, dest_ref, base_scr):
    i = pl.program_id(0)
    tp = e_ref.shape[1]

    @pl.when(i == 0)
    def _():
        base_scr[...] = pstart_ref[:, 0:1].astype(F32)

    ei = lax.broadcasted_iota(I32, (N_EXPERTS, tp), 0)
    r = lax.broadcasted_iota(I32, (tp, tp), 0)
    c = lax.broadcasted_iota(I32, (tp, tp), 1)
    tri = jnp.where(r <= c, 1.0, 0.0).astype(BF16)
    base = base_scr[...]
    rows = []
    for k in range(TOP_K):
        hit = ei == e_ref[k:k + 1, :]
        oh = jnp.where(hit, 1.0, 0.0)
        cum = _bdot(oh, tri)
        rows.append(jnp.sum(jnp.where(hit, cum - 1.0 + base, 0.0), axis=0, keepdims=True))
        base = base + cum[:, tp - 1:tp]
    base_scr[...] = base
    dest_ref[...] = jnp.concatenate(rows, axis=0).astype(I32)


def moe_dest(eidx, pstart, tp):
    N = eidx.shape[1]
    return pl.pallas_call(
        _moe_dest_kernel,
        out_shape=jax.ShapeDtypeStruct((TOP_K, N), I32),
        grid=(N // tp,),
        in_specs=[pl.BlockSpec((TOP_K, tp), lambda i: (0, i)),
                  pl.BlockSpec((N_EXPERTS, LANE), lambda i: (0, 0))],
        out_specs=pl.BlockSpec((TOP_K, tp), lambda i: (0, i)),
        scratch_shapes=[pltpu.VMEM((N_EXPERTS, 1), F32)],
        compiler_params=_cp(("arbitrary",)),
        name="moe_dest",
    )(eidx, pstart)


def _tile_at(ref, token):
    return ref.at[pl.ds(pl.multiple_of(token * ROWT, ROWT), ROWT), :]


def _moe_scatter_kernel(dest_ref, h_ref, xs_hbm, sem):
    tp = dest_ref.shape[1]

    def issue(j, carry):
        for k in range(TOP_K):
            pltpu.make_async_copy(_tile_at(h_ref, j), _tile_at(xs_hbm, dest_ref[k, j]), sem).start(priority=k % 2)
        return carry

    lax.fori_loop(0, tp, issue, 0)

    def drain(j, carry):
        for k in range(TOP_K):
            pltpu.make_async_copy(h_ref.at[pl.ds(0, ROWT), :], xs_hbm.at[pl.ds(0, ROWT), :], sem).wait()
        return carry

    lax.fori_loop(0, tp, drain, 0)


def moe_scatter(dest, h2t, n_slots, tp):
    N = dest.shape[1]
    return pl.pallas_call(
        _moe_scatter_kernel,
        out_shape=jax.ShapeDtypeStruct((n_slots * ROWT, LANE), U32),
        grid=(N // tp,),
        in_specs=[pl.BlockSpec((TOP_K, tp), lambda i: (0, i), memory_space=pltpu.SMEM),
                  pl.BlockSpec((tp * ROWT, LANE), lambda i: (i, 0))],
        out_specs=pl.BlockSpec(memory_space=pl.ANY),
        scratch_shapes=[pltpu.SemaphoreType.DMA(())],
        compiler_params=_cp(("arbitrary",)),
        name="moe_scatter",
    )(dest, h2t)


PAD_BITS = tuple(1 << b for b in reversed(range(MOE_BLK.bit_length() - 1)))


def _moe_padfill_kernel(ps_ref, xs_in, xs_hbm, zero_scr, sem):
    del xs_in
    zero_scr[...] = jnp.zeros_like(zero_scr)

    def pad_copies(e, wait):
        cnt = ps_ref[1, e]
        npad = ((cnt + (MOE_BLK - 1)) & (-MOE_BLK)) - cnt
        off = ps_ref[0, e] + cnt
        for bit in PAD_BITS:
            @pl.when((npad & bit) != 0)
            def _():
                cp = pltpu.make_async_copy(zero_scr.at[pl.ds(0, bit * ROWT), :],
                                           xs_hbm.at[pl.ds(pl.multiple_of(off * ROWT, ROWT), bit * ROWT), :], sem)
                if wait:
                    cp.wait()
                else:
                    cp.start()
            off = off + (npad & bit)

    def issue(e, carry):
        pad_copies(e, False)
        return carry

    def drain(e, carry):
        pad_copies(e, True)
        return carry

    lax.fori_loop(0, N_EXPERTS, issue, 0)
    lax.fori_loop(0, N_EXPERTS, drain, 0)


def moe_padfill(pstart, xs):
    return pl.pallas_call(
        _moe_padfill_kernel,
        out_shape=jax.ShapeDtypeStruct(xs.shape, xs.dtype),
        grid=(1,),
        in_specs=[pl.BlockSpec(memory_space=pltpu.SMEM),
                  pl.BlockSpec(memory_space=pl.ANY)],
        out_specs=pl.BlockSpec(memory_space=pl.ANY),
        scratch_shapes=[pltpu.VMEM((PAD_BITS[0] * ROWT, LANE), U32), pltpu.SemaphoreType.DMA(())],
        input_output_aliases={1: 0},
        compiler_params=_cp(("arbitrary",)),
        name="moe_padfill",
    )(pstart, xs)


def _experts_kernel(blk_ref, xs_ref, w13_ref, w2_ref, y_ref, w13_bf, w2_bf):
    i = pl.program_id(0)
    nb = xs_ref.shape[0] // ROWT
    ff = w2_bf.shape[0]

    @pl.when(i < blk_ref[SUB // 2, 0])
    def _():
        prev = blk_ref[0, jnp.maximum(i - 1, 0)]

        @pl.when((i == 0) | (blk_ref[0, i] != prev))
        def _():
            w13_bf[...] = w13_ref[0, 0].astype(BF16)
            w2_bf[...] = w2_ref[0, 0].astype(BF16)

        x = _tile_rows_load(xs_ref, nb).astype(BF16)
        h = jnp.dot(x, w13_bf[...], preferred_element_type=F32)
        act = (_silu(h[:, 0:ff]) * h[:, ff:2 * ff]).astype(BF16)
        _tile_rows_store(y_ref, jnp.dot(act, w2_bf[...], preferred_element_type=F32))


def moe_experts(blk, xs, w13, w2, layer, nb_total):
    _, E, D, F2 = w13.shape
    last = lambda i, b: jnp.minimum(i, b[SUB // 2, 0] - 1)
    grid_spec = pltpu.PrefetchScalarGridSpec(
        num_scalar_prefetch=1,
        grid=(nb_total,),
        in_specs=[pl.BlockSpec((MOE_BLK * ROWT, LANE), lambda i, b: (last(i, b), 0)),
                  pl.BlockSpec((1, 1, D, F2), lambda i, b: (layer, b[0, last(i, b)], 0, 0)),
                  pl.BlockSpec((1, 1, F2 // 2, D), lambda i, b: (layer, b[0, last(i, b)], 0, 0))],
        out_specs=pl.BlockSpec((MOE_BLK * ROWT, LANE), lambda i, b: (last(i, b), 0)),
        scratch_shapes=[pltpu.VMEM((D, F2), BF16), pltpu.VMEM((F2 // 2, D), BF16)])
    return pl.pallas_call(
        _experts_kernel,
        out_shape=jax.ShapeDtypeStruct(xs.shape, U32),
        grid_spec=grid_spec,
        compiler_params=_cp(("arbitrary",)),
        name="moe_experts",
    )(blk, xs, w13, w2)


def _moe_combine_kernel(dest_ref, wt_ref, h_ref, x_ref, mod_ref, s13_ref, s2_ref, l2g_ref, l2b_ref, y_hbm,
                        o_ref, g_scr, sem, *, alpha, nt):
    s = pl.program_id(0)
    tm = h_ref.shape[0]
    ff = s2_ref.shape[0]
    rows = tm * ROWT
    slot_g = s % 2
    slot_c = 1 - slot_g

    def slot_copy(slot):
        return pltpu.make_async_copy(y_hbm.at[pl.ds(0, TOP_K * rows), :], g_scr.at[slot], sem.at[slot])

    @pl.when(s == 0)
    def _():
        g_scr[1] = jnp.zeros((TOP_K * rows, LANE), U32)

    @pl.when(s > 0)
    def _():
        slot_copy(slot_c).wait()

    for j in range(tm):
        for k in range(TOP_K):
            pltpu.make_async_copy(_tile_at(y_hbm, dest_ref[k, j]),
                                  g_scr.at[slot_g, pl.ds(k * rows + j * ROWT, ROWT), :],
                                  sem.at[slot_g]).start(priority=k % 2)
    hs = _bdot(h_ref[...], s13_ref[...])
    f = _bdot(_silu(hs[:, 0:ff]) * hs[:, ff:2 * ff], s2_ref[...])
    for k in range(TOP_K):
        f = f + wt_ref[:, k:k + 1] * _tile_rows_load(g_scr.at[slot_c], tm, base=k * rows)
    o_ref[...] = (_layer_norm(alpha * x_ref[...] + mod_ref[0, 5:6, :] * f) * l2g_ref[...] + l2b_ref[...])

    @pl.when(s == nt)
    def _():
        slot_copy(slot_g).wait()


def moe_combine(dest, wt_t, h2, x1, mod3, wts, y, row_of_tile, tm, alpha):
    N, D = h2.shape
    nt = N // tm
    full = lambda a: pl.BlockSpec(a.shape, lambda s: (0,) * a.ndim)
    kern = functools.partial(_moe_combine_kernel, alpha=alpha, nt=nt)
    nxt = lambda s: jnp.minimum(s, nt - 1)
    cur = lambda s: jnp.maximum(s - 1, 0)
    return pl.pallas_call(
        kern,
        out_shape=jax.ShapeDtypeStruct((N, D), F32),
        grid=(nt + 1,),
        in_specs=[pl.BlockSpec((TOP_K, tm), lambda s: (0, nxt(s)), memory_space=pltpu.SMEM),
                  pl.BlockSpec((tm, TOP_K), lambda s: (cur(s), 0)),
                  pl.BlockSpec((tm, D), lambda s: (cur(s), 0)),
                  pl.BlockSpec((tm, D), lambda s: (cur(s), 0)),
                  pl.BlockSpec((1, 6, D), lambda s: (row_of_tile(cur(s)), 0, 0))]
                 + [full(a) for a in wts] + [pl.BlockSpec(memory_space=pl.ANY)],
        out_specs=pl.BlockSpec((tm, D), lambda s: (cur(s), 0)),
        scratch_shapes=[pltpu.VMEM((2, TOP_K * tm * ROWT, LANE), U32), pltpu.SemaphoreType.DMA((2,))],
        compiler_params=_cp(("arbitrary",)),
        name="moe_combine",
    )(dest, wt_t, h2, x1, mod3, *wts, y)


def moe_ffn(h2, h2t, x1, mod3, mp, row_of_tile, tm, alpha):
    N = h2.shape[0]
    nb_total = (N * TOP_K + MOE_BLK - 1) // MOE_BLK + N_EXPERTS
    nbp = ((nb_total + LANE - 1) // LANE) * LANE
    eidx, wts = moe_router(h2, mp['rwt'], mp['bias'], tm)
    pstart, plan, blk = moe_counts(eidx, tm, nbp)
    dest = moe_dest(eidx, pstart, tm)
    xs = moe_scatter(dest, h2t, nb_total * MOE_BLK, tm)
    xs = moe_padfill(plan, xs)
    y = moe_experts(blk, xs, mp['w13'], mp['w2'], mp['layer'], nb_total)
    return moe_combine(dest, wts.T, h2, x1, mod3, mp['comb'], y, row_of_tile, tm, alpha)


def token_mixers(x2, B, T, W, mod3, row_of_tile, lp, states, tm):
    pa, pbg, gg, u = inproj(x2, mod3, lp['w_in'], row_of_tile, tm)
    pac = short_conv(pa.reshape(B, T, PA_W), lp['taps'], W).reshape(B * T, PA_W)
    ss, sd, gb = rwkv_prep(pac, lp['prep'], tm)
    s_rwkv, s_gla, s_s5 = states
    y2, f_rwkv = rwkv_scan(ss, sd, s_rwkv, B, T)
    o2, f_gla = gla_scan(pbg, lp['gla_aup'], lp['gla_ab'], s_gla, B, T)
    yd, f_s5 = s5_scan(u.reshape(B, T, S5_W), lp['s5_a'], lp['s5_bbd'], lp['s5_cbd'], s_s5)
    return (y2, gb, o2, gg, yd.reshape(2, B * T, S5_W), u), (f_rwkv, f_gla, f_s5)


def zero_states(B):
    return ((jnp.zeros((B, RW // LANE, LANE, LANE), F32),) * 2,
            (jnp.zeros((B, RW, GLA_KP), F32),) * 2,
            jnp.zeros((2, B, S5_LANES), F32))


def _inproj_columns():
    r_cols = 3 * RW + 2 * DECAY_RANK + 2 * ICLR_RANK + GATE_RANK
    kd = GLA_HEADS * GLA_K
    gq, gk, gv = r_cols, r_cols + kd, r_cols + 2 * kd
    gg = gv + RW
    gad = gg + RW
    pc = gad + 2 * GLA_RANK
    z = lambda n: [-1] * n
    cols = list(range(0, r_cols)) + z(PA_W - r_cols)
    cols += list(range(gq, gq + kd)) + z(256 - kd)
    cols += list(range(gk, gk + kd)) + z(256 - kd)
    cols += list(range(gv, gv + RW))
    cols += list(range(gad, gad + 2 * GLA_RANK)) + z(LANE - 2 * GLA_RANK)
    cols += list(range(gg, gg + RW))
    cols += list(range(pc, pc + S5_W))
    assert len(cols) == NP_W
    return np.asarray(cols, np.int32)


def _head_block_ones():
    h = np.arange(RW) // HEAD
    return jnp.asarray(h[:, None] == h[None, :], BF16)


def _layer_params(l, p):
    D = p['w_in'].shape[1]
    cols = _inproj_columns()
    w_in = jnp.concatenate([p['w_in'][l], jnp.zeros((D, 1), F32)], axis=1)
    w_in = jnp.take(w_in, jnp.asarray(np.where(cols < 0, w_in.shape[1] - 1, cols)), axis=1).astype(BF16)
    taps = p['rwkv_conv'][l].reshape(9, -1)
    taps = jnp.concatenate([taps, jnp.zeros((9, PA_W - taps.shape[1]), F32)], axis=1)
    wup = jnp.zeros((LANE, 2 * RW), F32)
    aup = jnp.zeros((LANE, 2 * RW), F32)
    for d in range(2):
        wup = wup.at[d * DECAY_RANK:(d + 1) * DECAY_RANK, d * RW:(d + 1) * RW].set(p['rwkv_w_up'][l, d])
        o = 2 * DECAY_RANK + d * ICLR_RANK
        aup = aup.at[o:o + ICLR_RANK, d * RW:(d + 1) * RW].set(p['rwkv_a_up'][l, d])
    gup = jnp.zeros((LANE, RW), F32).at[0:GATE_RANK].set(p['rwkv_g_up'][l])
    row = lambda a: a.reshape(1, -1)
    prep = (wup, aup, gup, row(p['rwkv_w0'][l]), row(p['rwkv_a0'][l]), row(p['rwkv_k_k'][l]),
            row(p['rwkv_k_a'][l]), row(p['rwkv_r_k'][l]), _head_block_ones())
    kd = GLA_HEADS * GLA_K
    gla_aup = jnp.zeros((2, LANE, GLA_KP), F32)
    for d in range(2):
        gla_aup = gla_aup.at[d, d * GLA_RANK:(d + 1) * GLA_RANK, 0:kd].set(p['gla_a_up'][l, d])
    gla_ab = jnp.zeros((2, 1, GLA_KP), F32).at[:, 0, 0:kd].set(p['gla_a_bias'][l])
    lam_re, lam_im = p['s5_lam_re'][l], p['s5_lam_im'][l]
    dt = jnp.exp(p['s5_log_dt'][l])[:, :, None]
    zr, zi = lam_re[:, None, :] * dt, lam_im[:, None, :] * dt
    mag = jnp.exp(zr)
    ab_r, ab_i = mag * jnp.cos(zi), mag * jnp.sin(zi)
    den = (lam_re * lam_re + lam_im * lam_im)[:, None, :]
    f_r = ((ab_r - 1) * lam_re[:, None, :] + ab_i * lam_im[:, None, :]) / den
    f_i = (ab_i * lam_re[:, None, :] - (ab_r - 1) * lam_im[:, None, :]) / den
    b_re, b_im = p['s5_b_re'][l], p['s5_b_im'][l]
    bb_r = f_r[..., None] * b_re - f_i[..., None] * b_im
    bb_i = f_r[..., None] * b_im + f_i[..., None] * b_re
    eye_g = jnp.eye(S5_GROUPS, dtype=F32)
    half = S5_LANES // 2

    def in_blockdiag(bb):
        return jnp.einsum('dgpc,gh->dgchp', bb, eye_g).reshape(2, S5_W, half)

    def out_blockdiag(cc):
        return jnp.einsum('dgcp,gh->dgphc', cc, eye_g).reshape(2, half, S5_W)

    s5_bbd = jnp.concatenate([in_blockdiag(bb_r), in_blockdiag(bb_i)], axis=2).astype(BF16)
    s5_cbd = jnp.concatenate([out_blockdiag(p['s5_c_re'][l]), -out_blockdiag(p['s5_c_im'][l])], axis=1).astype(BF16)
    s5_a = jnp.concatenate([ab_r.reshape(2, 1, half), ab_i.reshape(2, 1, half)], axis=2)
    mix = (_head_block_ones(), row(p['rwkv_ln_g'][l]), row(p['rwkv_ln_b'][l]),
           row(jnp.tile(p['gla_norm_g'][l], GLA_HEADS)), row(p['s5_d'][l]),
           p['s5_glu_w'][l].astype(BF16), row(p['s5_glu_b'][l]), p['w_out'][l].astype(BF16),
           row(p['ln1_g'][l]), row(p['ln1_b'][l]))
    moe = dict(rwt=p['router_w'][l].T,
               bias=jnp.broadcast_to(p['router_bias'][l][:, None], (N_EXPERTS, LANE)),
               w13=p['exp_w13'], w2=p['exp_w2'], layer=l,
               comb=(p['sh_w13'][l].astype(BF16), p['sh_w2'][l].astype(BF16), row(p['ln2_g'][l]), row(p['ln2_b'][l])))
    return dict(w_in=w_in, taps=taps, prep=prep, gla_aup=gla_aup, gla_ab=gla_ab,
                s5_a=s5_a, s5_bbd=s5_bbd, s5_cbd=s5_cbd, mix=mix, moe=moe)


_ARG_NAMES = ('x', 'c', 'ctx', 'c_ctx', 'w_mod', 'b_mod', 'w_in', 'rwkv_conv', 'rwkv_w0', 'rwkv_w_up', 'rwkv_a0',
              'rwkv_a_up', 'rwkv_g_up', 'rwkv_k_k', 'rwkv_k_a', 'rwkv_r_k', 'rwkv_ln_g', 'rwkv_ln_b', 'gla_a_up',
              'gla_a_bias', 'gla_norm_g', 's5_lam_re', 's5_lam_im', 's5_log_dt', 's5_b_re', 's5_b_im', 's5_c_re',
              's5_c_im', 's5_d', 's5_glu_w', 's5_glu_b', 'w_out', 'ln1_g', 'ln1_b', 'router_w', 'router_bias',
              'exp_w13', 'exp_w2', 'sh_w13', 'sh_w2', 'ln2_g', 'ln2_b')


def _tile(n, pref):
    t = pref
    while n % t:
        t //= 2
    return t


def kernel(x, c, ctx, c_ctx, w_mod, b_mod, w_in, rwkv_conv, rwkv_w0, rwkv_w_up, rwkv_a0, rwkv_a_up, rwkv_g_up,
           rwkv_k_k, rwkv_k_a, rwkv_r_k, rwkv_ln_g, rwkv_ln_b, gla_a_up, gla_a_bias, gla_norm_g, s5_lam_re,
           s5_lam_im, s5_log_dt, s5_b_re, s5_b_im, s5_c_re, s5_c_im, s5_d, s5_glu_w, s5_glu_b, w_out, ln1_g,
           ln1_b, router_w, router_bias, exp_w13, exp_w2, sh_w13, sh_w2, ln2_g, ln2_b):
    p = dict(zip(_ARG_NAMES, (x, c, ctx, c_ctx, w_mod, b_mod, w_in, rwkv_conv, rwkv_w0, rwkv_w_up, rwkv_a0,
                              rwkv_a_up, rwkv_g_up, rwkv_k_k, rwkv_k_a, rwkv_r_k, rwkv_ln_g, rwkv_ln_b, gla_a_up,
                              gla_a_bias, gla_norm_g, s5_lam_re, s5_lam_im, s5_log_dt, s5_b_re, s5_b_im, s5_c_re,
                              s5_c_im, s5_d, s5_glu_w, s5_glu_b, w_out, ln1_g, ln1_b, router_w, router_bias,
                              exp_w13, exp_w2, sh_w13, sh_w2, ln2_g, ln2_b)))
    B, T, D = x.shape
    TC = ctx.shape[1]
    L = w_mod.shape[0]
    alpha = (2 * L) ** 0.25
    n_lat, n_ctx = B * T, B * TC
    R = ((B + 1 + SUB - 1) // SUB) * SUB
    cc = jnp.zeros((R, D), F32).at[0:B].set(c).at[B].set(c_ctx)
    mod = mod_table(cc, w_mod, b_mod)
    tm = _tile(T, 256)
    tmc = _tile(n_ctx, 256)
    tmm = min(_tile(T, 128), _tile(n_ctx, 128))
    lat_row = lambda i: (i * tm) // T
    ctx_row = lambda i: B
    x2 = x.reshape(n_lat, D)
    c2 = ctx.reshape(n_ctx, D)
    for l in range(L):
        last = l == L - 1
        lp = _layer_params(l, p)
        mod3 = mod[l].reshape(R, 6, D)
        outs_c, st_c = token_mixers(c2, B, TC, TC, mod3, ctx_row, lp, zero_states(B), tmc)
        outs, _ = token_mixers(x2, B, T, GRID_W, mod3, lat_row, lp, st_c, tm)
        x1, h2, h2t = mix_out(*outs, x2, mod3, lp['mix'], lat_row, tm, alpha)
        if last:
            x2 = moe_ffn(h2, h2t, x1, mod3, lp['moe'], lambda i: (i * tmm) // T, tmm, alpha)
        else:
            c1, hc2, hc2t = mix_out(*outs_c, c2, mod3, lp['mix'], ctx_row, tmc, alpha)
            row_all = lambda i: jnp.where(i * tmm < n_ctx, B, (i * tmm - n_ctx) // T)
            out = moe_ffn(jnp.concatenate([hc2, h2]), jnp.concatenate([hc2t, h2t]), jnp.concatenate([c1, x1]),
                          mod3, lp['moe'], row_all, tmm, alpha)
            c2, x2 = out[:n_ctx], out[n_ctx:]
    return x2.reshape(B, T, D)
```

```python
import functools
import math

import numpy as np
import jax
import jax.numpy as jnp
from jax import lax
from jax.experimental import pallas as pl
from jax.experimental.pallas import tpu as pltpu

F32 = jnp.float32
BF16 = jnp.bfloat16
I32 = jnp.int32

GRID_W = 64
RWKV_HEADS = 6
HEAD = 64
RW = RWKV_HEADS * HEAD
DECAY_RANK = 32
ICLR_RANK = 32
GATE_RANK = 64
RWKV_GN_EPS = 64e-5
GLA_HEADS = 6
GLA_K = 32
GLA_RANK = 16
GLA_TEMP = 16.0
S5_GROUPS = 16
S5_GROUP = 16
S5_STATE = 64
S5_W = S5_GROUPS * S5_GROUP
S5_LANES = 2 * S5_GROUPS * S5_STATE
N_EXPERTS = 256
TOP_K = 8
N_EGROUPS = 8
TOPK_GROUPS = 4
ROUTE_SCALE = 2.5
LN_EPS = 1e-6
CHUNK = 64
MOE_BLK = 256
LANE = 128
SUB = 8
VMEM_LIMIT = 56 * 1024 * 1024

PA_W = 3 * RW + 2 * LANE
PBG_W = 256 + 256 + RW + LANE
NP_W = PA_W + PBG_W + RW + S5_W


def _cp(sem):
    return pltpu.CompilerParams(dimension_semantics=sem, vmem_limit_bytes=VMEM_LIMIT)


def _sigmoid(x):
    return 1.0 / (1.0 + jnp.exp(-x))


def _silu(x):
    return x * _sigmoid(x)


def _bdot(a, b, dims=None):
    a = a.astype(BF16)
    b = b.astype(BF16)
    if dims is None:
        return jnp.dot(a, b, preferred_element_type=F32)
    return lax.dot_general(a, b, dims, preferred_element_type=F32)


def _split2(x):
    hi = x.astype(BF16)
    lo = (x - hi.astype(F32)).astype(BF16)
    return hi, lo


def _split3(x):
    hi = x.astype(BF16)
    r = x - hi.astype(F32)
    mid = r.astype(BF16)
    lo = (r - mid.astype(F32)).astype(BF16)
    return hi, mid, lo


NT_DIMS = (((1,), (1,)), ((), ()))
TN_DIMS = (((0,), (0,)), ((), ()))


def _dot33(a, b, dims=None):
    ah, al = _split2(a)
    bh, bl = _split2(b)
    return _bdot(ah, bh, dims) + (_bdot(ah, bl, dims) + _bdot(al, bh, dims))


def _dot_exact_lhs(m_exact, x, dims=None):
    h, m, l = _split3(x)
    return _bdot(m_exact, h, dims) + (_bdot(m_exact, m, dims) + _bdot(m_exact, l, dims))


def _dot_exact_rhs(x, m_exact, dims=None):
    h, m, l = _split3(x)
    return _bdot(h, m_exact, dims) + (_bdot(m, m_exact, dims) + _bdot(l, m_exact, dims))


def _dot_split_rhs(x, m_exact, dims=None):
    h, l = _split2(x)
    return _bdot(h, m_exact, dims) + _bdot(l, m_exact, dims)


def _layer_norm(x):
    mu = jnp.mean(x, axis=-1, keepdims=True)
    xc = x - mu
    var = jnp.mean(xc * xc, axis=-1, keepdims=True)
    return xc * lax.rsqrt(var + LN_EPS)


def _mod_kernel(c_ref, w_ref, b_ref, o_ref):
    s = _silu(c_ref[...])
    o_ref[0] = _dot33(s, w_ref[0]) + b_ref[0]


def mod_table(cc, w_mod, b_mod):
    L, D, D6 = w_mod.shape
    R = cc.shape[0]
    tn = 1536
    return pl.pallas_call(
        _mod_kernel,
        out_shape=jax.ShapeDtypeStruct((L, R, D6), F32),
        grid=(L, D6 // tn),
        in_specs=[pl.BlockSpec((R, D), lambda l, j: (0, 0)),
                  pl.BlockSpec((1, D, tn), lambda l, j: (l, 0, j)),
                  pl.BlockSpec((1, 1, tn), lambda l, j: (l, 0, j))],
        out_specs=pl.BlockSpec((1, R, tn), lambda l, j: (l, 0, j)),
        compiler_params=_cp(("arbitrary", "arbitrary")),
        name="mod_table",
    )(cc, w_mod, b_mod.reshape(L, 1, D6))


def _inproj_kernel(x_ref, mod_ref, w_ref, pa_ref, pbg_ref, gg_ref, u_ref):
    x = x_ref[...]
    h = _layer_norm(x) * (1.0 + mod_ref[0, 1:2, :]) + mod_ref[0, 0:1, :]
    hb = h.astype(BF16)
    o = 0
    for ref in (pa_ref, pbg_ref, gg_ref, u_ref):
        w = ref.shape[-1]
        ref[...] = jnp.dot(hb, w_ref[:, o:o + w], preferred_element_type=F32)
        o += w


def inproj(x2, mod3, w_bf, row_of_tile, tm):
    N, D = x2.shape
    return pl.pallas_call(
        _inproj_kernel,
        out_shape=[jax.ShapeDtypeStruct((N, PA_W), F32),
                   jax.ShapeDtypeStruct((N, PBG_W), F32),
                   jax.ShapeDtypeStruct((N, RW), F32),
                   jax.ShapeDtypeStruct((N, S5_W), F32)],
        grid=(N // tm,),
        in_specs=[pl.BlockSpec((tm, D), lambda i: (i, 0)),
                  pl.BlockSpec((1, 6, D), lambda i: (row_of_tile(i), 0, 0)),
                  pl.BlockSpec((D, NP_W), lambda i: (0, 0))],
        out_specs=[pl.BlockSpec((tm, PA_W), lambda i: (i, 0)),
                   pl.BlockSpec((tm, PBG_W), lambda i: (i, 0)),
                   pl.BlockSpec((tm, RW), lambda i: (i, 0)),
                   pl.BlockSpec((tm, S5_W), lambda i: (i, 0))],
        compiler_params=_cp(("arbitrary",)),
        name="inproj",
    )(x2, mod3, w_bf)


CONV_PAD = 72


def _conv_kernel(x_ref, taps_ref, o_ref, buf_ref, *, T, W, vertical):
    pad = CONV_PAD
    zeros = jnp.zeros((pad, LANE), F32)
    buf_ref[0:pad, :] = zeros
    buf_ref[pad + T:pad + T + pad, :] = zeros
    buf_ref[pad:pad + T, :] = x_ref[0]
    ch = min(T, 256)
    col = lax.broadcasted_iota(I32, (ch, LANE), 0) & (W - 1)
    left_ok = col >= 1
    right_ok = col <= W - 2
    for c in range(T // ch):
        base = pad + c * ch
        acc = jnp.zeros((ch, LANE), F32)
        for dr in ((0, 1, 2) if vertical else (1,)):
            for dc in range(3):
                off = (dr - 1) * W + (dc - 1)
                v = buf_ref[base + off:base + off + ch, :]
                if dc == 0:
                    v = jnp.where(left_ok, v, 0.0)
                elif dc == 2:
                    v = jnp.where(right_ok, v, 0.0)
                acc = acc + v * taps_ref[3 * dr + dc:3 * dr + dc + 1, :]
        o_ref[0, c * ch:(c + 1) * ch, :] = acc


def short_conv(pa3, taps9, W):
    B, T, C = pa3.shape
    vertical = T > W
    assert W & (W - 1) == 0 and (not vertical or W + 1 <= CONV_PAD)
    kern = functools.partial(_conv_kernel, T=T, W=W, vertical=vertical)
    return pl.pallas_call(
        kern,
        out_shape=jax.ShapeDtypeStruct((B, T, C), F32),
        grid=(B, C // LANE),
        in_specs=[pl.BlockSpec((1, T, LANE), lambda b, j: (b, 0, j)),
                  pl.BlockSpec((9, LANE), lambda b, j: (0, j))],
        out_specs=pl.BlockSpec((1, T, LANE), lambda b, j: (b, 0, j)),
        scratch_shapes=[pltpu.VMEM((T + 2 * CONV_PAD, LANE), F32)],
        compiler_params=_cp(("arbitrary", "arbitrary")),
        name="short_conv",
    )(pa3, taps9)


def _rwkv_prep_kernel(pa_ref, wup_ref, aup_ref, gup_ref, w0_ref, a0_ref, kk_ref, ka_ref, rk_ref, hb_ref,
                      ss_ref, sd_ref, gb_ref):
    r = pa_ref[:, 0:RW]
    k = pa_ref[:, RW:2 * RW]
    v = pa_ref[:, 2 * RW:3 * RW]
    wa = pa_ref[:, 3 * RW:3 * RW + LANE]
    gd = pa_ref[:, 3 * RW + LANE:3 * RW + 2 * LANE]
    z = w0_ref[...] + _dot33(jnp.tanh(wa), wup_ref[...])
    lw = -_sigmoid(z) * math.exp(-0.5)
    a = _sigmoid(a0_ref[...] + _bdot(wa, aup_ref[...]))
    g = _bdot(_sigmoid(gd), gup_ref[...])
    hb = hb_ref[...]
    kk = k * kk_ref[...]
    kk = kk * lax.rsqrt(_dot_split_rhs(kk * kk, hb) + 1e-12)
    ka = ka_ref[...]
    ss_ref[:, 0:RW] = r
    ss_ref[:, RW:2 * RW] = v
    ss_ref[:, 2 * RW:3 * RW] = kk
    rk2 = jnp.zeros_like(r)
    for d in range(2):
        ad = a[:, d * RW:(d + 1) * RW]
        k2 = k * (1.0 + (ad - 1.0) * ka)
        sd_ref[d, :, 0:RW] = lw[:, d * RW:(d + 1) * RW]
        sd_ref[d, :, RW:2 * RW] = k2
        sd_ref[d, :, 2 * RW:3 * RW] = kk * ad
        rk2 = rk2 + r * k2
    bonus = _dot_split_rhs(rk2 * rk_ref[...], hb) * v
    gb_ref[:, 0:RW] = g
    gb_ref[:, RW:2 * RW] = bonus


def rwkv_prep(pa2, wts, tm):
    N = pa2.shape[0]
    full = lambda a: pl.BlockSpec(a.shape, lambda i: (0,) * a.ndim)
    return pl.pallas_call(
        _rwkv_prep_kernel,
        out_shape=[jax.ShapeDtypeStruct((N, 3 * RW), F32),
                   jax.ShapeDtypeStruct((2, N, 3 * RW), F32),
                   jax.ShapeDtypeStruct((N, 2 * RW), F32)],
        grid=(N // tm,),
        in_specs=[pl.BlockSpec((tm, PA_W), lambda i: (i, 0))] + [full(a) for a in wts],
        out_specs=[pl.BlockSpec((tm, 3 * RW), lambda i: (i, 0)),
                   pl.BlockSpec((2, tm, 3 * RW), lambda i: (0, i, 0)),
                   pl.BlockSpec((tm, 2 * RW), lambda i: (i, 0))],
        compiler_params=_cp(("arbitrary",)),
        name="rwkv_prep",
    )(pa2, *wts)


RWKV_GROUP = 8


def _rwkv_scan_kernel(ss_ref, sd_ref, s0_ref, y_ref, sfin_ref, s_scr, *, nsteps, group, reverse):
    n = pl.program_id(1)
    C = CHUNK
    P = 2 * C
    npair = RW // LANE

    @pl.when(n == 0)
    def _():
        s_scr[...] = s0_ref[0]

    row = lax.broadcasted_iota(I32, (P, P), 0)
    col = lax.broadcasted_iota(I32, (P, P), 1)
    same = (row >> 6) == (col >> 6)
    dlt = (col & (C - 1)) - (row & (C - 1)) if reverse else (row & (C - 1)) - (col & (C - 1))
    strict = same & (dlt > 0)
    incl = same & (dlt >= 0)
    eye = (row == col).astype(F32)
    lvl_masks = [((row >> (lvl + 1)) == (col >> (lvl + 1))) & ((row >> lvl) != (col >> lvl)) for lvl in range(6)]
    rc = lax.broadcasted_iota(I32, (C, C), 0)
    cc = lax.broadcasted_iota(I32, (C, C), 1)
    tri = jnp.where((cc >= rc) if reverse else (rc >= cc), 1.0, 0.0).astype(BF16)
    head0 = lax.broadcasted_iota(I32, (C, LANE), 1) < HEAD

    def stack(x):
        return jnp.concatenate([jnp.where(head0, x, 0.0), jnp.where(head0, 0.0, x)], axis=0)

    streams = [(g, p) for g in range(group) for p in range(npair)]
    tm_ = {}
    for (g, p) in streams:
        t0, t1 = g * C, (g + 1) * C
        lo, hi = p * LANE, (p + 1) * LANE
        lw = sd_ref[0, t0:t1, lo:hi]
        tm_[(g, p)] = dict(lw=lw, cl=_dot_exact_lhs(tri, lw))
    for (g, p) in streams:
        t = tm_[(g, p)]
        t0, t1 = g * C, (g + 1) * C
        lo, hi = p * LANE, (p + 1) * LANE
        r = ss_ref[t0:t1, lo:hi]
        v = ss_ref[t0:t1, RW + lo:RW + hi]
        kk = ss_ref[t0:t1, 2 * RW + lo:2 * RW + hi]
        k2 = sd_ref[0, t0:t1, RW + lo:RW + hi]
        b = sd_ref[0, t0:t1, 2 * RW + lo:2 * RW + hi]
        cl, lw = t['cl'], t['lw']
        t['ptot'] = jnp.exp(jnp.sum(lw, axis=0, keepdims=True))
        pinv = jnp.exp(-cl)
        left = jnp.concatenate([stack(-kk * jnp.exp(cl - lw)), stack(r * jnp.exp(cl))], axis=0)
        right = jnp.concatenate([stack(b * pinv), stack(k2 * pinv)], axis=0)
        t['v_st'] = stack(v)
        t['left'] = left.astype(BF16)
        t['bk'] = (right * t['ptot']).astype(BF16)
        aa = _bdot(left, right, NT_DIMS)
        t['nmat'] = jnp.where(strict, aa[0:P, 0:P], 0.0)
        t['a_ak'] = jnp.where(strict, aa[0:P, P:2 * P], 0.0)
        t['a_rbk'] = jnp.concatenate([jnp.where(incl, aa[P:2 * P, 0:P], 0.0),
                                      jnp.where(incl, aa[P:2 * P, P:2 * P], 0.0)], axis=1).astype(BF16)
        t['tinv'] = eye
    for sk in streams:
        t = tm_[sk]
        t['akv'] = _bdot(t['a_ak'], t['v_st'])
    for m in lvl_masks:
        for sk in streams:
            t = tm_[sk]
            t['et'] = _bdot(jnp.where(m, t['nmat'], 0.0), t['tinv'])
        for sk in streams:
            t = tm_[sk]
            t['tinv'] = t['tinv'] + _bdot(t['tinv'], t['et'])
    state = [s_scr[p] for p in range(npair)]
    pairs = range(npair)
    for g in (range(group - 1, -1, -1) if reverse else range(group)):
        ts = [tm_[(g, p)] for p in pairs]
        a_s = [_bdot(ts[p]['left'], state[p], NT_DIMS) for p in pairs]
        u = [_bdot(ts[p]['tinv'], a_s[p][0:P] + ts[p]['akv']) for p in pairs]
        uv = [jnp.concatenate([u[p], ts[p]['v_st']], axis=0) for p in pairs]
        state = [state[p] * ts[p]['ptot'] + _bdot(uv[p], ts[p]['bk'], TN_DIMS) for p in pairs]
        for p in pairs:
            y_st = a_s[p][P:2 * P] + _bdot(ts[p]['a_rbk'], uv[p])
            y_ref[g * C:(g + 1) * C, p * LANE:(p + 1) * LANE] = y_st[0:C] + y_st[C:P]
    for p in pairs:
        s_scr[p] = state[p]

    @pl.when(n == nsteps - 1)
    def _():
        sfin_ref[0] = s_scr[...]


def rwkv_scan_dir(ss, sd, s0, B, T, d):
    nc = T // CHUNK
    group = min(RWKV_GROUP, nc)
    nsteps = nc // group
    npair = RW // LANE
    rows = group * CHUNK
    blk = (lambda b, n: b * nsteps + nsteps - 1 - n) if d else (lambda b, n: b * nsteps + n)
    kern = functools.partial(_rwkv_scan_kernel, nsteps=nsteps, group=group, reverse=bool(d))
    return pl.pallas_call(
        kern,
        out_shape=[jax.ShapeDtypeStruct((B * T, RW), F32),
                   jax.ShapeDtypeStruct((B, npair, LANE, LANE), F32)],
        grid=(B, nsteps),
        in_specs=[pl.BlockSpec((rows, 3 * RW), lambda b, n: (blk(b, n), 0)),
                  pl.BlockSpec((1, rows, 3 * RW), lambda b, n: (d, blk(b, n), 0)),
                  pl.BlockSpec((1, npair, LANE, LANE), lambda b, n: (b, 0, 0, 0))],
        out_specs=[pl.BlockSpec((rows, RW), lambda b, n: (blk(b, n), 0)),
                   pl.BlockSpec((1, npair, LANE, LANE), lambda b, n: (b, 0, 0, 0))],
        scratch_shapes=[pltpu.VMEM((npair, LANE, LANE), F32)],
        compiler_params=_cp(("arbitrary", "arbitrary")),
        name="rwkv_scan_bwd" if d else "rwkv_scan_fwd",
    )(ss, sd, s0)


def rwkv_scan(ss, sd, s0, B, T):
    y0, f0 = rwkv_scan_dir(ss, sd, s0[0], B, T, 0)
    y1, f1 = rwkv_scan_dir(ss, sd, s0[1], B, T, 1)
    return (y0, y1), (f0, f1)


GLA_KP = 256


GLA_GROUP = 8


def _gla_kernel(pbg_ref, aup_ref, ab_ref, s0_ref, o_ref, sfin_ref, s_scr, *, nsteps, group, reverse):
    n = pl.program_id(1)
    C = CHUNK
    R = group * C

    @pl.when(n == 0)
    def _():
        s_scr[...] = s0_ref[0]

    q = pbg_ref[:, 0:GLA_KP]
    k = pbg_ref[:, GLA_KP:2 * GLA_KP]
    v = pbg_ref[:, 2 * GLA_KP:2 * GLA_KP + RW]
    ad = pbg_ref[:, 2 * GLA_KP + RW:2 * GLA_KP + RW + LANE]
    x = _dot33(ad, aup_ref[0]) + ab_ref[0]
    la = (jnp.minimum(x, 0.0) - jnp.log(1.0 + jnp.exp(-jnp.abs(x)))) * (1.0 / GLA_TEMP)
    chunks = range(group)
    sl = [slice(g * C, (g + 1) * C) for g in chunks]
    rr = lax.broadcasted_iota(I32, (C, C), 0)
    rc = lax.broadcasted_iota(I32, (C, C), 1)
    tri = jnp.where((rc >= rr) if reverse else (rr >= rc), 1.0, 0.0).astype(BF16)
    bcums = [_dot_exact_lhs(tri, la[sl[g]]) for g in chunks]
    last = 0 if reverse else C - 1
    tots = [bc[last:last + 1, :] for bc in bcums]
    bcum = jnp.concatenate(bcums, axis=0)
    tot = jnp.concatenate([jnp.broadcast_to(t, (C, GLA_KP)) for t in tots], axis=0)
    q_in = q * jnp.exp(bcum) * (GLA_K ** -0.5)
    k_in = k * jnp.exp(-bcum)
    k_st = k * jnp.exp(tot - bcum)
    dn = [jnp.exp(t) for t in tots]
    klane = lax.broadcasted_iota(I32, (C, GLA_KP), 1)
    rt = lax.broadcasted_iota(I32, (GLA_HEADS * C, C), 0) & (C - 1)
    ct = lax.broadcasted_iota(I32, (GLA_HEADS * C, C), 1)
    causal = (ct >= rt) if reverse else (rt >= ct)
    vlane = lax.broadcasted_iota(I32, (C, RW), 1)
    sv = lax.broadcasted_iota(I32, (RW, GLA_KP), 0) >> 6
    sk = lax.broadcasted_iota(I32, (RW, GLA_KP), 1) >> 5
    q_rows = [jnp.concatenate([jnp.where((klane >> 5) == h, q_in[sl[g]], 0.0) for h in range(GLA_HEADS)],
                              axis=0).astype(BF16) for g in chunks]
    att = [jnp.where(causal, _bdot(q_rows[g], k_in[sl[g]], NT_DIMS), 0.0) for g in chunks]
    o_rows = [_bdot(att[g], v[sl[g]]) for g in chunks]
    kv = [jnp.where(sv == sk, _bdot(v[sl[g]], k_st[sl[g]], TN_DIMS), 0.0) for g in chunks]
    s = s_scr[...]
    for g in (reversed(chunks) if reverse else chunks):
        o = _bdot(q_in[sl[g]], s, NT_DIMS)
        for h in range(GLA_HEADS):
            o = o + jnp.where((vlane >> 6) == h, o_rows[g][h * C:(h + 1) * C], 0.0)
        o_ref[sl[g], :] = o
        s = s * dn[g] + kv[g]
    s_scr[...] = s

    @pl.when(n == nsteps - 1)
    def _():
        sfin_ref[0] = s_scr[...]


def gla_scan_dir(pbg, aup, ab, s0, B, T, d):
    nc = T // CHUNK
    group = min(GLA_GROUP, nc)
    nsteps = nc // group
    rows = group * CHUNK
    blk = (lambda b, n: b * nsteps + nsteps - 1 - n) if d else (lambda b, n: b * nsteps + n)
    kern = functools.partial(_gla_kernel, nsteps=nsteps, group=group, reverse=bool(d))
    return pl.pallas_call(
        kern,
        out_shape=[jax.ShapeDtypeStruct((B * T, RW), F32),
                   jax.ShapeDtypeStruct((B, RW, GLA_KP), F32)],
        grid=(B, nsteps),
        in_specs=[pl.BlockSpec((rows, PBG_W), lambda b, n: (blk(b, n), 0)),
                  pl.BlockSpec((1, LANE, GLA_KP), lambda b, n: (d, 0, 0)),
                  pl.BlockSpec((1, 1, GLA_KP), lambda b, n: (d, 0, 0)),
                  pl.BlockSpec((1, RW, GLA_KP), lambda b, n: (b, 0, 0))],
        out_specs=[pl.BlockSpec((rows, RW), lambda b, n: (blk(b, n), 0)),
                   pl.BlockSpec((1, RW, GLA_KP), lambda b, n: (b, 0, 0))],
        scratch_shapes=[pltpu.VMEM((RW, GLA_KP), F32)],
        compiler_params=_cp(("arbitrary", "arbitrary")),
        name="gla_scan_bwd" if d else "gla_scan_fwd",
    )(pbg, aup, ab, s0)


def gla_scan(pbg, aup, ab, s0, B, T):
    o0, f0 = gla_scan_dir(pbg, aup, ab, s0[0], B, T, 0)
    o1, f1 = gla_scan_dir(pbg, aup, ab, s0[1], B, T, 1)
    return (o0, o1), (f0, f1)


S5_PITCH = 72


def _s5_kernel(u_ref, a_ref, bbd_ref, cbd_ref, s0_ref, y_ref, sfin_ref, x_scr, st_scr, *, nc, nb):
    d = pl.program_id(0)
    n = pl.program_id(1)
    C = CHUNK
    half = S5_LANES // 2

    @pl.when(n == 0)
    def _():
        st_scr[...] = s0_ref[0]

    nslab = S5_LANES // LANE
    bbd = bbd_ref[0]
    for b in range(nb):
        bu = _bdot(u_ref[b], bbd)
        for j in range(nslab):
            x_scr[j, b * S5_PITCH:b * S5_PITCH + C, :] = bu[:, j * LANE:(j + 1) * LANE]
    a_re = a_ref[0, :, 0:half]
    a_im = a_ref[0, :, half:S5_LANES]

    def step(i, carry):
        re, im = carry
        t = i + d * (C - 1 - 2 * i)
        rows = pl.ds(t, nb, stride=S5_PITCH)
        bu = jnp.concatenate([x_scr[j, rows, :] for j in range(nslab)], axis=1)
        nre = a_re * re - a_im * im + bu[:, 0:half]
        nim = a_re * im + a_im * re + bu[:, half:S5_LANES]
        for j in range(nslab // 2):
            x_scr[j, rows, :] = nre[:, j * LANE:(j + 1) * LANE]
            x_scr[nslab // 2 + j, rows, :] = nim[:, j * LANE:(j + 1) * LANE]
        return nre, nim

    st = st_scr[...]
    re, im = lax.fori_loop(0, C, step, (st[:, 0:half], st[:, half:S5_LANES]), unroll=4)
    st_scr[...] = jnp.concatenate([re, im], axis=1)
    cbd = cbd_ref[0]
    for b in range(nb):
        xs = jnp.concatenate([x_scr[j, b * S5_PITCH:b * S5_PITCH + C, :] for j in range(nslab)], axis=1)
        y_ref[0, b] = _bdot(xs, cbd)

    @pl.when(n == nc - 1)
    def _():
        sfin_ref[0] = st_scr[...]


def s5_scan(u3, a_bar, bbd, cbd, s0):
    B, T, _ = u3.shape
    nc = T // CHUNK
    chunk = lambda d, n: n + d * (nc - 1 - 2 * n)
    kern = functools.partial(_s5_kernel, nc=nc, nb=B)
    return pl.pallas_call(
        kern,
        out_shape=[jax.ShapeDtypeStruct((2, B, T, S5_W), F32),
                   jax.ShapeDtypeStruct((2, B, S5_LANES), F32)],
        grid=(2, nc),
        in_specs=[pl.BlockSpec((B, CHUNK, S5_W), lambda d, n: (0, chunk(d, n), 0)),
                  pl.BlockSpec((1, 1, S5_LANES), lambda d, n: (d, 0, 0)),
                  pl.BlockSpec((1, S5_W, S5_LANES), lambda d, n: (d, 0, 0)),
                  pl.BlockSpec((1, S5_LANES, S5_W), lambda d, n: (d, 0, 0)),
                  pl.BlockSpec((1, B, S5_LANES), lambda d, n: (d, 0, 0))],
        out_specs=[pl.BlockSpec((1, B, CHUNK, S5_W), lambda d, n: (d, 0, chunk(d, n), 0)),
                   pl.BlockSpec((1, B, S5_LANES), lambda d, n: (d, 0, 0))],
        scratch_shapes=[pltpu.VMEM((S5_LANES // LANE, B * S5_PITCH, LANE), F32),
                        pltpu.VMEM((B, S5_LANES), F32)],
        compiler_params=_cp(("arbitrary", "arbitrary")),
        name="s5_scan",
    )(u3, a_bar, bbd, cbd, s0)


ROWT = 4
U32 = jnp.uint32


def _pack_bf16_pair(a, b):
    au = lax.bitcast_convert_type(a.astype(BF16).astype(F32), U32)
    bu = lax.bitcast_convert_type(b.astype(BF16).astype(F32), U32)
    return (au >> 16) | bu


def _unpack_bf16_pair(w):
    return (lax.bitcast_convert_type(w << 16, F32), lax.bitcast_convert_type(w & jnp.uint32(0xFFFF0000), F32))


def _tile_rows_store(ref, val):
    tm = val.shape[0]
    assert val.shape[1] == 2 * ROWT * LANE
    for s in range(ROWT):
        ref[pl.ds(s, tm, stride=ROWT), :] = _pack_bf16_pair(val[:, 2 * s * LANE:(2 * s + 1) * LANE],
                                                            val[:, (2 * s + 1) * LANE:(2 * s + 2) * LANE])


def _tile_rows_load(ref, tm, base=0):
    parts = []
    for s in range(ROWT):
        parts.extend(_unpack_bf16_pair(ref[pl.ds(base + s, tm, stride=ROWT), :]))
    return jnp.concatenate(parts, axis=1)


def _mix_out_kernel(y0_ref, y1_ref, gb_ref, o0_ref, o1_ref, gg_ref, yd_ref, u_ref, x_ref, mod_ref,
                    hb_ref, lng_ref, lnb_ref, gng_ref, s5d_ref, gluw_ref, glub_ref, wout_ref, l1g_ref, l1b_ref,
                    x1_ref, h2_ref, h2t_ref, *, alpha):
    hb = hb_ref[...]
    inv = 1.0 / HEAD
    y = y0_ref[...] + y1_ref[...]
    yc = y - _dot_split_rhs(y, hb) * inv
    var = _dot_split_rhs(yc * yc, hb) * inv
    gn = yc * lax.rsqrt(var + RWKV_GN_EPS) * lng_ref[...] + lnb_ref[...]
    y_a = (gn + gb_ref[:, RW:2 * RW]) * gb_ref[:, 0:RW]
    o = o0_ref[...] + o1_ref[...]
    o = o * lax.rsqrt(_dot_split_rhs(o * o, hb) * inv + 1e-6) * gng_ref[...]
    y_b = o * _silu(gg_ref[...])
    c = s5d_ref[...] * u_ref[...] + yd_ref[0] + yd_ref[1]
    c = 0.5 * c * (1.0 + jnp.tanh(math.sqrt(2.0 / math.pi) * (c + 0.044715 * (c * c * c))))
    y_c = c * _sigmoid(_bdot(c, gluw_ref[...]) + glub_ref[...])
    y_mix = (_bdot(y_a, wout_ref[0:RW, :]) + _bdot(y_b, wout_ref[RW:2 * RW, :])
             + _bdot(y_c, wout_ref[2 * RW:2 * RW + S5_W, :]))
    x1 = _layer_norm(alpha * x_ref[...] + mod_ref[0, 2:3, :] * y_mix) * l1g_ref[...] + l1b_ref[...]
    x1_ref[...] = x1
    h2 = _layer_norm(x1) * (1.0 + mod_ref[0, 4:5, :]) + mod_ref[0, 3:4, :]
    h2_ref[...] = h2
    _tile_rows_store(h2t_ref, h2)


def mix_out(y01, gb, o01, gg, yd, u, x2, mod3, wts, row_of_tile, tm, alpha):
    N, D = x2.shape
    full = lambda a: pl.BlockSpec(a.shape, lambda i: (0,) * a.ndim)
    kern = functools.partial(_mix_out_kernel, alpha=alpha)
    return pl.pallas_call(
        kern,
        out_shape=[jax.ShapeDtypeStruct((N, D), F32),
                   jax.ShapeDtypeStruct((N, D), F32),
                   jax.ShapeDtypeStruct((N * ROWT, LANE), U32)],
        grid=(N // tm,),
        in_specs=[pl.BlockSpec((tm, RW), lambda i: (i, 0)),
                  pl.BlockSpec((tm, RW), lambda i: (i, 0)),
                  pl.BlockSpec((tm, 2 * RW), lambda i: (i, 0)),
                  pl.BlockSpec((tm, RW), lambda i: (i, 0)),
                  pl.BlockSpec((tm, RW), lambda i: (i, 0)),
                  pl.BlockSpec((tm, RW), lambda i: (i, 0)),
                  pl.BlockSpec((2, tm, S5_W), lambda i: (0, i, 0)),
                  pl.BlockSpec((tm, S5_W), lambda i: (i, 0)),
                  pl.BlockSpec((tm, D), lambda i: (i, 0)),
                  pl.BlockSpec((1, 6, D), lambda i: (row_of_tile(i), 0, 0))] + [full(a) for a in wts],
        out_specs=[pl.BlockSpec((tm, D), lambda i: (i, 0)),
                   pl.BlockSpec((tm, D), lambda i: (i, 0)),
                   pl.BlockSpec((tm * ROWT, LANE), lambda i: (i, 0))],
        compiler_params=_cp(("arbitrary",)),
        name="mix_out",
    )(y01[0], y01[1], gb, o01[0], o01[1], gg, yd, u, x2, mod3, *wts)


def _first_max(x, idx, big):
    m = jnp.max(x, axis=0, keepdims=True)
    first = jnp.min(jnp.where(x == m, idx, big), axis=0, keepdims=True)
    return m, first


def _router_kernel(h_ref, rwt_ref, bias_ref, e_ref, w_ref):
    tm = h_ref.shape[0]
    gsz = N_EXPERTS // N_EGROUPS
    ninf = -jnp.inf
    s = _sigmoid(_dot33(rwt_ref[...], h_ref[...], NT_DIMS))
    ssel = s + bias_ref[:, 0:1]
    gi = lax.broadcasted_iota(I32, (gsz, tm), 0)
    gscore = []
    for g in range(N_EGROUPS):
        xg = ssel[g * gsz:(g + 1) * gsz, :]
        m1, i1 = _first_max(xg, gi, gsz)
        m2 = jnp.max(jnp.where(gi == i1, ninf, xg), axis=0, keepdims=True)
        gscore.append(m1 + m2)
    cur = jnp.concatenate(gscore, axis=0)
    gidx = lax.broadcasted_iota(I32, (N_EGROUPS, tm), 0)
    picked = jnp.zeros((N_EGROUPS, tm), F32)
    for _ in range(TOPK_GROUPS):
        _, first = _first_max(cur, gidx, N_EGROUPS)
        hit = gidx == first
        picked = jnp.where(hit, 1.0, picked)
        cur = jnp.where(hit, ninf, cur)
    x = jnp.concatenate(
        [jnp.where(picked[g:g + 1, :] > 0.5, ssel[g * gsz:(g + 1) * gsz, :], ninf) for g in range(N_EGROUPS)], axis=0)
    ei = lax.broadcasted_iota(I32, (N_EXPERTS, tm), 0)
    idxs, ws = [], []
    for _ in range(TOP_K):
        _, first = _first_max(x, ei, N_EXPERTS)
        hit = ei == first
        idxs.append(first)
        ws.append(jnp.sum(jnp.where(hit, s, 0.0), axis=0, keepdims=True))
        x = jnp.where(hit, ninf, x)
    w = jnp.concatenate(ws, axis=0)
    e_ref[...] = jnp.concatenate(idxs, axis=0)
    w_ref[...] = w / jnp.sum(w, axis=0, keepdims=True) * ROUTE_SCALE


def moe_router(h2, rwt, bias_b, tm):
    N, D = h2.shape
    return pl.pallas_call(
        _router_kernel,
        out_shape=[jax.ShapeDtypeStruct((TOP_K, N), I32), jax.ShapeDtypeStruct((TOP_K, N), F32)],
        grid=(N // tm,),
        in_specs=[pl.BlockSpec((tm, D), lambda i: (i, 0)),
                  pl.BlockSpec((N_EXPERTS, D), lambda i: (0, 0)),
                  pl.BlockSpec((N_EXPERTS, LANE), lambda i: (0, 0))],
        out_specs=[pl.BlockSpec((TOP_K, tm), lambda i: (0, i)),
                   pl.BlockSpec((TOP_K, tm), lambda i: (0, i))],
        compiler_params=_cp(("arbitrary",)),
        name="moe_router",
    )(h2, rwt, bias_b)


def _moe_count_kernel(e_ref, pstart_ref, plan_ref, blk_ref, cnt_scr, *, nt, nbp):
    i = pl.program_id(0)
    tp = e_ref.shape[1]

    @pl.when(i == 0)
    def _():
        cnt_scr[...] = jnp.zeros_like(cnt_scr)

    ei = lax.broadcasted_iota(I32, (N_EXPERTS, tp), 0)
    acc = jnp.zeros((N_EXPERTS, 1), F32)
    for k in range(TOP_K):
        acc = acc + jnp.sum(jnp.where(ei == e_ref[k:k + 1, :], 1.0, 0.0), axis=1, keepdims=True)
    cnt_scr[...] = cnt_scr[...] + acc

    @pl.when(i == nt - 1)
    def _():
        cnt = cnt_scr[...].astype(I32)
        padded = (cnt + (MOE_BLK - 1)) & (-MOE_BLK)
        r = lax.broadcasted_iota(I32, (N_EXPERTS, N_EXPERTS), 0)
        c = lax.broadcasted_iota(I32, (N_EXPERTS, N_EXPERTS), 1)
        tri = jnp.where(c <= r, 1.0, 0.0).astype(BF16)
        padded_b = jnp.broadcast_to(padded.astype(F32), (N_EXPERTS, LANE))
        p_end = _dot_exact_lhs(tri, padded_b)
        pstart = p_end - padded_b
        pstart_ref[...] = pstart.astype(I32)
        diag = r == c
        ps_row = jnp.sum(jnp.where(diag, pstart[:, 0:1], 0.0), axis=0, keepdims=True)
        cnt_row = jnp.sum(jnp.where(diag, cnt_scr[...], 0.0), axis=0, keepdims=True)
        plan_ref[...] = jnp.concatenate([ps_row, cnt_row, jnp.zeros((SUB - 2, N_EXPERTS), F32)], axis=0).astype(I32)
        lim =(lax.broadcasted_iota(I32, (N_EXPERTS, nbp), 1) * MOE_BLK).astype(F32)
        be = jnp.sum(jnp.where(p_end[:, 0:1] <= lim, 1.0, 0.0), axis=0, keepdims=True)
        be = jnp.minimum(be, N_EXPERTS - 1.0)
        nused = jnp.max(p_end[:, 0:1], axis=0, keepdims=True) * (1.0 / MOE_BLK)
        blk_ref[...] = jnp.concatenate([jnp.broadcast_to(be, (SUB // 2, nbp)),
                                        jnp.broadcast_to(nused, (SUB // 2, nbp))], axis=0).astype(I32)


def moe_counts(eidx, tp, nbp):
    N = eidx.shape[1]
    nt = N // tp
    kern = functools.partial(_moe_count_kernel, nt=nt, nbp=nbp)
    return pl.pallas_call(
        kern,
        out_shape=[jax.ShapeDtypeStruct((N_EXPERTS, LANE), I32),
                   jax.ShapeDtypeStruct((SUB, N_EXPERTS), I32),
                   jax.ShapeDtypeStruct((SUB, nbp), I32)],
        grid=(nt,),
        in_specs=[pl.BlockSpec((TOP_K, tp), lambda i: (0, i))],
        out_specs=[pl.BlockSpec((N_EXPERTS, LANE), lambda i: (0, 0)),
                   pl.BlockSpec((SUB, N_EXPERTS), lambda i: (0, 0)),
                   pl.BlockSpec((SUB, nbp), lambda i: (0, 0))],
        scratch_shapes=[pltpu.VMEM((N_EXPERTS, 1), F32)],
        compiler_params=_cp(("arbitrary",)),
        name="moe_counts",
    )(eidx)


def _moe_dest_kernel(e_ref, pstart_ref, dest_ref, base_scr):
    i = pl.program_id(0)
    tp = e_ref.shape[1]

    @pl.when(i == 0)
    def _():
        base_scr[...] = pstart_ref[:, 0:1].astype(F32)

    ei = lax.broadcasted_iota(I32, (N_EXPERTS, tp), 0)
    r = lax.broadcasted_iota(I32, (tp, tp), 0)
    c = lax.broadcasted_iota(I32, (tp, tp), 1)
    tri = jnp.where(r <= c, 1.0, 0.0).astype(BF16)
    base = base_scr[...]
    rows = []
    for k in range(TOP_K):
        hit = ei == e_ref[k:k + 1, :]
        oh = jnp.where(hit, 1.0, 0.0)
        cum = _bdot(oh, tri)
        rows.append(jnp.sum(jnp.where(hit, cum - 1.0 + base, 0.0), axis=0, keepdims=True))
        base = base + cum[:, tp - 1:tp]
    base_scr[...] = base
    dest_ref[...] = jnp.concatenate(rows, axis=0).astype(I32)


def moe_dest(eidx, pstart, tp):
    N = eidx.shape[1]
    return pl.pallas_call(
        _moe_dest_kernel,
        out_shape=jax.ShapeDtypeStruct((TOP_K, N), I32),
        grid=(N // tp,),
        in_specs=[pl.BlockSpec((TOP_K, tp), lambda i: (0, i)),
                  pl.BlockSpec((N_EXPERTS, LANE), lambda i: (0, 0))],
        out_specs=pl.BlockSpec((TOP_K, tp), lambda i: (0, i)),
        scratch_shapes=[pltpu.VMEM((N_EXPERTS, 1), F32)],
        compiler_params=_cp(("arbitrary",)),
        name="moe_dest",
    )(eidx, pstart)


def _tile_at(ref, token):
    return ref.at[pl.ds(pl.multiple_of(token * ROWT, ROWT), ROWT), :]


def _moe_scatter_kernel(dest_ref, h_ref, xs_hbm, sem):
    tp = dest_ref.shape[1]

    def issue(j, carry):
        for k in range(TOP_K):
            pltpu.make_async_copy(_tile_at(h_ref, j), _tile_at(xs_hbm, dest_ref[k, j]), sem).start(priority=k % 2)
        return carry

    lax.fori_loop(0, tp, issue, 0)

    def drain(j, carry):
        for k in range(TOP_K):
            pltpu.make_async_copy(h_ref.at[pl.ds(0, ROWT), :], xs_hbm.at[pl.ds(0, ROWT), :], sem).wait()
        return carry

    lax.fori_loop(0, tp, drain, 0)


def moe_scatter(dest, h2t, n_slots, tp):
    N = dest.shape[1]
    return pl.pallas_call(
        _moe_scatter_kernel,
        out_shape=jax.ShapeDtypeStruct((n_slots * ROWT, LANE), U32),
        grid=(N // tp,),
        in_specs=[pl.BlockSpec((TOP_K, tp), lambda i: (0, i), memory_space=pltpu.SMEM),
                  pl.BlockSpec((tp * ROWT, LANE), lambda i: (i, 0))],
        out_specs=pl.BlockSpec(memory_space=pl.ANY),
        scratch_shapes=[pltpu.SemaphoreType.DMA(())],
        compiler_params=_cp(("arbitrary",)),
        name="moe_scatter",
    )(dest, h2t)


PAD_BITS = tuple(1 << b for b in reversed(range(MOE_BLK.bit_length() - 1)))


def _moe_padfill_kernel(ps_ref, xs_in, xs_hbm, zero_scr, sem):
    del xs_in
    zero_scr[...] = jnp.zeros_like(zero_scr)

    def pad_copies(e, wait):
        cnt = ps_ref[1, e]
        npad = ((cnt + (MOE_BLK - 1)) & (-MOE_BLK)) - cnt
        off = ps_ref[0, e] + cnt
        for bit in PAD_BITS:
            @pl.when((npad & bit) != 0)
            def _():
                cp = pltpu.make_async_copy(zero_scr.at[pl.ds(0, bit * ROWT), :],
                                           xs_hbm.at[pl.ds(pl.multiple_of(off * ROWT, ROWT), bit * ROWT), :], sem)
                if wait:
                    cp.wait()
                else:
                    cp.start()
            off = off + (npad & bit)

    def issue(e, carry):
        pad_copies(e, False)
        return carry

    def drain(e, carry):
        pad_copies(e, True)
        return carry

    lax.fori_loop(0, N_EXPERTS, issue, 0)
    lax.fori_loop(0, N_EXPERTS, drain, 0)


def moe_padfill(pstart, xs):
    return pl.pallas_call(
        _moe_padfill_kernel,
        out_shape=jax.ShapeDtypeStruct(xs.shape, xs.dtype),
        grid=(1,),
        in_specs=[pl.BlockSpec(memory_space=pltpu.SMEM),
                  pl.BlockSpec(memory_space=pl.ANY)],
        out_specs=pl.BlockSpec(memory_space=pl.ANY),
        scratch_shapes=[pltpu.VMEM((PAD_BITS[0] * ROWT, LANE), U32), pltpu.SemaphoreType.DMA(())],
        input_output_aliases={1: 0},
        compiler_params=_cp(("arbitrary",)),
        name="moe_padfill",
    )(pstart, xs)


def _experts_kernel(plan_ref, w13_ref, w2_ref, xs_hbm, y_hbm, w13_bf, w2_bf, x_buf, y_buf, sem_in, sem_out):
    e = pl.program_id(0)
    ff = w2_bf.shape[0]
    rows = MOE_BLK * ROWT
    cnt = plan_ref[1, e]
    shift = MOE_BLK.bit_length() - 1
    nblk = lax.shift_right_logical(cnt + (MOE_BLK - 1), shift)
    first = lax.shift_right_logical(plan_ref[0, e], shift)

    def window(ref, j):
        return ref.at[pl.ds(pl.multiple_of((first + j) * rows, rows), rows), :]

    def x_copy(j, slot):
        return pltpu.make_async_copy(window(xs_hbm, j), x_buf.at[slot], sem_in.at[slot])

    def y_copy(j, slot):
        return pltpu.make_async_copy(y_buf.at[slot], window(y_hbm, j), sem_out.at[slot])

    @pl.when(nblk > 0)
    def _():
        x_copy(0, 0).start()
        w13_bf[...] = w13_ref[0, 0].astype(BF16)
        w2_bf[...] = w2_ref[0, 0].astype(BF16)

        def body(j, carry):
            slot = j % 2
            x_copy(j, slot).wait()

            @pl.when(j + 1 < nblk)
            def _():
                x_copy(j + 1, 1 - slot).start()

            @pl.when(j >= 2)
            def _():
                y_copy(j - 2, slot).wait()

            x = _tile_rows_load(x_buf.at[slot], MOE_BLK).astype(BF16)
            h = jnp.dot(x, w13_bf[...], preferred_element_type=F32)
            act = (_silu(h[:, 0:ff]) * h[:, ff:2 * ff]).astype(BF16)
            _tile_rows_store(y_buf.at[slot], jnp.dot(act, w2_bf[...], preferred_element_type=F32))
            y_copy(j, slot).start()
            return carry

        lax.fori_loop(0, nblk, body, 0)

        @pl.when(nblk >= 2)
        def _():
            y_copy(nblk - 2, nblk % 2).wait()

        y_copy(nblk - 1, (nblk - 1) % 2).wait()


def moe_experts(plan, xs, w13, w2, layer):
    _, E, D, F2 = w13.shape
    rows = MOE_BLK * ROWT
    grid_spec = pltpu.PrefetchScalarGridSpec(
        num_scalar_prefetch=1,
        grid=(E,),
        in_specs=[pl.BlockSpec((1, 1, D, F2), lambda e, p: (layer, e, 0, 0)),
                  pl.BlockSpec((1, 1, F2 // 2, D), lambda e, p: (layer, e, 0, 0)),
                  pl.BlockSpec(memory_space=pl.ANY)],
        out_specs=pl.BlockSpec(memory_space=pl.ANY),
        scratch_shapes=[pltpu.VMEM((D, F2), BF16), pltpu.VMEM((F2 // 2, D), BF16),
                        pltpu.VMEM((2, rows, LANE), U32), pltpu.VMEM((2, rows, LANE), U32),
                        pltpu.SemaphoreType.DMA((2,)), pltpu.SemaphoreType.DMA((2,))])
    return pl.pallas_call(
        _experts_kernel,
        out_shape=jax.ShapeDtypeStruct(xs.shape, U32),
        grid_spec=grid_spec,
        compiler_params=_cp(("arbitrary",)),
        name="moe_experts",
    )(plan, w13, w2, xs)


def _moe_combine_kernel(dest_ref, wt_ref, h_ref, x_ref, mod_ref, s13_ref, s2_ref, l2g_ref, l2b_ref, y_hbm,
                        o_ref, g_scr, sem, *, alpha, nt):
    s = pl.program_id(0)
    tm = h_ref.shape[0]
    ff = s2_ref.shape[0]
    rows = tm * ROWT
    slot_g = s % 2
    slot_c = 1 - slot_g

    def slot_copy(slot):
        return pltpu.make_async_copy(y_hbm.at[pl.ds(0, TOP_K * rows), :], g_scr.at[slot], sem.at[slot])

    @pl.when(s == 0)
    def _():
        g_scr[1] = jnp.zeros((TOP_K * rows, LANE), U32)

    @pl.when(s > 0)
    def _():
        slot_copy(slot_c).wait()

    for j in range(tm):
        for k in range(TOP_K):
            pltpu.make_async_copy(_tile_at(y_hbm, dest_ref[k, j]),
                                  g_scr.at[slot_g, pl.ds(k * rows + j * ROWT, ROWT), :],
                                  sem.at[slot_g]).start(priority=k % 2)
    hs = _bdot(h_ref[...], s13_ref[...])
    f = _bdot(_silu(hs[:, 0:ff]) * hs[:, ff:2 * ff], s2_ref[...])
    for k in range(TOP_K):
        f = f + wt_ref[:, k:k + 1] * _tile_rows_load(g_scr.at[slot_c], tm, base=k * rows)
    o_ref[...] = (_layer_norm(alpha * x_ref[...] + mod_ref[0, 5:6, :] * f) * l2g_ref[...] + l2b_ref[...])

    @pl.when(s == nt)
    def _():
        slot_copy(slot_g).wait()


def moe_combine(dest, wt_t, h2, x1, mod3, wts, y, row_of_tile, tm, alpha):
    N, D = h2.shape
    nt = N // tm
    full = lambda a: pl.BlockSpec(a.shape, lambda s: (0,) * a.ndim)
    kern = functools.partial(_moe_combine_kernel, alpha=alpha, nt=nt)
    nxt = lambda s: jnp.minimum(s, nt - 1)
    cur = lambda s: jnp.maximum(s - 1, 0)
    return pl.pallas_call(
        kern,
        out_shape=jax.ShapeDtypeStruct((N, D), F32),
        grid=(nt + 1,),
        in_specs=[pl.BlockSpec((TOP_K, tm), lambda s: (0, nxt(s)), memory_space=pltpu.SMEM),
                  pl.BlockSpec((tm, TOP_K), lambda s: (cur(s), 0)),
                  pl.BlockSpec((tm, D), lambda s: (cur(s), 0)),
                  pl.BlockSpec((tm, D), lambda s: (cur(s), 0)),
                  pl.BlockSpec((1, 6, D), lambda s: (row_of_tile(cur(s)), 0, 0))]
                 + [full(a) for a in wts] + [pl.BlockSpec(memory_space=pl.ANY)],
        out_specs=pl.BlockSpec((tm, D), lambda s: (cur(s), 0)),
        scratch_shapes=[pltpu.VMEM((2, TOP_K * tm * ROWT, LANE), U32), pltpu.SemaphoreType.DMA((2,))],
        compiler_params=_cp(("arbitrary",)),
        name="moe_combine",
    )(dest, wt_t, h2, x1, mod3, *wts, y)


def moe_ffn(h2, h2t, x1, mod3, mp, row_of_tile, tm, alpha):
    N = h2.shape[0]
    nb_total = (N * TOP_K + MOE_BLK - 1) // MOE_BLK + N_EXPERTS
    nbp = ((nb_total + LANE - 1) // LANE) * LANE
    tp = _tile(N, 256)
    eidx, wts = moe_router(h2, mp['rwt'], mp['bias'], tp)
    pstart, plan, blk = moe_counts(eidx, tp, nbp)
    dest = moe_dest(eidx, pstart, tp)
    xs = moe_scatter(dest, h2t, nb_total * MOE_BLK, tp)
    xs = moe_padfill(plan, xs)
    y = moe_experts(plan, xs, mp['w13'], mp['w2'], mp['layer'])
    return moe_combine(dest, wts.T, h2, x1, mod3, mp['comb'], y, row_of_tile, tm, alpha)


def token_mixers(x2, B, T, W, mod3, row_of_tile, lp, states, tm):
    pa, pbg, gg, u = inproj(x2, mod3, lp['w_in'], row_of_tile, tm)
    pac = short_conv(pa.reshape(B, T, PA_W), lp['taps'], W).reshape(B * T, PA_W)
    ss, sd, gb = rwkv_prep(pac, lp['prep'], tm)
    s_rwkv, s_gla, s_s5 = states
    y2, f_rwkv = rwkv_scan(ss, sd, s_rwkv, B, T)
    o2, f_gla = gla_scan(pbg, lp['gla_aup'], lp['gla_ab'], s_gla, B, T)
    yd, f_s5 = s5_scan(u.reshape(B, T, S5_W), lp['s5_a'], lp['s5_bbd'], lp['s5_cbd'], s_s5)
    return (y2, gb, o2, gg, yd.reshape(2, B * T, S5_W), u), (f_rwkv, f_gla, f_s5)


def zero_states(B):
    return ((jnp.zeros((B, RW // LANE, LANE, LANE), F32),) * 2,
            (jnp.zeros((B, RW, GLA_KP), F32),) * 2,
            jnp.zeros((2, B, S5_LANES), F32))


def _inproj_columns():
    r_cols = 3 * RW + 2 * DECAY_RANK + 2 * ICLR_RANK + GATE_RANK
    kd = GLA_HEADS * GLA_K
    gq, gk, gv = r_cols, r_cols + kd, r_cols + 2 * kd
    gg = gv + RW
    gad = gg + RW
    pc = gad + 2 * GLA_RANK
    z = lambda n: [-1] * n
    cols = list(range(0, r_cols)) + z(PA_W - r_cols)
    cols += list(range(gq, gq + kd)) + z(256 - kd)
    cols += list(range(gk, gk + kd)) + z(256 - kd)
    cols += list(range(gv, gv + RW))
    cols += list(range(gad, gad + 2 * GLA_RANK)) + z(LANE - 2 * GLA_RANK)
    cols += list(range(gg, gg + RW))
    cols += list(range(pc, pc + S5_W))
    assert len(cols) == NP_W
    return np.asarray(cols, np.int32)


def _head_block_ones():
    h = np.arange(RW) // HEAD
    return jnp.asarray(h[:, None] == h[None, :], BF16)


def _layer_params(l, p):
    D = p['w_in'].shape[1]
    cols = _inproj_columns()
    w_in = jnp.concatenate([p['w_in'][l], jnp.zeros((D, 1), F32)], axis=1)
    w_in = jnp.take(w_in, jnp.asarray(np.where(cols < 0, w_in.shape[1] - 1, cols)), axis=1).astype(BF16)
    taps = p['rwkv_conv'][l].reshape(9, -1)
    taps = jnp.concatenate([taps, jnp.zeros((9, PA_W - taps.shape[1]), F32)], axis=1)
    wup = jnp.zeros((LANE, 2 * RW), F32)
    aup = jnp.zeros((LANE, 2 * RW), F32)
    for d in range(2):
        wup = wup.at[d * DECAY_RANK:(d + 1) * DECAY_RANK, d * RW:(d + 1) * RW].set(p['rwkv_w_up'][l, d])
        o = 2 * DECAY_RANK + d * ICLR_RANK
        aup = aup.at[o:o + ICLR_RANK, d * RW:(d + 1) * RW].set(p['rwkv_a_up'][l, d])
    gup = jnp.zeros((LANE, RW), F32).at[0:GATE_RANK].set(p['rwkv_g_up'][l])
    row = lambda a: a.reshape(1, -1)
    prep = (wup, aup, gup, row(p['rwkv_w0'][l]), row(p['rwkv_a0'][l]), row(p['rwkv_k_k'][l]),
            row(p['rwkv_k_a'][l]), row(p['rwkv_r_k'][l]), _head_block_ones())
    kd = GLA_HEADS * GLA_K
    gla_aup = jnp.zeros((2, LANE, GLA_KP), F32)
    for d in range(2):
        gla_aup = gla_aup.at[d, d * GLA_RANK:(d + 1) * GLA_RANK, 0:kd].set(p['gla_a_up'][l, d])
    gla_ab = jnp.zeros((2, 1, GLA_KP), F32).at[:, 0, 0:kd].set(p['gla_a_bias'][l])
    lam_re, lam_im = p['s5_lam_re'][l], p['s5_lam_im'][l]
    dt = jnp.exp(p['s5_log_dt'][l])[:, :, None]
    zr, zi = lam_re[:, None, :] * dt, lam_im[:, None, :] * dt
    mag = jnp.exp(zr)
    ab_r, ab_i = mag * jnp.cos(zi), mag * jnp.sin(zi)
    den = (lam_re * lam_re + lam_im * lam_im)[:, None, :]
    f_r = ((ab_r - 1) * lam_re[:, None, :] + ab_i * lam_im[:, None, :]) / den
    f_i = (ab_i * lam_re[:, None, :] - (ab_r - 1) * lam_im[:, None, :]) / den
    b_re, b_im = p['s5_b_re'][l], p['s5_b_im'][l]
    bb_r = f_r[..., None] * b_re - f_i[..., None] * b_im
    bb_i = f_r[..., None] * b_im + f_i[..., None] * b_re
    eye_g = jnp.eye(S5_GROUPS, dtype=F32)
    half = S5_LANES // 2

    def in_blockdiag(bb):
        return jnp.einsum('dgpc,gh->dgchp', bb, eye_g).reshape(2, S5_W, half)

    def out_blockdiag(cc):
        return jnp.einsum('dgcp,gh->dgphc', cc, eye_g).reshape(2, half, S5_W)

    s5_bbd = jnp.concatenate([in_blockdiag(bb_r), in_blockdiag(bb_i)], axis=2).astype(BF16)
    s5_cbd = jnp.concatenate([out_blockdiag(p['s5_c_re'][l]), -out_blockdiag(p['s5_c_im'][l])], axis=1).astype(BF16)
    s5_a = jnp.concatenate([ab_r.reshape(2, 1, half), ab_i.reshape(2, 1, half)], axis=2)
    mix = (_head_block_ones(), row(p['rwkv_ln_g'][l]), row(p['rwkv_ln_b'][l]),
           row(jnp.tile(p['gla_norm_g'][l], GLA_HEADS)), row(p['s5_d'][l]),
           p['s5_glu_w'][l].astype(BF16), row(p['s5_glu_b'][l]), p['w_out'][l].astype(BF16),
           row(p['ln1_g'][l]), row(p['ln1_b'][l]))
    moe = dict(rwt=p['router_w'][l].T,
               bias=jnp.broadcast_to(p['router_bias'][l][:, None], (N_EXPERTS, LANE)),
               w13=p['exp_w13'], w2=p['exp_w2'], layer=l,
               comb=(p['sh_w13'][l].astype(BF16), p['sh_w2'][l].astype(BF16), row(p['ln2_g'][l]), row(p['ln2_b'][l])))
    return dict(w_in=w_in, taps=taps, prep=prep, gla_aup=gla_aup, gla_ab=gla_ab,
                s5_a=s5_a, s5_bbd=s5_bbd, s5_cbd=s5_cbd, mix=mix, moe=moe)


_ARG_NAMES = ('x', 'c', 'ctx', 'c_ctx', 'w_mod', 'b_mod', 'w_in', 'rwkv_conv', 'rwkv_w0', 'rwkv_w_up', 'rwkv_a0',
              'rwkv_a_up', 'rwkv_g_up', 'rwkv_k_k', 'rwkv_k_a', 'rwkv_r_k', 'rwkv_ln_g', 'rwkv_ln_b', 'gla_a_up',
              'gla_a_bias', 'gla_norm_g', 's5_lam_re', 's5_lam_im', 's5_log_dt', 's5_b_re', 's5_b_im', 's5_c_re',
              's5_c_im', 's5_d', 's5_glu_w', 's5_glu_b', 'w_out', 'ln1_g', 'ln1_b', 'router_w', 'router_bias',
              'exp_w13', 'exp_w2', 'sh_w13', 'sh_w2', 'ln2_g', 'ln2_b')


def _tile(n, pref):
    t = pref
    while n % t:
        t //= 2
    return t


def kernel(x, c, ctx, c_ctx, w_mod, b_mod, w_in, rwkv_conv, rwkv_w0, rwkv_w_up, rwkv_a0, rwkv_a_up, rwkv_g_up,
           rwkv_k_k, rwkv_k_a, rwkv_r_k, rwkv_ln_g, rwkv_ln_b, gla_a_up, gla_a_bias, gla_norm_g, s5_lam_re,
           s5_lam_im, s5_log_dt, s5_b_re, s5_b_im, s5_c_re, s5_c_im, s5_d, s5_glu_w, s5_glu_b, w_out, ln1_g,
           ln1_b, router_w, router_bias, exp_w13, exp_w2, sh_w13, sh_w2, ln2_g, ln2_b):
    p = dict(zip(_ARG_NAMES, (x, c, ctx, c_ctx, w_mod, b_mod, w_in, rwkv_conv, rwkv_w0, rwkv_w_up, rwkv_a0,
                              rwkv_a_up, rwkv_g_up, rwkv_k_k, rwkv_k_a, rwkv_r_k, rwkv_ln_g, rwkv_ln_b, gla_a_up,
                              gla_a_bias, gla_norm_g, s5_lam_re, s5_lam_im, s5_log_dt, s5_b_re, s5_b_im, s5_c_re,
                              s5_c_im, s5_d, s5_glu_w, s5_glu_b, w_out, ln1_g, ln1_b, router_w, router_bias,
                              exp_w13, exp_w2, sh_w13, sh_w2, ln2_g, ln2_b)))
    B, T, D = x.shape
    TC = ctx.shape[1]
    L = w_mod.shape[0]
    alpha = (2 * L) ** 0.25
    n_lat, n_ctx = B * T, B * TC
    R = ((B + 1 + SUB - 1) // SUB) * SUB
    cc = jnp.zeros((R, D), F32).at[0:B].set(c).at[B].set(c_ctx)
    mod = mod_table(cc, w_mod, b_mod)
    tm = _tile(T, 512)
    tmc = _tile(n_ctx, 512)
    tmm = min(_tile(T, 128), _tile(n_ctx, 128))
    lat_row = lambda i: (i * tm) // T
    ctx_row = lambda i: B
    x2 = x.reshape(n_lat, D)
    c2 = ctx.reshape(n_ctx, D)
    for l in range(L):
        last = l == L - 1
        lp = _layer_params(l, p)
        mod3 = mod[l].reshape(R, 6, D)
        outs_c, st_c = token_mixers(c2, B, TC, TC, mod3, ctx_row, lp, zero_states(B), tmc)
        outs, _ = token_mixers(x2, B, T, GRID_W, mod3, lat_row, lp, st_c, tm)
        x1, h2, h2t = mix_out(*outs, x2, mod3, lp['mix'], lat_row, tm, alpha)
        if last:
            x2 = moe_ffn(h2, h2t, x1, mod3, lp['moe'], lambda i: (i * tmm) // T, tmm, alpha)
        else:
            c1, hc2, hc2t = mix_out(*outs_c, c2, mod3, lp['mix'], ctx_row, tmc, alpha)
            row_all = lambda i: jnp.where(i * tmm < n_ctx, B, (i * tmm - n_ctx) // T)
            out = moe_ffn(jnp.concatenate([hc2, h2]), jnp.concatenate([hc2t, h2t]), jnp.concatenate([c1, x1]),
                          mod3, lp['moe'], row_all, tmm, alpha)
            c2, x2 = out[:n_ctx], out[n_ctx:]
    return x2.reshape(B, T, D)
```

```python
import functools
import math

import numpy as np
import jax
import jax.numpy as jnp
from jax import lax
from jax.experimental import pallas as pl
from jax.experimental.pallas import tpu as pltpu

F32 = jnp.float32
BF16 = jnp.bfloat16
I32 = jnp.int32

GRID_W = 64
RWKV_HEADS = 6
HEAD = 64
RW = RWKV_HEADS * HEAD
DECAY_RANK = 32
ICLR_RANK = 32
GATE_RANK = 64
RWKV_GN_EPS = 64e-5
GLA_HEADS = 6
GLA_K = 32
GLA_RANK = 16
GLA_TEMP = 16.0
S5_GROUPS = 16
S5_GROUP = 16
S5_STATE = 64
S5_W = S5_GROUPS * S5_GROUP
S5_LANES = 2 * S5_GROUPS * S5_STATE
N_EXPERTS = 256
TOP_K = 8
N_EGROUPS = 8
TOPK_GROUPS = 4
ROUTE_SCALE = 2.5
LN_EPS = 1e-6
CHUNK = 64
MOE_BLK = 256
LANE = 128
SUB = 8
VMEM_LIMIT = 56 * 1024 * 1024

PA_W = 3 * RW + 2 * LANE
PBG_W = 256 + 256 + RW + LANE
NP_W = PA_W + PBG_W + RW + S5_W


def _cp(sem):
    return pltpu.CompilerParams(dimension_semantics=sem, vmem_limit_bytes=VMEM_LIMIT)


def _sigmoid(x):
    return 1.0 / (1.0 + jnp.exp(-x))


def _silu(x):
    return x * _sigmoid(x)


def _bdot(a, b, dims=None):
    a = a.astype(BF16)
    b = b.astype(BF16)
    if dims is None:
        return jnp.dot(a, b, preferred_element_type=F32)
    return lax.dot_general(a, b, dims, preferred_element_type=F32)


def _split2(x):
    hi = x.astype(BF16)
    lo = (x - hi.astype(F32)).astype(BF16)
    return hi, lo


def _split3(x):
    hi = x.astype(BF16)
    r = x - hi.astype(F32)
    mid = r.astype(BF16)
    lo = (r - mid.astype(F32)).astype(BF16)
    return hi, mid, lo


NT_DIMS = (((1,), (1,)), ((), ()))
TN_DIMS = (((0,), (0,)), ((), ()))


def _dot33(a, b, dims=None):
    ah, al = _split2(a)
    bh, bl = _split2(b)
    return _bdot(ah, bh, dims) + (_bdot(ah, bl, dims) + _bdot(al, bh, dims))


def _dot_exact_lhs(m_exact, x, dims=None):
    h, m, l = _split3(x)
    return _bdot(m_exact, h, dims) + (_bdot(m_exact, m, dims) + _bdot(m_exact, l, dims))


def _dot_exact_rhs(x, m_exact, dims=None):
    h, m, l = _split3(x)
    return _bdot(h, m_exact, dims) + (_bdot(m, m_exact, dims) + _bdot(l, m_exact, dims))


def _dot_split_rhs(x, m_exact, dims=None):
    h, l = _split2(x)
    return _bdot(h, m_exact, dims) + _bdot(l, m_exact, dims)


def _layer_norm(x):
    mu = jnp.mean(x, axis=-1, keepdims=True)
    xc = x - mu
    var = jnp.mean(xc * xc, axis=-1, keepdims=True)
    return xc * lax.rsqrt(var + LN_EPS)


def _mod_kernel(c_ref, w_ref, b_ref, o_ref):
    s = _silu(c_ref[...])
    o_ref[0] = _dot33(s, w_ref[0]) + b_ref[0]


def mod_table(cc, w_mod, b_mod):
    L, D, D6 = w_mod.shape
    R = cc.shape[0]
    tn = 1536
    return pl.pallas_call(
        _mod_kernel,
        out_shape=jax.ShapeDtypeStruct((L, R, D6), F32),
        grid=(L, D6 // tn),
        in_specs=[pl.BlockSpec((R, D), lambda l, j: (0, 0)),
                  pl.BlockSpec((1, D, tn), lambda l, j: (l, 0, j)),
                  pl.BlockSpec((1, 1, tn), lambda l, j: (l, 0, j))],
        out_specs=pl.BlockSpec((1, R, tn), lambda l, j: (l, 0, j)),
        compiler_params=_cp(("arbitrary", "arbitrary")),
        name="mod_table",
    )(cc, w_mod, b_mod.reshape(L, 1, D6))


def _inproj_kernel(x_ref, mod_ref, w_ref, pa_ref, pbg_ref, gg_ref, u_ref):
    x = x_ref[...]
    h = _layer_norm(x) * (1.0 + mod_ref[0, 1:2, :]) + mod_ref[0, 0:1, :]
    hb = h.astype(BF16)
    o = 0
    for ref in (pa_ref, pbg_ref, gg_ref, u_ref):
        w = ref.shape[-1]
        ref[...] = jnp.dot(hb, w_ref[:, o:o + w], preferred_element_type=F32)
        o += w


def inproj(x2, mod3, w_bf, row_of_tile, tm):
    N, D = x2.shape
    return pl.pallas_call(
        _inproj_kernel,
        out_shape=[jax.ShapeDtypeStruct((N, PA_W), F32),
                   jax.ShapeDtypeStruct((N, PBG_W), F32),
                   jax.ShapeDtypeStruct((N, RW), F32),
                   jax.ShapeDtypeStruct((N, S5_W), F32)],
        grid=(N // tm,),
        in_specs=[pl.BlockSpec((tm, D), lambda i: (i, 0)),
                  pl.BlockSpec((1, 6, D), lambda i: (row_of_tile(i), 0, 0)),
                  pl.BlockSpec((D, NP_W), lambda i: (0, 0))],
        out_specs=[pl.BlockSpec((tm, PA_W), lambda i: (i, 0)),
                   pl.BlockSpec((tm, PBG_W), lambda i: (i, 0)),
                   pl.BlockSpec((tm, RW), lambda i: (i, 0)),
                   pl.BlockSpec((tm, S5_W), lambda i: (i, 0))],
        compiler_params=_cp(("arbitrary",)),
        name="inproj",
    )(x2, mod3, w_bf)


CONV_PAD = 72


def _conv_kernel(x_ref, taps_ref, o_ref, buf_ref, *, T, W, vertical):
    pad = CONV_PAD
    zeros = jnp.zeros((pad, LANE), F32)
    buf_ref[0:pad, :] = zeros
    buf_ref[pad + T:pad + T + pad, :] = zeros
    buf_ref[pad:pad + T, :] = x_ref[0]
    ch = min(T, 256)
    col = lax.broadcasted_iota(I32, (ch, LANE), 0) & (W - 1)
    left_ok = col >= 1
    right_ok = col <= W - 2
    for c in range(T // ch):
        base = pad + c * ch
        acc = jnp.zeros((ch, LANE), F32)
        for dr in ((0, 1, 2) if vertical else (1,)):
            for dc in range(3):
                off = (dr - 1) * W + (dc - 1)
                v = buf_ref[base + off:base + off + ch, :]
                if dc == 0:
                    v = jnp.where(left_ok, v, 0.0)
                elif dc == 2:
                    v = jnp.where(right_ok, v, 0.0)
                acc = acc + v * taps_ref[3 * dr + dc:3 * dr + dc + 1, :]
        o_ref[0, c * ch:(c + 1) * ch, :] = acc


def short_conv(pa3, taps9, W):
    B, T, C = pa3.shape
    vertical = T > W
    assert W & (W - 1) == 0 and (not vertical or W + 1 <= CONV_PAD)
    kern = functools.partial(_conv_kernel, T=T, W=W, vertical=vertical)
    return pl.pallas_call(
        kern,
        out_shape=jax.ShapeDtypeStruct((B, T, C), F32),
        grid=(B, C // LANE),
        in_specs=[pl.BlockSpec((1, T, LANE), lambda b, j: (b, 0, j)),
                  pl.BlockSpec((9, LANE), lambda b, j: (0, j))],
        out_specs=pl.BlockSpec((1, T, LANE), lambda b, j: (b, 0, j)),
        scratch_shapes=[pltpu.VMEM((T + 2 * CONV_PAD, LANE), F32)],
        compiler_params=_cp(("arbitrary", "arbitrary")),
        name="short_conv",
    )(pa3, taps9)


def _rwkv_prep_kernel(pa_ref, wup_ref, aup_ref, gup_ref, w0_ref, a0_ref, kk_ref, ka_ref, rk_ref, hb_ref,
                      ss_ref, sd_ref, gb_ref):
    r = pa_ref[:, 0:RW]
    k = pa_ref[:, RW:2 * RW]
    v = pa_ref[:, 2 * RW:3 * RW]
    wa = pa_ref[:, 3 * RW:3 * RW + LANE]
    gd = pa_ref[:, 3 * RW + LANE:3 * RW + 2 * LANE]
    z = w0_ref[...] + _dot33(jnp.tanh(wa), wup_ref[...])
    lw = -_sigmoid(z) * math.exp(-0.5)
    a = _sigmoid(a0_ref[...] + _bdot(wa, aup_ref[...]))
    g = _bdot(_sigmoid(gd), gup_ref[...])
    hb = hb_ref[...]
    kk = k * kk_ref[...]
    kk = kk * lax.rsqrt(_dot_split_rhs(kk * kk, hb) + 1e-12)
    ka = ka_ref[...]
    ss_ref[:, 0:RW] = r
    ss_ref[:, RW:2 * RW] = v
    ss_ref[:, 2 * RW:3 * RW] = kk
    rk2 = jnp.zeros_like(r)
    for d in range(2):
        ad = a[:, d * RW:(d + 1) * RW]
        k2 = k * (1.0 + (ad - 1.0) * ka)
        sd_ref[d, :, 0:RW] = lw[:, d * RW:(d + 1) * RW]
        sd_ref[d, :, RW:2 * RW] = k2
        sd_ref[d, :, 2 * RW:3 * RW] = kk * ad
        rk2 = rk2 + r * k2
    bonus = _dot_split_rhs(rk2 * rk_ref[...], hb) * v
    gb_ref[:, 0:RW] = g
    gb_ref[:, RW:2 * RW] = bonus


def rwkv_prep(pa2, wts, tm):
    N = pa2.shape[0]
    full = lambda a: pl.BlockSpec(a.shape, lambda i: (0,) * a.ndim)
    return pl.pallas_call(
        _rwkv_prep_kernel,
        out_shape=[jax.ShapeDtypeStruct((N, 3 * RW), F32),
                   jax.ShapeDtypeStruct((2, N, 3 * RW), F32),
                   jax.ShapeDtypeStruct((N, 2 * RW), F32)],
        grid=(N // tm,),
        in_specs=[pl.BlockSpec((tm, PA_W), lambda i: (i, 0))] + [full(a) for a in wts],
        out_specs=[pl.BlockSpec((tm, 3 * RW), lambda i: (i, 0)),
                   pl.BlockSpec((2, tm, 3 * RW), lambda i: (0, i, 0)),
                   pl.BlockSpec((tm, 2 * RW), lambda i: (i, 0))],
        compiler_params=_cp(("arbitrary",)),
        name="rwkv_prep",
    )(pa2, *wts)


RWKV_GROUP = 8


def _rwkv_scan_kernel(ss_ref, sd_ref, s0_ref, y_ref, sfin_ref, s_scr, *, nsteps, group, reverse):
    n = pl.program_id(1)
    C = CHUNK
    P = 2 * C
    npair = RW // LANE

    @pl.when(n == 0)
    def _():
        s_scr[...] = s0_ref[0]

    row = lax.broadcasted_iota(I32, (P, P), 0)
    col = lax.broadcasted_iota(I32, (P, P), 1)
    same = (row >> 6) == (col >> 6)
    dlt = (col & (C - 1)) - (row & (C - 1)) if reverse else (row & (C - 1)) - (col & (C - 1))
    strict = same & (dlt > 0)
    incl = same & (dlt >= 0)
    eye = (row == col).astype(F32)
    lvl_masks = [((row >> (lvl + 1)) == (col >> (lvl + 1))) & ((row >> lvl) != (col >> lvl)) for lvl in range(6)]
    rc = lax.broadcasted_iota(I32, (C, C), 0)
    cc = lax.broadcasted_iota(I32, (C, C), 1)
    tri = jnp.where((cc >= rc) if reverse else (rc >= cc), 1.0, 0.0).astype(BF16)
    head0 = lax.broadcasted_iota(I32, (C, LANE), 1) < HEAD

    def stack(x):
        return jnp.concatenate([jnp.where(head0, x, 0.0), jnp.where(head0, 0.0, x)], axis=0)

    streams = [(g, p) for g in range(group) for p in range(npair)]
    tm_ = {}
    for (g, p) in streams:
        t0, t1 = g * C, (g + 1) * C
        lo, hi = p * LANE, (p + 1) * LANE
        lw = sd_ref[0, t0:t1, lo:hi]
        tm_[(g, p)] = dict(lw=lw, cl=_dot_exact_lhs(tri, lw))
    for (g, p) in streams:
        t = tm_[(g, p)]
        t0, t1 = g * C, (g + 1) * C
        lo, hi = p * LANE, (p + 1) * LANE
        r = ss_ref[t0:t1, lo:hi]
        v = ss_ref[t0:t1, RW + lo:RW + hi]
        kk = ss_ref[t0:t1, 2 * RW + lo:2 * RW + hi]
        k2 = sd_ref[0, t0:t1, RW + lo:RW + hi]
        b = sd_ref[0, t0:t1, 2 * RW + lo:2 * RW + hi]
        cl, lw = t['cl'], t['lw']
        t['ptot'] = jnp.exp(jnp.sum(lw, axis=0, keepdims=True))
        pinv = jnp.exp(-cl)
        left = jnp.concatenate([stack(-kk * jnp.exp(cl - lw)), stack(r * jnp.exp(cl))], axis=0)
        right = jnp.concatenate([stack(b * pinv), stack(k2 * pinv)], axis=0)
        t['v_st'] = stack(v)
        t['left'] = left.astype(BF16)
        t['bk'] = (right * t['ptot']).astype(BF16)
        aa = _bdot(left, right, NT_DIMS)
        t['nmat'] = jnp.where(strict, aa[0:P, 0:P], 0.0)
        t['a_ak'] = jnp.where(strict, aa[0:P, P:2 * P], 0.0)
        t['a_rbk'] = jnp.concatenate([jnp.where(incl, aa[P:2 * P, 0:P], 0.0),
                                      jnp.where(incl, aa[P:2 * P, P:2 * P], 0.0)], axis=1).astype(BF16)
        t['tinv'] = eye
    for sk in streams:
        t = tm_[sk]
        t['akv'] = _bdot(t['a_ak'], t['v_st'])
    for m in lvl_masks:
        for sk in streams:
            t = tm_[sk]
            t['et'] = _bdot(jnp.where(m, t['nmat'], 0.0), t['tinv'])
        for sk in streams:
            t = tm_[sk]
            t['tinv'] = t['tinv'] + _bdot(t['tinv'], t['et'])
    state = [s_scr[p] for p in range(npair)]
    pairs = range(npair)
    for g in (range(group - 1, -1, -1) if reverse else range(group)):
        ts = [tm_[(g, p)] for p in pairs]
        a_s = [_bdot(ts[p]['left'], state[p], NT_DIMS) for p in pairs]
        u = [_bdot(ts[p]['tinv'], a_s[p][0:P] + ts[p]['akv']) for p in pairs]
        uv = [jnp.concatenate([u[p], ts[p]['v_st']], axis=0) for p in pairs]
        state = [state[p] * ts[p]['ptot'] + _bdot(uv[p], ts[p]['bk'], TN_DIMS) for p in pairs]
        for p in pairs:
            y_st = a_s[p][P:2 * P] + _bdot(ts[p]['a_rbk'], uv[p])
            y_ref[g * C:(g + 1) * C, p * LANE:(p + 1) * LANE] = y_st[0:C] + y_st[C:P]
    for p in pairs:
        s_scr[p] = state[p]

    @pl.when(n == nsteps - 1)
    def _():
        sfin_ref[0] = s_scr[...]


def rwkv_scan_dir(ss, sd, s0, B, T, d):
    nc = T // CHUNK
    group = min(RWKV_GROUP, nc)
    nsteps = nc // group
    npair = RW // LANE
    rows = group * CHUNK
    blk = (lambda b, n: b * nsteps + nsteps - 1 - n) if d else (lambda b, n: b * nsteps + n)
    kern = functools.partial(_rwkv_scan_kernel, nsteps=nsteps, group=group, reverse=bool(d))
    return pl.pallas_call(
        kern,
        out_shape=[jax.ShapeDtypeStruct((B * T, RW), F32),
                   jax.ShapeDtypeStruct((B, npair, LANE, LANE), F32)],
        grid=(B, nsteps),
        in_specs=[pl.BlockSpec((rows, 3 * RW), lambda b, n: (blk(b, n), 0)),
                  pl.BlockSpec((1, rows, 3 * RW), lambda b, n: (d, blk(b, n), 0)),
                  pl.BlockSpec((1, npair, LANE, LANE), lambda b, n: (b, 0, 0, 0))],
        out_specs=[pl.BlockSpec((rows, RW), lambda b, n: (blk(b, n), 0)),
                   pl.BlockSpec((1, npair, LANE, LANE), lambda b, n: (b, 0, 0, 0))],
        scratch_shapes=[pltpu.VMEM((npair, LANE, LANE), F32)],
        compiler_params=_cp(("arbitrary", "arbitrary")),
        name="rwkv_scan_bwd" if d else "rwkv_scan_fwd",
    )(ss, sd, s0)


def rwkv_scan(ss, sd, s0, B, T):
    y0, f0 = rwkv_scan_dir(ss, sd, s0[0], B, T, 0)
    y1, f1 = rwkv_scan_dir(ss, sd, s0[1], B, T, 1)
    return (y0, y1), (f0, f1)


GLA_KP = 256


GLA_GROUP = 8


def _gla_kernel(pbg_ref, aup_ref, ab_ref, s0_ref, o_ref, sfin_ref, s_scr, *, nsteps, group, reverse):
    n = pl.program_id(1)
    C = CHUNK
    R = group * C

    @pl.when(n == 0)
    def _():
        s_scr[...] = s0_ref[0]

    q = pbg_ref[:, 0:GLA_KP]
    k = pbg_ref[:, GLA_KP:2 * GLA_KP]
    v = pbg_ref[:, 2 * GLA_KP:2 * GLA_KP + RW]
    ad = pbg_ref[:, 2 * GLA_KP + RW:2 * GLA_KP + RW + LANE]
    x = _dot33(ad, aup_ref[0]) + ab_ref[0]
    la = (jnp.minimum(x, 0.0) - jnp.log(1.0 + jnp.exp(-jnp.abs(x)))) * (1.0 / GLA_TEMP)
    chunks = range(group)
    sl = [slice(g * C, (g + 1) * C) for g in chunks]
    rr = lax.broadcasted_iota(I32, (C, C), 0)
    rc = lax.broadcasted_iota(I32, (C, C), 1)
    tri = jnp.where((rc >= rr) if reverse else (rr >= rc), 1.0, 0.0).astype(BF16)
    bcums = [_dot_exact_lhs(tri, la[sl[g]]) for g in chunks]
    last = 0 if reverse else C - 1
    tots = [bc[last:last + 1, :] for bc in bcums]
    bcum = jnp.concatenate(bcums, axis=0)
    tot = jnp.concatenate([jnp.broadcast_to(t, (C, GLA_KP)) for t in tots], axis=0)
    q_in = q * jnp.exp(bcum) * (GLA_K ** -0.5)
    k_in = k * jnp.exp(-bcum)
    k_st = k * jnp.exp(tot - bcum)
    dn = [jnp.exp(t) for t in tots]
    klane = lax.broadcasted_iota(I32, (C, GLA_KP), 1)
    rt = lax.broadcasted_iota(I32, (GLA_HEADS * C, C), 0) & (C - 1)
    ct = lax.broadcasted_iota(I32, (GLA_HEADS * C, C), 1)
    causal = (ct >= rt) if reverse else (rt >= ct)
    vlane = lax.broadcasted_iota(I32, (C, RW), 1)
    sv = lax.broadcasted_iota(I32, (RW, GLA_KP), 0) >> 6
    sk = lax.broadcasted_iota(I32, (RW, GLA_KP), 1) >> 5
    q_rows = [jnp.concatenate([jnp.where((klane >> 5) == h, q_in[sl[g]], 0.0) for h in range(GLA_HEADS)],
                              axis=0).astype(BF16) for g in chunks]
    att = [jnp.where(causal, _bdot(q_rows[g], k_in[sl[g]], NT_DIMS), 0.0) for g in chunks]
    o_rows = [_bdot(att[g], v[sl[g]]) for g in chunks]
    kv = [jnp.where(sv == sk, _bdot(v[sl[g]], k_st[sl[g]], TN_DIMS), 0.0) for g in chunks]
    s = s_scr[...]
    for g in (reversed(chunks) if reverse else chunks):
        o = _bdot(q_in[sl[g]], s, NT_DIMS)
        for h in range(GLA_HEADS):
            o = o + jnp.where((vlane >> 6) == h, o_rows[g][h * C:(h + 1) * C], 0.0)
        o_ref[sl[g], :] = o
        s = s * dn[g] + kv[g]
    s_scr[...] = s

    @pl.when(n == nsteps - 1)
    def _():
        sfin_ref[0] = s_scr[...]


def gla_scan_dir(pbg, aup, ab, s0, B, T, d):
    nc = T // CHUNK
    group = min(GLA_GROUP, nc)
    nsteps = nc // group
    rows = group * CHUNK
    blk = (lambda b, n: b * nsteps + nsteps - 1 - n) if d else (lambda b, n: b * nsteps + n)
    kern = functools.partial(_gla_kernel, nsteps=nsteps, group=group, reverse=bool(d))
    return pl.pallas_call(
        kern,
        out_shape=[jax.ShapeDtypeStruct((B * T, RW), F32),
                   jax.ShapeDtypeStruct((B, RW, GLA_KP), F32)],
        grid=(B, nsteps),
        in_specs=[pl.BlockSpec((rows, PBG_W), lambda b, n: (blk(b, n), 0)),
                  pl.BlockSpec((1, LANE, GLA_KP), lambda b, n: (d, 0, 0)),
                  pl.BlockSpec((1, 1, GLA_KP), lambda b, n: (d, 0, 0)),
                  pl.BlockSpec((1, RW, GLA_KP), lambda b, n: (b, 0, 0))],
        out_specs=[pl.BlockSpec((rows, RW), lambda b, n: (blk(b, n), 0)),
                   pl.BlockSpec((1, RW, GLA_KP), lambda b, n: (b, 0, 0))],
        scratch_shapes=[pltpu.VMEM((RW, GLA_KP), F32)],
        compiler_params=_cp(("arbitrary", "arbitrary")),
        name="gla_scan_bwd" if d else "gla_scan_fwd",
    )(pbg, aup, ab, s0)


def gla_scan(pbg, aup, ab, s0, B, T):
    o0, f0 = gla_scan_dir(pbg, aup, ab, s0[0], B, T, 0)
    o1, f1 = gla_scan_dir(pbg, aup, ab, s0[1], B, T, 1)
    return (o0, o1), (f0, f1)


S5_PITCH = 72


def _s5_kernel(u_ref, a_ref, bbd_ref, cbd_ref, s0_ref, y_ref, sfin_ref, x_scr, st_scr, *, nc, nb):
    d = pl.program_id(0)
    n = pl.program_id(1)
    C = CHUNK
    half = S5_LANES // 2

    @pl.when(n == 0)
    def _():
        st_scr[...] = s0_ref[0]

    nslab = S5_LANES // LANE
    bbd = bbd_ref[0]
    for b in range(nb):
        bu = _bdot(u_ref[b], bbd)
        for j in range(nslab):
            x_scr[j, b * S5_PITCH:b * S5_PITCH + C, :] = bu[:, j * LANE:(j + 1) * LANE]
    a_re = a_ref[0, :, 0:half]
    a_im = a_ref[0, :, half:S5_LANES]

    def step(i, carry):
        re, im = carry
        t = i + d * (C - 1 - 2 * i)
        rows = pl.ds(t, nb, stride=S5_PITCH)
        bu = jnp.concatenate([x_scr[j, rows, :] for j in range(nslab)], axis=1)
        nre = a_re * re - a_im * im + bu[:, 0:half]
        nim = a_re * im + a_im * re + bu[:, half:S5_LANES]
        for j in range(nslab // 2):
            x_scr[j, rows, :] = nre[:, j * LANE:(j + 1) * LANE]
            x_scr[nslab // 2 + j, rows, :] = nim[:, j * LANE:(j + 1) * LANE]
        return nre, nim

    st = st_scr[...]
    re, im = lax.fori_loop(0, C, step, (st[:, 0:half], st[:, half:S5_LANES]), unroll=4)
    st_scr[...] = jnp.concatenate([re, im], axis=1)
    cbd = cbd_ref[0]
    for b in range(nb):
        xs = jnp.concatenate([x_scr[j, b * S5_PITCH:b * S5_PITCH + C, :] for j in range(nslab)], axis=1)
        y_ref[0, b] = _bdot(xs, cbd)

    @pl.when(n == nc - 1)
    def _():
        sfin_ref[0] = st_scr[...]


def s5_scan(u3, a_bar, bbd, cbd, s0):
    B, T, _ = u3.shape
    nc = T // CHUNK
    chunk = lambda d, n: n + d * (nc - 1 - 2 * n)
    kern = functools.partial(_s5_kernel, nc=nc, nb=B)
    return pl.pallas_call(
        kern,
        out_shape=[jax.ShapeDtypeStruct((2, B, T, S5_W), F32),
                   jax.ShapeDtypeStruct((2, B, S5_LANES), F32)],
        grid=(2, nc),
        in_specs=[pl.BlockSpec((B, CHUNK, S5_W), lambda d, n: (0, chunk(d, n), 0)),
                  pl.BlockSpec((1, 1, S5_LANES), lambda d, n: (d, 0, 0)),
                  pl.BlockSpec((1, S5_W, S5_LANES), lambda d, n: (d, 0, 0)),
                  pl.BlockSpec((1, S5_LANES, S5_W), lambda d, n: (d, 0, 0)),
                  pl.BlockSpec((1, B, S5_LANES), lambda d, n: (d, 0, 0))],
        out_specs=[pl.BlockSpec((1, B, CHUNK, S5_W), lambda d, n: (d, 0, chunk(d, n), 0)),
                   pl.BlockSpec((1, B, S5_LANES), lambda d, n: (d, 0, 0))],
        scratch_shapes=[pltpu.VMEM((S5_LANES // LANE, B * S5_PITCH, LANE), F32),
                        pltpu.VMEM((B, S5_LANES), F32)],
        compiler_params=_cp(("arbitrary", "arbitrary")),
        name="s5_scan",
    )(u3, a_bar, bbd, cbd, s0)


ROWT = 4
U32 = jnp.uint32


def _pack_bf16_pair(a, b):
    au = lax.bitcast_convert_type(a.astype(BF16).astype(F32), U32)
    bu = lax.bitcast_convert_type(b.astype(BF16).astype(F32), U32)
    return (au >> 16) | bu


def _unpack_bf16_pair(w):
    return (lax.bitcast_convert_type(w << 16, F32), lax.bitcast_convert_type(w & jnp.uint32(0xFFFF0000), F32))


def _tile_rows_store(ref, val):
    tm = val.shape[0]
    assert val.shape[1] == 2 * ROWT * LANE
    for s in range(ROWT):
        ref[pl.ds(s, tm, stride=ROWT), :] = _pack_bf16_pair(val[:, 2 * s * LANE:(2 * s + 1) * LANE],
                                                            val[:, (2 * s + 1) * LANE:(2 * s + 2) * LANE])


def _tile_rows_load(ref, tm, base=0):
    parts = []
    for s in range(ROWT):
        parts.extend(_unpack_bf16_pair(ref[pl.ds(base + s, tm, stride=ROWT), :]))
    return jnp.concatenate(parts, axis=1)


def _mix_out_kernel(y0_ref, y1_ref, gb_ref, o0_ref, o1_ref, gg_ref, yd_ref, u_ref, x_ref, mod_ref,
                    hb_ref, lng_ref, lnb_ref, gng_ref, s5d_ref, gluw_ref, glub_ref, wout_ref, l1g_ref, l1b_ref,
                    x1_ref, h2_ref, h2t_ref, *, alpha):
    hb = hb_ref[...]
    inv = 1.0 / HEAD
    y = y0_ref[...] + y1_ref[...]
    yc = y - _dot_split_rhs(y, hb) * inv
    var = _dot_split_rhs(yc * yc, hb) * inv
    gn = yc * lax.rsqrt(var + RWKV_GN_EPS) * lng_ref[...] + lnb_ref[...]
    y_a = (gn + gb_ref[:, RW:2 * RW]) * gb_ref[:, 0:RW]
    o = o0_ref[...] + o1_ref[...]
    o = o * lax.rsqrt(_dot_split_rhs(o * o, hb) * inv + 1e-6) * gng_ref[...]
    y_b = o * _silu(gg_ref[...])
    c = s5d_ref[...] * u_ref[...] + yd_ref[0] + yd_ref[1]
    c = 0.5 * c * (1.0 + jnp.tanh(math.sqrt(2.0 / math.pi) * (c + 0.044715 * (c * c * c))))
    y_c = c * _sigmoid(_bdot(c, gluw_ref[...]) + glub_ref[...])
    y_mix = (_bdot(y_a, wout_ref[0:RW, :]) + _bdot(y_b, wout_ref[RW:2 * RW, :])
             + _bdot(y_c, wout_ref[2 * RW:2 * RW + S5_W, :]))
    x1 = _layer_norm(alpha * x_ref[...] + mod_ref[0, 2:3, :] * y_mix) * l1g_ref[...] + l1b_ref[...]
    x1_ref[...] = x1
    h2 = _layer_norm(x1) * (1.0 + mod_ref[0, 4:5, :]) + mod_ref[0, 3:4, :]
    h2_ref[...] = h2
    _tile_rows_store(h2t_ref, h2)


def mix_out(y01, gb, o01, gg, yd, u, x2, mod3, wts, row_of_tile, tm, alpha):
    N, D = x2.shape
    full = lambda a: pl.BlockSpec(a.shape, lambda i: (0,) * a.ndim)
    kern = functools.partial(_mix_out_kernel, alpha=alpha)
    return pl.pallas_call(
        kern,
        out_shape=[jax.ShapeDtypeStruct((N, D), F32),
                   jax.ShapeDtypeStruct((N, D), F32),
                   jax.ShapeDtypeStruct((N * ROWT, LANE), U32)],
        grid=(N // tm,),
        in_specs=[pl.BlockSpec((tm, RW), lambda i: (i, 0)),
                  pl.BlockSpec((tm, RW), lambda i: (i, 0)),
                  pl.BlockSpec((tm, 2 * RW), lambda i: (i, 0)),
                  pl.BlockSpec((tm, RW), lambda i: (i, 0)),
                  pl.BlockSpec((tm, RW), lambda i: (i, 0)),
                  pl.BlockSpec((tm, RW), lambda i: (i, 0)),
                  pl.BlockSpec((2, tm, S5_W), lambda i: (0, i, 0)),
                  pl.BlockSpec((tm, S5_W), lambda i: (i, 0)),
                  pl.BlockSpec((tm, D), lambda i: (i, 0)),
                  pl.BlockSpec((1, 6, D), lambda i: (row_of_tile(i), 0, 0))] + [full(a) for a in wts],
        out_specs=[pl.BlockSpec((tm, D), lambda i: (i, 0)),
                   pl.BlockSpec((tm, D), lambda i: (i, 0)),
                   pl.BlockSpec((tm * ROWT, LANE), lambda i: (i, 0))],
        compiler_params=_cp(("arbitrary",)),
        name="mix_out",
    )(y01[0], y01[1], gb, o01[0], o01[1], gg, yd, u, x2, mod3, *wts)


def _first_max(x, idx, big):
    m = jnp.max(x, axis=0, keepdims=True)
    first = jnp.min(jnp.where(x == m, idx, big), axis=0, keepdims=True)
    return m, first


def _router_kernel(h_ref, rwt_ref, bias_ref, e_ref, w_ref):
    tm = h_ref.shape[0]
    gsz = N_EXPERTS // N_EGROUPS
    ninf = -jnp.inf
    s = _sigmoid(_dot33(rwt_ref[...], h_ref[...], NT_DIMS))
    ssel = s + bias_ref[:, 0:1]
    gi = lax.broadcasted_iota(I32, (gsz, tm), 0)
    gscore = []
    for g in range(N_EGROUPS):
        xg = ssel[g * gsz:(g + 1) * gsz, :]
        m1, i1 = _first_max(xg, gi, gsz)
        m2 = jnp.max(jnp.where(gi == i1, ninf, xg), axis=0, keepdims=True)
        gscore.append(m1 + m2)
    cur = jnp.concatenate(gscore, axis=0)
    gidx = lax.broadcasted_iota(I32, (N_EGROUPS, tm), 0)
    picked = jnp.zeros((N_EGROUPS, tm), F32)
    for _ in range(TOPK_GROUPS):
        _, first = _first_max(cur, gidx, N_EGROUPS)
        hit = gidx == first
        picked = jnp.where(hit, 1.0, picked)
        cur = jnp.where(hit, ninf, cur)
    x = jnp.concatenate(
        [jnp.where(picked[g:g + 1, :] > 0.5, ssel[g * gsz:(g + 1) * gsz, :], ninf) for g in range(N_EGROUPS)], axis=0)
    ei = lax.broadcasted_iota(I32, (N_EXPERTS, tm), 0)
    idxs, ws = [], []
    for _ in range(TOP_K):
        _, first = _first_max(x, ei, N_EXPERTS)
        hit = ei == first
        idxs.append(first)
        ws.append(jnp.sum(jnp.where(hit, s, 0.0), axis=0, keepdims=True))
        x = jnp.where(hit, ninf, x)
    w = jnp.concatenate(ws, axis=0)
    e_ref[...] = jnp.concatenate(idxs, axis=0)
    w_ref[...] = w / jnp.sum(w, axis=0, keepdims=True) * ROUTE_SCALE


def moe_router(h2, rwt, bias_b, tm):
    N, D = h2.shape
    return pl.pallas_call(
        _router_kernel,
        out_shape=[jax.ShapeDtypeStruct((TOP_K, N), I32), jax.ShapeDtypeStruct((TOP_K, N), F32)],
        grid=(N // tm,),
        in_specs=[pl.BlockSpec((tm, D), lambda i: (i, 0)),
                  pl.BlockSpec((N_EXPERTS, D), lambda i: (0, 0)),
                  pl.BlockSpec((N_EXPERTS, LANE), lambda i: (0, 0))],
        out_specs=[pl.BlockSpec((TOP_K, tm), lambda i: (0, i)),
                   pl.BlockSpec((TOP_K, tm), lambda i: (0, i))],
        compiler_params=_cp(("arbitrary",)),
        name="moe_router",
    )(h2, rwt, bias_b)


def _moe_count_kernel(e_ref, pstart_ref, plan_ref, blk_ref, cnt_scr, *, nt, nbp):
    i = pl.program_id(0)
    tp = e_ref.shape[1]

    @pl.when(i == 0)
    def _():
        cnt_scr[...] = jnp.zeros_like(cnt_scr)

    ei = lax.broadcasted_iota(I32, (N_EXPERTS, tp), 0)
    acc = jnp.zeros((N_EXPERTS, 1), F32)
    for k in range(TOP_K):
        acc = acc + jnp.sum(jnp.where(ei == e_ref[k:k + 1, :], 1.0, 0.0), axis=1, keepdims=True)
    cnt_scr[...] = cnt_scr[...] + acc

    @pl.when(i == nt - 1)
    def _():
        cnt = cnt_scr[...].astype(I32)
        padded = (cnt + (MOE_BLK - 1)) & (-MOE_BLK)
        r = lax.broadcasted_iota(I32, (N_EXPERTS, N_EXPERTS), 0)
        c = lax.broadcasted_iota(I32, (N_EXPERTS, N_EXPERTS), 1)
        tri = jnp.where(c <= r, 1.0, 0.0).astype(BF16)
        padded_b = jnp.broadcast_to(padded.astype(F32), (N_EXPERTS, LANE))
        p_end = _dot_exact_lhs(tri, padded_b)
        pstart = p_end - padded_b
        pstart_ref[...] = pstart.astype(I32)
        diag = r == c
        ps_row = jnp.sum(jnp.where(diag, pstart[:, 0:1], 0.0), axis=0, keepdims=True)
        cnt_row = jnp.sum(jnp.where(diag, cnt_scr[...], 0.0), axis=0, keepdims=True)
        plan_ref[...] = jnp.concatenate([ps_row, cnt_row, jnp.zeros((SUB - 2, N_EXPERTS), F32)], axis=0).astype(I32)
        lim =(lax.broadcasted_iota(I32, (N_EXPERTS, nbp), 1) * MOE_BLK).astype(F32)
        be = jnp.sum(jnp.where(p_end[:, 0:1] <= lim, 1.0, 0.0), axis=0, keepdims=True)
        be = jnp.minimum(be, N_EXPERTS - 1.0)
        nused = jnp.max(p_end[:, 0:1], axis=0, keepdims=True) * (1.0 / MOE_BLK)
        blk_ref[...] = jnp.concatenate([jnp.broadcast_to(be, (SUB // 2, nbp)),
                                        jnp.broadcast_to(nused, (SUB // 2, nbp))], axis=0).astype(I32)


def moe_counts(eidx, tp, nbp):
    N = eidx.shape[1]
    nt = N // tp
    kern = functools.partial(_moe_count_kernel, nt=nt, nbp=nbp)
    return pl.pallas_call(
        kern,
        out_shape=[jax.ShapeDtypeStruct((N_EXPERTS, LANE), I32),
                   jax.ShapeDtypeStruct((SUB, N_EXPERTS), I32),
                   jax.ShapeDtypeStruct((SUB, nbp), I32)],
        grid=(nt,),
        in_specs=[pl.BlockSpec((TOP_K, tp), lambda i: (0, i))],
        out_specs=[pl.BlockSpec((N_EXPERTS, LANE), lambda i: (0, 0)),
                   pl.BlockSpec((SUB, N_EXPERTS), lambda i: (0, 0)),
                   pl.BlockSpec((SUB, nbp), lambda i: (0, 0))],
        scratch_shapes=[pltpu.VMEM((N_EXPERTS, 1), F32)],
        compiler_params=_cp(("arbitrary",)),
        name="moe_counts",
    )(eidx)


def _moe_dest_kernel(e_ref, pstart_ref, dest_ref, base_scr):
    i = pl.program_id(0)
    tp = e_ref.shape[1]

    @pl.when(i == 0)
    def _():
        base_scr[...] = pstart_ref[:, 0:1].astype(F32)

    ei = lax.broadcasted_iota(I32, (N_EXPERTS, tp), 0)
    r = lax.broadcasted_iota(I32, (tp, tp), 0)
    c = lax.broadcasted_iota(I32, (tp, tp), 1)
    tri = jnp.where(r <= c, 1.0, 0.0).astype(BF16)
    base = base_scr[...]
    rows = []
    for k in range(TOP_K):
        hit = ei == e_ref[k:k + 1, :]
        oh = jnp.where(hit, 1.0, 0.0)
        cum = _bdot(oh, tri)
        rows.append(jnp.sum(jnp.where(hit, cum - 1.0 + base, 0.0), axis=0, keepdims=True))
        base = base + cum[:, tp - 1:tp]
    base_scr[...] = base
    dest_ref[...] = jnp.concatenate(rows, axis=0).astype(I32)


def moe_dest(eidx, pstart, tp):
    N = eidx.shape[1]
    return pl.pallas_call(
        _moe_dest_kernel,
        out_shape=jax.ShapeDtypeStruct((TOP_K, N), I32),
        grid=(N // tp,),
        in_specs=[pl.BlockSpec((TOP_K, tp), lambda i: (0, i)),
                  pl.BlockSpec((N_EXPERTS, LANE), lambda i: (0, 0))],
        out_specs=pl.BlockSpec((TOP_K, tp), lambda i: (0, i)),
        scratch_shapes=[pltpu.VMEM((N_EXPERTS, 1), F32)],
        compiler_params=_cp(("arbitrary",)),
        name="moe_dest",
    )(eidx, pstart)


def _tile_at(ref, token):
    return ref.at[pl.ds(pl.multiple_of(token * ROWT, ROWT), ROWT), :]


def _moe_scatter_kernel(dest_ref, h_ref, xs_hbm, sem):
    tp = dest_ref.shape[1]

    def issue(j, carry):
        for k in range(TOP_K):
            pltpu.make_async_copy(_tile_at(h_ref, j), _tile_at(xs_hbm, dest_ref[k, j]), sem).start(priority=k % 2)
        return carry

    lax.fori_loop(0, tp, issue, 0)

    def drain(j, carry):
        for k in range(TOP_K):
            pltpu.make_async_copy(h_ref.at[pl.ds(0, ROWT), :], xs_hbm.at[pl.ds(0, ROWT), :], sem).wait()
        return carry

    lax.fori_loop(0, tp, drain, 0)


def moe_scatter(dest, h2t, n_slots, tp):
    N = dest.shape[1]
    return pl.pallas_call(
        _moe_scatter_kernel,
        out_shape=jax.ShapeDtypeStruct((n_slots * ROWT, LANE), U32),
        grid=(N // tp,),
        in_specs=[pl.BlockSpec((TOP_K, tp), lambda i: (0, i), memory_space=pltpu.SMEM),
                  pl.BlockSpec((tp * ROWT, LANE), lambda i: (i, 0))],
        out_specs=pl.BlockSpec(memory_space=pl.ANY),
        scratch_shapes=[pltpu.SemaphoreType.DMA(())],
        compiler_params=_cp(("arbitrary",)),
        name="moe_scatter",
    )(dest, h2t)


PAD_BITS = tuple(1 << b for b in reversed(range(MOE_BLK.bit_length() - 1)))


def _moe_padfill_kernel(ps_ref, xs_in, xs_hbm, zero_scr, sem):
    del xs_in
    zero_scr[...] = jnp.zeros_like(zero_scr)

    def pad_copies(e, wait):
        cnt = ps_ref[1, e]
        npad = ((cnt + (MOE_BLK - 1)) & (-MOE_BLK)) - cnt
        off = ps_ref[0, e] + cnt
        for bit in PAD_BITS:
            @pl.when((npad & bit) != 0)
            def _():
                cp = pltpu.make_async_copy(zero_scr.at[pl.ds(0, bit * ROWT), :],
                                           xs_hbm.at[pl.ds(pl.multiple_of(off * ROWT, ROWT), bit * ROWT), :], sem)
                if wait:
                    cp.wait()
                else:
                    cp.start()
            off = off + (npad & bit)

    def issue(e, carry):
        pad_copies(e, False)
        return carry

    def drain(e, carry):
        pad_copies(e, True)
        return carry

    lax.fori_loop(0, N_EXPERTS, issue, 0)
    lax.fori_loop(0, N_EXPERTS, drain, 0)


def moe_padfill(pstart, xs):
    return pl.pallas_call(
        _moe_padfill_kernel,
        out_shape=jax.ShapeDtypeStruct(xs.shape, xs.dtype),
        grid=(1,),
        in_specs=[pl.BlockSpec(memory_space=pltpu.SMEM),
                  pl.BlockSpec(memory_space=pl.ANY)],
        out_specs=pl.BlockSpec(memory_space=pl.ANY),
        scratch_shapes=[pltpu.VMEM((PAD_BITS[0] * ROWT, LANE), U32), pltpu.SemaphoreType.DMA(())],
        input_output_aliases={1: 0},
        compiler_params=_cp(("arbitrary",)),
        name="moe_padfill",
    )(pstart, xs)


EXP_NX = 4
EXP_NY = 4


def _experts_kernel(plan_ref, w13_ref, w2_ref, xs_hbm, y_hbm, w13_bf, w2_bf, x_buf, y_buf, sem_in, sem_out):
    e = pl.program_id(0)
    ne = pl.num_programs(0)
    ff = w2_bf.shape[0]
    rows = MOE_BLK * ROWT
    shift = MOE_BLK.bit_length() - 1
    blocks_of = lambda c: lax.shift_right_logical(c + (MOE_BLK - 1), shift)
    nblk = blocks_of(plan_ref[1, e])
    first = lax.shift_right_logical(plan_ref[0, e], shift)
    nused = lax.shift_right_logical(plan_ref[0, ne - 1], shift) + blocks_of(plan_ref[1, ne - 1])

    def window(ref, g):
        return ref.at[pl.ds(pl.multiple_of(g * rows, rows), rows), :]

    def x_copy(g):
        return pltpu.make_async_copy(window(xs_hbm, g), x_buf.at[g & (EXP_NX - 1)], sem_in.at[g & (EXP_NX - 1)])

    def y_copy(g):
        return pltpu.make_async_copy(y_buf.at[g & (EXP_NY - 1)], window(y_hbm, g), sem_out.at[g & (EXP_NY - 1)])

    @pl.when(e == 0)
    def _():
        for g in range(EXP_NX - 1):
            @pl.when(g < nused)
            def _():
                x_copy(g).start()

    @pl.when(nblk > 0)
    def _():
        w13_bf[...] = w13_ref[0, 0].astype(BF16)
        w2_bf[...] = w2_ref[0, 0].astype(BF16)

        def body(j, carry):
            g = first + j
            x_copy(g).wait()

            @pl.when(g + (EXP_NX - 1) < nused)
            def _():
                x_copy(g + (EXP_NX - 1)).start()

            @pl.when(g >= EXP_NY)
            def _():
                y_copy(g - EXP_NY).wait()

            x = _tile_rows_load(x_buf.at[g & (EXP_NX - 1)], MOE_BLK).astype(BF16)
            h = jnp.dot(x, w13_bf[...], preferred_element_type=F32)
            act = (_silu(h[:, 0:ff]) * h[:, ff:2 * ff]).astype(BF16)
            _tile_rows_store(y_buf.at[g & (EXP_NY - 1)], jnp.dot(act, w2_bf[...], preferred_element_type=F32))
            y_copy(g).start()
            return carry

        lax.fori_loop(0, nblk, body, 0)

    @pl.when(e == ne - 1)
    def _():
        for back in range(EXP_NY, 0, -1):
            @pl.when(nused >= back)
            def _():
                y_copy(nused - back).wait()


def moe_experts(plan, xs, w13, w2, layer):
    _, E, D, F2 = w13.shape
    rows = MOE_BLK * ROWT
    grid_spec = pltpu.PrefetchScalarGridSpec(
        num_scalar_prefetch=1,
        grid=(E,),
        in_specs=[pl.BlockSpec((1, 1, D, F2), lambda e, p: (layer, e, 0, 0)),
                  pl.BlockSpec((1, 1, F2 // 2, D), lambda e, p: (layer, e, 0, 0)),
                  pl.BlockSpec(memory_space=pl.ANY)],
        out_specs=pl.BlockSpec(memory_space=pl.ANY),
        scratch_shapes=[pltpu.VMEM((D, F2), BF16), pltpu.VMEM((F2 // 2, D), BF16),
                        pltpu.VMEM((EXP_NX, rows, LANE), U32), pltpu.VMEM((EXP_NY, rows, LANE), U32),
                        pltpu.SemaphoreType.DMA((EXP_NX,)), pltpu.SemaphoreType.DMA((EXP_NY,))])
    return pl.pallas_call(
        _experts_kernel,
        out_shape=jax.ShapeDtypeStruct(xs.shape, U32),
        grid_spec=grid_spec,
        compiler_params=_cp(("arbitrary",)),
        name="moe_experts",
    )(plan, w13, w2, xs)


def _moe_combine_kernel(dest_ref, wt_ref, h_ref, x_ref, mod_ref, s13_ref, s2_ref, l2g_ref, l2b_ref, y_hbm,
                        o_ref, g_scr, sem, *, alpha, nt):
    s = pl.program_id(0)
    tm = h_ref.shape[0]
    ff = s2_ref.shape[0]
    rows = tm * ROWT
    slot_g = s % 2
    slot_c = 1 - slot_g

    def slot_copy(slot):
        return pltpu.make_async_copy(y_hbm.at[pl.ds(0, TOP_K * rows), :], g_scr.at[slot], sem.at[slot])

    @pl.when(s == 0)
    def _():
        g_scr[1] = jnp.zeros((TOP_K * rows, LANE), U32)

    @pl.when(s > 0)
    def _():
        slot_copy(slot_c).wait()

    for j in range(tm):
        for k in range(TOP_K):
            pltpu.make_async_copy(_tile_at(y_hbm, dest_ref[k, j]),
                                  g_scr.at[slot_g, pl.ds(k * rows + j * ROWT, ROWT), :],
                                  sem.at[slot_g]).start(priority=k % 2)
    hs = _bdot(h_ref[...], s13_ref[...])
    f = _bdot(_silu(hs[:, 0:ff]) * hs[:, ff:2 * ff], s2_ref[...])
    for k in range(TOP_K):
        f = f + wt_ref[:, k:k + 1] * _tile_rows_load(g_scr.at[slot_c], tm, base=k * rows)
    o_ref[...] = (_layer_norm(alpha * x_ref[...] + mod_ref[0, 5:6, :] * f) * l2g_ref[...] + l2b_ref[...])

    @pl.when(s == nt)
    def _():
        slot_copy(slot_g).wait()


def moe_combine(dest, wt_t, h2, x1, mod3, wts, y, row_of_tile, tm, alpha):
    N, D = h2.shape
    nt = N // tm
    full = lambda a: pl.BlockSpec(a.shape, lambda s: (0,) * a.ndim)
    kern = functools.partial(_moe_combine_kernel, alpha=alpha, nt=nt)
    nxt = lambda s: jnp.minimum(s, nt - 1)
    cur = lambda s: jnp.maximum(s - 1, 0)
    return pl.pallas_call(
        kern,
        out_shape=jax.ShapeDtypeStruct((N, D), F32),
        grid=(nt + 1,),
        in_specs=[pl.BlockSpec((TOP_K, tm), lambda s: (0, nxt(s)), memory_space=pltpu.SMEM),
                  pl.BlockSpec((tm, TOP_K), lambda s: (cur(s), 0)),
                  pl.BlockSpec((tm, D), lambda s: (cur(s), 0)),
                  pl.BlockSpec((tm, D), lambda s: (cur(s), 0)),
                  pl.BlockSpec((1, 6, D), lambda s: (row_of_tile(cur(s)), 0, 0))]
                 + [full(a) for a in wts] + [pl.BlockSpec(memory_space=pl.ANY)],
        out_specs=pl.BlockSpec((tm, D), lambda s: (cur(s), 0)),
        scratch_shapes=[pltpu.VMEM((2, TOP_K * tm * ROWT, LANE), U32), pltpu.SemaphoreType.DMA((2,))],
        compiler_params=_cp(("arbitrary",)),
        name="moe_combine",
    )(dest, wt_t, h2, x1, mod3, *wts, y)


def moe_ffn(h2, h2t, x1, mod3, mp, row_of_tile, tm, alpha):
    N = h2.shape[0]
    nb_total = (N * TOP_K + MOE_BLK - 1) // MOE_BLK + N_EXPERTS
    nbp = ((nb_total + LANE - 1) // LANE) * LANE
    tp = _tile(N, 256)
    eidx, wts = moe_router(h2, mp['rwt'], mp['bias'], tp)
    pstart, plan, blk = moe_counts(eidx, tp, nbp)
    dest = moe_dest(eidx, pstart, tp)
    xs = moe_scatter(dest, h2t, nb_total * MOE_BLK, tp)
    xs = moe_padfill(plan, xs)
    y = moe_experts(plan, xs, mp['w13'], mp['w2'], mp['layer'])
    return moe_combine(dest, wts.T, h2, x1, mod3, mp['comb'], y, row_of_tile, tm, alpha)


def token_mixers(x2, B, T, W, mod3, row_of_tile, lp, states, tm):
    pa, pbg, gg, u = inproj(x2, mod3, lp['w_in'], row_of_tile, tm)
    pac = short_conv(pa.reshape(B, T, PA_W), lp['taps'], W).reshape(B * T, PA_W)
    ss, sd, gb = rwkv_prep(pac, lp['prep'], tm)
    s_rwkv, s_gla, s_s5 = states
    y2, f_rwkv = rwkv_scan(ss, sd, s_rwkv, B, T)
    o2, f_gla = gla_scan(pbg, lp['gla_aup'], lp['gla_ab'], s_gla, B, T)
    yd, f_s5 = s5_scan(u.reshape(B, T, S5_W), lp['s5_a'], lp['s5_bbd'], lp['s5_cbd'], s_s5)
    return (y2, gb, o2, gg, yd.reshape(2, B * T, S5_W), u), (f_rwkv, f_gla, f_s5)


def zero_states(B):
    return ((jnp.zeros((B, RW // LANE, LANE, LANE), F32),) * 2,
            (jnp.zeros((B, RW, GLA_KP), F32),) * 2,
            jnp.zeros((2, B, S5_LANES), F32))


def _inproj_columns():
    r_cols = 3 * RW + 2 * DECAY_RANK + 2 * ICLR_RANK + GATE_RANK
    kd = GLA_HEADS * GLA_K
    gq, gk, gv = r_cols, r_cols + kd, r_cols + 2 * kd
    gg = gv + RW
    gad = gg + RW
    pc = gad + 2 * GLA_RANK
    z = lambda n: [-1] * n
    cols = list(range(0, r_cols)) + z(PA_W - r_cols)
    cols += list(range(gq, gq + kd)) + z(256 - kd)
    cols += list(range(gk, gk + kd)) + z(256 - kd)
    cols += list(range(gv, gv + RW))
    cols += list(range(gad, gad + 2 * GLA_RANK)) + z(LANE - 2 * GLA_RANK)
    cols += list(range(gg, gg + RW))
    cols += list(range(pc, pc + S5_W))
    assert len(cols) == NP_W
    return np.asarray(cols, np.int32)


def _head_block_ones():
    h = np.arange(RW) // HEAD
    return jnp.asarray(h[:, None] == h[None, :], BF16)


def _layer_params(l, p):
    D = p['w_in'].shape[1]
    cols = _inproj_columns()
    w_in = jnp.concatenate([p['w_in'][l], jnp.zeros((D, 1), F32)], axis=1)
    w_in = jnp.take(w_in, jnp.asarray(np.where(cols < 0, w_in.shape[1] - 1, cols)), axis=1).astype(BF16)
    taps = p['rwkv_conv'][l].reshape(9, -1)
    taps = jnp.concatenate([taps, jnp.zeros((9, PA_W - taps.shape[1]), F32)], axis=1)
    wup = jnp.zeros((LANE, 2 * RW), F32)
    aup = jnp.zeros((LANE, 2 * RW), F32)
    for d in range(2):
        wup = wup.at[d * DECAY_RANK:(d + 1) * DECAY_RANK, d * RW:(d + 1) * RW].set(p['rwkv_w_up'][l, d])
        o = 2 * DECAY_RANK + d * ICLR_RANK
        aup = aup.at[o:o + ICLR_RANK, d * RW:(d + 1) * RW].set(p['rwkv_a_up'][l, d])
    gup = jnp.zeros((LANE, RW), F32).at[0:GATE_RANK].set(p['rwkv_g_up'][l])
    row = lambda a: a.reshape(1, -1)
    prep = (wup, aup, gup, row(p['rwkv_w0'][l]), row(p['rwkv_a0'][l]), row(p['rwkv_k_k'][l]),
            row(p['rwkv_k_a'][l]), row(p['rwkv_r_k'][l]), _head_block_ones())
    kd = GLA_HEADS * GLA_K
    gla_aup = jnp.zeros((2, LANE, GLA_KP), F32)
    for d in range(2):
        gla_aup = gla_aup.at[d, d * GLA_RANK:(d + 1) * GLA_RANK, 0:kd].set(p['gla_a_up'][l, d])
    gla_ab = jnp.zeros((2, 1, GLA_KP), F32).at[:, 0, 0:kd].set(p['gla_a_bias'][l])
    lam_re, lam_im = p['s5_lam_re'][l], p['s5_lam_im'][l]
    dt = jnp.exp(p['s5_log_dt'][l])[:, :, None]
    zr, zi = lam_re[:, None, :] * dt, lam_im[:, None, :] * dt
    mag = jnp.exp(zr)
    ab_r, ab_i = mag * jnp.cos(zi), mag * jnp.sin(zi)
    den = (lam_re * lam_re + lam_im * lam_im)[:, None, :]
    f_r = ((ab_r - 1) * lam_re[:, None, :] + ab_i * lam_im[:, None, :]) / den
    f_i = (ab_i * lam_re[:, None, :] - (ab_r - 1) * lam_im[:, None, :]) / den
    b_re, b_im = p['s5_b_re'][l], p['s5_b_im'][l]
    bb_r = f_r[..., None] * b_re - f_i[..., None] * b_im
    bb_i = f_r[..., None] * b_im + f_i[..., None] * b_re
    eye_g = jnp.eye(S5_GROUPS, dtype=F32)
    half = S5_LANES // 2

    def in_blockdiag(bb):
        return jnp.einsum('dgpc,gh->dgchp', bb, eye_g).reshape(2, S5_W, half)

    def out_blockdiag(cc):
        return jnp.einsum('dgcp,gh->dgphc', cc, eye_g).reshape(2, half, S5_W)

    s5_bbd = jnp.concatenate([in_blockdiag(bb_r), in_blockdiag(bb_i)], axis=2).astype(BF16)
    s5_cbd = jnp.concatenate([out_blockdiag(p['s5_c_re'][l]), -out_blockdiag(p['s5_c_im'][l])], axis=1).astype(BF16)
    s5_a = jnp.concatenate([ab_r.reshape(2, 1, half), ab_i.reshape(2, 1, half)], axis=2)
    mix = (_head_block_ones(), row(p['rwkv_ln_g'][l]), row(p['rwkv_ln_b'][l]),
           row(jnp.tile(p['gla_norm_g'][l], GLA_HEADS)), row(p['s5_d'][l]),
           p['s5_glu_w'][l].astype(BF16), row(p['s5_glu_b'][l]), p['w_out'][l].astype(BF16),
           row(p['ln1_g'][l]), row(p['ln1_b'][l]))
    moe = dict(rwt=p['router_w'][l].T,
               bias=jnp.broadcast_to(p['router_bias'][l][:, None], (N_EXPERTS, LANE)),
               w13=p['exp_w13'], w2=p['exp_w2'], layer=l,
               comb=(p['sh_w13'][l].astype(BF16), p['sh_w2'][l].astype(BF16), row(p['ln2_g'][l]), row(p['ln2_b'][l])))
    return dict(w_in=w_in, taps=taps, prep=prep, gla_aup=gla_aup, gla_ab=gla_ab,
                s5_a=s5_a, s5_bbd=s5_bbd, s5_cbd=s5_cbd, mix=mix, moe=moe)


_ARG_NAMES = ('x', 'c', 'ctx', 'c_ctx', 'w_mod', 'b_mod', 'w_in', 'rwkv_conv', 'rwkv_w0', 'rwkv_w_up', 'rwkv_a0',
              'rwkv_a_up', 'rwkv_g_up', 'rwkv_k_k', 'rwkv_k_a', 'rwkv_r_k', 'rwkv_ln_g', 'rwkv_ln_b', 'gla_a_up',
              'gla_a_bias', 'gla_norm_g', 's5_lam_re', 's5_lam_im', 's5_log_dt', 's5_b_re', 's5_b_im', 's5_c_re',
              's5_c_im', 's5_d', 's5_glu_w', 's5_glu_b', 'w_out', 'ln1_g', 'ln1_b', 'router_w', 'router_bias',
              'exp_w13', 'exp_w2', 'sh_w13', 'sh_w2', 'ln2_g', 'ln2_b')


def _tile(n, pref):
    t = pref
    while n % t:
        t //= 2
    return t


def kernel(x, c, ctx, c_ctx, w_mod, b_mod, w_in, rwkv_conv, rwkv_w0, rwkv_w_up, rwkv_a0, rwkv_a_up, rwkv_g_up,
           rwkv_k_k, rwkv_k_a, rwkv_r_k, rwkv_ln_g, rwkv_ln_b, gla_a_up, gla_a_bias, gla_norm_g, s5_lam_re,
           s5_lam_im, s5_log_dt, s5_b_re, s5_b_im, s5_c_re, s5_c_im, s5_d, s5_glu_w, s5_glu_b, w_out, ln1_g,
           ln1_b, router_w, router_bias, exp_w13, exp_w2, sh_w13, sh_w2, ln2_g, ln2_b):
    p = dict(zip(_ARG_NAMES, (x, c, ctx, c_ctx, w_mod, b_mod, w_in, rwkv_conv, rwkv_w0, rwkv_w_up, rwkv_a0,
                              rwkv_a_up, rwkv_g_up, rwkv_k_k, rwkv_k_a, rwkv_r_k, rwkv_ln_g, rwkv_ln_b, gla_a_up,
                              gla_a_bias, gla_norm_g, s5_lam_re, s5_lam_im, s5_log_dt, s5_b_re, s5_b_im, s5_c_re,
                              s5_c_im, s5_d, s5_glu_w, s5_glu_b, w_out, ln1_g, ln1_b, router_w, router_bias,
                              exp_w13, exp_w2, sh_w13, sh_w2, ln2_g, ln2_b)))
    B, T, D = x.shape
    TC = ctx.shape[1]
    L = w_mod.shape[0]
    alpha = (2 * L) ** 0.25
    n_lat, n_ctx = B * T, B * TC
    R = ((B + 1 + SUB - 1) // SUB) * SUB
    cc = jnp.zeros((R, D), F32).at[0:B].set(c).at[B].set(c_ctx)
    mod = mod_table(cc, w_mod, b_mod)
    tm = _tile(T, 512)
    tmc = _tile(n_ctx, 512)
    tmm = min(_tile(T, 128), _tile(n_ctx, 128))
    lat_row = lambda i: (i * tm) // T
    ctx_row = lambda i: B
    x2 = x.reshape(n_lat, D)
    c2 = ctx.reshape(n_ctx, D)
    for l in range(L):
        last = l == L - 1
        lp = _layer_params(l, p)
        mod3 = mod[l].reshape(R, 6, D)
        outs_c, st_c = token_mixers(c2, B, TC, TC, mod3, ctx_row, lp, zero_states(B), tmc)
        outs, _ = token_mixers(x2, B, T, GRID_W, mod3, lat_row, lp, st_c, tm)
        x1, h2, h2t = mix_out(*outs, x2, mod3, lp['mix'], lat_row, tm, alpha)
        if last:
            x2 = moe_ffn(h2, h2t, x1, mod3, lp['moe'], lambda i: (i * tmm) // T, tmm, alpha)
        else:
            c1, hc2, hc2t = mix_out(*outs_c, c2, mod3, lp['mix'], ctx_row, tmc, alpha)
            row_all = lambda i: jnp.where(i * tmm < n_ctx, B, (i * tmm - n_ctx) // T)
            out = moe_ffn(jnp.concatenate([hc2, h2]), jnp.concatenate([hc2t, h2t]), jnp.concatenate([c1, x1]),
                          mod3, lp['moe'], row_all, tmm, alpha)
            c2, x2 = out[:n_ctx], out[n_ctx:]
    return x2.reshape(B, T, D)
```

```python
import functools
import math

import numpy as np
import jax
import jax.numpy as jnp
from jax import lax
from jax.experimental import pallas as pl
from jax.experimental.pallas import tpu as pltpu

F32 = jnp.float32
BF16 = jnp.bfloat16
I32 = jnp.int32

GRID_W = 64
RWKV_HEADS = 6
HEAD = 64
RW = RWKV_HEADS * HEAD
DECAY_RANK = 32
ICLR_RANK = 32
GATE_RANK = 64
RWKV_GN_EPS = 64e-5
GLA_HEADS = 6
GLA_K = 32
GLA_RANK = 16
GLA_TEMP = 16.0
S5_GROUPS = 16
S5_GROUP = 16
S5_STATE = 64
S5_W = S5_GROUPS * S5_GROUP
S5_LANES = 2 * S5_GROUPS * S5_STATE
N_EXPERTS = 256
TOP_K = 8
N_EGROUPS = 8
TOPK_GROUPS = 4
ROUTE_SCALE = 2.5
LN_EPS = 1e-6
CHUNK = 64
MOE_BLK = 256
LANE = 128
SUB = 8
VMEM_LIMIT = 56 * 1024 * 1024

PA_W = 3 * RW + 2 * LANE
PBG_W = 256 + 256 + RW + LANE
NP_W = PA_W + PBG_W + RW + S5_W


def _cp(sem):
    return pltpu.CompilerParams(dimension_semantics=sem, vmem_limit_bytes=VMEM_LIMIT)


def _sigmoid(x):
    return 1.0 / (1.0 + jnp.exp(-x))


def _silu(x):
    return x * _sigmoid(x)


def _bdot(a, b, dims=None):
    a = a.astype(BF16)
    b = b.astype(BF16)
    if dims is None:
        return jnp.dot(a, b, preferred_element_type=F32)
    return lax.dot_general(a, b, dims, preferred_element_type=F32)


def _split2(x):
    hi = x.astype(BF16)
    lo = (x - hi.astype(F32)).astype(BF16)
    return hi, lo


def _split3(x):
    hi = x.astype(BF16)
    r = x - hi.astype(F32)
    mid = r.astype(BF16)
    lo = (r - mid.astype(F32)).astype(BF16)
    return hi, mid, lo


NT_DIMS = (((1,), (1,)), ((), ()))
TN_DIMS = (((0,), (0,)), ((), ()))


def _dot33(a, b, dims=None):
    ah, al = _split2(a)
    bh, bl = _split2(b)
    return _bdot(ah, bh, dims) + (_bdot(ah, bl, dims) + _bdot(al, bh, dims))


def _dot_exact_lhs(m_exact, x, dims=None):
    h, m, l = _split3(x)
    return _bdot(m_exact, h, dims) + (_bdot(m_exact, m, dims) + _bdot(m_exact, l, dims))


def _dot_exact_rhs(x, m_exact, dims=None):
    h, m, l = _split3(x)
    return _bdot(h, m_exact, dims) + (_bdot(m, m_exact, dims) + _bdot(l, m_exact, dims))


def _dot_split_rhs(x, m_exact, dims=None):
    h, l = _split2(x)
    return _bdot(h, m_exact, dims) + _bdot(l, m_exact, dims)


def _layer_norm(x):
    mu = jnp.mean(x, axis=-1, keepdims=True)
    xc = x - mu
    var = jnp.mean(xc * xc, axis=-1, keepdims=True)
    return xc * lax.rsqrt(var + LN_EPS)


def _mod_kernel(c_ref, w_ref, b_ref, o_ref):
    s = _silu(c_ref[...])
    o_ref[0] = _dot33(s, w_ref[0]) + b_ref[0]


def mod_table(cc, w_mod, b_mod):
    L, D, D6 = w_mod.shape
    R = cc.shape[0]
    tn = 1536
    return pl.pallas_call(
        _mod_kernel,
        out_shape=jax.ShapeDtypeStruct((L, R, D6), F32),
        grid=(L, D6 // tn),
        in_specs=[pl.BlockSpec((R, D), lambda l, j: (0, 0)),
                  pl.BlockSpec((1, D, tn), lambda l, j: (l, 0, j)),
                  pl.BlockSpec((1, 1, tn), lambda l, j: (l, 0, j))],
        out_specs=pl.BlockSpec((1, R, tn), lambda l, j: (l, 0, j)),
        compiler_params=_cp(("arbitrary", "arbitrary")),
        name="mod_table",
    )(cc, w_mod, b_mod.reshape(L, 1, D6))


def _inproj_kernel(x_ref, mod_ref, w_ref, pa_ref, pbg_ref, gg_ref, u_ref):
    x = x_ref[...]
    h = _layer_norm(x) * (1.0 + mod_ref[0, 1:2, :]) + mod_ref[0, 0:1, :]
    hb = h.astype(BF16)
    o = 0
    for ref in (pa_ref, pbg_ref, gg_ref, u_ref):
        w = ref.shape[-1]
        ref[...] = jnp.dot(hb, w_ref[:, o:o + w], preferred_element_type=F32)
        o += w


def inproj(stream, mod3, w_bf, row_of_tile, tm):
    x2, row0, N = stream
    D = x2.shape[1]
    t0 = row0 // tm
    assert row0 % tm == 0 and N % tm == 0
    return pl.pallas_call(
        _inproj_kernel,
        out_shape=[jax.ShapeDtypeStruct((N, PA_W), F32),
                   jax.ShapeDtypeStruct((N, PBG_W), F32),
                   jax.ShapeDtypeStruct((N, RW), F32),
                   jax.ShapeDtypeStruct((N, S5_W), F32)],
        grid=(N // tm,),
        in_specs=[pl.BlockSpec((tm, D), lambda i: (i + t0, 0)),
                  pl.BlockSpec((1, 6, D), lambda i: (row_of_tile(i), 0, 0)),
                  pl.BlockSpec((D, NP_W), lambda i: (0, 0))],
        out_specs=[pl.BlockSpec((tm, PA_W), lambda i: (i, 0)),
                   pl.BlockSpec((tm, PBG_W), lambda i: (i, 0)),
                   pl.BlockSpec((tm, RW), lambda i: (i, 0)),
                   pl.BlockSpec((tm, S5_W), lambda i: (i, 0))],
        compiler_params=_cp(("arbitrary",)),
        name="inproj",
    )(x2, mod3, w_bf)


CONV_PAD = 72


def _conv_kernel(x_ref, taps_ref, o_ref, buf_ref, *, T, W, vertical):
    pad = CONV_PAD
    zeros = jnp.zeros((pad, LANE), F32)
    buf_ref[0:pad, :] = zeros
    buf_ref[pad + T:pad + T + pad, :] = zeros
    buf_ref[pad:pad + T, :] = x_ref[0]
    ch = min(T, 256)
    col = lax.broadcasted_iota(I32, (ch, LANE), 0) & (W - 1)
    left_ok = col >= 1
    right_ok = col <= W - 2
    for c in range(T // ch):
        base = pad + c * ch
        acc = jnp.zeros((ch, LANE), F32)
        for dr in ((0, 1, 2) if vertical else (1,)):
            for dc in range(3):
                off = (dr - 1) * W + (dc - 1)
                v = buf_ref[base + off:base + off + ch, :]
                if dc == 0:
                    v = jnp.where(left_ok, v, 0.0)
                elif dc == 2:
                    v = jnp.where(right_ok, v, 0.0)
                acc = acc + v * taps_ref[3 * dr + dc:3 * dr + dc + 1, :]
        o_ref[0, c * ch:(c + 1) * ch, :] = acc


def short_conv(pa3, taps9, W):
    B, T, C = pa3.shape
    vertical = T > W
    assert W & (W - 1) == 0 and (not vertical or W + 1 <= CONV_PAD)
    kern = functools.partial(_conv_kernel, T=T, W=W, vertical=vertical)
    return pl.pallas_call(
        kern,
        out_shape=jax.ShapeDtypeStruct((B, T, C), F32),
        grid=(B, C // LANE),
        in_specs=[pl.BlockSpec((1, T, LANE), lambda b, j: (b, 0, j)),
                  pl.BlockSpec((9, LANE), lambda b, j: (0, j))],
        out_specs=pl.BlockSpec((1, T, LANE), lambda b, j: (b, 0, j)),
        scratch_shapes=[pltpu.VMEM((T + 2 * CONV_PAD, LANE), F32)],
        compiler_params=_cp(("arbitrary", "arbitrary")),
        name="short_conv",
    )(pa3, taps9)


def _rwkv_prep_kernel(pa_ref, wup_ref, aup_ref, gup_ref, w0_ref, a0_ref, kk_ref, ka_ref, rk_ref, hb_ref,
                      ss_ref, sd_ref, gb_ref):
    r = pa_ref[:, 0:RW]
    k = pa_ref[:, RW:2 * RW]
    v = pa_ref[:, 2 * RW:3 * RW]
    wa = pa_ref[:, 3 * RW:3 * RW + LANE]
    gd = pa_ref[:, 3 * RW + LANE:3 * RW + 2 * LANE]
    z = w0_ref[...] + _dot33(jnp.tanh(wa), wup_ref[...])
    lw = -_sigmoid(z) * math.exp(-0.5)
    a = _sigmoid(a0_ref[...] + _bdot(wa, aup_ref[...]))
    g = _bdot(_sigmoid(gd), gup_ref[...])
    hb = hb_ref[...]
    kk = k * kk_ref[...]
    kk = kk * lax.rsqrt(_dot_split_rhs(kk * kk, hb) + 1e-12)
    ka = ka_ref[...]
    ss_ref[:, 0:RW] = r
    ss_ref[:, RW:2 * RW] = v
    ss_ref[:, 2 * RW:3 * RW] = kk
    rk2 = jnp.zeros_like(r)
    for d in range(2):
        ad = a[:, d * RW:(d + 1) * RW]
        k2 = k * (1.0 + (ad - 1.0) * ka)
        sd_ref[d, :, 0:RW] = lw[:, d * RW:(d + 1) * RW]
        sd_ref[d, :, RW:2 * RW] = k2
        sd_ref[d, :, 2 * RW:3 * RW] = kk * ad
        rk2 = rk2 + r * k2
    bonus = _dot_split_rhs(rk2 * rk_ref[...], hb) * v
    gb_ref[:, 0:RW] = g
    gb_ref[:, RW:2 * RW] = bonus


def rwkv_prep(pa2, wts, tm):
    N = pa2.shape[0]
    full = lambda a: pl.BlockSpec(a.shape, lambda i: (0,) * a.ndim)
    return pl.pallas_call(
        _rwkv_prep_kernel,
        out_shape=[jax.ShapeDtypeStruct((N, 3 * RW), F32),
                   jax.ShapeDtypeStruct((2, N, 3 * RW), F32),
                   jax.ShapeDtypeStruct((N, 2 * RW), F32)],
        grid=(N // tm,),
        in_specs=[pl.BlockSpec((tm, PA_W), lambda i: (i, 0))] + [full(a) for a in wts],
        out_specs=[pl.BlockSpec((tm, 3 * RW), lambda i: (i, 0)),
                   pl.BlockSpec((2, tm, 3 * RW), lambda i: (0, i, 0)),
                   pl.BlockSpec((tm, 2 * RW), lambda i: (i, 0))],
        compiler_params=_cp(("arbitrary",)),
        name="rwkv_prep",
    )(pa2, *wts)


RWKV_GROUP = 8


def _rwkv_scan_kernel(ss_ref, sd_ref, s0_ref, y_ref, sfin_ref, s_scr, *, nsteps, group, reverse):
    n = pl.program_id(1)
    C = CHUNK
    P = 2 * C
    npair = RW // LANE

    @pl.when(n == 0)
    def _():
        s_scr[...] = s0_ref[0]

    row = lax.broadcasted_iota(I32, (P, P), 0)
    col = lax.broadcasted_iota(I32, (P, P), 1)
    same = (row >> 6) == (col >> 6)
    dlt = (col & (C - 1)) - (row & (C - 1)) if reverse else (row & (C - 1)) - (col & (C - 1))
    strict = same & (dlt > 0)
    incl = same & (dlt >= 0)
    eye = (row == col).astype(F32)
    lvl_masks = [((row >> (lvl + 1)) == (col >> (lvl + 1))) & ((row >> lvl) != (col >> lvl)) for lvl in range(6)]
    rc = lax.broadcasted_iota(I32, (C, C), 0)
    cc = lax.broadcasted_iota(I32, (C, C), 1)
    tri = jnp.where((cc >= rc) if reverse else (rc >= cc), 1.0, 0.0).astype(BF16)
    head0 = lax.broadcasted_iota(I32, (C, LANE), 1) < HEAD

    def stack(x):
        return jnp.concatenate([jnp.where(head0, x, 0.0), jnp.where(head0, 0.0, x)], axis=0)

    streams = [(g, p) for g in range(group) for p in range(npair)]
    tm_ = {}
    for (g, p) in streams:
        t0, t1 = g * C, (g + 1) * C
        lo, hi = p * LANE, (p + 1) * LANE
        lw = sd_ref[0, t0:t1, lo:hi]
        tm_[(g, p)] = dict(lw=lw, cl=_dot_exact_lhs(tri, lw))
    for (g, p) in streams:
        t = tm_[(g, p)]
        t0, t1 = g * C, (g + 1) * C
        lo, hi = p * LANE, (p + 1) * LANE
        r = ss_ref[t0:t1, lo:hi]
        v = ss_ref[t0:t1, RW + lo:RW + hi]
        kk = ss_ref[t0:t1, 2 * RW + lo:2 * RW + hi]
        k2 = sd_ref[0, t0:t1, RW + lo:RW + hi]
        b = sd_ref[0, t0:t1, 2 * RW + lo:2 * RW + hi]
        cl, lw = t['cl'], t['lw']
        t['ptot'] = jnp.exp(jnp.sum(lw, axis=0, keepdims=True))
        pinv = jnp.exp(-cl)
        left = jnp.concatenate([stack(-kk * jnp.exp(cl - lw)), stack(r * jnp.exp(cl))], axis=0)
        right = jnp.concatenate([stack(b * pinv), stack(k2 * pinv)], axis=0)
        t['v_st'] = stack(v)
        t['left'] = left.astype(BF16)
        t['bk'] = (right * t['ptot']).astype(BF16)
        aa = _bdot(left, right, NT_DIMS)
        t['nmat'] = jnp.where(strict, aa[0:P, 0:P], 0.0)
        t['a_ak'] = jnp.where(strict, aa[0:P, P:2 * P], 0.0)
        t['a_rbk'] = jnp.concatenate([jnp.where(incl, aa[P:2 * P, 0:P], 0.0),
                                      jnp.where(incl, aa[P:2 * P, P:2 * P], 0.0)], axis=1).astype(BF16)
        t['tinv'] = eye + jnp.where(lvl_masks[0], t['nmat'], 0.0)
    for sk in streams:
        t = tm_[sk]
        t['akv'] = _bdot(t['a_ak'], t['v_st'])
    for m in lvl_masks[1:]:
        for sk in streams:
            t = tm_[sk]
            t['et'] = _bdot(jnp.where(m, t['nmat'], 0.0), t['tinv'])
        for sk in streams:
            t = tm_[sk]
            t['tinv'] = t['tinv'] + _bdot(t['tinv'], t['et'])
    state = [s_scr[p] for p in range(npair)]
    pairs = range(npair)
    for g in (range(group - 1, -1, -1) if reverse else range(group)):
        ts = [tm_[(g, p)] for p in pairs]
        a_s = [_bdot(ts[p]['left'], state[p], NT_DIMS) for p in pairs]
        u = [_bdot(ts[p]['tinv'], a_s[p][0:P] + ts[p]['akv']) for p in pairs]
        uv = [jnp.concatenate([u[p], ts[p]['v_st']], axis=0) for p in pairs]
        state = [state[p] * ts[p]['ptot'] + _bdot(uv[p], ts[p]['bk'], TN_DIMS) for p in pairs]
        for p in pairs:
            y_st = a_s[p][P:2 * P] + _bdot(ts[p]['a_rbk'], uv[p])
            y_ref[g * C:(g + 1) * C, p * LANE:(p + 1) * LANE] = y_st[0:C] + y_st[C:P]
    for p in pairs:
        s_scr[p] = state[p]

    @pl.when(n == nsteps - 1)
    def _():
        sfin_ref[0] = s_scr[...]


def rwkv_scan_dir(ss, sd, s0, B, T, d):
    nc = T // CHUNK
    group = min(RWKV_GROUP, nc)
    nsteps = nc // group
    npair = RW // LANE
    rows = group * CHUNK
    blk = (lambda b, n: b * nsteps + nsteps - 1 - n) if d else (lambda b, n: b * nsteps + n)
    kern = functools.partial(_rwkv_scan_kernel, nsteps=nsteps, group=group, reverse=bool(d))
    return pl.pallas_call(
        kern,
        out_shape=[jax.ShapeDtypeStruct((B * T, RW), F32),
                   jax.ShapeDtypeStruct((B, npair, LANE, LANE), F32)],
        grid=(B, nsteps),
        in_specs=[pl.BlockSpec((rows, 3 * RW), lambda b, n: (blk(b, n), 0)),
                  pl.BlockSpec((1, rows, 3 * RW), lambda b, n: (d, blk(b, n), 0)),
                  pl.BlockSpec((1, npair, LANE, LANE), lambda b, n: (b, 0, 0, 0))],
        out_specs=[pl.BlockSpec((rows, RW), lambda b, n: (blk(b, n), 0)),
                   pl.BlockSpec((1, npair, LANE, LANE), lambda b, n: (b, 0, 0, 0))],
        scratch_shapes=[pltpu.VMEM((npair, LANE, LANE), F32)],
        compiler_params=_cp(("arbitrary", "arbitrary")),
        name="rwkv_scan_bwd" if d else "rwkv_scan_fwd",
    )(ss, sd, s0)


def rwkv_scan(ss, sd, s0, B, T):
    y0, f0 = rwkv_scan_dir(ss, sd, s0[0], B, T, 0)
    y1, f1 = rwkv_scan_dir(ss, sd, s0[1], B, T, 1)
    return (y0, y1), (f0, f1)


GLA_KP = 256


GLA_GROUP = 8


def _gla_kernel(pbg_ref, aup_ref, ab_ref, s0_ref, o_ref, sfin_ref, s_scr, *, nsteps, group, reverse):
    n = pl.program_id(1)
    C = CHUNK
    R = group * C

    @pl.when(n == 0)
    def _():
        s_scr[...] = s0_ref[0]

    q = pbg_ref[:, 0:GLA_KP]
    k = pbg_ref[:, GLA_KP:2 * GLA_KP]
    v = pbg_ref[:, 2 * GLA_KP:2 * GLA_KP + RW]
    ad = pbg_ref[:, 2 * GLA_KP + RW:2 * GLA_KP + RW + LANE]
    x = _dot33(ad, aup_ref[0]) + ab_ref[0]
    la = (jnp.minimum(x, 0.0) - jnp.log(1.0 + jnp.exp(-jnp.abs(x)))) * (1.0 / GLA_TEMP)
    chunks = range(group)
    sl = [slice(g * C, (g + 1) * C) for g in chunks]
    rr = lax.broadcasted_iota(I32, (C, C), 0)
    rc = lax.broadcasted_iota(I32, (C, C), 1)
    tri = jnp.where((rc >= rr) if reverse else (rr >= rc), 1.0, 0.0).astype(BF16)
    bcums = [_dot_exact_lhs(tri, la[sl[g]]) for g in chunks]
    last = 0 if reverse else C - 1
    tots = [bc[last:last + 1, :] for bc in bcums]
    bcum = jnp.concatenate(bcums, axis=0)
    tot = jnp.concatenate([jnp.broadcast_to(t, (C, GLA_KP)) for t in tots], axis=0)
    q_in = q * jnp.exp(bcum) * (GLA_K ** -0.5)
    k_in = k * jnp.exp(-bcum)
    k_st = k * jnp.exp(tot - bcum)
    dn = [jnp.exp(t) for t in tots]
    klane = lax.broadcasted_iota(I32, (C, GLA_KP), 1)
    rt = lax.broadcasted_iota(I32, (GLA_HEADS * C, C), 0) & (C - 1)
    ct = lax.broadcasted_iota(I32, (GLA_HEADS * C, C), 1)
    causal = (ct >= rt) if reverse else (rt >= ct)
    vlane = lax.broadcasted_iota(I32, (C, RW), 1)
    sv = lax.broadcasted_iota(I32, (RW, GLA_KP), 0) >> 6
    sk = lax.broadcasted_iota(I32, (RW, GLA_KP), 1) >> 5
    q_rows = [jnp.concatenate([jnp.where((klane >> 5) == h, q_in[sl[g]], 0.0) for h in range(GLA_HEADS)],
                              axis=0).astype(BF16) for g in chunks]
    att = [jnp.where(causal, _bdot(q_rows[g], k_in[sl[g]], NT_DIMS), 0.0) for g in chunks]
    o_rows = [_bdot(att[g], v[sl[g]]) for g in chunks]
    kv = [jnp.where(sv == sk, _bdot(v[sl[g]], k_st[sl[g]], TN_DIMS), 0.0) for g in chunks]
    s = s_scr[...]
    for g in (reversed(chunks) if reverse else chunks):
        o = _bdot(q_in[sl[g]], s, NT_DIMS)
        for h in range(GLA_HEADS):
            o = o + jnp.where((vlane >> 6) == h, o_rows[g][h * C:(h + 1) * C], 0.0)
        o_ref[sl[g], :] = o
        s = s * dn[g] + kv[g]
    s_scr[...] = s

    @pl.when(n == nsteps - 1)
    def _():
        sfin_ref[0] = s_scr[...]


def gla_scan_dir(pbg, aup, ab, s0, B, T, d):
    nc = T // CHUNK
    group = min(GLA_GROUP, nc)
    nsteps = nc // group
    rows = group * CHUNK
    blk = (lambda b, n: b * nsteps + nsteps - 1 - n) if d else (lambda b, n: b * nsteps + n)
    kern = functools.partial(_gla_kernel, nsteps=nsteps, group=group, reverse=bool(d))
    return pl.pallas_call(
        kern,
        out_shape=[jax.ShapeDtypeStruct((B * T, RW), F32),
                   jax.ShapeDtypeStruct((B, RW, GLA_KP), F32)],
        grid=(B, nsteps),
        in_specs=[pl.BlockSpec((rows, PBG_W), lambda b, n: (blk(b, n), 0)),
                  pl.BlockSpec((1, LANE, GLA_KP), lambda b, n: (d, 0, 0)),
                  pl.BlockSpec((1, 1, GLA_KP), lambda b, n: (d, 0, 0)),
                  pl.BlockSpec((1, RW, GLA_KP), lambda b, n: (b, 0, 0))],
        out_specs=[pl.BlockSpec((rows, RW), lambda b, n: (blk(b, n), 0)),
                   pl.BlockSpec((1, RW, GLA_KP), lambda b, n: (b, 0, 0))],
        scratch_shapes=[pltpu.VMEM((RW, GLA_KP), F32)],
        compiler_params=_cp(("arbitrary", "arbitrary")),
        name="gla_scan_bwd" if d else "gla_scan_fwd",
    )(pbg, aup, ab, s0)


def gla_scan(pbg, aup, ab, s0, B, T):
    o0, f0 = gla_scan_dir(pbg, aup, ab, s0[0], B, T, 0)
    o1, f1 = gla_scan_dir(pbg, aup, ab, s0[1], B, T, 1)
    return (o0, o1), (f0, f1)


S5_PITCH = 72


def _s5_kernel(u_ref, a_ref, bbd_ref, cbd_ref, s0_ref, y_ref, sfin_ref, x_scr, st_scr, *, nc, nb):
    d = pl.program_id(0)
    n = pl.program_id(1)
    C = CHUNK
    half = S5_LANES // 2

    @pl.when(n == 0)
    def _():
        st_scr[...] = s0_ref[0]

    nslab = S5_LANES // LANE
    bbd = bbd_ref[0]
    for b in range(nb):
        bu = _bdot(u_ref[b], bbd)
        for j in range(nslab):
            x_scr[j, b * S5_PITCH:b * S5_PITCH + C, :] = bu[:, j * LANE:(j + 1) * LANE]
    a_re = a_ref[0, :, 0:half]
    a_im = a_ref[0, :, half:S5_LANES]

    def step(i, carry):
        re, im = carry
        t = i + d * (C - 1 - 2 * i)
        rows = pl.ds(t, nb, stride=S5_PITCH)
        bu = jnp.concatenate([x_scr[j, rows, :] for j in range(nslab)], axis=1)
        nre = a_re * re - a_im * im + bu[:, 0:half]
        nim = a_re * im + a_im * re + bu[:, half:S5_LANES]
        for j in range(nslab // 2):
            x_scr[j, rows, :] = nre[:, j * LANE:(j + 1) * LANE]
            x_scr[nslab // 2 + j, rows, :] = nim[:, j * LANE:(j + 1) * LANE]
        return nre, nim

    st = st_scr[...]
    re, im = lax.fori_loop(0, C, step, (st[:, 0:half], st[:, half:S5_LANES]), unroll=4)
    st_scr[...] = jnp.concatenate([re, im], axis=1)
    cbd = cbd_ref[0]
    for b in range(nb):
        xs = jnp.concatenate([x_scr[j, b * S5_PITCH:b * S5_PITCH + C, :] for j in range(nslab)], axis=1)
        y_ref[0, b] = _bdot(xs, cbd)

    @pl.when(n == nc - 1)
    def _():
        sfin_ref[0] = st_scr[...]


def s5_scan(u3, a_bar, bbd, cbd, s0):
    B, T, _ = u3.shape
    nc = T // CHUNK
    chunk = lambda d, n: n + d * (nc - 1 - 2 * n)
    kern = functools.partial(_s5_kernel, nc=nc, nb=B)
    return pl.pallas_call(
        kern,
        out_shape=[jax.ShapeDtypeStruct((2, B, T, S5_W), F32),
                   jax.ShapeDtypeStruct((2, B, S5_LANES), F32)],
        grid=(2, nc),
        in_specs=[pl.BlockSpec((B, CHUNK, S5_W), lambda d, n: (0, chunk(d, n), 0)),
                  pl.BlockSpec((1, 1, S5_LANES), lambda d, n: (d, 0, 0)),
                  pl.BlockSpec((1, S5_W, S5_LANES), lambda d, n: (d, 0, 0)),
                  pl.BlockSpec((1, S5_LANES, S5_W), lambda d, n: (d, 0, 0)),
                  pl.BlockSpec((1, B, S5_LANES), lambda d, n: (d, 0, 0))],
        out_specs=[pl.BlockSpec((1, B, CHUNK, S5_W), lambda d, n: (d, 0, chunk(d, n), 0)),
                   pl.BlockSpec((1, B, S5_LANES), lambda d, n: (d, 0, 0))],
        scratch_shapes=[pltpu.VMEM((S5_LANES // LANE, B * S5_PITCH, LANE), F32),
                        pltpu.VMEM((B, S5_LANES), F32)],
        compiler_params=_cp(("arbitrary", "arbitrary")),
        name="s5_scan",
    )(u3, a_bar, bbd, cbd, s0)


ROWT = 4
U32 = jnp.uint32


def _pack_bf16_pair(a, b):
    au = lax.bitcast_convert_type(a.astype(BF16).astype(F32), U32)
    bu = lax.bitcast_convert_type(b.astype(BF16).astype(F32), U32)
    return (au >> 16) | bu


def _unpack_bf16_pair(w):
    return (lax.bitcast_convert_type(w << 16, F32), lax.bitcast_convert_type(w & jnp.uint32(0xFFFF0000), F32))


def _tile_rows_store(ref, val):
    tm = val.shape[0]
    assert val.shape[1] == 2 * ROWT * LANE
    for s in range(ROWT):
        ref[pl.ds(s, tm, stride=ROWT), :] = _pack_bf16_pair(val[:, 2 * s * LANE:(2 * s + 1) * LANE],
                                                            val[:, (2 * s + 1) * LANE:(2 * s + 2) * LANE])


def _tile_rows_load(ref, tm, base=0):
    parts = []
    for s in range(ROWT):
        parts.extend(_unpack_bf16_pair(ref[pl.ds(base + s, tm, stride=ROWT), :]))
    return jnp.concatenate(parts, axis=1)


def _mix_out_kernel(y0_ref, y1_ref, gb_ref, o0_ref, o1_ref, gg_ref, yd_ref, u_ref, x_ref, mod_ref,
                    hb_ref, lng_ref, lnb_ref, gng_ref, s5d_ref, gluw_ref, glub_ref, wout_ref, l1g_ref, l1b_ref,
                    *rest, alpha):
    x1_ref, h2_ref, h2t_ref = rest[-3:]
    hb = hb_ref[...]
    inv = 1.0 / HEAD
    y = y0_ref[...] + y1_ref[...]
    yc = y - _dot_split_rhs(y, hb) * inv
    var = _dot_split_rhs(yc * yc, hb) * inv
    gn = yc * lax.rsqrt(var + RWKV_GN_EPS) * lng_ref[...] + lnb_ref[...]
    y_a = (gn + gb_ref[:, RW:2 * RW]) * gb_ref[:, 0:RW]
    o = o0_ref[...] + o1_ref[...]
    o = o * lax.rsqrt(_dot_split_rhs(o * o, hb) * inv + 1e-6) * gng_ref[...]
    y_b = o * _silu(gg_ref[...])
    c = s5d_ref[...] * u_ref[...] + yd_ref[0] + yd_ref[1]
    c = 0.5 * c * (1.0 + jnp.tanh(math.sqrt(2.0 / math.pi) * (c + 0.044715 * (c * c * c))))
    y_c = c * _sigmoid(_bdot(c, gluw_ref[...]) + glub_ref[...])
    y_mix = (_bdot(y_a, wout_ref[0:RW, :]) + _bdot(y_b, wout_ref[RW:2 * RW, :])
             + _bdot(y_c, wout_ref[2 * RW:2 * RW + S5_W, :]))
    x1 = _layer_norm(alpha * x_ref[...] + mod_ref[0, 2:3, :] * y_mix) * l1g_ref[...] + l1b_ref[...]
    x1_ref[...] = x1
    h2 = _layer_norm(x1) * (1.0 + mod_ref[0, 4:5, :]) + mod_ref[0, 3:4, :]
    h2_ref[...] = h2
    _tile_rows_store(h2t_ref, h2)


def mix_out(y01, gb, o01, gg, yd, u, stream, mod3, wts, row_of_tile, tm, alpha, out_rows=None, out_row0=0, into=None):
    x2, row0, N = stream
    D = x2.shape[1]
    out_rows = N if out_rows is None else out_rows
    assert row0 % tm == 0 and out_row0 % tm == 0 and N % tm == 0
    t0, o0 = row0 // tm, out_row0 // tm
    full = lambda a: pl.BlockSpec(a.shape, lambda i: (0,) * a.ndim)
    kern = functools.partial(_mix_out_kernel, alpha=alpha)
    ins = [y01[0], y01[1], gb, o01[0], o01[1], gg, yd, u, x2, mod3, *wts]
    extra = [] if into is None else list(into)
    return pl.pallas_call(
        kern,
        out_shape=[jax.ShapeDtypeStruct((out_rows, D), F32),
                   jax.ShapeDtypeStruct((out_rows, D), F32),
                   jax.ShapeDtypeStruct((out_rows * ROWT, LANE), U32)],
        grid=(N // tm,),
        in_specs=[pl.BlockSpec((tm, RW), lambda i: (i, 0)),
                  pl.BlockSpec((tm, RW), lambda i: (i, 0)),
                  pl.BlockSpec((tm, 2 * RW), lambda i: (i, 0)),
                  pl.BlockSpec((tm, RW), lambda i: (i, 0)),
                  pl.BlockSpec((tm, RW), lambda i: (i, 0)),
                  pl.BlockSpec((tm, RW), lambda i: (i, 0)),
                  pl.BlockSpec((2, tm, S5_W), lambda i: (0, i, 0)),
                  pl.BlockSpec((tm, S5_W), lambda i: (i, 0)),
                  pl.BlockSpec((tm, D), lambda i: (i + t0, 0)),
                  pl.BlockSpec((1, 6, D), lambda i: (row_of_tile(i), 0, 0))] + [full(a) for a in wts]
                 + [pl.BlockSpec(memory_space=pl.ANY)] * len(extra),
        out_specs=[pl.BlockSpec((tm, D), lambda i: (i + o0, 0)),
                   pl.BlockSpec((tm, D), lambda i: (i + o0, 0)),
                   pl.BlockSpec((tm * ROWT, LANE), lambda i: (i + o0, 0))],
        input_output_aliases={len(ins) + k: k for k in range(len(extra))},
        compiler_params=_cp(("arbitrary",)),
        name="mix_out",
    )(*ins, *extra)


def _first_max(x, idx, big):
    m = jnp.max(x, axis=0, keepdims=True)
    first = jnp.min(jnp.where(x == m, idx, big), axis=0, keepdims=True)
    return m, first


def _router_kernel(h_ref, rwt_ref, bias_ref, e_ref, w_ref):
    tm = h_ref.shape[0]
    gsz = N_EXPERTS // N_EGROUPS
    ninf = -jnp.inf
    s = _sigmoid(_dot33(rwt_ref[...], h_ref[...], NT_DIMS))
    ssel = s + bias_ref[:, 0:1]
    gi = lax.broadcasted_iota(I32, (gsz, tm), 0)
    gscore = []
    for g in range(N_EGROUPS):
        xg = ssel[g * gsz:(g + 1) * gsz, :]
        m1, i1 = _first_max(xg, gi, gsz)
        m2 = jnp.max(jnp.where(gi == i1, ninf, xg), axis=0, keepdims=True)
        gscore.append(m1 + m2)
    cur = jnp.concatenate(gscore, axis=0)
    gidx = lax.broadcasted_iota(I32, (N_EGROUPS, tm), 0)
    picked = jnp.zeros((N_EGROUPS, tm), F32)
    for _ in range(TOPK_GROUPS):
        _, first = _first_max(cur, gidx, N_EGROUPS)
        hit = gidx == first
        picked = jnp.where(hit, 1.0, picked)
        cur = jnp.where(hit, ninf, cur)
    x = jnp.concatenate(
        [jnp.where(picked[g:g + 1, :] > 0.5, ssel[g * gsz:(g + 1) * gsz, :], ninf) for g in range(N_EGROUPS)], axis=0)
    ei = lax.broadcasted_iota(I32, (N_EXPERTS, tm), 0)
    idxs, ws = [], []
    for _ in range(TOP_K):
        _, first = _first_max(x, ei, N_EXPERTS)
        hit = ei == first
        idxs.append(first)
        ws.append(jnp.sum(jnp.where(hit, s, 0.0), axis=0, keepdims=True))
        x = jnp.where(hit, ninf, x)
    w = jnp.concatenate(ws, axis=0)
    e_ref[...] = jnp.concatenate(idxs, axis=0)
    w_ref[...] = w / jnp.sum(w, axis=0, keepdims=True) * ROUTE_SCALE


def moe_router(h2, rwt, bias_b, tm):
    N, D = h2.shape
    return pl.pallas_call(
        _router_kernel,
        out_shape=[jax.ShapeDtypeStruct((TOP_K, N), I32), jax.ShapeDtypeStruct((TOP_K, N), F32)],
        grid=(N // tm,),
        in_specs=[pl.BlockSpec((tm, D), lambda i: (i, 0)),
                  pl.BlockSpec((N_EXPERTS, D), lambda i: (0, 0)),
                  pl.BlockSpec((N_EXPERTS, LANE), lambda i: (0, 0))],
        out_specs=[pl.BlockSpec((TOP_K, tm), lambda i: (0, i)),
                   pl.BlockSpec((TOP_K, tm), lambda i: (0, i))],
        compiler_params=_cp(("arbitrary",)),
        name="moe_router",
    )(h2, rwt, bias_b)


def _moe_count_kernel(e_ref, pstart_ref, plan_ref, blk_ref, cnt_scr, *, nt, nbp):
    i = pl.program_id(0)
    tp = e_ref.shape[1]

    @pl.when(i == 0)
    def _():
        cnt_scr[...] = jnp.zeros_like(cnt_scr)

    ei = lax.broadcasted_iota(I32, (N_EXPERTS, tp), 0)
    acc = jnp.zeros((N_EXPERTS, 1), F32)
    for k in range(TOP_K):
        acc = acc + jnp.sum(jnp.where(ei == e_ref[k:k + 1, :], 1.0, 0.0), axis=1, keepdims=True)
    cnt_scr[...] = cnt_scr[...] + acc

    @pl.when(i == nt - 1)
    def _():
        cnt = cnt_scr[...].astype(I32)
        padded = (cnt + (MOE_BLK - 1)) & (-MOE_BLK)
        r = lax.broadcasted_iota(I32, (N_EXPERTS, N_EXPERTS), 0)
        c = lax.broadcasted_iota(I32, (N_EXPERTS, N_EXPERTS), 1)
        tri = jnp.where(c <= r, 1.0, 0.0).astype(BF16)
        padded_b = jnp.broadcast_to(padded.astype(F32), (N_EXPERTS, LANE))
        p_end = _dot_exact_lhs(tri, padded_b)
        pstart = p_end - padded_b
        pstart_ref[...] = pstart.astype(I32)
        diag = r == c
        ps_row = jnp.sum(jnp.where(diag, pstart[:, 0:1], 0.0), axis=0, keepdims=True)
        cnt_row = jnp.sum(jnp.where(diag, cnt_scr[...], 0.0), axis=0, keepdims=True)
        plan_ref[...] = jnp.concatenate([ps_row, cnt_row, jnp.zeros((SUB - 2, N_EXPERTS), F32)], axis=0).astype(I32)
        lim =(lax.broadcasted_iota(I32, (N_EXPERTS, nbp), 1) * MOE_BLK).astype(F32)
        be = jnp.sum(jnp.where(p_end[:, 0:1] <= lim, 1.0, 0.0), axis=0, keepdims=True)
        be = jnp.minimum(be, N_EXPERTS - 1.0)
        nused = jnp.max(p_end[:, 0:1], axis=0, keepdims=True) * (1.0 / MOE_BLK)
        blk_ref[...] = jnp.concatenate([jnp.broadcast_to(be, (SUB // 2, nbp)),
                                        jnp.broadcast_to(nused, (SUB // 2, nbp))], axis=0).astype(I32)


def moe_counts(eidx, tp, nbp):
    N = eidx.shape[1]
    nt = N // tp
    kern = functools.partial(_moe_count_kernel, nt=nt, nbp=nbp)
    return pl.pallas_call(
        kern,
        out_shape=[jax.ShapeDtypeStruct((N_EXPERTS, LANE), I32),
                   jax.ShapeDtypeStruct((SUB, N_EXPERTS), I32),
                   jax.ShapeDtypeStruct((SUB, nbp), I32)],
        grid=(nt,),
        in_specs=[pl.BlockSpec((TOP_K, tp), lambda i: (0, i))],
        out_specs=[pl.BlockSpec((N_EXPERTS, LANE), lambda i: (0, 0)),
                   pl.BlockSpec((SUB, N_EXPERTS), lambda i: (0, 0)),
                   pl.BlockSpec((SUB, nbp), lambda i: (0, 0))],
        scratch_shapes=[pltpu.VMEM((N_EXPERTS, 1), F32)],
        compiler_params=_cp(("arbitrary",)),
        name="moe_counts",
    )(eidx)


def _moe_dest_kernel(e_ref, pstart_ref, dest_ref, base_scr):
    i = pl.program_id(0)
    tp = e_ref.shape[1]

    @pl.when(i == 0)
    def _():
        base_scr[...] = pstart_ref[:, 0:1].astype(F32)

    ei = lax.broadcasted_iota(I32, (N_EXPERTS, tp), 0)
    r = lax.broadcasted_iota(I32, (tp, tp), 0)
    c = lax.broadcasted_iota(I32, (tp, tp), 1)
    tri = jnp.where(r <= c, 1.0, 0.0).astype(BF16)
    base = base_scr[...]
    rows = []
    for k in range(TOP_K):
        hit = ei == e_ref[k:k + 1, :]
        oh = jnp.where(hit, 1.0, 0.0)
        cum = _bdot(oh, tri)
        rows.append(jnp.sum(jnp.where(hit, cum - 1.0 + base, 0.0), axis=0, keepdims=True))
        base = base + cum[:, tp - 1:tp]
    base_scr[...] = base
    dest_ref[...] = jnp.concatenate(rows, axis=0).astype(I32)


def moe_dest(eidx, pstart, tp):
    N = eidx.shape[1]
    return pl.pallas_call(
        _moe_dest_kernel,
        out_shape=jax.ShapeDtypeStruct((TOP_K, N), I32),
        grid=(N // tp,),
        in_specs=[pl.BlockSpec((TOP_K, tp), lambda i: (0, i)),
                  pl.BlockSpec((N_EXPERTS, LANE), lambda i: (0, 0))],
        out_specs=pl.BlockSpec((TOP_K, tp), lambda i: (0, i)),
        scratch_shapes=[pltpu.VMEM((N_EXPERTS, 1), F32)],
        compiler_params=_cp(("arbitrary",)),
        name="moe_dest",
    )(eidx, pstart)


def _tile_at(ref, token):
    return ref.at[pl.ds(pl.multiple_of(token * ROWT, ROWT), ROWT), :]


def _moe_scatter_kernel(dest_ref, h_ref, xs_hbm, sem):
    tp = dest_ref.shape[1]

    def issue(j, carry):
        for k in range(TOP_K):
            pltpu.make_async_copy(_tile_at(h_ref, j), _tile_at(xs_hbm, dest_ref[k, j]), sem).start(priority=k % 2)
        return carry

    lax.fori_loop(0, tp, issue, 0)
    total = xs_hbm.at[pl.ds(0, tp * TOP_K * ROWT), :]
    pltpu.make_async_copy(total, total, sem).wait()


def moe_scatter(dest, h2t, n_slots, tp):
    N = dest.shape[1]
    return pl.pallas_call(
        _moe_scatter_kernel,
        out_shape=jax.ShapeDtypeStruct((n_slots * ROWT, LANE), U32),
        grid=(N // tp,),
        in_specs=[pl.BlockSpec((TOP_K, tp), lambda i: (0, i), memory_space=pltpu.SMEM),
                  pl.BlockSpec((tp * ROWT, LANE), lambda i: (i, 0))],
        out_specs=pl.BlockSpec(memory_space=pl.ANY),
        scratch_shapes=[pltpu.SemaphoreType.DMA(())],
        compiler_params=_cp(("arbitrary",)),
        name="moe_scatter",
    )(dest, h2t)


PAD_BITS = tuple(1 << b for b in reversed(range(MOE_BLK.bit_length() - 1)))


def _moe_padfill_kernel(ps_ref, xs_in, xs_hbm, zero_scr, sem):
    del xs_in
    zero_scr[...] = jnp.zeros_like(zero_scr)

    def pad_copies(e, wait):
        cnt = ps_ref[1, e]
        npad = ((cnt + (MOE_BLK - 1)) & (-MOE_BLK)) - cnt
        off = ps_ref[0, e] + cnt
        for bit in PAD_BITS:
            @pl.when((npad & bit) != 0)
            def _():
                cp = pltpu.make_async_copy(zero_scr.at[pl.ds(0, bit * ROWT), :],
                                           xs_hbm.at[pl.ds(pl.multiple_of(off * ROWT, ROWT), bit * ROWT), :], sem)
                if wait:
                    cp.wait()
                else:
                    cp.start()
            off = off + (npad & bit)

    def issue(e, carry):
        pad_copies(e, False)
        return carry

    def drain(e, carry):
        pad_copies(e, True)
        return carry

    lax.fori_loop(0, N_EXPERTS, issue, 0)
    lax.fori_loop(0, N_EXPERTS, drain, 0)


def moe_padfill(pstart, xs):
    return pl.pallas_call(
        _moe_padfill_kernel,
        out_shape=jax.ShapeDtypeStruct(xs.shape, xs.dtype),
        grid=(1,),
        in_specs=[pl.BlockSpec(memory_space=pltpu.SMEM),
                  pl.BlockSpec(memory_space=pl.ANY)],
        out_specs=pl.BlockSpec(memory_space=pl.ANY),
        scratch_shapes=[pltpu.VMEM((PAD_BITS[0] * ROWT, LANE), U32), pltpu.SemaphoreType.DMA(())],
        input_output_aliases={1: 0},
        compiler_params=_cp(("arbitrary",)),
        name="moe_padfill",
    )(pstart, xs)


EXP_NX = 4
EXP_NY = 4


def _experts_kernel(plan_ref, w13_ref, w2_ref, xs_hbm, y_hbm, w13_bf, w2_bf, x_buf, y_buf, sem_in, sem_out):
    e = pl.program_id(0)
    ne = pl.num_programs(0)
    ff = w2_bf.shape[0]
    rows = MOE_BLK * ROWT
    shift = MOE_BLK.bit_length() - 1
    blocks_of = lambda c: lax.shift_right_logical(c + (MOE_BLK - 1), shift)
    nblk = blocks_of(plan_ref[1, e])
    first = lax.shift_right_logical(plan_ref[0, e], shift)
    nused = lax.shift_right_logical(plan_ref[0, ne - 1], shift) + blocks_of(plan_ref[1, ne - 1])

    def window(ref, g):
        return ref.at[pl.ds(pl.multiple_of(g * rows, rows), rows), :]

    def x_copy(g):
        return pltpu.make_async_copy(window(xs_hbm, g), x_buf.at[g & (EXP_NX - 1)], sem_in.at[g & (EXP_NX - 1)])

    def y_copy(g):
        return pltpu.make_async_copy(y_buf.at[g & (EXP_NY - 1)], window(y_hbm, g), sem_out.at[g & (EXP_NY - 1)])

    @pl.when(e == 0)
    def _():
        for g in range(EXP_NX - 1):
            @pl.when(g < nused)
            def _():
                x_copy(g).start()

    @pl.when(nblk > 0)
    def _():
        w13_bf[...] = w13_ref[0, 0].astype(BF16)
        w2_bf[...] = w2_ref[0, 0].astype(BF16)

        def body(j, carry):
            g = first + j
            x_copy(g).wait()

            @pl.when(g + (EXP_NX - 1) < nused)
            def _():
                x_copy(g + (EXP_NX - 1)).start()

            @pl.when(g >= EXP_NY)
            def _():
                y_copy(g - EXP_NY).wait()

            x = _tile_rows_load(x_buf.at[g & (EXP_NX - 1)], MOE_BLK).astype(BF16)
            h = jnp.dot(x, w13_bf[...], preferred_element_type=F32)
            act = (_silu(h[:, 0:ff]) * h[:, ff:2 * ff]).astype(BF16)
            _tile_rows_store(y_buf.at[g & (EXP_NY - 1)], jnp.dot(act, w2_bf[...], preferred_element_type=F32))
            y_copy(g).start()
            return carry

        lax.fori_loop(0, nblk, body, 0)

    @pl.when(e == ne - 1)
    def _():
        for back in range(EXP_NY, 0, -1):
            @pl.when(nused >= back)
            def _():
                y_copy(nused - back).wait()


def moe_experts(plan, xs, w13, w2, layer):
    _, E, D, F2 = w13.shape
    rows = MOE_BLK * ROWT
    grid_spec = pltpu.PrefetchScalarGridSpec(
        num_scalar_prefetch=1,
        grid=(E,),
        in_specs=[pl.BlockSpec((1, 1, D, F2), lambda e, p: (layer, e, 0, 0)),
                  pl.BlockSpec((1, 1, F2 // 2, D), lambda e, p: (layer, e, 0, 0)),
                  pl.BlockSpec(memory_space=pl.ANY)],
        out_specs=pl.BlockSpec(memory_space=pl.ANY),
        scratch_shapes=[pltpu.VMEM((D, F2), BF16), pltpu.VMEM((F2 // 2, D), BF16),
                        pltpu.VMEM((EXP_NX, rows, LANE), U32), pltpu.VMEM((EXP_NY, rows, LANE), U32),
                        pltpu.SemaphoreType.DMA((EXP_NX,)), pltpu.SemaphoreType.DMA((EXP_NY,))])
    return pl.pallas_call(
        _experts_kernel,
        out_shape=jax.ShapeDtypeStruct(xs.shape, U32),
        grid_spec=grid_spec,
        compiler_params=_cp(("arbitrary",)),
        name="moe_experts",
    )(plan, w13, w2, xs)


def _moe_combine_kernel(dest_ref, wt_ref, h_ref, x_ref, mod_ref, s13_ref, s2_ref, l2g_ref, l2b_ref, y_hbm,
                        o_ref, g_scr, sem, *, alpha, nt):
    s = pl.program_id(0)
    tm = h_ref.shape[0]
    ff = s2_ref.shape[0]
    rows = tm * ROWT
    slot_g = s % 2
    slot_c = 1 - slot_g

    def slot_copy(slot):
        return pltpu.make_async_copy(y_hbm.at[pl.ds(0, TOP_K * rows), :], g_scr.at[slot], sem.at[slot])

    @pl.when(s == 0)
    def _():
        g_scr[1] = jnp.zeros((TOP_K * rows, LANE), U32)

    @pl.when(s > 0)
    def _():
        slot_copy(slot_c).wait()

    for j in range(tm):
        for k in range(TOP_K):
            pltpu.make_async_copy(_tile_at(y_hbm, dest_ref[k, j]),
                                  g_scr.at[slot_g, pl.ds(k * rows + j * ROWT, ROWT), :],
                                  sem.at[slot_g]).start(priority=k % 2)
    hs = _bdot(h_ref[...], s13_ref[...])
    f = _bdot(_silu(hs[:, 0:ff]) * hs[:, ff:2 * ff], s2_ref[...])
    for k in range(TOP_K):
        f = f + wt_ref[:, k:k + 1] * _tile_rows_load(g_scr.at[slot_c], tm, base=k * rows)
    o_ref[...] = (_layer_norm(alpha * x_ref[...] + mod_ref[0, 5:6, :] * f) * l2g_ref[...] + l2b_ref[...])

    @pl.when(s == nt)
    def _():
        slot_copy(slot_g).wait()


def moe_combine(dest, wt_t, h2, x1, mod3, wts, y, row_of_tile, tm, alpha):
    N, D = h2.shape
    nt = N // tm
    full = lambda a: pl.BlockSpec(a.shape, lambda s: (0,) * a.ndim)
    kern = functools.partial(_moe_combine_kernel, alpha=alpha, nt=nt)
    nxt = lambda s: jnp.minimum(s, nt - 1)
    cur = lambda s: jnp.maximum(s - 1, 0)
    return pl.pallas_call(
        kern,
        out_shape=jax.ShapeDtypeStruct((N, D), F32),
        grid=(nt + 1,),
        in_specs=[pl.BlockSpec((TOP_K, tm), lambda s: (0, nxt(s)), memory_space=pltpu.SMEM),
                  pl.BlockSpec((tm, TOP_K), lambda s: (cur(s), 0)),
                  pl.BlockSpec((tm, D), lambda s: (cur(s), 0)),
                  pl.BlockSpec((tm, D), lambda s: (cur(s), 0)),
                  pl.BlockSpec((1, 6, D), lambda s: (row_of_tile(cur(s)), 0, 0))]
                 + [full(a) for a in wts] + [pl.BlockSpec(memory_space=pl.ANY)],
        out_specs=pl.BlockSpec((tm, D), lambda s: (cur(s), 0)),
        scratch_shapes=[pltpu.VMEM((2, TOP_K * tm * ROWT, LANE), U32), pltpu.SemaphoreType.DMA((2,))],
        compiler_params=_cp(("arbitrary",)),
        name="moe_combine",
    )(dest, wt_t, h2, x1, mod3, *wts, y)


def moe_ffn(h2, h2t, x1, mod3, mp, row_of_tile, tm, alpha):
    N = h2.shape[0]
    nb_total = (N * TOP_K + MOE_BLK - 1) // MOE_BLK + N_EXPERTS
    nbp = ((nb_total + LANE - 1) // LANE) * LANE
    tp = _tile(N, 256)
    eidx, wts = moe_router(h2, mp['rwt'], mp['bias'], tp)
    pstart, plan, blk = moe_counts(eidx, tp, nbp)
    dest = moe_dest(eidx, pstart, tp)
    xs = moe_scatter(dest, h2t, nb_total * MOE_BLK, tp)
    xs = moe_padfill(plan, xs)
    y = moe_experts(plan, xs, mp['w13'], mp['w2'], mp['layer'])
    return moe_combine(dest, wts.T, h2, x1, mod3, mp['comb'], y, row_of_tile, tm, alpha)


def token_mixers(stream, B, T, W, mod3, row_of_tile, lp, states, tm):
    pa, pbg, gg, u = inproj(stream, mod3, lp['w_in'], row_of_tile, tm)
    pac = short_conv(pa.reshape(B, T, PA_W), lp['taps'], W).reshape(B * T, PA_W)
    ss, sd, gb = rwkv_prep(pac, lp['prep'], tm)
    s_rwkv, s_gla, s_s5 = states
    y2, f_rwkv = rwkv_scan(ss, sd, s_rwkv, B, T)
    o2, f_gla = gla_scan(pbg, lp['gla_aup'], lp['gla_ab'], s_gla, B, T)
    yd, f_s5 = s5_scan(u.reshape(B, T, S5_W), lp['s5_a'], lp['s5_bbd'], lp['s5_cbd'], s_s5)
    return (y2, gb, o2, gg, yd.reshape(2, B * T, S5_W), u), (f_rwkv, f_gla, f_s5)


def zero_states(B):
    return ((jnp.zeros((B, RW // LANE, LANE, LANE), F32),) * 2,
            (jnp.zeros((B, RW, GLA_KP), F32),) * 2,
            jnp.zeros((2, B, S5_LANES), F32))


def _inproj_columns():
    r_cols = 3 * RW + 2 * DECAY_RANK + 2 * ICLR_RANK + GATE_RANK
    kd = GLA_HEADS * GLA_K
    gq, gk, gv = r_cols, r_cols + kd, r_cols + 2 * kd
    gg = gv + RW
    gad = gg + RW
    pc = gad + 2 * GLA_RANK
    z = lambda n: [-1] * n
    cols = list(range(0, r_cols)) + z(PA_W - r_cols)
    cols += list(range(gq, gq + kd)) + z(256 - kd)
    cols += list(range(gk, gk + kd)) + z(256 - kd)
    cols += list(range(gv, gv + RW))
    cols += list(range(gad, gad + 2 * GLA_RANK)) + z(LANE - 2 * GLA_RANK)
    cols += list(range(gg, gg + RW))
    cols += list(range(pc, pc + S5_W))
    assert len(cols) == NP_W
    return np.asarray(cols, np.int32)


def _head_block_ones():
    h = np.arange(RW) // HEAD
    return jnp.asarray(h[:, None] == h[None, :], BF16)


def _layer_params(l, p):
    D = p['w_in'].shape[1]
    cols = _inproj_columns()
    w_in = jnp.concatenate([p['w_in'][l], jnp.zeros((D, 1), F32)], axis=1)
    w_in = jnp.take(w_in, jnp.asarray(np.where(cols < 0, w_in.shape[1] - 1, cols)), axis=1).astype(BF16)
    taps = p['rwkv_conv'][l].reshape(9, -1)
    taps = jnp.concatenate([taps, jnp.zeros((9, PA_W - taps.shape[1]), F32)], axis=1)
    wup = jnp.zeros((LANE, 2 * RW), F32)
    aup = jnp.zeros((LANE, 2 * RW), F32)
    for d in range(2):
        wup = wup.at[d * DECAY_RANK:(d + 1) * DECAY_RANK, d * RW:(d + 1) * RW].set(p['rwkv_w_up'][l, d])
        o = 2 * DECAY_RANK + d * ICLR_RANK
        aup = aup.at[o:o + ICLR_RANK, d * RW:(d + 1) * RW].set(p['rwkv_a_up'][l, d])
    gup = jnp.zeros((LANE, RW), F32).at[0:GATE_RANK].set(p['rwkv_g_up'][l])
    row = lambda a: a.reshape(1, -1)
    prep = (wup, aup, gup, row(p['rwkv_w0'][l]), row(p['rwkv_a0'][l]), row(p['rwkv_k_k'][l]),
            row(p['rwkv_k_a'][l]), row(p['rwkv_r_k'][l]), _head_block_ones())
    kd = GLA_HEADS * GLA_K
    gla_aup = jnp.zeros((2, LANE, GLA_KP), F32)
    for d in range(2):
        gla_aup = gla_aup.at[d, d * GLA_RANK:(d + 1) * GLA_RANK, 0:kd].set(p['gla_a_up'][l, d])
    gla_ab = jnp.zeros((2, 1, GLA_KP), F32).at[:, 0, 0:kd].set(p['gla_a_bias'][l])
    lam_re, lam_im = p['s5_lam_re'][l], p['s5_lam_im'][l]
    dt = jnp.exp(p['s5_log_dt'][l])[:, :, None]
    zr, zi = lam_re[:, None, :] * dt, lam_im[:, None, :] * dt
    mag = jnp.exp(zr)
    ab_r, ab_i = mag * jnp.cos(zi), mag * jnp.sin(zi)
    den = (lam_re * lam_re + lam_im * lam_im)[:, None, :]
    f_r = ((ab_r - 1) * lam_re[:, None, :] + ab_i * lam_im[:, None, :]) / den
    f_i = (ab_i * lam_re[:, None, :] - (ab_r - 1) * lam_im[:, None, :]) / den
    b_re, b_im = p['s5_b_re'][l], p['s5_b_im'][l]
    bb_r = f_r[..., None] * b_re - f_i[..., None] * b_im
    bb_i = f_r[..., None] * b_im + f_i[..., None] * b_re
    eye_g = jnp.eye(S5_GROUPS, dtype=F32)
    half = S5_LANES // 2

    def in_blockdiag(bb):
        return jnp.einsum('dgpc,gh->dgchp', bb, eye_g).reshape(2, S5_W, half)

    def out_blockdiag(cc):
        return jnp.einsum('dgcp,gh->dgphc', cc, eye_g).reshape(2, half, S5_W)

    s5_bbd = jnp.concatenate([in_blockdiag(bb_r), in_blockdiag(bb_i)], axis=2).astype(BF16)
    s5_cbd = jnp.concatenate([out_blockdiag(p['s5_c_re'][l]), -out_blockdiag(p['s5_c_im'][l])], axis=1).astype(BF16)
    s5_a = jnp.concatenate([ab_r.reshape(2, 1, half), ab_i.reshape(2, 1, half)], axis=2)
    mix = (_head_block_ones(), row(p['rwkv_ln_g'][l]), row(p['rwkv_ln_b'][l]),
           row(jnp.tile(p['gla_norm_g'][l], GLA_HEADS)), row(p['s5_d'][l]),
           p['s5_glu_w'][l].astype(BF16), row(p['s5_glu_b'][l]), p['w_out'][l].astype(BF16),
           row(p['ln1_g'][l]), row(p['ln1_b'][l]))
    moe = dict(rwt=p['router_w'][l].T,
               bias=jnp.broadcast_to(p['router_bias'][l][:, None], (N_EXPERTS, LANE)),
               w13=p['exp_w13'], w2=p['exp_w2'], layer=l,
               comb=(p['sh_w13'][l].astype(BF16), p['sh_w2'][l].astype(BF16), row(p['ln2_g'][l]), row(p['ln2_b'][l])))
    return dict(w_in=w_in, taps=taps, prep=prep, gla_aup=gla_aup, gla_ab=gla_ab,
                s5_a=s5_a, s5_bbd=s5_bbd, s5_cbd=s5_cbd, mix=mix, moe=moe)


_ARG_NAMES = ('x', 'c', 'ctx', 'c_ctx', 'w_mod', 'b_mod', 'w_in', 'rwkv_conv', 'rwkv_w0', 'rwkv_w_up', 'rwkv_a0',
              'rwkv_a_up', 'rwkv_g_up', 'rwkv_k_k', 'rwkv_k_a', 'rwkv_r_k', 'rwkv_ln_g', 'rwkv_ln_b', 'gla_a_up',
              'gla_a_bias', 'gla_norm_g', 's5_lam_re', 's5_lam_im', 's5_log_dt', 's5_b_re', 's5_b_im', 's5_c_re',
              's5_c_im', 's5_d', 's5_glu_w', 's5_glu_b', 'w_out', 'ln1_g', 'ln1_b', 'router_w', 'router_bias',
              'exp_w13', 'exp_w2', 'sh_w13', 'sh_w2', 'ln2_g', 'ln2_b')


def _tile(n, pref):
    t = pref
    while n % t:
        t //= 2
    return t


def kernel(x, c, ctx, c_ctx, w_mod, b_mod, w_in, rwkv_conv, rwkv_w0, rwkv_w_up, rwkv_a0, rwkv_a_up, rwkv_g_up,
           rwkv_k_k, rwkv_k_a, rwkv_r_k, rwkv_ln_g, rwkv_ln_b, gla_a_up, gla_a_bias, gla_norm_g, s5_lam_re,
           s5_lam_im, s5_log_dt, s5_b_re, s5_b_im, s5_c_re, s5_c_im, s5_d, s5_glu_w, s5_glu_b, w_out, ln1_g,
           ln1_b, router_w, router_bias, exp_w13, exp_w2, sh_w13, sh_w2, ln2_g, ln2_b):
    p = dict(zip(_ARG_NAMES, (x, c, ctx, c_ctx, w_mod, b_mod, w_in, rwkv_conv, rwkv_w0, rwkv_w_up, rwkv_a0,
                              rwkv_a_up, rwkv_g_up, rwkv_k_k, rwkv_k_a, rwkv_r_k, rwkv_ln_g, rwkv_ln_b, gla_a_up,
                              gla_a_bias, gla_norm_g, s5_lam_re, s5_lam_im, s5_log_dt, s5_b_re, s5_b_im, s5_c_re,
                              s5_c_im, s5_d, s5_glu_w, s5_glu_b, w_out, ln1_g, ln1_b, router_w, router_bias,
                              exp_w13, exp_w2, sh_w13, sh_w2, ln2_g, ln2_b)))
    B, T, D = x.shape
    TC = ctx.shape[1]
    L = w_mod.shape[0]
    alpha = (2 * L) ** 0.25
    n_lat, n_ctx = B * T, B * TC
    R = ((B + 1 + SUB - 1) // SUB) * SUB
    cc = jnp.zeros((R, D), F32).at[0:B].set(c).at[B].set(c_ctx)
    mod = mod_table(cc, w_mod, b_mod)
    tm = _tile(T, 512)
    tmc = _tile(n_ctx, 512)
    tmm = min(_tile(T, 128), _tile(n_ctx, 128))
    lat_row = lambda i: (i * tm) // T
    ctx_row = lambda i: B
    lat = (x.reshape(n_lat, D), 0, n_lat)
    con = (ctx.reshape(n_ctx, D), 0, n_ctx)
    for l in range(L):
        last = l == L - 1
        lp = _layer_params(l, p)
        mod3 = mod[l].reshape(R, 6, D)
        outs_c, st_c = token_mixers(con, B, TC, TC, mod3, ctx_row, lp, zero_states(B), tmc)
        outs, _ = token_mixers(lat, B, T, GRID_W, mod3, lat_row, lp, st_c, tm)
        if last:
            x1, h2, h2t = mix_out(*outs, lat, mod3, lp['mix'], lat_row, tm, alpha)
            out = moe_ffn(h2, h2t, x1, mod3, lp['moe'], lambda i: (i * tmm) // T, tmm, alpha)
            lat = (out, 0, n_lat)
        else:
            n_all = n_ctx + n_lat
            bufs = mix_out(*outs_c, con, mod3, lp['mix'], ctx_row, tmc, alpha, out_rows=n_all)
            x1, h2, h2t = mix_out(*outs, lat, mod3, lp['mix'], lat_row, tm, alpha,
                                  out_rows=n_all, out_row0=n_ctx, into=bufs)
            row_all = lambda i: jnp.where(i * tmm < n_ctx, B, (i * tmm - n_ctx) // T)
            out = moe_ffn(h2, h2t, x1, mod3, lp['moe'], row_all, tmm, alpha)
            con, lat = (out, 0, n_ctx), (out, n_ctx, n_lat)
    out, row0, _ = lat
    return out[row0:row0 + n_lat].reshape(B, T, D)
```

```python
import functools
import math

import numpy as np
import jax
import jax.numpy as jnp
from jax import lax
from jax.experimental import pallas as pl
from jax.experimental.pallas import tpu as pltpu

F32 = jnp.float32
BF16 = jnp.bfloat16
I32 = jnp.int32

GRID_W = 64
RWKV_HEADS = 6
HEAD = 64
RW = RWKV_HEADS * HEAD
DECAY_RANK = 32
ICLR_RANK = 32
GATE_RANK = 64
RWKV_GN_EPS = 64e-5
GLA_HEADS = 6
GLA_K = 32
GLA_RANK = 16
GLA_TEMP = 16.0
S5_GROUPS = 16
S5_GROUP = 16
S5_STATE = 64
S5_W = S5_GROUPS * S5_GROUP
S5_LANES = 2 * S5_GROUPS * S5_STATE
N_EXPERTS = 256
TOP_K = 8
N_EGROUPS = 8
TOPK_GROUPS = 4
ROUTE_SCALE = 2.5
LN_EPS = 1e-6
CHUNK = 64
MOE_BLK = 256
LANE = 128
SUB = 8
VMEM_LIMIT = 56 * 1024 * 1024

PA_W = 3 * RW + 2 * LANE
PBG_W = 256 + 256 + RW + LANE
NP_W = PA_W + PBG_W + RW + S5_W


def _cp(sem):
    return pltpu.CompilerParams(dimension_semantics=sem, vmem_limit_bytes=VMEM_LIMIT)


def _sigmoid(x):
    return 1.0 / (1.0 + jnp.exp(-x))


def _silu(x):
    return x * _sigmoid(x)


def _bdot(a, b, dims=None):
    a = a.astype(BF16)
    b = b.astype(BF16)
    if dims is None:
        return jnp.dot(a, b, preferred_element_type=F32)
    return lax.dot_general(a, b, dims, preferred_element_type=F32)


def _split2(x):
    hi = x.astype(BF16)
    lo = (x - hi.astype(F32)).astype(BF16)
    return hi, lo


def _split3(x):
    hi = x.astype(BF16)
    r = x - hi.astype(F32)
    mid = r.astype(BF16)
    lo = (r - mid.astype(F32)).astype(BF16)
    return hi, mid, lo


NT_DIMS = (((1,), (1,)), ((), ()))
TN_DIMS = (((0,), (0,)), ((), ()))


def _dot33(a, b, dims=None):
    ah, al = _split2(a)
    bh, bl = _split2(b)
    return _bdot(ah, bh, dims) + (_bdot(ah, bl, dims) + _bdot(al, bh, dims))


def _dot_exact_lhs(m_exact, x, dims=None):
    h, m, l = _split3(x)
    return _bdot(m_exact, h, dims) + (_bdot(m_exact, m, dims) + _bdot(m_exact, l, dims))


def _dot_exact_rhs(x, m_exact, dims=None):
    h, m, l = _split3(x)
    return _bdot(h, m_exact, dims) + (_bdot(m, m_exact, dims) + _bdot(l, m_exact, dims))


def _dot_split_rhs(x, m_exact, dims=None):
    h, l = _split2(x)
    return _bdot(h, m_exact, dims) + _bdot(l, m_exact, dims)


def _layer_norm(x):
    mu = jnp.mean(x, axis=-1, keepdims=True)
    xc = x - mu
    var = jnp.mean(xc * xc, axis=-1, keepdims=True)
    return xc * lax.rsqrt(var + LN_EPS)


def _mod_kernel(c_ref, w_ref, b_ref, o_ref):
    s = _silu(c_ref[...])
    o_ref[0] = _dot33(s, w_ref[0]) + b_ref[0]


def mod_table(cc, w_mod, b_mod):
    L, D, D6 = w_mod.shape
    R = cc.shape[0]
    tn = 1536
    return pl.pallas_call(
        _mod_kernel,
        out_shape=jax.ShapeDtypeStruct((L, R, D6), F32),
        grid=(L, D6 // tn),
        in_specs=[pl.BlockSpec((R, D), lambda l, j: (0, 0)),
                  pl.BlockSpec((1, D, tn), lambda l, j: (l, 0, j)),
                  pl.BlockSpec((1, 1, tn), lambda l, j: (l, 0, j))],
        out_specs=pl.BlockSpec((1, R, tn), lambda l, j: (l, 0, j)),
        compiler_params=_cp(("arbitrary", "arbitrary")),
        name="mod_table",
    )(cc, w_mod, b_mod.reshape(L, 1, D6))


def _inproj_kernel(x_ref, mod_ref, w_ref, pa_ref, pbg_ref, gg_ref, u_ref):
    x = x_ref[...]
    h = _layer_norm(x) * (1.0 + mod_ref[0, 1:2, :]) + mod_ref[0, 0:1, :]
    hb = h.astype(BF16)
    o = 0
    for ref in (pa_ref, pbg_ref, gg_ref, u_ref):
        w = ref.shape[-1]
        ref[...] = jnp.dot(hb, w_ref[:, o:o + w], preferred_element_type=F32)
        o += w


def inproj(stream, mod3, w_bf, row_of_tile, tm):
    x2, row0, N = stream
    D = x2.shape[1]
    t0 = row0 // tm
    assert row0 % tm == 0 and N % tm == 0
    return pl.pallas_call(
        _inproj_kernel,
        out_shape=[jax.ShapeDtypeStruct((N, PA_W), F32),
                   jax.ShapeDtypeStruct((N, PBG_W), F32),
                   jax.ShapeDtypeStruct((N, RW), F32),
                   jax.ShapeDtypeStruct((N, S5_W), F32)],
        grid=(N // tm,),
        in_specs=[pl.BlockSpec((tm, D), lambda i: (i + t0, 0)),
                  pl.BlockSpec((1, 6, D), lambda i: (row_of_tile(i), 0, 0)),
                  pl.BlockSpec((D, NP_W), lambda i: (0, 0))],
        out_specs=[pl.BlockSpec((tm, PA_W), lambda i: (i, 0)),
                   pl.BlockSpec((tm, PBG_W), lambda i: (i, 0)),
                   pl.BlockSpec((tm, RW), lambda i: (i, 0)),
                   pl.BlockSpec((tm, S5_W), lambda i: (i, 0))],
        compiler_params=_cp(("arbitrary",)),
        name="inproj",
    )(x2, mod3, w_bf)


CONV_PAD = 72


def _conv_kernel(x_ref, taps_ref, o_ref, buf_ref, *, T, W, vertical):
    pad = CONV_PAD
    zeros = jnp.zeros((pad, LANE), F32)
    buf_ref[0:pad, :] = zeros
    buf_ref[pad + T:pad + T + pad, :] = zeros
    buf_ref[pad:pad + T, :] = x_ref[0]
    ch = min(T, 256)
    col = lax.broadcasted_iota(I32, (ch, LANE), 0) & (W - 1)
    left_ok = col >= 1
    right_ok = col <= W - 2
    for c in range(T // ch):
        base = pad + c * ch
        acc = jnp.zeros((ch, LANE), F32)
        for dr in ((0, 1, 2) if vertical else (1,)):
            for dc in range(3):
                off = (dr - 1) * W + (dc - 1)
                v = buf_ref[base + off:base + off + ch, :]
                if dc == 0:
                    v = jnp.where(left_ok, v, 0.0)
                elif dc == 2:
                    v = jnp.where(right_ok, v, 0.0)
                acc = acc + v * taps_ref[3 * dr + dc:3 * dr + dc + 1, :]
        o_ref[0, c * ch:(c + 1) * ch, :] = acc


def short_conv(pa3, taps9, W):
    B, T, C = pa3.shape
    vertical = T > W
    assert W & (W - 1) == 0 and (not vertical or W + 1 <= CONV_PAD)
    kern = functools.partial(_conv_kernel, T=T, W=W, vertical=vertical)
    return pl.pallas_call(
        kern,
        out_shape=jax.ShapeDtypeStruct((B, T, C), F32),
        grid=(B, C // LANE),
        in_specs=[pl.BlockSpec((1, T, LANE), lambda b, j: (b, 0, j)),
                  pl.BlockSpec((9, LANE), lambda b, j: (0, j))],
        out_specs=pl.BlockSpec((1, T, LANE), lambda b, j: (b, 0, j)),
        scratch_shapes=[pltpu.VMEM((T + 2 * CONV_PAD, LANE), F32)],
        compiler_params=_cp(("arbitrary", "arbitrary")),
        name="short_conv",
    )(pa3, taps9)


def _rwkv_prep_kernel(pa_ref, wup_ref, aup_ref, gup_ref, w0_ref, a0_ref, kk_ref, ka_ref, rk_ref, hb_ref,
                      ss_ref, sd_ref, gb_ref):
    r = pa_ref[:, 0:RW]
    k = pa_ref[:, RW:2 * RW]
    v = pa_ref[:, 2 * RW:3 * RW]
    wa = pa_ref[:, 3 * RW:3 * RW + LANE]
    gd = pa_ref[:, 3 * RW + LANE:3 * RW + 2 * LANE]
    z = w0_ref[...] + _dot33(jnp.tanh(wa), wup_ref[...])
    lw = -_sigmoid(z) * math.exp(-0.5)
    a = _sigmoid(a0_ref[...] + _bdot(wa, aup_ref[...]))
    g = _bdot(_sigmoid(gd), gup_ref[...])
    hb = hb_ref[...]
    kk = k * kk_ref[...]
    kk = kk * lax.rsqrt(_dot_split_rhs(kk * kk, hb) + 1e-12)
    ka = ka_ref[...]
    ss_ref[:, 0:RW] = r
    ss_ref[:, RW:2 * RW] = v
    ss_ref[:, 2 * RW:3 * RW] = kk
    rk2 = jnp.zeros_like(r)
    for d in range(2):
        ad = a[:, d * RW:(d + 1) * RW]
        k2 = k * (1.0 + (ad - 1.0) * ka)
        sd_ref[d, :, 0:RW] = lw[:, d * RW:(d + 1) * RW]
        sd_ref[d, :, RW:2 * RW] = k2
        sd_ref[d, :, 2 * RW:3 * RW] = kk * ad
        rk2 = rk2 + r * k2
    bonus = _dot_split_rhs(rk2 * rk_ref[...], hb) * v
    gb_ref[:, 0:RW] = g
    gb_ref[:, RW:2 * RW] = bonus


def rwkv_prep(pa2, wts, tm):
    N = pa2.shape[0]
    full = lambda a: pl.BlockSpec(a.shape, lambda i: (0,) * a.ndim)
    return pl.pallas_call(
        _rwkv_prep_kernel,
        out_shape=[jax.ShapeDtypeStruct((N, 3 * RW), F32),
                   jax.ShapeDtypeStruct((2, N, 3 * RW), F32),
                   jax.ShapeDtypeStruct((N, 2 * RW), F32)],
        grid=(N // tm,),
        in_specs=[pl.BlockSpec((tm, PA_W), lambda i: (i, 0))] + [full(a) for a in wts],
        out_specs=[pl.BlockSpec((tm, 3 * RW), lambda i: (i, 0)),
                   pl.BlockSpec((2, tm, 3 * RW), lambda i: (0, i, 0)),
                   pl.BlockSpec((tm, 2 * RW), lambda i: (i, 0))],
        compiler_params=_cp(("arbitrary",)),
        name="rwkv_prep",
    )(pa2, *wts)


RWKV_GROUP = 8


def _rwkv_scan_kernel(ss_ref, sd_ref, s0_ref, y_ref, sfin_ref, s_scr, *, nsteps, group, reverse):
    n = pl.program_id(1)
    C = CHUNK
    P = 2 * C
    npair = RW // LANE

    @pl.when(n == 0)
    def _():
        s_scr[...] = s0_ref[0]

    row = lax.broadcasted_iota(I32, (P, P), 0)
    col = lax.broadcasted_iota(I32, (P, P), 1)
    same = (row >> 6) == (col >> 6)
    dlt = (col & (C - 1)) - (row & (C - 1)) if reverse else (row & (C - 1)) - (col & (C - 1))
    strict = same & (dlt > 0)
    incl = same & (dlt >= 0)
    eye = (row == col).astype(F32)
    lvl_masks = [((row >> (lvl + 1)) == (col >> (lvl + 1))) & ((row >> lvl) != (col >> lvl)) for lvl in range(6)]
    rc = lax.broadcasted_iota(I32, (C, C), 0)
    cc = lax.broadcasted_iota(I32, (C, C), 1)
    tri = jnp.where((cc >= rc) if reverse else (rc >= cc), 1.0, 0.0).astype(BF16)
    head0 = lax.broadcasted_iota(I32, (C, LANE), 1) < HEAD

    def stack(x):
        return jnp.concatenate([jnp.where(head0, x, 0.0), jnp.where(head0, 0.0, x)], axis=0)

    streams = [(g, p) for g in range(group) for p in range(npair)]
    tm_ = {}
    for (g, p) in streams:
        t0, t1 = g * C, (g + 1) * C
        lo, hi = p * LANE, (p + 1) * LANE
        lw = sd_ref[0, t0:t1, lo:hi]
        tm_[(g, p)] = dict(lw=lw, cl=_dot_exact_lhs(tri, lw))
    for (g, p) in streams:
        t = tm_[(g, p)]
        t0, t1 = g * C, (g + 1) * C
        lo, hi = p * LANE, (p + 1) * LANE
        r = ss_ref[t0:t1, lo:hi]
        v = ss_ref[t0:t1, RW + lo:RW + hi]
        kk = ss_ref[t0:t1, 2 * RW + lo:2 * RW + hi]
        k2 = sd_ref[0, t0:t1, RW + lo:RW + hi]
        b = sd_ref[0, t0:t1, 2 * RW + lo:2 * RW + hi]
        cl, lw = t['cl'], t['lw']
        t['ptot'] = jnp.exp(jnp.sum(lw, axis=0, keepdims=True))
        pinv = jnp.exp(-cl)
        left = jnp.concatenate([stack(-kk * jnp.exp(cl - lw)), stack(r * jnp.exp(cl))], axis=0)
        right = jnp.concatenate([stack(b * pinv), stack(k2 * pinv)], axis=0)
        t['v_st'] = stack(v)
        t['left'] = left.astype(BF16)
        t['bk'] = (right * t['ptot']).astype(BF16)
        aa = _bdot(left, right, NT_DIMS)
        t['nmat'] = jnp.where(strict, aa[0:P, 0:P], 0.0)
        t['a_ak'] = jnp.where(strict, aa[0:P, P:2 * P], 0.0)
        t['a_rbk'] = jnp.concatenate([jnp.where(incl, aa[P:2 * P, 0:P], 0.0),
                                      jnp.where(incl, aa[P:2 * P, P:2 * P], 0.0)], axis=1).astype(BF16)
        t['tinv'] = eye + jnp.where(lvl_masks[0], t['nmat'], 0.0)
    for sk in streams:
        t = tm_[sk]
        t['akv'] = _bdot(t['a_ak'], t['v_st'])
    for m in lvl_masks[1:]:
        for sk in streams:
            t = tm_[sk]
            t['et'] = _bdot(jnp.where(m, t['nmat'], 0.0), t['tinv'])
        for sk in streams:
            t = tm_[sk]
            t['tinv'] = t['tinv'] + _bdot(t['tinv'], t['et'])
    state = [s_scr[p] for p in range(npair)]
    pairs = range(npair)
    for g in (range(group - 1, -1, -1) if reverse else range(group)):
        ts = [tm_[(g, p)] for p in pairs]
        a_s = [_bdot(ts[p]['left'], state[p], NT_DIMS) for p in pairs]
        u = [_bdot(ts[p]['tinv'], a_s[p][0:P] + ts[p]['akv']) for p in pairs]
        uv = [jnp.concatenate([u[p], ts[p]['v_st']], axis=0) for p in pairs]
        state = [state[p] * ts[p]['ptot'] + _bdot(uv[p], ts[p]['bk'], TN_DIMS) for p in pairs]
        for p in pairs:
            y_st = a_s[p][P:2 * P] + _bdot(ts[p]['a_rbk'], uv[p])
            y_ref[g * C:(g + 1) * C, p * LANE:(p + 1) * LANE] = y_st[0:C] + y_st[C:P]
    for p in pairs:
        s_scr[p] = state[p]

    @pl.when(n == nsteps - 1)
    def _():
        sfin_ref[0] = s_scr[...]


def rwkv_scan_dir(ss, sd, s0, B, T, d):
    nc = T // CHUNK
    group = min(RWKV_GROUP, nc)
    nsteps = nc // group
    npair = RW // LANE
    rows = group * CHUNK
    blk = (lambda b, n: b * nsteps + nsteps - 1 - n) if d else (lambda b, n: b * nsteps + n)
    kern = functools.partial(_rwkv_scan_kernel, nsteps=nsteps, group=group, reverse=bool(d))
    return pl.pallas_call(
        kern,
        out_shape=[jax.ShapeDtypeStruct((B * T, RW), F32),
                   jax.ShapeDtypeStruct((B, npair, LANE, LANE), F32)],
        grid=(B, nsteps),
        in_specs=[pl.BlockSpec((rows, 3 * RW), lambda b, n: (blk(b, n), 0)),
                  pl.BlockSpec((1, rows, 3 * RW), lambda b, n: (d, blk(b, n), 0)),
                  pl.BlockSpec((1, npair, LANE, LANE), lambda b, n: (b, 0, 0, 0))],
        out_specs=[pl.BlockSpec((rows, RW), lambda b, n: (blk(b, n), 0)),
                   pl.BlockSpec((1, npair, LANE, LANE), lambda b, n: (b, 0, 0, 0))],
        scratch_shapes=[pltpu.VMEM((npair, LANE, LANE), F32)],
        compiler_params=_cp(("arbitrary", "arbitrary")),
        name="rwkv_scan_bwd" if d else "rwkv_scan_fwd",
    )(ss, sd, s0)


def rwkv_scan(ss, sd, s0, B, T):
    y0, f0 = rwkv_scan_dir(ss, sd, s0[0], B, T, 0)
    y1, f1 = rwkv_scan_dir(ss, sd, s0[1], B, T, 1)
    return (y0, y1), (f0, f1)


GLA_KP = 256


GLA_GROUP = 8


def _gla_kernel(pbg_ref, aup_ref, ab_ref, s0_ref, o_ref, sfin_ref, s_scr, *, nsteps, group, reverse):
    n = pl.program_id(1)
    C = CHUNK
    R = group * C

    @pl.when(n == 0)
    def _():
        s_scr[...] = s0_ref[0]

    q = pbg_ref[:, 0:GLA_KP]
    k = pbg_ref[:, GLA_KP:2 * GLA_KP]
    v = pbg_ref[:, 2 * GLA_KP:2 * GLA_KP + RW]
    ad = pbg_ref[:, 2 * GLA_KP + RW:2 * GLA_KP + RW + LANE]
    x = _dot33(ad, aup_ref[0]) + ab_ref[0]
    la = (jnp.minimum(x, 0.0) - jnp.log(1.0 + jnp.exp(-jnp.abs(x)))) * (1.0 / GLA_TEMP)
    chunks = range(group)
    sl = [slice(g * C, (g + 1) * C) for g in chunks]
    rr = lax.broadcasted_iota(I32, (C, C), 0)
    rc = lax.broadcasted_iota(I32, (C, C), 1)
    tri = jnp.where((rc >= rr) if reverse else (rr >= rc), 1.0, 0.0).astype(BF16)
    bcums = [_dot_exact_lhs(tri, la[sl[g]]) for g in chunks]
    last = 0 if reverse else C - 1
    tots = [bc[last:last + 1, :] for bc in bcums]
    bcum = jnp.concatenate(bcums, axis=0)
    tot = jnp.concatenate([jnp.broadcast_to(t, (C, GLA_KP)) for t in tots], axis=0)
    q_in = q * jnp.exp(bcum) * (GLA_K ** -0.5)
    k_in = k * jnp.exp(-bcum)
    k_st = k * jnp.exp(tot - bcum)
    dn = [jnp.exp(t) for t in tots]
    klane = lax.broadcasted_iota(I32, (C, GLA_KP), 1)
    rt = lax.broadcasted_iota(I32, (GLA_HEADS * C, C), 0) & (C - 1)
    ct = lax.broadcasted_iota(I32, (GLA_HEADS * C, C), 1)
    causal = (ct >= rt) if reverse else (rt >= ct)
    vlane = lax.broadcasted_iota(I32, (C, RW), 1)
    sv = lax.broadcasted_iota(I32, (RW, GLA_KP), 0) >> 6
    sk = lax.broadcasted_iota(I32, (RW, GLA_KP), 1) >> 5
    q_rows = [jnp.concatenate([jnp.where((klane >> 5) == h, q_in[sl[g]], 0.0) for h in range(GLA_HEADS)],
                              axis=0).astype(BF16) for g in chunks]
    att = [jnp.where(causal, _bdot(q_rows[g], k_in[sl[g]], NT_DIMS), 0.0) for g in chunks]
    o_rows = [_bdot(att[g], v[sl[g]]) for g in chunks]
    kv = [jnp.where(sv == sk, _bdot(v[sl[g]], k_st[sl[g]], TN_DIMS), 0.0) for g in chunks]
    s = s_scr[...]
    for g in (reversed(chunks) if reverse else chunks):
        o = _bdot(q_in[sl[g]], s, NT_DIMS)
        for h in range(GLA_HEADS):
            o = o + jnp.where((vlane >> 6) == h, o_rows[g][h * C:(h + 1) * C], 0.0)
        o_ref[sl[g], :] = o
        s = s * dn[g] + kv[g]
    s_scr[...] = s

    @pl.when(n == nsteps - 1)
    def _():
        sfin_ref[0] = s_scr[...]


def gla_scan_dir(pbg, aup, ab, s0, B, T, d):
    nc = T // CHUNK
    group = min(GLA_GROUP, nc)
    nsteps = nc // group
    rows = group * CHUNK
    blk = (lambda b, n: b * nsteps + nsteps - 1 - n) if d else (lambda b, n: b * nsteps + n)
    kern = functools.partial(_gla_kernel, nsteps=nsteps, group=group, reverse=bool(d))
    return pl.pallas_call(
        kern,
        out_shape=[jax.ShapeDtypeStruct((B * T, RW), F32),
                   jax.ShapeDtypeStruct((B, RW, GLA_KP), F32)],
        grid=(B, nsteps),
        in_specs=[pl.BlockSpec((rows, PBG_W), lambda b, n: (blk(b, n), 0)),
                  pl.BlockSpec((1, LANE, GLA_KP), lambda b, n: (d, 0, 0)),
                  pl.BlockSpec((1, 1, GLA_KP), lambda b, n: (d, 0, 0)),
                  pl.BlockSpec((1, RW, GLA_KP), lambda b, n: (b, 0, 0))],
        out_specs=[pl.BlockSpec((rows, RW), lambda b, n: (blk(b, n), 0)),
                   pl.BlockSpec((1, RW, GLA_KP), lambda b, n: (b, 0, 0))],
        scratch_shapes=[pltpu.VMEM((RW, GLA_KP), F32)],
        compiler_params=_cp(("arbitrary", "arbitrary")),
        name="gla_scan_bwd" if d else "gla_scan_fwd",
    )(pbg, aup, ab, s0)


def gla_scan(pbg, aup, ab, s0, B, T):
    o0, f0 = gla_scan_dir(pbg, aup, ab, s0[0], B, T, 0)
    o1, f1 = gla_scan_dir(pbg, aup, ab, s0[1], B, T, 1)
    return (o0, o1), (f0, f1)


S5_PITCH = 68


def _s5_kernel(u_ref, a_ref, bbd_ref, cbd_ref, s0_ref, y_ref, sfin_ref, x_scr, st_scr, *, nc, nb):
    d = pl.program_id(0)
    n = pl.program_id(1)
    C = CHUNK
    half = S5_LANES // 2

    @pl.when(n == 0)
    def _():
        st_scr[...] = s0_ref[0]

    nslab = S5_LANES // LANE
    bbd = bbd_ref[0]
    for b in range(nb):
        bu = _bdot(u_ref[b], bbd)
        for j in range(nslab):
            x_scr[j, b * S5_PITCH:b * S5_PITCH + C, :] = bu[:, j * LANE:(j + 1) * LANE]
    a_re = a_ref[0, :, 0:half]
    a_im = a_ref[0, :, half:S5_LANES]

    def step(i, carry):
        re, im = carry
        t = i + d * (C - 1 - 2 * i)
        rows = pl.ds(t, nb, stride=S5_PITCH)
        bu = jnp.concatenate([x_scr[j, rows, :] for j in range(nslab)], axis=1)
        nre = a_re * re - a_im * im + bu[:, 0:half]
        nim = a_re * im + a_im * re + bu[:, half:S5_LANES]
        for j in range(nslab // 2):
            x_scr[j, rows, :] = nre[:, j * LANE:(j + 1) * LANE]
            x_scr[nslab // 2 + j, rows, :] = nim[:, j * LANE:(j + 1) * LANE]
        return nre, nim

    st = st_scr[...]
    re, im = lax.fori_loop(0, C, step, (st[:, 0:half], st[:, half:S5_LANES]), unroll=4)
    st_scr[...] = jnp.concatenate([re, im], axis=1)
    cbd = cbd_ref[0]
    for b in range(nb):
        xs = jnp.concatenate([x_scr[j, b * S5_PITCH:b * S5_PITCH + C, :] for j in range(nslab)], axis=1)
        y_ref[0, b] = _bdot(xs, cbd)

    @pl.when(n == nc - 1)
    def _():
        sfin_ref[0] = st_scr[...]


def s5_scan(u3, a_bar, bbd, cbd, s0):
    B, T, _ = u3.shape
    nc = T // CHUNK
    chunk = lambda d, n: n + d * (nc - 1 - 2 * n)
    kern = functools.partial(_s5_kernel, nc=nc, nb=B)
    return pl.pallas_call(
        kern,
        out_shape=[jax.ShapeDtypeStruct((2, B, T, S5_W), F32),
                   jax.ShapeDtypeStruct((2, B, S5_LANES), F32)],
        grid=(2, nc),
        in_specs=[pl.BlockSpec((B, CHUNK, S5_W), lambda d, n: (0, chunk(d, n), 0)),
                  pl.BlockSpec((1, 1, S5_LANES), lambda d, n: (d, 0, 0)),
                  pl.BlockSpec((1, S5_W, S5_LANES), lambda d, n: (d, 0, 0)),
                  pl.BlockSpec((1, S5_LANES, S5_W), lambda d, n: (d, 0, 0)),
                  pl.BlockSpec((1, B, S5_LANES), lambda d, n: (d, 0, 0))],
        out_specs=[pl.BlockSpec((1, B, CHUNK, S5_W), lambda d, n: (d, 0, chunk(d, n), 0)),
                   pl.BlockSpec((1, B, S5_LANES), lambda d, n: (d, 0, 0))],
        scratch_shapes=[pltpu.VMEM((S5_LANES // LANE, B * S5_PITCH, LANE), F32),
                        pltpu.VMEM((B, S5_LANES), F32)],
        compiler_params=_cp(("arbitrary", "arbitrary")),
        name="s5_scan",
    )(u3, a_bar, bbd, cbd, s0)


ROWT = 4
U32 = jnp.uint32


def _pack_bf16_pair(a, b):
    au = lax.bitcast_convert_type(a.astype(BF16).astype(F32), U32)
    bu = lax.bitcast_convert_type(b.astype(BF16).astype(F32), U32)
    return (au >> 16) | bu


def _unpack_bf16_pair(w):
    return (lax.bitcast_convert_type(w << 16, F32), lax.bitcast_convert_type(w & jnp.uint32(0xFFFF0000), F32))


def _tile_rows_store(ref, val):
    tm = val.shape[0]
    assert val.shape[1] == 2 * ROWT * LANE
    for s in range(ROWT):
        ref[pl.ds(s, tm, stride=ROWT), :] = _pack_bf16_pair(val[:, 2 * s * LANE:(2 * s + 1) * LANE],
                                                            val[:, (2 * s + 1) * LANE:(2 * s + 2) * LANE])


def _tile_rows_load(ref, tm, base=0):
    parts = []
    for s in range(ROWT):
        parts.extend(_unpack_bf16_pair(ref[pl.ds(base + s, tm, stride=ROWT), :]))
    return jnp.concatenate(parts, axis=1)


def _mix_out_kernel(y0_ref, y1_ref, gb_ref, o0_ref, o1_ref, gg_ref, yd_ref, u_ref, x_ref, mod_ref,
                    hb_ref, lng_ref, lnb_ref, gng_ref, s5d_ref, gluw_ref, glub_ref, wout_ref, l1g_ref, l1b_ref,
                    *rest, alpha):
    x1_ref, h2_ref, h2t_ref = rest[-3:]
    hb = hb_ref[...]
    inv = 1.0 / HEAD
    y = y0_ref[...] + y1_ref[...]
    yc = y - _dot_split_rhs(y, hb) * inv
    var = _dot_split_rhs(yc * yc, hb) * inv
    gn = yc * lax.rsqrt(var + RWKV_GN_EPS) * lng_ref[...] + lnb_ref[...]
    y_a = (gn + gb_ref[:, RW:2 * RW]) * gb_ref[:, 0:RW]
    o = o0_ref[...] + o1_ref[...]
    o = o * lax.rsqrt(_dot_split_rhs(o * o, hb) * inv + 1e-6) * gng_ref[...]
    y_b = o * _silu(gg_ref[...])
    c = s5d_ref[...] * u_ref[...] + yd_ref[0] + yd_ref[1]
    c = 0.5 * c * (1.0 + jnp.tanh(math.sqrt(2.0 / math.pi) * (c + 0.044715 * (c * c * c))))
    y_c = c * _sigmoid(_bdot(c, gluw_ref[...]) + glub_ref[...])
    y_mix = (_bdot(y_a, wout_ref[0:RW, :]) + _bdot(y_b, wout_ref[RW:2 * RW, :])
             + _bdot(y_c, wout_ref[2 * RW:2 * RW + S5_W, :]))
    x1 = _layer_norm(alpha * x_ref[...] + mod_ref[0, 2:3, :] * y_mix) * l1g_ref[...] + l1b_ref[...]
    x1_ref[...] = x1
    h2 = _layer_norm(x1) * (1.0 + mod_ref[0, 4:5, :]) + mod_ref[0, 3:4, :]
    h2_ref[...] = h2
    _tile_rows_store(h2t_ref, h2)


def mix_out(y01, gb, o01, gg, yd, u, stream, mod3, wts, row_of_tile, tm, alpha, out_rows=None, out_row0=0, into=None):
    x2, row0, N = stream
    D = x2.shape[1]
    out_rows = N if out_rows is None else out_rows
    assert row0 % tm == 0 and out_row0 % tm == 0 and N % tm == 0
    t0, o0 = row0 // tm, out_row0 // tm
    full = lambda a: pl.BlockSpec(a.shape, lambda i: (0,) * a.ndim)
    kern = functools.partial(_mix_out_kernel, alpha=alpha)
    ins = [y01[0], y01[1], gb, o01[0], o01[1], gg, yd, u, x2, mod3, *wts]
    extra = [] if into is None else list(into)
    return pl.pallas_call(
        kern,
        out_shape=[jax.ShapeDtypeStruct((out_rows, D), F32),
                   jax.ShapeDtypeStruct((out_rows, D), F32),
                   jax.ShapeDtypeStruct((out_rows * ROWT, LANE), U32)],
        grid=(N // tm,),
        in_specs=[pl.BlockSpec((tm, RW), lambda i: (i, 0)),
                  pl.BlockSpec((tm, RW), lambda i: (i, 0)),
                  pl.BlockSpec((tm, 2 * RW), lambda i: (i, 0)),
                  pl.BlockSpec((tm, RW), lambda i: (i, 0)),
                  pl.BlockSpec((tm, RW), lambda i: (i, 0)),
                  pl.BlockSpec((tm, RW), lambda i: (i, 0)),
                  pl.BlockSpec((2, tm, S5_W), lambda i: (0, i, 0)),
                  pl.BlockSpec((tm, S5_W), lambda i: (i, 0)),
                  pl.BlockSpec((tm, D), lambda i: (i + t0, 0)),
                  pl.BlockSpec((1, 6, D), lambda i: (row_of_tile(i), 0, 0))] + [full(a) for a in wts]
                 + [pl.BlockSpec(memory_space=pl.ANY)] * len(extra),
        out_specs=[pl.BlockSpec((tm, D), lambda i: (i + o0, 0)),
                   pl.BlockSpec((tm, D), lambda i: (i + o0, 0)),
                   pl.BlockSpec((tm * ROWT, LANE), lambda i: (i + o0, 0))],
        input_output_aliases={len(ins) + k: k for k in range(len(extra))},
        compiler_params=_cp(("arbitrary",)),
        name="mix_out",
    )(*ins, *extra)


def _first_max(x, idx, big):
    m = jnp.max(x, axis=0, keepdims=True)
    first = jnp.min(jnp.where(x == m, idx, big), axis=0, keepdims=True)
    return m, first


def _router_kernel(h_ref, rwt_ref, bias_ref, e_ref, w_ref):
    tm = h_ref.shape[0]
    gsz = N_EXPERTS // N_EGROUPS
    ninf = -jnp.inf
    s = _sigmoid(_dot33(rwt_ref[...], h_ref[...], NT_DIMS))
    ssel = s + bias_ref[:, 0:1]
    gi = lax.broadcasted_iota(I32, (gsz, tm), 0)
    gscore = []
    for g in range(N_EGROUPS):
        xg = ssel[g * gsz:(g + 1) * gsz, :]
        m1, i1 = _first_max(xg, gi, gsz)
        m2 = jnp.max(jnp.where(gi == i1, ninf, xg), axis=0, keepdims=True)
        gscore.append(m1 + m2)
    cur = jnp.concatenate(gscore, axis=0)
    gidx = lax.broadcasted_iota(I32, (N_EGROUPS, tm), 0)
    picked = jnp.zeros((N_EGROUPS, tm), F32)
    for _ in range(TOPK_GROUPS):
        _, first = _first_max(cur, gidx, N_EGROUPS)
        hit = gidx == first
        picked = jnp.where(hit, 1.0, picked)
        cur = jnp.where(hit, ninf, cur)
    x = jnp.concatenate(
        [jnp.where(picked[g:g + 1, :] > 0.5, ssel[g * gsz:(g + 1) * gsz, :], ninf) for g in range(N_EGROUPS)], axis=0)
    ei = lax.broadcasted_iota(I32, (N_EXPERTS, tm), 0)
    idxs, ws = [], []
    for _ in range(TOP_K):
        _, first = _first_max(x, ei, N_EXPERTS)
        hit = ei == first
        idxs.append(first)
        ws.append(jnp.sum(jnp.where(hit, s, 0.0), axis=0, keepdims=True))
        x = jnp.where(hit, ninf, x)
    w = jnp.concatenate(ws, axis=0)
    e_ref[...] = jnp.concatenate(idxs, axis=0)
    w_ref[...] = w / jnp.sum(w, axis=0, keepdims=True) * ROUTE_SCALE


def moe_router(h2, rwt, bias_b, tm):
    N, D = h2.shape
    return pl.pallas_call(
        _router_kernel,
        out_shape=[jax.ShapeDtypeStruct((TOP_K, N), I32), jax.ShapeDtypeStruct((TOP_K, N), F32)],
        grid=(N // tm,),
        in_specs=[pl.BlockSpec((tm, D), lambda i: (i, 0)),
                  pl.BlockSpec((N_EXPERTS, D), lambda i: (0, 0)),
                  pl.BlockSpec((N_EXPERTS, LANE), lambda i: (0, 0))],
        out_specs=[pl.BlockSpec((TOP_K, tm), lambda i: (0, i)),
                   pl.BlockSpec((TOP_K, tm), lambda i: (0, i))],
        compiler_params=_cp(("arbitrary",)),
        name="moe_router",
    )(h2, rwt, bias_b)


def _moe_count_kernel(e_ref, pstart_ref, plan_ref, blk_ref, cnt_scr, *, nt, nbp):
    i = pl.program_id(0)
    tp = e_ref.shape[1]

    @pl.when(i == 0)
    def _():
        cnt_scr[...] = jnp.zeros_like(cnt_scr)

    ei = lax.broadcasted_iota(I32, (N_EXPERTS, tp), 0)
    acc = jnp.zeros((N_EXPERTS, 1), F32)
    for k in range(TOP_K):
        acc = acc + jnp.sum(jnp.where(ei == e_ref[k:k + 1, :], 1.0, 0.0), axis=1, keepdims=True)
    cnt_scr[...] = cnt_scr[...] + acc

    @pl.when(i == nt - 1)
    def _():
        cnt = cnt_scr[...].astype(I32)
        padded = (cnt + (MOE_BLK - 1)) & (-MOE_BLK)
        r = lax.broadcasted_iota(I32, (N_EXPERTS, N_EXPERTS), 0)
        c = lax.broadcasted_iota(I32, (N_EXPERTS, N_EXPERTS), 1)
        tri = jnp.where(c <= r, 1.0, 0.0).astype(BF16)
        padded_b = jnp.broadcast_to(padded.astype(F32), (N_EXPERTS, LANE))
        p_end = _dot_exact_lhs(tri, padded_b)
        pstart = p_end - padded_b
        pstart_ref[...] = pstart.astype(I32)
        diag = r == c
        ps_row = jnp.sum(jnp.where(diag, pstart[:, 0:1], 0.0), axis=0, keepdims=True)
        cnt_row = jnp.sum(jnp.where(diag, cnt_scr[...], 0.0), axis=0, keepdims=True)
        plan_ref[...] = jnp.concatenate([ps_row, cnt_row, jnp.zeros((SUB - 2, N_EXPERTS), F32)], axis=0).astype(I32)
        lim =(lax.broadcasted_iota(I32, (N_EXPERTS, nbp), 1) * MOE_BLK).astype(F32)
        be = jnp.sum(jnp.where(p_end[:, 0:1] <= lim, 1.0, 0.0), axis=0, keepdims=True)
        be = jnp.minimum(be, N_EXPERTS - 1.0)
        nused = jnp.max(p_end[:, 0:1], axis=0, keepdims=True) * (1.0 / MOE_BLK)
        blk_ref[...] = jnp.concatenate([jnp.broadcast_to(be, (SUB // 2, nbp)),
                                        jnp.broadcast_to(nused, (SUB // 2, nbp))], axis=0).astype(I32)


def moe_counts(eidx, tp, nbp):
    N = eidx.shape[1]
    nt = N // tp
    kern = functools.partial(_moe_count_kernel, nt=nt, nbp=nbp)
    return pl.pallas_call(
        kern,
        out_shape=[jax.ShapeDtypeStruct((N_EXPERTS, LANE), I32),
                   jax.ShapeDtypeStruct((SUB, N_EXPERTS), I32),
                   jax.ShapeDtypeStruct((SUB, nbp), I32)],
        grid=(nt,),
        in_specs=[pl.BlockSpec((TOP_K, tp), lambda i: (0, i))],
        out_specs=[pl.BlockSpec((N_EXPERTS, LANE), lambda i: (0, 0)),
                   pl.BlockSpec((SUB, N_EXPERTS), lambda i: (0, 0)),
                   pl.BlockSpec((SUB, nbp), lambda i: (0, 0))],
        scratch_shapes=[pltpu.VMEM((N_EXPERTS, 1), F32)],
        compiler_params=_cp(("arbitrary",)),
        name="moe_counts",
    )(eidx)


def _moe_dest_kernel(e_ref, pstart_ref, dest_ref, base_scr):
    i = pl.program_id(0)
    tp = e_ref.shape[1]

    @pl.when(i == 0)
    def _():
        base_scr[...] = pstart_ref[:, 0:1].astype(F32)

    ei = lax.broadcasted_iota(I32, (N_EXPERTS, tp), 0)
    r = lax.broadcasted_iota(I32, (tp, tp), 0)
    c = lax.broadcasted_iota(I32, (tp, tp), 1)
    tri = jnp.where(r <= c, 1.0, 0.0).astype(BF16)
    base = base_scr[...]
    rows = []
    for k in range(TOP_K):
        hit = ei == e_ref[k:k + 1, :]
        oh = jnp.where(hit, 1.0, 0.0)
        cum = _bdot(oh, tri)
        rows.append(jnp.sum(jnp.where(hit, cum - 1.0 + base, 0.0), axis=0, keepdims=True))
        base = base + cum[:, tp - 1:tp]
    base_scr[...] = base
    dest_ref[...] = jnp.concatenate(rows, axis=0).astype(I32)


def moe_dest(eidx, pstart, tp):
    N = eidx.shape[1]
    return pl.pallas_call(
        _moe_dest_kernel,
        out_shape=jax.ShapeDtypeStruct((TOP_K, N), I32),
        grid=(N // tp,),
        in_specs=[pl.BlockSpec((TOP_K, tp), lambda i: (0, i)),
                  pl.BlockSpec((N_EXPERTS, LANE), lambda i: (0, 0))],
        out_specs=pl.BlockSpec((TOP_K, tp), lambda i: (0, i)),
        scratch_shapes=[pltpu.VMEM((N_EXPERTS, 1), F32)],
        compiler_params=_cp(("arbitrary",)),
        name="moe_dest",
    )(eidx, pstart)


def _tile_at(ref, token):
    return ref.at[pl.ds(pl.multiple_of(token * ROWT, ROWT), ROWT), :]


SCAT_RING = 3


def _moe_scatter_kernel(dest_ref, h_ref, xs_hbm, ring, sem, *, nt):
    s = pl.program_id(0)
    tp = dest_ref.shape[1]
    slot = lax.rem(s, SCAT_RING)
    total = xs_hbm.at[pl.ds(0, tp * TOP_K * ROWT), :]

    def slot_wait(sl):
        pltpu.make_async_copy(total, total, sem.at[sl]).wait()

    @pl.when(s >= SCAT_RING)
    def _():
        slot_wait(slot)

    ring[slot] = h_ref[...]

    def issue(j, carry):
        for k in range(TOP_K):
            pltpu.make_async_copy(_tile_at(ring.at[slot], j), _tile_at(xs_hbm, dest_ref[k, j]),
                                  sem.at[slot]).start(priority=k % 2)
        return carry

    lax.fori_loop(0, tp, issue, 0)

    @pl.when(s == nt - 1)
    def _():
        for back in range(SCAT_RING - 1, -1, -1):
            if nt - 1 - back >= 0:
                slot_wait((nt - 1 - back) % SCAT_RING)


def moe_scatter(dest, h2t, n_slots, tp):
    N = dest.shape[1]
    nt = N // tp
    return pl.pallas_call(
        functools.partial(_moe_scatter_kernel, nt=nt),
        out_shape=jax.ShapeDtypeStruct((n_slots * ROWT, LANE), U32),
        grid=(nt,),
        in_specs=[pl.BlockSpec((TOP_K, tp), lambda i: (0, i), memory_space=pltpu.SMEM),
                  pl.BlockSpec((tp * ROWT, LANE), lambda i: (i, 0))],
        out_specs=pl.BlockSpec(memory_space=pl.ANY),
        scratch_shapes=[pltpu.VMEM((SCAT_RING, tp * ROWT, LANE), U32), pltpu.SemaphoreType.DMA((SCAT_RING,))],
        compiler_params=_cp(("arbitrary",)),
        name="moe_scatter",
    )(dest, h2t)


PAD_BITS = tuple(1 << b for b in reversed(range(MOE_BLK.bit_length() - 1)))


def _moe_padfill_kernel(ps_ref, xs_in, xs_hbm, zero_scr, sem):
    del xs_in
    zero_scr[...] = jnp.zeros_like(zero_scr)

    def pad_copies(e, wait):
        cnt = ps_ref[1, e]
        npad = ((cnt + (MOE_BLK - 1)) & (-MOE_BLK)) - cnt
        off = ps_ref[0, e] + cnt
        for bit in PAD_BITS:
            @pl.when((npad & bit) != 0)
            def _():
                cp = pltpu.make_async_copy(zero_scr.at[pl.ds(0, bit * ROWT), :],
                                           xs_hbm.at[pl.ds(pl.multiple_of(off * ROWT, ROWT), bit * ROWT), :], sem)
                if wait:
                    cp.wait()
                else:
                    cp.start()
            off = off + (npad & bit)

    def issue(e, carry):
        pad_copies(e, False)
        return carry

    def drain(e, carry):
        pad_copies(e, True)
        return carry

    lax.fori_loop(0, N_EXPERTS, issue, 0)
    lax.fori_loop(0, N_EXPERTS, drain, 0)


def moe_padfill(pstart, xs):
    return pl.pallas_call(
        _moe_padfill_kernel,
        out_shape=jax.ShapeDtypeStruct(xs.shape, xs.dtype),
        grid=(1,),
        in_specs=[pl.BlockSpec(memory_space=pltpu.SMEM),
                  pl.BlockSpec(memory_space=pl.ANY)],
        out_specs=pl.BlockSpec(memory_space=pl.ANY),
        scratch_shapes=[pltpu.VMEM((PAD_BITS[0] * ROWT, LANE), U32), pltpu.SemaphoreType.DMA(())],
        input_output_aliases={1: 0},
        compiler_params=_cp(("arbitrary",)),
        name="moe_padfill",
    )(pstart, xs)


EXP_NX = 8
EXP_PF = 4
EXP_NY = 4


def _experts_kernel(plan_ref, w13_ref, w2_ref, xs_hbm, y_hbm, w13_bf, w2_bf, x_buf, y_buf, sem_in, sem_out):
    e = pl.program_id(0)
    ne = pl.num_programs(0)
    ff = w2_bf.shape[0]
    rows = MOE_BLK * ROWT
    shift = MOE_BLK.bit_length() - 1
    blocks_of = lambda c: lax.shift_right_logical(c + (MOE_BLK - 1), shift)
    nblk = blocks_of(plan_ref[1, e])
    first = lax.shift_right_logical(plan_ref[0, e], shift)
    nused = lax.shift_right_logical(plan_ref[0, ne - 1], shift) + blocks_of(plan_ref[1, ne - 1])

    def window(ref, g):
        return ref.at[pl.ds(pl.multiple_of(g * rows, rows), rows), :]

    def x_copy(g):
        return pltpu.make_async_copy(window(xs_hbm, g), x_buf.at[g & (EXP_NX - 1)], sem_in.at[g & (EXP_NX - 1)])

    def y_copy(g):
        return pltpu.make_async_copy(y_buf.at[g & (EXP_NY - 1)], window(y_hbm, g), sem_out.at[g & (EXP_NY - 1)])

    @pl.when(e == 0)
    def _():
        for g in range(EXP_PF):
            @pl.when(g < nused)
            def _():
                x_copy(g).start()

    def process(gs):
        for g in gs:
            x_copy(g).wait()

            @pl.when(g + EXP_PF < nused)
            def _():
                x_copy(g + EXP_PF).start()

            @pl.when(g >= EXP_NY)
            def _():
                y_copy(g - EXP_NY).wait()

        xs = [_tile_rows_load(x_buf.at[g & (EXP_NX - 1)], MOE_BLK).astype(BF16) for g in gs]
        hs = [jnp.dot(x, w13_bf[...], preferred_element_type=F32) for x in xs]
        acts = [(_silu(h[:, 0:ff]) * h[:, ff:2 * ff]).astype(BF16) for h in hs]
        ys = [jnp.dot(a, w2_bf[...], preferred_element_type=F32) for a in acts]
        for g, y in zip(gs, ys):
            _tile_rows_store(y_buf.at[g & (EXP_NY - 1)], y)
            y_copy(g).start()

    @pl.when(nblk > 0)
    def _():
        w13_bf[...] = w13_ref[0, 0].astype(BF16)
        w2_bf[...] = w2_ref[0, 0].astype(BF16)
        npair = lax.shift_right_logical(nblk, 1)

        def body(j, carry):
            g = first + 2 * j
            process([g, g + 1])
            return carry

        lax.fori_loop(0, npair, body, 0)

        @pl.when((nblk & 1) == 1)
        def _():
            process([first + nblk - 1])

    @pl.when(e == ne - 1)
    def _():
        for back in range(EXP_NY, 0, -1):
            @pl.when(nused >= back)
            def _():
                y_copy(nused - back).wait()


def moe_experts(plan, xs, w13, w2, layer):
    _, E, D, F2 = w13.shape
    rows = MOE_BLK * ROWT
    grid_spec = pltpu.PrefetchScalarGridSpec(
        num_scalar_prefetch=1,
        grid=(E,),
        in_specs=[pl.BlockSpec((1, 1, D, F2), lambda e, p: (layer, e, 0, 0)),
                  pl.BlockSpec((1, 1, F2 // 2, D), lambda e, p: (layer, e, 0, 0)),
                  pl.BlockSpec(memory_space=pl.ANY)],
        out_specs=pl.BlockSpec(memory_space=pl.ANY),
        scratch_shapes=[pltpu.VMEM((D, F2), BF16), pltpu.VMEM((F2 // 2, D), BF16),
                        pltpu.VMEM((EXP_NX, rows, LANE), U32), pltpu.VMEM((EXP_NY, rows, LANE), U32),
                        pltpu.SemaphoreType.DMA((EXP_NX,)), pltpu.SemaphoreType.DMA((EXP_NY,))])
    return pl.pallas_call(
        _experts_kernel,
        out_shape=jax.ShapeDtypeStruct(xs.shape, U32),
        grid_spec=grid_spec,
        compiler_params=_cp(("arbitrary",)),
        name="moe_experts",
    )(plan, w13, w2, xs)


def _moe_combine_kernel(dest_ref, wt_ref, h_ref, x_ref, mod_ref, s13_ref, s2_ref, l2g_ref, l2b_ref, y_hbm,
                        o_ref, g_scr, sem, *, alpha, nt):
    s = pl.program_id(0)
    tm = h_ref.shape[0]
    ff = s2_ref.shape[0]
    rows = tm * ROWT
    slot_g = s % 2
    slot_c = 1 - slot_g

    def slot_copy(slot):
        return pltpu.make_async_copy(y_hbm.at[pl.ds(0, TOP_K * rows), :], g_scr.at[slot], sem.at[slot])

    @pl.when(s == 0)
    def _():
        g_scr[1] = jnp.zeros((TOP_K * rows, LANE), U32)

    @pl.when(s > 0)
    def _():
        slot_copy(slot_c).wait()

    for j in range(tm):
        for k in range(TOP_K):
            pltpu.make_async_copy(_tile_at(y_hbm, dest_ref[k, j]),
                                  g_scr.at[slot_g, pl.ds(k * rows + j * ROWT, ROWT), :],
                                  sem.at[slot_g]).start(priority=k % 2)
    hs = _bdot(h_ref[...], s13_ref[...])
    f = _bdot(_silu(hs[:, 0:ff]) * hs[:, ff:2 * ff], s2_ref[...])
    for k in range(TOP_K):
        f = f + wt_ref[:, k:k + 1] * _tile_rows_load(g_scr.at[slot_c], tm, base=k * rows)
    o_ref[...] = (_layer_norm(alpha * x_ref[...] + mod_ref[0, 5:6, :] * f) * l2g_ref[...] + l2b_ref[...])

    @pl.when(s == nt)
    def _():
        slot_copy(slot_g).wait()


def moe_combine(dest, wt_t, h2, x1, mod3, wts, y, row_of_tile, tm, alpha):
    N, D = h2.shape
    nt = N // tm
    full = lambda a: pl.BlockSpec(a.shape, lambda s: (0,) * a.ndim)
    kern = functools.partial(_moe_combine_kernel, alpha=alpha, nt=nt)
    nxt = lambda s: jnp.minimum(s, nt - 1)
    cur = lambda s: jnp.maximum(s - 1, 0)
    return pl.pallas_call(
        kern,
        out_shape=jax.ShapeDtypeStruct((N, D), F32),
        grid=(nt + 1,),
        in_specs=[pl.BlockSpec((TOP_K, tm), lambda s: (0, nxt(s)), memory_space=pltpu.SMEM),
                  pl.BlockSpec((tm, TOP_K), lambda s: (cur(s), 0)),
                  pl.BlockSpec((tm, D), lambda s: (cur(s), 0)),
                  pl.BlockSpec((tm, D), lambda s: (cur(s), 0)),
                  pl.BlockSpec((1, 6, D), lambda s: (row_of_tile(cur(s)), 0, 0))]
                 + [full(a) for a in wts] + [pl.BlockSpec(memory_space=pl.ANY)],
        out_specs=pl.BlockSpec((tm, D), lambda s: (cur(s), 0)),
        scratch_shapes=[pltpu.VMEM((2, TOP_K * tm * ROWT, LANE), U32), pltpu.SemaphoreType.DMA((2,))],
        compiler_params=_cp(("arbitrary",)),
        name="moe_combine",
    )(dest, wt_t, h2, x1, mod3, *wts, y)


def moe_ffn(h2, h2t, x1, mod3, mp, row_of_tile, tm, alpha):
    N = h2.shape[0]
    nb_total = (N * TOP_K + MOE_BLK - 1) // MOE_BLK + N_EXPERTS
    nbp = ((nb_total + LANE - 1) // LANE) * LANE
    tp = _tile(N, 256)
    eidx, wts = moe_router(h2, mp['rwt'], mp['bias'], tp)
    pstart, plan, blk = moe_counts(eidx, tp, nbp)
    dest = moe_dest(eidx, pstart, tp)
    xs = moe_scatter(dest, h2t, nb_total * MOE_BLK, tp)
    xs = moe_padfill(plan, xs)
    y = moe_experts(plan, xs, mp['w13'], mp['w2'], mp['layer'])
    return moe_combine(dest, wts.T, h2, x1, mod3, mp['comb'], y, row_of_tile, tm, alpha)


def token_mixers(stream, B, T, W, mod3, row_of_tile, lp, states, tm):
    pa, pbg, gg, u = inproj(stream, mod3, lp['w_in'], row_of_tile, tm)
    pac = short_conv(pa.reshape(B, T, PA_W), lp['taps'], W).reshape(B * T, PA_W)
    ss, sd, gb = rwkv_prep(pac, lp['prep'], tm)
    s_rwkv, s_gla, s_s5 = states
    y2, f_rwkv = rwkv_scan(ss, sd, s_rwkv, B, T)
    o2, f_gla = gla_scan(pbg, lp['gla_aup'], lp['gla_ab'], s_gla, B, T)
    yd, f_s5 = s5_scan(u.reshape(B, T, S5_W), lp['s5_a'], lp['s5_bbd'], lp['s5_cbd'], s_s5)
    return (y2, gb, o2, gg, yd.reshape(2, B * T, S5_W), u), (f_rwkv, f_gla, f_s5)


def zero_states(B):
    return ((jnp.zeros((B, RW // LANE, LANE, LANE), F32),) * 2,
            (jnp.zeros((B, RW, GLA_KP), F32),) * 2,
            jnp.zeros((2, B, S5_LANES), F32))


def _inproj_columns():
    r_cols = 3 * RW + 2 * DECAY_RANK + 2 * ICLR_RANK + GATE_RANK
    kd = GLA_HEADS * GLA_K
    gq, gk, gv = r_cols, r_cols + kd, r_cols + 2 * kd
    gg = gv + RW
    gad = gg + RW
    pc = gad + 2 * GLA_RANK
    z = lambda n: [-1] * n
    cols = list(range(0, r_cols)) + z(PA_W - r_cols)
    cols += list(range(gq, gq + kd)) + z(256 - kd)
    cols += list(range(gk, gk + kd)) + z(256 - kd)
    cols += list(range(gv, gv + RW))
    cols += list(range(gad, gad + 2 * GLA_RANK)) + z(LANE - 2 * GLA_RANK)
    cols += list(range(gg, gg + RW))
    cols += list(range(pc, pc + S5_W))
    assert len(cols) == NP_W
    return np.asarray(cols, np.int32)


def _head_block_ones():
    h = np.arange(RW) // HEAD
    return jnp.asarray(h[:, None] == h[None, :], BF16)


def _layer_params(l, p):
    D = p['w_in'].shape[1]
    cols = _inproj_columns()
    w_in = jnp.concatenate([p['w_in'][l], jnp.zeros((D, 1), F32)], axis=1)
    w_in = jnp.take(w_in, jnp.asarray(np.where(cols < 0, w_in.shape[1] - 1, cols)), axis=1).astype(BF16)
    taps = p['rwkv_conv'][l].reshape(9, -1)
    taps = jnp.concatenate([taps, jnp.zeros((9, PA_W - taps.shape[1]), F32)], axis=1)
    wup = jnp.zeros((LANE, 2 * RW), F32)
    aup = jnp.zeros((LANE, 2 * RW), F32)
    for d in range(2):
        wup = wup.at[d * DECAY_RANK:(d + 1) * DECAY_RANK, d * RW:(d + 1) * RW].set(p['rwkv_w_up'][l, d])
        o = 2 * DECAY_RANK + d * ICLR_RANK
        aup = aup.at[o:o + ICLR_RANK, d * RW:(d + 1) * RW].set(p['rwkv_a_up'][l, d])
    gup = jnp.zeros((LANE, RW), F32).at[0:GATE_RANK].set(p['rwkv_g_up'][l])
    row = lambda a: a.reshape(1, -1)
    prep = (wup, aup, gup, row(p['rwkv_w0'][l]), row(p['rwkv_a0'][l]), row(p['rwkv_k_k'][l]),
            row(p['rwkv_k_a'][l]), row(p['rwkv_r_k'][l]), _head_block_ones())
    kd = GLA_HEADS * GLA_K
    gla_aup = jnp.zeros((2, LANE, GLA_KP), F32)
    for d in range(2):
        gla_aup = gla_aup.at[d, d * GLA_RANK:(d + 1) * GLA_RANK, 0:kd].set(p['gla_a_up'][l, d])
    gla_ab = jnp.zeros((2, 1, GLA_KP), F32).at[:, 0, 0:kd].set(p['gla_a_bias'][l])
    lam_re, lam_im = p['s5_lam_re'][l], p['s5_lam_im'][l]
    dt = jnp.exp(p['s5_log_dt'][l])[:, :, None]
    zr, zi = lam_re[:, None, :] * dt, lam_im[:, None, :] * dt
    mag = jnp.exp(zr)
    ab_r, ab_i = mag * jnp.cos(zi), mag * jnp.sin(zi)
    den = (lam_re * lam_re + lam_im * lam_im)[:, None, :]
    f_r = ((ab_r - 1) * lam_re[:, None, :] + ab_i * lam_im[:, None, :]) / den
    f_i = (ab_i * lam_re[:, None, :] - (ab_r - 1) * lam_im[:, None, :]) / den
    b_re, b_im = p['s5_b_re'][l], p['s5_b_im'][l]
    bb_r = f_r[..., None] * b_re - f_i[..., None] * b_im
    bb_i = f_r[..., None] * b_im + f_i[..., None] * b_re
    eye_g = jnp.eye(S5_GROUPS, dtype=F32)
    half = S5_LANES // 2

    def in_blockdiag(bb):
        return jnp.einsum('dgpc,gh->dgchp', bb, eye_g).reshape(2, S5_W, half)

    def out_blockdiag(cc):
        return jnp.einsum('dgcp,gh->dgphc', cc, eye_g).reshape(2, half, S5_W)

    s5_bbd = jnp.concatenate([in_blockdiag(bb_r), in_blockdiag(bb_i)], axis=2).astype(BF16)
    s5_cbd = jnp.concatenate([out_blockdiag(p['s5_c_re'][l]), -out_blockdiag(p['s5_c_im'][l])], axis=1).astype(BF16)
    s5_a = jnp.concatenate([ab_r.reshape(2, 1, half), ab_i.reshape(2, 1, half)], axis=2)
    mix = (_head_block_ones(), row(p['rwkv_ln_g'][l]), row(p['rwkv_ln_b'][l]),
           row(jnp.tile(p['gla_norm_g'][l], GLA_HEADS)), row(p['s5_d'][l]),
           p['s5_glu_w'][l].astype(BF16), row(p['s5_glu_b'][l]), p['w_out'][l].astype(BF16),
           row(p['ln1_g'][l]), row(p['ln1_b'][l]))
    moe = dict(rwt=p['router_w'][l].T,
               bias=jnp.broadcast_to(p['router_bias'][l][:, None], (N_EXPERTS, LANE)),
               w13=p['exp_w13'], w2=p['exp_w2'], layer=l,
               comb=(p['sh_w13'][l].astype(BF16), p['sh_w2'][l].astype(BF16), row(p['ln2_g'][l]), row(p['ln2_b'][l])))
    return dict(w_in=w_in, taps=taps, prep=prep, gla_aup=gla_aup, gla_ab=gla_ab,
                s5_a=s5_a, s5_bbd=s5_bbd, s5_cbd=s5_cbd, mix=mix, moe=moe)


_ARG_NAMES = ('x', 'c', 'ctx', 'c_ctx', 'w_mod', 'b_mod', 'w_in', 'rwkv_conv', 'rwkv_w0', 'rwkv_w_up', 'rwkv_a0',
              'rwkv_a_up', 'rwkv_g_up', 'rwkv_k_k', 'rwkv_k_a', 'rwkv_r_k', 'rwkv_ln_g', 'rwkv_ln_b', 'gla_a_up',
              'gla_a_bias', 'gla_norm_g', 's5_lam_re', 's5_lam_im', 's5_log_dt', 's5_b_re', 's5_b_im', 's5_c_re',
              's5_c_im', 's5_d', 's5_glu_w', 's5_glu_b', 'w_out', 'ln1_g', 'ln1_b', 'router_w', 'router_bias',
              'exp_w13', 'exp_w2', 'sh_w13', 'sh_w2', 'ln2_g', 'ln2_b')


def _tile(n, pref):
    t = pref
    while n % t:
        t //= 2
    return t


def kernel(x, c, ctx, c_ctx, w_mod, b_mod, w_in, rwkv_conv, rwkv_w0, rwkv_w_up, rwkv_a0, rwkv_a_up, rwkv_g_up,
           rwkv_k_k, rwkv_k_a, rwkv_r_k, rwkv_ln_g, rwkv_ln_b, gla_a_up, gla_a_bias, gla_norm_g, s5_lam_re,
           s5_lam_im, s5_log_dt, s5_b_re, s5_b_im, s5_c_re, s5_c_im, s5_d, s5_glu_w, s5_glu_b, w_out, ln1_g,
           ln1_b, router_w, router_bias, exp_w13, exp_w2, sh_w13, sh_w2, ln2_g, ln2_b):
    p = dict(zip(_ARG_NAMES, (x, c, ctx, c_ctx, w_mod, b_mod, w_in, rwkv_conv, rwkv_w0, rwkv_w_up, rwkv_a0,
                              rwkv_a_up, rwkv_g_up, rwkv_k_k, rwkv_k_a, rwkv_r_k, rwkv_ln_g, rwkv_ln_b, gla_a_up,
                              gla_a_bias, gla_norm_g, s5_lam_re, s5_lam_im, s5_log_dt, s5_b_re, s5_b_im, s5_c_re,
                              s5_c_im, s5_d, s5_glu_w, s5_glu_b, w_out, ln1_g, ln1_b, router_w, router_bias,
                              exp_w13, exp_w2, sh_w13, sh_w2, ln2_g, ln2_b)))
    B, T, D = x.shape
    TC = ctx.shape[1]
    L = w_mod.shape[0]
    alpha = (2 * L) ** 0.25
    n_lat, n_ctx = B * T, B * TC
    R = ((B + 1 + SUB - 1) // SUB) * SUB
    cc = jnp.zeros((R, D), F32).at[0:B].set(c).at[B].set(c_ctx)
    mod = mod_table(cc, w_mod, b_mod)
    tm = _tile(T, 512)
    tmc = _tile(n_ctx, 512)
    tmm = min(_tile(T, 128), _tile(n_ctx, 128))
    lat_row = lambda i: (i * tm) // T
    ctx_row = lambda i: B
    lat = (x.reshape(n_lat, D), 0, n_lat)
    con = (ctx.reshape(n_ctx, D), 0, n_ctx)
    for l in range(L):
        last = l == L - 1
        lp = _layer_params(l, p)
        mod3 = mod[l].reshape(R, 6, D)
        outs_c, st_c = token_mixers(con, B, TC, TC, mod3, ctx_row, lp, zero_states(B), tmc)
        outs, _ = token_mixers(lat, B, T, GRID_W, mod3, lat_row, lp, st_c, tm)
        if last:
            x1, h2, h2t = mix_out(*outs, lat, mod3, lp['mix'], lat_row, tm, alpha)
            out = moe_ffn(h2, h2t, x1, mod3, lp['moe'], lambda i: (i * tmm) // T, tmm, alpha)
            lat = (out, 0, n_lat)
        else:
            n_all = n_ctx + n_lat
            bufs = mix_out(*outs_c, con, mod3, lp['mix'], ctx_row, tmc, alpha, out_rows=n_all)
            x1, h2, h2t = mix_out(*outs, lat, mod3, lp['mix'], lat_row, tm, alpha,
                                  out_rows=n_all, out_row0=n_ctx, into=bufs)
            row_all = lambda i: jnp.where(i * tmm < n_ctx, B, (i * tmm - n_ctx) // T)
            out = moe_ffn(h2, h2t, x1, mod3, lp['moe'], row_all, tmm, alpha)
            con, lat = (out, 0, n_ctx), (out, n_ctx, n_lat)
    out, row0, _ = lat
    return out[row0:row0 + n_lat].reshape(B, T, D)
```

```python
import functools
import math

import numpy as np
import jax
import jax.numpy as jnp
from jax import lax
from jax.experimental import pallas as pl
from jax.experimental.pallas import tpu as pltpu

F32 = jnp.float32
BF16 = jnp.bfloat16
I32 = jnp.int32

GRID_W = 64
RWKV_HEADS = 6
HEAD = 64
RW = RWKV_HEADS * HEAD
DECAY_RANK = 32
ICLR_RANK = 32
GATE_RANK = 64
RWKV_GN_EPS = 64e-5
GLA_HEADS = 6
GLA_K = 32
GLA_RANK = 16
GLA_TEMP = 16.0
S5_GROUPS = 16
S5_GROUP = 16
S5_STATE = 64
S5_W = S5_GROUPS * S5_GROUP
S5_LANES = 2 * S5_GROUPS * S5_STATE
N_EXPERTS = 256
TOP_K = 8
N_EGROUPS = 8
TOPK_GROUPS = 4
ROUTE_SCALE = 2.5
LN_EPS = 1e-6
CHUNK = 64
MOE_BLK = 256
LANE = 128
SUB = 8
VMEM_LIMIT = 56 * 1024 * 1024

PA_W = 3 * RW + 2 * LANE
PBG_W = 256 + 256 + RW + LANE
NP_W = PA_W + PBG_W + RW + S5_W


def _cp(sem):
    return pltpu.CompilerParams(dimension_semantics=sem, vmem_limit_bytes=VMEM_LIMIT)


def _sigmoid(x):
    return 1.0 / (1.0 + jnp.exp(-x))


def _silu(x):
    return x * _sigmoid(x)


def _bdot(a, b, dims=None):
    a = a.astype(BF16)
    b = b.astype(BF16)
    if dims is None:
        return jnp.dot(a, b, preferred_element_type=F32)
    return lax.dot_general(a, b, dims, preferred_element_type=F32)


def _split2(x):
    hi = x.astype(BF16)
    lo = (x - hi.astype(F32)).astype(BF16)
    return hi, lo


def _split3(x):
    hi = x.astype(BF16)
    r = x - hi.astype(F32)
    mid = r.astype(BF16)
    lo = (r - mid.astype(F32)).astype(BF16)
    return hi, mid, lo


NT_DIMS = (((1,), (1,)), ((), ()))
TN_DIMS = (((0,), (0,)), ((), ()))


def _dot33(a, b, dims=None):
    ah, al = _split2(a)
    bh, bl = _split2(b)
    return _bdot(ah, bh, dims) + (_bdot(ah, bl, dims) + _bdot(al, bh, dims))


def _dot_exact_lhs(m_exact, x, dims=None):
    h, m, l = _split3(x)
    return _bdot(m_exact, h, dims) + (_bdot(m_exact, m, dims) + _bdot(m_exact, l, dims))


def _dot_exact_rhs(x, m_exact, dims=None):
    h, m, l = _split3(x)
    return _bdot(h, m_exact, dims) + (_bdot(m, m_exact, dims) + _bdot(l, m_exact, dims))


def _dot_split_rhs(x, m_exact, dims=None):
    h, l = _split2(x)
    return _bdot(h, m_exact, dims) + _bdot(l, m_exact, dims)


def _layer_norm(x):
    mu = jnp.mean(x, axis=-1, keepdims=True)
    xc = x - mu
    var = jnp.mean(xc * xc, axis=-1, keepdims=True)
    return xc * lax.rsqrt(var + LN_EPS)


def _mod_kernel(c_ref, w_ref, b_ref, o_ref):
    s = _silu(c_ref[...])
    o_ref[0] = _dot33(s, w_ref[0]) + b_ref[0]


def mod_table(cc, w_mod, b_mod):
    L, D, D6 = w_mod.shape
    R = cc.shape[0]
    tn = 1536
    return pl.pallas_call(
        _mod_kernel,
        out_shape=jax.ShapeDtypeStruct((L, R, D6), F32),
        grid=(L, D6 // tn),
        in_specs=[pl.BlockSpec((R, D), lambda l, j: (0, 0)),
                  pl.BlockSpec((1, D, tn), lambda l, j: (l, 0, j)),
                  pl.BlockSpec((1, 1, tn), lambda l, j: (l, 0, j))],
        out_specs=pl.BlockSpec((1, R, tn), lambda l, j: (l, 0, j)),
        compiler_params=_cp(("arbitrary", "arbitrary")),
        name="mod_table",
    )(cc, w_mod, b_mod.reshape(L, 1, D6))


def _inproj_kernel(x_ref, mod_ref, w_ref, pa_ref, pbg_ref, gg_ref, u_ref):
    x = x_ref[...]
    h = _layer_norm(x) * (1.0 + mod_ref[0, 1:2, :]) + mod_ref[0, 0:1, :]
    hb = h.astype(BF16)
    o = 0
    for ref in (pa_ref, pbg_ref, gg_ref, u_ref):
        w = ref.shape[-1]
        ref[...] = jnp.dot(hb, w_ref[:, o:o + w], preferred_element_type=F32)
        o += w


def inproj(stream, mod3, w_bf, row_of_tile, tm):
    x2, row0, N = stream
    D = x2.shape[1]
    t0 = row0 // tm
    assert row0 % tm == 0 and N % tm == 0
    return pl.pallas_call(
        _inproj_kernel,
        out_shape=[jax.ShapeDtypeStruct((N, PA_W), F32),
                   jax.ShapeDtypeStruct((N, PBG_W), F32),
                   jax.ShapeDtypeStruct((N, RW), F32),
                   jax.ShapeDtypeStruct((N, S5_W), F32)],
        grid=(N // tm,),
        in_specs=[pl.BlockSpec((tm, D), lambda i: (i + t0, 0)),
                  pl.BlockSpec((1, 6, D), lambda i: (row_of_tile(i), 0, 0)),
                  pl.BlockSpec((D, NP_W), lambda i: (0, 0))],
        out_specs=[pl.BlockSpec((tm, PA_W), lambda i: (i, 0)),
                   pl.BlockSpec((tm, PBG_W), lambda i: (i, 0)),
                   pl.BlockSpec((tm, RW), lambda i: (i, 0)),
                   pl.BlockSpec((tm, S5_W), lambda i: (i, 0))],
        compiler_params=_cp(("arbitrary",)),
        name="inproj",
    )(x2, mod3, w_bf)


CONV_PAD = 72


def _conv_kernel(x_ref, taps_ref, o_ref, buf_ref, *, T, W, vertical):
    pad = CONV_PAD
    zeros = jnp.zeros((pad, LANE), F32)
    buf_ref[0:pad, :] = zeros
    buf_ref[pad + T:pad + T + pad, :] = zeros
    buf_ref[pad:pad + T, :] = x_ref[0]
    ch = min(T, 256)
    col = lax.broadcasted_iota(I32, (ch, LANE), 0) & (W - 1)
    left_ok = col >= 1
    right_ok = col <= W - 2
    for c in range(T // ch):
        base = pad + c * ch
        acc = jnp.zeros((ch, LANE), F32)
        for dr in ((0, 1, 2) if vertical else (1,)):
            for dc in range(3):
                off = (dr - 1) * W + (dc - 1)
                v = buf_ref[base + off:base + off + ch, :]
                if dc == 0:
                    v = jnp.where(left_ok, v, 0.0)
                elif dc == 2:
                    v = jnp.where(right_ok, v, 0.0)
                acc = acc + v * taps_ref[3 * dr + dc:3 * dr + dc + 1, :]
        o_ref[0, c * ch:(c + 1) * ch, :] = acc


def short_conv(pa3, taps9, W):
    B, T, C = pa3.shape
    vertical = T > W
    assert W & (W - 1) == 0 and (not vertical or W + 1 <= CONV_PAD)
    kern = functools.partial(_conv_kernel, T=T, W=W, vertical=vertical)
    return pl.pallas_call(
        kern,
        out_shape=jax.ShapeDtypeStruct((B, T, C), F32),
        grid=(B, C // LANE),
        in_specs=[pl.BlockSpec((1, T, LANE), lambda b, j: (b, 0, j)),
                  pl.BlockSpec((9, LANE), lambda b, j: (0, j))],
        out_specs=pl.BlockSpec((1, T, LANE), lambda b, j: (b, 0, j)),
        scratch_shapes=[pltpu.VMEM((T + 2 * CONV_PAD, LANE), F32)],
        compiler_params=_cp(("arbitrary", "arbitrary")),
        name="short_conv",
    )(pa3, taps9)


def _rwkv_prep_kernel(pa_ref, wup_ref, aup_ref, gup_ref, w0_ref, a0_ref, kk_ref, ka_ref, rk_ref, hb_ref,
                      ss_ref, sd_ref, gb_ref):
    r = pa_ref[:, 0:RW]
    k = pa_ref[:, RW:2 * RW]
    v = pa_ref[:, 2 * RW:3 * RW]
    wa = pa_ref[:, 3 * RW:3 * RW + LANE]
    gd = pa_ref[:, 3 * RW + LANE:3 * RW + 2 * LANE]
    z = w0_ref[...] + _bdot(jnp.tanh(wa), wup_ref[...])
    lw = -_sigmoid(z) * math.exp(-0.5)
    a = _sigmoid(a0_ref[...] + _bdot(wa, aup_ref[...]))
    g = _bdot(_sigmoid(gd), gup_ref[...])
    hb = hb_ref[...]
    kk = k * kk_ref[...]
    kk = kk * lax.rsqrt(_bdot(kk * kk, hb) + 1e-12)
    ka = ka_ref[...]
    ss_ref[:, 0:RW] = r
    ss_ref[:, RW:2 * RW] = v
    ss_ref[:, 2 * RW:3 * RW] = kk
    rk2 = jnp.zeros_like(r)
    for d in range(2):
        ad = a[:, d * RW:(d + 1) * RW]
        k2 = k * (1.0 + (ad - 1.0) * ka)
        sd_ref[d, :, 0:RW] = lw[:, d * RW:(d + 1) * RW]
        sd_ref[d, :, RW:2 * RW] = k2
        sd_ref[d, :, 2 * RW:3 * RW] = kk * ad
        rk2 = rk2 + r * k2
    bonus = _bdot(rk2 * rk_ref[...], hb) * v
    gb_ref[:, 0:RW] = g
    gb_ref[:, RW:2 * RW] = bonus


def rwkv_prep(pa2, wts, tm):
    N = pa2.shape[0]
    full = lambda a: pl.BlockSpec(a.shape, lambda i: (0,) * a.ndim)
    return pl.pallas_call(
        _rwkv_prep_kernel,
        out_shape=[jax.ShapeDtypeStruct((N, 3 * RW), F32),
                   jax.ShapeDtypeStruct((2, N, 3 * RW), F32),
                   jax.ShapeDtypeStruct((N, 2 * RW), F32)],
        grid=(N // tm,),
        in_specs=[pl.BlockSpec((tm, PA_W), lambda i: (i, 0))] + [full(a) for a in wts],
        out_specs=[pl.BlockSpec((tm, 3 * RW), lambda i: (i, 0)),
                   pl.BlockSpec((2, tm, 3 * RW), lambda i: (0, i, 0)),
                   pl.BlockSpec((tm, 2 * RW), lambda i: (i, 0))],
        compiler_params=_cp(("arbitrary",)),
        name="rwkv_prep",
    )(pa2, *wts)


RWKV_GROUP = 8


def _rwkv_scan_kernel(ss_ref, sd_ref, s0_ref, y_ref, sfin_ref, s_scr, *, nsteps, group, reverse):
    n = pl.program_id(1)
    C = CHUNK
    P = 2 * C
    npair = RW // LANE

    @pl.when(n == 0)
    def _():
        s_scr[...] = s0_ref[0]

    row = lax.broadcasted_iota(I32, (P, P), 0)
    col = lax.broadcasted_iota(I32, (P, P), 1)
    same = (row >> 6) == (col >> 6)
    dlt = (col & (C - 1)) - (row & (C - 1)) if reverse else (row & (C - 1)) - (col & (C - 1))
    strict = same & (dlt > 0)
    incl = same & (dlt >= 0)
    eye = (row == col).astype(F32)
    lvl_masks = [((row >> (lvl + 1)) == (col >> (lvl + 1))) & ((row >> lvl) != (col >> lvl)) for lvl in range(6)]
    rc = lax.broadcasted_iota(I32, (C, C), 0)
    cc = lax.broadcasted_iota(I32, (C, C), 1)
    tri = jnp.where((cc >= rc) if reverse else (rc >= cc), 1.0, 0.0).astype(BF16)
    head0 = lax.broadcasted_iota(I32, (C, LANE), 1) < HEAD

    def stack(x):
        return jnp.concatenate([jnp.where(head0, x, 0.0), jnp.where(head0, 0.0, x)], axis=0)

    pairs = range(npair)
    tm_ = {}

    def independent_stages(chunks):
        streams = [(g, p) for g in chunks for p in pairs]

        def st_cumsum():
            for (g, p) in streams:
                lw = sd_ref[0, g * C:(g + 1) * C, p * LANE:(p + 1) * LANE]
                tm_[(g, p)] = dict(lw=lw, cl=_dot_exact_lhs(tri, lw))

        def st_scores():
            for (g, p) in streams:
                t = tm_[(g, p)]
                t0, t1 = g * C, (g + 1) * C
                lo, hi = p * LANE, (p + 1) * LANE
                r = ss_ref[t0:t1, lo:hi]
                v = ss_ref[t0:t1, RW + lo:RW + hi]
                kk = ss_ref[t0:t1, 2 * RW + lo:2 * RW + hi]
                k2 = sd_ref[0, t0:t1, RW + lo:RW + hi]
                b = sd_ref[0, t0:t1, 2 * RW + lo:2 * RW + hi]
                cl, lw = t['cl'], t['lw']
                t['ptot'] = jnp.exp(jnp.sum(lw, axis=0, keepdims=True))
                pinv = jnp.exp(-cl)
                left = jnp.concatenate([stack(-kk * jnp.exp(cl - lw)), stack(r * jnp.exp(cl))], axis=0)
                right = jnp.concatenate([stack(b * pinv), stack(k2 * pinv)], axis=0)
                t['v_st'] = stack(v)
                t['left'] = left.astype(BF16)
                t['bk'] = (right * t['ptot']).astype(BF16)
                aa = _bdot(left, right, NT_DIMS)
                t['nmat'] = jnp.where(strict, aa[0:P, 0:P], 0.0)
                t['a_ak'] = jnp.where(strict, aa[0:P, P:2 * P], 0.0)
                t['a_rbk'] = jnp.concatenate([jnp.where(incl, aa[P:2 * P, 0:P], 0.0),
                                              jnp.where(incl, aa[P:2 * P, P:2 * P], 0.0)], axis=1).astype(BF16)
                t['tinv'] = eye + jnp.where(lvl_masks[0], t['nmat'], 0.0)

        def st_akv():
            for sk in streams:
                t = tm_[sk]
                t['akv'] = _bdot(t['a_ak'], t['v_st'])

        def st_et(m):
            def run():
                for sk in streams:
                    t = tm_[sk]
                    t['et'] = _bdot(jnp.where(m, t['nmat'], 0.0), t['tinv'])
            return run

        def st_tinv():
            for sk in streams:
                t = tm_[sk]
                t['tinv'] = t['tinv'] + _bdot(t['tinv'], t['et'])

        stages = [st_cumsum, st_scores, st_akv]
        for m in lvl_masks[1:]:
            stages += [st_et(m), st_tinv]
        return stages

    state = [s_scr[p] for p in pairs]
    carry = {}

    def dependent_stages(chunks):
        stages = []
        for g in chunks:
            def st_as(g=g):
                carry['a_s'] = [_bdot(tm_[(g, p)]['left'], state[p], NT_DIMS) for p in pairs]

            def st_u(g=g):
                u = [_bdot(tm_[(g, p)]['tinv'], carry['a_s'][p][0:P] + tm_[(g, p)]['akv']) for p in pairs]
                carry['uv'] = [jnp.concatenate([u[p], tm_[(g, p)]['v_st']], axis=0) for p in pairs]

            def st_state(g=g):
                for p in pairs:
                    t = tm_[(g, p)]
                    state[p] = state[p] * t['ptot'] + _bdot(carry['uv'][p], t['bk'], TN_DIMS)

            def st_y(g=g, a_s=None):
                for p in pairs:
                    y_st = carry['a_s'][p][P:2 * P] + _bdot(tm_[(g, p)]['a_rbk'], carry['uv'][p])
                    y_ref[g * C:(g + 1) * C, p * LANE:(p + 1) * LANE] = y_st[0:C] + y_st[C:P]

            stages += [st_as, st_u, st_y, st_state]
        return stages

    order = list(range(group - 1, -1, -1) if reverse else range(group))
    half = group // 2 if group >= 8 else group
    wave1, wave2 = order[:half], order[half:]
    for stage in independent_stages(wave1):
        stage()
    dep1 = dependent_stages(wave1)
    ind2 = independent_stages(wave2) if wave2 else []
    for i in range(max(len(dep1), len(ind2))):
        if i < len(ind2):
            ind2[i]()
        if i < len(dep1):
            dep1[i]()
    for stage in dependent_stages(wave2):
        stage()
    for p in pairs:
        s_scr[p] = state[p]

    @pl.when(n == nsteps - 1)
    def _():
        sfin_ref[0] = s_scr[...]


def rwkv_scan_dir(ss, sd, s0, B, T, d):
    nc = T // CHUNK
    group = min(RWKV_GROUP, nc)
    nsteps = nc // group
    npair = RW // LANE
    rows = group * CHUNK
    blk = (lambda b, n: b * nsteps + nsteps - 1 - n) if d else (lambda b, n: b * nsteps + n)
    kern = functools.partial(_rwkv_scan_kernel, nsteps=nsteps, group=group, reverse=bool(d))
    return pl.pallas_call(
        kern,
        out_shape=[jax.ShapeDtypeStruct((B * T, RW), F32),
                   jax.ShapeDtypeStruct((B, npair, LANE, LANE), F32)],
        grid=(B, nsteps),
        in_specs=[pl.BlockSpec((rows, 3 * RW), lambda b, n: (blk(b, n), 0)),
                  pl.BlockSpec((1, rows, 3 * RW), lambda b, n: (d, blk(b, n), 0)),
                  pl.BlockSpec((1, npair, LANE, LANE), lambda b, n: (b, 0, 0, 0))],
        out_specs=[pl.BlockSpec((rows, RW), lambda b, n: (blk(b, n), 0)),
                   pl.BlockSpec((1, npair, LANE, LANE), lambda b, n: (b, 0, 0, 0))],
        scratch_shapes=[pltpu.VMEM((npair, LANE, LANE), F32)],
        compiler_params=_cp(("arbitrary", "arbitrary")),
        name="rwkv_scan_bwd" if d else "rwkv_scan_fwd",
    )(ss, sd, s0)


def rwkv_scan(ss, sd, s0, B, T):
    y0, f0 = rwkv_scan_dir(ss, sd, s0[0], B, T, 0)
    y1, f1 = rwkv_scan_dir(ss, sd, s0[1], B, T, 1)
    return (y0, y1), (f0, f1)


GLA_KP = 256


GLA_GROUP = 8


def _gla_kernel(pbg_ref, aup_ref, ab_ref, s0_ref, o_ref, sfin_ref, s_scr, *, nsteps, group, reverse):
    n = pl.program_id(1)
    C = CHUNK
    R = group * C

    @pl.when(n == 0)
    def _():
        s_scr[...] = s0_ref[0]

    q = pbg_ref[:, 0:GLA_KP]
    k = pbg_ref[:, GLA_KP:2 * GLA_KP]
    v = pbg_ref[:, 2 * GLA_KP:2 * GLA_KP + RW]
    ad = pbg_ref[:, 2 * GLA_KP + RW:2 * GLA_KP + RW + LANE]
    x = _dot33(ad, aup_ref[0]) + ab_ref[0]
    la = (jnp.minimum(x, 0.0) - jnp.log(1.0 + jnp.exp(-jnp.abs(x)))) * (1.0 / GLA_TEMP)
    chunks = range(group)
    sl = [slice(g * C, (g + 1) * C) for g in chunks]
    rr = lax.broadcasted_iota(I32, (C, C), 0)
    rc = lax.broadcasted_iota(I32, (C, C), 1)
    tri = jnp.where((rc >= rr) if reverse else (rr >= rc), 1.0, 0.0).astype(BF16)
    bcums = [_dot_exact_lhs(tri, la[sl[g]]) for g in chunks]
    last = 0 if reverse else C - 1
    tots = [bc[last:last + 1, :] for bc in bcums]
    bcum = jnp.concatenate(bcums, axis=0)
    tot = jnp.concatenate([jnp.broadcast_to(t, (C, GLA_KP)) for t in tots], axis=0)
    q_in = q * jnp.exp(bcum) * (GLA_K ** -0.5)
    k_in = k * jnp.exp(-bcum)
    k_st = k * jnp.exp(tot - bcum)
    dn = [jnp.exp(t) for t in tots]
    klane = lax.broadcasted_iota(I32, (C, GLA_KP), 1)
    rt = lax.broadcasted_iota(I32, (GLA_HEADS * C, C), 0) & (C - 1)
    ct = lax.broadcasted_iota(I32, (GLA_HEADS * C, C), 1)
    causal = (ct >= rt) if reverse else (rt >= ct)
    vlane = lax.broadcasted_iota(I32, (C, RW), 1)
    sv = lax.broadcasted_iota(I32, (RW, GLA_KP), 0) >> 6
    sk = lax.broadcasted_iota(I32, (RW, GLA_KP), 1) >> 5
    q_rows = [jnp.concatenate([jnp.where((klane >> 5) == h, q_in[sl[g]], 0.0) for h in range(GLA_HEADS)],
                              axis=0).astype(BF16) for g in chunks]
    att = [jnp.where(causal, _bdot(q_rows[g], k_in[sl[g]], NT_DIMS), 0.0) for g in chunks]
    o_rows = [_bdot(att[g], v[sl[g]]) for g in chunks]
    kv = [jnp.where(sv == sk, _bdot(v[sl[g]], k_st[sl[g]], TN_DIMS), 0.0) for g in chunks]
    s = s_scr[...]
    for g in (reversed(chunks) if reverse else chunks):
        o = _bdot(q_in[sl[g]], s, NT_DIMS)
        for h in range(GLA_HEADS):
            o = o + jnp.where((vlane >> 6) == h, o_rows[g][h * C:(h + 1) * C], 0.0)
        o_ref[sl[g], :] = o
        s = s * dn[g] + kv[g]
    s_scr[...] = s

    @pl.when(n == nsteps - 1)
    def _():
        sfin_ref[0] = s_scr[...]


def gla_scan_dir(pbg, aup, ab, s0, B, T, d):
    nc = T // CHUNK
    group = min(GLA_GROUP, nc)
    nsteps = nc // group
    rows = group * CHUNK
    blk = (lambda b, n: b * nsteps + nsteps - 1 - n) if d else (lambda b, n: b * nsteps + n)
    kern = functools.partial(_gla_kernel, nsteps=nsteps, group=group, reverse=bool(d))
    return pl.pallas_call(
        kern,
        out_shape=[jax.ShapeDtypeStruct((B * T, RW), F32),
                   jax.ShapeDtypeStruct((B, RW, GLA_KP), F32)],
        grid=(B, nsteps),
        in_specs=[pl.BlockSpec((rows, PBG_W), lambda b, n: (blk(b, n), 0)),
                  pl.BlockSpec((1, LANE, GLA_KP), lambda b, n: (d, 0, 0)),
                  pl.BlockSpec((1, 1, GLA_KP), lambda b, n: (d, 0, 0)),
                  pl.BlockSpec((1, RW, GLA_KP), lambda b, n: (b, 0, 0))],
        out_specs=[pl.BlockSpec((rows, RW), lambda b, n: (blk(b, n), 0)),
                   pl.BlockSpec((1, RW, GLA_KP), lambda b, n: (b, 0, 0))],
        scratch_shapes=[pltpu.VMEM((RW, GLA_KP), F32)],
        compiler_params=_cp(("arbitrary", "arbitrary")),
        name="gla_scan_bwd" if d else "gla_scan_fwd",
    )(pbg, aup, ab, s0)


def gla_scan(pbg, aup, ab, s0, B, T):
    o0, f0 = gla_scan_dir(pbg, aup, ab, s0[0], B, T, 0)
    o1, f1 = gla_scan_dir(pbg, aup, ab, s0[1], B, T, 1)
    return (o0, o1), (f0, f1)


S5_PITCH = 68


def _s5_kernel(u_ref, a_ref, bbd_ref, cbd_ref, s0_ref, y_ref, sfin_ref, x_scr, st_scr, *, nc, nb):
    d = pl.program_id(0)
    n = pl.program_id(1)
    C = CHUNK
    half = S5_LANES // 2

    @pl.when(n == 0)
    def _():
        st_scr[...] = s0_ref[0]

    nslab = S5_LANES // LANE
    bbd = bbd_ref[0]
    for b in range(nb):
        bu = _bdot(u_ref[b], bbd)
        for j in range(nslab):
            x_scr[j, b * S5_PITCH:b * S5_PITCH + C, :] = bu[:, j * LANE:(j + 1) * LANE]
    a_re = a_ref[0, :, 0:half]
    a_im = a_ref[0, :, half:S5_LANES]

    def step(i, carry):
        re, im = carry
        t = i + d * (C - 1 - 2 * i)
        rows = pl.ds(t, nb, stride=S5_PITCH)
        bu = jnp.concatenate([x_scr[j, rows, :] for j in range(nslab)], axis=1)
        nre = a_re * re - a_im * im + bu[:, 0:half]
        nim = a_re * im + a_im * re + bu[:, half:S5_LANES]
        for j in range(nslab // 2):
            x_scr[j, rows, :] = nre[:, j * LANE:(j + 1) * LANE]
            x_scr[nslab // 2 + j, rows, :] = nim[:, j * LANE:(j + 1) * LANE]
        return nre, nim

    st = st_scr[...]
    re, im = lax.fori_loop(0, C, step, (st[:, 0:half], st[:, half:S5_LANES]), unroll=4)
    st_scr[...] = jnp.concatenate([re, im], axis=1)
    cbd = cbd_ref[0]
    for b in range(nb):
        xs = jnp.concatenate([x_scr[j, b * S5_PITCH:b * S5_PITCH + C, :] for j in range(nslab)], axis=1)
        y_ref[0, b] = _bdot(xs, cbd)

    @pl.when(n == nc - 1)
    def _():
        sfin_ref[0] = st_scr[...]


def s5_scan(u3, a_bar, bbd, cbd, s0):
    B, T, _ = u3.shape
    nc = T // CHUNK
    chunk = lambda d, n: n + d * (nc - 1 - 2 * n)
    kern = functools.partial(_s5_kernel, nc=nc, nb=B)
    return pl.pallas_call(
        kern,
        out_shape=[jax.ShapeDtypeStruct((2, B, T, S5_W), F32),
                   jax.ShapeDtypeStruct((2, B, S5_LANES), F32)],
        grid=(2, nc),
        in_specs=[pl.BlockSpec((B, CHUNK, S5_W), lambda d, n: (0, chunk(d, n), 0)),
                  pl.BlockSpec((1, 1, S5_LANES), lambda d, n: (d, 0, 0)),
                  pl.BlockSpec((1, S5_W, S5_LANES), lambda d, n: (d, 0, 0)),
                  pl.BlockSpec((1, S5_LANES, S5_W), lambda d, n: (d, 0, 0)),
                  pl.BlockSpec((1, B, S5_LANES), lambda d, n: (d, 0, 0))],
        out_specs=[pl.BlockSpec((1, B, CHUNK, S5_W), lambda d, n: (d, 0, chunk(d, n), 0)),
                   pl.BlockSpec((1, B, S5_LANES), lambda d, n: (d, 0, 0))],
        scratch_shapes=[pltpu.VMEM((S5_LANES // LANE, B * S5_PITCH, LANE), F32),
                        pltpu.VMEM((B, S5_LANES), F32)],
        compiler_params=_cp(("arbitrary", "arbitrary")),
        name="s5_scan",
    )(u3, a_bar, bbd, cbd, s0)


ROWT = 4
U32 = jnp.uint32


def _pack_bf16_pair(a, b):
    au = lax.bitcast_convert_type(a.astype(BF16).astype(F32), U32)
    bu = lax.bitcast_convert_type(b.astype(BF16).astype(F32), U32)
    return (au >> 16) | bu


def _unpack_bf16_pair(w):
    return (lax.bitcast_convert_type(w << 16, F32), lax.bitcast_convert_type(w & jnp.uint32(0xFFFF0000), F32))


def _tile_rows_store(ref, val):
    tm = val.shape[0]
    assert val.shape[1] == 2 * ROWT * LANE
    for s in range(ROWT):
        ref[pl.ds(s, tm, stride=ROWT), :] = _pack_bf16_pair(val[:, 2 * s * LANE:(2 * s + 1) * LANE],
                                                            val[:, (2 * s + 1) * LANE:(2 * s + 2) * LANE])


def _tile_rows_load(ref, tm, base=0):
    parts = []
    for s in range(ROWT):
        parts.extend(_unpack_bf16_pair(ref[pl.ds(base + s, tm, stride=ROWT), :]))
    return jnp.concatenate(parts, axis=1)


def _mix_out_kernel(y0_ref, y1_ref, gb_ref, o0_ref, o1_ref, gg_ref, yd_ref, u_ref, x_ref, mod_ref,
                    hb_ref, lng_ref, lnb_ref, gng_ref, s5d_ref, gluw_ref, glub_ref, wout_ref, l1g_ref, l1b_ref,
                    *rest, alpha):
    x1_ref, h2_ref, h2t_ref = rest[-3:]
    hb = hb_ref[...]
    inv = 1.0 / HEAD
    y = y0_ref[...] + y1_ref[...]
    yc = y - _dot_split_rhs(y, hb) * inv
    var = _bdot(yc * yc, hb) * inv
    gn = yc * lax.rsqrt(var + RWKV_GN_EPS) * lng_ref[...] + lnb_ref[...]
    y_a = (gn + gb_ref[:, RW:2 * RW]) * gb_ref[:, 0:RW]
    o = o0_ref[...] + o1_ref[...]
    o = o * lax.rsqrt(_bdot(o * o, hb) * inv + 1e-6) * gng_ref[...]
    y_b = o * _silu(gg_ref[...])
    c = s5d_ref[...] * u_ref[...] + yd_ref[0] + yd_ref[1]
    c = 0.5 * c * (1.0 + jnp.tanh(math.sqrt(2.0 / math.pi) * (c + 0.044715 * (c * c * c))))
    y_c = c * _sigmoid(_bdot(c, gluw_ref[...]) + glub_ref[...])
    y_mix = (_bdot(y_a, wout_ref[0:RW, :]) + _bdot(y_b, wout_ref[RW:2 * RW, :])
             + _bdot(y_c, wout_ref[2 * RW:2 * RW + S5_W, :]))
    x1 = _layer_norm(alpha * x_ref[...] + mod_ref[0, 2:3, :] * y_mix) * l1g_ref[...] + l1b_ref[...]
    x1_ref[...] = x1
    h2 = _layer_norm(x1) * (1.0 + mod_ref[0, 4:5, :]) + mod_ref[0, 3:4, :]
    h2_ref[...] = h2
    _tile_rows_store(h2t_ref, h2)


def mix_out(y01, gb, o01, gg, yd, u, stream, mod3, wts, row_of_tile, tm, alpha, out_rows=None, out_row0=0, into=None):
    x2, row0, N = stream
    D = x2.shape[1]
    out_rows = N if out_rows is None else out_rows
    assert row0 % tm == 0 and out_row0 % tm == 0 and N % tm == 0
    t0, o0 = row0 // tm, out_row0 // tm
    full = lambda a: pl.BlockSpec(a.shape, lambda i: (0,) * a.ndim)
    kern = functools.partial(_mix_out_kernel, alpha=alpha)
    ins = [y01[0], y01[1], gb, o01[0], o01[1], gg, yd, u, x2, mod3, *wts]
    extra = [] if into is None else list(into)
    return pl.pallas_call(
        kern,
        out_shape=[jax.ShapeDtypeStruct((out_rows, D), F32),
                   jax.ShapeDtypeStruct((out_rows, D), F32),
                   jax.ShapeDtypeStruct((out_rows * ROWT, LANE), U32)],
        grid=(N // tm,),
        in_specs=[pl.BlockSpec((tm, RW), lambda i: (i, 0)),
                  pl.BlockSpec((tm, RW), lambda i: (i, 0)),
                  pl.BlockSpec((tm, 2 * RW), lambda i: (i, 0)),
                  pl.BlockSpec((tm, RW), lambda i: (i, 0)),
                  pl.BlockSpec((tm, RW), lambda i: (i, 0)),
                  pl.BlockSpec((tm, RW), lambda i: (i, 0)),
                  pl.BlockSpec((2, tm, S5_W), lambda i: (0, i, 0)),
                  pl.BlockSpec((tm, S5_W), lambda i: (i, 0)),
                  pl.BlockSpec((tm, D), lambda i: (i + t0, 0)),
                  pl.BlockSpec((1, 6, D), lambda i: (row_of_tile(i), 0, 0))] + [full(a) for a in wts]
                 + [pl.BlockSpec(memory_space=pl.ANY)] * len(extra),
        out_specs=[pl.BlockSpec((tm, D), lambda i: (i + o0, 0)),
                   pl.BlockSpec((tm, D), lambda i: (i + o0, 0)),
                   pl.BlockSpec((tm * ROWT, LANE), lambda i: (i + o0, 0))],
        input_output_aliases={len(ins) + k: k for k in range(len(extra))},
        compiler_params=_cp(("arbitrary",)),
        name="mix_out",
    )(*ins, *extra)


def _first_max(x, idx, big):
    m = jnp.max(x, axis=0, keepdims=True)
    first = jnp.min(jnp.where(x == m, idx, big), axis=0, keepdims=True)
    return m, first


def _router_kernel(h_ref, rwt_ref, bias_ref, e_ref, w_ref):
    tm = h_ref.shape[0]
    gsz = N_EXPERTS // N_EGROUPS
    ninf = -jnp.inf
    s = _sigmoid(_dot33(rwt_ref[...], h_ref[...], NT_DIMS))
    ssel = s + bias_ref[:, 0:1]
    gi = lax.broadcasted_iota(I32, (gsz, tm), 0)
    gscore = []
    for g in range(N_EGROUPS):
        xg = ssel[g * gsz:(g + 1) * gsz, :]
        m1, i1 = _first_max(xg, gi, gsz)
        m2 = jnp.max(jnp.where(gi == i1, ninf, xg), axis=0, keepdims=True)
        gscore.append(m1 + m2)
    cur = jnp.concatenate(gscore, axis=0)
    gidx = lax.broadcasted_iota(I32, (N_EGROUPS, tm), 0)
    picked = jnp.zeros((N_EGROUPS, tm), F32)
    for _ in range(TOPK_GROUPS):
        _, first = _first_max(cur, gidx, N_EGROUPS)
        hit = gidx == first
        picked = jnp.where(hit, 1.0, picked)
        cur = jnp.where(hit, ninf, cur)
    x = jnp.concatenate(
        [jnp.where(picked[g:g + 1, :] > 0.5, ssel[g * gsz:(g + 1) * gsz, :], ninf) for g in range(N_EGROUPS)], axis=0)
    ei = lax.broadcasted_iota(I32, (N_EXPERTS, tm), 0)
    idxs, ws = [], []
    for _ in range(TOP_K):
        _, first = _first_max(x, ei, N_EXPERTS)
        hit = ei == first
        idxs.append(first)
        ws.append(jnp.sum(jnp.where(hit, s, 0.0), axis=0, keepdims=True))
        x = jnp.where(hit, ninf, x)
    w = jnp.concatenate(ws, axis=0)
    e_ref[...] = jnp.concatenate(idxs, axis=0)
    w_ref[...] = w / jnp.sum(w, axis=0, keepdims=True) * ROUTE_SCALE


def moe_router(h2, rwt, bias_b, tm):
    N, D = h2.shape
    return pl.pallas_call(
        _router_kernel,
        out_shape=[jax.ShapeDtypeStruct((TOP_K, N), I32), jax.ShapeDtypeStruct((TOP_K, N), F32)],
        grid=(N // tm,),
        in_specs=[pl.BlockSpec((tm, D), lambda i: (i, 0)),
                  pl.BlockSpec((N_EXPERTS, D), lambda i: (0, 0)),
                  pl.BlockSpec((N_EXPERTS, LANE), lambda i: (0, 0))],
        out_specs=[pl.BlockSpec((TOP_K, tm), lambda i: (0, i)),
                   pl.BlockSpec((TOP_K, tm), lambda i: (0, i))],
        compiler_params=_cp(("arbitrary",)),
        name="moe_router",
    )(h2, rwt, bias_b)


def _moe_count_kernel(e_ref, pstart_ref, plan_ref, blk_ref, cnt_scr, *, nt, nbp):
    i = pl.program_id(0)
    tp = e_ref.shape[1]

    @pl.when(i == 0)
    def _():
        cnt_scr[...] = jnp.zeros_like(cnt_scr)

    ei = lax.broadcasted_iota(I32, (N_EXPERTS, tp), 0)
    acc = jnp.zeros((N_EXPERTS, 1), F32)
    for k in range(TOP_K):
        acc = acc + jnp.sum(jnp.where(ei == e_ref[k:k + 1, :], 1.0, 0.0), axis=1, keepdims=True)
    cnt_scr[...] = cnt_scr[...] + acc

    @pl.when(i == nt - 1)
    def _():
        cnt = cnt_scr[...].astype(I32)
        padded = (cnt + (MOE_BLK - 1)) & (-MOE_BLK)
        r = lax.broadcasted_iota(I32, (N_EXPERTS, N_EXPERTS), 0)
        c = lax.broadcasted_iota(I32, (N_EXPERTS, N_EXPERTS), 1)
        tri = jnp.where(c <= r, 1.0, 0.0).astype(BF16)
        padded_b = jnp.broadcast_to(padded.astype(F32), (N_EXPERTS, LANE))
        p_end = _dot_exact_lhs(tri, padded_b)
        pstart = p_end - padded_b
        pstart_ref[...] = pstart.astype(I32)
        diag = r == c
        ps_row = jnp.sum(jnp.where(diag, pstart[:, 0:1], 0.0), axis=0, keepdims=True)
        cnt_row = jnp.sum(jnp.where(diag, cnt_scr[...], 0.0), axis=0, keepdims=True)
        plan_ref[...] = jnp.concatenate([ps_row, cnt_row, jnp.zeros((SUB - 2, N_EXPERTS), F32)], axis=0).astype(I32)
        lim =(lax.broadcasted_iota(I32, (N_EXPERTS, nbp), 1) * MOE_BLK).astype(F32)
        be = jnp.sum(jnp.where(p_end[:, 0:1] <= lim, 1.0, 0.0), axis=0, keepdims=True)
        be = jnp.minimum(be, N_EXPERTS - 1.0)
        nused = jnp.max(p_end[:, 0:1], axis=0, keepdims=True) * (1.0 / MOE_BLK)
        blk_ref[...] = jnp.concatenate([jnp.broadcast_to(be, (SUB // 2, nbp)),
                                        jnp.broadcast_to(nused, (SUB // 2, nbp))], axis=0).astype(I32)


def moe_counts(eidx, tp, nbp):
    N = eidx.shape[1]
    nt = N // tp
    kern = functools.partial(_moe_count_kernel, nt=nt, nbp=nbp)
    return pl.pallas_call(
        kern,
        out_shape=[jax.ShapeDtypeStruct((N_EXPERTS, LANE), I32),
                   jax.ShapeDtypeStruct((SUB, N_EXPERTS), I32),
                   jax.ShapeDtypeStruct((SUB, nbp), I32)],
        grid=(nt,),
        in_specs=[pl.BlockSpec((TOP_K, tp), lambda i: (0, i))],
        out_specs=[pl.BlockSpec((N_EXPERTS, LANE), lambda i: (0, 0)),
                   pl.BlockSpec((SUB, N_EXPERTS), lambda i: (0, 0)),
                   pl.BlockSpec((SUB, nbp), lambda i: (0, 0))],
        scratch_shapes=[pltpu.VMEM((N_EXPERTS, 1), F32)],
        compiler_params=_cp(("arbitrary",)),
        name="moe_counts",
    )(eidx)


def _moe_dest_kernel(e_ref, pstart_ref, dest_ref, base_scr):
    i = pl.program_id(0)
    tp = e_ref.shape[1]

    @pl.when(i == 0)
    def _():
        base_scr[...] = pstart_ref[:, 0:1].astype(F32)

    ei = lax.broadcasted_iota(I32, (N_EXPERTS, tp), 0)
    r = lax.broadcasted_iota(I32, (tp, tp), 0)
    c = lax.broadcasted_iota(I32, (tp, tp), 1)
    tri = jnp.where(r <= c, 1.0, 0.0).astype(BF16)
    base = base_scr[...]
    rows = []
    for k in range(TOP_K):
        hit = ei == e_ref[k:k + 1, :]
        oh = jnp.where(hit, 1.0, 0.0)
        cum = _bdot(oh, tri)
        rows.append(jnp.sum(jnp.where(hit, cum - 1.0 + base, 0.0), axis=0, keepdims=True))
        base = base + cum[:, tp - 1:tp]
    base_scr[...] = base
    dest_ref[...] = jnp.concatenate(rows, axis=0).astype(I32)


def moe_dest(eidx, pstart, tp):
    N = eidx.shape[1]
    return pl.pallas_call(
        _moe_dest_kernel,
        out_shape=jax.ShapeDtypeStruct((TOP_K, N), I32),
        grid=(N // tp,),
        in_specs=[pl.BlockSpec((TOP_K, tp), lambda i: (0, i)),
                  pl.BlockSpec((N_EXPERTS, LANE), lambda i: (0, 0))],
        out_specs=pl.BlockSpec((TOP_K, tp), lambda i: (0, i)),
        scratch_shapes=[pltpu.VMEM((N_EXPERTS, 1), F32)],
        compiler_params=_cp(("arbitrary",)),
        name="moe_dest",
    )(eidx, pstart)


def _tile_at(ref, token):
    return ref.at[pl.ds(pl.multiple_of(token * ROWT, ROWT), ROWT), :]


SCAT_RING = 3


def _moe_scatter_kernel(dest_ref, h_ref, xs_hbm, ring, sem, *, nt):
    s = pl.program_id(0)
    tp = dest_ref.shape[1]
    slot = lax.rem(s, SCAT_RING)
    total = xs_hbm.at[pl.ds(0, tp * TOP_K * ROWT), :]

    def slot_wait(sl):
        pltpu.make_async_copy(total, total, sem.at[sl]).wait()

    @pl.when(s >= SCAT_RING)
    def _():
        slot_wait(slot)

    ring[slot] = h_ref[...]

    def issue(j, carry):
        for k in range(TOP_K):
            pltpu.make_async_copy(_tile_at(ring.at[slot], j), _tile_at(xs_hbm, dest_ref[k, j]),
                                  sem.at[slot]).start(priority=k % 2)
        return carry

    lax.fori_loop(0, tp, issue, 0)

    @pl.when(s == nt - 1)
    def _():
        for back in range(SCAT_RING - 1, -1, -1):
            if nt - 1 - back >= 0:
                slot_wait((nt - 1 - back) % SCAT_RING)


def moe_scatter(dest, h2t, n_slots, tp):
    N = dest.shape[1]
    nt = N // tp
    return pl.pallas_call(
        functools.partial(_moe_scatter_kernel, nt=nt),
        out_shape=jax.ShapeDtypeStruct((n_slots * ROWT, LANE), U32),
        grid=(nt,),
        in_specs=[pl.BlockSpec((TOP_K, tp), lambda i: (0, i), memory_space=pltpu.SMEM),
                  pl.BlockSpec((tp * ROWT, LANE), lambda i: (i, 0))],
        out_specs=pl.BlockSpec(memory_space=pl.ANY),
        scratch_shapes=[pltpu.VMEM((SCAT_RING, tp * ROWT, LANE), U32), pltpu.SemaphoreType.DMA((SCAT_RING,))],
        compiler_params=_cp(("arbitrary",)),
        name="moe_scatter",
    )(dest, h2t)


PAD_BITS = tuple(1 << b for b in reversed(range(MOE_BLK.bit_length() - 1)))


def _moe_padfill_kernel(ps_ref, xs_in, xs_hbm, zero_scr, sem):
    del xs_in
    zero_scr[...] = jnp.zeros_like(zero_scr)

    def pad_copies(e, wait):
        cnt = ps_ref[1, e]
        npad = ((cnt + (MOE_BLK - 1)) & (-MOE_BLK)) - cnt
        off = ps_ref[0, e] + cnt
        for bit in PAD_BITS:
            @pl.when((npad & bit) != 0)
            def _():
                cp = pltpu.make_async_copy(zero_scr.at[pl.ds(0, bit * ROWT), :],
                                           xs_hbm.at[pl.ds(pl.multiple_of(off * ROWT, ROWT), bit * ROWT), :], sem)
                if wait:
                    cp.wait()
                else:
                    cp.start()
            off = off + (npad & bit)

    def issue(e, carry):
        pad_copies(e, False)
        return carry

    def drain(e, carry):
        pad_copies(e, True)
        return carry

    lax.fori_loop(0, N_EXPERTS, issue, 0)
    lax.fori_loop(0, N_EXPERTS, drain, 0)


def moe_padfill(pstart, xs):
    return pl.pallas_call(
        _moe_padfill_kernel,
        out_shape=jax.ShapeDtypeStruct(xs.shape, xs.dtype),
        grid=(1,),
        in_specs=[pl.BlockSpec(memory_space=pltpu.SMEM),
                  pl.BlockSpec(memory_space=pl.ANY)],
        out_specs=pl.BlockSpec(memory_space=pl.ANY),
        scratch_shapes=[pltpu.VMEM((PAD_BITS[0] * ROWT, LANE), U32), pltpu.SemaphoreType.DMA(())],
        input_output_aliases={1: 0},
        compiler_params=_cp(("arbitrary",)),
        name="moe_padfill",
    )(pstart, xs)


EXP_NX = 8
EXP_PF = 4
EXP_NY = 4


def _experts_kernel(plan_ref, w13_ref, w2_ref, xs_hbm, y_hbm, w13_bf, w2_bf, x_buf, y_buf, sem_in, sem_out):
    e = pl.program_id(0)
    ne = pl.num_programs(0)
    ff = w2_bf.shape[0]
    rows = MOE_BLK * ROWT
    shift = MOE_BLK.bit_length() - 1
    blocks_of = lambda c: lax.shift_right_logical(c + (MOE_BLK - 1), shift)
    nblk = blocks_of(plan_ref[1, e])
    first = lax.shift_right_logical(plan_ref[0, e], shift)
    nused = lax.shift_right_logical(plan_ref[0, ne - 1], shift) + blocks_of(plan_ref[1, ne - 1])

    def window(ref, g):
        return ref.at[pl.ds(pl.multiple_of(g * rows, rows), rows), :]

    def x_copy(g):
        return pltpu.make_async_copy(window(xs_hbm, g), x_buf.at[g & (EXP_NX - 1)], sem_in.at[g & (EXP_NX - 1)])

    def y_copy(g):
        return pltpu.make_async_copy(y_buf.at[g & (EXP_NY - 1)], window(y_hbm, g), sem_out.at[g & (EXP_NY - 1)])

    @pl.when(e == 0)
    def _():
        for g in range(EXP_PF):
            @pl.when(g < nused)
            def _():
                x_copy(g).start()

    def process(gs):
        for g in gs:
            x_copy(g).wait()

            @pl.when(g + EXP_PF < nused)
            def _():
                x_copy(g + EXP_PF).start()

            @pl.when(g >= EXP_NY)
            def _():
                y_copy(g - EXP_NY).wait()

        xs = [_tile_rows_load(x_buf.at[g & (EXP_NX - 1)], MOE_BLK).astype(BF16) for g in gs]
        hs = [jnp.dot(x, w13_bf[...], preferred_element_type=F32) for x in xs]
        acts = [(_silu(h[:, 0:ff]) * h[:, ff:2 * ff]).astype(BF16) for h in hs]
        ys = [jnp.dot(a, w2_bf[...], preferred_element_type=F32) for a in acts]
        for g, y in zip(gs, ys):
            _tile_rows_store(y_buf.at[g & (EXP_NY - 1)], y)
            y_copy(g).start()

    @pl.when(nblk > 0)
    def _():
        w13_bf[...] = w13_ref[0, 0].astype(BF16)
        w2_bf[...] = w2_ref[0, 0].astype(BF16)
        npair = lax.shift_right_logical(nblk, 1)

        def body(j, carry):
            g = first + 2 * j
            process([g, g + 1])
            return carry

        lax.fori_loop(0, npair, body, 0)

        @pl.when((nblk & 1) == 1)
        def _():
            process([first + nblk - 1])

    @pl.when(e == ne - 1)
    def _():
        for back in range(EXP_NY, 0, -1):
            @pl.when(nused >= back)
            def _():
                y_copy(nused - back).wait()


def moe_experts(plan, xs, w13, w2, layer):
    _, E, D, F2 = w13.shape
    rows = MOE_BLK * ROWT
    grid_spec = pltpu.PrefetchScalarGridSpec(
        num_scalar_prefetch=1,
        grid=(E,),
        in_specs=[pl.BlockSpec((1, 1, D, F2), lambda e, p: (layer, e, 0, 0)),
                  pl.BlockSpec((1, 1, F2 // 2, D), lambda e, p: (layer, e, 0, 0)),
                  pl.BlockSpec(memory_space=pl.ANY)],
        out_specs=pl.BlockSpec(memory_space=pl.ANY),
        scratch_shapes=[pltpu.VMEM((D, F2), BF16), pltpu.VMEM((F2 // 2, D), BF16),
                        pltpu.VMEM((EXP_NX, rows, LANE), U32), pltpu.VMEM((EXP_NY, rows, LANE), U32),
                        pltpu.SemaphoreType.DMA((EXP_NX,)), pltpu.SemaphoreType.DMA((EXP_NY,))])
    return pl.pallas_call(
        _experts_kernel,
        out_shape=jax.ShapeDtypeStruct(xs.shape, U32),
        grid_spec=grid_spec,
        compiler_params=_cp(("arbitrary",)),
        name="moe_experts",
    )(plan, w13, w2, xs)


def _moe_combine_kernel(dest_ref, wt_ref, h_ref, x_ref, mod_ref, s13_ref, s2_ref, l2g_ref, l2b_ref, y_hbm,
                        o_ref, g_scr, sem, *, alpha, nt):
    s = pl.program_id(0)
    tm = h_ref.shape[0]
    ff = s2_ref.shape[0]
    rows = tm * ROWT
    slot_g = s % 2
    slot_c = 1 - slot_g

    def slot_copy(slot):
        return pltpu.make_async_copy(y_hbm.at[pl.ds(0, TOP_K * rows), :], g_scr.at[slot], sem.at[slot])

    @pl.when(s == 0)
    def _():
        g_scr[1] = jnp.zeros((TOP_K * rows, LANE), U32)

    @pl.when(s > 0)
    def _():
        slot_copy(slot_c).wait()

    for j in range(tm):
        for k in range(TOP_K):
            pltpu.make_async_copy(_tile_at(y_hbm, dest_ref[k, j]),
                                  g_scr.at[slot_g, pl.ds(k * rows + j * ROWT, ROWT), :],
                                  sem.at[slot_g]).start(priority=k % 2)
    hs = _bdot(h_ref[...], s13_ref[...])
    f = _bdot(_silu(hs[:, 0:ff]) * hs[:, ff:2 * ff], s2_ref[...])
    for k in range(TOP_K):
        f = f + wt_ref[:, k:k + 1] * _tile_rows_load(g_scr.at[slot_c], tm, base=k * rows)
    o_ref[...] = (_layer_norm(alpha * x_ref[...] + mod_ref[0, 5:6, :] * f) * l2g_ref[...] + l2b_ref[...])

    @pl.when(s == nt)
    def _():
        slot_copy(slot_g).wait()


def moe_combine(dest, wt_t, h2, x1, mod3, wts, y, row_of_tile, tm, alpha):
    N, D = h2.shape
    nt = N // tm
    full = lambda a: pl.BlockSpec(a.shape, lambda s: (0,) * a.ndim)
    kern = functools.partial(_moe_combine_kernel, alpha=alpha, nt=nt)
    nxt = lambda s: jnp.minimum(s, nt - 1)
    cur = lambda s: jnp.maximum(s - 1, 0)
    return pl.pallas_call(
        kern,
        out_shape=jax.ShapeDtypeStruct((N, D), F32),
        grid=(nt + 1,),
        in_specs=[pl.BlockSpec((TOP_K, tm), lambda s: (0, nxt(s)), memory_space=pltpu.SMEM),
                  pl.BlockSpec((tm, TOP_K), lambda s: (cur(s), 0)),
                  pl.BlockSpec((tm, D), lambda s: (cur(s), 0)),
                  pl.BlockSpec((tm, D), lambda s: (cur(s), 0)),
                  pl.BlockSpec((1, 6, D), lambda s: (row_of_tile(cur(s)), 0, 0))]
                 + [full(a) for a in wts] + [pl.BlockSpec(memory_space=pl.ANY)],
        out_specs=pl.BlockSpec((tm, D), lambda s: (cur(s), 0)),
        scratch_shapes=[pltpu.VMEM((2, TOP_K * tm * ROWT, LANE), U32), pltpu.SemaphoreType.DMA((2,))],
        compiler_params=_cp(("arbitrary",)),
        name="moe_combine",
    )(dest, wt_t, h2, x1, mod3, *wts, y)


def moe_ffn(h2, h2t, x1, mod3, mp, row_of_tile, tm, alpha):
    N = h2.shape[0]
    nb_total = (N * TOP_K + MOE_BLK - 1) // MOE_BLK + N_EXPERTS
    nbp = ((nb_total + LANE - 1) // LANE) * LANE
    tp = _tile(N, 256)
    eidx, wts = moe_router(h2, mp['rwt'], mp['bias'], tp)
    pstart, plan, blk = moe_counts(eidx, tp, nbp)
    dest = moe_dest(eidx, pstart, tp)
    xs = moe_scatter(dest, h2t, nb_total * MOE_BLK, tp)
    xs = moe_padfill(plan, xs)
    y = moe_experts(plan, xs, mp['w13'], mp['w2'], mp['layer'])
    return moe_combine(dest, wts.T, h2, x1, mod3, mp['comb'], y, row_of_tile, tm, alpha)


def token_mixers(stream, B, T, W, mod3, row_of_tile, lp, states, tm):
    pa, pbg, gg, u = inproj(stream, mod3, lp['w_in'], row_of_tile, tm)
    pac = short_conv(pa.reshape(B, T, PA_W), lp['taps'], W).reshape(B * T, PA_W)
    ss, sd, gb = rwkv_prep(pac, lp['prep'], tm)
    s_rwkv, s_gla, s_s5 = states
    y2, f_rwkv = rwkv_scan(ss, sd, s_rwkv, B, T)
    o2, f_gla = gla_scan(pbg, lp['gla_aup'], lp['gla_ab'], s_gla, B, T)
    yd, f_s5 = s5_scan(u.reshape(B, T, S5_W), lp['s5_a'], lp['s5_bbd'], lp['s5_cbd'], s_s5)
    return (y2, gb, o2, gg, yd.reshape(2, B * T, S5_W), u), (f_rwkv, f_gla, f_s5)


def zero_states(B):
    return ((jnp.zeros((B, RW // LANE, LANE, LANE), F32),) * 2,
            (jnp.zeros((B, RW, GLA_KP), F32),) * 2,
            jnp.zeros((2, B, S5_LANES), F32))


def _inproj_columns():
    r_cols = 3 * RW + 2 * DECAY_RANK + 2 * ICLR_RANK + GATE_RANK
    kd = GLA_HEADS * GLA_K
    gq, gk, gv = r_cols, r_cols + kd, r_cols + 2 * kd
    gg = gv + RW
    gad = gg + RW
    pc = gad + 2 * GLA_RANK
    z = lambda n: [-1] * n
    cols = list(range(0, r_cols)) + z(PA_W - r_cols)
    cols += list(range(gq, gq + kd)) + z(256 - kd)
    cols += list(range(gk, gk + kd)) + z(256 - kd)
    cols += list(range(gv, gv + RW))
    cols += list(range(gad, gad + 2 * GLA_RANK)) + z(LANE - 2 * GLA_RANK)
    cols += list(range(gg, gg + RW))
    cols += list(range(pc, pc + S5_W))
    assert len(cols) == NP_W
    return np.asarray(cols, np.int32)


def _head_block_ones():
    h = np.arange(RW) // HEAD
    return jnp.asarray(h[:, None] == h[None, :], BF16)


def _layer_params(l, p):
    D = p['w_in'].shape[1]
    cols = _inproj_columns()
    w_in = jnp.concatenate([p['w_in'][l], jnp.zeros((D, 1), F32)], axis=1)
    w_in = jnp.take(w_in, jnp.asarray(np.where(cols < 0, w_in.shape[1] - 1, cols)), axis=1).astype(BF16)
    taps = p['rwkv_conv'][l].reshape(9, -1)
    taps = jnp.concatenate([taps, jnp.zeros((9, PA_W - taps.shape[1]), F32)], axis=1)
    wup = jnp.zeros((LANE, 2 * RW), F32)
    aup = jnp.zeros((LANE, 2 * RW), F32)
    for d in range(2):
        wup = wup.at[d * DECAY_RANK:(d + 1) * DECAY_RANK, d * RW:(d + 1) * RW].set(p['rwkv_w_up'][l, d])
        o = 2 * DECAY_RANK + d * ICLR_RANK
        aup = aup.at[o:o + ICLR_RANK, d * RW:(d + 1) * RW].set(p['rwkv_a_up'][l, d])
    gup = jnp.zeros((LANE, RW), F32).at[0:GATE_RANK].set(p['rwkv_g_up'][l])
    row = lambda a: a.reshape(1, -1)
    prep = (wup, aup, gup, row(p['rwkv_w0'][l]), row(p['rwkv_a0'][l]), row(p['rwkv_k_k'][l]),
            row(p['rwkv_k_a'][l]), row(p['rwkv_r_k'][l]), _head_block_ones())
    kd = GLA_HEADS * GLA_K
    gla_aup = jnp.zeros((2, LANE, GLA_KP), F32)
    for d in range(2):
        gla_aup = gla_aup.at[d, d * GLA_RANK:(d + 1) * GLA_RANK, 0:kd].set(p['gla_a_up'][l, d])
    gla_ab = jnp.zeros((2, 1, GLA_KP), F32).at[:, 0, 0:kd].set(p['gla_a_bias'][l])
    lam_re, lam_im = p['s5_lam_re'][l], p['s5_lam_im'][l]
    dt = jnp.exp(p['s5_log_dt'][l])[:, :, None]
    zr, zi = lam_re[:, None, :] * dt, lam_im[:, None, :] * dt
    mag = jnp.exp(zr)
    ab_r, ab_i = mag * jnp.cos(zi), mag * jnp.sin(zi)
    den = (lam_re * lam_re + lam_im * lam_im)[:, None, :]
    f_r = ((ab_r - 1) * lam_re[:, None, :] + ab_i * lam_im[:, None, :]) / den
    f_i = (ab_i * lam_re[:, None, :] - (ab_r - 1) * lam_im[:, None, :]) / den
    b_re, b_im = p['s5_b_re'][l], p['s5_b_im'][l]
    bb_r = f_r[..., None] * b_re - f_i[..., None] * b_im
    bb_i = f_r[..., None] * b_im + f_i[..., None] * b_re
    eye_g = jnp.eye(S5_GROUPS, dtype=F32)
    half = S5_LANES // 2

    def in_blockdiag(bb):
        return jnp.einsum('dgpc,gh->dgchp', bb, eye_g).reshape(2, S5_W, half)

    def out_blockdiag(cc):
        return jnp.einsum('dgcp,gh->dgphc', cc, eye_g).reshape(2, half, S5_W)

    s5_bbd = jnp.concatenate([in_blockdiag(bb_r), in_blockdiag(bb_i)], axis=2).astype(BF16)
    s5_cbd = jnp.concatenate([out_blockdiag(p['s5_c_re'][l]), -out_blockdiag(p['s5_c_im'][l])], axis=1).astype(BF16)
    s5_a = jnp.concatenate([ab_r.reshape(2, 1, half), ab_i.reshape(2, 1, half)], axis=2)
    mix = (_head_block_ones(), row(p['rwkv_ln_g'][l]), row(p['rwkv_ln_b'][l]),
           row(jnp.tile(p['gla_norm_g'][l], GLA_HEADS)), row(p['s5_d'][l]),
           p['s5_glu_w'][l].astype(BF16), row(p['s5_glu_b'][l]), p['w_out'][l].astype(BF16),
           row(p['ln1_g'][l]), row(p['ln1_b'][l]))
    moe = dict(rwt=p['router_w'][l].T,
               bias=jnp.broadcast_to(p['router_bias'][l][:, None], (N_EXPERTS, LANE)),
               w13=p['exp_w13'], w2=p['exp_w2'], layer=l,
               comb=(p['sh_w13'][l].astype(BF16), p['sh_w2'][l].astype(BF16), row(p['ln2_g'][l]), row(p['ln2_b'][l])))
    return dict(w_in=w_in, taps=taps, prep=prep, gla_aup=gla_aup, gla_ab=gla_ab,
                s5_a=s5_a, s5_bbd=s5_bbd, s5_cbd=s5_cbd, mix=mix, moe=moe)


_ARG_NAMES = ('x', 'c', 'ctx', 'c_ctx', 'w_mod', 'b_mod', 'w_in', 'rwkv_conv', 'rwkv_w0', 'rwkv_w_up', 'rwkv_a0',
              'rwkv_a_up', 'rwkv_g_up', 'rwkv_k_k', 'rwkv_k_a', 'rwkv_r_k', 'rwkv_ln_g', 'rwkv_ln_b', 'gla_a_up',
              'gla_a_bias', 'gla_norm_g', 's5_lam_re', 's5_lam_im', 's5_log_dt', 's5_b_re', 's5_b_im', 's5_c_re',
              's5_c_im', 's5_d', 's5_glu_w', 's5_glu_b', 'w_out', 'ln1_g', 'ln1_b', 'router_w', 'router_bias',
              'exp_w13', 'exp_w2', 'sh_w13', 'sh_w2', 'ln2_g', 'ln2_b')


def _tile(n, pref):
    t = pref
    while n % t:
        t //= 2
    return t


def kernel(x, c, ctx, c_ctx, w_mod, b_mod, w_in, rwkv_conv, rwkv_w0, rwkv_w_up, rwkv_a0, rwkv_a_up, rwkv_g_up,
           rwkv_k_k, rwkv_k_a, rwkv_r_k, rwkv_ln_g, rwkv_ln_b, gla_a_up, gla_a_bias, gla_norm_g, s5_lam_re,
           s5_lam_im, s5_log_dt, s5_b_re, s5_b_im, s5_c_re, s5_c_im, s5_d, s5_glu_w, s5_glu_b, w_out, ln1_g,
           ln1_b, router_w, router_bias, exp_w13, exp_w2, sh_w13, sh_w2, ln2_g, ln2_b):
    p = dict(zip(_ARG_NAMES, (x, c, ctx, c_ctx, w_mod, b_mod, w_in, rwkv_conv, rwkv_w0, rwkv_w_up, rwkv_a0,
                              rwkv_a_up, rwkv_g_up, rwkv_k_k, rwkv_k_a, rwkv_r_k, rwkv_ln_g, rwkv_ln_b, gla_a_up,
                              gla_a_bias, gla_norm_g, s5_lam_re, s5_lam_im, s5_log_dt, s5_b_re, s5_b_im, s5_c_re,
                              s5_c_im, s5_d, s5_glu_w, s5_glu_b, w_out, ln1_g, ln1_b, router_w, router_bias,
                              exp_w13, exp_w2, sh_w13, sh_w2, ln2_g, ln2_b)))
    B, T, D = x.shape
    TC = ctx.shape[1]
    L = w_mod.shape[0]
    alpha = (2 * L) ** 0.25
    n_lat, n_ctx = B * T, B * TC
    R = ((B + 1 + SUB - 1) // SUB) * SUB
    cc = jnp.zeros((R, D), F32).at[0:B].set(c).at[B].set(c_ctx)
    mod = mod_table(cc, w_mod, b_mod)
    tm = _tile(T, 512)
    tmc = _tile(n_ctx, 512)
    tmm = min(_tile(T, 128), _tile(n_ctx, 128))
    lat_row = lambda i: (i * tm) // T
    ctx_row = lambda i: B
    lat = (x.reshape(n_lat, D), 0, n_lat)
    con = (ctx.reshape(n_ctx, D), 0, n_ctx)
    for l in range(L):
        last = l == L - 1
        lp = _layer_params(l, p)
        mod3 = mod[l].reshape(R, 6, D)
        outs_c, st_c = token_mixers(con, B, TC, TC, mod3, ctx_row, lp, zero_states(B), tmc)
        outs, _ = token_mixers(lat, B, T, GRID_W, mod3, lat_row, lp, st_c, tm)
        if last:
            x1, h2, h2t = mix_out(*outs, lat, mod3, lp['mix'], lat_row, tm, alpha)
            out = moe_ffn(h2, h2t, x1, mod3, lp['moe'], lambda i: (i * tmm) // T, tmm, alpha)
            lat = (out, 0, n_lat)
        else:
            n_all = n_ctx + n_lat
            bufs = mix_out(*outs_c, con, mod3, lp['mix'], ctx_row, tmc, alpha, out_rows=n_all)
            x1, h2, h2t = mix_out(*outs, lat, mod3, lp['mix'], lat_row, tm, alpha,
                                  out_rows=n_all, out_row0=n_ctx, into=bufs)
            row_all = lambda i: jnp.where(i * tmm < n_ctx, B, (i * tmm - n_ctx) // T)
            out = moe_ffn(h2, h2t, x1, mod3, lp['moe'], row_all, tmm, alpha)
            con, lat = (out, 0, n_ctx), (out, n_ctx, n_lat)
    out, row0, _ = lat
    return out[row0:row0 + n_lat].reshape(B, T, D)
```

```python
import functools
import math

import numpy as np
import jax
import jax.numpy as jnp
from jax import lax
from jax.experimental import pallas as pl
from jax.experimental.pallas import tpu as pltpu

F32 = jnp.float32
BF16 = jnp.bfloat16
I32 = jnp.int32
ACT = BF16

GRID_W = 64
RWKV_HEADS = 6
HEAD = 64
RW = RWKV_HEADS * HEAD
DECAY_RANK = 32
ICLR_RANK = 32
GATE_RANK = 64
RWKV_GN_EPS = 64e-5
GLA_HEADS = 6
GLA_K = 32
GLA_RANK = 16
GLA_TEMP = 16.0
S5_GROUPS = 16
S5_GROUP = 16
S5_STATE = 64
S5_W = S5_GROUPS * S5_GROUP
S5_LANES = 2 * S5_GROUPS * S5_STATE
N_EXPERTS = 256
TOP_K = 8
N_EGROUPS = 8
TOPK_GROUPS = 4
ROUTE_SCALE = 2.5
LN_EPS = 1e-6
CHUNK = 64
MOE_BLK = 256
LANE = 128
SUB = 8
VMEM_LIMIT = 56 * 1024 * 1024

PA_W = 3 * RW + 2 * LANE
PBG_W = 256 + 256 + RW + LANE
NP_W = PA_W + PBG_W + RW + S5_W


def _cp(sem):
    return pltpu.CompilerParams(dimension_semantics=sem, vmem_limit_bytes=VMEM_LIMIT)


def _sigmoid(x):
    return 1.0 / (1.0 + jnp.exp(-x))


def _silu(x):
    return x * _sigmoid(x)


def _bdot(a, b, dims=None):
    a = a.astype(BF16)
    b = b.astype(BF16)
    if dims is None:
        return jnp.dot(a, b, preferred_element_type=F32)
    return lax.dot_general(a, b, dims, preferred_element_type=F32)


def _split2(x):
    hi = x.astype(BF16)
    lo = (x - hi.astype(F32)).astype(BF16)
    return hi, lo


def _split3(x):
    hi = x.astype(BF16)
    r = x - hi.astype(F32)
    mid = r.astype(BF16)
    lo = (r - mid.astype(F32)).astype(BF16)
    return hi, mid, lo


NT_DIMS = (((1,), (1,)), ((), ()))
TN_DIMS = (((0,), (0,)), ((), ()))


def _dot33(a, b, dims=None):
    ah, al = _split2(a)
    bh, bl = _split2(b)
    return _bdot(ah, bh, dims) + (_bdot(ah, bl, dims) + _bdot(al, bh, dims))


def _dot_exact_lhs(m_exact, x, dims=None):
    h, m, l = _split3(x)
    return _bdot(m_exact, h, dims) + (_bdot(m_exact, m, dims) + _bdot(m_exact, l, dims))


def _dot_exact_rhs(x, m_exact, dims=None):
    h, m, l = _split3(x)
    return _bdot(h, m_exact, dims) + (_bdot(m, m_exact, dims) + _bdot(l, m_exact, dims))


def _dot_split_rhs(x, m_exact, dims=None):
    h, l = _split2(x)
    return _bdot(h, m_exact, dims) + _bdot(l, m_exact, dims)


def _layer_norm(x):
    mu = jnp.mean(x, axis=-1, keepdims=True)
    xc = x - mu
    var = jnp.mean(xc * xc, axis=-1, keepdims=True)
    return xc * lax.rsqrt(var + LN_EPS)


def _mod_kernel(c_ref, w_ref, b_ref, o_ref):
    s = _silu(c_ref[...])
    o_ref[0] = _dot33(s, w_ref[0]) + b_ref[0]


def mod_table(cc, w_mod, b_mod):
    L, D, D6 = w_mod.shape
    R = cc.shape[0]
    tn = 1536
    return pl.pallas_call(
        _mod_kernel,
        out_shape=jax.ShapeDtypeStruct((L, R, D6), F32),
        grid=(L, D6 // tn),
        in_specs=[pl.BlockSpec((R, D), lambda l, j: (0, 0)),
                  pl.BlockSpec((1, D, tn), lambda l, j: (l, 0, j)),
                  pl.BlockSpec((1, 1, tn), lambda l, j: (l, 0, j))],
        out_specs=pl.BlockSpec((1, R, tn), lambda l, j: (l, 0, j)),
        compiler_params=_cp(("arbitrary", "arbitrary")),
        name="mod_table",
    )(cc, w_mod, b_mod.reshape(L, 1, D6))


def _inproj_kernel(x_ref, mod_ref, w_ref, pa_ref, pbg_ref, gg_ref, u_ref):
    x = x_ref[...]
    h = _layer_norm(x) * (1.0 + mod_ref[0, 1:2, :]) + mod_ref[0, 0:1, :]
    hb = h.astype(BF16)
    o = 0
    for ref in (pa_ref, pbg_ref, gg_ref, u_ref):
        w = ref.shape[-1]
        ref[...] = jnp.dot(hb, w_ref[:, o:o + w], preferred_element_type=F32).astype(ref.dtype)
        o += w


def inproj(stream, mod3, w_bf, row_of_tile, tm):
    x2, row0, N = stream
    D = x2.shape[1]
    t0 = row0 // tm
    assert row0 % tm == 0 and N % tm == 0
    return pl.pallas_call(
        _inproj_kernel,
        out_shape=[jax.ShapeDtypeStruct((N, PA_W), ACT),
                   jax.ShapeDtypeStruct((N, PBG_W), ACT),
                   jax.ShapeDtypeStruct((N, RW), ACT),
                   jax.ShapeDtypeStruct((N, S5_W), ACT)],
        grid=(N // tm,),
        in_specs=[pl.BlockSpec((tm, D), lambda i: (i + t0, 0)),
                  pl.BlockSpec((1, 6, D), lambda i: (row_of_tile(i), 0, 0)),
                  pl.BlockSpec((D, NP_W), lambda i: (0, 0))],
        out_specs=[pl.BlockSpec((tm, PA_W), lambda i: (i, 0)),
                   pl.BlockSpec((tm, PBG_W), lambda i: (i, 0)),
                   pl.BlockSpec((tm, RW), lambda i: (i, 0)),
                   pl.BlockSpec((tm, S5_W), lambda i: (i, 0))],
        compiler_params=_cp(("arbitrary",)),
        name="inproj",
    )(x2, mod3, w_bf)


CONV_PAD = 72


def _conv_kernel(x_ref, taps_ref, o_ref, buf_ref, *, T, W, vertical):
    pad = CONV_PAD
    zeros = jnp.zeros((pad, LANE), F32)
    buf_ref[0:pad, :] = zeros
    buf_ref[pad + T:pad + T + pad, :] = zeros
    buf_ref[pad:pad + T, :] = x_ref[0].astype(F32)
    ch = min(T, 256)
    col = lax.broadcasted_iota(I32, (ch, LANE), 0) & (W - 1)
    left_ok = col >= 1
    right_ok = col <= W - 2
    for c in range(T // ch):
        base = pad + c * ch
        acc = jnp.zeros((ch, LANE), F32)
        for dr in ((0, 1, 2) if vertical else (1,)):
            for dc in range(3):
                off = (dr - 1) * W + (dc - 1)
                v = buf_ref[base + off:base + off + ch, :]
                if dc == 0:
                    v = jnp.where(left_ok, v, 0.0)
                elif dc == 2:
                    v = jnp.where(right_ok, v, 0.0)
                acc = acc + v * taps_ref[3 * dr + dc:3 * dr + dc + 1, :]
        o_ref[0, c * ch:(c + 1) * ch, :] = acc.astype(o_ref.dtype)


def short_conv(pa3, taps9, W):
    B, T, C = pa3.shape
    vertical = T > W
    assert W & (W - 1) == 0 and (not vertical or W + 1 <= CONV_PAD)
    kern = functools.partial(_conv_kernel, T=T, W=W, vertical=vertical)
    return pl.pallas_call(
        kern,
        out_shape=jax.ShapeDtypeStruct((B, T, C), ACT),
        grid=(B, C // LANE),
        in_specs=[pl.BlockSpec((1, T, LANE), lambda b, j: (b, 0, j)),
                  pl.BlockSpec((9, LANE), lambda b, j: (0, j))],
        out_specs=pl.BlockSpec((1, T, LANE), lambda b, j: (b, 0, j)),
        scratch_shapes=[pltpu.VMEM((T + 2 * CONV_PAD, LANE), F32)],
        compiler_params=_cp(("arbitrary", "arbitrary")),
        name="short_conv",
    )(pa3, taps9)


def _rwkv_prep_kernel(pa_ref, wup_ref, aup_ref, gup_ref, w0_ref, a0_ref, kk_ref, ka_ref, rk_ref, hb_ref,
                      ss_ref, lw_ref, kb_ref, gb_ref):
    r = pa_ref[:, 0:RW].astype(F32)
    k = pa_ref[:, RW:2 * RW].astype(F32)
    v = pa_ref[:, 2 * RW:3 * RW].astype(F32)
    wa = pa_ref[:, 3 * RW:3 * RW + LANE].astype(F32)
    gd = pa_ref[:, 3 * RW + LANE:3 * RW + 2 * LANE].astype(F32)
    z = w0_ref[...] + _bdot(jnp.tanh(wa), wup_ref[...])
    lw = -_sigmoid(z) * math.exp(-0.5)
    a = _sigmoid(a0_ref[...] + _bdot(wa, aup_ref[...]))
    g = _bdot(_sigmoid(gd), gup_ref[...])
    hb = hb_ref[...]
    kk = k * kk_ref[...]
    kk = kk * lax.rsqrt(_bdot(kk * kk, hb) + 1e-12)
    ka = ka_ref[...]
    ss_ref[:, 0:RW] = r.astype(ACT)
    ss_ref[:, RW:2 * RW] = v.astype(ACT)
    ss_ref[:, 2 * RW:3 * RW] = kk.astype(ACT)
    rk2 = jnp.zeros_like(r)
    for d in range(2):
        ad = a[:, d * RW:(d + 1) * RW]
        k2 = k * (1.0 + (ad - 1.0) * ka)
        lw_ref[d] = lw[:, d * RW:(d + 1) * RW]
        kb_ref[d, :, 0:RW] = k2.astype(ACT)
        kb_ref[d, :, RW:2 * RW] = (kk * ad).astype(ACT)
        rk2 = rk2 + r * k2
    bonus = _bdot(rk2 * rk_ref[...], hb) * v
    gb_ref[:, 0:RW] = g.astype(ACT)
    gb_ref[:, RW:2 * RW] = bonus.astype(ACT)


def rwkv_prep(pa2, wts, tm):
    N = pa2.shape[0]
    full = lambda a: pl.BlockSpec(a.shape, lambda i: (0,) * a.ndim)
    return pl.pallas_call(
        _rwkv_prep_kernel,
        out_shape=[jax.ShapeDtypeStruct((N, 3 * RW), ACT),
                   jax.ShapeDtypeStruct((2, N, RW), F32),
                   jax.ShapeDtypeStruct((2, N, 2 * RW), ACT),
                   jax.ShapeDtypeStruct((N, 2 * RW), ACT)],
        grid=(N // tm,),
        in_specs=[pl.BlockSpec((tm, PA_W), lambda i: (i, 0))] + [full(a) for a in wts],
        out_specs=[pl.BlockSpec((tm, 3 * RW), lambda i: (i, 0)),
                   pl.BlockSpec((2, tm, RW), lambda i: (0, i, 0)),
                   pl.BlockSpec((2, tm, 2 * RW), lambda i: (0, i, 0)),
                   pl.BlockSpec((tm, 2 * RW), lambda i: (i, 0))],
        compiler_params=_cp(("arbitrary",)),
        name="rwkv_prep",
    )(pa2, *wts)


RWKV_GROUP = 8


def _rwkv_scan_kernel(ss_ref, lw_ref, kb_ref, s0_ref, y_ref, sfin_ref, s_scr, *, nsteps, group, reverse):
    n = pl.program_id(1)
    C = CHUNK
    P = 2 * C
    npair = RW // LANE

    @pl.when(n == 0)
    def _():
        s_scr[...] = s0_ref[0]

    row = lax.broadcasted_iota(I32, (P, P), 0)
    col = lax.broadcasted_iota(I32, (P, P), 1)
    same = (row >> 6) == (col >> 6)
    dlt = (col & (C - 1)) - (row & (C - 1)) if reverse else (row & (C - 1)) - (col & (C - 1))
    strict = same & (dlt > 0)
    incl = same & (dlt >= 0)
    eye = (row == col).astype(F32)
    lvl_masks = [((row >> (lvl + 1)) == (col >> (lvl + 1))) & ((row >> lvl) != (col >> lvl)) for lvl in range(6)]
    rc = lax.broadcasted_iota(I32, (C, C), 0)
    cc = lax.broadcasted_iota(I32, (C, C), 1)
    tri = jnp.where((cc >= rc) if reverse else (rc >= cc), 1.0, 0.0).astype(BF16)
    head0 = lax.broadcasted_iota(I32, (C, LANE), 1) < HEAD

    def stack(x):
        return jnp.concatenate([jnp.where(head0, x, 0.0), jnp.where(head0, 0.0, x)], axis=0)

    pairs = range(npair)
    tm_ = {}

    def independent_stages(chunks):
        streams = [(g, p) for g in chunks for p in pairs]

        def st_cumsum():
            for (g, p) in streams:
                lw = lw_ref[0, g * C:(g + 1) * C, p * LANE:(p + 1) * LANE]
                tm_[(g, p)] = dict(lw=lw, cl=_dot_exact_lhs(tri, lw))

        def st_scores():
            for (g, p) in streams:
                t = tm_[(g, p)]
                t0, t1 = g * C, (g + 1) * C
                lo, hi = p * LANE, (p + 1) * LANE
                r = ss_ref[t0:t1, lo:hi].astype(F32)
                v = ss_ref[t0:t1, RW + lo:RW + hi].astype(F32)
                kk = ss_ref[t0:t1, 2 * RW + lo:2 * RW + hi].astype(F32)
                k2 = kb_ref[0, t0:t1, lo:hi].astype(F32)
                b = kb_ref[0, t0:t1, RW + lo:RW + hi].astype(F32)
                cl, lw = t['cl'], t['lw']
                t['ptot'] = jnp.exp(jnp.sum(lw, axis=0, keepdims=True))
                pinv = jnp.exp(-cl)
                left = jnp.concatenate([stack(-kk * jnp.exp(cl - lw)), stack(r * jnp.exp(cl))], axis=0)
                right = jnp.concatenate([stack(b * pinv), stack(k2 * pinv)], axis=0)
                t['v_st'] = stack(v)
                t['left'] = left.astype(BF16)
                t['bk'] = (right * t['ptot']).astype(BF16)
                aa = _bdot(left, right, NT_DIMS)
                t['nmat'] = jnp.where(strict, aa[0:P, 0:P], 0.0)
                t['a_ak'] = jnp.where(strict, aa[0:P, P:2 * P], 0.0)
                t['a_rbk'] = jnp.concatenate([jnp.where(incl, aa[P:2 * P, 0:P], 0.0),
                                              jnp.where(incl, aa[P:2 * P, P:2 * P], 0.0)], axis=1).astype(BF16)
                t['tinv'] = eye + jnp.where(lvl_masks[0], t['nmat'], 0.0)

        def st_akv():
            for sk in streams:
                t = tm_[sk]
                t['akv'] = _bdot(t['a_ak'], t['v_st'])

        def st_et(m):
            def run():
                for sk in streams:
                    t = tm_[sk]
                    t['et'] = _bdot(jnp.where(m, t['nmat'], 0.0), t['tinv'])
            return run

        def st_tinv():
            for sk in streams:
                t = tm_[sk]
                t['tinv'] = t['tinv'] + _bdot(t['tinv'], t['et'])

        stages = [st_cumsum, st_scores, st_akv]
        for m in lvl_masks[1:]:
            stages += [st_et(m), st_tinv]
        return stages

    state = [s_scr[p] for p in pairs]
    carry = {}

    def dependent_stages(chunks):
        stages = []
        for g in chunks:
            def st_as(g=g):
                carry['a_s'] = [_bdot(tm_[(g, p)]['left'], state[p], NT_DIMS) for p in pairs]

            def st_u(g=g):
                u = [_bdot(tm_[(g, p)]['tinv'], carry['a_s'][p][0:P] + tm_[(g, p)]['akv']) for p in pairs]
                carry['uv'] = [jnp.concatenate([u[p], tm_[(g, p)]['v_st']], axis=0) for p in pairs]

            def st_state(g=g):
                for p in pairs:
                    t = tm_[(g, p)]
                    state[p] = state[p] * t['ptot'] + _bdot(carry['uv'][p], t['bk'], TN_DIMS)

            def st_y(g=g, a_s=None):
                for p in pairs:
                    y_st = carry['a_s'][p][P:2 * P] + _bdot(tm_[(g, p)]['a_rbk'], carry['uv'][p])
                    y_ref[g * C:(g + 1) * C, p * LANE:(p + 1) * LANE] = (y_st[0:C] + y_st[C:P]).astype(y_ref.dtype)

            stages += [st_as, st_u, st_y, st_state]
        return stages

    order = list(range(group - 1, -1, -1) if reverse else range(group))
    half = group // 2 if group >= 8 else group
    wave1, wave2 = order[:half], order[half:]
    for stage in independent_stages(wave1):
        stage()
    dep1 = dependent_stages(wave1)
    ind2 = independent_stages(wave2) if wave2 else []
    for i in range(max(len(dep1), len(ind2))):
        if i < len(ind2):
            ind2[i]()
        if i < len(dep1):
            dep1[i]()
    for stage in dependent_stages(wave2):
        stage()
    for p in pairs:
        s_scr[p] = state[p]

    @pl.when(n == nsteps - 1)
    def _():
        sfin_ref[0] = s_scr[...]


def rwkv_scan_dir(ss, lw, kb, s0, B, T, d):
    nc = T // CHUNK
    group = min(RWKV_GROUP, nc)
    nsteps = nc // group
    npair = RW // LANE
    rows = group * CHUNK
    blk = (lambda b, n: b * nsteps + nsteps - 1 - n) if d else (lambda b, n: b * nsteps + n)
    kern = functools.partial(_rwkv_scan_kernel, nsteps=nsteps, group=group, reverse=bool(d))
    return pl.pallas_call(
        kern,
        out_shape=[jax.ShapeDtypeStruct((B * T, RW), ACT),
                   jax.ShapeDtypeStruct((B, npair, LANE, LANE), F32)],
        grid=(B, nsteps),
        in_specs=[pl.BlockSpec((rows, 3 * RW), lambda b, n: (blk(b, n), 0)),
                  pl.BlockSpec((1, rows, RW), lambda b, n: (d, blk(b, n), 0)),
                  pl.BlockSpec((1, rows, 2 * RW), lambda b, n: (d, blk(b, n), 0)),
                  pl.BlockSpec((1, npair, LANE, LANE), lambda b, n: (b, 0, 0, 0))],
        out_specs=[pl.BlockSpec((rows, RW), lambda b, n: (blk(b, n), 0)),
                   pl.BlockSpec((1, npair, LANE, LANE), lambda b, n: (b, 0, 0, 0))],
        scratch_shapes=[pltpu.VMEM((npair, LANE, LANE), F32)],
        compiler_params=_cp(("arbitrary", "arbitrary")),
        name="rwkv_scan_bwd" if d else "rwkv_scan_fwd",
    )(ss, lw, kb, s0)


def rwkv_scan(ss, lw, kb, s0, B, T):
    y0, f0 = rwkv_scan_dir(ss, lw, kb, s0[0], B, T, 0)
    y1, f1 = rwkv_scan_dir(ss, lw, kb, s0[1], B, T, 1)
    return (y0, y1), (f0, f1)


GLA_KP = 256


GLA_GROUP = 8


def _gla_kernel(pbg_ref, aup_ref, ab_ref, s0_ref, o_ref, sfin_ref, s_scr, *, nsteps, group, reverse):
    n = pl.program_id(1)
    C = CHUNK
    R = group * C

    @pl.when(n == 0)
    def _():
        s_scr[...] = s0_ref[0]

    q = pbg_ref[:, 0:GLA_KP].astype(F32)
    k = pbg_ref[:, GLA_KP:2 * GLA_KP].astype(F32)
    v = pbg_ref[:, 2 * GLA_KP:2 * GLA_KP + RW].astype(F32)
    ad = pbg_ref[:, 2 * GLA_KP + RW:2 * GLA_KP + RW + LANE].astype(F32)
    x = _dot33(ad, aup_ref[0]) + ab_ref[0]
    la = (jnp.minimum(x, 0.0) - jnp.log(1.0 + jnp.exp(-jnp.abs(x)))) * (1.0 / GLA_TEMP)
    chunks = range(group)
    sl = [slice(g * C, (g + 1) * C) for g in chunks]
    rr = lax.broadcasted_iota(I32, (C, C), 0)
    rc = lax.broadcasted_iota(I32, (C, C), 1)
    tri = jnp.where((rc >= rr) if reverse else (rr >= rc), 1.0, 0.0).astype(BF16)
    bcums = [_dot_exact_lhs(tri, la[sl[g]]) for g in chunks]
    last = 0 if reverse else C - 1
    tots = [bc[last:last + 1, :] for bc in bcums]
    bcum = jnp.concatenate(bcums, axis=0)
    tot = jnp.concatenate([jnp.broadcast_to(t, (C, GLA_KP)) for t in tots], axis=0)
    q_in = q * jnp.exp(bcum) * (GLA_K ** -0.5)
    k_in = k * jnp.exp(-bcum)
    k_st = k * jnp.exp(tot - bcum)
    dn = [jnp.exp(t) for t in tots]
    klane = lax.broadcasted_iota(I32, (C, GLA_KP), 1)
    rt = lax.broadcasted_iota(I32, (GLA_HEADS * C, C), 0) & (C - 1)
    ct = lax.broadcasted_iota(I32, (GLA_HEADS * C, C), 1)
    causal = (ct >= rt) if reverse else (rt >= ct)
    vlane = lax.broadcasted_iota(I32, (C, RW), 1)
    sv = lax.broadcasted_iota(I32, (RW, GLA_KP), 0) >> 6
    sk = lax.broadcasted_iota(I32, (RW, GLA_KP), 1) >> 5
    q_rows = [jnp.concatenate([jnp.where((klane >> 5) == h, q_in[sl[g]], 0.0) for h in range(GLA_HEADS)],
                              axis=0).astype(BF16) for g in chunks]
    att = [jnp.where(causal, _bdot(q_rows[g], k_in[sl[g]], NT_DIMS), 0.0) for g in chunks]
    o_rows = [_bdot(att[g], v[sl[g]]) for g in chunks]
    kv = [jnp.where(sv == sk, _bdot(v[sl[g]], k_st[sl[g]], TN_DIMS), 0.0) for g in chunks]
    s = s_scr[...]
    for g in (reversed(chunks) if reverse else chunks):
        o = _bdot(q_in[sl[g]], s, NT_DIMS)
        for h in range(GLA_HEADS):
            o = o + jnp.where((vlane >> 6) == h, o_rows[g][h * C:(h + 1) * C], 0.0)
        o_ref[sl[g], :] = o.astype(o_ref.dtype)
        s = s * dn[g] + kv[g]
    s_scr[...] = s

    @pl.when(n == nsteps - 1)
    def _():
        sfin_ref[0] = s_scr[...]


def gla_scan_dir(pbg, aup, ab, s0, B, T, d):
    nc = T // CHUNK
    group = min(GLA_GROUP, nc)
    nsteps = nc // group
    rows = group * CHUNK
    blk = (lambda b, n: b * nsteps + nsteps - 1 - n) if d else (lambda b, n: b * nsteps + n)
    kern = functools.partial(_gla_kernel, nsteps=nsteps, group=group, reverse=bool(d))
    return pl.pallas_call(
        kern,
        out_shape=[jax.ShapeDtypeStruct((B * T, RW), ACT),
                   jax.ShapeDtypeStruct((B, RW, GLA_KP), F32)],
        grid=(B, nsteps),
        in_specs=[pl.BlockSpec((rows, PBG_W), lambda b, n: (blk(b, n), 0)),
                  pl.BlockSpec((1, LANE, GLA_KP), lambda b, n: (d, 0, 0)),
                  pl.BlockSpec((1, 1, GLA_KP), lambda b, n: (d, 0, 0)),
                  pl.BlockSpec((1, RW, GLA_KP), lambda b, n: (b, 0, 0))],
        out_specs=[pl.BlockSpec((rows, RW), lambda b, n: (blk(b, n), 0)),
                   pl.BlockSpec((1, RW, GLA_KP), lambda b, n: (b, 0, 0))],
        scratch_shapes=[pltpu.VMEM((RW, GLA_KP), F32)],
        compiler_params=_cp(("arbitrary", "arbitrary")),
        name="gla_scan_bwd" if d else "gla_scan_fwd",
    )(pbg, aup, ab, s0)


def gla_scan(pbg, aup, ab, s0, B, T):
    o0, f0 = gla_scan_dir(pbg, aup, ab, s0[0], B, T, 0)
    o1, f1 = gla_scan_dir(pbg, aup, ab, s0[1], B, T, 1)
    return (o0, o1), (f0, f1)


S5_PITCH = 68


def _s5_kernel(u_ref, a_ref, bbd_ref, cbd_ref, s0_ref, y_ref, sfin_ref, x_scr, st_scr, *, nc, nb):
    d = pl.program_id(0)
    n = pl.program_id(1)
    C = CHUNK
    half = S5_LANES // 2

    @pl.when(n == 0)
    def _():
        st_scr[...] = s0_ref[0]

    nslab = S5_LANES // LANE
    bbd = bbd_ref[0]
    for b in range(nb):
        bu = _bdot(u_ref[b], bbd)
        for j in range(nslab):
            x_scr[j, b * S5_PITCH:b * S5_PITCH + C, :] = bu[:, j * LANE:(j + 1) * LANE]
    a_re = a_ref[0, :, 0:half]
    a_im = a_ref[0, :, half:S5_LANES]

    def step(i, carry):
        re, im = carry
        t = i + d * (C - 1 - 2 * i)
        rows = pl.ds(t, nb, stride=S5_PITCH)
        bu = jnp.concatenate([x_scr[j, rows, :] for j in range(nslab)], axis=1)
        nre = a_re * re - a_im * im + bu[:, 0:half]
        nim = a_re * im + a_im * re + bu[:, half:S5_LANES]
        for j in range(nslab // 2):
            x_scr[j, rows, :] = nre[:, j * LANE:(j + 1) * LANE]
            x_scr[nslab // 2 + j, rows, :] = nim[:, j * LANE:(j + 1) * LANE]
        return nre, nim

    st = st_scr[...]
    re, im = lax.fori_loop(0, C, step, (st[:, 0:half], st[:, half:S5_LANES]), unroll=4)
    st_scr[...] = jnp.concatenate([re, im], axis=1)
    cbd = cbd_ref[0]
    for b in range(nb):
        xs = jnp.concatenate([x_scr[j, b * S5_PITCH:b * S5_PITCH + C, :] for j in range(nslab)], axis=1)
        y_ref[0, b] = _bdot(xs, cbd).astype(y_ref.dtype)

    @pl.when(n == nc - 1)
    def _():
        sfin_ref[0] = st_scr[...]


def s5_scan(u3, a_bar, bbd, cbd, s0):
    B, T, _ = u3.shape
    nc = T // CHUNK
    chunk = lambda d, n: n + d * (nc - 1 - 2 * n)
    kern = functools.partial(_s5_kernel, nc=nc, nb=B)
    return pl.pallas_call(
        kern,
        out_shape=[jax.ShapeDtypeStruct((2, B, T, S5_W), ACT),
                   jax.ShapeDtypeStruct((2, B, S5_LANES), F32)],
        grid=(2, nc),
        in_specs=[pl.BlockSpec((B, CHUNK, S5_W), lambda d, n: (0, chunk(d, n), 0)),
                  pl.BlockSpec((1, 1, S5_LANES), lambda d, n: (d, 0, 0)),
                  pl.BlockSpec((1, S5_W, S5_LANES), lambda d, n: (d, 0, 0)),
                  pl.BlockSpec((1, S5_LANES, S5_W), lambda d, n: (d, 0, 0)),
                  pl.BlockSpec((1, B, S5_LANES), lambda d, n: (d, 0, 0))],
        out_specs=[pl.BlockSpec((1, B, CHUNK, S5_W), lambda d, n: (d, 0, chunk(d, n), 0)),
                   pl.BlockSpec((1, B, S5_LANES), lambda d, n: (d, 0, 0))],
        scratch_shapes=[pltpu.VMEM((S5_LANES // LANE, B * S5_PITCH, LANE), F32),
                        pltpu.VMEM((B, S5_LANES), F32)],
        compiler_params=_cp(("arbitrary", "arbitrary")),
        name="s5_scan",
    )(u3, a_bar, bbd, cbd, s0)


ROWT = 4
U32 = jnp.uint32


def _pack_bf16_pair(a, b):
    au = lax.bitcast_convert_type(a.astype(BF16).astype(F32), U32)
    bu = lax.bitcast_convert_type(b.astype(BF16).astype(F32), U32)
    return (au >> 16) | bu


def _unpack_bf16_pair(w):
    return (lax.bitcast_convert_type(w << 16, F32), lax.bitcast_convert_type(w & jnp.uint32(0xFFFF0000), F32))


def _tile_rows_store(ref, val):
    tm = val.shape[0]
    assert val.shape[1] == 2 * ROWT * LANE
    for s in range(ROWT):
        ref[pl.ds(s, tm, stride=ROWT), :] = _pack_bf16_pair(val[:, 2 * s * LANE:(2 * s + 1) * LANE],
                                                            val[:, (2 * s + 1) * LANE:(2 * s + 2) * LANE])


def _tile_rows_load(ref, tm, base=0):
    parts = []
    for s in range(ROWT):
        parts.extend(_unpack_bf16_pair(ref[pl.ds(base + s, tm, stride=ROWT), :]))
    return jnp.concatenate(parts, axis=1)


def _mix_out_kernel(y0_ref, y1_ref, gb_ref, o0_ref, o1_ref, gg_ref, yd_ref, u_ref, x_ref, mod_ref,
                    hb_ref, lng_ref, lnb_ref, gng_ref, s5d_ref, gluw_ref, glub_ref, wout_ref, l1g_ref, l1b_ref,
                    *rest, alpha):
    x1_ref, h2_ref, h2t_ref = rest[-3:]
    hb = hb_ref[...]
    inv = 1.0 / HEAD
    f32 = lambda ref_val: ref_val.astype(F32)
    y = f32(y0_ref[...]) + f32(y1_ref[...])
    yc = y - _dot_split_rhs(y, hb) * inv
    var = _bdot(yc * yc, hb) * inv
    gn = yc * lax.rsqrt(var + RWKV_GN_EPS) * lng_ref[...] + lnb_ref[...]
    y_a = (gn + f32(gb_ref[:, RW:2 * RW])) * f32(gb_ref[:, 0:RW])
    o = f32(o0_ref[...]) + f32(o1_ref[...])
    o = o * lax.rsqrt(_bdot(o * o, hb) * inv + 1e-6) * gng_ref[...]
    y_b = o * _silu(f32(gg_ref[...]))
    c = s5d_ref[...] * f32(u_ref[...]) + f32(yd_ref[0]) + f32(yd_ref[1])
    c = 0.5 * c * (1.0 + jnp.tanh(math.sqrt(2.0 / math.pi) * (c + 0.044715 * (c * c * c))))
    y_c = c * _sigmoid(_bdot(c, gluw_ref[...]) + glub_ref[...])
    y_mix = (_bdot(y_a, wout_ref[0:RW, :]) + _bdot(y_b, wout_ref[RW:2 * RW, :])
             + _bdot(y_c, wout_ref[2 * RW:2 * RW + S5_W, :]))
    x1 = _layer_norm(alpha * x_ref[...] + mod_ref[0, 2:3, :] * y_mix) * l1g_ref[...] + l1b_ref[...]
    x1_ref[...] = x1
    h2 = _layer_norm(x1) * (1.0 + mod_ref[0, 4:5, :]) + mod_ref[0, 3:4, :]
    h2_ref[...] = h2
    _tile_rows_store(h2t_ref, h2)


def mix_out(y01, gb, o01, gg, yd, u, stream, mod3, wts, row_of_tile, tm, alpha, out_rows=None, out_row0=0, into=None):
    x2, row0, N = stream
    D = x2.shape[1]
    out_rows = N if out_rows is None else out_rows
    assert row0 % tm == 0 and out_row0 % tm == 0 and N % tm == 0
    t0, o0 = row0 // tm, out_row0 // tm
    full = lambda a: pl.BlockSpec(a.shape, lambda i: (0,) * a.ndim)
    kern = functools.partial(_mix_out_kernel, alpha=alpha)
    ins = [y01[0], y01[1], gb, o01[0], o01[1], gg, yd, u, x2, mod3, *wts]
    extra = [] if into is None else list(into)
    return pl.pallas_call(
        kern,
        out_shape=[jax.ShapeDtypeStruct((out_rows, D), F32),
                   jax.ShapeDtypeStruct((out_rows, D), F32),
                   jax.ShapeDtypeStruct((out_rows * ROWT, LANE), U32)],
        grid=(N // tm,),
        in_specs=[pl.BlockSpec((tm, RW), lambda i: (i, 0)),
                  pl.BlockSpec((tm, RW), lambda i: (i, 0)),
                  pl.BlockSpec((tm, 2 * RW), lambda i: (i, 0)),
                  pl.BlockSpec((tm, RW), lambda i: (i, 0)),
                  pl.BlockSpec((tm, RW), lambda i: (i, 0)),
                  pl.BlockSpec((tm, RW), lambda i: (i, 0)),
                  pl.BlockSpec((2, tm, S5_W), lambda i: (0, i, 0)),
                  pl.BlockSpec((tm, S5_W), lambda i: (i, 0)),
                  pl.BlockSpec((tm, D), lambda i: (i + t0, 0)),
                  pl.BlockSpec((1, 6, D), lambda i: (row_of_tile(i), 0, 0))] + [full(a) for a in wts]
                 + [pl.BlockSpec(memory_space=pl.ANY)] * len(extra),
        out_specs=[pl.BlockSpec((tm, D), lambda i: (i + o0, 0)),
                   pl.BlockSpec((tm, D), lambda i: (i + o0, 0)),
                   pl.BlockSpec((tm * ROWT, LANE), lambda i: (i + o0, 0))],
        input_output_aliases={len(ins) + k: k for k in range(len(extra))},
        compiler_params=_cp(("arbitrary",)),
        name="mix_out",
    )(*ins, *extra)


def _first_max(x, idx, big):
    m = jnp.max(x, axis=0, keepdims=True)
    first = jnp.min(jnp.where(x == m, idx, big), axis=0, keepdims=True)
    return m, first


def _router_kernel(h_ref, rwt_ref, bias_ref, e_ref, w_ref):
    tm = h_ref.shape[0]
    gsz = N_EXPERTS // N_EGROUPS
    ninf = -jnp.inf
    s = _sigmoid(_dot33(rwt_ref[...], h_ref[...], NT_DIMS))
    ssel = s + bias_ref[:, 0:1]
    gi = lax.broadcasted_iota(I32, (gsz, tm), 0)
    gscore = []
    for g in range(N_EGROUPS):
        xg = ssel[g * gsz:(g + 1) * gsz, :]
        m1, i1 = _first_max(xg, gi, gsz)
        m2 = jnp.max(jnp.where(gi == i1, ninf, xg), axis=0, keepdims=True)
        gscore.append(m1 + m2)
    cur = jnp.concatenate(gscore, axis=0)
    gidx = lax.broadcasted_iota(I32, (N_EGROUPS, tm), 0)
    picked = jnp.zeros((N_EGROUPS, tm), F32)
    for _ in range(TOPK_GROUPS):
        _, first = _first_max(cur, gidx, N_EGROUPS)
        hit = gidx == first
        picked = jnp.where(hit, 1.0, picked)
        cur = jnp.where(hit, ninf, cur)
    x = jnp.concatenate(
        [jnp.where(picked[g:g + 1, :] > 0.5, ssel[g * gsz:(g + 1) * gsz, :], ninf) for g in range(N_EGROUPS)], axis=0)
    ei = lax.broadcasted_iota(I32, (N_EXPERTS, tm), 0)
    idxs, ws = [], []
    for _ in range(TOP_K):
        _, first = _first_max(x, ei, N_EXPERTS)
        hit = ei == first
        idxs.append(first)
        ws.append(jnp.sum(jnp.where(hit, s, 0.0), axis=0, keepdims=True))
        x = jnp.where(hit, ninf, x)
    w = jnp.concatenate(ws, axis=0)
    e_ref[...] = jnp.concatenate(idxs, axis=0)
    w_ref[...] = w / jnp.sum(w, axis=0, keepdims=True) * ROUTE_SCALE


def moe_router(h2, rwt, bias_b, tm):
    N, D = h2.shape
    return pl.pallas_call(
        _router_kernel,
        out_shape=[jax.ShapeDtypeStruct((TOP_K, N), I32), jax.ShapeDtypeStruct((TOP_K, N), F32)],
        grid=(N // tm,),
        in_specs=[pl.BlockSpec((tm, D), lambda i: (i, 0)),
                  pl.BlockSpec((N_EXPERTS, D), lambda i: (0, 0)),
                  pl.BlockSpec((N_EXPERTS, LANE), lambda i: (0, 0))],
        out_specs=[pl.BlockSpec((TOP_K, tm), lambda i: (0, i)),
                   pl.BlockSpec((TOP_K, tm), lambda i: (0, i))],
        compiler_params=_cp(("arbitrary",)),
        name="moe_router",
    )(h2, rwt, bias_b)


def _moe_count_kernel(e_ref, pstart_ref, plan_ref, blk_ref, cnt_scr, *, nt, nbp):
    i = pl.program_id(0)
    tp = e_ref.shape[1]

    @pl.when(i == 0)
    def _():
        cnt_scr[...] = jnp.zeros_like(cnt_scr)

    ei = lax.broadcasted_iota(I32, (N_EXPERTS, tp), 0)
    acc = jnp.zeros((N_EXPERTS, 1), F32)
    for k in range(TOP_K):
        acc = acc + jnp.sum(jnp.where(ei == e_ref[k:k + 1, :], 1.0, 0.0), axis=1, keepdims=True)
    cnt_scr[...] = cnt_scr[...] + acc

    @pl.when(i == nt - 1)
    def _():
        cnt = cnt_scr[...].astype(I32)
        padded = (cnt + (MOE_BLK - 1)) & (-MOE_BLK)
        r = lax.broadcasted_iota(I32, (N_EXPERTS, N_EXPERTS), 0)
        c = lax.broadcasted_iota(I32, (N_EXPERTS, N_EXPERTS), 1)
        tri = jnp.where(c <= r, 1.0, 0.0).astype(BF16)
        padded_b = jnp.broadcast_to(padded.astype(F32), (N_EXPERTS, LANE))
        p_end = _dot_exact_lhs(tri, padded_b)
        pstart = p_end - padded_b
        pstart_ref[...] = pstart.astype(I32)
        diag = r == c
        ps_row = jnp.sum(jnp.where(diag, pstart[:, 0:1], 0.0), axis=0, keepdims=True)
        cnt_row = jnp.sum(jnp.where(diag, cnt_scr[...], 0.0), axis=0, keepdims=True)
        plan_ref[...] = jnp.concatenate([ps_row, cnt_row, jnp.zeros((SUB - 2, N_EXPERTS), F32)], axis=0).astype(I32)
        lim =(lax.broadcasted_iota(I32, (N_EXPERTS, nbp), 1) * MOE_BLK).astype(F32)
        be = jnp.sum(jnp.where(p_end[:, 0:1] <= lim, 1.0, 0.0), axis=0, keepdims=True)
        be = jnp.minimum(be, N_EXPERTS - 1.0)
        nused = jnp.max(p_end[:, 0:1], axis=0, keepdims=True) * (1.0 / MOE_BLK)
        blk_ref[...] = jnp.concatenate([jnp.broadcast_to(be, (SUB // 2, nbp)),
                                        jnp.broadcast_to(nused, (SUB // 2, nbp))], axis=0).astype(I32)


def moe_counts(eidx, tp, nbp):
    N = eidx.shape[1]
    nt = N // tp
    kern = functools.partial(_moe_count_kernel, nt=nt, nbp=nbp)
    return pl.pallas_call(
        kern,
        out_shape=[jax.ShapeDtypeStruct((N_EXPERTS, LANE), I32),
                   jax.ShapeDtypeStruct((SUB, N_EXPERTS), I32),
                   jax.ShapeDtypeStruct((SUB, nbp), I32)],
        grid=(nt,),
        in_specs=[pl.BlockSpec((TOP_K, tp), lambda i: (0, i))],
        out_specs=[pl.BlockSpec((N_EXPERTS, LANE), lambda i: (0, 0)),
                   pl.BlockSpec((SUB, N_EXPERTS), lambda i: (0, 0)),
                   pl.BlockSpec((SUB, nbp), lambda i: (0, 0))],
        scratch_shapes=[pltpu.VMEM((N_EXPERTS, 1), F32)],
        compiler_params=_cp(("arbitrary",)),
        name="moe_counts",
    )(eidx)


def _moe_dest_kernel(e_ref, pstart_ref, dest_ref, base_scr):
    i = pl.program_id(0)
    tp = e_ref.shape[1]

    @pl.when(i == 0)
    def _():
        base_scr[...] = pstart_ref[:, 0:1].astype(F32)

    ei = lax.broadcasted_iota(I32, (N_EXPERTS, tp), 0)
    r = lax.broadcasted_iota(I32, (tp, tp), 0)
    c = lax.broadcasted_iota(I32, (tp, tp), 1)
    tri = jnp.where(r <= c, 1.0, 0.0).astype(BF16)
    base = base_scr[...]
    rows = []
    for k in range(TOP_K):
        hit = ei == e_ref[k:k + 1, :]
        oh = jnp.where(hit, 1.0, 0.0)
        cum = _bdot(oh, tri)
        rows.append(jnp.sum(jnp.where(hit, cum - 1.0 + base, 0.0), axis=0, keepdims=True))
        base = base + cum[:, tp - 1:tp]
    base_scr[...] = base
    dest_ref[...] = jnp.concatenate(rows, axis=0).astype(I32)


def moe_dest(eidx, pstart, tp):
    N = eidx.shape[1]
    return pl.pallas_call(
        _moe_dest_kernel,
        out_shape=jax.ShapeDtypeStruct((TOP_K, N), I32),
        grid=(N // tp,),
        in_specs=[pl.BlockSpec((TOP_K, tp), lambda i: (0, i)),
                  pl.BlockSpec((N_EXPERTS, LANE), lambda i: (0, 0))],
        out_specs=pl.BlockSpec((TOP_K, tp), lambda i: (0, i)),
        scratch_shapes=[pltpu.VMEM((N_EXPERTS, 1), F32)],
        compiler_params=_cp(("arbitrary",)),
        name="moe_dest",
    )(eidx, pstart)


def _tile_at(ref, token):
    return ref.at[pl.ds(pl.multiple_of(token * ROWT, ROWT), ROWT), :]


SCAT_RING = 3


def _moe_scatter_kernel(dest_ref, h_ref, xs_hbm, ring, sem, *, nt):
    s = pl.program_id(0)
    tp = dest_ref.shape[1]
    slot = lax.rem(s, SCAT_RING)
    total = xs_hbm.at[pl.ds(0, tp * TOP_K * ROWT), :]

    def slot_wait(sl):
        pltpu.make_async_copy(total, total, sem.at[sl]).wait()

    @pl.when(s >= SCAT_RING)
    def _():
        slot_wait(slot)

    ring[slot] = h_ref[...]

    def issue(j, carry):
        for k in range(TOP_K):
            pltpu.make_async_copy(_tile_at(ring.at[slot], j), _tile_at(xs_hbm, dest_ref[k, j]),
                                  sem.at[slot]).start(priority=k % 2)
        return carry

    lax.fori_loop(0, tp, issue, 0)

    @pl.when(s == nt - 1)
    def _():
        for back in range(SCAT_RING - 1, -1, -1):
            if nt - 1 - back >= 0:
                slot_wait((nt - 1 - back) % SCAT_RING)


def moe_scatter(dest, h2t, n_slots, tp):
    N = dest.shape[1]
    nt = N // tp
    return pl.pallas_call(
        functools.partial(_moe_scatter_kernel, nt=nt),
        out_shape=jax.ShapeDtypeStruct((n_slots * ROWT, LANE), U32),
        grid=(nt,),
        in_specs=[pl.BlockSpec((TOP_K, tp), lambda i: (0, i), memory_space=pltpu.SMEM),
                  pl.BlockSpec((tp * ROWT, LANE), lambda i: (i, 0))],
        out_specs=pl.BlockSpec(memory_space=pl.ANY),
        scratch_shapes=[pltpu.VMEM((SCAT_RING, tp * ROWT, LANE), U32), pltpu.SemaphoreType.DMA((SCAT_RING,))],
        compiler_params=_cp(("arbitrary",)),
        name="moe_scatter",
    )(dest, h2t)


PAD_BITS = tuple(1 << b for b in reversed(range(MOE_BLK.bit_length() - 1)))


def _moe_padfill_kernel(ps_ref, xs_in, xs_hbm, zero_scr, sem):
    del xs_in
    zero_scr[...] = jnp.zeros_like(zero_scr)

    def pad_copies(e, wait):
        cnt = ps_ref[1, e]
        npad = ((cnt + (MOE_BLK - 1)) & (-MOE_BLK)) - cnt
        off = ps_ref[0, e] + cnt
        for bit in PAD_BITS:
            @pl.when((npad & bit) != 0)
            def _():
                cp = pltpu.make_async_copy(zero_scr.at[pl.ds(0, bit * ROWT), :],
                                           xs_hbm.at[pl.ds(pl.multiple_of(off * ROWT, ROWT), bit * ROWT), :], sem)
                if wait:
                    cp.wait()
                else:
                    cp.start()
            off = off + (npad & bit)

    def issue(e, carry):
        pad_copies(e, False)
        return carry

    def drain(e, carry):
        pad_copies(e, True)
        return carry

    lax.fori_loop(0, N_EXPERTS, issue, 0)
    lax.fori_loop(0, N_EXPERTS, drain, 0)


def moe_padfill(pstart, xs):
    return pl.pallas_call(
        _moe_padfill_kernel,
        out_shape=jax.ShapeDtypeStruct(xs.shape, xs.dtype),
        grid=(1,),
        in_specs=[pl.BlockSpec(memory_space=pltpu.SMEM),
                  pl.BlockSpec(memory_space=pl.ANY)],
        out_specs=pl.BlockSpec(memory_space=pl.ANY),
        scratch_shapes=[pltpu.VMEM((PAD_BITS[0] * ROWT, LANE), U32), pltpu.SemaphoreType.DMA(())],
        input_output_aliases={1: 0},
        compiler_params=_cp(("arbitrary",)),
        name="moe_padfill",
    )(pstart, xs)


EXP_NX = 8
EXP_PF = 4
EXP_NY = 4


def _experts_kernel(plan_ref, w13_ref, w2_ref, xs_hbm, y_hbm, w13_bf, w2_bf, x_buf, y_buf, sem_in, sem_out):
    e = pl.program_id(0)
    ne = pl.num_programs(0)
    ff = w2_bf.shape[0]
    rows = MOE_BLK * ROWT
    shift = MOE_BLK.bit_length() - 1
    blocks_of = lambda c: lax.shift_right_logical(c + (MOE_BLK - 1), shift)
    nblk = blocks_of(plan_ref[1, e])
    first = lax.shift_right_logical(plan_ref[0, e], shift)
    nused = lax.shift_right_logical(plan_ref[0, ne - 1], shift) + blocks_of(plan_ref[1, ne - 1])

    def window(ref, g):
        return ref.at[pl.ds(pl.multiple_of(g * rows, rows), rows), :]

    def x_copy(g):
        return pltpu.make_async_copy(window(xs_hbm, g), x_buf.at[g & (EXP_NX - 1)], sem_in.at[g & (EXP_NX - 1)])

    def y_copy(g):
        return pltpu.make_async_copy(y_buf.at[g & (EXP_NY - 1)], window(y_hbm, g), sem_out.at[g & (EXP_NY - 1)])

    @pl.when(e == 0)
    def _():
        for g in range(EXP_PF):
            @pl.when(g < nused)
            def _():
                x_copy(g).start()

    def process(gs):
        for g in gs:
            x_copy(g).wait()

            @pl.when(g + EXP_PF < nused)
            def _():
                x_copy(g + EXP_PF).start()

            @pl.when(g >= EXP_NY)
            def _():
                y_copy(g - EXP_NY).wait()

        xs = [_tile_rows_load(x_buf.at[g & (EXP_NX - 1)], MOE_BLK).astype(BF16) for g in gs]
        hs = [jnp.dot(x, w13_bf[...], preferred_element_type=F32) for x in xs]
        acts = [(_silu(h[:, 0:ff]) * h[:, ff:2 * ff]).astype(BF16) for h in hs]
        ys = [jnp.dot(a, w2_bf[...], preferred_element_type=F32) for a in acts]
        for g, y in zip(gs, ys):
            _tile_rows_store(y_buf.at[g & (EXP_NY - 1)], y)
            y_copy(g).start()

    @pl.when(nblk > 0)
    def _():
        w13_bf[...] = w13_ref[0, 0].astype(BF16)
        w2_bf[...] = w2_ref[0, 0].astype(BF16)
        npair = lax.shift_right_logical(nblk, 1)

        def body(j, carry):
            g = first + 2 * j
            process([g, g + 1])
            return carry

        lax.fori_loop(0, npair, body, 0)

        @pl.when((nblk & 1) == 1)
        def _():
            process([first + nblk - 1])

    @pl.when(e == ne - 1)
    def _():
        for back in range(EXP_NY, 0, -1):
            @pl.when(nused >= back)
            def _():
                y_copy(nused - back).wait()


def moe_experts(plan, xs, w13, w2, layer):
    _, E, D, F2 = w13.shape
    rows = MOE_BLK * ROWT
    grid_spec = pltpu.PrefetchScalarGridSpec(
        num_scalar_prefetch=1,
        grid=(E,),
        in_specs=[pl.BlockSpec((1, 1, D, F2), lambda e, p: (layer, e, 0, 0)),
                  pl.BlockSpec((1, 1, F2 // 2, D), lambda e, p: (layer, e, 0, 0)),
                  pl.BlockSpec(memory_space=pl.ANY)],
        out_specs=pl.BlockSpec(memory_space=pl.ANY),
        scratch_shapes=[pltpu.VMEM((D, F2), BF16), pltpu.VMEM((F2 // 2, D), BF16),
                        pltpu.VMEM((EXP_NX, rows, LANE), U32), pltpu.VMEM((EXP_NY, rows, LANE), U32),
                        pltpu.SemaphoreType.DMA((EXP_NX,)), pltpu.SemaphoreType.DMA((EXP_NY,))])
    return pl.pallas_call(
        _experts_kernel,
        out_shape=jax.ShapeDtypeStruct(xs.shape, U32),
        grid_spec=grid_spec,
        compiler_params=_cp(("arbitrary",)),
        name="moe_experts",
    )(plan, w13, w2, xs)


def _moe_combine_kernel(dest_ref, wt_ref, h_ref, x_ref, mod_ref, s13_ref, s2_ref, l2g_ref, l2b_ref, y_hbm,
                        o_ref, g_scr, sem, *, alpha, nt):
    s = pl.program_id(0)
    tm = h_ref.shape[0]
    ff = s2_ref.shape[0]
    rows = tm * ROWT
    slot_g = s % 2
    slot_c = 1 - slot_g

    def slot_copy(slot):
        return pltpu.make_async_copy(y_hbm.at[pl.ds(0, TOP_K * rows), :], g_scr.at[slot], sem.at[slot])

    @pl.when(s == 0)
    def _():
        g_scr[1] = jnp.zeros((TOP_K * rows, LANE), U32)

    @pl.when(s > 0)
    def _():
        slot_copy(slot_c).wait()

    for j in range(tm):
        for k in range(TOP_K):
            pltpu.make_async_copy(_tile_at(y_hbm, dest_ref[k, j]),
                                  g_scr.at[slot_g, pl.ds(k * rows + j * ROWT, ROWT), :],
                                  sem.at[slot_g]).start(priority=k % 2)
    hs = _bdot(h_ref[...], s13_ref[...])
    f = _bdot(_silu(hs[:, 0:ff]) * hs[:, ff:2 * ff], s2_ref[...])
    for k in range(TOP_K):
        f = f + wt_ref[:, k:k + 1] * _tile_rows_load(g_scr.at[slot_c], tm, base=k * rows)
    o_ref[...] = (_layer_norm(alpha * x_ref[...] + mod_ref[0, 5:6, :] * f) * l2g_ref[...] + l2b_ref[...])

    @pl.when(s == nt)
    def _():
        slot_copy(slot_g).wait()


def moe_combine(dest, wt_t, h2, x1, mod3, wts, y, row_of_tile, tm, alpha):
    N, D = h2.shape
    nt = N // tm
    full = lambda a: pl.BlockSpec(a.shape, lambda s: (0,) * a.ndim)
    kern = functools.partial(_moe_combine_kernel, alpha=alpha, nt=nt)
    nxt = lambda s: jnp.minimum(s, nt - 1)
    cur = lambda s: jnp.maximum(s - 1, 0)
    return pl.pallas_call(
        kern,
        out_shape=jax.ShapeDtypeStruct((N, D), F32),
        grid=(nt + 1,),
        in_specs=[pl.BlockSpec((TOP_K, tm), lambda s: (0, nxt(s)), memory_space=pltpu.SMEM),
                  pl.BlockSpec((tm, TOP_K), lambda s: (cur(s), 0)),
                  pl.BlockSpec((tm, D), lambda s: (cur(s), 0)),
                  pl.BlockSpec((tm, D), lambda s: (cur(s), 0)),
                  pl.BlockSpec((1, 6, D), lambda s: (row_of_tile(cur(s)), 0, 0))]
                 + [full(a) for a in wts] + [pl.BlockSpec(memory_space=pl.ANY)],
        out_specs=pl.BlockSpec((tm, D), lambda s: (cur(s), 0)),
        scratch_shapes=[pltpu.VMEM((2, TOP_K * tm * ROWT, LANE), U32), pltpu.SemaphoreType.DMA((2,))],
        compiler_params=_cp(("arbitrary",)),
        name="moe_combine",
    )(dest, wt_t, h2, x1, mod3, *wts, y)


def moe_ffn(h2, h2t, x1, mod3, mp, row_of_tile, tm, alpha):
    N = h2.shape[0]
    nb_total = (N * TOP_K + MOE_BLK - 1) // MOE_BLK + N_EXPERTS
    nbp = ((nb_total + LANE - 1) // LANE) * LANE
    tp = _tile(N, 256)
    eidx, wts = moe_router(h2, mp['rwt'], mp['bias'], tp)
    pstart, plan, blk = moe_counts(eidx, tp, nbp)
    dest = moe_dest(eidx, pstart, tp)
    xs = moe_scatter(dest, h2t, nb_total * MOE_BLK, tp)
    xs = moe_padfill(plan, xs)
    y = moe_experts(plan, xs, mp['w13'], mp['w2'], mp['layer'])
    return moe_combine(dest, wts.T, h2, x1, mod3, mp['comb'], y, row_of_tile, tm, alpha)


def token_mixers(stream, B, T, W, mod3, row_of_tile, lp, states, tm):
    pa, pbg, gg, u = inproj(stream, mod3, lp['w_in'], row_of_tile, tm)
    pac = short_conv(pa.reshape(B, T, PA_W), lp['taps'], W).reshape(B * T, PA_W)
    ss, lw, kb, gb = rwkv_prep(pac, lp['prep'], tm)
    s_rwkv, s_gla, s_s5 = states
    y2, f_rwkv = rwkv_scan(ss, lw, kb, s_rwkv, B, T)
    o2, f_gla = gla_scan(pbg, lp['gla_aup'], lp['gla_ab'], s_gla, B, T)
    yd, f_s5 = s5_scan(u.reshape(B, T, S5_W), lp['s5_a'], lp['s5_bbd'], lp['s5_cbd'], s_s5)
    return (y2, gb, o2, gg, yd.reshape(2, B * T, S5_W), u), (f_rwkv, f_gla, f_s5)


def zero_states(B):
    return ((jnp.zeros((B, RW // LANE, LANE, LANE), F32),) * 2,
            (jnp.zeros((B, RW, GLA_KP), F32),) * 2,
            jnp.zeros((2, B, S5_LANES), F32))


def _inproj_columns():
    r_cols = 3 * RW + 2 * DECAY_RANK + 2 * ICLR_RANK + GATE_RANK
    kd = GLA_HEADS * GLA_K
    gq, gk, gv = r_cols, r_cols + kd, r_cols + 2 * kd
    gg = gv + RW
    gad = gg + RW
    pc = gad + 2 * GLA_RANK
    z = lambda n: [-1] * n
    cols = list(range(0, r_cols)) + z(PA_W - r_cols)
    cols += list(range(gq, gq + kd)) + z(256 - kd)
    cols += list(range(gk, gk + kd)) + z(256 - kd)
    cols += list(range(gv, gv + RW))
    cols += list(range(gad, gad + 2 * GLA_RANK)) + z(LANE - 2 * GLA_RANK)
    cols += list(range(gg, gg + RW))
    cols += list(range(pc, pc + S5_W))
    assert len(cols) == NP_W
    return np.asarray(cols, np.int32)


def _head_block_ones():
    h = np.arange(RW) // HEAD
    return jnp.asarray(h[:, None] == h[None, :], BF16)


def _layer_params(l, p):
    D = p['w_in'].shape[1]
    cols = _inproj_columns()
    w_in = jnp.concatenate([p['w_in'][l], jnp.zeros((D, 1), F32)], axis=1)
    w_in = jnp.take(w_in, jnp.asarray(np.where(cols < 0, w_in.shape[1] - 1, cols)), axis=1).astype(BF16)
    taps = p['rwkv_conv'][l].reshape(9, -1)
    taps = jnp.concatenate([taps, jnp.zeros((9, PA_W - taps.shape[1]), F32)], axis=1)
    wup = jnp.zeros((LANE, 2 * RW), F32)
    aup = jnp.zeros((LANE, 2 * RW), F32)
    for d in range(2):
        wup = wup.at[d * DECAY_RANK:(d + 1) * DECAY_RANK, d * RW:(d + 1) * RW].set(p['rwkv_w_up'][l, d])
        o = 2 * DECAY_RANK + d * ICLR_RANK
        aup = aup.at[o:o + ICLR_RANK, d * RW:(d + 1) * RW].set(p['rwkv_a_up'][l, d])
    gup = jnp.zeros((LANE, RW), F32).at[0:GATE_RANK].set(p['rwkv_g_up'][l])
    row = lambda a: a.reshape(1, -1)
    prep = (wup, aup, gup, row(p['rwkv_w0'][l]), row(p['rwkv_a0'][l]), row(p['rwkv_k_k'][l]),
            row(p['rwkv_k_a'][l]), row(p['rwkv_r_k'][l]), _head_block_ones())
    kd = GLA_HEADS * GLA_K
    gla_aup = jnp.zeros((2, LANE, GLA_KP), F32)
    for d in range(2):
        gla_aup = gla_aup.at[d, d * GLA_RANK:(d + 1) * GLA_RANK, 0:kd].set(p['gla_a_up'][l, d])
    gla_ab = jnp.zeros((2, 1, GLA_KP), F32).at[:, 0, 0:kd].set(p['gla_a_bias'][l])
    lam_re, lam_im = p['s5_lam_re'][l], p['s5_lam_im'][l]
    dt = jnp.exp(p['s5_log_dt'][l])[:, :, None]
    zr, zi = lam_re[:, None, :] * dt, lam_im[:, None, :] * dt
    mag = jnp.exp(zr)
    ab_r, ab_i = mag * jnp.cos(zi), mag * jnp.sin(zi)
    den = (lam_re * lam_re + lam_im * lam_im)[:, None, :]
    f_r = ((ab_r - 1) * lam_re[:, None, :] + ab_i * lam_im[:, None, :]) / den
    f_i = (ab_i * lam_re[:, None, :] - (ab_r - 1) * lam_im[:, None, :]) / den
    b_re, b_im = p['s5_b_re'][l], p['s5_b_im'][l]
    bb_r = f_r[..., None] * b_re - f_i[..., None] * b_im
    bb_i = f_r[..., None] * b_im + f_i[..., None] * b_re
    eye_g = jnp.eye(S5_GROUPS, dtype=F32)
    half = S5_LANES // 2

    def in_blockdiag(bb):
        return jnp.einsum('dgpc,gh->dgchp', bb, eye_g).reshape(2, S5_W, half)

    def out_blockdiag(cc):
        return jnp.einsum('dgcp,gh->dgphc', cc, eye_g).reshape(2, half, S5_W)

    s5_bbd = jnp.concatenate([in_blockdiag(bb_r), in_blockdiag(bb_i)], axis=2).astype(BF16)
    s5_cbd = jnp.concatenate([out_blockdiag(p['s5_c_re'][l]), -out_blockdiag(p['s5_c_im'][l])], axis=1).astype(BF16)
    s5_a = jnp.concatenate([ab_r.reshape(2, 1, half), ab_i.reshape(2, 1, half)], axis=2)
    mix = (_head_block_ones(), row(p['rwkv_ln_g'][l]), row(p['rwkv_ln_b'][l]),
           row(jnp.tile(p['gla_norm_g'][l], GLA_HEADS)), row(p['s5_d'][l]),
           p['s5_glu_w'][l].astype(BF16), row(p['s5_glu_b'][l]), p['w_out'][l].astype(BF16),
           row(p['ln1_g'][l]), row(p['ln1_b'][l]))
    moe = dict(rwt=p['router_w'][l].T,
               bias=jnp.broadcast_to(p['router_bias'][l][:, None], (N_EXPERTS, LANE)),
               w13=p['exp_w13'], w2=p['exp_w2'], layer=l,
               comb=(p['sh_w13'][l].astype(BF16), p['sh_w2'][l].astype(BF16), row(p['ln2_g'][l]), row(p['ln2_b'][l])))
    return dict(w_in=w_in, taps=taps, prep=prep, gla_aup=gla_aup, gla_ab=gla_ab,
                s5_a=s5_a, s5_bbd=s5_bbd, s5_cbd=s5_cbd, mix=mix, moe=moe)


_ARG_NAMES = ('x', 'c', 'ctx', 'c_ctx', 'w_mod', 'b_mod', 'w_in', 'rwkv_conv', 'rwkv_w0', 'rwkv_w_up', 'rwkv_a0',
              'rwkv_a_up', 'rwkv_g_up', 'rwkv_k_k', 'rwkv_k_a', 'rwkv_r_k', 'rwkv_ln_g', 'rwkv_ln_b', 'gla_a_up',
              'gla_a_bias', 'gla_norm_g', 's5_lam_re', 's5_lam_im', 's5_log_dt', 's5_b_re', 's5_b_im', 's5_c_re',
              's5_c_im', 's5_d', 's5_glu_w', 's5_glu_b', 'w_out', 'ln1_g', 'ln1_b', 'router_w', 'router_bias',
              'exp_w13', 'exp_w2', 'sh_w13', 'sh_w2', 'ln2_g', 'ln2_b')


def _tile(n, pref):
    t = pref
    while n % t:
        t //= 2
    return t


def kernel(x, c, ctx, c_ctx, w_mod, b_mod, w_in, rwkv_conv, rwkv_w0, rwkv_w_up, rwkv_a0, rwkv_a_up, rwkv_g_up,
           rwkv_k_k, rwkv_k_a, rwkv_r_k, rwkv_ln_g, rwkv_ln_b, gla_a_up, gla_a_bias, gla_norm_g, s5_lam_re,
           s5_lam_im, s5_log_dt, s5_b_re, s5_b_im, s5_c_re, s5_c_im, s5_d, s5_glu_w, s5_glu_b, w_out, ln1_g,
           ln1_b, router_w, router_bias, exp_w13, exp_w2, sh_w13, sh_w2, ln2_g, ln2_b):
    p = dict(zip(_ARG_NAMES, (x, c, ctx, c_ctx, w_mod, b_mod, w_in, rwkv_conv, rwkv_w0, rwkv_w_up, rwkv_a0,
                              rwkv_a_up, rwkv_g_up, rwkv_k_k, rwkv_k_a, rwkv_r_k, rwkv_ln_g, rwkv_ln_b, gla_a_up,
                              gla_a_bias, gla_norm_g, s5_lam_re, s5_lam_im, s5_log_dt, s5_b_re, s5_b_im, s5_c_re,
                              s5_c_im, s5_d, s5_glu_w, s5_glu_b, w_out, ln1_g, ln1_b, router_w, router_bias,
                              exp_w13, exp_w2, sh_w13, sh_w2, ln2_g, ln2_b)))
    B, T, D = x.shape
    TC = ctx.shape[1]
    L = w_mod.shape[0]
    alpha = (2 * L) ** 0.25
    n_lat, n_ctx = B * T, B * TC
    R = ((B + 1 + SUB - 1) // SUB) * SUB
    cc = jnp.zeros((R, D), F32).at[0:B].set(c).at[B].set(c_ctx)
    mod = mod_table(cc, w_mod, b_mod)
    tm = _tile(T, 512)
    tmc = _tile(n_ctx, 512)
    tmm = min(_tile(T, 128), _tile(n_ctx, 128))
    lat_row = lambda i: (i * tm) // T
    ctx_row = lambda i: B
    lat = (x.reshape(n_lat, D), 0, n_lat)
    con = (ctx.reshape(n_ctx, D), 0, n_ctx)
    for l in range(L):
        last = l == L - 1
        lp = _layer_params(l, p)
        mod3 = mod[l].reshape(R, 6, D)
        outs_c, st_c = token_mixers(con, B, TC, TC, mod3, ctx_row, lp, zero_states(B), tmc)
        outs, _ = token_mixers(lat, B, T, GRID_W, mod3, lat_row, lp, st_c, tm)
        if last:
            x1, h2, h2t = mix_out(*outs, lat, mod3, lp['mix'], lat_row, tm, alpha)
            out = moe_ffn(h2, h2t, x1, mod3, lp['moe'], lambda i: (i * tmm) // T, tmm, alpha)
            lat = (out, 0, n_lat)
        else:
            n_all = n_ctx + n_lat
            bufs = mix_out(*outs_c, con, mod3, lp['mix'], ctx_row, tmc, alpha, out_rows=n_all)
            x1, h2, h2t = mix_out(*outs, lat, mod3, lp['mix'], lat_row, tm, alpha,
                                  out_rows=n_all, out_row0=n_ctx, into=bufs)
            row_all = lambda i: jnp.where(i * tmm < n_ctx, B, (i * tmm - n_ctx) // T)
            out = moe_ffn(h2, h2t, x1, mod3, lp['moe'], row_all, tmm, alpha)
            con, lat = (out, 0, n_ctx), (out, n_ctx, n_lat)
    out, row0, _ = lat
    return out[row0:row0 + n_lat].reshape(B, T, D)
```

```python
import functools
import math

import numpy as np
import jax
import jax.numpy as jnp
from jax import lax
from jax.experimental import pallas as pl
from jax.experimental.pallas import tpu as pltpu

F32 = jnp.float32
BF16 = jnp.bfloat16
I32 = jnp.int32
ACT = BF16

GRID_W = 64
RWKV_HEADS = 6
HEAD = 64
RW = RWKV_HEADS * HEAD
DECAY_RANK = 32
ICLR_RANK = 32
GATE_RANK = 64
RWKV_GN_EPS = 64e-5
GLA_HEADS = 6
GLA_K = 32
GLA_RANK = 16
GLA_TEMP = 16.0
S5_GROUPS = 16
S5_GROUP = 16
S5_STATE = 64
S5_W = S5_GROUPS * S5_GROUP
S5_LANES = 2 * S5_GROUPS * S5_STATE
N_EXPERTS = 256
TOP_K = 8
N_EGROUPS = 8
TOPK_GROUPS = 4
ROUTE_SCALE = 2.5
LN_EPS = 1e-6
CHUNK = 64
MOE_BLK = 256
LANE = 128
SUB = 8
VMEM_LIMIT = 56 * 1024 * 1024

PA_W = 3 * RW + 2 * LANE
PBG_W = 256 + 256 + RW + LANE
NP_W = PA_W + PBG_W + RW + S5_W


def _cp(sem):
    return pltpu.CompilerParams(dimension_semantics=sem, vmem_limit_bytes=VMEM_LIMIT)


def _sigmoid(x):
    return 1.0 / (1.0 + jnp.exp(-x))


def _silu(x):
    return x * _sigmoid(x)


def _bdot(a, b, dims=None):
    a = a.astype(BF16)
    b = b.astype(BF16)
    if dims is None:
        return jnp.dot(a, b, preferred_element_type=F32)
    return lax.dot_general(a, b, dims, preferred_element_type=F32)


def _split2(x):
    hi = x.astype(BF16)
    lo = (x - hi.astype(F32)).astype(BF16)
    return hi, lo


def _split3(x):
    hi = x.astype(BF16)
    r = x - hi.astype(F32)
    mid = r.astype(BF16)
    lo = (r - mid.astype(F32)).astype(BF16)
    return hi, mid, lo


NT_DIMS = (((1,), (1,)), ((), ()))
TN_DIMS = (((0,), (0,)), ((), ()))


def _dot33(a, b, dims=None):
    ah, al = _split2(a)
    bh, bl = _split2(b)
    return _bdot(ah, bh, dims) + (_bdot(ah, bl, dims) + _bdot(al, bh, dims))


def _dot_exact_lhs(m_exact, x, dims=None):
    h, m, l = _split3(x)
    return _bdot(m_exact, h, dims) + (_bdot(m_exact, m, dims) + _bdot(m_exact, l, dims))


def _dot_exact_rhs(x, m_exact, dims=None):
    h, m, l = _split3(x)
    return _bdot(h, m_exact, dims) + (_bdot(m, m_exact, dims) + _bdot(l, m_exact, dims))


def _dot_split_rhs(x, m_exact, dims=None):
    h, l = _split2(x)
    return _bdot(h, m_exact, dims) + _bdot(l, m_exact, dims)


def _layer_norm(x):
    mu = jnp.mean(x, axis=-1, keepdims=True)
    xc = x - mu
    var = jnp.mean(xc * xc, axis=-1, keepdims=True)
    return xc * lax.rsqrt(var + LN_EPS)


def _mod_kernel(c_ref, w_ref, b_ref, o_ref):
    s = _silu(c_ref[...])
    o_ref[0] = _dot33(s, w_ref[0]) + b_ref[0]


def mod_table(cc, w_mod, b_mod):
    L, D, D6 = w_mod.shape
    R = cc.shape[0]
    tn = 1536
    return pl.pallas_call(
        _mod_kernel,
        out_shape=jax.ShapeDtypeStruct((L, R, D6), F32),
        grid=(L, D6 // tn),
        in_specs=[pl.BlockSpec((R, D), lambda l, j: (0, 0)),
                  pl.BlockSpec((1, D, tn), lambda l, j: (l, 0, j)),
                  pl.BlockSpec((1, 1, tn), lambda l, j: (l, 0, j))],
        out_specs=pl.BlockSpec((1, R, tn), lambda l, j: (l, 0, j)),
        compiler_params=_cp(("arbitrary", "arbitrary")),
        name="mod_table",
    )(cc, w_mod, b_mod.reshape(L, 1, D6))


def _inproj_kernel(x_ref, mod_ref, w_ref, pa_ref, pbg_ref, gg_ref, u_ref):
    x = x_ref[...]
    h = _layer_norm(x) * (1.0 + mod_ref[0, 1:2, :]) + mod_ref[0, 0:1, :]
    hb = h.astype(BF16)
    o = 0
    for ref in (pa_ref, pbg_ref, gg_ref, u_ref):
        w = ref.shape[-1]
        ref[...] = jnp.dot(hb, w_ref[:, o:o + w], preferred_element_type=F32).astype(ref.dtype)
        o += w


def inproj(stream, mod3, w_bf, row_of_tile, tm):
    x2, row0, N = stream
    D = x2.shape[1]
    t0 = row0 // tm
    assert row0 % tm == 0 and N % tm == 0
    return pl.pallas_call(
        _inproj_kernel,
        out_shape=[jax.ShapeDtypeStruct((N, PA_W), ACT),
                   jax.ShapeDtypeStruct((N, PBG_W), ACT),
                   jax.ShapeDtypeStruct((N, RW), ACT),
                   jax.ShapeDtypeStruct((N, S5_W), ACT)],
        grid=(N // tm,),
        in_specs=[pl.BlockSpec((tm, D), lambda i: (i + t0, 0)),
                  pl.BlockSpec((1, 6, D), lambda i: (row_of_tile(i), 0, 0)),
                  pl.BlockSpec((D, NP_W), lambda i: (0, 0))],
        out_specs=[pl.BlockSpec((tm, PA_W), lambda i: (i, 0)),
                   pl.BlockSpec((tm, PBG_W), lambda i: (i, 0)),
                   pl.BlockSpec((tm, RW), lambda i: (i, 0)),
                   pl.BlockSpec((tm, S5_W), lambda i: (i, 0))],
        compiler_params=_cp(("arbitrary",)),
        name="inproj",
    )(x2, mod3, w_bf)


CONV_PAD = 72


def _conv_kernel(x_ref, taps_ref, o_ref, buf_ref, *, T, W, vertical):
    pad = CONV_PAD
    zeros = jnp.zeros((pad, LANE), F32)
    buf_ref[0:pad, :] = zeros
    buf_ref[pad + T:pad + T + pad, :] = zeros
    buf_ref[pad:pad + T, :] = x_ref[0].astype(F32)
    ch = min(T, 256)
    col = lax.broadcasted_iota(I32, (ch, LANE), 0) & (W - 1)
    left_ok = col >= 1
    right_ok = col <= W - 2
    for c in range(T // ch):
        base = pad + c * ch
        acc = jnp.zeros((ch, LANE), F32)
        for dr in ((0, 1, 2) if vertical else (1,)):
            for dc in range(3):
                off = (dr - 1) * W + (dc - 1)
                v = buf_ref[base + off:base + off + ch, :]
                if dc == 0:
                    v = jnp.where(left_ok, v, 0.0)
                elif dc == 2:
                    v = jnp.where(right_ok, v, 0.0)
                acc = acc + v * taps_ref[3 * dr + dc:3 * dr + dc + 1, :]
        o_ref[0, c * ch:(c + 1) * ch, :] = acc.astype(o_ref.dtype)


def short_conv(pa3, taps9, W):
    B, T, C = pa3.shape
    vertical = T > W
    assert W & (W - 1) == 0 and (not vertical or W + 1 <= CONV_PAD)
    kern = functools.partial(_conv_kernel, T=T, W=W, vertical=vertical)
    return pl.pallas_call(
        kern,
        out_shape=jax.ShapeDtypeStruct((B, T, C), ACT),
        grid=(B, C // LANE),
        in_specs=[pl.BlockSpec((1, T, LANE), lambda b, j: (b, 0, j)),
                  pl.BlockSpec((9, LANE), lambda b, j: (0, j))],
        out_specs=pl.BlockSpec((1, T, LANE), lambda b, j: (b, 0, j)),
        scratch_shapes=[pltpu.VMEM((T + 2 * CONV_PAD, LANE), F32)],
        compiler_params=_cp(("arbitrary", "arbitrary")),
        name="short_conv",
    )(pa3, taps9)


def _rwkv_prep_kernel(pa_ref, wup_ref, aup_ref, gup_ref, w0_ref, a0_ref, kk_ref, ka_ref, rk_ref, hb_ref,
                      ss_ref, lw_ref, kb_ref, gb_ref):
    r = pa_ref[:, 0:RW].astype(F32)
    k = pa_ref[:, RW:2 * RW].astype(F32)
    v = pa_ref[:, 2 * RW:3 * RW].astype(F32)
    wa = pa_ref[:, 3 * RW:3 * RW + LANE].astype(F32)
    gd = pa_ref[:, 3 * RW + LANE:3 * RW + 2 * LANE].astype(F32)
    z = w0_ref[...] + _bdot(jnp.tanh(wa), wup_ref[...])
    lw = -_sigmoid(z) * math.exp(-0.5)
    a = _sigmoid(a0_ref[...] + _bdot(wa, aup_ref[...]))
    g = _bdot(_sigmoid(gd), gup_ref[...])
    hb = hb_ref[...]
    kk = k * kk_ref[...]
    kk = kk * lax.rsqrt(_bdot(kk * kk, hb) + 1e-12)
    ka = ka_ref[...]
    ss_ref[:, 0:RW] = r.astype(ACT)
    ss_ref[:, RW:2 * RW] = v.astype(ACT)
    ss_ref[:, 2 * RW:3 * RW] = kk.astype(ACT)
    rk2 = jnp.zeros_like(r)
    for d in range(2):
        ad = a[:, d * RW:(d + 1) * RW]
        k2 = k * (1.0 + (ad - 1.0) * ka)
        lw_ref[d] = lw[:, d * RW:(d + 1) * RW]
        kb_ref[d, :, 0:RW] = k2.astype(ACT)
        kb_ref[d, :, RW:2 * RW] = (kk * ad).astype(ACT)
        rk2 = rk2 + r * k2
    bonus = _bdot(rk2 * rk_ref[...], hb) * v
    gb_ref[:, 0:RW] = g.astype(ACT)
    gb_ref[:, RW:2 * RW] = bonus.astype(ACT)


def rwkv_prep(pa2, wts, tm):
    N = pa2.shape[0]
    full = lambda a: pl.BlockSpec(a.shape, lambda i: (0,) * a.ndim)
    return pl.pallas_call(
        _rwkv_prep_kernel,
        out_shape=[jax.ShapeDtypeStruct((N, 3 * RW), ACT),
                   jax.ShapeDtypeStruct((2, N, RW), F32),
                   jax.ShapeDtypeStruct((2, N, 2 * RW), ACT),
                   jax.ShapeDtypeStruct((N, 2 * RW), ACT)],
        grid=(N // tm,),
        in_specs=[pl.BlockSpec((tm, PA_W), lambda i: (i, 0))] + [full(a) for a in wts],
        out_specs=[pl.BlockSpec((tm, 3 * RW), lambda i: (i, 0)),
                   pl.BlockSpec((2, tm, RW), lambda i: (0, i, 0)),
                   pl.BlockSpec((2, tm, 2 * RW), lambda i: (0, i, 0)),
                   pl.BlockSpec((tm, 2 * RW), lambda i: (i, 0))],
        compiler_params=_cp(("arbitrary",)),
        name="rwkv_prep",
    )(pa2, *wts)


RWKV_GROUP = 16
RWKV_WAVE = 4


def _rwkv_scan_kernel(ss_ref, lw_ref, kb_ref, s0_ref, y_ref, sfin_ref, s_scr, *, nsteps, group, reverse):
    n = pl.program_id(1)
    C = CHUNK
    P = 2 * C
    npair = RW // LANE

    @pl.when(n == 0)
    def _():
        s_scr[...] = s0_ref[0]

    row = lax.broadcasted_iota(I32, (P, P), 0)
    col = lax.broadcasted_iota(I32, (P, P), 1)
    same = (row >> 6) == (col >> 6)
    dlt = (col & (C - 1)) - (row & (C - 1)) if reverse else (row & (C - 1)) - (col & (C - 1))
    strict = same & (dlt > 0)
    incl = same & (dlt >= 0)
    eye = (row == col).astype(F32)
    lvl_masks = [((row >> (lvl + 1)) == (col >> (lvl + 1))) & ((row >> lvl) != (col >> lvl)) for lvl in range(6)]
    rc = lax.broadcasted_iota(I32, (C, C), 0)
    cc = lax.broadcasted_iota(I32, (C, C), 1)
    tri = jnp.where((cc >= rc) if reverse else (rc >= cc), 1.0, 0.0).astype(BF16)
    head0 = lax.broadcasted_iota(I32, (C, LANE), 1) < HEAD

    def stack(x):
        return jnp.concatenate([jnp.where(head0, x, 0.0), jnp.where(head0, 0.0, x)], axis=0)

    pairs = range(npair)
    tm_ = {}

    def independent_stages(chunks):
        streams = [(g, p) for g in chunks for p in pairs]

        def st_cumsum():
            for (g, p) in streams:
                lw = lw_ref[0, g * C:(g + 1) * C, p * LANE:(p + 1) * LANE]
                tm_[(g, p)] = dict(lw=lw, cl=_dot_exact_lhs(tri, lw))

        def st_scores():
            for (g, p) in streams:
                t = tm_[(g, p)]
                t0, t1 = g * C, (g + 1) * C
                lo, hi = p * LANE, (p + 1) * LANE
                r = ss_ref[t0:t1, lo:hi].astype(F32)
                v = ss_ref[t0:t1, RW + lo:RW + hi].astype(F32)
                kk = ss_ref[t0:t1, 2 * RW + lo:2 * RW + hi].astype(F32)
                k2 = kb_ref[0, t0:t1, lo:hi].astype(F32)
                b = kb_ref[0, t0:t1, RW + lo:RW + hi].astype(F32)
                cl, lw = t['cl'], t['lw']
                t['ptot'] = jnp.exp(jnp.sum(lw, axis=0, keepdims=True))
                pinv = jnp.exp(-cl)
                left = jnp.concatenate([stack(-kk * jnp.exp(cl - lw)), stack(r * jnp.exp(cl))], axis=0)
                right = jnp.concatenate([stack(b * pinv), stack(k2 * pinv)], axis=0)
                t['v_st'] = stack(v)
                t['left'] = left.astype(BF16)
                t['bk'] = (right * t['ptot']).astype(BF16)
                aa = _bdot(left, right, NT_DIMS)
                t['nmat'] = jnp.where(strict, aa[0:P, 0:P], 0.0)
                t['a_ak'] = jnp.where(strict, aa[0:P, P:2 * P], 0.0)
                t['a_rbk'] = jnp.concatenate([jnp.where(incl, aa[P:2 * P, 0:P], 0.0),
                                              jnp.where(incl, aa[P:2 * P, P:2 * P], 0.0)], axis=1).astype(BF16)
                t['tinv'] = eye + jnp.where(lvl_masks[0], t['nmat'], 0.0)

        def st_akv():
            for sk in streams:
                t = tm_[sk]
                t['akv'] = _bdot(t['a_ak'], t['v_st'])

        def st_et(m):
            def run():
                for sk in streams:
                    t = tm_[sk]
                    t['et'] = _bdot(jnp.where(m, t['nmat'], 0.0), t['tinv'])
            return run

        def st_tinv():
            for sk in streams:
                t = tm_[sk]
                t['tinv'] = t['tinv'] + _bdot(t['tinv'], t['et'])

        stages = [st_cumsum, st_scores, st_akv]
        for m in lvl_masks[1:]:
            stages += [st_et(m), st_tinv]
        return stages

    state = [s_scr[p] for p in pairs]
    carry = {}

    def dependent_stages(chunks):
        stages = []
        for g in chunks:
            def st_as(g=g):
                carry['a_s'] = [_bdot(tm_[(g, p)]['left'], state[p], NT_DIMS) for p in pairs]

            def st_u(g=g):
                u = [_bdot(tm_[(g, p)]['tinv'], carry['a_s'][p][0:P] + tm_[(g, p)]['akv']) for p in pairs]
                carry['uv'] = [jnp.concatenate([u[p], tm_[(g, p)]['v_st']], axis=0) for p in pairs]

            def st_state(g=g):
                for p in pairs:
                    t = tm_[(g, p)]
                    state[p] = state[p] * t['ptot'] + _bdot(carry['uv'][p], t['bk'], TN_DIMS)

            def st_y(g=g, a_s=None):
                for p in pairs:
                    y_st = carry['a_s'][p][P:2 * P] + _bdot(tm_[(g, p)]['a_rbk'], carry['uv'][p])
                    y_ref[g * C:(g + 1) * C, p * LANE:(p + 1) * LANE] = (y_st[0:C] + y_st[C:P]).astype(y_ref.dtype)

            stages += [st_as, st_u, st_y, st_state]
        return stages

    order = list(range(group - 1, -1, -1) if reverse else range(group))
    wave = RWKV_WAVE if group >= 2 * RWKV_WAVE else group
    pending = []
    for w in range(0, group, wave):
        ind = independent_stages(order[w:w + wave])
        for i in range(max(len(ind), len(pending))):
            if i < len(ind):
                ind[i]()
            if i < len(pending):
                pending[i]()
        pending = dependent_stages(order[w:w + wave])
    for stage in pending:
        stage()
    for p in pairs:
        s_scr[p] = state[p]

    @pl.when(n == nsteps - 1)
    def _():
        sfin_ref[0] = s_scr[...]


def rwkv_scan_dir(ss, lw, kb, s0, B, T, d):
    nc = T // CHUNK
    group = min(RWKV_GROUP, nc)
    nsteps = nc // group
    npair = RW // LANE
    rows = group * CHUNK
    blk = (lambda b, n: b * nsteps + nsteps - 1 - n) if d else (lambda b, n: b * nsteps + n)
    kern = functools.partial(_rwkv_scan_kernel, nsteps=nsteps, group=group, reverse=bool(d))
    return pl.pallas_call(
        kern,
        out_shape=[jax.ShapeDtypeStruct((B * T, RW), ACT),
                   jax.ShapeDtypeStruct((B, npair, LANE, LANE), F32)],
        grid=(B, nsteps),
        in_specs=[pl.BlockSpec((rows, 3 * RW), lambda b, n: (blk(b, n), 0)),
                  pl.BlockSpec((1, rows, RW), lambda b, n: (d, blk(b, n), 0)),
                  pl.BlockSpec((1, rows, 2 * RW), lambda b, n: (d, blk(b, n), 0)),
                  pl.BlockSpec((1, npair, LANE, LANE), lambda b, n: (b, 0, 0, 0))],
        out_specs=[pl.BlockSpec((rows, RW), lambda b, n: (blk(b, n), 0)),
                   pl.BlockSpec((1, npair, LANE, LANE), lambda b, n: (b, 0, 0, 0))],
        scratch_shapes=[pltpu.VMEM((npair, LANE, LANE), F32)],
        compiler_params=_cp(("arbitrary", "arbitrary")),
        name="rwkv_scan_bwd" if d else "rwkv_scan_fwd",
    )(ss, lw, kb, s0)


def rwkv_scan(ss, lw, kb, s0, B, T):
    y0, f0 = rwkv_scan_dir(ss, lw, kb, s0[0], B, T, 0)
    y1, f1 = rwkv_scan_dir(ss, lw, kb, s0[1], B, T, 1)
    return (y0, y1), (f0, f1)


GLA_KP = 256


GLA_GROUP = 8


def _gla_kernel(pbg_ref, aup_ref, ab_ref, s0_ref, o_ref, sfin_ref, s_scr, *, nsteps, group, reverse):
    n = pl.program_id(1)
    C = CHUNK
    R = group * C

    @pl.when(n == 0)
    def _():
        s_scr[...] = s0_ref[0]

    q = pbg_ref[:, 0:GLA_KP].astype(F32)
    k = pbg_ref[:, GLA_KP:2 * GLA_KP].astype(F32)
    v = pbg_ref[:, 2 * GLA_KP:2 * GLA_KP + RW].astype(F32)
    ad = pbg_ref[:, 2 * GLA_KP + RW:2 * GLA_KP + RW + LANE].astype(F32)
    x = _dot33(ad, aup_ref[0]) + ab_ref[0]
    la = (jnp.minimum(x, 0.0) - jnp.log(1.0 + jnp.exp(-jnp.abs(x)))) * (1.0 / GLA_TEMP)
    chunks = range(group)
    sl = [slice(g * C, (g + 1) * C) for g in chunks]
    rr = lax.broadcasted_iota(I32, (C, C), 0)
    rc = lax.broadcasted_iota(I32, (C, C), 1)
    tri = jnp.where((rc >= rr) if reverse else (rr >= rc), 1.0, 0.0).astype(BF16)
    bcums = [_dot_exact_lhs(tri, la[sl[g]]) for g in chunks]
    last = 0 if reverse else C - 1
    tots = [bc[last:last + 1, :] for bc in bcums]
    bcum = jnp.concatenate(bcums, axis=0)
    tot = jnp.concatenate([jnp.broadcast_to(t, (C, GLA_KP)) for t in tots], axis=0)
    q_in = q * jnp.exp(bcum) * (GLA_K ** -0.5)
    k_in = k * jnp.exp(-bcum)
    k_st = k * jnp.exp(tot - bcum)
    dn = [jnp.exp(t) for t in tots]
    klane = lax.broadcasted_iota(I32, (C, GLA_KP), 1)
    rt = lax.broadcasted_iota(I32, (GLA_HEADS * C, C), 0) & (C - 1)
    ct = lax.broadcasted_iota(I32, (GLA_HEADS * C, C), 1)
    causal = (ct >= rt) if reverse else (rt >= ct)
    vlane = lax.broadcasted_iota(I32, (C, RW), 1)
    sv = lax.broadcasted_iota(I32, (RW, GLA_KP), 0) >> 6
    sk = lax.broadcasted_iota(I32, (RW, GLA_KP), 1) >> 5
    q_rows = [jnp.concatenate([jnp.where((klane >> 5) == h, q_in[sl[g]], 0.0) for h in range(GLA_HEADS)],
                              axis=0).astype(BF16) for g in chunks]
    att = [jnp.where(causal, _bdot(q_rows[g], k_in[sl[g]], NT_DIMS), 0.0) for g in chunks]
    o_rows = [_bdot(att[g], v[sl[g]]) for g in chunks]
    kv = [jnp.where(sv == sk, _bdot(v[sl[g]], k_st[sl[g]], TN_DIMS), 0.0) for g in chunks]
    s = s_scr[...]
    for g in (reversed(chunks) if reverse else chunks):
        o = _bdot(q_in[sl[g]], s, NT_DIMS)
        for h in range(GLA_HEADS):
            o = o + jnp.where((vlane >> 6) == h, o_rows[g][h * C:(h + 1) * C], 0.0)
        o_ref[sl[g], :] = o.astype(o_ref.dtype)
        s = s * dn[g] + kv[g]
    s_scr[...] = s

    @pl.when(n == nsteps - 1)
    def _():
        sfin_ref[0] = s_scr[...]


def gla_scan_dir(pbg, aup, ab, s0, B, T, d):
    nc = T // CHUNK
    group = min(GLA_GROUP, nc)
    nsteps = nc // group
    rows = group * CHUNK
    blk = (lambda b, n: b * nsteps + nsteps - 1 - n) if d else (lambda b, n: b * nsteps + n)
    kern = functools.partial(_gla_kernel, nsteps=nsteps, group=group, reverse=bool(d))
    return pl.pallas_call(
        kern,
        out_shape=[jax.ShapeDtypeStruct((B * T, RW), ACT),
                   jax.ShapeDtypeStruct((B, RW, GLA_KP), F32)],
        grid=(B, nsteps),
        in_specs=[pl.BlockSpec((rows, PBG_W), lambda b, n: (blk(b, n), 0)),
                  pl.BlockSpec((1, LANE, GLA_KP), lambda b, n: (d, 0, 0)),
                  pl.BlockSpec((1, 1, GLA_KP), lambda b, n: (d, 0, 0)),
                  pl.BlockSpec((1, RW, GLA_KP), lambda b, n: (b, 0, 0))],
        out_specs=[pl.BlockSpec((rows, RW), lambda b, n: (blk(b, n), 0)),
                   pl.BlockSpec((1, RW, GLA_KP), lambda b, n: (b, 0, 0))],
        scratch_shapes=[pltpu.VMEM((RW, GLA_KP), F32)],
        compiler_params=_cp(("arbitrary", "arbitrary")),
        name="gla_scan_bwd" if d else "gla_scan_fwd",
    )(pbg, aup, ab, s0)


def gla_scan(pbg, aup, ab, s0, B, T):
    o0, f0 = gla_scan_dir(pbg, aup, ab, s0[0], B, T, 0)
    o1, f1 = gla_scan_dir(pbg, aup, ab, s0[1], B, T, 1)
    return (o0, o1), (f0, f1)


S5_PITCH = 68


def _s5_kernel(u_ref, a_ref, bbd_ref, cbd_ref, s0_ref, y_ref, sfin_ref, x_scr, st_scr, *, nc, nb):
    d = pl.program_id(0)
    n = pl.program_id(1)
    C = CHUNK
    half = S5_LANES // 2

    @pl.when(n == 0)
    def _():
        st_scr[...] = s0_ref[0]

    nslab = S5_LANES // LANE
    bu = _bdot(jnp.concatenate([u_ref[b] for b in range(nb)], axis=0), bbd_ref[0])
    for b in range(nb):
        for j in range(nslab):
            x_scr[j, b * S5_PITCH:b * S5_PITCH + C, :] = bu[b * C:(b + 1) * C, j * LANE:(j + 1) * LANE]
    a_re = a_ref[0, :, 0:half]
    a_im = a_ref[0, :, half:S5_LANES]

    def step(i, carry):
        re, im = carry
        t = i + d * (C - 1 - 2 * i)
        rows = pl.ds(t, nb, stride=S5_PITCH)
        bu = jnp.concatenate([x_scr[j, rows, :] for j in range(nslab)], axis=1)
        nre = a_re * re - a_im * im + bu[:, 0:half]
        nim = a_re * im + a_im * re + bu[:, half:S5_LANES]
        for j in range(nslab // 2):
            x_scr[j, rows, :] = nre[:, j * LANE:(j + 1) * LANE]
            x_scr[nslab // 2 + j, rows, :] = nim[:, j * LANE:(j + 1) * LANE]
        return nre, nim

    st = st_scr[...]
    re, im = lax.fori_loop(0, C, step, (st[:, 0:half], st[:, half:S5_LANES]), unroll=4)
    st_scr[...] = jnp.concatenate([re, im], axis=1)
    xs = jnp.concatenate(
        [jnp.concatenate([x_scr[j, b * S5_PITCH:b * S5_PITCH + C, :] for j in range(nslab)], axis=1).astype(BF16)
         for b in range(nb)], axis=0)
    y = _bdot(xs, cbd_ref[0])
    for b in range(nb):
        y_ref[0, b] = y[b * C:(b + 1) * C, :].astype(y_ref.dtype)

    @pl.when(n == nc - 1)
    def _():
        sfin_ref[0] = st_scr[...]


def s5_scan(u3, a_bar, bbd, cbd, s0):
    B, T, _ = u3.shape
    nc = T // CHUNK
    chunk = lambda d, n: n + d * (nc - 1 - 2 * n)
    kern = functools.partial(_s5_kernel, nc=nc, nb=B)
    return pl.pallas_call(
        kern,
        out_shape=[jax.ShapeDtypeStruct((2, B, T, S5_W), ACT),
                   jax.ShapeDtypeStruct((2, B, S5_LANES), F32)],
        grid=(2, nc),
        in_specs=[pl.BlockSpec((B, CHUNK, S5_W), lambda d, n: (0, chunk(d, n), 0)),
                  pl.BlockSpec((1, 1, S5_LANES), lambda d, n: (d, 0, 0)),
                  pl.BlockSpec((1, S5_W, S5_LANES), lambda d, n: (d, 0, 0)),
                  pl.BlockSpec((1, S5_LANES, S5_W), lambda d, n: (d, 0, 0)),
                  pl.BlockSpec((1, B, S5_LANES), lambda d, n: (d, 0, 0))],
        out_specs=[pl.BlockSpec((1, B, CHUNK, S5_W), lambda d, n: (d, 0, chunk(d, n), 0)),
                   pl.BlockSpec((1, B, S5_LANES), lambda d, n: (d, 0, 0))],
        scratch_shapes=[pltpu.VMEM((S5_LANES // LANE, B * S5_PITCH, LANE), F32),
                        pltpu.VMEM((B, S5_LANES), F32)],
        compiler_params=_cp(("arbitrary", "arbitrary")),
        name="s5_scan",
    )(u3, a_bar, bbd, cbd, s0)


ROWT = 4
U32 = jnp.uint32


def _pack_bf16_pair(a, b):
    au = lax.bitcast_convert_type(a.astype(BF16).astype(F32), U32)
    bu = lax.bitcast_convert_type(b.astype(BF16).astype(F32), U32)
    return (au >> 16) | bu


def _unpack_bf16_pair(w):
    return (lax.bitcast_convert_type(w << 16, F32), lax.bitcast_convert_type(w & jnp.uint32(0xFFFF0000), F32))


def _tile_rows_store(ref, val):
    tm = val.shape[0]
    assert val.shape[1] == 2 * ROWT * LANE
    for s in range(ROWT):
        ref[pl.ds(s, tm, stride=ROWT), :] = _pack_bf16_pair(val[:, 2 * s * LANE:(2 * s + 1) * LANE],
                                                            val[:, (2 * s + 1) * LANE:(2 * s + 2) * LANE])


def _tile_rows_load(ref, tm, base=0):
    parts = []
    for s in range(ROWT):
        parts.extend(_unpack_bf16_pair(ref[pl.ds(base + s, tm, stride=ROWT), :]))
    return jnp.concatenate(parts, axis=1)


def _mix_out_kernel(y0_ref, y1_ref, gb_ref, o0_ref, o1_ref, gg_ref, yd_ref, u_ref, x_ref, mod_ref,
                    hb_ref, lng_ref, lnb_ref, gng_ref, s5d_ref, gluw_ref, glub_ref, wout_ref, l1g_ref, l1b_ref,
                    *rest, alpha):
    x1_ref, h2_ref, h2t_ref = rest[-3:]
    hb = hb_ref[...]
    inv = 1.0 / HEAD
    f32 = lambda ref_val: ref_val.astype(F32)
    y = f32(y0_ref[...]) + f32(y1_ref[...])
    yc = y - _dot_split_rhs(y, hb) * inv
    var = _bdot(yc * yc, hb) * inv
    gn = yc * lax.rsqrt(var + RWKV_GN_EPS) * lng_ref[...] + lnb_ref[...]
    y_a = (gn + f32(gb_ref[:, RW:2 * RW])) * f32(gb_ref[:, 0:RW])
    o = f32(o0_ref[...]) + f32(o1_ref[...])
    o = o * lax.rsqrt(_bdot(o * o, hb) * inv + 1e-6) * gng_ref[...]
    y_b = o * _silu(f32(gg_ref[...]))
    c = s5d_ref[...] * f32(u_ref[...]) + f32(yd_ref[0]) + f32(yd_ref[1])
    c = 0.5 * c * (1.0 + jnp.tanh(math.sqrt(2.0 / math.pi) * (c + 0.044715 * (c * c * c))))
    y_c = c * _sigmoid(_bdot(c, gluw_ref[...]) + glub_ref[...])
    y_mix = (_bdot(y_a, wout_ref[0:RW, :]) + _bdot(y_b, wout_ref[RW:2 * RW, :])
             + _bdot(y_c, wout_ref[2 * RW:2 * RW + S5_W, :]))
    x1 = _layer_norm(alpha * x_ref[...] + mod_ref[0, 2:3, :] * y_mix) * l1g_ref[...] + l1b_ref[...]
    x1_ref[...] = x1
    h2 = _layer_norm(x1) * (1.0 + mod_ref[0, 4:5, :]) + mod_ref[0, 3:4, :]
    h2_ref[...] = h2
    _tile_rows_store(h2t_ref, h2)


def mix_out(y01, gb, o01, gg, yd, u, stream, mod3, wts, row_of_tile, tm, alpha, out_rows=None, out_row0=0, into=None):
    x2, row0, N = stream
    D = x2.shape[1]
    out_rows = N if out_rows is None else out_rows
    assert row0 % tm == 0 and out_row0 % tm == 0 and N % tm == 0
    t0, o0 = row0 // tm, out_row0 // tm
    full = lambda a: pl.BlockSpec(a.shape, lambda i: (0,) * a.ndim)
    kern = functools.partial(_mix_out_kernel, alpha=alpha)
    ins = [y01[0], y01[1], gb, o01[0], o01[1], gg, yd, u, x2, mod3, *wts]
    extra = [] if into is None else list(into)
    return pl.pallas_call(
        kern,
        out_shape=[jax.ShapeDtypeStruct((out_rows, D), F32),
                   jax.ShapeDtypeStruct((out_rows, D), F32),
                   jax.ShapeDtypeStruct((out_rows * ROWT, LANE), U32)],
        grid=(N // tm,),
        in_specs=[pl.BlockSpec((tm, RW), lambda i: (i, 0)),
                  pl.BlockSpec((tm, RW), lambda i: (i, 0)),
                  pl.BlockSpec((tm, 2 * RW), lambda i: (i, 0)),
                  pl.BlockSpec((tm, RW), lambda i: (i, 0)),
                  pl.BlockSpec((tm, RW), lambda i: (i, 0)),
                  pl.BlockSpec((tm, RW), lambda i: (i, 0)),
                  pl.BlockSpec((2, tm, S5_W), lambda i: (0, i, 0)),
                  pl.BlockSpec((tm, S5_W), lambda i: (i, 0)),
                  pl.BlockSpec((tm, D), lambda i: (i + t0, 0)),
                  pl.BlockSpec((1, 6, D), lambda i: (row_of_tile(i), 0, 0))] + [full(a) for a in wts]
                 + [pl.BlockSpec(memory_space=pl.ANY)] * len(extra),
        out_specs=[pl.BlockSpec((tm, D), lambda i: (i + o0, 0)),
                   pl.BlockSpec((tm, D), lambda i: (i + o0, 0)),
                   pl.BlockSpec((tm * ROWT, LANE), lambda i: (i + o0, 0))],
        input_output_aliases={len(ins) + k: k for k in range(len(extra))},
        compiler_params=_cp(("arbitrary",)),
        name="mix_out",
    )(*ins, *extra)


def _first_max(x, idx, big):
    m = jnp.max(x, axis=0, keepdims=True)
    first = jnp.min(jnp.where(x == m, idx, big), axis=0, keepdims=True)
    return m, first


def _router_kernel(h_ref, rwt_ref, bias_ref, e_ref, w_ref):
    tm = h_ref.shape[0]
    gsz = N_EXPERTS // N_EGROUPS
    ninf = -jnp.inf
    s = _sigmoid(_dot33(rwt_ref[...], h_ref[...], NT_DIMS))
    ssel = s + bias_ref[:, 0:1]
    gi = lax.broadcasted_iota(I32, (gsz, tm), 0)
    gscore = []
    for g in range(N_EGROUPS):
        xg = ssel[g * gsz:(g + 1) * gsz, :]
        m1, i1 = _first_max(xg, gi, gsz)
        m2 = jnp.max(jnp.where(gi == i1, ninf, xg), axis=0, keepdims=True)
        gscore.append(m1 + m2)
    cur = jnp.concatenate(gscore, axis=0)
    gidx = lax.broadcasted_iota(I32, (N_EGROUPS, tm), 0)
    picked = jnp.zeros((N_EGROUPS, tm), F32)
    for _ in range(TOPK_GROUPS):
        _, first = _first_max(cur, gidx, N_EGROUPS)
        hit = gidx == first
        picked = jnp.where(hit, 1.0, picked)
        cur = jnp.where(hit, ninf, cur)
    x = jnp.concatenate(
        [jnp.where(picked[g:g + 1, :] > 0.5, ssel[g * gsz:(g + 1) * gsz, :], ninf) for g in range(N_EGROUPS)], axis=0)
    ei = lax.broadcasted_iota(I32, (N_EXPERTS, tm), 0)
    idxs, ws = [], []
    for _ in range(TOP_K):
        _, first = _first_max(x, ei, N_EXPERTS)
        hit = ei == first
        idxs.append(first)
        ws.append(jnp.sum(jnp.where(hit, s, 0.0), axis=0, keepdims=True))
        x = jnp.where(hit, ninf, x)
    w = jnp.concatenate(ws, axis=0)
    e_ref[...] = jnp.concatenate(idxs, axis=0)
    w_ref[...] = w / jnp.sum(w, axis=0, keepdims=True) * ROUTE_SCALE


def moe_router(h2, rwt, bias_b, tm):
    N, D = h2.shape
    return pl.pallas_call(
        _router_kernel,
        out_shape=[jax.ShapeDtypeStruct((TOP_K, N), I32), jax.ShapeDtypeStruct((TOP_K, N), F32)],
        grid=(N // tm,),
        in_specs=[pl.BlockSpec((tm, D), lambda i: (i, 0)),
                  pl.BlockSpec((N_EXPERTS, D), lambda i: (0, 0)),
                  pl.BlockSpec((N_EXPERTS, LANE), lambda i: (0, 0))],
        out_specs=[pl.BlockSpec((TOP_K, tm), lambda i: (0, i)),
                   pl.BlockSpec((TOP_K, tm), lambda i: (0, i))],
        compiler_params=_cp(("arbitrary",)),
        name="moe_router",
    )(h2, rwt, bias_b)


def _moe_count_kernel(e_ref, pstart_ref, plan_ref, blk_ref, cnt_scr, *, nt, nbp):
    i = pl.program_id(0)
    tp = e_ref.shape[1]

    @pl.when(i == 0)
    def _():
        cnt_scr[...] = jnp.zeros_like(cnt_scr)

    ei = lax.broadcasted_iota(I32, (N_EXPERTS, tp), 0)
    acc = jnp.zeros((N_EXPERTS, 1), F32)
    for k in range(TOP_K):
        acc = acc + jnp.sum(jnp.where(ei == e_ref[k:k + 1, :], 1.0, 0.0), axis=1, keepdims=True)
    cnt_scr[...] = cnt_scr[...] + acc

    @pl.when(i == nt - 1)
    def _():
        cnt = cnt_scr[...].astype(I32)
        padded = (cnt + (MOE_BLK - 1)) & (-MOE_BLK)
        r = lax.broadcasted_iota(I32, (N_EXPERTS, N_EXPERTS), 0)
        c = lax.broadcasted_iota(I32, (N_EXPERTS, N_EXPERTS), 1)
        tri = jnp.where(c <= r, 1.0, 0.0).astype(BF16)
        padded_b = jnp.broadcast_to(padded.astype(F32), (N_EXPERTS, LANE))
        p_end = _dot_exact_lhs(tri, padded_b)
        pstart = p_end - padded_b
        pstart_ref[...] = pstart.astype(I32)
        diag = r == c
        ps_row = jnp.sum(jnp.where(diag, pstart[:, 0:1], 0.0), axis=0, keepdims=True)
        cnt_row = jnp.sum(jnp.where(diag, cnt_scr[...], 0.0), axis=0, keepdims=True)
        plan_ref[...] = jnp.concatenate([ps_row, cnt_row, jnp.zeros((SUB - 2, N_EXPERTS), F32)], axis=0).astype(I32)
        lim =(lax.broadcasted_iota(I32, (N_EXPERTS, nbp), 1) * MOE_BLK).astype(F32)
        be = jnp.sum(jnp.where(p_end[:, 0:1] <= lim, 1.0, 0.0), axis=0, keepdims=True)
        be = jnp.minimum(be, N_EXPERTS - 1.0)
        nused = jnp.max(p_end[:, 0:1], axis=0, keepdims=True) * (1.0 / MOE_BLK)
        blk_ref[...] = jnp.concatenate([jnp.broadcast_to(be, (SUB // 2, nbp)),
                                        jnp.broadcast_to(nused, (SUB // 2, nbp))], axis=0).astype(I32)


def moe_counts(eidx, tp, nbp):
    N = eidx.shape[1]
    nt = N // tp
    kern = functools.partial(_moe_count_kernel, nt=nt, nbp=nbp)
    return pl.pallas_call(
        kern,
        out_shape=[jax.ShapeDtypeStruct((N_EXPERTS, LANE), I32),
                   jax.ShapeDtypeStruct((SUB, N_EXPERTS), I32),
                   jax.ShapeDtypeStruct((SUB, nbp), I32)],
        grid=(nt,),
        in_specs=[pl.BlockSpec((TOP_K, tp), lambda i: (0, i))],
        out_specs=[pl.BlockSpec((N_EXPERTS, LANE), lambda i: (0, 0)),
                   pl.BlockSpec((SUB, N_EXPERTS), lambda i: (0, 0)),
                   pl.BlockSpec((SUB, nbp), lambda i: (0, 0))],
        scratch_shapes=[pltpu.VMEM((N_EXPERTS, 1), F32)],
        compiler_params=_cp(("arbitrary",)),
        name="moe_counts",
    )(eidx)


def _moe_dest_kernel(e_ref, pstart_ref, dest_ref, base_scr):
    i = pl.program_id(0)
    tp = e_ref.shape[1]

    @pl.when(i == 0)
    def _():
        base_scr[...] = pstart_ref[:, 0:1].astype(F32)

    ei = lax.broadcasted_iota(I32, (N_EXPERTS, tp), 0)
    r = lax.broadcasted_iota(I32, (tp, tp), 0)
    c = lax.broadcasted_iota(I32, (tp, tp), 1)
    tri = jnp.where(r <= c, 1.0, 0.0).astype(BF16)
    base = base_scr[...]
    rows = []
    for k in range(TOP_K):
        hit = ei == e_ref[k:k + 1, :]
        oh = jnp.where(hit, 1.0, 0.0)
        cum = _bdot(oh, tri)
        rows.append(jnp.sum(jnp.where(hit, cum - 1.0 + base, 0.0), axis=0, keepdims=True))
        base = base + cum[:, tp - 1:tp]
    base_scr[...] = base
    dest_ref[...] = jnp.concatenate(rows, axis=0).astype(I32)


def moe_dest(eidx, pstart, tp):
    N = eidx.shape[1]
    return pl.pallas_call(
        _moe_dest_kernel,
        out_shape=jax.ShapeDtypeStruct((TOP_K, N), I32),
        grid=(N // tp,),
        in_specs=[pl.BlockSpec((TOP_K, tp), lambda i: (0, i)),
                  pl.BlockSpec((N_EXPERTS, LANE), lambda i: (0, 0))],
        out_specs=pl.BlockSpec((TOP_K, tp), lambda i: (0, i)),
        scratch_shapes=[pltpu.VMEM((N_EXPERTS, 1), F32)],
        compiler_params=_cp(("arbitrary",)),
        name="moe_dest",
    )(eidx, pstart)


def _tile_at(ref, token):
    return ref.at[pl.ds(pl.multiple_of(token * ROWT, ROWT), ROWT), :]


SCAT_RING = 3


def _moe_scatter_kernel(dest_ref, h_ref, xs_hbm, ring, sem, *, nt):
    s = pl.program_id(0)
    tp = dest_ref.shape[1]
    slot = lax.rem(s, SCAT_RING)
    total = xs_hbm.at[pl.ds(0, tp * TOP_K * ROWT), :]

    def slot_wait(sl):
        pltpu.make_async_copy(total, total, sem.at[sl]).wait()

    @pl.when(s >= SCAT_RING)
    def _():
        slot_wait(slot)

    ring[slot] = h_ref[...]

    def issue(j, carry):
        for k in range(TOP_K):
            pltpu.make_async_copy(_tile_at(ring.at[slot], j), _tile_at(xs_hbm, dest_ref[k, j]),
                                  sem.at[slot]).start(priority=k % 2)
        return carry

    lax.fori_loop(0, tp, issue, 0)

    @pl.when(s == nt - 1)
    def _():
        for back in range(SCAT_RING - 1, -1, -1):
            if nt - 1 - back >= 0:
                slot_wait((nt - 1 - back) % SCAT_RING)


def moe_scatter(dest, h2t, n_slots, tp):
    N = dest.shape[1]
    nt = N // tp
    return pl.pallas_call(
        functools.partial(_moe_scatter_kernel, nt=nt),
        out_shape=jax.ShapeDtypeStruct((n_slots * ROWT, LANE), U32),
        grid=(nt,),
        in_specs=[pl.BlockSpec((TOP_K, tp), lambda i: (0, i), memory_space=pltpu.SMEM),
                  pl.BlockSpec((tp * ROWT, LANE), lambda i: (i, 0))],
        out_specs=pl.BlockSpec(memory_space=pl.ANY),
        scratch_shapes=[pltpu.VMEM((SCAT_RING, tp * ROWT, LANE), U32), pltpu.SemaphoreType.DMA((SCAT_RING,))],
        compiler_params=_cp(("arbitrary",)),
        name="moe_scatter",
    )(dest, h2t)


PAD_BITS = tuple(1 << b for b in reversed(range(MOE_BLK.bit_length() - 1)))


def _moe_padfill_kernel(ps_ref, xs_in, xs_hbm, zero_scr, sem):
    del xs_in
    zero_scr[...] = jnp.zeros_like(zero_scr)

    def pad_copies(e, wait):
        cnt = ps_ref[1, e]
        npad = ((cnt + (MOE_BLK - 1)) & (-MOE_BLK)) - cnt
        off = ps_ref[0, e] + cnt
        for bit in PAD_BITS:
            @pl.when((npad & bit) != 0)
            def _():
                cp = pltpu.make_async_copy(zero_scr.at[pl.ds(0, bit * ROWT), :],
                                           xs_hbm.at[pl.ds(pl.multiple_of(off * ROWT, ROWT), bit * ROWT), :], sem)
                if wait:
                    cp.wait()
                else:
                    cp.start()
            off = off + (npad & bit)

    def issue(e, carry):
        pad_copies(e, False)
        return carry

    def drain(e, carry):
        pad_copies(e, True)
        return carry

    lax.fori_loop(0, N_EXPERTS, issue, 0)
    lax.fori_loop(0, N_EXPERTS, drain, 0)


def moe_padfill(pstart, xs):
    return pl.pallas_call(
        _moe_padfill_kernel,
        out_shape=jax.ShapeDtypeStruct(xs.shape, xs.dtype),
        grid=(1,),
        in_specs=[pl.BlockSpec(memory_space=pltpu.SMEM),
                  pl.BlockSpec(memory_space=pl.ANY)],
        out_specs=pl.BlockSpec(memory_space=pl.ANY),
        scratch_shapes=[pltpu.VMEM((PAD_BITS[0] * ROWT, LANE), U32), pltpu.SemaphoreType.DMA(())],
        input_output_aliases={1: 0},
        compiler_params=_cp(("arbitrary",)),
        name="moe_padfill",
    )(pstart, xs)


EXP_NX = 8
EXP_PF = 4
EXP_NY = 4


def _experts_kernel(plan_ref, w13_ref, w2_ref, xs_hbm, y_hbm, w13_bf, w2_bf, x_buf, y_buf, sem_in, sem_out):
    e = pl.program_id(0)
    ne = pl.num_programs(0)
    ff = w2_bf.shape[0]
    rows = MOE_BLK * ROWT
    shift = MOE_BLK.bit_length() - 1
    blocks_of = lambda c: lax.shift_right_logical(c + (MOE_BLK - 1), shift)
    nblk = blocks_of(plan_ref[1, e])
    first = lax.shift_right_logical(plan_ref[0, e], shift)
    nused = lax.shift_right_logical(plan_ref[0, ne - 1], shift) + blocks_of(plan_ref[1, ne - 1])

    def window(ref, g):
        return ref.at[pl.ds(pl.multiple_of(g * rows, rows), rows), :]

    def x_copy(g):
        return pltpu.make_async_copy(window(xs_hbm, g), x_buf.at[g & (EXP_NX - 1)], sem_in.at[g & (EXP_NX - 1)])

    def y_copy(g):
        return pltpu.make_async_copy(y_buf.at[g & (EXP_NY - 1)], window(y_hbm, g), sem_out.at[g & (EXP_NY - 1)])

    @pl.when(e == 0)
    def _():
        for g in range(EXP_PF):
            @pl.when(g < nused)
            def _():
                x_copy(g).start()

    def process(gs):
        for g in gs:
            x_copy(g).wait()

            @pl.when(g + EXP_PF < nused)
            def _():
                x_copy(g + EXP_PF).start()

            @pl.when(g >= EXP_NY)
            def _():
                y_copy(g - EXP_NY).wait()

        xs = [_tile_rows_load(x_buf.at[g & (EXP_NX - 1)], MOE_BLK).astype(BF16) for g in gs]
        hs = [jnp.dot(x, w13_bf[...], preferred_element_type=F32) for x in xs]
        acts = [(_silu(h[:, 0:ff]) * h[:, ff:2 * ff]).astype(BF16) for h in hs]
        ys = [jnp.dot(a, w2_bf[...], preferred_element_type=F32) for a in acts]
        for g, y in zip(gs, ys):
            _tile_rows_store(y_buf.at[g & (EXP_NY - 1)], y)
            y_copy(g).start()

    @pl.when(nblk > 0)
    def _():
        w13_bf[...] = w13_ref[0, 0].astype(BF16)
        w2_bf[...] = w2_ref[0, 0].astype(BF16)
        npair = lax.shift_right_logical(nblk, 1)

        def body(j, carry):
            g = first + 2 * j
            process([g, g + 1])
            return carry

        lax.fori_loop(0, npair, body, 0)

        @pl.when((nblk & 1) == 1)
        def _():
            process([first + nblk - 1])

    @pl.when(e == ne - 1)
    def _():
        for back in range(EXP_NY, 0, -1):
            @pl.when(nused >= back)
            def _():
                y_copy(nused - back).wait()


def moe_experts(plan, xs, w13, w2, layer):
    _, E, D, F2 = w13.shape
    rows = MOE_BLK * ROWT
    grid_spec = pltpu.PrefetchScalarGridSpec(
        num_scalar_prefetch=1,
        grid=(E,),
        in_specs=[pl.BlockSpec((1, 1, D, F2), lambda e, p: (layer, e, 0, 0)),
                  pl.BlockSpec((1, 1, F2 // 2, D), lambda e, p: (layer, e, 0, 0)),
                  pl.BlockSpec(memory_space=pl.ANY)],
        out_specs=pl.BlockSpec(memory_space=pl.ANY),
        scratch_shapes=[pltpu.VMEM((D, F2), BF16), pltpu.VMEM((F2 // 2, D), BF16),
                        pltpu.VMEM((EXP_NX, rows, LANE), U32), pltpu.VMEM((EXP_NY, rows, LANE), U32),
                        pltpu.SemaphoreType.DMA((EXP_NX,)), pltpu.SemaphoreType.DMA((EXP_NY,))])
    return pl.pallas_call(
        _experts_kernel,
        out_shape=jax.ShapeDtypeStruct(xs.shape, U32),
        grid_spec=grid_spec,
        compiler_params=_cp(("arbitrary",)),
        name="moe_experts",
    )(plan, w13, w2, xs)


def _moe_combine_kernel(dest_ref, wt_ref, h_ref, x_ref, mod_ref, s13_ref, s2_ref, l2g_ref, l2b_ref, y_hbm,
                        o_ref, g_scr, sem, *, alpha, nt):
    s = pl.program_id(0)
    tm = h_ref.shape[0]
    ff = s2_ref.shape[0]
    rows = tm * ROWT
    slot_g = s % 2
    slot_c = 1 - slot_g

    def slot_copy(slot):
        return pltpu.make_async_copy(y_hbm.at[pl.ds(0, TOP_K * rows), :], g_scr.at[slot], sem.at[slot])

    @pl.when(s == 0)
    def _():
        g_scr[1] = jnp.zeros((TOP_K * rows, LANE), U32)

    @pl.when(s > 0)
    def _():
        slot_copy(slot_c).wait()

    for j in range(tm):
        for k in range(TOP_K):
            pltpu.make_async_copy(_tile_at(y_hbm, dest_ref[k, j]),
                                  g_scr.at[slot_g, pl.ds(k * rows + j * ROWT, ROWT), :],
                                  sem.at[slot_g]).start(priority=k % 2)
    hs = _bdot(h_ref[...], s13_ref[...])
    f = _bdot(_silu(hs[:, 0:ff]) * hs[:, ff:2 * ff], s2_ref[...])
    for k in range(TOP_K):
        f = f + wt_ref[:, k:k + 1] * _tile_rows_load(g_scr.at[slot_c], tm, base=k * rows)
    o_ref[...] = (_layer_norm(alpha * x_ref[...] + mod_ref[0, 5:6, :] * f) * l2g_ref[...] + l2b_ref[...])

    @pl.when(s == nt)
    def _():
        slot_copy(slot_g).wait()


def moe_combine(dest, wt_t, h2, x1, mod3, wts, y, row_of_tile, tm, alpha):
    N, D = h2.shape
    nt = N // tm
    full = lambda a: pl.BlockSpec(a.shape, lambda s: (0,) * a.ndim)
    kern = functools.partial(_moe_combine_kernel, alpha=alpha, nt=nt)
    nxt = lambda s: jnp.minimum(s, nt - 1)
    cur = lambda s: jnp.maximum(s - 1, 0)
    return pl.pallas_call(
        kern,
        out_shape=jax.ShapeDtypeStruct((N, D), F32),
        grid=(nt + 1,),
        in_specs=[pl.BlockSpec((TOP_K, tm), lambda s: (0, nxt(s)), memory_space=pltpu.SMEM),
                  pl.BlockSpec((tm, TOP_K), lambda s: (cur(s), 0)),
                  pl.BlockSpec((tm, D), lambda s: (cur(s), 0)),
                  pl.BlockSpec((tm, D), lambda s: (cur(s), 0)),
                  pl.BlockSpec((1, 6, D), lambda s: (row_of_tile(cur(s)), 0, 0))]
                 + [full(a) for a in wts] + [pl.BlockSpec(memory_space=pl.ANY)],
        out_specs=pl.BlockSpec((tm, D), lambda s: (cur(s), 0)),
        scratch_shapes=[pltpu.VMEM((2, TOP_K * tm * ROWT, LANE), U32), pltpu.SemaphoreType.DMA((2,))],
        compiler_params=_cp(("arbitrary",)),
        name="moe_combine",
    )(dest, wt_t, h2, x1, mod3, *wts, y)


def moe_ffn(h2, h2t, x1, mod3, mp, row_of_tile, tm, alpha):
    N = h2.shape[0]
    nb_total = (N * TOP_K + MOE_BLK - 1) // MOE_BLK + N_EXPERTS
    nbp = ((nb_total + LANE - 1) // LANE) * LANE
    tp = _tile(N, 256)
    eidx, wts = moe_router(h2, mp['rwt'], mp['bias'], tp)
    pstart, plan, blk = moe_counts(eidx, tp, nbp)
    dest = moe_dest(eidx, pstart, tp)
    xs = moe_scatter(dest, h2t, nb_total * MOE_BLK, tp)
    xs = moe_padfill(plan, xs)
    y = moe_experts(plan, xs, mp['w13'], mp['w2'], mp['layer'])
    return moe_combine(dest, wts.T, h2, x1, mod3, mp['comb'], y, row_of_tile, tm, alpha)


def token_mixers(stream, B, T, W, mod3, row_of_tile, lp, states, tm):
    pa, pbg, gg, u = inproj(stream, mod3, lp['w_in'], row_of_tile, tm)
    pac = short_conv(pa.reshape(B, T, PA_W), lp['taps'], W).reshape(B * T, PA_W)
    ss, lw, kb, gb = rwkv_prep(pac, lp['prep'], tm)
    s_rwkv, s_gla, s_s5 = states
    y2, f_rwkv = rwkv_scan(ss, lw, kb, s_rwkv, B, T)
    o2, f_gla = gla_scan(pbg, lp['gla_aup'], lp['gla_ab'], s_gla, B, T)
    yd, f_s5 = s5_scan(u.reshape(B, T, S5_W), lp['s5_a'], lp['s5_bbd'], lp['s5_cbd'], s_s5)
    return (y2, gb, o2, gg, yd.reshape(2, B * T, S5_W), u), (f_rwkv, f_gla, f_s5)


def zero_states(B):
    return ((jnp.zeros((B, RW // LANE, LANE, LANE), F32),) * 2,
            (jnp.zeros((B, RW, GLA_KP), F32),) * 2,
            jnp.zeros((2, B, S5_LANES), F32))


def _inproj_columns():
    r_cols = 3 * RW + 2 * DECAY_RANK + 2 * ICLR_RANK + GATE_RANK
    kd = GLA_HEADS * GLA_K
    gq, gk, gv = r_cols, r_cols + kd, r_cols + 2 * kd
    gg = gv + RW
    gad = gg + RW
    pc = gad + 2 * GLA_RANK
    z = lambda n: [-1] * n
    cols = list(range(0, r_cols)) + z(PA_W - r_cols)
    cols += list(range(gq, gq + kd)) + z(256 - kd)
    cols += list(range(gk, gk + kd)) + z(256 - kd)
    cols += list(range(gv, gv + RW))
    cols += list(range(gad, gad + 2 * GLA_RANK)) + z(LANE - 2 * GLA_RANK)
    cols += list(range(gg, gg + RW))
    cols += list(range(pc, pc + S5_W))
    assert len(cols) == NP_W
    return np.asarray(cols, np.int32)


def _head_block_ones():
    h = np.arange(RW) // HEAD
    return jnp.asarray(h[:, None] == h[None, :], BF16)


def _layer_params(l, p):
    D = p['w_in'].shape[1]
    cols = _inproj_columns()
    w_in = jnp.concatenate([p['w_in'][l], jnp.zeros((D, 1), F32)], axis=1)
    w_in = jnp.take(w_in, jnp.asarray(np.where(cols < 0, w_in.shape[1] - 1, cols)), axis=1).astype(BF16)
    taps = p['rwkv_conv'][l].reshape(9, -1)
    taps = jnp.concatenate([taps, jnp.zeros((9, PA_W - taps.shape[1]), F32)], axis=1)
    wup = jnp.zeros((LANE, 2 * RW), F32)
    aup = jnp.zeros((LANE, 2 * RW), F32)
    for d in range(2):
        wup = wup.at[d * DECAY_RANK:(d + 1) * DECAY_RANK, d * RW:(d + 1) * RW].set(p['rwkv_w_up'][l, d])
        o = 2 * DECAY_RANK + d * ICLR_RANK
        aup = aup.at[o:o + ICLR_RANK, d * RW:(d + 1) * RW].set(p['rwkv_a_up'][l, d])
    gup = jnp.zeros((LANE, RW), F32).at[0:GATE_RANK].set(p['rwkv_g_up'][l])
    row = lambda a: a.reshape(1, -1)
    prep = (wup, aup, gup, row(p['rwkv_w0'][l]), row(p['rwkv_a0'][l]), row(p['rwkv_k_k'][l]),
            row(p['rwkv_k_a'][l]), row(p['rwkv_r_k'][l]), _head_block_ones())
    kd = GLA_HEADS * GLA_K
    gla_aup = jnp.zeros((2, LANE, GLA_KP), F32)
    for d in range(2):
        gla_aup = gla_aup.at[d, d * GLA_RANK:(d + 1) * GLA_RANK, 0:kd].set(p['gla_a_up'][l, d])
    gla_ab = jnp.zeros((2, 1, GLA_KP), F32).at[:, 0, 0:kd].set(p['gla_a_bias'][l])
    lam_re, lam_im = p['s5_lam_re'][l], p['s5_lam_im'][l]
    dt = jnp.exp(p['s5_log_dt'][l])[:, :, None]
    zr, zi = lam_re[:, None, :] * dt, lam_im[:, None, :] * dt
    mag = jnp.exp(zr)
    ab_r, ab_i = mag * jnp.cos(zi), mag * jnp.sin(zi)
    den = (lam_re * lam_re + lam_im * lam_im)[:, None, :]
    f_r = ((ab_r - 1) * lam_re[:, None, :] + ab_i * lam_im[:, None, :]) / den
    f_i = (ab_i * lam_re[:, None, :] - (ab_r - 1) * lam_im[:, None, :]) / den
    b_re, b_im = p['s5_b_re'][l], p['s5_b_im'][l]
    bb_r = f_r[..., None] * b_re - f_i[..., None] * b_im
    bb_i = f_r[..., None] * b_im + f_i[..., None] * b_re
    eye_g = jnp.eye(S5_GROUPS, dtype=F32)
    half = S5_LANES // 2

    def in_blockdiag(bb):
        return jnp.einsum('dgpc,gh->dgchp', bb, eye_g).reshape(2, S5_W, half)

    def out_blockdiag(cc):
        return jnp.einsum('dgcp,gh->dgphc', cc, eye_g).reshape(2, half, S5_W)

    s5_bbd = jnp.concatenate([in_blockdiag(bb_r), in_blockdiag(bb_i)], axis=2).astype(BF16)
    s5_cbd = jnp.concatenate([out_blockdiag(p['s5_c_re'][l]), -out_blockdiag(p['s5_c_im'][l])], axis=1).astype(BF16)
    s5_a = jnp.concatenate([ab_r.reshape(2, 1, half), ab_i.reshape(2, 1, half)], axis=2)
    mix = (_head_block_ones(), row(p['rwkv_ln_g'][l]), row(p['rwkv_ln_b'][l]),
           row(jnp.tile(p['gla_norm_g'][l], GLA_HEADS)), row(p['s5_d'][l]),
           p['s5_glu_w'][l].astype(BF16), row(p['s5_glu_b'][l]), p['w_out'][l].astype(BF16),
           row(p['ln1_g'][l]), row(p['ln1_b'][l]))
    moe = dict(rwt=p['router_w'][l].T,
               bias=jnp.broadcast_to(p['router_bias'][l][:, None], (N_EXPERTS, LANE)),
               w13=p['exp_w13'], w2=p['exp_w2'], layer=l,
               comb=(p['sh_w13'][l].astype(BF16), p['sh_w2'][l].astype(BF16), row(p['ln2_g'][l]), row(p['ln2_b'][l])))
    return dict(w_in=w_in, taps=taps, prep=prep, gla_aup=gla_aup, gla_ab=gla_ab,
                s5_a=s5_a, s5_bbd=s5_bbd, s5_cbd=s5_cbd, mix=mix, moe=moe)


_ARG_NAMES = ('x', 'c', 'ctx', 'c_ctx', 'w_mod', 'b_mod', 'w_in', 'rwkv_conv', 'rwkv_w0', 'rwkv_w_up', 'rwkv_a0',
              'rwkv_a_up', 'rwkv_g_up', 'rwkv_k_k', 'rwkv_k_a', 'rwkv_r_k', 'rwkv_ln_g', 'rwkv_ln_b', 'gla_a_up',
              'gla_a_bias', 'gla_norm_g', 's5_lam_re', 's5_lam_im', 's5_log_dt', 's5_b_re', 's5_b_im', 's5_c_re',
              's5_c_im', 's5_d', 's5_glu_w', 's5_glu_b', 'w_out', 'ln1_g', 'ln1_b', 'router_w', 'router_bias',
              'exp_w13', 'exp_w2', 'sh_w13', 'sh_w2', 'ln2_g', 'ln2_b')


def _tile(n, pref):
    t = pref
    while n % t:
        t //= 2
    return t


def kernel(x, c, ctx, c_ctx, w_mod, b_mod, w_in, rwkv_conv, rwkv_w0, rwkv_w_up, rwkv_a0, rwkv_a_up, rwkv_g_up,
           rwkv_k_k, rwkv_k_a, rwkv_r_k, rwkv_ln_g, rwkv_ln_b, gla_a_up, gla_a_bias, gla_norm_g, s5_lam_re,
           s5_lam_im, s5_log_dt, s5_b_re, s5_b_im, s5_c_re, s5_c_im, s5_d, s5_glu_w, s5_glu_b, w_out, ln1_g,
           ln1_b, router_w, router_bias, exp_w13, exp_w2, sh_w13, sh_w2, ln2_g, ln2_b):
    p = dict(zip(_ARG_NAMES, (x, c, ctx, c_ctx, w_mod, b_mod, w_in, rwkv_conv, rwkv_w0, rwkv_w_up, rwkv_a0,
                              rwkv_a_up, rwkv_g_up, rwkv_k_k, rwkv_k_a, rwkv_r_k, rwkv_ln_g, rwkv_ln_b, gla_a_up,
                              gla_a_bias, gla_norm_g, s5_lam_re, s5_lam_im, s5_log_dt, s5_b_re, s5_b_im, s5_c_re,
                              s5_c_im, s5_d, s5_glu_w, s5_glu_b, w_out, ln1_g, ln1_b, router_w, router_bias,
                              exp_w13, exp_w2, sh_w13, sh_w2, ln2_g, ln2_b)))
    B, T, D = x.shape
    TC = ctx.shape[1]
    L = w_mod.shape[0]
    alpha = (2 * L) ** 0.25
    n_lat, n_ctx = B * T, B * TC
    R = ((B + 1 + SUB - 1) // SUB) * SUB
    cc = jnp.zeros((R, D), F32).at[0:B].set(c).at[B].set(c_ctx)
    mod = mod_table(cc, w_mod, b_mod)
    tm = _tile(T, 512)
    tmc = _tile(n_ctx, 512)
    tmm = min(_tile(T, 128), _tile(n_ctx, 128))
    lat_row = lambda i: (i * tm) // T
    ctx_row = lambda i: B
    lat = (x.reshape(n_lat, D), 0, n_lat)
    con = (ctx.reshape(n_ctx, D), 0, n_ctx)
    for l in range(L):
        last = l == L - 1
        lp = _layer_params(l, p)
        mod3 = mod[l].reshape(R, 6, D)
        outs_c, st_c = token_mixers(con, B, TC, TC, mod3, ctx_row, lp, zero_states(B), tmc)
        outs, _ = token_mixers(lat, B, T, GRID_W, mod3, lat_row, lp, st_c, tm)
        if last:
            x1, h2, h2t = mix_out(*outs, lat, mod3, lp['mix'], lat_row, tm, alpha)
            out = moe_ffn(h2, h2t, x1, mod3, lp['moe'], lambda i: (i * tmm) // T, tmm, alpha)
            lat = (out, 0, n_lat)
        else:
            n_all = n_ctx + n_lat
            bufs = mix_out(*outs_c, con, mod3, lp['mix'], ctx_row, tmc, alpha, out_rows=n_all)
            x1, h2, h2t = mix_out(*outs, lat, mod3, lp['mix'], lat_row, tm, alpha,
                                  out_rows=n_all, out_row0=n_ctx, into=bufs)
            row_all = lambda i: jnp.where(i * tmm < n_ctx, B, (i * tmm - n_ctx) // T)
            out = moe_ffn(h2, h2t, x1, mod3, lp['moe'], row_all, tmm, alpha)
            con, lat = (out, 0, n_ctx), (out, n_ctx, n_lat)
    out, row0, _ = lat
    return out[row0:row0 + n_lat].reshape(B, T, D)
```

```python
import functools
import math

import numpy as np
import jax
import jax.numpy as jnp
from jax import lax
from jax.experimental import pallas as pl
from jax.experimental.pallas import tpu as pltpu

F32 = jnp.float32
BF16 = jnp.bfloat16
I32 = jnp.int32
ACT = BF16

GRID_W = 64
RWKV_HEADS = 6
HEAD = 64
RW = RWKV_HEADS * HEAD
DECAY_RANK = 32
ICLR_RANK = 32
GATE_RANK = 64
RWKV_GN_EPS = 64e-5
GLA_HEADS = 6
GLA_K = 32
GLA_RANK = 16
GLA_TEMP = 16.0
S5_GROUPS = 16
S5_GROUP = 16
S5_STATE = 64
S5_W = S5_GROUPS * S5_GROUP
S5_LANES = 2 * S5_GROUPS * S5_STATE
N_EXPERTS = 256
TOP_K = 8
N_EGROUPS = 8
TOPK_GROUPS = 4
ROUTE_SCALE = 2.5
LN_EPS = 1e-6
CHUNK = 64
MOE_BLK = 256
LANE = 128
SUB = 8
VMEM_LIMIT = 56 * 1024 * 1024

PA_W = 3 * RW + 2 * LANE
PBG_W = 256 + 256 + RW + LANE
NP_W = PA_W + PBG_W + RW + S5_W


def _cp(sem):
    return pltpu.CompilerParams(dimension_semantics=sem, vmem_limit_bytes=VMEM_LIMIT)


def _sigmoid(x):
    return 1.0 / (1.0 + jnp.exp(-x))


def _silu(x):
    return x * _sigmoid(x)


def _bdot(a, b, dims=None):
    a = a.astype(BF16)
    b = b.astype(BF16)
    if dims is None:
        return jnp.dot(a, b, preferred_element_type=F32)
    return lax.dot_general(a, b, dims, preferred_element_type=F32)


def _split2(x):
    hi = x.astype(BF16)
    lo = (x - hi.astype(F32)).astype(BF16)
    return hi, lo


def _split3(x):
    hi = x.astype(BF16)
    r = x - hi.astype(F32)
    mid = r.astype(BF16)
    lo = (r - mid.astype(F32)).astype(BF16)
    return hi, mid, lo


NT_DIMS = (((1,), (1,)), ((), ()))
TN_DIMS = (((0,), (0,)), ((), ()))


def _dot33(a, b, dims=None):
    ah, al = _split2(a)
    bh, bl = _split2(b)
    return _bdot(ah, bh, dims) + (_bdot(ah, bl, dims) + _bdot(al, bh, dims))


def _dot_exact_lhs(m_exact, x, dims=None):
    h, m, l = _split3(x)
    return _bdot(m_exact, h, dims) + (_bdot(m_exact, m, dims) + _bdot(m_exact, l, dims))


def _dot_exact_rhs(x, m_exact, dims=None):
    h, m, l = _split3(x)
    return _bdot(h, m_exact, dims) + (_bdot(m, m_exact, dims) + _bdot(l, m_exact, dims))


def _dot_split_rhs(x, m_exact, dims=None):
    h, l = _split2(x)
    return _bdot(h, m_exact, dims) + _bdot(l, m_exact, dims)


def _layer_norm(x):
    mu = jnp.mean(x, axis=-1, keepdims=True)
    xc = x - mu
    var = jnp.mean(xc * xc, axis=-1, keepdims=True)
    return xc * lax.rsqrt(var + LN_EPS)


def _mod_kernel(c_ref, w_ref, b_ref, o_ref):
    s = _silu(c_ref[...])
    o_ref[0] = _dot33(s, w_ref[0]) + b_ref[0]


def mod_table(cc, w_mod, b_mod):
    L, D, D6 = w_mod.shape
    R = cc.shape[0]
    tn = 1536
    return pl.pallas_call(
        _mod_kernel,
        out_shape=jax.ShapeDtypeStruct((L, R, D6), F32),
        grid=(L, D6 // tn),
        in_specs=[pl.BlockSpec((R, D), lambda l, j: (0, 0)),
                  pl.BlockSpec((1, D, tn), lambda l, j: (l, 0, j)),
                  pl.BlockSpec((1, 1, tn), lambda l, j: (l, 0, j))],
        out_specs=pl.BlockSpec((1, R, tn), lambda l, j: (l, 0, j)),
        compiler_params=_cp(("arbitrary", "arbitrary")),
        name="mod_table",
    )(cc, w_mod, b_mod.reshape(L, 1, D6))


def _inproj_kernel(x_ref, mod_ref, w_ref, pa_ref, pbg_ref, gg_ref, u_ref):
    x = x_ref[...]
    h = _layer_norm(x) * (1.0 + mod_ref[0, 1:2, :]) + mod_ref[0, 0:1, :]
    hb = h.astype(BF16)
    o = 0
    for ref in (pa_ref, pbg_ref, gg_ref, u_ref):
        w = ref.shape[-1]
        ref[...] = jnp.dot(hb, w_ref[:, o:o + w], preferred_element_type=F32).astype(ref.dtype)
        o += w


def inproj(stream, mod3, w_bf, row_of_tile, tm):
    x2, row0, N = stream
    D = x2.shape[1]
    t0 = row0 // tm
    assert row0 % tm == 0 and N % tm == 0
    return pl.pallas_call(
        _inproj_kernel,
        out_shape=[jax.ShapeDtypeStruct((N, PA_W), ACT),
                   jax.ShapeDtypeStruct((N, PBG_W), ACT),
                   jax.ShapeDtypeStruct((N, RW), ACT),
                   jax.ShapeDtypeStruct((N, S5_W), ACT)],
        grid=(N // tm,),
        in_specs=[pl.BlockSpec((tm, D), lambda i: (i + t0, 0)),
                  pl.BlockSpec((1, 6, D), lambda i: (row_of_tile(i), 0, 0)),
                  pl.BlockSpec((D, NP_W), lambda i: (0, 0))],
        out_specs=[pl.BlockSpec((tm, PA_W), lambda i: (i, 0)),
                   pl.BlockSpec((tm, PBG_W), lambda i: (i, 0)),
                   pl.BlockSpec((tm, RW), lambda i: (i, 0)),
                   pl.BlockSpec((tm, S5_W), lambda i: (i, 0))],
        compiler_params=_cp(("arbitrary",)),
        name="inproj",
    )(x2, mod3, w_bf)


CONV_PAD = 72


def _conv_kernel(x_ref, taps_ref, o_ref, buf_ref, *, T, W, vertical):
    pad = CONV_PAD
    zeros = jnp.zeros((pad, LANE), F32)
    buf_ref[0:pad, :] = zeros
    buf_ref[pad + T:pad + T + pad, :] = zeros
    buf_ref[pad:pad + T, :] = x_ref[0].astype(F32)
    ch = min(T, 256)
    col = lax.broadcasted_iota(I32, (ch, LANE), 0) & (W - 1)
    left_ok = col >= 1
    right_ok = col <= W - 2
    for c in range(T // ch):
        base = pad + c * ch
        acc = jnp.zeros((ch, LANE), F32)
        for dr in ((0, 1, 2) if vertical else (1,)):
            for dc in range(3):
                off = (dr - 1) * W + (dc - 1)
                v = buf_ref[base + off:base + off + ch, :]
                if dc == 0:
                    v = jnp.where(left_ok, v, 0.0)
                elif dc == 2:
                    v = jnp.where(right_ok, v, 0.0)
                acc = acc + v * taps_ref[3 * dr + dc:3 * dr + dc + 1, :]
        o_ref[0, c * ch:(c + 1) * ch, :] = acc.astype(o_ref.dtype)


def short_conv(pa3, taps9, W):
    B, T, C = pa3.shape
    vertical = T > W
    assert W & (W - 1) == 0 and (not vertical or W + 1 <= CONV_PAD)
    kern = functools.partial(_conv_kernel, T=T, W=W, vertical=vertical)
    return pl.pallas_call(
        kern,
        out_shape=jax.ShapeDtypeStruct((B, T, C), ACT),
        grid=(B, C // LANE),
        in_specs=[pl.BlockSpec((1, T, LANE), lambda b, j: (b, 0, j)),
                  pl.BlockSpec((9, LANE), lambda b, j: (0, j))],
        out_specs=pl.BlockSpec((1, T, LANE), lambda b, j: (b, 0, j)),
        scratch_shapes=[pltpu.VMEM((T + 2 * CONV_PAD, LANE), F32)],
        compiler_params=_cp(("arbitrary", "arbitrary")),
        name="short_conv",
    )(pa3, taps9)


def _rwkv_prep_kernel(pa_ref, wup_ref, aup_ref, gup_ref, w0_ref, a0_ref, kk_ref, ka_ref, rk_ref, hb_ref,
                      ss_ref, lw_ref, kb_ref, gb_ref):
    r = pa_ref[:, 0:RW].astype(F32)
    k = pa_ref[:, RW:2 * RW].astype(F32)
    v = pa_ref[:, 2 * RW:3 * RW].astype(F32)
    wa = pa_ref[:, 3 * RW:3 * RW + LANE].astype(F32)
    gd = pa_ref[:, 3 * RW + LANE:3 * RW + 2 * LANE].astype(F32)
    z = w0_ref[...] + _bdot(jnp.tanh(wa), wup_ref[...])
    lw = -_sigmoid(z) * math.exp(-0.5)
    a = _sigmoid(a0_ref[...] + _bdot(wa, aup_ref[...]))
    g = _bdot(_sigmoid(gd), gup_ref[...])
    hb = hb_ref[...]
    kk = k * kk_ref[...]
    kk = kk * lax.rsqrt(_bdot(kk * kk, hb) + 1e-12)
    ka = ka_ref[...]
    ss_ref[:, 0:RW] = r.astype(ACT)
    ss_ref[:, RW:2 * RW] = v.astype(ACT)
    ss_ref[:, 2 * RW:3 * RW] = kk.astype(ACT)
    rk2 = jnp.zeros_like(r)
    for d in range(2):
        ad = a[:, d * RW:(d + 1) * RW]
        k2 = k * (1.0 + (ad - 1.0) * ka)
        lw_ref[d] = lw[:, d * RW:(d + 1) * RW]
        kb_ref[d, :, 0:RW] = k2.astype(ACT)
        kb_ref[d, :, RW:2 * RW] = (kk * ad).astype(ACT)
        rk2 = rk2 + r * k2
    bonus = _bdot(rk2 * rk_ref[...], hb) * v
    gb_ref[:, 0:RW] = g.astype(ACT)
    gb_ref[:, RW:2 * RW] = bonus.astype(ACT)


def rwkv_prep(pa2, wts, tm):
    N = pa2.shape[0]
    full = lambda a: pl.BlockSpec(a.shape, lambda i: (0,) * a.ndim)
    return pl.pallas_call(
        _rwkv_prep_kernel,
        out_shape=[jax.ShapeDtypeStruct((N, 3 * RW), ACT),
                   jax.ShapeDtypeStruct((2, N, RW), F32),
                   jax.ShapeDtypeStruct((2, N, 2 * RW), ACT),
                   jax.ShapeDtypeStruct((N, 2 * RW), ACT)],
        grid=(N // tm,),
        in_specs=[pl.BlockSpec((tm, PA_W), lambda i: (i, 0))] + [full(a) for a in wts],
        out_specs=[pl.BlockSpec((tm, 3 * RW), lambda i: (i, 0)),
                   pl.BlockSpec((2, tm, RW), lambda i: (0, i, 0)),
                   pl.BlockSpec((2, tm, 2 * RW), lambda i: (0, i, 0)),
                   pl.BlockSpec((tm, 2 * RW), lambda i: (i, 0))],
        compiler_params=_cp(("arbitrary",)),
        name="rwkv_prep",
    )(pa2, *wts)


RWKV_GROUP = 16
RWKV_WAVE = 4


def _rwkv_scan_kernel(ss_ref, lw_ref, kb_ref, s0_ref, y_ref, sfin_ref, s_scr, *, nsteps, group, reverse):
    n = pl.program_id(1)
    C = CHUNK
    P = 2 * C
    npair = RW // LANE

    @pl.when(n == 0)
    def _():
        s_scr[...] = s0_ref[0]

    row = lax.broadcasted_iota(I32, (P, P), 0)
    col = lax.broadcasted_iota(I32, (P, P), 1)
    same = (row >> 6) == (col >> 6)
    dlt = (col & (C - 1)) - (row & (C - 1)) if reverse else (row & (C - 1)) - (col & (C - 1))
    strict = same & (dlt > 0)
    incl = same & (dlt >= 0)
    eye = (row == col).astype(F32)
    lvl_masks = [((row >> (lvl + 1)) == (col >> (lvl + 1))) & ((row >> lvl) != (col >> lvl)) for lvl in range(6)]
    rc = lax.broadcasted_iota(I32, (C, C), 0)
    cc = lax.broadcasted_iota(I32, (C, C), 1)
    tri = jnp.where((cc >= rc) if reverse else (rc >= cc), 1.0, 0.0).astype(BF16)
    head0 = lax.broadcasted_iota(I32, (C, LANE), 1) < HEAD

    def stack(x):
        return jnp.concatenate([jnp.where(head0, x, 0.0), jnp.where(head0, 0.0, x)], axis=0)

    pairs = range(npair)
    tm_ = {}

    def independent_stages(chunks):
        streams = [(g, p) for g in chunks for p in pairs]

        def st_cumsum():
            for (g, p) in streams:
                lw = lw_ref[0, g * C:(g + 1) * C, p * LANE:(p + 1) * LANE]
                tm_[(g, p)] = dict(lw=lw, cl=_dot_exact_lhs(tri, lw))

        def st_scores():
            for (g, p) in streams:
                t = tm_[(g, p)]
                t0, t1 = g * C, (g + 1) * C
                lo, hi = p * LANE, (p + 1) * LANE
                r = ss_ref[t0:t1, lo:hi].astype(F32)
                v = ss_ref[t0:t1, RW + lo:RW + hi].astype(F32)
                kk = ss_ref[t0:t1, 2 * RW + lo:2 * RW + hi].astype(F32)
                k2 = kb_ref[0, t0:t1, lo:hi].astype(F32)
                b = kb_ref[0, t0:t1, RW + lo:RW + hi].astype(F32)
                cl, lw = t['cl'], t['lw']
                t['ptot'] = jnp.exp(jnp.sum(lw, axis=0, keepdims=True))
                pinv = jnp.exp(-cl)
                left = jnp.concatenate([stack(-kk * jnp.exp(cl - lw)), stack(r * jnp.exp(cl))], axis=0)
                right = jnp.concatenate([stack(b * pinv), stack(k2 * pinv)], axis=0)
                t['v_st'] = stack(v)
                t['left'] = left.astype(BF16)
                t['bk'] = (right * t['ptot']).astype(BF16)
                aa = _bdot(left, right, NT_DIMS)
                t['nmat'] = jnp.where(strict, aa[0:P, 0:P], 0.0)
                t['a_ak'] = jnp.where(strict, aa[0:P, P:2 * P], 0.0)
                t['a_rbk'] = jnp.concatenate([jnp.where(incl, aa[P:2 * P, 0:P], 0.0),
                                              jnp.where(incl, aa[P:2 * P, P:2 * P], 0.0)], axis=1).astype(BF16)
                t['tinv'] = eye + jnp.where(lvl_masks[0], t['nmat'], 0.0)

        def st_akv():
            for sk in streams:
                t = tm_[sk]
                t['akv'] = _bdot(t['a_ak'], t['v_st'])

        def st_et(m):
            def run():
                for sk in streams:
                    t = tm_[sk]
                    t['et'] = _bdot(jnp.where(m, t['nmat'], 0.0), t['tinv'])
            return run

        def st_tinv():
            for sk in streams:
                t = tm_[sk]
                t['tinv'] = t['tinv'] + _bdot(t['tinv'], t['et'])

        stages = [st_cumsum, st_scores, st_akv]
        for m in lvl_masks[1:]:
            stages += [st_et(m), st_tinv]
        return stages

    state = [s_scr[p] for p in pairs]
    carry = {}

    def dependent_stages(chunks):
        stages = []
        for g in chunks:
            def st_as(g=g):
                carry['a_s'] = [_bdot(tm_[(g, p)]['left'], state[p], NT_DIMS) for p in pairs]

            def st_u(g=g):
                u = [_bdot(tm_[(g, p)]['tinv'], carry['a_s'][p][0:P] + tm_[(g, p)]['akv']) for p in pairs]
                carry['uv'] = [jnp.concatenate([u[p], tm_[(g, p)]['v_st']], axis=0) for p in pairs]

            def st_state(g=g):
                for p in pairs:
                    t = tm_[(g, p)]
                    state[p] = state[p] * t['ptot'] + _bdot(carry['uv'][p], t['bk'], TN_DIMS)

            def st_y(g=g, a_s=None):
                for p in pairs:
                    y_st = carry['a_s'][p][P:2 * P] + _bdot(tm_[(g, p)]['a_rbk'], carry['uv'][p])
                    y_ref[g * C:(g + 1) * C, p * LANE:(p + 1) * LANE] = (y_st[0:C] + y_st[C:P]).astype(y_ref.dtype)

            stages += [st_as, st_u, st_y, st_state]
        return stages

    order = list(range(group - 1, -1, -1) if reverse else range(group))
    wave = RWKV_WAVE if group >= 2 * RWKV_WAVE else group
    pending = []
    for w in range(0, group, wave):
        ind = independent_stages(order[w:w + wave])
        for i in range(max(len(ind), len(pending))):
            if i < len(ind):
                ind[i]()
            if i < len(pending):
                pending[i]()
        pending = dependent_stages(order[w:w + wave])
    for stage in pending:
        stage()
    for p in pairs:
        s_scr[p] = state[p]

    @pl.when(n == nsteps - 1)
    def _():
        sfin_ref[0] = s_scr[...]


def rwkv_scan_dir(ss, lw, kb, s0, B, T, d):
    nc = T // CHUNK
    group = min(RWKV_GROUP, nc)
    nsteps = nc // group
    npair = RW // LANE
    rows = group * CHUNK
    blk = (lambda b, n: b * nsteps + nsteps - 1 - n) if d else (lambda b, n: b * nsteps + n)
    kern = functools.partial(_rwkv_scan_kernel, nsteps=nsteps, group=group, reverse=bool(d))
    return pl.pallas_call(
        kern,
        out_shape=[jax.ShapeDtypeStruct((B * T, RW), ACT),
                   jax.ShapeDtypeStruct((B, npair, LANE, LANE), F32)],
        grid=(B, nsteps),
        in_specs=[pl.BlockSpec((rows, 3 * RW), lambda b, n: (blk(b, n), 0)),
                  pl.BlockSpec((1, rows, RW), lambda b, n: (d, blk(b, n), 0)),
                  pl.BlockSpec((1, rows, 2 * RW), lambda b, n: (d, blk(b, n), 0)),
                  pl.BlockSpec((1, npair, LANE, LANE), lambda b, n: (b, 0, 0, 0))],
        out_specs=[pl.BlockSpec((rows, RW), lambda b, n: (blk(b, n), 0)),
                   pl.BlockSpec((1, npair, LANE, LANE), lambda b, n: (b, 0, 0, 0))],
        scratch_shapes=[pltpu.VMEM((npair, LANE, LANE), F32)],
        compiler_params=_cp(("arbitrary", "arbitrary")),
        name="rwkv_scan_bwd" if d else "rwkv_scan_fwd",
    )(ss, lw, kb, s0)


def rwkv_scan(ss, lw, kb, s0, B, T):
    y0, f0 = rwkv_scan_dir(ss, lw, kb, s0[0], B, T, 0)
    y1, f1 = rwkv_scan_dir(ss, lw, kb, s0[1], B, T, 1)
    return (y0, y1), (f0, f1)


GLA_KP = 256


GLA_GROUP = 8


def _gla_kernel(pbg_ref, aup_ref, ab_ref, s0_ref, o_ref, sfin_ref, s_scr, *, nsteps, group, reverse):
    n = pl.program_id(1)
    C = CHUNK
    R = group * C

    @pl.when(n == 0)
    def _():
        s_scr[...] = s0_ref[0]

    q = pbg_ref[:, 0:GLA_KP].astype(F32)
    k = pbg_ref[:, GLA_KP:2 * GLA_KP].astype(F32)
    v = pbg_ref[:, 2 * GLA_KP:2 * GLA_KP + RW].astype(F32)
    ad = pbg_ref[:, 2 * GLA_KP + RW:2 * GLA_KP + RW + LANE].astype(F32)
    x = _dot33(ad, aup_ref[0]) + ab_ref[0]
    la = (jnp.minimum(x, 0.0) - jnp.log(1.0 + jnp.exp(-jnp.abs(x)))) * (1.0 / GLA_TEMP)
    chunks = range(group)
    sl = [slice(g * C, (g + 1) * C) for g in chunks]
    rr = lax.broadcasted_iota(I32, (C, C), 0)
    rc = lax.broadcasted_iota(I32, (C, C), 1)
    tri = jnp.where((rc >= rr) if reverse else (rr >= rc), 1.0, 0.0).astype(BF16)
    bcums = [_dot_exact_lhs(tri, la[sl[g]]) for g in chunks]
    last = 0 if reverse else C - 1
    tots = [bc[last:last + 1, :] for bc in bcums]
    bcum = jnp.concatenate(bcums, axis=0)
    tot = jnp.concatenate([jnp.broadcast_to(t, (C, GLA_KP)) for t in tots], axis=0)
    q_in = q * jnp.exp(bcum) * (GLA_K ** -0.5)
    k_in = k * jnp.exp(-bcum)
    k_st = k * jnp.exp(tot - bcum)
    dn = [jnp.exp(t) for t in tots]
    klane = lax.broadcasted_iota(I32, (C, GLA_KP), 1)
    rt = lax.broadcasted_iota(I32, (GLA_HEADS * C, C), 0) & (C - 1)
    ct = lax.broadcasted_iota(I32, (GLA_HEADS * C, C), 1)
    causal = (ct >= rt) if reverse else (rt >= ct)
    vlane = lax.broadcasted_iota(I32, (C, RW), 1)
    sv = lax.broadcasted_iota(I32, (RW, GLA_KP), 0) >> 6
    sk = lax.broadcasted_iota(I32, (RW, GLA_KP), 1) >> 5
    q_rows = [jnp.concatenate([jnp.where((klane >> 5) == h, q_in[sl[g]], 0.0) for h in range(GLA_HEADS)],
                              axis=0).astype(BF16) for g in chunks]
    att = [jnp.where(causal, _bdot(q_rows[g], k_in[sl[g]], NT_DIMS), 0.0) for g in chunks]
    o_rows = [_bdot(att[g], v[sl[g]]) for g in chunks]
    kv = [jnp.where(sv == sk, _bdot(v[sl[g]], k_st[sl[g]], TN_DIMS), 0.0) for g in chunks]
    s = s_scr[...]
    for g in (reversed(chunks) if reverse else chunks):
        o = _bdot(q_in[sl[g]], s, NT_DIMS)
        for h in range(GLA_HEADS):
            o = o + jnp.where((vlane >> 6) == h, o_rows[g][h * C:(h + 1) * C], 0.0)
        o_ref[sl[g], :] = o.astype(o_ref.dtype)
        s = s * dn[g] + kv[g]
    s_scr[...] = s

    @pl.when(n == nsteps - 1)
    def _():
        sfin_ref[0] = s_scr[...]


def gla_scan_dir(pbg, aup, ab, s0, B, T, d):
    nc = T // CHUNK
    group = min(GLA_GROUP, nc)
    nsteps = nc // group
    rows = group * CHUNK
    blk = (lambda b, n: b * nsteps + nsteps - 1 - n) if d else (lambda b, n: b * nsteps + n)
    kern = functools.partial(_gla_kernel, nsteps=nsteps, group=group, reverse=bool(d))
    return pl.pallas_call(
        kern,
        out_shape=[jax.ShapeDtypeStruct((B * T, RW), ACT),
                   jax.ShapeDtypeStruct((B, RW, GLA_KP), F32)],
        grid=(B, nsteps),
        in_specs=[pl.BlockSpec((rows, PBG_W), lambda b, n: (blk(b, n), 0)),
                  pl.BlockSpec((1, LANE, GLA_KP), lambda b, n: (d, 0, 0)),
                  pl.BlockSpec((1, 1, GLA_KP), lambda b, n: (d, 0, 0)),
                  pl.BlockSpec((1, RW, GLA_KP), lambda b, n: (b, 0, 0))],
        out_specs=[pl.BlockSpec((rows, RW), lambda b, n: (blk(b, n), 0)),
                   pl.BlockSpec((1, RW, GLA_KP), lambda b, n: (b, 0, 0))],
        scratch_shapes=[pltpu.VMEM((RW, GLA_KP), F32)],
        compiler_params=_cp(("arbitrary", "arbitrary")),
        name="gla_scan_bwd" if d else "gla_scan_fwd",
    )(pbg, aup, ab, s0)


def gla_scan(pbg, aup, ab, s0, B, T):
    o0, f0 = gla_scan_dir(pbg, aup, ab, s0[0], B, T, 0)
    o1, f1 = gla_scan_dir(pbg, aup, ab, s0[1], B, T, 1)
    return (o0, o1), (f0, f1)


S5_CHUNK = 128
S5_PITCH = 132


def _s5_kernel(u_ref, a_ref, bbd_ref, cbd_ref, s0_ref, y_ref, sfin_ref, x_scr, st_scr, *, nc, nb):
    d = pl.program_id(0)
    n = pl.program_id(1)
    C = u_ref.shape[1]
    half = S5_LANES // 2

    @pl.when(n == 0)
    def _():
        st_scr[...] = s0_ref[0]

    nslab = S5_LANES // LANE
    bu = _bdot(jnp.concatenate([u_ref[b] for b in range(nb)], axis=0), bbd_ref[0])
    for b in range(nb):
        for j in range(nslab):
            x_scr[j, b * S5_PITCH:b * S5_PITCH + C, :] = bu[b * C:(b + 1) * C, j * LANE:(j + 1) * LANE]
    a_re = a_ref[0, :, 0:half]
    a_im = a_ref[0, :, half:S5_LANES]

    def step(i, carry):
        re, im = carry
        t = i + d * (C - 1 - 2 * i)
        rows = pl.ds(t, nb, stride=S5_PITCH)
        bu = jnp.concatenate([x_scr[j, rows, :] for j in range(nslab)], axis=1)
        nre = a_re * re - a_im * im + bu[:, 0:half]
        nim = a_re * im + a_im * re + bu[:, half:S5_LANES]
        for j in range(nslab // 2):
            x_scr[j, rows, :] = nre[:, j * LANE:(j + 1) * LANE]
            x_scr[nslab // 2 + j, rows, :] = nim[:, j * LANE:(j + 1) * LANE]
        return nre, nim

    st = st_scr[...]
    re, im = lax.fori_loop(0, C, step, (st[:, 0:half], st[:, half:S5_LANES]), unroll=4)
    st_scr[...] = jnp.concatenate([re, im], axis=1)
    xs = jnp.concatenate(
        [jnp.concatenate([x_scr[j, b * S5_PITCH:b * S5_PITCH + C, :] for j in range(nslab)], axis=1).astype(BF16)
         for b in range(nb)], axis=0)
    y = _bdot(xs, cbd_ref[0])
    for b in range(nb):
        y_ref[0, b] = y[b * C:(b + 1) * C, :].astype(y_ref.dtype)

    @pl.when(n == nc - 1)
    def _():
        sfin_ref[0] = st_scr[...]


def s5_scan(u3, a_bar, bbd, cbd, s0):
    B, T, _ = u3.shape
    C = min(S5_CHUNK, T)
    assert C + 4 <= S5_PITCH
    nc = T // C
    chunk = lambda d, n: n + d * (nc - 1 - 2 * n)
    kern = functools.partial(_s5_kernel, nc=nc, nb=B)
    return pl.pallas_call(
        kern,
        out_shape=[jax.ShapeDtypeStruct((2, B, T, S5_W), ACT),
                   jax.ShapeDtypeStruct((2, B, S5_LANES), F32)],
        grid=(2, nc),
        in_specs=[pl.BlockSpec((B, C, S5_W), lambda d, n: (0, chunk(d, n), 0)),
                  pl.BlockSpec((1, 1, S5_LANES), lambda d, n: (d, 0, 0)),
                  pl.BlockSpec((1, S5_W, S5_LANES), lambda d, n: (d, 0, 0)),
                  pl.BlockSpec((1, S5_LANES, S5_W), lambda d, n: (d, 0, 0)),
                  pl.BlockSpec((1, B, S5_LANES), lambda d, n: (d, 0, 0))],
        out_specs=[pl.BlockSpec((1, B, C, S5_W), lambda d, n: (d, 0, chunk(d, n), 0)),
                   pl.BlockSpec((1, B, S5_LANES), lambda d, n: (d, 0, 0))],
        scratch_shapes=[pltpu.VMEM((S5_LANES // LANE, B * S5_PITCH, LANE), F32),
                        pltpu.VMEM((B, S5_LANES), F32)],
        compiler_params=_cp(("arbitrary", "arbitrary")),
        name="s5_scan",
    )(u3, a_bar, bbd, cbd, s0)


ROWT = 4
U32 = jnp.uint32


def _pack_bf16_pair(a, b):
    au = lax.bitcast_convert_type(a.astype(BF16).astype(F32), U32)
    bu = lax.bitcast_convert_type(b.astype(BF16).astype(F32), U32)
    return (au >> 16) | bu


def _unpack_bf16_pair(w):
    return (lax.bitcast_convert_type(w << 16, F32), lax.bitcast_convert_type(w & jnp.uint32(0xFFFF0000), F32))


def _tile_rows_store(ref, val):
    tm = val.shape[0]
    assert val.shape[1] == 2 * ROWT * LANE
    for s in range(ROWT):
        ref[pl.ds(s, tm, stride=ROWT), :] = _pack_bf16_pair(val[:, 2 * s * LANE:(2 * s + 1) * LANE],
                                                            val[:, (2 * s + 1) * LANE:(2 * s + 2) * LANE])


def _tile_rows_load(ref, tm, base=0):
    parts = []
    for s in range(ROWT):
        parts.extend(_unpack_bf16_pair(ref[pl.ds(base + s, tm, stride=ROWT), :]))
    return jnp.concatenate(parts, axis=1)


def _mix_out_kernel(y0_ref, y1_ref, gb_ref, o0_ref, o1_ref, gg_ref, yd_ref, u_ref, x_ref, mod_ref,
                    hb_ref, lng_ref, lnb_ref, gng_ref, s5d_ref, gluw_ref, glub_ref, wout_ref, l1g_ref, l1b_ref,
                    *rest, alpha):
    x1_ref, h2_ref, h2t_ref = rest[-3:]
    hb = hb_ref[...]
    inv = 1.0 / HEAD
    f32 = lambda ref_val: ref_val.astype(F32)
    y = f32(y0_ref[...]) + f32(y1_ref[...])
    yc = y - _dot_split_rhs(y, hb) * inv
    var = _bdot(yc * yc, hb) * inv
    gn = yc * lax.rsqrt(var + RWKV_GN_EPS) * lng_ref[...] + lnb_ref[...]
    y_a = (gn + f32(gb_ref[:, RW:2 * RW])) * f32(gb_ref[:, 0:RW])
    o = f32(o0_ref[...]) + f32(o1_ref[...])
    o = o * lax.rsqrt(_bdot(o * o, hb) * inv + 1e-6) * gng_ref[...]
    y_b = o * _silu(f32(gg_ref[...]))
    c = s5d_ref[...] * f32(u_ref[...]) + f32(yd_ref[0]) + f32(yd_ref[1])
    c = 0.5 * c * (1.0 + jnp.tanh(math.sqrt(2.0 / math.pi) * (c + 0.044715 * (c * c * c))))
    y_c = c * _sigmoid(_bdot(c, gluw_ref[...]) + glub_ref[...])
    y_mix = (_bdot(y_a, wout_ref[0:RW, :]) + _bdot(y_b, wout_ref[RW:2 * RW, :])
             + _bdot(y_c, wout_ref[2 * RW:2 * RW + S5_W, :]))
    x1 = _layer_norm(alpha * x_ref[...] + mod_ref[0, 2:3, :] * y_mix) * l1g_ref[...] + l1b_ref[...]
    x1_ref[...] = x1
    h2 = _layer_norm(x1) * (1.0 + mod_ref[0, 4:5, :]) + mod_ref[0, 3:4, :]
    h2_ref[...] = h2
    _tile_rows_store(h2t_ref, h2)


def mix_out(y01, gb, o01, gg, yd, u, stream, mod3, wts, row_of_tile, tm, alpha, out_rows=None, out_row0=0, into=None):
    x2, row0, N = stream
    D = x2.shape[1]
    out_rows = N if out_rows is None else out_rows
    assert row0 % tm == 0 and out_row0 % tm == 0 and N % tm == 0
    t0, o0 = row0 // tm, out_row0 // tm
    full = lambda a: pl.BlockSpec(a.shape, lambda i: (0,) * a.ndim)
    kern = functools.partial(_mix_out_kernel, alpha=alpha)
    ins = [y01[0], y01[1], gb, o01[0], o01[1], gg, yd, u, x2, mod3, *wts]
    extra = [] if into is None else list(into)
    return pl.pallas_call(
        kern,
        out_shape=[jax.ShapeDtypeStruct((out_rows, D), F32),
                   jax.ShapeDtypeStruct((out_rows, D), F32),
                   jax.ShapeDtypeStruct((out_rows * ROWT, LANE), U32)],
        grid=(N // tm,),
        in_specs=[pl.BlockSpec((tm, RW), lambda i: (i, 0)),
                  pl.BlockSpec((tm, RW), lambda i: (i, 0)),
                  pl.BlockSpec((tm, 2 * RW), lambda i: (i, 0)),
                  pl.BlockSpec((tm, RW), lambda i: (i, 0)),
                  pl.BlockSpec((tm, RW), lambda i: (i, 0)),
                  pl.BlockSpec((tm, RW), lambda i: (i, 0)),
                  pl.BlockSpec((2, tm, S5_W), lambda i: (0, i, 0)),
                  pl.BlockSpec((tm, S5_W), lambda i: (i, 0)),
                  pl.BlockSpec((tm, D), lambda i: (i + t0, 0)),
                  pl.BlockSpec((1, 6, D), lambda i: (row_of_tile(i), 0, 0))] + [full(a) for a in wts]
                 + [pl.BlockSpec(memory_space=pl.ANY)] * len(extra),
        out_specs=[pl.BlockSpec((tm, D), lambda i: (i + o0, 0)),
                   pl.BlockSpec((tm, D), lambda i: (i + o0, 0)),
                   pl.BlockSpec((tm * ROWT, LANE), lambda i: (i + o0, 0))],
        input_output_aliases={len(ins) + k: k for k in range(len(extra))},
        compiler_params=_cp(("arbitrary",)),
        name="mix_out",
    )(*ins, *extra)


def _first_max(x, idx, big):
    m = jnp.max(x, axis=0, keepdims=True)
    first = jnp.min(jnp.where(x == m, idx, big), axis=0, keepdims=True)
    return m, first


def _router_kernel(h_ref, rwt_ref, bias_ref, e_ref, w_ref):
    tm = h_ref.shape[0]
    gsz = N_EXPERTS // N_EGROUPS
    ninf = -jnp.inf
    s = _sigmoid(_dot33(rwt_ref[...], h_ref[...], NT_DIMS))
    ssel = s + bias_ref[:, 0:1]
    gi = lax.broadcasted_iota(I32, (gsz, tm), 0)
    gscore = []
    for g in range(N_EGROUPS):
        xg = ssel[g * gsz:(g + 1) * gsz, :]
        m1, i1 = _first_max(xg, gi, gsz)
        m2 = jnp.max(jnp.where(gi == i1, ninf, xg), axis=0, keepdims=True)
        gscore.append(m1 + m2)
    cur = jnp.concatenate(gscore, axis=0)
    gidx = lax.broadcasted_iota(I32, (N_EGROUPS, tm), 0)
    picked = jnp.zeros((N_EGROUPS, tm), F32)
    for _ in range(TOPK_GROUPS):
        _, first = _first_max(cur, gidx, N_EGROUPS)
        hit = gidx == first
        picked = jnp.where(hit, 1.0, picked)
        cur = jnp.where(hit, ninf, cur)
    x = jnp.concatenate(
        [jnp.where(picked[g:g + 1, :] > 0.5, ssel[g * gsz:(g + 1) * gsz, :], ninf) for g in range(N_EGROUPS)], axis=0)
    ei = lax.broadcasted_iota(I32, (N_EXPERTS, tm), 0)
    idxs, ws = [], []
    for _ in range(TOP_K):
        _, first = _first_max(x, ei, N_EXPERTS)
        hit = ei == first
        idxs.append(first)
        ws.append(jnp.sum(jnp.where(hit, s, 0.0), axis=0, keepdims=True))
        x = jnp.where(hit, ninf, x)
    w = jnp.concatenate(ws, axis=0)
    e_ref[...] = jnp.concatenate(idxs, axis=0)
    w_ref[...] = w / jnp.sum(w, axis=0, keepdims=True) * ROUTE_SCALE


def moe_router(h2, rwt, bias_b, tm):
    N, D = h2.shape
    return pl.pallas_call(
        _router_kernel,
        out_shape=[jax.ShapeDtypeStruct((TOP_K, N), I32), jax.ShapeDtypeStruct((TOP_K, N), F32)],
        grid=(N // tm,),
        in_specs=[pl.BlockSpec((tm, D), lambda i: (i, 0)),
                  pl.BlockSpec((N_EXPERTS, D), lambda i: (0, 0)),
                  pl.BlockSpec((N_EXPERTS, LANE), lambda i: (0, 0))],
        out_specs=[pl.BlockSpec((TOP_K, tm), lambda i: (0, i)),
                   pl.BlockSpec((TOP_K, tm), lambda i: (0, i))],
        compiler_params=_cp(("arbitrary",)),
        name="moe_router",
    )(h2, rwt, bias_b)


def _moe_count_kernel(e_ref, pstart_ref, plan_ref, blk_ref, cnt_scr, *, nt, nbp):
    i = pl.program_id(0)
    tp = e_ref.shape[1]

    @pl.when(i == 0)
    def _():
        cnt_scr[...] = jnp.zeros_like(cnt_scr)

    ei = lax.broadcasted_iota(I32, (N_EXPERTS, tp), 0)
    acc = jnp.zeros((N_EXPERTS, 1), F32)
    for k in range(TOP_K):
        acc = acc + jnp.sum(jnp.where(ei == e_ref[k:k + 1, :], 1.0, 0.0), axis=1, keepdims=True)
    cnt_scr[...] = cnt_scr[...] + acc

    @pl.when(i == nt - 1)
    def _():
        cnt = cnt_scr[...].astype(I32)
        padded = (cnt + (MOE_BLK - 1)) & (-MOE_BLK)
        r = lax.broadcasted_iota(I32, (N_EXPERTS, N_EXPERTS), 0)
        c = lax.broadcasted_iota(I32, (N_EXPERTS, N_EXPERTS), 1)
        tri = jnp.where(c <= r, 1.0, 0.0).astype(BF16)
        padded_b = jnp.broadcast_to(padded.astype(F32), (N_EXPERTS, LANE))
        p_end = _dot_exact_lhs(tri, padded_b)
        pstart = p_end - padded_b
        pstart_ref[...] = pstart.astype(I32)
        diag = r == c
        ps_row = jnp.sum(jnp.where(diag, pstart[:, 0:1], 0.0), axis=0, keepdims=True)
        cnt_row = jnp.sum(jnp.where(diag, cnt_scr[...], 0.0), axis=0, keepdims=True)
        plan_ref[...] = jnp.concatenate([ps_row, cnt_row, jnp.zeros((SUB - 2, N_EXPERTS), F32)], axis=0).astype(I32)
        lim =(lax.broadcasted_iota(I32, (N_EXPERTS, nbp), 1) * MOE_BLK).astype(F32)
        be = jnp.sum(jnp.where(p_end[:, 0:1] <= lim, 1.0, 0.0), axis=0, keepdims=True)
        be = jnp.minimum(be, N_EXPERTS - 1.0)
        nused = jnp.max(p_end[:, 0:1], axis=0, keepdims=True) * (1.0 / MOE_BLK)
        blk_ref[...] = jnp.concatenate([jnp.broadcast_to(be, (SUB // 2, nbp)),
                                        jnp.broadcast_to(nused, (SUB // 2, nbp))], axis=0).astype(I32)


def moe_counts(eidx, tp, nbp):
    N = eidx.shape[1]
    nt = N // tp
    kern = functools.partial(_moe_count_kernel, nt=nt, nbp=nbp)
    return pl.pallas_call(
        kern,
        out_shape=[jax.ShapeDtypeStruct((N_EXPERTS, LANE), I32),
                   jax.ShapeDtypeStruct((SUB, N_EXPERTS), I32),
                   jax.ShapeDtypeStruct((SUB, nbp), I32)],
        grid=(nt,),
        in_specs=[pl.BlockSpec((TOP_K, tp), lambda i: (0, i))],
        out_specs=[pl.BlockSpec((N_EXPERTS, LANE), lambda i: (0, 0)),
                   pl.BlockSpec((SUB, N_EXPERTS), lambda i: (0, 0)),
                   pl.BlockSpec((SUB, nbp), lambda i: (0, 0))],
        scratch_shapes=[pltpu.VMEM((N_EXPERTS, 1), F32)],
        compiler_params=_cp(("arbitrary",)),
        name="moe_counts",
    )(eidx)


def _moe_dest_kernel(e_ref, pstart_ref, dest_ref, base_scr):
    i = pl.program_id(0)
    tp = e_ref.shape[1]

    @pl.when(i == 0)
    def _():
        base_scr[...] = pstart_ref[:, 0:1].astype(F32)

    ei = lax.broadcasted_iota(I32, (N_EXPERTS, tp), 0)
    r = lax.broadcasted_iota(I32, (tp, tp), 0)
    c = lax.broadcasted_iota(I32, (tp, tp), 1)
    tri = jnp.where(r <= c, 1.0, 0.0).astype(BF16)
    base = base_scr[...]
    rows = []
    for k in range(TOP_K):
        hit = ei == e_ref[k:k + 1, :]
        oh = jnp.where(hit, 1.0, 0.0)
        cum = _bdot(oh, tri)
        rows.append(jnp.sum(jnp.where(hit, cum - 1.0 + base, 0.0), axis=0, keepdims=True))
        base = base + cum[:, tp - 1:tp]
    base_scr[...] = base
    dest_ref[...] = jnp.concatenate(rows, axis=0).astype(I32)


def moe_dest(eidx, pstart, tp):
    N = eidx.shape[1]
    return pl.pallas_call(
        _moe_dest_kernel,
        out_shape=jax.ShapeDtypeStruct((TOP_K, N), I32),
        grid=(N // tp,),
        in_specs=[pl.BlockSpec((TOP_K, tp), lambda i: (0, i)),
                  pl.BlockSpec((N_EXPERTS, LANE), lambda i: (0, 0))],
        out_specs=pl.BlockSpec((TOP_K, tp), lambda i: (0, i)),
        scratch_shapes=[pltpu.VMEM((N_EXPERTS, 1), F32)],
        compiler_params=_cp(("arbitrary",)),
        name="moe_dest",
    )(eidx, pstart)


def _tile_at(ref, token):
    return ref.at[pl.ds(pl.multiple_of(token * ROWT, ROWT), ROWT), :]


SCAT_RING = 3


def _moe_scatter_kernel(dest_ref, h_ref, xs_hbm, ring, sem, *, nt):
    s = pl.program_id(0)
    tp = dest_ref.shape[1]
    slot = lax.rem(s, SCAT_RING)
    total = xs_hbm.at[pl.ds(0, tp * TOP_K * ROWT), :]

    def slot_wait(sl):
        pltpu.make_async_copy(total, total, sem.at[sl]).wait()

    @pl.when(s >= SCAT_RING)
    def _():
        slot_wait(slot)

    ring[slot] = h_ref[...]

    def issue(j, carry):
        for k in range(TOP_K):
            pltpu.make_async_copy(_tile_at(ring.at[slot], j), _tile_at(xs_hbm, dest_ref[k, j]),
                                  sem.at[slot]).start(priority=k % 2)
        return carry

    lax.fori_loop(0, tp, issue, 0)

    @pl.when(s == nt - 1)
    def _():
        for back in range(SCAT_RING - 1, -1, -1):
            if nt - 1 - back >= 0:
                slot_wait((nt - 1 - back) % SCAT_RING)


def moe_scatter(dest, h2t, n_slots, tp):
    N = dest.shape[1]
    nt = N // tp
    return pl.pallas_call(
        functools.partial(_moe_scatter_kernel, nt=nt),
        out_shape=jax.ShapeDtypeStruct((n_slots * ROWT, LANE), U32),
        grid=(nt,),
        in_specs=[pl.BlockSpec((TOP_K, tp), lambda i: (0, i), memory_space=pltpu.SMEM),
                  pl.BlockSpec((tp * ROWT, LANE), lambda i: (i, 0))],
        out_specs=pl.BlockSpec(memory_space=pl.ANY),
        scratch_shapes=[pltpu.VMEM((SCAT_RING, tp * ROWT, LANE), U32), pltpu.SemaphoreType.DMA((SCAT_RING,))],
        compiler_params=_cp(("arbitrary",)),
        name="moe_scatter",
    )(dest, h2t)


PAD_BITS = tuple(1 << b for b in reversed(range(MOE_BLK.bit_length() - 1)))


def _moe_padfill_kernel(ps_ref, xs_in, xs_hbm, zero_scr, sem):
    del xs_in
    zero_scr[...] = jnp.zeros_like(zero_scr)

    def pad_copies(e, wait):
        cnt = ps_ref[1, e]
        npad = ((cnt + (MOE_BLK - 1)) & (-MOE_BLK)) - cnt
        off = ps_ref[0, e] + cnt
        for bit in PAD_BITS:
            @pl.when((npad & bit) != 0)
            def _():
                cp = pltpu.make_async_copy(zero_scr.at[pl.ds(0, bit * ROWT), :],
                                           xs_hbm.at[pl.ds(pl.multiple_of(off * ROWT, ROWT), bit * ROWT), :], sem)
                if wait:
                    cp.wait()
                else:
                    cp.start()
            off = off + (npad & bit)

    def issue(e, carry):
        pad_copies(e, False)
        return carry

    def drain(e, carry):
        pad_copies(e, True)
        return carry

    lax.fori_loop(0, N_EXPERTS, issue, 0)
    lax.fori_loop(0, N_EXPERTS, drain, 0)


def moe_padfill(pstart, xs):
    return pl.pallas_call(
        _moe_padfill_kernel,
        out_shape=jax.ShapeDtypeStruct(xs.shape, xs.dtype),
        grid=(1,),
        in_specs=[pl.BlockSpec(memory_space=pltpu.SMEM),
                  pl.BlockSpec(memory_space=pl.ANY)],
        out_specs=pl.BlockSpec(memory_space=pl.ANY),
        scratch_shapes=[pltpu.VMEM((PAD_BITS[0] * ROWT, LANE), U32), pltpu.SemaphoreType.DMA(())],
        input_output_aliases={1: 0},
        compiler_params=_cp(("arbitrary",)),
        name="moe_padfill",
    )(pstart, xs)


EXP_NX = 8
EXP_PF = 4
EXP_NY = 4


def _experts_kernel(plan_ref, w13_ref, w2_ref, xs_hbm, y_hbm, w13_bf, w2_bf, x_buf, y_buf, sem_in, sem_out):
    e = pl.program_id(0)
    ne = pl.num_programs(0)
    ff = w2_bf.shape[0]
    rows = MOE_BLK * ROWT
    shift = MOE_BLK.bit_length() - 1
    blocks_of = lambda c: lax.shift_right_logical(c + (MOE_BLK - 1), shift)
    nblk = blocks_of(plan_ref[1, e])
    first = lax.shift_right_logical(plan_ref[0, e], shift)
    nused = lax.shift_right_logical(plan_ref[0, ne - 1], shift) + blocks_of(plan_ref[1, ne - 1])

    def window(ref, g):
        return ref.at[pl.ds(pl.multiple_of(g * rows, rows), rows), :]

    def x_copy(g):
        return pltpu.make_async_copy(window(xs_hbm, g), x_buf.at[g & (EXP_NX - 1)], sem_in.at[g & (EXP_NX - 1)])

    def y_copy(g):
        return pltpu.make_async_copy(y_buf.at[g & (EXP_NY - 1)], window(y_hbm, g), sem_out.at[g & (EXP_NY - 1)])

    @pl.when(e == 0)
    def _():
        for g in range(EXP_PF):
            @pl.when(g < nused)
            def _():
                x_copy(g).start()

    def process(gs):
        for g in gs:
            x_copy(g).wait()

            @pl.when(g + EXP_PF < nused)
            def _():
                x_copy(g + EXP_PF).start()

            @pl.when(g >= EXP_NY)
            def _():
                y_copy(g - EXP_NY).wait()

        x = jnp.concatenate([_tile_rows_load(x_buf.at[g & (EXP_NX - 1)], MOE_BLK).astype(BF16) for g in gs], axis=0)
        h = jnp.dot(x, w13_bf[...], preferred_element_type=F32)
        act = (_silu(h[:, 0:ff]) * h[:, ff:2 * ff]).astype(BF16)
        y = jnp.dot(act, w2_bf[...], preferred_element_type=F32)
        for i, g in enumerate(gs):
            _tile_rows_store(y_buf.at[g & (EXP_NY - 1)], y[i * MOE_BLK:(i + 1) * MOE_BLK])
            y_copy(g).start()

    @pl.when(nblk > 0)
    def _():
        w13_bf[...] = w13_ref[0, 0].astype(BF16)
        w2_bf[...] = w2_ref[0, 0].astype(BF16)
        npair = lax.shift_right_logical(nblk, 1)

        def body(j, carry):
            g = first + 2 * j
            process([g, g + 1])
            return carry

        lax.fori_loop(0, npair, body, 0)

        @pl.when((nblk & 1) == 1)
        def _():
            process([first + nblk - 1])

    @pl.when(e == ne - 1)
    def _():
        for back in range(EXP_NY, 0, -1):
            @pl.when(nused >= back)
            def _():
                y_copy(nused - back).wait()


def moe_experts(plan, xs, w13, w2, layer):
    _, E, D, F2 = w13.shape
    rows = MOE_BLK * ROWT
    grid_spec = pltpu.PrefetchScalarGridSpec(
        num_scalar_prefetch=1,
        grid=(E,),
        in_specs=[pl.BlockSpec((1, 1, D, F2), lambda e, p: (layer, e, 0, 0)),
                  pl.BlockSpec((1, 1, F2 // 2, D), lambda e, p: (layer, e, 0, 0)),
                  pl.BlockSpec(memory_space=pl.ANY)],
        out_specs=pl.BlockSpec(memory_space=pl.ANY),
        scratch_shapes=[pltpu.VMEM((D, F2), BF16), pltpu.VMEM((F2 // 2, D), BF16),
                        pltpu.VMEM((EXP_NX, rows, LANE), U32), pltpu.VMEM((EXP_NY, rows, LANE), U32),
                        pltpu.SemaphoreType.DMA((EXP_NX,)), pltpu.SemaphoreType.DMA((EXP_NY,))])
    return pl.pallas_call(
        _experts_kernel,
        out_shape=jax.ShapeDtypeStruct(xs.shape, U32),
        grid_spec=grid_spec,
        compiler_params=_cp(("arbitrary",)),
        name="moe_experts",
    )(plan, w13, w2, xs)


def _moe_combine_kernel(dest_ref, wt_ref, h_ref, x_ref, mod_ref, s13_ref, s2_ref, l2g_ref, l2b_ref, y_hbm,
                        o_ref, g_scr, sem, *, alpha, nt):
    s = pl.program_id(0)
    tm = h_ref.shape[0]
    ff = s2_ref.shape[0]
    rows = tm * ROWT
    slot_g = s % 2
    slot_c = 1 - slot_g

    def slot_copy(slot):
        return pltpu.make_async_copy(y_hbm.at[pl.ds(0, TOP_K * rows), :], g_scr.at[slot], sem.at[slot])

    @pl.when(s == 0)
    def _():
        g_scr[1] = jnp.zeros((TOP_K * rows, LANE), U32)

    @pl.when(s > 0)
    def _():
        slot_copy(slot_c).wait()

    for j in range(tm):
        for k in range(TOP_K):
            pltpu.make_async_copy(_tile_at(y_hbm, dest_ref[k, j]),
                                  g_scr.at[slot_g, pl.ds(k * rows + j * ROWT, ROWT), :],
                                  sem.at[slot_g]).start(priority=k % 2)
    hs = _bdot(h_ref[...], s13_ref[...])
    f = _bdot(_silu(hs[:, 0:ff]) * hs[:, ff:2 * ff], s2_ref[...])
    for k in range(TOP_K):
        f = f + wt_ref[:, k:k + 1] * _tile_rows_load(g_scr.at[slot_c], tm, base=k * rows)
    o_ref[...] = (_layer_norm(alpha * x_ref[...] + mod_ref[0, 5:6, :] * f) * l2g_ref[...] + l2b_ref[...])

    @pl.when(s == nt)
    def _():
        slot_copy(slot_g).wait()


def moe_combine(dest, wt_t, h2, x1, mod3, wts, y, row_of_tile, tm, alpha):
    N, D = h2.shape
    nt = N // tm
    full = lambda a: pl.BlockSpec(a.shape, lambda s: (0,) * a.ndim)
    kern = functools.partial(_moe_combine_kernel, alpha=alpha, nt=nt)
    nxt = lambda s: jnp.minimum(s, nt - 1)
    cur = lambda s: jnp.maximum(s - 1, 0)
    return pl.pallas_call(
        kern,
        out_shape=jax.ShapeDtypeStruct((N, D), F32),
        grid=(nt + 1,),
        in_specs=[pl.BlockSpec((TOP_K, tm), lambda s: (0, nxt(s)), memory_space=pltpu.SMEM),
                  pl.BlockSpec((tm, TOP_K), lambda s: (cur(s), 0)),
                  pl.BlockSpec((tm, D), lambda s: (cur(s), 0)),
                  pl.BlockSpec((tm, D), lambda s: (cur(s), 0)),
                  pl.BlockSpec((1, 6, D), lambda s: (row_of_tile(cur(s)), 0, 0))]
                 + [full(a) for a in wts] + [pl.BlockSpec(memory_space=pl.ANY)],
        out_specs=pl.BlockSpec((tm, D), lambda s: (cur(s), 0)),
        scratch_shapes=[pltpu.VMEM((2, TOP_K * tm * ROWT, LANE), U32), pltpu.SemaphoreType.DMA((2,))],
        compiler_params=_cp(("arbitrary",)),
        name="moe_combine",
    )(dest, wt_t, h2, x1, mod3, *wts, y)


def moe_ffn(h2, h2t, x1, mod3, mp, row_of_tile, tm, alpha):
    N = h2.shape[0]
    nb_total = (N * TOP_K + MOE_BLK - 1) // MOE_BLK + N_EXPERTS
    nbp = ((nb_total + LANE - 1) // LANE) * LANE
    tp = _tile(N, 256)
    eidx, wts = moe_router(h2, mp['rwt'], mp['bias'], tp)
    pstart, plan, blk = moe_counts(eidx, tp, nbp)
    dest = moe_dest(eidx, pstart, tp)
    xs = moe_scatter(dest, h2t, nb_total * MOE_BLK, tp)
    xs = moe_padfill(plan, xs)
    y = moe_experts(plan, xs, mp['w13'], mp['w2'], mp['layer'])
    return moe_combine(dest, wts.T, h2, x1, mod3, mp['comb'], y, row_of_tile, tm, alpha)


def token_mixers(stream, B, T, W, mod3, row_of_tile, lp, states, tm):
    pa, pbg, gg, u = inproj(stream, mod3, lp['w_in'], row_of_tile, tm)
    pac = short_conv(pa.reshape(B, T, PA_W), lp['taps'], W).reshape(B * T, PA_W)
    ss, lw, kb, gb = rwkv_prep(pac, lp['prep'], tm)
    s_rwkv, s_gla, s_s5 = states
    y2, f_rwkv = rwkv_scan(ss, lw, kb, s_rwkv, B, T)
    o2, f_gla = gla_scan(pbg, lp['gla_aup'], lp['gla_ab'], s_gla, B, T)
    yd, f_s5 = s5_scan(u.reshape(B, T, S5_W), lp['s5_a'], lp['s5_bbd'], lp['s5_cbd'], s_s5)
    return (y2, gb, o2, gg, yd.reshape(2, B * T, S5_W), u), (f_rwkv, f_gla, f_s5)


def zero_states(B):
    return ((jnp.zeros((B, RW // LANE, LANE, LANE), F32),) * 2,
            (jnp.zeros((B, RW, GLA_KP), F32),) * 2,
            jnp.zeros((2, B, S5_LANES), F32))


def _inproj_columns():
    r_cols = 3 * RW + 2 * DECAY_RANK + 2 * ICLR_RANK + GATE_RANK
    kd = GLA_HEADS * GLA_K
    gq, gk, gv = r_cols, r_cols + kd, r_cols + 2 * kd
    gg = gv + RW
    gad = gg + RW
    pc = gad + 2 * GLA_RANK
    z = lambda n: [-1] * n
    cols = list(range(0, r_cols)) + z(PA_W - r_cols)
    cols += list(range(gq, gq + kd)) + z(256 - kd)
    cols += list(range(gk, gk + kd)) + z(256 - kd)
    cols += list(range(gv, gv + RW))
    cols += list(range(gad, gad + 2 * GLA_RANK)) + z(LANE - 2 * GLA_RANK)
    cols += list(range(gg, gg + RW))
    cols += list(range(pc, pc + S5_W))
    assert len(cols) == NP_W
    return np.asarray(cols, np.int32)


def _head_block_ones():
    h = np.arange(RW) // HEAD
    return jnp.asarray(h[:, None] == h[None, :], BF16)


def _layer_params(l, p):
    D = p['w_in'].shape[1]
    cols = _inproj_columns()
    w_in = jnp.concatenate([p['w_in'][l], jnp.zeros((D, 1), F32)], axis=1)
    w_in = jnp.take(w_in, jnp.asarray(np.where(cols < 0, w_in.shape[1] - 1, cols)), axis=1).astype(BF16)
    taps = p['rwkv_conv'][l].reshape(9, -1)
    taps = jnp.concatenate([taps, jnp.zeros((9, PA_W - taps.shape[1]), F32)], axis=1)
    wup = jnp.zeros((LANE, 2 * RW), F32)
    aup = jnp.zeros((LANE, 2 * RW), F32)
    for d in range(2):
        wup = wup.at[d * DECAY_RANK:(d + 1) * DECAY_RANK, d * RW:(d + 1) * RW].set(p['rwkv_w_up'][l, d])
        o = 2 * DECAY_RANK + d * ICLR_RANK
        aup = aup.at[o:o + ICLR_RANK, d * RW:(d + 1) * RW].set(p['rwkv_a_up'][l, d])
    gup = jnp.zeros((LANE, RW), F32).at[0:GATE_RANK].set(p['rwkv_g_up'][l])
    row = lambda a: a.reshape(1, -1)
    prep = (wup, aup, gup, row(p['rwkv_w0'][l]), row(p['rwkv_a0'][l]), row(p['rwkv_k_k'][l]),
            row(p['rwkv_k_a'][l]), row(p['rwkv_r_k'][l]), _head_block_ones())
    kd = GLA_HEADS * GLA_K
    gla_aup = jnp.zeros((2, LANE, GLA_KP), F32)
    for d in range(2):
        gla_aup = gla_aup.at[d, d * GLA_RANK:(d + 1) * GLA_RANK, 0:kd].set(p['gla_a_up'][l, d])
    gla_ab = jnp.zeros((2, 1, GLA_KP), F32).at[:, 0, 0:kd].set(p['gla_a_bias'][l])
    lam_re, lam_im = p['s5_lam_re'][l], p['s5_lam_im'][l]
    dt = jnp.exp(p['s5_log_dt'][l])[:, :, None]
    zr, zi = lam_re[:, None, :] * dt, lam_im[:, None, :] * dt
    mag = jnp.exp(zr)
    ab_r, ab_i = mag * jnp.cos(zi), mag * jnp.sin(zi)
    den = (lam_re * lam_re + lam_im * lam_im)[:, None, :]
    f_r = ((ab_r - 1) * lam_re[:, None, :] + ab_i * lam_im[:, None, :]) / den
    f_i = (ab_i * lam_re[:, None, :] - (ab_r - 1) * lam_im[:, None, :]) / den
    b_re, b_im = p['s5_b_re'][l], p['s5_b_im'][l]
    bb_r = f_r[..., None] * b_re - f_i[..., None] * b_im
    bb_i = f_r[..., None] * b_im + f_i[..., None] * b_re
    eye_g = jnp.eye(S5_GROUPS, dtype=F32)
    half = S5_LANES // 2

    def in_blockdiag(bb):
        return jnp.einsum('dgpc,gh->dgchp', bb, eye_g).reshape(2, S5_W, half)

    def out_blockdiag(cc):
        return jnp.einsum('dgcp,gh->dgphc', cc, eye_g).reshape(2, half, S5_W)

    s5_bbd = jnp.concatenate([in_blockdiag(bb_r), in_blockdiag(bb_i)], axis=2).astype(BF16)
    s5_cbd = jnp.concatenate([out_blockdiag(p['s5_c_re'][l]), -out_blockdiag(p['s5_c_im'][l])], axis=1).astype(BF16)
    s5_a = jnp.concatenate([ab_r.reshape(2, 1, half), ab_i.reshape(2, 1, half)], axis=2)
    mix = (_head_block_ones(), row(p['rwkv_ln_g'][l]), row(p['rwkv_ln_b'][l]),
           row(jnp.tile(p['gla_norm_g'][l], GLA_HEADS)), row(p['s5_d'][l]),
           p['s5_glu_w'][l].astype(BF16), row(p['s5_glu_b'][l]), p['w_out'][l].astype(BF16),
           row(p['ln1_g'][l]), row(p['ln1_b'][l]))
    moe = dict(rwt=p['router_w'][l].T,
               bias=jnp.broadcast_to(p['router_bias'][l][:, None], (N_EXPERTS, LANE)),
               w13=p['exp_w13'], w2=p['exp_w2'], layer=l,
               comb=(p['sh_w13'][l].astype(BF16), p['sh_w2'][l].astype(BF16), row(p['ln2_g'][l]), row(p['ln2_b'][l])))
    return dict(w_in=w_in, taps=taps, prep=prep, gla_aup=gla_aup, gla_ab=gla_ab,
                s5_a=s5_a, s5_bbd=s5_bbd, s5_cbd=s5_cbd, mix=mix, moe=moe)


_ARG_NAMES = ('x', 'c', 'ctx', 'c_ctx', 'w_mod', 'b_mod', 'w_in', 'rwkv_conv', 'rwkv_w0', 'rwkv_w_up', 'rwkv_a0',
              'rwkv_a_up', 'rwkv_g_up', 'rwkv_k_k', 'rwkv_k_a', 'rwkv_r_k', 'rwkv_ln_g', 'rwkv_ln_b', 'gla_a_up',
              'gla_a_bias', 'gla_norm_g', 's5_lam_re', 's5_lam_im', 's5_log_dt', 's5_b_re', 's5_b_im', 's5_c_re',
              's5_c_im', 's5_d', 's5_glu_w', 's5_glu_b', 'w_out', 'ln1_g', 'ln1_b', 'router_w', 'router_bias',
              'exp_w13', 'exp_w2', 'sh_w13', 'sh_w2', 'ln2_g', 'ln2_b')


def _tile(n, pref):
    t = pref
    while n % t:
        t //= 2
    return t


def kernel(x, c, ctx, c_ctx, w_mod, b_mod, w_in, rwkv_conv, rwkv_w0, rwkv_w_up, rwkv_a0, rwkv_a_up, rwkv_g_up,
           rwkv_k_k, rwkv_k_a, rwkv_r_k, rwkv_ln_g, rwkv_ln_b, gla_a_up, gla_a_bias, gla_norm_g, s5_lam_re,
           s5_lam_im, s5_log_dt, s5_b_re, s5_b_im, s5_c_re, s5_c_im, s5_d, s5_glu_w, s5_glu_b, w_out, ln1_g,
           ln1_b, router_w, router_bias, exp_w13, exp_w2, sh_w13, sh_w2, ln2_g, ln2_b):
    p = dict(zip(_ARG_NAMES, (x, c, ctx, c_ctx, w_mod, b_mod, w_in, rwkv_conv, rwkv_w0, rwkv_w_up, rwkv_a0,
                              rwkv_a_up, rwkv_g_up, rwkv_k_k, rwkv_k_a, rwkv_r_k, rwkv_ln_g, rwkv_ln_b, gla_a_up,
                              gla_a_bias, gla_norm_g, s5_lam_re, s5_lam_im, s5_log_dt, s5_b_re, s5_b_im, s5_c_re,
                              s5_c_im, s5_d, s5_glu_w, s5_glu_b, w_out, ln1_g, ln1_b, router_w, router_bias,
                              exp_w13, exp_w2, sh_w13, sh_w2, ln2_g, ln2_b)))
    B, T, D = x.shape
    TC = ctx.shape[1]
    L = w_mod.shape[0]
    alpha = (2 * L) ** 0.25
    n_lat, n_ctx = B * T, B * TC
    R = ((B + 1 + SUB - 1) // SUB) * SUB
    cc = jnp.zeros((R, D), F32).at[0:B].set(c).at[B].set(c_ctx)
    mod = mod_table(cc, w_mod, b_mod)
    tm = _tile(T, 512)
    tmc = _tile(n_ctx, 512)
    tmm = min(_tile(T, 128), _tile(n_ctx, 128))
    lat_row = lambda i: (i * tm) // T
    ctx_row = lambda i: B
    lat = (x.reshape(n_lat, D), 0, n_lat)
    con = (ctx.reshape(n_ctx, D), 0, n_ctx)
    for l in range(L):
        last = l == L - 1
        lp = _layer_params(l, p)
        mod3 = mod[l].reshape(R, 6, D)
        outs_c, st_c = token_mixers(con, B, TC, TC, mod3, ctx_row, lp, zero_states(B), tmc)
        outs, _ = token_mixers(lat, B, T, GRID_W, mod3, lat_row, lp, st_c, tm)
        if last:
            x1, h2, h2t = mix_out(*outs, lat, mod3, lp['mix'], lat_row, tm, alpha)
            out = moe_ffn(h2, h2t, x1, mod3, lp['moe'], lambda i: (i * tmm) // T, tmm, alpha)
            lat = (out, 0, n_lat)
        else:
            n_all = n_ctx + n_lat
            bufs = mix_out(*outs_c, con, mod3, lp['mix'], ctx_row, tmc, alpha, out_rows=n_all)
            x1, h2, h2t = mix_out(*outs, lat, mod3, lp['mix'], lat_row, tm, alpha,
                                  out_rows=n_all, out_row0=n_ctx, into=bufs)
            row_all = lambda i: jnp.where(i * tmm < n_ctx, B, (i * tmm - n_ctx) // T)
            out = moe_ffn(h2, h2t, x1, mod3, lp['moe'], row_all, tmm, alpha)
            con, lat = (out, 0, n_ctx), (out, n_ctx, n_lat)
    out, row0, _ = lat
    return out[row0:row0 + n_lat].reshape(B, T, D)
```

```python
import functools
import math

import numpy as np
import jax
import jax.numpy as jnp
from jax import lax
from jax.experimental import pallas as pl
from jax.experimental.pallas import tpu as pltpu

F32 = jnp.float32
BF16 = jnp.bfloat16
I32 = jnp.int32
ACT = BF16

GRID_W = 64
RWKV_HEADS = 6
HEAD = 64
RW = RWKV_HEADS * HEAD
DECAY_RANK = 32
ICLR_RANK = 32
GATE_RANK = 64
RWKV_GN_EPS = 64e-5
GLA_HEADS = 6
GLA_K = 32
GLA_RANK = 16
GLA_TEMP = 16.0
S5_GROUPS = 16
S5_GROUP = 16
S5_STATE = 64
S5_W = S5_GROUPS * S5_GROUP
S5_LANES = 2 * S5_GROUPS * S5_STATE
N_EXPERTS = 256
TOP_K = 8
N_EGROUPS = 8
TOPK_GROUPS = 4
ROUTE_SCALE = 2.5
LN_EPS = 1e-6
CHUNK = 64
MOE_BLK = 256
LANE = 128
SUB = 8
VMEM_LIMIT = 56 * 1024 * 1024

PA_W = 3 * RW + 2 * LANE
PBG_W = 256 + 256 + RW + LANE
NP_W = PA_W + PBG_W + RW + S5_W


def _cp(sem):
    return pltpu.CompilerParams(dimension_semantics=sem, vmem_limit_bytes=VMEM_LIMIT)


def _sigmoid(x):
    return 1.0 / (1.0 + jnp.exp(-x))


def _silu(x):
    return x * _sigmoid(x)


def _bdot(a, b, dims=None):
    a = a.astype(BF16)
    b = b.astype(BF16)
    if dims is None:
        return jnp.dot(a, b, preferred_element_type=F32)
    return lax.dot_general(a, b, dims, preferred_element_type=F32)


def _split2(x):
    hi = x.astype(BF16)
    lo = (x - hi.astype(F32)).astype(BF16)
    return hi, lo


def _split3(x):
    hi = x.astype(BF16)
    r = x - hi.astype(F32)
    mid = r.astype(BF16)
    lo = (r - mid.astype(F32)).astype(BF16)
    return hi, mid, lo


NT_DIMS = (((1,), (1,)), ((), ()))
TN_DIMS = (((0,), (0,)), ((), ()))


def _dot33(a, b, dims=None):
    ah, al = _split2(a)
    bh, bl = _split2(b)
    return _bdot(ah, bh, dims) + (_bdot(ah, bl, dims) + _bdot(al, bh, dims))


def _dot_exact_lhs(m_exact, x, dims=None):
    h, m, l = _split3(x)
    return _bdot(m_exact, h, dims) + (_bdot(m_exact, m, dims) + _bdot(m_exact, l, dims))


def _dot_exact_rhs(x, m_exact, dims=None):
    h, m, l = _split3(x)
    return _bdot(h, m_exact, dims) + (_bdot(m, m_exact, dims) + _bdot(l, m_exact, dims))


def _dot_split_rhs(x, m_exact, dims=None):
    h, l = _split2(x)
    return _bdot(h, m_exact, dims) + _bdot(l, m_exact, dims)


def _layer_norm(x):
    mu = jnp.mean(x, axis=-1, keepdims=True)
    xc = x - mu
    var = jnp.mean(xc * xc, axis=-1, keepdims=True)
    return xc * lax.rsqrt(var + LN_EPS)


def _mod_kernel(c_ref, w_ref, b_ref, o_ref):
    s = _silu(c_ref[...])
    o_ref[0] = _dot33(s, w_ref[0]) + b_ref[0]


def mod_table(cc, w_mod, b_mod):
    L, D, D6 = w_mod.shape
    R = cc.shape[0]
    tn = 1536
    return pl.pallas_call(
        _mod_kernel,
        out_shape=jax.ShapeDtypeStruct((L, R, D6), F32),
        grid=(L, D6 // tn),
        in_specs=[pl.BlockSpec((R, D), lambda l, j: (0, 0)),
                  pl.BlockSpec((1, D, tn), lambda l, j: (l, 0, j)),
                  pl.BlockSpec((1, 1, tn), lambda l, j: (l, 0, j))],
        out_specs=pl.BlockSpec((1, R, tn), lambda l, j: (l, 0, j)),
        compiler_params=_cp(("arbitrary", "arbitrary")),
        name="mod_table",
    )(cc, w_mod, b_mod.reshape(L, 1, D6))


def _inproj_kernel(x_ref, mod_ref, w_ref, pa_ref, pbg_ref, gg_ref, u_ref):
    x = x_ref[...]
    h = _layer_norm(x) * (1.0 + mod_ref[0, 1:2, :]) + mod_ref[0, 0:1, :]
    hb = h.astype(BF16)
    o = 0
    for ref in (pa_ref, pbg_ref, gg_ref, u_ref):
        w = ref.shape[-1]
        ref[...] = jnp.dot(hb, w_ref[:, o:o + w], preferred_element_type=F32).astype(ref.dtype)
        o += w


def inproj(stream, mod3, w_bf, row_of_tile, tm):
    x2, row0, N = stream
    D = x2.shape[1]
    t0 = row0 // tm
    assert row0 % tm == 0 and N % tm == 0
    return pl.pallas_call(
        _inproj_kernel,
        out_shape=[jax.ShapeDtypeStruct((N, PA_W), ACT),
                   jax.ShapeDtypeStruct((N, PBG_W), ACT),
                   jax.ShapeDtypeStruct((N, RW), ACT),
                   jax.ShapeDtypeStruct((N, S5_W), ACT)],
        grid=(N // tm,),
        in_specs=[pl.BlockSpec((tm, D), lambda i: (i + t0, 0)),
                  pl.BlockSpec((1, 6, D), lambda i: (row_of_tile(i), 0, 0)),
                  pl.BlockSpec((D, NP_W), lambda i: (0, 0))],
        out_specs=[pl.BlockSpec((tm, PA_W), lambda i: (i, 0)),
                   pl.BlockSpec((tm, PBG_W), lambda i: (i, 0)),
                   pl.BlockSpec((tm, RW), lambda i: (i, 0)),
                   pl.BlockSpec((tm, S5_W), lambda i: (i, 0))],
        compiler_params=_cp(("arbitrary",)),
        name="inproj",
    )(x2, mod3, w_bf)


CONV_PAD = 72


def _conv_kernel(x_ref, taps_ref, o_ref, buf_ref, *, T, W, vertical):
    pad = CONV_PAD
    zeros = jnp.zeros((pad, LANE), F32)
    buf_ref[0:pad, :] = zeros
    buf_ref[pad + T:pad + T + pad, :] = zeros
    buf_ref[pad:pad + T, :] = x_ref[0].astype(F32)
    ch = min(T, 256)
    col = lax.broadcasted_iota(I32, (ch, LANE), 0) & (W - 1)
    left_ok = col >= 1
    right_ok = col <= W - 2
    for c in range(T // ch):
        base = pad + c * ch
        acc = jnp.zeros((ch, LANE), F32)
        for dr in ((0, 1, 2) if vertical else (1,)):
            for dc in range(3):
                off = (dr - 1) * W + (dc - 1)
                v = buf_ref[base + off:base + off + ch, :]
                if dc == 0:
                    v = jnp.where(left_ok, v, 0.0)
                elif dc == 2:
                    v = jnp.where(right_ok, v, 0.0)
                acc = acc + v * taps_ref[3 * dr + dc:3 * dr + dc + 1, :]
        o_ref[0, c * ch:(c + 1) * ch, :] = acc.astype(o_ref.dtype)


def short_conv(pa3, taps9, W):
    B, T, C = pa3.shape
    vertical = T > W
    assert W & (W - 1) == 0 and (not vertical or W + 1 <= CONV_PAD)
    kern = functools.partial(_conv_kernel, T=T, W=W, vertical=vertical)
    return pl.pallas_call(
        kern,
        out_shape=jax.ShapeDtypeStruct((B, T, C), ACT),
        grid=(B, C // LANE),
        in_specs=[pl.BlockSpec((1, T, LANE), lambda b, j: (b, 0, j)),
                  pl.BlockSpec((9, LANE), lambda b, j: (0, j))],
        out_specs=pl.BlockSpec((1, T, LANE), lambda b, j: (b, 0, j)),
        scratch_shapes=[pltpu.VMEM((T + 2 * CONV_PAD, LANE), F32)],
        compiler_params=_cp(("arbitrary", "arbitrary")),
        name="short_conv",
    )(pa3, taps9)


def _rwkv_prep_kernel(pa_ref, wup_ref, aup_ref, gup_ref, w0_ref, a0_ref, kk_ref, ka_ref, rk_ref, hb_ref,
                      ss_ref, lw_ref, kb_ref, gb_ref):
    r = pa_ref[:, 0:RW].astype(F32)
    k = pa_ref[:, RW:2 * RW].astype(F32)
    v = pa_ref[:, 2 * RW:3 * RW].astype(F32)
    wa = pa_ref[:, 3 * RW:3 * RW + LANE].astype(F32)
    gd = pa_ref[:, 3 * RW + LANE:3 * RW + 2 * LANE].astype(F32)
    z = w0_ref[...] + _bdot(jnp.tanh(wa), wup_ref[...])
    lw = -_sigmoid(z) * math.exp(-0.5)
    a = _sigmoid(a0_ref[...] + _bdot(wa, aup_ref[...]))
    g = _bdot(_sigmoid(gd), gup_ref[...])
    hb = hb_ref[...]
    kk = k * kk_ref[...]
    kk = kk * lax.rsqrt(_bdot(kk * kk, hb) + 1e-12)
    ka = ka_ref[...]
    ss_ref[:, 0:RW] = r.astype(ACT)
    ss_ref[:, RW:2 * RW] = v.astype(ACT)
    ss_ref[:, 2 * RW:3 * RW] = kk.astype(ACT)
    rk2 = jnp.zeros_like(r)
    for d in range(2):
        ad = a[:, d * RW:(d + 1) * RW]
        k2 = k * (1.0 + (ad - 1.0) * ka)
        lw_ref[d] = lw[:, d * RW:(d + 1) * RW]
        kb_ref[d, :, 0:RW] = k2.astype(ACT)
        kb_ref[d, :, RW:2 * RW] = (kk * ad).astype(ACT)
        rk2 = rk2 + r * k2
    bonus = _bdot(rk2 * rk_ref[...], hb) * v
    gb_ref[:, 0:RW] = g.astype(ACT)
    gb_ref[:, RW:2 * RW] = bonus.astype(ACT)


def rwkv_prep(pa2, wts, tm):
    N = pa2.shape[0]
    full = lambda a: pl.BlockSpec(a.shape, lambda i: (0,) * a.ndim)
    return pl.pallas_call(
        _rwkv_prep_kernel,
        out_shape=[jax.ShapeDtypeStruct((N, 3 * RW), ACT),
                   jax.ShapeDtypeStruct((2, N, RW), F32),
                   jax.ShapeDtypeStruct((2, N, 2 * RW), ACT),
                   jax.ShapeDtypeStruct((N, 2 * RW), ACT)],
        grid=(N // tm,),
        in_specs=[pl.BlockSpec((tm, PA_W), lambda i: (i, 0))] + [full(a) for a in wts],
        out_specs=[pl.BlockSpec((tm, 3 * RW), lambda i: (i, 0)),
                   pl.BlockSpec((2, tm, RW), lambda i: (0, i, 0)),
                   pl.BlockSpec((2, tm, 2 * RW), lambda i: (0, i, 0)),
                   pl.BlockSpec((tm, 2 * RW), lambda i: (i, 0))],
        compiler_params=_cp(("arbitrary",)),
        name="rwkv_prep",
    )(pa2, *wts)


RWKV_GROUP = 32
RWKV_WAVE = 4


def _rwkv_scan_kernel(ss_ref, lw_ref, kb_ref, s0_ref, y_ref, sfin_ref, s_scr, *, nsteps, group, reverse):
    n = pl.program_id(1)
    C = CHUNK
    P = 2 * C
    npair = RW // LANE

    @pl.when(n == 0)
    def _():
        s_scr[...] = s0_ref[0]

    row = lax.broadcasted_iota(I32, (P, P), 0)
    col = lax.broadcasted_iota(I32, (P, P), 1)
    same = (row >> 6) == (col >> 6)
    dlt = (col & (C - 1)) - (row & (C - 1)) if reverse else (row & (C - 1)) - (col & (C - 1))
    strict = same & (dlt > 0)
    incl = same & (dlt >= 0)
    eye = (row == col).astype(F32)
    lvl_masks = [((row >> (lvl + 1)) == (col >> (lvl + 1))) & ((row >> lvl) != (col >> lvl)) for lvl in range(6)]
    rc = lax.broadcasted_iota(I32, (C, C), 0)
    cc = lax.broadcasted_iota(I32, (C, C), 1)
    tri = jnp.where((cc >= rc) if reverse else (rc >= cc), 1.0, 0.0).astype(BF16)
    head0 = lax.broadcasted_iota(I32, (C, LANE), 1) < HEAD

    def stack(x):
        return jnp.concatenate([jnp.where(head0, x, 0.0), jnp.where(head0, 0.0, x)], axis=0)

    pairs = range(npair)
    tm_ = {}

    def independent_stages(chunks):
        streams = [(g, p) for g in chunks for p in pairs]

        def st_cumsum():
            for (g, p) in streams:
                lw = lw_ref[0, g * C:(g + 1) * C, p * LANE:(p + 1) * LANE]
                tm_[(g, p)] = dict(lw=lw, cl=_dot_exact_lhs(tri, lw))

        def st_scores():
            for (g, p) in streams:
                t = tm_[(g, p)]
                t0, t1 = g * C, (g + 1) * C
                lo, hi = p * LANE, (p + 1) * LANE
                r = ss_ref[t0:t1, lo:hi].astype(F32)
                v = ss_ref[t0:t1, RW + lo:RW + hi].astype(F32)
                kk = ss_ref[t0:t1, 2 * RW + lo:2 * RW + hi].astype(F32)
                k2 = kb_ref[0, t0:t1, lo:hi].astype(F32)
                b = kb_ref[0, t0:t1, RW + lo:RW + hi].astype(F32)
                cl, lw = t['cl'], t['lw']
                t['ptot'] = jnp.exp(jnp.sum(lw, axis=0, keepdims=True))
                pinv = jnp.exp(-cl)
                left = jnp.concatenate([stack(-kk * jnp.exp(cl - lw)), stack(r * jnp.exp(cl))], axis=0)
                right = jnp.concatenate([stack(b * pinv), stack(k2 * pinv)], axis=0)
                t['v_st'] = stack(v)
                t['left'] = left.astype(BF16)
                t['bk'] = (right * t['ptot']).astype(BF16)
                aa = _bdot(left, right, NT_DIMS)
                t['nmat'] = jnp.where(strict, aa[0:P, 0:P], 0.0)
                t['a_ak'] = jnp.where(strict, aa[0:P, P:2 * P], 0.0)
                t['a_rbk'] = jnp.concatenate([jnp.where(incl, aa[P:2 * P, 0:P], 0.0),
                                              jnp.where(incl, aa[P:2 * P, P:2 * P], 0.0)], axis=1).astype(BF16)
                t['tinv'] = eye + jnp.where(lvl_masks[0], t['nmat'], 0.0)

        def st_akv():
            for sk in streams:
                t = tm_[sk]
                t['akv'] = _bdot(t['a_ak'], t['v_st'])

        def st_et(m):
            def run():
                for sk in streams:
                    t = tm_[sk]
                    t['et'] = _bdot(jnp.where(m, t['nmat'], 0.0), t['tinv'])
            return run

        def st_tinv():
            for sk in streams:
                t = tm_[sk]
                t['tinv'] = t['tinv'] + _bdot(t['tinv'], t['et'])

        stages = [st_cumsum, st_scores, st_akv]
        for m in lvl_masks[1:]:
            stages += [st_et(m), st_tinv]
        return stages

    state = [s_scr[p] for p in pairs]
    carry = {}

    def dependent_stages(chunks):
        stages = []
        for g in chunks:
            def st_as(g=g):
                carry['a_s'] = [_bdot(tm_[(g, p)]['left'], state[p], NT_DIMS) for p in pairs]

            def st_u(g=g):
                u = [_bdot(tm_[(g, p)]['tinv'], carry['a_s'][p][0:P] + tm_[(g, p)]['akv']) for p in pairs]
                carry['uv'] = [jnp.concatenate([u[p], tm_[(g, p)]['v_st']], axis=0) for p in pairs]

            def st_state(g=g):
                for p in pairs:
                    t = tm_[(g, p)]
                    state[p] = state[p] * t['ptot'] + _bdot(carry['uv'][p], t['bk'], TN_DIMS)

            def st_y(g=g, a_s=None):
                for p in pairs:
                    y_st = carry['a_s'][p][P:2 * P] + _bdot(tm_[(g, p)]['a_rbk'], carry['uv'][p])
                    y_ref[g * C:(g + 1) * C, p * LANE:(p + 1) * LANE] = (y_st[0:C] + y_st[C:P]).astype(y_ref.dtype)

            stages += [st_as, st_u, st_y, st_state]
        return stages

    order = list(range(group - 1, -1, -1) if reverse else range(group))
    wave = RWKV_WAVE if group >= 2 * RWKV_WAVE else group
    pending = []
    for w in range(0, group, wave):
        ind = independent_stages(order[w:w + wave])
        for i in range(max(len(ind), len(pending))):
            if i < len(ind):
                ind[i]()
            if i < len(pending):
                pending[i]()
        pending = dependent_stages(order[w:w + wave])
    for stage in pending:
        stage()
    for p in pairs:
        s_scr[p] = state[p]

    @pl.when(n == nsteps - 1)
    def _():
        sfin_ref[0] = s_scr[...]


def rwkv_scan_dir(ss, lw, kb, s0, B, T, d):
    nc = T // CHUNK
    group = min(RWKV_GROUP, nc)
    nsteps = nc // group
    npair = RW // LANE
    rows = group * CHUNK
    blk = (lambda b, n: b * nsteps + nsteps - 1 - n) if d else (lambda b, n: b * nsteps + n)
    kern = functools.partial(_rwkv_scan_kernel, nsteps=nsteps, group=group, reverse=bool(d))
    return pl.pallas_call(
        kern,
        out_shape=[jax.ShapeDtypeStruct((B * T, RW), ACT),
                   jax.ShapeDtypeStruct((B, npair, LANE, LANE), F32)],
        grid=(B, nsteps),
        in_specs=[pl.BlockSpec((rows, 3 * RW), lambda b, n: (blk(b, n), 0)),
                  pl.BlockSpec((1, rows, RW), lambda b, n: (d, blk(b, n), 0)),
                  pl.BlockSpec((1, rows, 2 * RW), lambda b, n: (d, blk(b, n), 0)),
                  pl.BlockSpec((1, npair, LANE, LANE), lambda b, n: (b, 0, 0, 0))],
        out_specs=[pl.BlockSpec((rows, RW), lambda b, n: (blk(b, n), 0)),
                   pl.BlockSpec((1, npair, LANE, LANE), lambda b, n: (b, 0, 0, 0))],
        scratch_shapes=[pltpu.VMEM((npair, LANE, LANE), F32)],
        compiler_params=_cp(("arbitrary", "arbitrary")),
        name="rwkv_scan_bwd" if d else "rwkv_scan_fwd",
    )(ss, lw, kb, s0)


def rwkv_scan(ss, lw, kb, s0, B, T):
    y0, f0 = rwkv_scan_dir(ss, lw, kb, s0[0], B, T, 0)
    y1, f1 = rwkv_scan_dir(ss, lw, kb, s0[1], B, T, 1)
    return (y0, y1), (f0, f1)


GLA_KP = 256


GLA_GROUP = 16


def _gla_kernel(pbg_ref, aup_ref, ab_ref, s0_ref, o_ref, sfin_ref, s_scr, *, nsteps, group, reverse):
    n = pl.program_id(1)
    C = CHUNK
    R = group * C

    @pl.when(n == 0)
    def _():
        s_scr[...] = s0_ref[0]

    q = pbg_ref[:, 0:GLA_KP].astype(F32)
    k = pbg_ref[:, GLA_KP:2 * GLA_KP].astype(F32)
    v = pbg_ref[:, 2 * GLA_KP:2 * GLA_KP + RW].astype(F32)
    ad = pbg_ref[:, 2 * GLA_KP + RW:2 * GLA_KP + RW + LANE].astype(F32)
    x = _dot33(ad, aup_ref[0]) + ab_ref[0]
    la = (jnp.minimum(x, 0.0) - jnp.log(1.0 + jnp.exp(-jnp.abs(x)))) * (1.0 / GLA_TEMP)
    chunks = range(group)
    sl = [slice(g * C, (g + 1) * C) for g in chunks]
    rr = lax.broadcasted_iota(I32, (C, C), 0)
    rc = lax.broadcasted_iota(I32, (C, C), 1)
    tri = jnp.where((rc >= rr) if reverse else (rr >= rc), 1.0, 0.0).astype(BF16)
    bcums = [_dot_exact_lhs(tri, la[sl[g]]) for g in chunks]
    last = 0 if reverse else C - 1
    tots = [bc[last:last + 1, :] for bc in bcums]
    bcum = jnp.concatenate(bcums, axis=0)
    tot = jnp.concatenate([jnp.broadcast_to(t, (C, GLA_KP)) for t in tots], axis=0)
    q_in = q * jnp.exp(bcum) * (GLA_K ** -0.5)
    k_in = k * jnp.exp(-bcum)
    k_st = k * jnp.exp(tot - bcum)
    dn = [jnp.exp(t) for t in tots]
    klane = lax.broadcasted_iota(I32, (C, GLA_KP), 1)
    rt = lax.broadcasted_iota(I32, (GLA_HEADS * C, C), 0) & (C - 1)
    ct = lax.broadcasted_iota(I32, (GLA_HEADS * C, C), 1)
    causal = (ct >= rt) if reverse else (rt >= ct)
    vlane = lax.broadcasted_iota(I32, (C, RW), 1)
    sv = lax.broadcasted_iota(I32, (RW, GLA_KP), 0) >> 6
    sk = lax.broadcasted_iota(I32, (RW, GLA_KP), 1) >> 5
    q_rows = [jnp.concatenate([jnp.where((klane >> 5) == h, q_in[sl[g]], 0.0) for h in range(GLA_HEADS)],
                              axis=0).astype(BF16) for g in chunks]
    att = [jnp.where(causal, _bdot(q_rows[g], k_in[sl[g]], NT_DIMS), 0.0) for g in chunks]
    o_rows = [_bdot(att[g], v[sl[g]]) for g in chunks]
    kv = [jnp.where(sv == sk, _bdot(v[sl[g]], k_st[sl[g]], TN_DIMS), 0.0) for g in chunks]
    s = s_scr[...]
    for g in (reversed(chunks) if reverse else chunks):
        o = _bdot(q_in[sl[g]], s, NT_DIMS)
        for h in range(GLA_HEADS):
            o = o + jnp.where((vlane >> 6) == h, o_rows[g][h * C:(h + 1) * C], 0.0)
        o_ref[sl[g], :] = o.astype(o_ref.dtype)
        s = s * dn[g] + kv[g]
    s_scr[...] = s

    @pl.when(n == nsteps - 1)
    def _():
        sfin_ref[0] = s_scr[...]


def gla_scan_dir(pbg, aup, ab, s0, B, T, d):
    nc = T // CHUNK
    group = min(GLA_GROUP, nc)
    nsteps = nc // group
    rows = group * CHUNK
    blk = (lambda b, n: b * nsteps + nsteps - 1 - n) if d else (lambda b, n: b * nsteps + n)
    kern = functools.partial(_gla_kernel, nsteps=nsteps, group=group, reverse=bool(d))
    return pl.pallas_call(
        kern,
        out_shape=[jax.ShapeDtypeStruct((B * T, RW), ACT),
                   jax.ShapeDtypeStruct((B, RW, GLA_KP), F32)],
        grid=(B, nsteps),
        in_specs=[pl.BlockSpec((rows, PBG_W), lambda b, n: (blk(b, n), 0)),
                  pl.BlockSpec((1, LANE, GLA_KP), lambda b, n: (d, 0, 0)),
                  pl.BlockSpec((1, 1, GLA_KP), lambda b, n: (d, 0, 0)),
                  pl.BlockSpec((1, RW, GLA_KP), lambda b, n: (b, 0, 0))],
        out_specs=[pl.BlockSpec((rows, RW), lambda b, n: (blk(b, n), 0)),
                   pl.BlockSpec((1, RW, GLA_KP), lambda b, n: (b, 0, 0))],
        scratch_shapes=[pltpu.VMEM((RW, GLA_KP), F32)],
        compiler_params=_cp(("arbitrary", "arbitrary")),
        name="gla_scan_bwd" if d else "gla_scan_fwd",
    )(pbg, aup, ab, s0)


def gla_scan(pbg, aup, ab, s0, B, T):
    o0, f0 = gla_scan_dir(pbg, aup, ab, s0[0], B, T, 0)
    o1, f1 = gla_scan_dir(pbg, aup, ab, s0[1], B, T, 1)
    return (o0, o1), (f0, f1)


S5_CHUNK = 128
S5_PITCH = 132


def _s5_kernel(u_ref, a_ref, bbd_ref, cbd_ref, s0_ref, y_ref, sfin_ref, x_scr, st_scr, *, nc, nb):
    d = pl.program_id(0)
    n = pl.program_id(1)
    C = u_ref.shape[1]
    half = S5_LANES // 2

    @pl.when(n == 0)
    def _():
        st_scr[...] = s0_ref[0]

    nslab = S5_LANES // LANE
    bu = _bdot(jnp.concatenate([u_ref[b] for b in range(nb)], axis=0), bbd_ref[0])
    for b in range(nb):
        for j in range(nslab):
            x_scr[j, b * S5_PITCH:b * S5_PITCH + C, :] = bu[b * C:(b + 1) * C, j * LANE:(j + 1) * LANE]
    a_re = a_ref[0, :, 0:half]
    a_im = a_ref[0, :, half:S5_LANES]

    def step(i, carry):
        re, im = carry
        t = i + d * (C - 1 - 2 * i)
        rows = pl.ds(t, nb, stride=S5_PITCH)
        bu = jnp.concatenate([x_scr[j, rows, :] for j in range(nslab)], axis=1)
        nre = a_re * re - a_im * im + bu[:, 0:half]
        nim = a_re * im + a_im * re + bu[:, half:S5_LANES]
        for j in range(nslab // 2):
            x_scr[j, rows, :] = nre[:, j * LANE:(j + 1) * LANE]
            x_scr[nslab // 2 + j, rows, :] = nim[:, j * LANE:(j + 1) * LANE]
        return nre, nim

    st = st_scr[...]
    re, im = lax.fori_loop(0, C, step, (st[:, 0:half], st[:, half:S5_LANES]), unroll=4)
    st_scr[...] = jnp.concatenate([re, im], axis=1)
    xs = jnp.concatenate(
        [jnp.concatenate([x_scr[j, b * S5_PITCH:b * S5_PITCH + C, :] for j in range(nslab)], axis=1).astype(BF16)
         for b in range(nb)], axis=0)
    y = _bdot(xs, cbd_ref[0])
    for b in range(nb):
        y_ref[0, b] = y[b * C:(b + 1) * C, :].astype(y_ref.dtype)

    @pl.when(n == nc - 1)
    def _():
        sfin_ref[0] = st_scr[...]


def s5_scan(u3, a_bar, bbd, cbd, s0):
    B, T, _ = u3.shape
    C = min(S5_CHUNK, T)
    assert C + 4 <= S5_PITCH
    nc = T // C
    chunk = lambda d, n: n + d * (nc - 1 - 2 * n)
    kern = functools.partial(_s5_kernel, nc=nc, nb=B)
    return pl.pallas_call(
        kern,
        out_shape=[jax.ShapeDtypeStruct((2, B, T, S5_W), ACT),
                   jax.ShapeDtypeStruct((2, B, S5_LANES), F32)],
        grid=(2, nc),
        in_specs=[pl.BlockSpec((B, C, S5_W), lambda d, n: (0, chunk(d, n), 0)),
                  pl.BlockSpec((1, 1, S5_LANES), lambda d, n: (d, 0, 0)),
                  pl.BlockSpec((1, S5_W, S5_LANES), lambda d, n: (d, 0, 0)),
                  pl.BlockSpec((1, S5_LANES, S5_W), lambda d, n: (d, 0, 0)),
                  pl.BlockSpec((1, B, S5_LANES), lambda d, n: (d, 0, 0))],
        out_specs=[pl.BlockSpec((1, B, C, S5_W), lambda d, n: (d, 0, chunk(d, n), 0)),
                   pl.BlockSpec((1, B, S5_LANES), lambda d, n: (d, 0, 0))],
        scratch_shapes=[pltpu.VMEM((S5_LANES // LANE, B * S5_PITCH, LANE), F32),
                        pltpu.VMEM((B, S5_LANES), F32)],
        compiler_params=_cp(("arbitrary", "arbitrary")),
        name="s5_scan",
    )(u3, a_bar, bbd, cbd, s0)


ROWT = 4
U32 = jnp.uint32


def _pack_bf16_pair(a, b):
    au = lax.bitcast_convert_type(a.astype(BF16).astype(F32), U32)
    bu = lax.bitcast_convert_type(b.astype(BF16).astype(F32), U32)
    return (au >> 16) | bu


def _unpack_bf16_pair(w):
    return (lax.bitcast_convert_type(w << 16, F32), lax.bitcast_convert_type(w & jnp.uint32(0xFFFF0000), F32))


def _tile_rows_store(ref, val):
    tm = val.shape[0]
    assert val.shape[1] == 2 * ROWT * LANE
    for s in range(ROWT):
        ref[pl.ds(s, tm, stride=ROWT), :] = _pack_bf16_pair(val[:, 2 * s * LANE:(2 * s + 1) * LANE],
                                                            val[:, (2 * s + 1) * LANE:(2 * s + 2) * LANE])


def _tile_rows_load(ref, tm, base=0):
    parts = []
    for s in range(ROWT):
        parts.extend(_unpack_bf16_pair(ref[pl.ds(base + s, tm, stride=ROWT), :]))
    return jnp.concatenate(parts, axis=1)


def _mix_out_kernel(y0_ref, y1_ref, gb_ref, o0_ref, o1_ref, gg_ref, yd_ref, u_ref, x_ref, mod_ref,
                    hb_ref, lng_ref, lnb_ref, gng_ref, s5d_ref, gluw_ref, glub_ref, wout_ref, l1g_ref, l1b_ref,
                    *rest, alpha):
    x1_ref, h2_ref, h2t_ref = rest[-3:]
    hb = hb_ref[...]
    inv = 1.0 / HEAD
    f32 = lambda ref_val: ref_val.astype(F32)
    y = f32(y0_ref[...]) + f32(y1_ref[...])
    yc = y - _dot_split_rhs(y, hb) * inv
    var = _bdot(yc * yc, hb) * inv
    gn = yc * lax.rsqrt(var + RWKV_GN_EPS) * lng_ref[...] + lnb_ref[...]
    y_a = (gn + f32(gb_ref[:, RW:2 * RW])) * f32(gb_ref[:, 0:RW])
    o = f32(o0_ref[...]) + f32(o1_ref[...])
    o = o * lax.rsqrt(_bdot(o * o, hb) * inv + 1e-6) * gng_ref[...]
    y_b = o * _silu(f32(gg_ref[...]))
    c = s5d_ref[...] * f32(u_ref[...]) + f32(yd_ref[0]) + f32(yd_ref[1])
    c = 0.5 * c * (1.0 + jnp.tanh(math.sqrt(2.0 / math.pi) * (c + 0.044715 * (c * c * c))))
    y_c = c * _sigmoid(_bdot(c, gluw_ref[...]) + glub_ref[...])
    y_mix = (_bdot(y_a, wout_ref[0:RW, :]) + _bdot(y_b, wout_ref[RW:2 * RW, :])
             + _bdot(y_c, wout_ref[2 * RW:2 * RW + S5_W, :]))
    x1 = _layer_norm(alpha * x_ref[...] + mod_ref[0, 2:3, :] * y_mix) * l1g_ref[...] + l1b_ref[...]
    x1_ref[...] = x1
    h2 = _layer_norm(x1) * (1.0 + mod_ref[0, 4:5, :]) + mod_ref[0, 3:4, :]
    h2_ref[...] = h2
    _tile_rows_store(h2t_ref, h2)


def mix_out(y01, gb, o01, gg, yd, u, stream, mod3, wts, row_of_tile, tm, alpha, out_rows=None, out_row0=0, into=None):
    x2, row0, N = stream
    D = x2.shape[1]
    out_rows = N if out_rows is None else out_rows
    assert row0 % tm == 0 and out_row0 % tm == 0 and N % tm == 0
    t0, o0 = row0 // tm, out_row0 // tm
    full = lambda a: pl.BlockSpec(a.shape, lambda i: (0,) * a.ndim)
    kern = functools.partial(_mix_out_kernel, alpha=alpha)
    ins = [y01[0], y01[1], gb, o01[0], o01[1], gg, yd, u, x2, mod3, *wts]
    extra = [] if into is None else list(into)
    return pl.pallas_call(
        kern,
        out_shape=[jax.ShapeDtypeStruct((out_rows, D), F32),
                   jax.ShapeDtypeStruct((out_rows, D), F32),
                   jax.ShapeDtypeStruct((out_rows * ROWT, LANE), U32)],
        grid=(N // tm,),
        in_specs=[pl.BlockSpec((tm, RW), lambda i: (i, 0)),
                  pl.BlockSpec((tm, RW), lambda i: (i, 0)),
                  pl.BlockSpec((tm, 2 * RW), lambda i: (i, 0)),
                  pl.BlockSpec((tm, RW), lambda i: (i, 0)),
                  pl.BlockSpec((tm, RW), lambda i: (i, 0)),
                  pl.BlockSpec((tm, RW), lambda i: (i, 0)),
                  pl.BlockSpec((2, tm, S5_W), lambda i: (0, i, 0)),
                  pl.BlockSpec((tm, S5_W), lambda i: (i, 0)),
                  pl.BlockSpec((tm, D), lambda i: (i + t0, 0)),
                  pl.BlockSpec((1, 6, D), lambda i: (row_of_tile(i), 0, 0))] + [full(a) for a in wts]
                 + [pl.BlockSpec(memory_space=pl.ANY)] * len(extra),
        out_specs=[pl.BlockSpec((tm, D), lambda i: (i + o0, 0)),
                   pl.BlockSpec((tm, D), lambda i: (i + o0, 0)),
                   pl.BlockSpec((tm * ROWT, LANE), lambda i: (i + o0, 0))],
        input_output_aliases={len(ins) + k: k for k in range(len(extra))},
        compiler_params=_cp(("arbitrary",)),
        name="mix_out",
    )(*ins, *extra)


def _first_max(x, idx, big):
    m = jnp.max(x, axis=0, keepdims=True)
    first = jnp.min(jnp.where(x == m, idx, big), axis=0, keepdims=True)
    return m, first


def _router_kernel(h_ref, rwt_ref, bias_ref, e_ref, w_ref):
    tm = h_ref.shape[0]
    gsz = N_EXPERTS // N_EGROUPS
    ninf = -jnp.inf
    s = _sigmoid(_dot33(rwt_ref[...], h_ref[...], NT_DIMS))
    ssel = s + bias_ref[:, 0:1]
    gi = lax.broadcasted_iota(I32, (gsz, tm), 0)
    gscore = []
    for g in range(N_EGROUPS):
        xg = ssel[g * gsz:(g + 1) * gsz, :]
        m1, i1 = _first_max(xg, gi, gsz)
        m2 = jnp.max(jnp.where(gi == i1, ninf, xg), axis=0, keepdims=True)
        gscore.append(m1 + m2)
    cur = jnp.concatenate(gscore, axis=0)
    gidx = lax.broadcasted_iota(I32, (N_EGROUPS, tm), 0)
    picked = jnp.zeros((N_EGROUPS, tm), F32)
    for _ in range(TOPK_GROUPS):
        _, first = _first_max(cur, gidx, N_EGROUPS)
        hit = gidx == first
        picked = jnp.where(hit, 1.0, picked)
        cur = jnp.where(hit, ninf, cur)
    x = jnp.concatenate(
        [jnp.where(picked[g:g + 1, :] > 0.5, ssel[g * gsz:(g + 1) * gsz, :], ninf) for g in range(N_EGROUPS)], axis=0)
    ei = lax.broadcasted_iota(I32, (N_EXPERTS, tm), 0)
    idxs, ws = [], []
    for _ in range(TOP_K):
        _, first = _first_max(x, ei, N_EXPERTS)
        hit = ei == first
        idxs.append(first)
        ws.append(jnp.sum(jnp.where(hit, s, 0.0), axis=0, keepdims=True))
        x = jnp.where(hit, ninf, x)
    w = jnp.concatenate(ws, axis=0)
    e_ref[...] = jnp.concatenate(idxs, axis=0)
    w_ref[...] = w / jnp.sum(w, axis=0, keepdims=True) * ROUTE_SCALE


def moe_router(h2, rwt, bias_b, tm):
    N, D = h2.shape
    return pl.pallas_call(
        _router_kernel,
        out_shape=[jax.ShapeDtypeStruct((TOP_K, N), I32), jax.ShapeDtypeStruct((TOP_K, N), F32)],
        grid=(N // tm,),
        in_specs=[pl.BlockSpec((tm, D), lambda i: (i, 0)),
                  pl.BlockSpec((N_EXPERTS, D), lambda i: (0, 0)),
                  pl.BlockSpec((N_EXPERTS, LANE), lambda i: (0, 0))],
        out_specs=[pl.BlockSpec((TOP_K, tm), lambda i: (0, i)),
                   pl.BlockSpec((TOP_K, tm), lambda i: (0, i))],
        compiler_params=_cp(("arbitrary",)),
        name="moe_router",
    )(h2, rwt, bias_b)


def _moe_count_kernel(e_ref, pstart_ref, plan_ref, blk_ref, cnt_scr, *, nt, nbp):
    i = pl.program_id(0)
    tp = e_ref.shape[1]

    @pl.when(i == 0)
    def _():
        cnt_scr[...] = jnp.zeros_like(cnt_scr)

    ei = lax.broadcasted_iota(I32, (N_EXPERTS, tp), 0)
    acc = jnp.zeros((N_EXPERTS, 1), F32)
    for k in range(TOP_K):
        acc = acc + jnp.sum(jnp.where(ei == e_ref[k:k + 1, :], 1.0, 0.0), axis=1, keepdims=True)
    cnt_scr[...] = cnt_scr[...] + acc

    @pl.when(i == nt - 1)
    def _():
        cnt = cnt_scr[...].astype(I32)
        padded = (cnt + (MOE_BLK - 1)) & (-MOE_BLK)
        r = lax.broadcasted_iota(I32, (N_EXPERTS, N_EXPERTS), 0)
        c = lax.broadcasted_iota(I32, (N_EXPERTS, N_EXPERTS), 1)
        tri = jnp.where(c <= r, 1.0, 0.0).astype(BF16)
        padded_b = jnp.broadcast_to(padded.astype(F32), (N_EXPERTS, LANE))
        p_end = _dot_exact_lhs(tri, padded_b)
        pstart = p_end - padded_b
        pstart_ref[...] = pstart.astype(I32)
        diag = r == c
        ps_row = jnp.sum(jnp.where(diag, pstart[:, 0:1], 0.0), axis=0, keepdims=True)
        cnt_row = jnp.sum(jnp.where(diag, cnt_scr[...], 0.0), axis=0, keepdims=True)
        plan_ref[...] = jnp.concatenate([ps_row, cnt_row, jnp.zeros((SUB - 2, N_EXPERTS), F32)], axis=0).astype(I32)
        lim =(lax.broadcasted_iota(I32, (N_EXPERTS, nbp), 1) * MOE_BLK).astype(F32)
        be = jnp.sum(jnp.where(p_end[:, 0:1] <= lim, 1.0, 0.0), axis=0, keepdims=True)
        be = jnp.minimum(be, N_EXPERTS - 1.0)
        nused = jnp.max(p_end[:, 0:1], axis=0, keepdims=True) * (1.0 / MOE_BLK)
        blk_ref[...] = jnp.concatenate([jnp.broadcast_to(be, (SUB // 2, nbp)),
                                        jnp.broadcast_to(nused, (SUB // 2, nbp))], axis=0).astype(I32)


def moe_counts(eidx, tp, nbp):
    N = eidx.shape[1]
    nt = N // tp
    kern = functools.partial(_moe_count_kernel, nt=nt, nbp=nbp)
    return pl.pallas_call(
        kern,
        out_shape=[jax.ShapeDtypeStruct((N_EXPERTS, LANE), I32),
                   jax.ShapeDtypeStruct((SUB, N_EXPERTS), I32),
                   jax.ShapeDtypeStruct((SUB, nbp), I32)],
        grid=(nt,),
        in_specs=[pl.BlockSpec((TOP_K, tp), lambda i: (0, i))],
        out_specs=[pl.BlockSpec((N_EXPERTS, LANE), lambda i: (0, 0)),
                   pl.BlockSpec((SUB, N_EXPERTS), lambda i: (0, 0)),
                   pl.BlockSpec((SUB, nbp), lambda i: (0, 0))],
        scratch_shapes=[pltpu.VMEM((N_EXPERTS, 1), F32)],
        compiler_params=_cp(("arbitrary",)),
        name="moe_counts",
    )(eidx)


def _moe_dest_kernel(e_ref, pstart_ref, dest_ref, base_scr):
    i = pl.program_id(0)
    tp = e_ref.shape[1]

    @pl.when(i == 0)
    def _():
        base_scr[...] = pstart_ref[:, 0:1].astype(F32)

    ei = lax.broadcasted_iota(I32, (N_EXPERTS, tp), 0)
    r = lax.broadcasted_iota(I32, (tp, tp), 0)
    c = lax.broadcasted_iota(I32, (tp, tp), 1)
    tri = jnp.where(r <= c, 1.0, 0.0).astype(BF16)
    base = base_scr[...]
    rows = []
    for k in range(TOP_K):
        hit = ei == e_ref[k:k + 1, :]
        oh = jnp.where(hit, 1.0, 0.0)
        cum = _bdot(oh, tri)
        rows.append(jnp.sum(jnp.where(hit, cum - 1.0 + base, 0.0), axis=0, keepdims=True))
        base = base + cum[:, tp - 1:tp]
    base_scr[...] = base
    dest_ref[...] = jnp.concatenate(rows, axis=0).astype(I32)


def moe_dest(eidx, pstart, tp):
    N = eidx.shape[1]
    return pl.pallas_call(
        _moe_dest_kernel,
        out_shape=jax.ShapeDtypeStruct((TOP_K, N), I32),
        grid=(N // tp,),
        in_specs=[pl.BlockSpec((TOP_K, tp), lambda i: (0, i)),
                  pl.BlockSpec((N_EXPERTS, LANE), lambda i: (0, 0))],
        out_specs=pl.BlockSpec((TOP_K, tp), lambda i: (0, i)),
        scratch_shapes=[pltpu.VMEM((N_EXPERTS, 1), F32)],
        compiler_params=_cp(("arbitrary",)),
        name="moe_dest",
    )(eidx, pstart)


def _tile_at(ref, token):
    return ref.at[pl.ds(pl.multiple_of(token * ROWT, ROWT), ROWT), :]


SCAT_RING = 3


def _moe_scatter_kernel(dest_ref, h_ref, xs_hbm, ring, sem, *, nt):
    s = pl.program_id(0)
    tp = dest_ref.shape[1]
    slot = lax.rem(s, SCAT_RING)
    total = xs_hbm.at[pl.ds(0, tp * TOP_K * ROWT), :]

    def slot_wait(sl):
        pltpu.make_async_copy(total, total, sem.at[sl]).wait()

    @pl.when(s >= SCAT_RING)
    def _():
        slot_wait(slot)

    ring[slot] = h_ref[...]

    def issue(j, carry):
        for k in range(TOP_K):
            pltpu.make_async_copy(_tile_at(ring.at[slot], j), _tile_at(xs_hbm, dest_ref[k, j]),
                                  sem.at[slot]).start(priority=k % 2)
        return carry

    lax.fori_loop(0, tp, issue, 0)

    @pl.when(s == nt - 1)
    def _():
        for back in range(SCAT_RING - 1, -1, -1):
            if nt - 1 - back >= 0:
                slot_wait((nt - 1 - back) % SCAT_RING)


def moe_scatter(dest, h2t, n_slots, tp):
    N = dest.shape[1]
    nt = N // tp
    return pl.pallas_call(
        functools.partial(_moe_scatter_kernel, nt=nt),
        out_shape=jax.ShapeDtypeStruct((n_slots * ROWT, LANE), U32),
        grid=(nt,),
        in_specs=[pl.BlockSpec((TOP_K, tp), lambda i: (0, i), memory_space=pltpu.SMEM),
                  pl.BlockSpec((tp * ROWT, LANE), lambda i: (i, 0))],
        out_specs=pl.BlockSpec(memory_space=pl.ANY),
        scratch_shapes=[pltpu.VMEM((SCAT_RING, tp * ROWT, LANE), U32), pltpu.SemaphoreType.DMA((SCAT_RING,))],
        compiler_params=_cp(("arbitrary",)),
        name="moe_scatter",
    )(dest, h2t)


PAD_BITS = tuple(1 << b for b in reversed(range(MOE_BLK.bit_length() - 1)))


def _moe_padfill_kernel(ps_ref, xs_in, xs_hbm, zero_scr, sem):
    del xs_in
    zero_scr[...] = jnp.zeros_like(zero_scr)

    def pad_copies(e, wait):
        cnt = ps_ref[1, e]
        npad = ((cnt + (MOE_BLK - 1)) & (-MOE_BLK)) - cnt
        off = ps_ref[0, e] + cnt
        for bit in PAD_BITS:
            @pl.when((npad & bit) != 0)
            def _():
                cp = pltpu.make_async_copy(zero_scr.at[pl.ds(0, bit * ROWT), :],
                                           xs_hbm.at[pl.ds(pl.multiple_of(off * ROWT, ROWT), bit * ROWT), :], sem)
                if wait:
                    cp.wait()
                else:
                    cp.start()
            off = off + (npad & bit)

    def issue(e, carry):
        pad_copies(e, False)
        return carry

    def drain(e, carry):
        pad_copies(e, True)
        return carry

    lax.fori_loop(0, N_EXPERTS, issue, 0)
    lax.fori_loop(0, N_EXPERTS, drain, 0)


def moe_padfill(pstart, xs):
    return pl.pallas_call(
        _moe_padfill_kernel,
        out_shape=jax.ShapeDtypeStruct(xs.shape, xs.dtype),
        grid=(1,),
        in_specs=[pl.BlockSpec(memory_space=pltpu.SMEM),
                  pl.BlockSpec(memory_space=pl.ANY)],
        out_specs=pl.BlockSpec(memory_space=pl.ANY),
        scratch_shapes=[pltpu.VMEM((PAD_BITS[0] * ROWT, LANE), U32), pltpu.SemaphoreType.DMA(())],
        input_output_aliases={1: 0},
        compiler_params=_cp(("arbitrary",)),
        name="moe_padfill",
    )(pstart, xs)


EXP_NX = 8
EXP_PF = 4
EXP_NY = 4


def _experts_kernel(plan_ref, w13_ref, w2_ref, xs_hbm, y_hbm, w13_bf, w2_bf, x_buf, y_buf, sem_in, sem_out):
    e = pl.program_id(0)
    ne = pl.num_programs(0)
    ff = w2_bf.shape[0]
    rows = MOE_BLK * ROWT
    shift = MOE_BLK.bit_length() - 1
    blocks_of = lambda c: lax.shift_right_logical(c + (MOE_BLK - 1), shift)
    nblk = blocks_of(plan_ref[1, e])
    first = lax.shift_right_logical(plan_ref[0, e], shift)
    nused = lax.shift_right_logical(plan_ref[0, ne - 1], shift) + blocks_of(plan_ref[1, ne - 1])

    def window(ref, g):
        return ref.at[pl.ds(pl.multiple_of(g * rows, rows), rows), :]

    def x_copy(g):
        return pltpu.make_async_copy(window(xs_hbm, g), x_buf.at[g & (EXP_NX - 1)], sem_in.at[g & (EXP_NX - 1)])

    def y_copy(g):
        return pltpu.make_async_copy(y_buf.at[g & (EXP_NY - 1)], window(y_hbm, g), sem_out.at[g & (EXP_NY - 1)])

    @pl.when(e == 0)
    def _():
        for g in range(EXP_PF):
            @pl.when(g < nused)
            def _():
                x_copy(g).start()

    def process(gs):
        for g in gs:
            x_copy(g).wait()

            @pl.when(g + EXP_PF < nused)
            def _():
                x_copy(g + EXP_PF).start()

            @pl.when(g >= EXP_NY)
            def _():
                y_copy(g - EXP_NY).wait()

        x = jnp.concatenate([_tile_rows_load(x_buf.at[g & (EXP_NX - 1)], MOE_BLK).astype(BF16) for g in gs], axis=0)
        h = jnp.dot(x, w13_bf[...], preferred_element_type=F32)
        act = (_silu(h[:, 0:ff]) * h[:, ff:2 * ff]).astype(BF16)
        y = jnp.dot(act, w2_bf[...], preferred_element_type=F32)
        for i, g in enumerate(gs):
            _tile_rows_store(y_buf.at[g & (EXP_NY - 1)], y[i * MOE_BLK:(i + 1) * MOE_BLK])
            y_copy(g).start()

    @pl.when(nblk > 0)
    def _():
        w13_bf[...] = w13_ref[0, 0].astype(BF16)
        w2_bf[...] = w2_ref[0, 0].astype(BF16)
        npair = lax.shift_right_logical(nblk, 1)

        def body(j, carry):
            g = first + 2 * j
            process([g, g + 1])
            return carry

        lax.fori_loop(0, npair, body, 0)

        @pl.when((nblk & 1) == 1)
        def _():
            process([first + nblk - 1])

    @pl.when(e == ne - 1)
    def _():
        for back in range(EXP_NY, 0, -1):
            @pl.when(nused >= back)
            def _():
                y_copy(nused - back).wait()


def moe_experts(plan, xs, w13, w2, layer):
    _, E, D, F2 = w13.shape
    rows = MOE_BLK * ROWT
    grid_spec = pltpu.PrefetchScalarGridSpec(
        num_scalar_prefetch=1,
        grid=(E,),
        in_specs=[pl.BlockSpec((1, 1, D, F2), lambda e, p: (layer, e, 0, 0)),
                  pl.BlockSpec((1, 1, F2 // 2, D), lambda e, p: (layer, e, 0, 0)),
                  pl.BlockSpec(memory_space=pl.ANY)],
        out_specs=pl.BlockSpec(memory_space=pl.ANY),
        scratch_shapes=[pltpu.VMEM((D, F2), BF16), pltpu.VMEM((F2 // 2, D), BF16),
                        pltpu.VMEM((EXP_NX, rows, LANE), U32), pltpu.VMEM((EXP_NY, rows, LANE), U32),
                        pltpu.SemaphoreType.DMA((EXP_NX,)), pltpu.SemaphoreType.DMA((EXP_NY,))])
    return pl.pallas_call(
        _experts_kernel,
        out_shape=jax.ShapeDtypeStruct(xs.shape, U32),
        grid_spec=grid_spec,
        compiler_params=_cp(("arbitrary",)),
        name="moe_experts",
    )(plan, w13, w2, xs)


def _moe_combine_kernel(dest_ref, wt_ref, h_ref, x_ref, mod_ref, s13_ref, s2_ref, l2g_ref, l2b_ref, y_hbm,
                        o_ref, g_scr, sem, *, alpha, nt):
    s = pl.program_id(0)
    tm = h_ref.shape[0]
    ff = s2_ref.shape[0]
    rows = tm * ROWT
    slot_g = s % 2
    slot_c = 1 - slot_g

    def slot_copy(slot):
        return pltpu.make_async_copy(y_hbm.at[pl.ds(0, TOP_K * rows), :], g_scr.at[slot], sem.at[slot])

    @pl.when(s == 0)
    def _():
        g_scr[1] = jnp.zeros((TOP_K * rows, LANE), U32)

    @pl.when(s > 0)
    def _():
        slot_copy(slot_c).wait()

    for j in range(tm):
        for k in range(TOP_K):
            pltpu.make_async_copy(_tile_at(y_hbm, dest_ref[k, j]),
                                  g_scr.at[slot_g, pl.ds(k * rows + j * ROWT, ROWT), :],
                                  sem.at[slot_g]).start(priority=k % 2)
    hs = _bdot(h_ref[...], s13_ref[...])
    f = _bdot(_silu(hs[:, 0:ff]) * hs[:, ff:2 * ff], s2_ref[...])
    for k in range(TOP_K):
        f = f + wt_ref[:, k:k + 1] * _tile_rows_load(g_scr.at[slot_c], tm, base=k * rows)
    o_ref[...] = (_layer_norm(alpha * x_ref[...] + mod_ref[0, 5:6, :] * f) * l2g_ref[...] + l2b_ref[...])

    @pl.when(s == nt)
    def _():
        slot_copy(slot_g).wait()


def moe_combine(dest, wt_t, h2, x1, mod3, wts, y, row_of_tile, tm, alpha):
    N, D = h2.shape
    nt = N // tm
    full = lambda a: pl.BlockSpec(a.shape, lambda s: (0,) * a.ndim)
    kern = functools.partial(_moe_combine_kernel, alpha=alpha, nt=nt)
    nxt = lambda s: jnp.minimum(s, nt - 1)
    cur = lambda s: jnp.maximum(s - 1, 0)
    return pl.pallas_call(
        kern,
        out_shape=jax.ShapeDtypeStruct((N, D), F32),
        grid=(nt + 1,),
        in_specs=[pl.BlockSpec((TOP_K, tm), lambda s: (0, nxt(s)), memory_space=pltpu.SMEM),
                  pl.BlockSpec((tm, TOP_K), lambda s: (cur(s), 0)),
                  pl.BlockSpec((tm, D), lambda s: (cur(s), 0)),
                  pl.BlockSpec((tm, D), lambda s: (cur(s), 0)),
                  pl.BlockSpec((1, 6, D), lambda s: (row_of_tile(cur(s)), 0, 0))]
                 + [full(a) for a in wts] + [pl.BlockSpec(memory_space=pl.ANY)],
        out_specs=pl.BlockSpec((tm, D), lambda s: (cur(s), 0)),
        scratch_shapes=[pltpu.VMEM((2, TOP_K * tm * ROWT, LANE), U32), pltpu.SemaphoreType.DMA((2,))],
        compiler_params=_cp(("arbitrary",)),
        name="moe_combine",
    )(dest, wt_t, h2, x1, mod3, *wts, y)


def moe_ffn(h2, h2t, x1, mod3, mp, row_of_tile, tm, alpha):
    N = h2.shape[0]
    nb_total = (N * TOP_K + MOE_BLK - 1) // MOE_BLK + N_EXPERTS
    nbp = ((nb_total + LANE - 1) // LANE) * LANE
    tp = _tile(N, 256)
    eidx, wts = moe_router(h2, mp['rwt'], mp['bias'], tp)
    pstart, plan, blk = moe_counts(eidx, tp, nbp)
    dest = moe_dest(eidx, pstart, tp)
    xs = moe_scatter(dest, h2t, nb_total * MOE_BLK, tp)
    xs = moe_padfill(plan, xs)
    y = moe_experts(plan, xs, mp['w13'], mp['w2'], mp['layer'])
    return moe_combine(dest, wts.T, h2, x1, mod3, mp['comb'], y, row_of_tile, tm, alpha)


def token_mixers(stream, B, T, W, mod3, row_of_tile, lp, states, tm):
    pa, pbg, gg, u = inproj(stream, mod3, lp['w_in'], row_of_tile, tm)
    pac = short_conv(pa.reshape(B, T, PA_W), lp['taps'], W).reshape(B * T, PA_W)
    ss, lw, kb, gb = rwkv_prep(pac, lp['prep'], tm)
    s_rwkv, s_gla, s_s5 = states
    y2, f_rwkv = rwkv_scan(ss, lw, kb, s_rwkv, B, T)
    o2, f_gla = gla_scan(pbg, lp['gla_aup'], lp['gla_ab'], s_gla, B, T)
    yd, f_s5 = s5_scan(u.reshape(B, T, S5_W), lp['s5_a'], lp['s5_bbd'], lp['s5_cbd'], s_s5)
    return (y2, gb, o2, gg, yd.reshape(2, B * T, S5_W), u), (f_rwkv, f_gla, f_s5)


def zero_states(B):
    return ((jnp.zeros((B, RW // LANE, LANE, LANE), F32),) * 2,
            (jnp.zeros((B, RW, GLA_KP), F32),) * 2,
            jnp.zeros((2, B, S5_LANES), F32))


def _inproj_columns():
    r_cols = 3 * RW + 2 * DECAY_RANK + 2 * ICLR_RANK + GATE_RANK
    kd = GLA_HEADS * GLA_K
    gq, gk, gv = r_cols, r_cols + kd, r_cols + 2 * kd
    gg = gv + RW
    gad = gg + RW
    pc = gad + 2 * GLA_RANK
    z = lambda n: [-1] * n
    cols = list(range(0, r_cols)) + z(PA_W - r_cols)
    cols += list(range(gq, gq + kd)) + z(256 - kd)
    cols += list(range(gk, gk + kd)) + z(256 - kd)
    cols += list(range(gv, gv + RW))
    cols += list(range(gad, gad + 2 * GLA_RANK)) + z(LANE - 2 * GLA_RANK)
    cols += list(range(gg, gg + RW))
    cols += list(range(pc, pc + S5_W))
    assert len(cols) == NP_W
    return np.asarray(cols, np.int32)


def _head_block_ones():
    h = np.arange(RW) // HEAD
    return jnp.asarray(h[:, None] == h[None, :], BF16)


def _layer_params(l, p):
    D = p['w_in'].shape[1]
    cols = _inproj_columns()
    w_in = jnp.concatenate([p['w_in'][l], jnp.zeros((D, 1), F32)], axis=1)
    w_in = jnp.take(w_in, jnp.asarray(np.where(cols < 0, w_in.shape[1] - 1, cols)), axis=1).astype(BF16)
    taps = p['rwkv_conv'][l].reshape(9, -1)
    taps = jnp.concatenate([taps, jnp.zeros((9, PA_W - taps.shape[1]), F32)], axis=1)
    wup = jnp.zeros((LANE, 2 * RW), F32)
    aup = jnp.zeros((LANE, 2 * RW), F32)
    for d in range(2):
        wup = wup.at[d * DECAY_RANK:(d + 1) * DECAY_RANK, d * RW:(d + 1) * RW].set(p['rwkv_w_up'][l, d])
        o = 2 * DECAY_RANK + d * ICLR_RANK
        aup = aup.at[o:o + ICLR_RANK, d * RW:(d + 1) * RW].set(p['rwkv_a_up'][l, d])
    gup = jnp.zeros((LANE, RW), F32).at[0:GATE_RANK].set(p['rwkv_g_up'][l])
    row = lambda a: a.reshape(1, -1)
    prep = (wup, aup, gup, row(p['rwkv_w0'][l]), row(p['rwkv_a0'][l]), row(p['rwkv_k_k'][l]),
            row(p['rwkv_k_a'][l]), row(p['rwkv_r_k'][l]), _head_block_ones())
    kd = GLA_HEADS * GLA_K
    gla_aup = jnp.zeros((2, LANE, GLA_KP), F32)
    for d in range(2):
        gla_aup = gla_aup.at[d, d * GLA_RANK:(d + 1) * GLA_RANK, 0:kd].set(p['gla_a_up'][l, d])
    gla_ab = jnp.zeros((2, 1, GLA_KP), F32).at[:, 0, 0:kd].set(p['gla_a_bias'][l])
    lam_re, lam_im = p['s5_lam_re'][l], p['s5_lam_im'][l]
    dt = jnp.exp(p['s5_log_dt'][l])[:, :, None]
    zr, zi = lam_re[:, None, :] * dt, lam_im[:, None, :] * dt
    mag = jnp.exp(zr)
    ab_r, ab_i = mag * jnp.cos(zi), mag * jnp.sin(zi)
    den = (lam_re * lam_re + lam_im * lam_im)[:, None, :]
    f_r = ((ab_r - 1) * lam_re[:, None, :] + ab_i * lam_im[:, None, :]) / den
    f_i = (ab_i * lam_re[:, None, :] - (ab_r - 1) * lam_im[:, None, :]) / den
    b_re, b_im = p['s5_b_re'][l], p['s5_b_im'][l]
    bb_r = f_r[..., None] * b_re - f_i[..., None] * b_im
    bb_i = f_r[..., None] * b_im + f_i[..., None] * b_re
    eye_g = jnp.eye(S5_GROUPS, dtype=F32)
    half = S5_LANES // 2

    def in_blockdiag(bb):
        return jnp.einsum('dgpc,gh->dgchp', bb, eye_g).reshape(2, S5_W, half)

    def out_blockdiag(cc):
        return jnp.einsum('dgcp,gh->dgphc', cc, eye_g).reshape(2, half, S5_W)

    s5_bbd = jnp.concatenate([in_blockdiag(bb_r), in_blockdiag(bb_i)], axis=2).astype(BF16)
    s5_cbd = jnp.concatenate([out_blockdiag(p['s5_c_re'][l]), -out_blockdiag(p['s5_c_im'][l])], axis=1).astype(BF16)
    s5_a = jnp.concatenate([ab_r.reshape(2, 1, half), ab_i.reshape(2, 1, half)], axis=2)
    mix = (_head_block_ones(), row(p['rwkv_ln_g'][l]), row(p['rwkv_ln_b'][l]),
           row(jnp.tile(p['gla_norm_g'][l], GLA_HEADS)), row(p['s5_d'][l]),
           p['s5_glu_w'][l].astype(BF16), row(p['s5_glu_b'][l]), p['w_out'][l].astype(BF16),
           row(p['ln1_g'][l]), row(p['ln1_b'][l]))
    moe = dict(rwt=p['router_w'][l].T,
               bias=jnp.broadcast_to(p['router_bias'][l][:, None], (N_EXPERTS, LANE)),
               w13=p['exp_w13'], w2=p['exp_w2'], layer=l,
               comb=(p['sh_w13'][l].astype(BF16), p['sh_w2'][l].astype(BF16), row(p['ln2_g'][l]), row(p['ln2_b'][l])))
    return dict(w_in=w_in, taps=taps, prep=prep, gla_aup=gla_aup, gla_ab=gla_ab,
                s5_a=s5_a, s5_bbd=s5_bbd, s5_cbd=s5_cbd, mix=mix, moe=moe)


_ARG_NAMES = ('x', 'c', 'ctx', 'c_ctx', 'w_mod', 'b_mod', 'w_in', 'rwkv_conv', 'rwkv_w0', 'rwkv_w_up', 'rwkv_a0',
              'rwkv_a_up', 'rwkv_g_up', 'rwkv_k_k', 'rwkv_k_a', 'rwkv_r_k', 'rwkv_ln_g', 'rwkv_ln_b', 'gla_a_up',
              'gla_a_bias', 'gla_norm_g', 's5_lam_re', 's5_lam_im', 's5_log_dt', 's5_b_re', 's5_b_im', 's5_c_re',
              's5_c_im', 's5_d', 's5_glu_w', 's5_glu_b', 'w_out', 'ln1_g', 'ln1_b', 'router_w', 'router_bias',
              'exp_w13', 'exp_w2', 'sh_w13', 'sh_w2', 'ln2_g', 'ln2_b')


def _tile(n, pref):
    t = pref
    while n % t:
        t //= 2
    return t


def kernel(x, c, ctx, c_ctx, w_mod, b_mod, w_in, rwkv_conv, rwkv_w0, rwkv_w_up, rwkv_a0, rwkv_a_up, rwkv_g_up,
           rwkv_k_k, rwkv_k_a, rwkv_r_k, rwkv_ln_g, rwkv_ln_b, gla_a_up, gla_a_bias, gla_norm_g, s5_lam_re,
           s5_lam_im, s5_log_dt, s5_b_re, s5_b_im, s5_c_re, s5_c_im, s5_d, s5_glu_w, s5_glu_b, w_out, ln1_g,
           ln1_b, router_w, router_bias, exp_w13, exp_w2, sh_w13, sh_w2, ln2_g, ln2_b):
    p = dict(zip(_ARG_NAMES, (x, c, ctx, c_ctx, w_mod, b_mod, w_in, rwkv_conv, rwkv_w0, rwkv_w_up, rwkv_a0,
                              rwkv_a_up, rwkv_g_up, rwkv_k_k, rwkv_k_a, rwkv_r_k, rwkv_ln_g, rwkv_ln_b, gla_a_up,
                              gla_a_bias, gla_norm_g, s5_lam_re, s5_lam_im, s5_log_dt, s5_b_re, s5_b_im, s5_c_re,
                              s5_c_im, s5_d, s5_glu_w, s5_glu_b, w_out, ln1_g, ln1_b, router_w, router_bias,
                              exp_w13, exp_w2, sh_w13, sh_w2, ln2_g, ln2_b)))
    B, T, D = x.shape
    TC = ctx.shape[1]
    L = w_mod.shape[0]
    alpha = (2 * L) ** 0.25
    n_lat, n_ctx = B * T, B * TC
    R = ((B + 1 + SUB - 1) // SUB) * SUB
    cc = jnp.zeros((R, D), F32).at[0:B].set(c).at[B].set(c_ctx)
    mod = mod_table(cc, w_mod, b_mod)
    tm = _tile(T, 512)
    tmc = _tile(n_ctx, 512)
    tmm = min(_tile(T, 128), _tile(n_ctx, 128))
    lat_row = lambda i: (i * tm) // T
    ctx_row = lambda i: B
    lat = (x.reshape(n_lat, D), 0, n_lat)
    con = (ctx.reshape(n_ctx, D), 0, n_ctx)
    for l in range(L):
        last = l == L - 1
        lp = _layer_params(l, p)
        mod3 = mod[l].reshape(R, 6, D)
        outs_c, st_c = token_mixers(con, B, TC, TC, mod3, ctx_row, lp, zero_states(B), tmc)
        outs, _ = token_mixers(lat, B, T, GRID_W, mod3, lat_row, lp, st_c, tm)
        if last:
            x1, h2, h2t = mix_out(*outs, lat, mod3, lp['mix'], lat_row, tm, alpha)
            out = moe_ffn(h2, h2t, x1, mod3, lp['moe'], lambda i: (i * tmm) // T, tmm, alpha)
            lat = (out, 0, n_lat)
        else:
            n_all = n_ctx + n_lat
            bufs = mix_out(*outs_c, con, mod3, lp['mix'], ctx_row, tmc, alpha, out_rows=n_all)
            x1, h2, h2t = mix_out(*outs, lat, mod3, lp['mix'], lat_row, tm, alpha,
                                  out_rows=n_all, out_row0=n_ctx, into=bufs)
            row_all = lambda i: jnp.where(i * tmm < n_ctx, B, (i * tmm - n_ctx) // T)
            out = moe_ffn(h2, h2t, x1, mod3, lp['moe'], row_all, tmm, alpha)
            con, lat = (out, 0, n_ctx), (out, n_ctx, n_lat)
    out, row0, _ = lat
    return out[row0:row0 + n_lat].reshape(B, T, D)
```

```python
import functools
import math

import numpy as np
import jax
import jax.numpy as jnp
from jax import lax
from jax.experimental import pallas as pl
from jax.experimental.pallas import tpu as pltpu

F32 = jnp.float32
BF16 = jnp.bfloat16
I32 = jnp.int32
ACT = BF16

GRID_W = 64
RWKV_HEADS = 6
HEAD = 64
RW = RWKV_HEADS * HEAD
DECAY_RANK = 32
ICLR_RANK = 32
GATE_RANK = 64
RWKV_GN_EPS = 64e-5
GLA_HEADS = 6
GLA_K = 32
GLA_RANK = 16
GLA_TEMP = 16.0
S5_GROUPS = 16
S5_GROUP = 16
S5_STATE = 64
S5_W = S5_GROUPS * S5_GROUP
S5_LANES = 2 * S5_GROUPS * S5_STATE
N_EXPERTS = 256
TOP_K = 8
N_EGROUPS = 8
TOPK_GROUPS = 4
ROUTE_SCALE = 2.5
LN_EPS = 1e-6
CHUNK = 64
MOE_BLK = 256
LANE = 128
SUB = 8
VMEM_LIMIT = 56 * 1024 * 1024

PA_W = 3 * RW + 2 * LANE
PBG_W = 256 + 256 + RW + LANE
NP_W = PA_W + PBG_W + RW + S5_W


def _cp(sem):
    return pltpu.CompilerParams(dimension_semantics=sem, vmem_limit_bytes=VMEM_LIMIT)


def _sigmoid(x):
    return 1.0 / (1.0 + jnp.exp(-x))


def _silu(x):
    return x * _sigmoid(x)


def _bdot(a, b, dims=None):
    a = a.astype(BF16)
    b = b.astype(BF16)
    if dims is None:
        return jnp.dot(a, b, preferred_element_type=F32)
    return lax.dot_general(a, b, dims, preferred_element_type=F32)


def _split2(x):
    hi = x.astype(BF16)
    lo = (x - hi.astype(F32)).astype(BF16)
    return hi, lo


def _split3(x):
    hi = x.astype(BF16)
    r = x - hi.astype(F32)
    mid = r.astype(BF16)
    lo = (r - mid.astype(F32)).astype(BF16)
    return hi, mid, lo


NT_DIMS = (((1,), (1,)), ((), ()))
TN_DIMS = (((0,), (0,)), ((), ()))


def _dot33(a, b, dims=None):
    ah, al = _split2(a)
    bh, bl = _split2(b)
    return _bdot(ah, bh, dims) + (_bdot(ah, bl, dims) + _bdot(al, bh, dims))


def _dot_exact_lhs(m_exact, x, dims=None):
    h, m, l = _split3(x)
    return _bdot(m_exact, h, dims) + (_bdot(m_exact, m, dims) + _bdot(m_exact, l, dims))


def _dot_exact_rhs(x, m_exact, dims=None):
    h, m, l = _split3(x)
    return _bdot(h, m_exact, dims) + (_bdot(m, m_exact, dims) + _bdot(l, m_exact, dims))


def _dot_split_rhs(x, m_exact, dims=None):
    h, l = _split2(x)
    return _bdot(h, m_exact, dims) + _bdot(l, m_exact, dims)


def _layer_norm(x):
    mu = jnp.mean(x, axis=-1, keepdims=True)
    xc = x - mu
    var = jnp.mean(xc * xc, axis=-1, keepdims=True)
    return xc * lax.rsqrt(var + LN_EPS)


def _mod_kernel(c_ref, w_ref, b_ref, o_ref):
    s = _silu(c_ref[...])
    o_ref[0] = _dot33(s, w_ref[0]) + b_ref[0]


def mod_table(cc, w_mod, b_mod):
    L, D, D6 = w_mod.shape
    R = cc.shape[0]
    tn = 1536
    return pl.pallas_call(
        _mod_kernel,
        out_shape=jax.ShapeDtypeStruct((L, R, D6), F32),
        grid=(L, D6 // tn),
        in_specs=[pl.BlockSpec((R, D), lambda l, j: (0, 0)),
                  pl.BlockSpec((1, D, tn), lambda l, j: (l, 0, j)),
                  pl.BlockSpec((1, 1, tn), lambda l, j: (l, 0, j))],
        out_specs=pl.BlockSpec((1, R, tn), lambda l, j: (l, 0, j)),
        compiler_params=_cp(("arbitrary", "arbitrary")),
        name="mod_table",
    )(cc, w_mod, b_mod.reshape(L, 1, D6))


def _inproj_kernel(x_ref, mod_ref, w_ref, pa_ref, pbg_ref, gg_ref, u_ref):
    x = x_ref[...]
    h = _layer_norm(x) * (1.0 + mod_ref[0, 1:2, :]) + mod_ref[0, 0:1, :]
    hb = h.astype(BF16)
    o = 0
    for ref in (pa_ref, pbg_ref, gg_ref, u_ref):
        w = ref.shape[-1]
        ref[...] = jnp.dot(hb, w_ref[:, o:o + w], preferred_element_type=F32).astype(ref.dtype)
        o += w


def inproj(stream, mod3, w_bf, row_of_tile, tm):
    x2, row0, N = stream
    D = x2.shape[1]
    t0 = row0 // tm
    assert row0 % tm == 0 and N % tm == 0
    return pl.pallas_call(
        _inproj_kernel,
        out_shape=[jax.ShapeDtypeStruct((N, PA_W), ACT),
                   jax.ShapeDtypeStruct((N, PBG_W), ACT),
                   jax.ShapeDtypeStruct((N, RW), ACT),
                   jax.ShapeDtypeStruct((N, S5_W), ACT)],
        grid=(N // tm,),
        in_specs=[pl.BlockSpec((tm, D), lambda i: (i + t0, 0)),
                  pl.BlockSpec((1, 6, D), lambda i: (row_of_tile(i), 0, 0)),
                  pl.BlockSpec((D, NP_W), lambda i: (0, 0))],
        out_specs=[pl.BlockSpec((tm, PA_W), lambda i: (i, 0)),
                   pl.BlockSpec((tm, PBG_W), lambda i: (i, 0)),
                   pl.BlockSpec((tm, RW), lambda i: (i, 0)),
                   pl.BlockSpec((tm, S5_W), lambda i: (i, 0))],
        compiler_params=_cp(("arbitrary",)),
        name="inproj",
    )(x2, mod3, w_bf)


CONV_PAD = 72


def _conv_kernel(x_ref, taps_ref, o_ref, buf_ref, *, T, W, vertical):
    pad = CONV_PAD
    zeros = jnp.zeros((pad, LANE), F32)
    buf_ref[0:pad, :] = zeros
    buf_ref[pad + T:pad + T + pad, :] = zeros
    buf_ref[pad:pad + T, :] = x_ref[0].astype(F32)
    ch = min(T, 256)
    col = lax.broadcasted_iota(I32, (ch, LANE), 0) & (W - 1)
    left_ok = col >= 1
    right_ok = col <= W - 2
    for c in range(T // ch):
        base = pad + c * ch
        acc = jnp.zeros((ch, LANE), F32)
        for dr in ((0, 1, 2) if vertical else (1,)):
            for dc in range(3):
                off = (dr - 1) * W + (dc - 1)
                v = buf_ref[base + off:base + off + ch, :]
                if dc == 0:
                    v = jnp.where(left_ok, v, 0.0)
                elif dc == 2:
                    v = jnp.where(right_ok, v, 0.0)
                acc = acc + v * taps_ref[3 * dr + dc:3 * dr + dc + 1, :]
        o_ref[0, c * ch:(c + 1) * ch, :] = acc.astype(o_ref.dtype)


def short_conv(pa3, taps9, W):
    B, T, C = pa3.shape
    vertical = T > W
    assert W & (W - 1) == 0 and (not vertical or W + 1 <= CONV_PAD)
    kern = functools.partial(_conv_kernel, T=T, W=W, vertical=vertical)
    return pl.pallas_call(
        kern,
        out_shape=jax.ShapeDtypeStruct((B, T, C), ACT),
        grid=(B, C // LANE),
        in_specs=[pl.BlockSpec((1, T, LANE), lambda b, j: (b, 0, j)),
                  pl.BlockSpec((9, LANE), lambda b, j: (0, j))],
        out_specs=pl.BlockSpec((1, T, LANE), lambda b, j: (b, 0, j)),
        scratch_shapes=[pltpu.VMEM((T + 2 * CONV_PAD, LANE), F32)],
        compiler_params=_cp(("arbitrary", "arbitrary")),
        name="short_conv",
    )(pa3, taps9)


def _rwkv_prep_kernel(pa_ref, wup_ref, aup_ref, gup_ref, w0_ref, a0_ref, kk_ref, ka_ref, rk_ref, hb_ref,
                      ss_ref, lw_ref, kb_ref, gb_ref):
    r = pa_ref[:, 0:RW].astype(F32)
    k = pa_ref[:, RW:2 * RW].astype(F32)
    v = pa_ref[:, 2 * RW:3 * RW].astype(F32)
    wa = pa_ref[:, 3 * RW:3 * RW + LANE].astype(F32)
    gd = pa_ref[:, 3 * RW + LANE:3 * RW + 2 * LANE].astype(F32)
    z = w0_ref[...] + _bdot(jnp.tanh(wa), wup_ref[...])
    lw = -_sigmoid(z) * math.exp(-0.5)
    a = _sigmoid(a0_ref[...] + _bdot(wa, aup_ref[...]))
    g = _bdot(_sigmoid(gd), gup_ref[...])
    hb = hb_ref[...]
    kk = k * kk_ref[...]
    kk = kk * lax.rsqrt(_bdot(kk * kk, hb) + 1e-12)
    ka = ka_ref[...]
    ss_ref[:, 0:RW] = r.astype(ACT)
    ss_ref[:, RW:2 * RW] = v.astype(ACT)
    ss_ref[:, 2 * RW:3 * RW] = kk.astype(ACT)
    rk2 = jnp.zeros_like(r)
    for d in range(2):
        ad = a[:, d * RW:(d + 1) * RW]
        k2 = k * (1.0 + (ad - 1.0) * ka)
        lw_ref[d] = lw[:, d * RW:(d + 1) * RW]
        kb_ref[d, :, 0:RW] = k2.astype(ACT)
        kb_ref[d, :, RW:2 * RW] = (kk * ad).astype(ACT)
        rk2 = rk2 + r * k2
    bonus = _bdot(rk2 * rk_ref[...], hb) * v
    gb_ref[:, 0:RW] = g.astype(ACT)
    gb_ref[:, RW:2 * RW] = bonus.astype(ACT)


def rwkv_prep(pa2, wts, tm):
    N = pa2.shape[0]
    full = lambda a: pl.BlockSpec(a.shape, lambda i: (0,) * a.ndim)
    return pl.pallas_call(
        _rwkv_prep_kernel,
        out_shape=[jax.ShapeDtypeStruct((N, 3 * RW), ACT),
                   jax.ShapeDtypeStruct((2, N, RW), F32),
                   jax.ShapeDtypeStruct((2, N, 2 * RW), ACT),
                   jax.ShapeDtypeStruct((N, 2 * RW), ACT)],
        grid=(N // tm,),
        in_specs=[pl.BlockSpec((tm, PA_W), lambda i: (i, 0))] + [full(a) for a in wts],
        out_specs=[pl.BlockSpec((tm, 3 * RW), lambda i: (i, 0)),
                   pl.BlockSpec((2, tm, RW), lambda i: (0, i, 0)),
                   pl.BlockSpec((2, tm, 2 * RW), lambda i: (0, i, 0)),
                   pl.BlockSpec((tm, 2 * RW), lambda i: (i, 0))],
        compiler_params=_cp(("arbitrary",)),
        name="rwkv_prep",
    )(pa2, *wts)


RWKV_GROUP = 32
RWKV_WAVE = 4


def _rwkv_scan_kernel(ss_ref, lw_ref, kb_ref, s0_ref, y_ref, sfin_ref, s_scr, *, nsteps, group, reverse):
    n = pl.program_id(1)
    C = CHUNK
    P = 2 * C
    npair = RW // LANE

    @pl.when(n == 0)
    def _():
        s_scr[...] = s0_ref[0]

    row = lax.broadcasted_iota(I32, (P, P), 0)
    col = lax.broadcasted_iota(I32, (P, P), 1)
    same = (row >> 6) == (col >> 6)
    dlt = (col & (C - 1)) - (row & (C - 1)) if reverse else (row & (C - 1)) - (col & (C - 1))
    strict = same & (dlt > 0)
    incl = same & (dlt >= 0)
    eye = (row == col).astype(F32)
    lvl_masks = [((row >> (lvl + 1)) == (col >> (lvl + 1))) & ((row >> lvl) != (col >> lvl)) for lvl in range(6)]
    rc = lax.broadcasted_iota(I32, (C, C), 0)
    cc = lax.broadcasted_iota(I32, (C, C), 1)
    tri = jnp.where((cc >= rc) if reverse else (rc >= cc), 1.0, 0.0).astype(BF16)
    head0 = lax.broadcasted_iota(I32, (C, LANE), 1) < HEAD

    def stack(x):
        return jnp.concatenate([jnp.where(head0, x, 0.0), jnp.where(head0, 0.0, x)], axis=0)

    pairs = range(npair)
    tm_ = {}

    def independent_stages(chunks):
        streams = [(g, p) for g in chunks for p in pairs]

        def st_cumsum():
            for (g, p) in streams:
                lw = lw_ref[0, g * C:(g + 1) * C, p * LANE:(p + 1) * LANE]
                tm_[(g, p)] = dict(lw=lw, cl=_dot_exact_lhs(tri, lw))

        def st_scores():
            for (g, p) in streams:
                t = tm_[(g, p)]
                t0, t1 = g * C, (g + 1) * C
                lo, hi = p * LANE, (p + 1) * LANE
                r = ss_ref[t0:t1, lo:hi].astype(F32)
                v = ss_ref[t0:t1, RW + lo:RW + hi].astype(F32)
                kk = ss_ref[t0:t1, 2 * RW + lo:2 * RW + hi].astype(F32)
                k2 = kb_ref[0, t0:t1, lo:hi].astype(F32)
                b = kb_ref[0, t0:t1, RW + lo:RW + hi].astype(F32)
                cl, lw = t['cl'], t['lw']
                t['ptot'] = jnp.exp(jnp.sum(lw, axis=0, keepdims=True))
                pinv = jnp.exp(-cl)
                left = jnp.concatenate([stack(-kk * jnp.exp(cl - lw)), stack(r * jnp.exp(cl))], axis=0)
                right = jnp.concatenate([stack(b * pinv), stack(k2 * pinv)], axis=0)
                t['v_st'] = stack(v)
                t['left'] = left.astype(BF16)
                t['bk'] = (right * t['ptot']).astype(BF16)
                aa = _bdot(left, right, NT_DIMS)
                t['nmat'] = jnp.where(strict, aa[0:P, 0:P], 0.0)
                t['a_ak'] = jnp.where(strict, aa[0:P, P:2 * P], 0.0)
                t['a_rbk'] = jnp.concatenate([jnp.where(incl, aa[P:2 * P, 0:P], 0.0),
                                              jnp.where(incl, aa[P:2 * P, P:2 * P], 0.0)], axis=1).astype(BF16)
                t['tinv'] = eye + jnp.where(lvl_masks[0], t['nmat'], 0.0)

        def st_akv():
            for sk in streams:
                t = tm_[sk]
                t['akv'] = _bdot(t['a_ak'], t['v_st'])

        def st_et(m):
            def run():
                for sk in streams:
                    t = tm_[sk]
                    t['et'] = _bdot(jnp.where(m, t['nmat'], 0.0), t['tinv'])
            return run

        def st_tinv():
            for sk in streams:
                t = tm_[sk]
                t['tinv'] = t['tinv'] + _bdot(t['tinv'], t['et'])

        stages = [st_cumsum, st_scores, st_akv]
        for m in lvl_masks[1:]:
            stages += [st_et(m), st_tinv]
        return stages

    state = [s_scr[p] for p in pairs]
    carry = {}

    def dependent_stages(chunks):
        stages = []
        for g in chunks:
            def st_as(g=g):
                carry['a_s'] = [_bdot(tm_[(g, p)]['left'], state[p], NT_DIMS) for p in pairs]

            def st_u(g=g):
                u = [_bdot(tm_[(g, p)]['tinv'], carry['a_s'][p][0:P] + tm_[(g, p)]['akv']) for p in pairs]
                carry['uv'] = [jnp.concatenate([u[p], tm_[(g, p)]['v_st']], axis=0) for p in pairs]

            def st_state(g=g):
                for p in pairs:
                    t = tm_[(g, p)]
                    state[p] = state[p] * t['ptot'] + _bdot(carry['uv'][p], t['bk'], TN_DIMS)

            def st_y(g=g, a_s=None):
                for p in pairs:
                    y_st = carry['a_s'][p][P:2 * P] + _bdot(tm_[(g, p)]['a_rbk'], carry['uv'][p])
                    y_ref[g * C:(g + 1) * C, p * LANE:(p + 1) * LANE] = (y_st[0:C] + y_st[C:P]).astype(y_ref.dtype)

            stages += [st_as, st_u, st_y, st_state]
        return stages

    order = list(range(group - 1, -1, -1) if reverse else range(group))
    wave = RWKV_WAVE if group >= 2 * RWKV_WAVE else group
    pending = []
    for w in range(0, group, wave):
        ind = independent_stages(order[w:w + wave])
        for i in range(max(len(ind), len(pending))):
            if i < len(ind):
                ind[i]()
            if i < len(pending):
                pending[i]()
        pending = dependent_stages(order[w:w + wave])
    for stage in pending:
        stage()
    for p in pairs:
        s_scr[p] = state[p]

    @pl.when(n == nsteps - 1)
    def _():
        sfin_ref[0] = s_scr[...]


def rwkv_scan_dir(ss, lw, kb, s0, B, T, d):
    nc = T // CHUNK
    group = min(RWKV_GROUP, nc)
    nsteps = nc // group
    npair = RW // LANE
    rows = group * CHUNK
    blk = (lambda b, n: b * nsteps + nsteps - 1 - n) if d else (lambda b, n: b * nsteps + n)
    kern = functools.partial(_rwkv_scan_kernel, nsteps=nsteps, group=group, reverse=bool(d))
    return pl.pallas_call(
        kern,
        out_shape=[jax.ShapeDtypeStruct((B * T, RW), ACT),
                   jax.ShapeDtypeStruct((B, npair, LANE, LANE), F32)],
        grid=(B, nsteps),
        in_specs=[pl.BlockSpec((rows, 3 * RW), lambda b, n: (blk(b, n), 0)),
                  pl.BlockSpec((1, rows, RW), lambda b, n: (d, blk(b, n), 0)),
                  pl.BlockSpec((1, rows, 2 * RW), lambda b, n: (d, blk(b, n), 0)),
                  pl.BlockSpec((1, npair, LANE, LANE), lambda b, n: (b, 0, 0, 0))],
        out_specs=[pl.BlockSpec((rows, RW), lambda b, n: (blk(b, n), 0)),
                   pl.BlockSpec((1, npair, LANE, LANE), lambda b, n: (b, 0, 0, 0))],
        scratch_shapes=[pltpu.VMEM((npair, LANE, LANE), F32)],
        compiler_params=_cp(("arbitrary", "arbitrary")),
        name="rwkv_scan_bwd" if d else "rwkv_scan_fwd",
    )(ss, lw, kb, s0)


def rwkv_scan(ss, lw, kb, s0, B, T):
    y0, f0 = rwkv_scan_dir(ss, lw, kb, s0[0], B, T, 0)
    y1, f1 = rwkv_scan_dir(ss, lw, kb, s0[1], B, T, 1)
    return (y0, y1), (f0, f1)


GLA_KP = 256


GLA_GROUP = 16


def _gla_kernel(pbg_ref, aup_ref, ab_ref, s0_ref, o_ref, sfin_ref, s_scr, *, nsteps, group, reverse):
    n = pl.program_id(1)
    C = CHUNK
    R = group * C

    @pl.when(n == 0)
    def _():
        s_scr[...] = s0_ref[0]

    q = pbg_ref[:, 0:GLA_KP].astype(F32)
    k = pbg_ref[:, GLA_KP:2 * GLA_KP].astype(F32)
    v = pbg_ref[:, 2 * GLA_KP:2 * GLA_KP + RW].astype(F32)
    ad = pbg_ref[:, 2 * GLA_KP + RW:2 * GLA_KP + RW + LANE].astype(F32)
    x = _dot33(ad, aup_ref[0]) + ab_ref[0]
    la = (jnp.minimum(x, 0.0) - jnp.log(1.0 + jnp.exp(-jnp.abs(x)))) * (1.0 / GLA_TEMP)
    chunks = range(group)
    sl = [slice(g * C, (g + 1) * C) for g in chunks]
    rr = lax.broadcasted_iota(I32, (C, C), 0)
    rc = lax.broadcasted_iota(I32, (C, C), 1)
    tri = jnp.where((rc >= rr) if reverse else (rr >= rc), 1.0, 0.0).astype(BF16)
    bcums = [_dot_exact_lhs(tri, la[sl[g]]) for g in chunks]
    last = 0 if reverse else C - 1
    tots = [bc[last:last + 1, :] for bc in bcums]
    bcum = jnp.concatenate(bcums, axis=0)
    tot = jnp.concatenate([jnp.broadcast_to(t, (C, GLA_KP)) for t in tots], axis=0)
    q_in = q * jnp.exp(bcum) * (GLA_K ** -0.5)
    k_in = k * jnp.exp(-bcum)
    k_st = k * jnp.exp(tot - bcum)
    dn = [jnp.exp(t) for t in tots]
    klane = lax.broadcasted_iota(I32, (C, GLA_KP), 1)
    rt = lax.broadcasted_iota(I32, (GLA_HEADS * C, C), 0) & (C - 1)
    ct = lax.broadcasted_iota(I32, (GLA_HEADS * C, C), 1)
    causal = (ct >= rt) if reverse else (rt >= ct)
    vlane = lax.broadcasted_iota(I32, (C, RW), 1)
    sv = lax.broadcasted_iota(I32, (RW, GLA_KP), 0) >> 6
    sk = lax.broadcasted_iota(I32, (RW, GLA_KP), 1) >> 5
    q_rows = [jnp.concatenate([jnp.where((klane >> 5) == h, q_in[sl[g]], 0.0) for h in range(GLA_HEADS)],
                              axis=0).astype(BF16) for g in chunks]
    att = [jnp.where(causal, _bdot(q_rows[g], k_in[sl[g]], NT_DIMS), 0.0) for g in chunks]
    o_rows = [_bdot(att[g], v[sl[g]]) for g in chunks]
    kv = [jnp.where(sv == sk, _bdot(v[sl[g]], k_st[sl[g]], TN_DIMS), 0.0) for g in chunks]
    s = s_scr[...]
    for g in (reversed(chunks) if reverse else chunks):
        o = _bdot(q_in[sl[g]], s, NT_DIMS)
        for h in range(GLA_HEADS):
            o = o + jnp.where((vlane >> 6) == h, o_rows[g][h * C:(h + 1) * C], 0.0)
        o_ref[sl[g], :] = o.astype(o_ref.dtype)
        s = s * dn[g] + kv[g]
    s_scr[...] = s

    @pl.when(n == nsteps - 1)
    def _():
        sfin_ref[0] = s_scr[...]


def gla_scan_dir(pbg, aup, ab, s0, B, T, d):
    nc = T // CHUNK
    group = min(GLA_GROUP, nc)
    nsteps = nc // group
    rows = group * CHUNK
    blk = (lambda b, n: b * nsteps + nsteps - 1 - n) if d else (lambda b, n: b * nsteps + n)
    kern = functools.partial(_gla_kernel, nsteps=nsteps, group=group, reverse=bool(d))
    return pl.pallas_call(
        kern,
        out_shape=[jax.ShapeDtypeStruct((B * T, RW), ACT),
                   jax.ShapeDtypeStruct((B, RW, GLA_KP), F32)],
        grid=(B, nsteps),
        in_specs=[pl.BlockSpec((rows, PBG_W), lambda b, n: (blk(b, n), 0)),
                  pl.BlockSpec((1, LANE, GLA_KP), lambda b, n: (d, 0, 0)),
                  pl.BlockSpec((1, 1, GLA_KP), lambda b, n: (d, 0, 0)),
                  pl.BlockSpec((1, RW, GLA_KP), lambda b, n: (b, 0, 0))],
        out_specs=[pl.BlockSpec((rows, RW), lambda b, n: (blk(b, n), 0)),
                   pl.BlockSpec((1, RW, GLA_KP), lambda b, n: (b, 0, 0))],
        scratch_shapes=[pltpu.VMEM((RW, GLA_KP), F32)],
        compiler_params=_cp(("arbitrary", "arbitrary")),
        name="gla_scan_bwd" if d else "gla_scan_fwd",
    )(pbg, aup, ab, s0)


def gla_scan(pbg, aup, ab, s0, B, T):
    o0, f0 = gla_scan_dir(pbg, aup, ab, s0[0], B, T, 0)
    o1, f1 = gla_scan_dir(pbg, aup, ab, s0[1], B, T, 1)
    return (o0, o1), (f0, f1)


S5_CHUNK = 128
S5_PITCH = 132


def _s5_kernel(u_ref, a_ref, bbd_ref, cbd_ref, s0_ref, y_ref, sfin_ref, x_scr, st_scr, *, nc, nb):
    d = pl.program_id(0)
    n = pl.program_id(1)
    C = u_ref.shape[1]
    half = S5_LANES // 2

    @pl.when(n == 0)
    def _():
        st_scr[...] = s0_ref[0]

    nslab = S5_LANES // LANE
    bu = _bdot(jnp.concatenate([u_ref[b] for b in range(nb)], axis=0), bbd_ref[0])
    for b in range(nb):
        for j in range(nslab):
            x_scr[j, b * S5_PITCH:b * S5_PITCH + C, :] = bu[b * C:(b + 1) * C, j * LANE:(j + 1) * LANE]
    a_re = a_ref[0, :, 0:half]
    a_im = a_ref[0, :, half:S5_LANES]

    def step(i, carry):
        re, im = carry
        t = i + d * (C - 1 - 2 * i)
        rows = pl.ds(t, nb, stride=S5_PITCH)
        bu = jnp.concatenate([x_scr[j, rows, :] for j in range(nslab)], axis=1)
        nre = a_re * re - a_im * im + bu[:, 0:half]
        nim = a_re * im + a_im * re + bu[:, half:S5_LANES]
        for j in range(nslab // 2):
            x_scr[j, rows, :] = nre[:, j * LANE:(j + 1) * LANE]
            x_scr[nslab // 2 + j, rows, :] = nim[:, j * LANE:(j + 1) * LANE]
        return nre, nim

    st = st_scr[...]
    re, im = lax.fori_loop(0, C, step, (st[:, 0:half], st[:, half:S5_LANES]), unroll=4)
    st_scr[...] = jnp.concatenate([re, im], axis=1)
    xs = jnp.concatenate(
        [jnp.concatenate([x_scr[j, b * S5_PITCH:b * S5_PITCH + C, :] for j in range(nslab)], axis=1).astype(BF16)
         for b in range(nb)], axis=0)
    y = _bdot(xs, cbd_ref[0])
    for b in range(nb):
        y_ref[0, b] = y[b * C:(b + 1) * C, :].astype(y_ref.dtype)

    @pl.when(n == nc - 1)
    def _():
        sfin_ref[0] = st_scr[...]


def s5_scan(u3, a_bar, bbd, cbd, s0):
    B, T, _ = u3.shape
    C = min(S5_CHUNK, T)
    assert C + 4 <= S5_PITCH
    nc = T // C
    chunk = lambda d, n: n + d * (nc - 1 - 2 * n)
    kern = functools.partial(_s5_kernel, nc=nc, nb=B)
    return pl.pallas_call(
        kern,
        out_shape=[jax.ShapeDtypeStruct((2, B, T, S5_W), ACT),
                   jax.ShapeDtypeStruct((2, B, S5_LANES), F32)],
        grid=(2, nc),
        in_specs=[pl.BlockSpec((B, C, S5_W), lambda d, n: (0, chunk(d, n), 0)),
                  pl.BlockSpec((1, 1, S5_LANES), lambda d, n: (d, 0, 0)),
                  pl.BlockSpec((1, S5_W, S5_LANES), lambda d, n: (d, 0, 0)),
                  pl.BlockSpec((1, S5_LANES, S5_W), lambda d, n: (d, 0, 0)),
                  pl.BlockSpec((1, B, S5_LANES), lambda d, n: (d, 0, 0))],
        out_specs=[pl.BlockSpec((1, B, C, S5_W), lambda d, n: (d, 0, chunk(d, n), 0)),
                   pl.BlockSpec((1, B, S5_LANES), lambda d, n: (d, 0, 0))],
        scratch_shapes=[pltpu.VMEM((S5_LANES // LANE, B * S5_PITCH, LANE), F32),
                        pltpu.VMEM((B, S5_LANES), F32)],
        compiler_params=_cp(("arbitrary", "arbitrary")),
        name="s5_scan",
    )(u3, a_bar, bbd, cbd, s0)


ROWT = 4
U32 = jnp.uint32


def _pack_bf16_pair(a, b):
    au = lax.bitcast_convert_type(a.astype(BF16).astype(F32), U32)
    bu = lax.bitcast_convert_type(b.astype(BF16).astype(F32), U32)
    return (au >> 16) | bu


def _unpack_bf16_pair(w):
    return (lax.bitcast_convert_type(w << 16, F32), lax.bitcast_convert_type(w & jnp.uint32(0xFFFF0000), F32))


def _tile_rows_store(ref, val):
    tm = val.shape[0]
    assert val.shape[1] == 2 * ROWT * LANE
    for s in range(ROWT):
        ref[pl.ds(s, tm, stride=ROWT), :] = _pack_bf16_pair(val[:, 2 * s * LANE:(2 * s + 1) * LANE],
                                                            val[:, (2 * s + 1) * LANE:(2 * s + 2) * LANE])


def _tile_rows_load(ref, tm, base=0):
    parts = []
    for s in range(ROWT):
        parts.extend(_unpack_bf16_pair(ref[pl.ds(base + s, tm, stride=ROWT), :]))
    return jnp.concatenate(parts, axis=1)


def _mix_out_kernel(y0_ref, y1_ref, gb_ref, o0_ref, o1_ref, gg_ref, yd_ref, u_ref, x_ref, mod_ref,
                    hb_ref, lng_ref, lnb_ref, gng_ref, s5d_ref, gluw_ref, glub_ref, wout_ref, l1g_ref, l1b_ref,
                    *rest, alpha):
    x1_ref, h2_ref, h2t_ref = rest[-3:]
    hb = hb_ref[...]
    inv = 1.0 / HEAD
    f32 = lambda ref_val: ref_val.astype(F32)
    y = f32(y0_ref[...]) + f32(y1_ref[...])
    yc = y - _dot_split_rhs(y, hb) * inv
    var = _bdot(yc * yc, hb) * inv
    gn = yc * lax.rsqrt(var + RWKV_GN_EPS) * lng_ref[...] + lnb_ref[...]
    y_a = (gn + f32(gb_ref[:, RW:2 * RW])) * f32(gb_ref[:, 0:RW])
    o = f32(o0_ref[...]) + f32(o1_ref[...])
    o = o * lax.rsqrt(_bdot(o * o, hb) * inv + 1e-6) * gng_ref[...]
    y_b = o * _silu(f32(gg_ref[...]))
    c = s5d_ref[...] * f32(u_ref[...]) + f32(yd_ref[0]) + f32(yd_ref[1])
    c = 0.5 * c * (1.0 + jnp.tanh(math.sqrt(2.0 / math.pi) * (c + 0.044715 * (c * c * c))))
    y_c = c * _sigmoid(_bdot(c, gluw_ref[...]) + glub_ref[...])
    y_mix = (_bdot(y_a, wout_ref[0:RW, :]) + _bdot(y_b, wout_ref[RW:2 * RW, :])
             + _bdot(y_c, wout_ref[2 * RW:2 * RW + S5_W, :]))
    x1 = _layer_norm(alpha * x_ref[...] + mod_ref[0, 2:3, :] * y_mix) * l1g_ref[...] + l1b_ref[...]
    x1_ref[...] = x1
    h2 = _layer_norm(x1) * (1.0 + mod_ref[0, 4:5, :]) + mod_ref[0, 3:4, :]
    h2_ref[...] = h2
    _tile_rows_store(h2t_ref, h2)


def mix_out(y01, gb, o01, gg, yd, u, stream, mod3, wts, row_of_tile, tm, alpha, out_rows=None, out_row0=0, into=None):
    x2, row0, N = stream
    D = x2.shape[1]
    out_rows = N if out_rows is None else out_rows
    assert row0 % tm == 0 and out_row0 % tm == 0 and N % tm == 0
    t0, o0 = row0 // tm, out_row0 // tm
    full = lambda a: pl.BlockSpec(a.shape, lambda i: (0,) * a.ndim)
    kern = functools.partial(_mix_out_kernel, alpha=alpha)
    ins = [y01[0], y01[1], gb, o01[0], o01[1], gg, yd, u, x2, mod3, *wts]
    extra = [] if into is None else list(into)
    return pl.pallas_call(
        kern,
        out_shape=[jax.ShapeDtypeStruct((out_rows, D), F32),
                   jax.ShapeDtypeStruct((out_rows, D), F32),
                   jax.ShapeDtypeStruct((out_rows * ROWT, LANE), U32)],
        grid=(N // tm,),
        in_specs=[pl.BlockSpec((tm, RW), lambda i: (i, 0)),
                  pl.BlockSpec((tm, RW), lambda i: (i, 0)),
                  pl.BlockSpec((tm, 2 * RW), lambda i: (i, 0)),
                  pl.BlockSpec((tm, RW), lambda i: (i, 0)),
                  pl.BlockSpec((tm, RW), lambda i: (i, 0)),
                  pl.BlockSpec((tm, RW), lambda i: (i, 0)),
                  pl.BlockSpec((2, tm, S5_W), lambda i: (0, i, 0)),
                  pl.BlockSpec((tm, S5_W), lambda i: (i, 0)),
                  pl.BlockSpec((tm, D), lambda i: (i + t0, 0)),
                  pl.BlockSpec((1, 6, D), lambda i: (row_of_tile(i), 0, 0))] + [full(a) for a in wts]
                 + [pl.BlockSpec(memory_space=pl.ANY)] * len(extra),
        out_specs=[pl.BlockSpec((tm, D), lambda i: (i + o0, 0)),
                   pl.BlockSpec((tm, D), lambda i: (i + o0, 0)),
                   pl.BlockSpec((tm * ROWT, LANE), lambda i: (i + o0, 0))],
        input_output_aliases={len(ins) + k: k for k in range(len(extra))},
        compiler_params=_cp(("arbitrary",)),
        name="mix_out",
    )(*ins, *extra)


def _first_max(x, idx, big):
    m = jnp.max(x, axis=0, keepdims=True)
    first = jnp.min(jnp.where(x == m, idx, big), axis=0, keepdims=True)
    return m, first


def _router_kernel(h_ref, rwt_ref, bias_ref, e_ref, w_ref):
    tm = h_ref.shape[0]
    gsz = N_EXPERTS // N_EGROUPS
    ninf = -jnp.inf
    s = _sigmoid(_dot33(rwt_ref[...], h_ref[...], NT_DIMS))
    ssel = s + bias_ref[:, 0:1]
    gi = lax.broadcasted_iota(I32, (gsz, tm), 0)
    gscore = []
    for g in range(N_EGROUPS):
        xg = ssel[g * gsz:(g + 1) * gsz, :]
        m1, i1 = _first_max(xg, gi, gsz)
        m2 = jnp.max(jnp.where(gi == i1, ninf, xg), axis=0, keepdims=True)
        gscore.append(m1 + m2)
    cur = jnp.concatenate(gscore, axis=0)
    gidx = lax.broadcasted_iota(I32, (N_EGROUPS, tm), 0)
    picked = jnp.zeros((N_EGROUPS, tm), F32)
    for _ in range(TOPK_GROUPS):
        _, first = _first_max(cur, gidx, N_EGROUPS)
        hit = gidx == first
        picked = jnp.where(hit, 1.0, picked)
        cur = jnp.where(hit, ninf, cur)
    x = jnp.concatenate(
        [jnp.where(picked[g:g + 1, :] > 0.5, ssel[g * gsz:(g + 1) * gsz, :], ninf) for g in range(N_EGROUPS)], axis=0)
    ei = lax.broadcasted_iota(I32, (N_EXPERTS, tm), 0)
    idxs, ws = [], []
    for _ in range(TOP_K):
        _, first = _first_max(x, ei, N_EXPERTS)
        hit = ei == first
        idxs.append(first)
        ws.append(jnp.sum(jnp.where(hit, s, 0.0), axis=0, keepdims=True))
        x = jnp.where(hit, ninf, x)
    w = jnp.concatenate(ws, axis=0)
    e_ref[...] = jnp.concatenate(idxs, axis=0)
    w_ref[...] = w / jnp.sum(w, axis=0, keepdims=True) * ROUTE_SCALE


def moe_router(h2, rwt, bias_b, tm):
    N, D = h2.shape
    return pl.pallas_call(
        _router_kernel,
        out_shape=[jax.ShapeDtypeStruct((TOP_K, N), I32), jax.ShapeDtypeStruct((TOP_K, N), F32)],
        grid=(N // tm,),
        in_specs=[pl.BlockSpec((tm, D), lambda i: (i, 0)),
                  pl.BlockSpec((N_EXPERTS, D), lambda i: (0, 0)),
                  pl.BlockSpec((N_EXPERTS, LANE), lambda i: (0, 0))],
        out_specs=[pl.BlockSpec((TOP_K, tm), lambda i: (0, i)),
                   pl.BlockSpec((TOP_K, tm), lambda i: (0, i))],
        compiler_params=_cp(("arbitrary",)),
        name="moe_router",
    )(h2, rwt, bias_b)


def _moe_count_kernel(e_ref, pstart_ref, plan_ref, blk_ref, cnt_scr, *, nt, nbp):
    i = pl.program_id(0)
    tp = e_ref.shape[1]

    @pl.when(i == 0)
    def _():
        cnt_scr[...] = jnp.zeros_like(cnt_scr)

    ei = lax.broadcasted_iota(I32, (N_EXPERTS, tp), 0)
    acc = jnp.zeros((N_EXPERTS, 1), F32)
    for k in range(TOP_K):
        acc = acc + jnp.sum(jnp.where(ei == e_ref[k:k + 1, :], 1.0, 0.0), axis=1, keepdims=True)
    cnt_scr[...] = cnt_scr[...] + acc

    @pl.when(i == nt - 1)
    def _():
        cnt = cnt_scr[...].astype(I32)
        padded = (cnt + (MOE_BLK - 1)) & (-MOE_BLK)
        r = lax.broadcasted_iota(I32, (N_EXPERTS, N_EXPERTS), 0)
        c = lax.broadcasted_iota(I32, (N_EXPERTS, N_EXPERTS), 1)
        tri = jnp.where(c <= r, 1.0, 0.0).astype(BF16)
        padded_b = jnp.broadcast_to(padded.astype(F32), (N_EXPERTS, LANE))
        p_end = _dot_exact_lhs(tri, padded_b)
        pstart = p_end - padded_b
        pstart_ref[...] = pstart.astype(I32)
        diag = r == c
        ps_row = jnp.sum(jnp.where(diag, pstart[:, 0:1], 0.0), axis=0, keepdims=True)
        cnt_row = jnp.sum(jnp.where(diag, cnt_scr[...], 0.0), axis=0, keepdims=True)
        plan_ref[...] = jnp.concatenate([ps_row, cnt_row, jnp.zeros((SUB - 2, N_EXPERTS), F32)], axis=0).astype(I32)
        lim =(lax.broadcasted_iota(I32, (N_EXPERTS, nbp), 1) * MOE_BLK).astype(F32)
        be = jnp.sum(jnp.where(p_end[:, 0:1] <= lim, 1.0, 0.0), axis=0, keepdims=True)
        be = jnp.minimum(be, N_EXPERTS - 1.0)
        nused = jnp.max(p_end[:, 0:1], axis=0, keepdims=True) * (1.0 / MOE_BLK)
        blk_ref[...] = jnp.concatenate([jnp.broadcast_to(be, (SUB // 2, nbp)),
                                        jnp.broadcast_to(nused, (SUB // 2, nbp))], axis=0).astype(I32)


def moe_counts(eidx, tp, nbp):
    N = eidx.shape[1]
    nt = N // tp
    kern = functools.partial(_moe_count_kernel, nt=nt, nbp=nbp)
    return pl.pallas_call(
        kern,
        out_shape=[jax.ShapeDtypeStruct((N_EXPERTS, LANE), I32),
                   jax.ShapeDtypeStruct((SUB, N_EXPERTS), I32),
                   jax.ShapeDtypeStruct((SUB, nbp), I32)],
        grid=(nt,),
        in_specs=[pl.BlockSpec((TOP_K, tp), lambda i: (0, i))],
        out_specs=[pl.BlockSpec((N_EXPERTS, LANE), lambda i: (0, 0)),
                   pl.BlockSpec((SUB, N_EXPERTS), lambda i: (0, 0)),
                   pl.BlockSpec((SUB, nbp), lambda i: (0, 0))],
        scratch_shapes=[pltpu.VMEM((N_EXPERTS, 1), F32)],
        compiler_params=_cp(("arbitrary",)),
        name="moe_counts",
    )(eidx)


def _moe_dest_kernel(e_ref, pstart_ref, dest_ref, base_scr):
    i = pl.program_id(0)
    tp = e_ref.shape[1]

    @pl.when(i == 0)
    def _():
        base_scr[...] = pstart_ref[:, 0:1].astype(F32)

    ei = lax.broadcasted_iota(I32, (N_EXPERTS, tp), 0)
    r = lax.broadcasted_iota(I32, (tp, tp), 0)
    c = lax.broadcasted_iota(I32, (tp, tp), 1)
    tri = jnp.where(r <= c, 1.0, 0.0).astype(BF16)
    base = base_scr[...]
    rows = []
    for k in range(TOP_K):
        hit = ei == e_ref[k:k + 1, :]
        oh = jnp.where(hit, 1.0, 0.0)
        cum = _bdot(oh, tri)
        rows.append(jnp.sum(jnp.where(hit, cum - 1.0 + base, 0.0), axis=0, keepdims=True))
        base = base + cum[:, tp - 1:tp]
    base_scr[...] = base
    dest_ref[...] = jnp.concatenate(rows, axis=0).astype(I32)


def moe_dest(eidx, pstart, tp):
    N = eidx.shape[1]
    return pl.pallas_call(
        _moe_dest_kernel,
        out_shape=jax.ShapeDtypeStruct((TOP_K, N), I32),
        grid=(N // tp,),
        in_specs=[pl.BlockSpec((TOP_K, tp), lambda i: (0, i)),
                  pl.BlockSpec((N_EXPERTS, LANE), lambda i: (0, 0))],
        out_specs=pl.BlockSpec((TOP_K, tp), lambda i: (0, i)),
        scratch_shapes=[pltpu.VMEM((N_EXPERTS, 1), F32)],
        compiler_params=_cp(("arbitrary",)),
        name="moe_dest",
    )(eidx, pstart)


def _tile_at(ref, token):
    return ref.at[pl.ds(pl.multiple_of(token * ROWT, ROWT), ROWT), :]


SCAT_RING = 3


def _moe_scatter_kernel(dest_ref, h_ref, xs_hbm, ring, sem, *, nt):
    s = pl.program_id(0)
    tp = dest_ref.shape[1]
    slot = lax.rem(s, SCAT_RING)
    total = xs_hbm.at[pl.ds(0, tp * TOP_K * ROWT), :]

    def slot_wait(sl):
        pltpu.make_async_copy(total, total, sem.at[sl]).wait()

    @pl.when(s >= SCAT_RING)
    def _():
        slot_wait(slot)

    ring[slot] = h_ref[...]

    def issue(j, carry):
        for k in range(TOP_K):
            pltpu.make_async_copy(_tile_at(ring.at[slot], j), _tile_at(xs_hbm, dest_ref[k, j]),
                                  sem.at[slot]).start(priority=k % 2)
        return carry

    lax.fori_loop(0, tp, issue, 0)

    @pl.when(s == nt - 1)
    def _():
        for back in range(SCAT_RING - 1, -1, -1):
            if nt - 1 - back >= 0:
                slot_wait((nt - 1 - back) % SCAT_RING)


def moe_scatter(dest, h2t, n_slots, tp):
    N = dest.shape[1]
    nt = N // tp
    return pl.pallas_call(
        functools.partial(_moe_scatter_kernel, nt=nt),
        out_shape=jax.ShapeDtypeStruct((n_slots * ROWT, LANE), U32),
        grid=(nt,),
        in_specs=[pl.BlockSpec((TOP_K, tp), lambda i: (0, i), memory_space=pltpu.SMEM),
                  pl.BlockSpec((tp * ROWT, LANE), lambda i: (i, 0))],
        out_specs=pl.BlockSpec(memory_space=pl.ANY),
        scratch_shapes=[pltpu.VMEM((SCAT_RING, tp * ROWT, LANE), U32), pltpu.SemaphoreType.DMA((SCAT_RING,))],
        compiler_params=_cp(("arbitrary",)),
        name="moe_scatter",
    )(dest, h2t)


PAD_BITS = tuple(1 << b for b in reversed(range(MOE_BLK.bit_length() - 1)))


def _moe_padfill_kernel(ps_ref, xs_in, xs_hbm, zero_scr, sem):
    del xs_in
    zero_scr[...] = jnp.zeros_like(zero_scr)

    def pad_copies(e, wait):
        cnt = ps_ref[1, e]
        npad = ((cnt + (MOE_BLK - 1)) & (-MOE_BLK)) - cnt
        off = ps_ref[0, e] + cnt
        for bit in PAD_BITS:
            @pl.when((npad & bit) != 0)
            def _():
                cp = pltpu.make_async_copy(zero_scr.at[pl.ds(0, bit * ROWT), :],
                                           xs_hbm.at[pl.ds(pl.multiple_of(off * ROWT, ROWT), bit * ROWT), :], sem)
                if wait:
                    cp.wait()
                else:
                    cp.start()
            off = off + (npad & bit)

    def issue(e, carry):
        pad_copies(e, False)
        return carry

    def drain(e, carry):
        pad_copies(e, True)
        return carry

    lax.fori_loop(0, N_EXPERTS, issue, 0)
    lax.fori_loop(0, N_EXPERTS, drain, 0)


def moe_padfill(pstart, xs):
    return pl.pallas_call(
        _moe_padfill_kernel,
        out_shape=jax.ShapeDtypeStruct(xs.shape, xs.dtype),
        grid=(1,),
        in_specs=[pl.BlockSpec(memory_space=pltpu.SMEM),
                  pl.BlockSpec(memory_space=pl.ANY)],
        out_specs=pl.BlockSpec(memory_space=pl.ANY),
        scratch_shapes=[pltpu.VMEM((PAD_BITS[0] * ROWT, LANE), U32), pltpu.SemaphoreType.DMA(())],
        input_output_aliases={1: 0},
        compiler_params=_cp(("arbitrary",)),
        name="moe_padfill",
    )(pstart, xs)


EXP_NX = 8
EXP_PF = 4
EXP_NY = 4


def _experts_kernel(plan_ref, w13_ref, w2_ref, xs_hbm, y_hbm, w13_bf, w2_bf, x_buf, y_buf, sem_in, sem_out):
    e = pl.program_id(0)
    ne = pl.num_programs(0)
    ff = w2_bf.shape[0]
    rows = MOE_BLK * ROWT
    shift = MOE_BLK.bit_length() - 1
    blocks_of = lambda c: lax.shift_right_logical(c + (MOE_BLK - 1), shift)
    nblk = blocks_of(plan_ref[1, e])
    first = lax.shift_right_logical(plan_ref[0, e], shift)
    nused = lax.shift_right_logical(plan_ref[0, ne - 1], shift) + blocks_of(plan_ref[1, ne - 1])

    def window(ref, g):
        return ref.at[pl.ds(pl.multiple_of(g * rows, rows), rows), :]

    def x_copy(g):
        return pltpu.make_async_copy(window(xs_hbm, g), x_buf.at[g & (EXP_NX - 1)], sem_in.at[g & (EXP_NX - 1)])

    def y_copy(g):
        return pltpu.make_async_copy(y_buf.at[g & (EXP_NY - 1)], window(y_hbm, g), sem_out.at[g & (EXP_NY - 1)])

    @pl.when(e == 0)
    def _():
        for g in range(EXP_PF):
            @pl.when(g < nused)
            def _():
                x_copy(g).start()

    def process(gs):
        for g in gs:
            x_copy(g).wait()

            @pl.when(g + EXP_PF < nused)
            def _():
                x_copy(g + EXP_PF).start()

            @pl.when(g >= EXP_NY)
            def _():
                y_copy(g - EXP_NY).wait()

        x = jnp.concatenate([_tile_rows_load(x_buf.at[g & (EXP_NX - 1)], MOE_BLK).astype(BF16) for g in gs], axis=0)
        h = jnp.dot(x, w13_bf[...], preferred_element_type=F32)
        act = (_silu(h[:, 0:ff]) * h[:, ff:2 * ff]).astype(BF16)
        y = jnp.dot(act, w2_bf[...], preferred_element_type=F32)
        for i, g in enumerate(gs):
            _tile_rows_store(y_buf.at[g & (EXP_NY - 1)], y[i * MOE_BLK:(i + 1) * MOE_BLK])
            y_copy(g).start()

    @pl.when(nblk > 0)
    def _():
        w13_bf[...] = w13_ref[0, 0].astype(BF16)
        w2_bf[...] = w2_ref[0, 0].astype(BF16)
        npair = lax.shift_right_logical(nblk, 1)

        def body(j, carry):
            g = first + 2 * j
            process([g, g + 1])
            return carry

        lax.fori_loop(0, npair, body, 0)

        @pl.when((nblk & 1) == 1)
        def _():
            process([first + nblk - 1])

    @pl.when(e == ne - 1)
    def _():
        for back in range(EXP_NY, 0, -1):
            @pl.when(nused >= back)
            def _():
                y_copy(nused - back).wait()


def moe_experts(plan, xs, w13, w2, layer):
    _, E, D, F2 = w13.shape
    rows = MOE_BLK * ROWT
    grid_spec = pltpu.PrefetchScalarGridSpec(
        num_scalar_prefetch=1,
        grid=(E,),
        in_specs=[pl.BlockSpec((1, 1, D, F2), lambda e, p: (layer, e, 0, 0)),
                  pl.BlockSpec((1, 1, F2 // 2, D), lambda e, p: (layer, e, 0, 0)),
                  pl.BlockSpec(memory_space=pl.ANY)],
        out_specs=pl.BlockSpec(memory_space=pl.ANY),
        scratch_shapes=[pltpu.VMEM((D, F2), BF16), pltpu.VMEM((F2 // 2, D), BF16),
                        pltpu.VMEM((EXP_NX, rows, LANE), U32), pltpu.VMEM((EXP_NY, rows, LANE), U32),
                        pltpu.SemaphoreType.DMA((EXP_NX,)), pltpu.SemaphoreType.DMA((EXP_NY,))])
    return pl.pallas_call(
        _experts_kernel,
        out_shape=jax.ShapeDtypeStruct(xs.shape, U32),
        grid_spec=grid_spec,
        compiler_params=_cp(("arbitrary",)),
        name="moe_experts",
    )(plan, w13, w2, xs)


def _moe_combine_kernel(dest_ref, wt_ref, h_ref, x_ref, mod_ref, s13_ref, s2_ref, l2g_ref, l2b_ref, y_hbm,
                        o_ref, g_scr, sem, *, alpha, nt):
    s = pl.program_id(0)
    tm = h_ref.shape[0]
    ff = s2_ref.shape[0]
    rows = tm * ROWT
    slot_g = s % 2
    slot_c = 1 - slot_g

    def slot_copy(slot):
        return pltpu.make_async_copy(y_hbm.at[pl.ds(0, TOP_K * rows), :], g_scr.at[slot], sem.at[slot])

    @pl.when(s == 0)
    def _():
        g_scr[1] = jnp.zeros((TOP_K * rows, LANE), U32)

    @pl.when(s > 0)
    def _():
        slot_copy(slot_c).wait()

    for j in range(tm):
        for k in range(TOP_K):
            pltpu.make_async_copy(_tile_at(y_hbm, dest_ref[k, j]),
                                  g_scr.at[slot_g, pl.ds(k * rows + j * ROWT, ROWT), :],
                                  sem.at[slot_g]).start(priority=k % 2)
    hs = _bdot(h_ref[...], s13_ref[...])
    f = _bdot(_silu(hs[:, 0:ff]) * hs[:, ff:2 * ff], s2_ref[...])
    for k in range(TOP_K):
        f = f + wt_ref[:, k:k + 1] * _tile_rows_load(g_scr.at[slot_c], tm, base=k * rows)
    o_ref[...] = (_layer_norm(alpha * x_ref[...] + mod_ref[0, 5:6, :] * f) * l2g_ref[...] + l2b_ref[...])

    @pl.when(s == nt)
    def _():
        slot_copy(slot_g).wait()


def moe_combine(dest, wt_t, h2, x1, mod3, wts, y, row_of_tile, tm, alpha):
    N, D = h2.shape
    nt = N // tm
    full = lambda a: pl.BlockSpec(a.shape, lambda s: (0,) * a.ndim)
    kern = functools.partial(_moe_combine_kernel, alpha=alpha, nt=nt)
    nxt = lambda s: jnp.minimum(s, nt - 1)
    cur = lambda s: jnp.maximum(s - 1, 0)
    return pl.pallas_call(
        kern,
        out_shape=jax.ShapeDtypeStruct((N, D), F32),
        grid=(nt + 1,),
        in_specs=[pl.BlockSpec((TOP_K, tm), lambda s: (0, nxt(s)), memory_space=pltpu.SMEM),
                  pl.BlockSpec((tm, TOP_K), lambda s: (cur(s), 0)),
                  pl.BlockSpec((tm, D), lambda s: (cur(s), 0)),
                  pl.BlockSpec((tm, D), lambda s: (cur(s), 0)),
                  pl.BlockSpec((1, 6, D), lambda s: (row_of_tile(cur(s)), 0, 0))]
                 + [full(a) for a in wts] + [pl.BlockSpec(memory_space=pl.ANY)],
        out_specs=pl.BlockSpec((tm, D), lambda s: (cur(s), 0)),
        scratch_shapes=[pltpu.VMEM((2, TOP_K * tm * ROWT, LANE), U32), pltpu.SemaphoreType.DMA((2,))],
        compiler_params=_cp(("arbitrary",)),
        name="moe_combine",
    )(dest, wt_t, h2, x1, mod3, *wts, y)


def moe_ffn(h2, h2t, x1, mod3, mp, row_of_tile, tm, alpha):
    N = h2.shape[0]
    nb_total = (N * TOP_K + MOE_BLK - 1) // MOE_BLK + N_EXPERTS
    nbp = ((nb_total + LANE - 1) // LANE) * LANE
    tp = _tile(N, 256)
    eidx, wts = moe_router(h2, mp['rwt'], mp['bias'], tp)
    pstart, plan, blk = moe_counts(eidx, tp, nbp)
    dest = moe_dest(eidx, pstart, tp)
    xs = moe_scatter(dest, h2t, nb_total * MOE_BLK, tp)
    xs = moe_padfill(plan, xs)
    y = moe_experts(plan, xs, mp['w13'], mp['w2'], mp['layer'])
    return moe_combine(dest, wts.T, h2, x1, mod3, mp['comb'], y, row_of_tile, tm, alpha)


def token_mixers(stream, B, T, W, mod3, row_of_tile, lp, states, tm):
    pa, pbg, gg, u = inproj(stream, mod3, lp['w_in'], row_of_tile, tm)
    pac = short_conv(pa.reshape(B, T, PA_W), lp['taps'], W).reshape(B * T, PA_W)
    ss, lw, kb, gb = rwkv_prep(pac, lp['prep'], tm)
    s_rwkv, s_gla, s_s5 = states
    y2, f_rwkv = rwkv_scan(ss, lw, kb, s_rwkv, B, T)
    o2, f_gla = gla_scan(pbg, lp['gla_aup'], lp['gla_ab'], s_gla, B, T)
    yd, f_s5 = s5_scan(u.reshape(B, T, S5_W), lp['s5_a'], lp['s5_bbd'], lp['s5_cbd'], s_s5)
    return (y2, gb, o2, gg, yd.reshape(2, B * T, S5_W), u), (f_rwkv, f_gla, f_s5)


def zero_states(B):
    return ((jnp.zeros((B, RW // LANE, LANE, LANE), F32),) * 2,
            (jnp.zeros((B, RW, GLA_KP), F32),) * 2,
            jnp.zeros((2, B, S5_LANES), F32))


def _inproj_columns():
    r_cols = 3 * RW + 2 * DECAY_RANK + 2 * ICLR_RANK + GATE_RANK
    kd = GLA_HEADS * GLA_K
    gq, gk, gv = r_cols, r_cols + kd, r_cols + 2 * kd
    gg = gv + RW
    gad = gg + RW
    pc = gad + 2 * GLA_RANK
    z = lambda n: [-1] * n
    cols = list(range(0, r_cols)) + z(PA_W - r_cols)
    cols += list(range(gq, gq + kd)) + z(256 - kd)
    cols += list(range(gk, gk + kd)) + z(256 - kd)
    cols += list(range(gv, gv + RW))
    cols += list(range(gad, gad + 2 * GLA_RANK)) + z(LANE - 2 * GLA_RANK)
    cols += list(range(gg, gg + RW))
    cols += list(range(pc, pc + S5_W))
    assert len(cols) == NP_W
    return np.asarray(cols, np.int32)


def _head_block_ones():
    h = np.arange(RW) // HEAD
    return jnp.asarray(h[:, None] == h[None, :], BF16)


def _layer_params(l, p):
    D = p['w_in'].shape[1]
    cols = _inproj_columns()
    w_in = jnp.concatenate([p['w_in'][l], jnp.zeros((D, 1), F32)], axis=1)
    w_in = jnp.take(w_in, jnp.asarray(np.where(cols < 0, w_in.shape[1] - 1, cols)), axis=1).astype(BF16)
    taps = p['rwkv_conv'][l].reshape(9, -1)
    taps = jnp.concatenate([taps, jnp.zeros((9, PA_W - taps.shape[1]), F32)], axis=1)
    wup = jnp.zeros((LANE, 2 * RW), F32)
    aup = jnp.zeros((LANE, 2 * RW), F32)
    for d in range(2):
        wup = wup.at[d * DECAY_RANK:(d + 1) * DECAY_RANK, d * RW:(d + 1) * RW].set(p['rwkv_w_up'][l, d])
        o = 2 * DECAY_RANK + d * ICLR_RANK
        aup = aup.at[o:o + ICLR_RANK, d * RW:(d + 1) * RW].set(p['rwkv_a_up'][l, d])
    gup = jnp.zeros((LANE, RW), F32).at[0:GATE_RANK].set(p['rwkv_g_up'][l])
    row = lambda a: a.reshape(1, -1)
    prep = (wup, aup, gup, row(p['rwkv_w0'][l]), row(p['rwkv_a0'][l]), row(p['rwkv_k_k'][l]),
            row(p['rwkv_k_a'][l]), row(p['rwkv_r_k'][l]), _head_block_ones())
    kd = GLA_HEADS * GLA_K
    gla_aup = jnp.zeros((2, LANE, GLA_KP), F32)
    for d in range(2):
        gla_aup = gla_aup.at[d, d * GLA_RANK:(d + 1) * GLA_RANK, 0:kd].set(p['gla_a_up'][l, d])
    gla_ab = jnp.zeros((2, 1, GLA_KP), F32).at[:, 0, 0:kd].set(p['gla_a_bias'][l])
    lam_re, lam_im = p['s5_lam_re'][l], p['s5_lam_im'][l]
    dt = jnp.exp(p['s5_log_dt'][l])[:, :, None]
    zr, zi = lam_re[:, None, :] * dt, lam_im[:, None, :] * dt
    mag = jnp.exp(zr)
    ab_r, ab_i = mag * jnp.cos(zi), mag * jnp.sin(zi)
    den = (lam_re * lam_re + lam_im * lam_im)[:, None, :]
    f_r = ((ab_r - 1) * lam_re[:, None, :] + ab_i * lam_im[:, None, :]) / den
    f_i = (ab_i * lam_re[:, None, :] - (ab_r - 1) * lam_im[:, None, :]) / den
    b_re, b_im = p['s5_b_re'][l], p['s5_b_im'][l]
    bb_r = f_r[..., None] * b_re - f_i[..., None] * b_im
    bb_i = f_r[..., None] * b_im + f_i[..., None] * b_re
    eye_g = jnp.eye(S5_GROUPS, dtype=F32)
    half = S5_LANES // 2

    def in_blockdiag(bb):
        return jnp.einsum('dgpc,gh->dgchp', bb, eye_g).reshape(2, S5_W, half)

    def out_blockdiag(cc):
        return jnp.einsum('dgcp,gh->dgphc', cc, eye_g).reshape(2, half, S5_W)

    s5_bbd = jnp.concatenate([in_blockdiag(bb_r), in_blockdiag(bb_i)], axis=2).astype(BF16)
    s5_cbd = jnp.concatenate([out_blockdiag(p['s5_c_re'][l]), -out_blockdiag(p['s5_c_im'][l])], axis=1).astype(BF16)
    s5_a = jnp.concatenate([ab_r.reshape(2, 1, half), ab_i.reshape(2, 1, half)], axis=2)
    mix = (_head_block_ones(), row(p['rwkv_ln_g'][l]), row(p['rwkv_ln_b'][l]),
           row(jnp.tile(p['gla_norm_g'][l], GLA_HEADS)), row(p['s5_d'][l]),
           p['s5_glu_w'][l].astype(BF16), row(p['s5_glu_b'][l]), p['w_out'][l].astype(BF16),
           row(p['ln1_g'][l]), row(p['ln1_b'][l]))
    moe = dict(rwt=p['router_w'][l].T,
               bias=jnp.broadcast_to(p['router_bias'][l][:, None], (N_EXPERTS, LANE)),
               w13=p['exp_w13'], w2=p['exp_w2'], layer=l,
               comb=(p['sh_w13'][l].astype(BF16), p['sh_w2'][l].astype(BF16), row(p['ln2_g'][l]), row(p['ln2_b'][l])))
    return dict(w_in=w_in, taps=taps, prep=prep, gla_aup=gla_aup, gla_ab=gla_ab,
                s5_a=s5_a, s5_bbd=s5_bbd, s5_cbd=s5_cbd, mix=mix, moe=moe)


_ARG_NAMES = ('x', 'c', 'ctx', 'c_ctx', 'w_mod', 'b_mod', 'w_in', 'rwkv_conv', 'rwkv_w0', 'rwkv_w_up', 'rwkv_a0',
              'rwkv_a_up', 'rwkv_g_up', 'rwkv_k_k', 'rwkv_k_a', 'rwkv_r_k', 'rwkv_ln_g', 'rwkv_ln_b', 'gla_a_up',
              'gla_a_bias', 'gla_norm_g', 's5_lam_re', 's5_lam_im', 's5_log_dt', 's5_b_re', 's5_b_im', 's5_c_re',
              's5_c_im', 's5_d', 's5_glu_w', 's5_glu_b', 'w_out', 'ln1_g', 'ln1_b', 'router_w', 'router_bias',
              'exp_w13', 'exp_w2', 'sh_w13', 'sh_w2', 'ln2_g', 'ln2_b')


def _tile(n, pref):
    t = pref
    while n % t:
        t //= 2
    return t


def kernel(x, c, ctx, c_ctx, w_mod, b_mod, w_in, rwkv_conv, rwkv_w0, rwkv_w_up, rwkv_a0, rwkv_a_up, rwkv_g_up,
           rwkv_k_k, rwkv_k_a, rwkv_r_k, rwkv_ln_g, rwkv_ln_b, gla_a_up, gla_a_bias, gla_norm_g, s5_lam_re,
           s5_lam_im, s5_log_dt, s5_b_re, s5_b_im, s5_c_re, s5_c_im, s5_d, s5_glu_w, s5_glu_b, w_out, ln1_g,
           ln1_b, router_w, router_bias, exp_w13, exp_w2, sh_w13, sh_w2, ln2_g, ln2_b):
    p = dict(zip(_ARG_NAMES, (x, c, ctx, c_ctx, w_mod, b_mod, w_in, rwkv_conv, rwkv_w0, rwkv_w_up, rwkv_a0,
                              rwkv_a_up, rwkv_g_up, rwkv_k_k, rwkv_k_a, rwkv_r_k, rwkv_ln_g, rwkv_ln_b, gla_a_up,
                              gla_a_bias, gla_norm_g, s5_lam_re, s5_lam_im, s5_log_dt, s5_b_re, s5_b_im, s5_c_re,
                              s5_c_im, s5_d, s5_glu_w, s5_glu_b, w_out, ln1_g, ln1_b, router_w, router_bias,
                              exp_w13, exp_w2, sh_w13, sh_w2, ln2_g, ln2_b)))
    B, T, D = x.shape
    TC = ctx.shape[1]
    L = w_mod.shape[0]
    alpha = (2 * L) ** 0.25
    n_lat, n_ctx = B * T, B * TC
    R = ((B + 1 + SUB - 1) // SUB) * SUB
    cc = jnp.zeros((R, D), F32).at[0:B].set(c).at[B].set(c_ctx)
    mod = mod_table(cc, w_mod, b_mod)
    tm = _tile(T, 512)
    tmc = _tile(n_ctx, 512)
    tmm = min(_tile(T, 256), _tile(n_ctx, 256))
    lat_row = lambda i: (i * tm) // T
    ctx_row = lambda i: B
    lat = (x.reshape(n_lat, D), 0, n_lat)
    con = (ctx.reshape(n_ctx, D), 0, n_ctx)
    for l in range(L):
        last = l == L - 1
        lp = _layer_params(l, p)
        mod3 = mod[l].reshape(R, 6, D)
        outs_c, st_c = token_mixers(con, B, TC, TC, mod3, ctx_row, lp, zero_states(B), tmc)
        outs, _ = token_mixers(lat, B, T, GRID_W, mod3, lat_row, lp, st_c, tm)
        if last:
            x1, h2, h2t = mix_out(*outs, lat, mod3, lp['mix'], lat_row, tm, alpha)
            out = moe_ffn(h2, h2t, x1, mod3, lp['moe'], lambda i: (i * tmm) // T, tmm, alpha)
            lat = (out, 0, n_lat)
        else:
            n_all = n_ctx + n_lat
            bufs = mix_out(*outs_c, con, mod3, lp['mix'], ctx_row, tmc, alpha, out_rows=n_all)
            x1, h2, h2t = mix_out(*outs, lat, mod3, lp['mix'], lat_row, tm, alpha,
                                  out_rows=n_all, out_row0=n_ctx, into=bufs)
            row_all = lambda i: jnp.where(i * tmm < n_ctx, B, (i * tmm - n_ctx) // T)
            out = moe_ffn(h2, h2t, x1, mod3, lp['moe'], row_all, tmm, alpha)
            con, lat = (out, 0, n_ctx), (out, n_ctx, n_lat)
    out, row0, _ = lat
    return out[row0:row0 + n_lat].reshape(B, T, D)
```
